```python
import math
import jax, jax.numpy as jnp
from jax import lax
import numpy as np

D_MODEL = 1024
BATCH = 8
SEQ = 2048
DEPTH = 4

N_BRANCH = 4
MIX_WIDTH = D_MODEL // N_BRANCH
CONV_WIDTH = 3
SSM_GROUP = 16
SSM_GROUPS = MIX_WIDTH // SSM_GROUP
SSM_STATE = 64
DT_MIN = 1e-3
DT_MAX = 1e-1
LAMBDA_RE_MAX = -1e-4
POOL_WINDOWS = (2, 4, 8, 16)
POOL_GROUP = MIX_WIDTH // len(POOL_WINDOWS)
SB_HEAD_DIM = 64
SB_HEADS = MIX_WIDTH // SB_HEAD_DIM
Q_BLOCK = 128
D_FF = 2816
FFN_RES_WEIGHT = 0.5
N_SUB = 3
EPS = 1e-6

CONV_COLS = 3 * MIX_WIDTH
SSM_COLS = MIX_WIDTH
POOL_COLS = MIX_WIDTH
SB_COLS = 3 * MIX_WIDTH
GATE_COLS = N_BRANCH * D_MODEL
IN_COLS = CONV_COLS + SSM_COLS + POOL_COLS + SB_COLS + GATE_COLS
IN_SPLITS = (CONV_COLS, CONV_COLS + SSM_COLS, CONV_COLS + SSM_COLS + POOL_COLS,
             CONV_COLS + SSM_COLS + POOL_COLS + SB_COLS)

kernel_name = "hybrid_parallel_gated_mixer_trunk"


def rmsnorm(x, g):
    xf = x.astype(jnp.float32)
    y = xf * lax.rsqrt(jnp.mean(xf * xf, axis=-1, keepdims=True) + EPS)
    return (y * g.astype(jnp.float32)).astype(x.dtype)


def modulate(h, shift, scale):
    return h * (1.0 + scale[:, None, :]) + shift[:, None, :]


def swiglu_ffn(h, w_in, w_out):
    a, b = jnp.split(h @ w_in, 2, axis=-1)
    return (jax.nn.silu(a) * b) @ w_out


def short_conv_mixer(p, conv_w, w_out):
    b_g, c_g, v = jnp.split(p, 3, axis=-1)
    u = c_g * v
    L = u.shape[1]
    up = jnp.pad(u, ((0, 0), (CONV_WIDTH - 1, 0), (0, 0)))
    y = conv_w[0] * up[:, 0:L]
    for k in range(1, CONV_WIDTH):
        y = y + conv_w[k] * up[:, k:k + L]
    return (b_g * y) @ w_out


def s5_mixer(u, lam_re, lam_im, log_dt, b_re, b_im, c_re, c_im, d_skip, w_glu):
    f32 = jnp.float32
    Bsz, L, W = u.shape
    uf = u.astype(f32).reshape(Bsz, L, SSM_GROUPS, SSM_GROUP)
    lr = jnp.minimum(lam_re.astype(f32), LAMBDA_RE_MAX)
    li = lam_im.astype(f32)
    dt = jnp.exp(log_dt.astype(f32))[:, None]
    mag = jnp.exp(lr * dt)
    ab_re = mag * jnp.cos(li * dt)
    ab_im = mag * jnp.sin(li * dt)
    den = lr * lr + li * li
    nr = ab_re - 1.0
    f_re = (nr * lr + ab_im * li) / den
    f_im = (ab_im * lr - nr * li) / den
    br = b_re.astype(f32)
    bi = b_im.astype(f32)
    bb_re = f_re[..., None] * br - f_im[..., None] * bi
    bb_im = f_re[..., None] * bi + f_im[..., None] * br
    bu_re = jnp.einsum('blgh,gph->blgp', uf, bb_re)
    bu_im = jnp.einsum('blgh,gph->blgp', uf, bb_im)
    a_re = jnp.broadcast_to(ab_re, bu_re.shape)
    a_im = jnp.broadcast_to(ab_im, bu_im.shape)

    def combine(e1, e2):
        a1r, a1i, b1r, b1i = e1
        a2r, a2i, b2r, b2i = e2
        return (a2r * a1r - a2i * a1i,
                a2r * a1i + a2i * a1r,
                a2r * b1r - a2i * b1i + b2r,
                a2r * b1i + a2i * b1r + b2i)

    _, _, s_re, s_im = lax.associative_scan(combine, (a_re, a_im, bu_re, bu_im), axis=1)
    y = (jnp.einsum('blgp,ghp->blgh', s_re, c_re.astype(f32))
         - jnp.einsum('blgp,ghp->blgh', s_im, c_im.astype(f32)))
    y = y.reshape(Bsz, L, W) + d_skip.astype(f32) * uf.reshape(Bsz, L, W)
    y = jax.nn.gelu(y).astype(u.dtype)
    a, g = jnp.split(y @ w_glu, 2, axis=-1)
    return a * jax.nn.sigmoid(g)


def pool_mixer(u, w_pool, pool_scale, w_out):
    f32 = jnp.float32
    Bsz, L, W = u.shape
    uf = u.astype(f32).reshape(Bsz, L, len(POOL_WINDOWS), POOL_GROUP)
    cs = jnp.cumsum(uf, axis=1)
    pos = jnp.arange(L)
    outs = []
    for gi, w in enumerate(POOL_WINDOWS):
        cg = cs[:, :, gi]
        lag = jnp.pad(cg, ((0, 0), (w, 0), (0, 0)))[:, :L]
        cnt = jnp.minimum(pos + 1, w).astype(f32)[None, :, None]
        outs.append((cg - lag) / cnt - uf[:, :, gi])
    pooled = jnp.stack(outs, axis=2)
    mixed = jnp.einsum('blgc,gcd->blgd', pooled, w_pool.astype(f32)).reshape(Bsz, L, W)
    return (mixed * pool_scale.astype(f32)).astype(u.dtype) @ w_out


def stick_breaking_attention(p, w_out):
    f32 = jnp.float32
    Bsz, L, _ = p.shape
    q, k, v = jnp.split(p, 3, axis=-1)
    q = q.astype(f32).reshape(Bsz, L, SB_HEADS, SB_HEAD_DIM) * (SB_HEAD_DIM ** -0.5)
    k = k.astype(f32).reshape(Bsz, L, SB_HEADS, SB_HEAD_DIM)
    v = v.astype(f32).reshape(Bsz, L, SB_HEADS, SB_HEAD_DIM)
    outs = []
    for start in range(0, L, Q_BLOCK):
        end = start + Q_BLOCK
        z = jnp.einsum('bqhd,bkhd->bhqk', q[:, start:end], k[:, :end])
        t_idx = jnp.arange(start, end)[:, None]
        s_idx = jnp.arange(end)[None, :]
        mask = s_idx < t_idx
        log_keep = jnp.where(mask, jax.nn.log_sigmoid(-z), 0.0)
        log_w = jax.nn.log_sigmoid(z) + lax.cumsum(log_keep, axis=3, reverse=True) - log_keep
        a = jnp.where(mask, jnp.exp(log_w), 0.0)
        outs.append(jnp.einsum('bhqk,bkhd->bqhd', a, v[:, :end]))
    o = jnp.concatenate(outs, axis=1).reshape(Bsz, L, MIX_WIDTH).astype(p.dtype)
    return o @ w_out


def _fwd_setup_inputs(seed: int = 0) -> dict:
    key = jax.random.key(seed)
    ks = jax.random.split(key, 32)
    f32 = jnp.float32

    def nrm(k, shape, fan_in, gain=1.0):
        return jax.random.normal(k, shape, f32) * (gain * fan_in ** -0.5)

    W = MIX_WIDTH
    x = jax.random.normal(ks[0], (BATCH, SEQ, D_MODEL), f32)
    c = jax.random.normal(ks[1], (BATCH, D_MODEL), f32)
    w_ada = nrm(ks[2], (DEPTH, D_MODEL, N_SUB * 3 * D_MODEL), D_MODEL, 0.1)
    b_ada = 0.01 * jax.random.normal(ks[3], (DEPTH, N_SUB * 3 * D_MODEL), f32)
    g_pre = 1.0 + 0.02 * jax.random.normal(ks[4], (DEPTH, N_SUB, D_MODEL), f32)
    g_post = 1.0 + 0.02 * jax.random.normal(ks[5], (DEPTH, N_SUB, D_MODEL), f32)
    w_ff_in = nrm(ks[6], (DEPTH, 2, D_MODEL, 2 * D_FF), D_MODEL)
    w_ff_out = nrm(ks[7], (DEPTH, 2, D_FF, D_MODEL), D_FF)
    w_in = nrm(ks[8], (DEPTH, D_MODEL, IN_COLS), D_MODEL)
    conv_w = nrm(ks[9], (DEPTH, CONV_WIDTH, W), CONV_WIDTH)
    w_conv_out = nrm(ks[10], (DEPTH, W, D_MODEL), W)
    lam_re = -0.5 + 0.01 * jax.random.normal(ks[11], (DEPTH, SSM_GROUPS, SSM_STATE), f32)
    lam_im = (math.pi * jnp.arange(SSM_STATE, dtype=f32))[None, None, :] \
        + 0.01 * jax.random.normal(ks[12], (DEPTH, SSM_GROUPS, SSM_STATE), f32)
    log_dt = jax.random.uniform(ks[13], (DEPTH, SSM_GROUPS), f32,
                                math.log(DT_MIN), math.log(DT_MAX))
    ssm_b_re = nrm(ks[14], (DEPTH, SSM_GROUPS, SSM_STATE, SSM_GROUP), 2 * SSM_GROUP)
    ssm_b_im = nrm(ks[15], (DEPTH, SSM_GROUPS, SSM_STATE, SSM_GROUP), 2 * SSM_GROUP)
    ssm_c_re = nrm(ks[16], (DEPTH, SSM_GROUPS, SSM_GROUP, SSM_STATE), 2 * SSM_STATE)
    ssm_c_im = nrm(ks[17], (DEPTH, SSM_GROUPS, SSM_GROUP, SSM_STATE), 2 * SSM_STATE)
    ssm_d = jax.random.normal(ks[18], (DEPTH, W), f32)
    w_glu = nrm(ks[19], (DEPTH, W, 2 * D_MODEL), W)
    w_pool = nrm(ks[20], (DEPTH, len(POOL_WINDOWS), POOL_GROUP, POOL_GROUP), POOL_GROUP)
    pool_scale = 1.0 + 0.1 * jax.random.normal(ks[21], (DEPTH, W), f32)
    w_pool_out = nrm(ks[22], (DEPTH, W, D_MODEL), W)
    w_sb_out = nrm(ks[23], (DEPTH, W, D_MODEL), W)
    w_out = nrm(ks[24], (DEPTH, D_MODEL, D_MODEL), D_MODEL)
    return {"x": x, "c": c, "w_ada": w_ada, "b_ada": b_ada, "g_pre": g_pre, "g_post": g_post,
            "w_ff_in": w_ff_in, "w_ff_out": w_ff_out, "w_in": w_in, "conv_w": conv_w,
            "w_conv_out": w_conv_out, "lam_re": lam_re, "lam_im": lam_im, "log_dt": log_dt,
            "ssm_b_re": ssm_b_re, "ssm_b_im": ssm_b_im, "ssm_c_re": ssm_c_re, "ssm_c_im": ssm_c_im,
            "ssm_d": ssm_d, "w_glu": w_glu, "w_pool": w_pool, "pool_scale": pool_scale,
            "w_pool_out": w_pool_out, "w_sb_out": w_sb_out, "w_out": w_out}


def _fwd_reference(x, c, w_ada, b_ada, g_pre, g_post, w_ff_in, w_ff_out, w_in, conv_w,
              w_conv_out, lam_re, lam_im, log_dt, ssm_b_re, ssm_b_im, ssm_c_re, ssm_c_im,
              ssm_d, w_glu, w_pool, pool_scale, w_pool_out, w_sb_out, w_out):
    Bsz, L, D = x.shape
    c_act = jax.nn.silu(c)
    for l in range(DEPTH):
        ada = (c_act @ w_ada[l] + b_ada[l]).reshape(Bsz, N_SUB, 3, D)

        h = modulate(rmsnorm(x, g_pre[l, 0]), ada[:, 0, 0], ada[:, 0, 1])
        f = swiglu_ffn(h, w_ff_in[l, 0], w_ff_out[l, 0])
        x = x + FFN_RES_WEIGHT * (1.0 + ada[:, 0, 2])[:, None, :] * rmsnorm(f, g_post[l, 0])

        h = modulate(rmsnorm(x, g_pre[l, 1]), ada[:, 1, 0], ada[:, 1, 1])
        p = h @ w_in[l]
        p_conv, p_ssm, p_pool, p_sb, p_gate = jnp.split(p, IN_SPLITS, axis=-1)
        y_a = short_conv_mixer(p_conv, conv_w[l], w_conv_out[l])
        y_b = s5_mixer(p_ssm, lam_re[l], lam_im[l], log_dt[l], ssm_b_re[l], ssm_b_im[l],
                       ssm_c_re[l], ssm_c_im[l], ssm_d[l], w_glu[l])
        y_c = pool_mixer(p_pool, w_pool[l], pool_scale[l], w_pool_out[l])
        y_d = stick_breaking_attention(p_sb, w_sb_out[l])
        gates = jax.nn.sigmoid(p_gate).reshape(Bsz, L, N_BRANCH, D)
        merged = (gates[:, :, 0] * y_a + gates[:, :, 1] * y_b
                  + gates[:, :, 2] * y_c + gates[:, :, 3] * y_d)
        m = merged @ w_out[l]
        x = x + (1.0 + ada[:, 1, 2])[:, None, :] * rmsnorm(m, g_post[l, 1])

        h = modulate(rmsnorm(x, g_pre[l, 2]), ada[:, 2, 0], ada[:, 2, 1])
        f = swiglu_ffn(h, w_ff_in[l, 1], w_ff_out[l, 1])
        x = x + FFN_RES_WEIGHT * (1.0 + ada[:, 2, 2])[:, None, :] * rmsnorm(f, g_post[l, 2])
    return x


import jax as _jax
import jax.numpy as _jnp

TWIN_FORMAT = 'train_step'
FWD_PARAMS = ['x', 'c', 'w_ada', 'b_ada', 'g_pre', 'g_post', 'w_ff_in', 'w_ff_out', 'w_in', 'conv_w', 'w_conv_out', 'lam_re', 'lam_im', 'log_dt', 'ssm_b_re', 'ssm_b_im', 'ssm_c_re', 'ssm_c_im', 'ssm_d', 'w_glu', 'w_pool', 'pool_scale', 'w_pool_out', 'w_sb_out', 'w_out']
TWIN_WEIGHTS = ['w_ada', 'b_ada', 'g_pre', 'g_post', 'w_ff_in', 'w_ff_out', 'w_in', 'conv_w', 'w_conv_out', 'lam_re', 'lam_im', 'log_dt', 'ssm_b_re', 'ssm_b_im', 'ssm_c_re', 'ssm_c_im', 'ssm_d', 'w_glu', 'w_pool', 'pool_scale', 'w_pool_out', 'w_sb_out', 'w_out']
TWIN_DIFF_INPUT = 'x'
TWIN_INPUTS = ['x', 'c', 'w_ada', 'b_ada', 'g_pre', 'g_post', 'w_ff_in', 'w_ff_out', 'w_in', 'conv_w', 'w_conv_out', 'lam_re', 'lam_im', 'log_dt', 'ssm_b_re', 'ssm_b_im', 'ssm_c_re', 'ssm_c_im', 'ssm_d', 'w_glu', 'w_pool', 'pool_scale', 'w_pool_out', 'w_sb_out', 'w_out', 'loss_target', 'm_w_ada', 'm_b_ada', 'm_g_pre', 'm_g_post', 'm_w_ff_in', 'm_w_ff_out', 'm_w_in', 'm_conv_w', 'm_w_conv_out', 'm_lam_re', 'm_lam_im', 'm_log_dt', 'm_ssm_b_re', 'm_ssm_b_im', 'm_ssm_c_re', 'm_ssm_c_im', 'm_ssm_d', 'm_w_glu', 'm_w_pool', 'm_pool_scale', 'm_w_pool_out', 'm_w_sb_out', 'm_w_out', 'v_w_ada', 'v_b_ada', 'v_g_pre', 'v_g_post', 'v_w_ff_in', 'v_w_ff_out', 'v_w_in', 'v_conv_w', 'v_w_conv_out', 'v_lam_re', 'v_lam_im', 'v_log_dt', 'v_ssm_b_re', 'v_ssm_b_im', 'v_ssm_c_re', 'v_ssm_c_im', 'v_ssm_d', 'v_w_glu', 'v_w_pool', 'v_pool_scale', 'v_w_pool_out', 'v_w_sb_out', 'v_w_out']
TWIN_OUTPUTS = ['loss', 'grad_x', 'grad_w_ada', 'grad_b_ada', 'grad_g_pre', 'grad_g_post', 'grad_w_ff_in', 'grad_w_ff_out', 'grad_w_in', 'grad_conv_w', 'grad_w_conv_out', 'grad_lam_re', 'grad_lam_im', 'grad_log_dt', 'grad_ssm_b_re', 'grad_ssm_b_im', 'grad_ssm_c_re', 'grad_ssm_c_im', 'grad_ssm_d', 'grad_w_glu', 'grad_w_pool', 'grad_pool_scale', 'grad_w_pool_out', 'grad_w_sb_out', 'grad_w_out', 'delta_w_ada', 'delta_b_ada', 'delta_g_pre', 'delta_g_post', 'delta_w_ff_in', 'delta_w_ff_out', 'delta_w_in', 'delta_conv_w', 'delta_w_conv_out', 'delta_lam_re', 'delta_lam_im', 'delta_log_dt', 'delta_ssm_b_re', 'delta_ssm_b_im', 'delta_ssm_c_re', 'delta_ssm_c_im', 'delta_ssm_d', 'delta_w_glu', 'delta_w_pool', 'delta_pool_scale', 'delta_w_pool_out', 'delta_w_sb_out', 'delta_w_out', 'new_m_w_ada', 'new_m_b_ada', 'new_m_g_pre', 'new_m_g_post', 'new_m_w_ff_in', 'new_m_w_ff_out', 'new_m_w_in', 'new_m_conv_w', 'new_m_w_conv_out', 'new_m_lam_re', 'new_m_lam_im', 'new_m_log_dt', 'new_m_ssm_b_re', 'new_m_ssm_b_im', 'new_m_ssm_c_re', 'new_m_ssm_c_im', 'new_m_ssm_d', 'new_m_w_glu', 'new_m_w_pool', 'new_m_pool_scale', 'new_m_w_pool_out', 'new_m_w_sb_out', 'new_m_w_out', 'new_v_w_ada', 'new_v_b_ada', 'new_v_g_pre', 'new_v_g_post', 'new_v_w_ff_in', 'new_v_w_ff_out', 'new_v_w_in', 'new_v_conv_w', 'new_v_w_conv_out', 'new_v_lam_re', 'new_v_lam_im', 'new_v_log_dt', 'new_v_ssm_b_re', 'new_v_ssm_b_im', 'new_v_ssm_c_re', 'new_v_ssm_c_im', 'new_v_ssm_d', 'new_v_w_glu', 'new_v_w_pool', 'new_v_pool_scale', 'new_v_w_pool_out', 'new_v_w_sb_out', 'new_v_w_out']
TWIN_LEAF_KINDS = {'loss': 'loss', 'grad_x': 'grad_x', 'grad_w_ada': 'grad_w', 'grad_b_ada': 'grad_w', 'grad_g_pre': 'grad_w', 'grad_g_post': 'grad_w', 'grad_w_ff_in': 'grad_w', 'grad_w_ff_out': 'grad_w', 'grad_w_in': 'grad_w', 'grad_conv_w': 'grad_w', 'grad_w_conv_out': 'grad_w', 'grad_lam_re': 'grad_w', 'grad_lam_im': 'grad_w', 'grad_log_dt': 'grad_w', 'grad_ssm_b_re': 'grad_w', 'grad_ssm_b_im': 'grad_w', 'grad_ssm_c_re': 'grad_w', 'grad_ssm_c_im': 'grad_w', 'grad_ssm_d': 'grad_w', 'grad_w_glu': 'grad_w', 'grad_w_pool': 'grad_w', 'grad_pool_scale': 'grad_w', 'grad_w_pool_out': 'grad_w', 'grad_w_sb_out': 'grad_w', 'grad_w_out': 'grad_w', 'delta_w_ada': 'delta_w', 'delta_b_ada': 'delta_w', 'delta_g_pre': 'delta_w', 'delta_g_post': 'delta_w', 'delta_w_ff_in': 'delta_w', 'delta_w_ff_out': 'delta_w', 'delta_w_in': 'delta_w', 'delta_conv_w': 'delta_w', 'delta_w_conv_out': 'delta_w', 'delta_lam_re': 'delta_w', 'delta_lam_im': 'delta_w', 'delta_log_dt': 'delta_w', 'delta_ssm_b_re': 'delta_w', 'delta_ssm_b_im': 'delta_w', 'delta_ssm_c_re': 'delta_w', 'delta_ssm_c_im': 'delta_w', 'delta_ssm_d': 'delta_w', 'delta_w_glu': 'delta_w', 'delta_w_pool': 'delta_w', 'delta_pool_scale': 'delta_w', 'delta_w_pool_out': 'delta_w', 'delta_w_sb_out': 'delta_w', 'delta_w_out': 'delta_w', 'new_m_w_ada': 'new_m', 'new_m_b_ada': 'new_m', 'new_m_g_pre': 'new_m', 'new_m_g_post': 'new_m', 'new_m_w_ff_in': 'new_m', 'new_m_w_ff_out': 'new_m', 'new_m_w_in': 'new_m', 'new_m_conv_w': 'new_m', 'new_m_w_conv_out': 'new_m', 'new_m_lam_re': 'new_m', 'new_m_lam_im': 'new_m', 'new_m_log_dt': 'new_m', 'new_m_ssm_b_re': 'new_m', 'new_m_ssm_b_im': 'new_m', 'new_m_ssm_c_re': 'new_m', 'new_m_ssm_c_im': 'new_m', 'new_m_ssm_d': 'new_m', 'new_m_w_glu': 'new_m', 'new_m_w_pool': 'new_m', 'new_m_pool_scale': 'new_m', 'new_m_w_pool_out': 'new_m', 'new_m_w_sb_out': 'new_m', 'new_m_w_out': 'new_m', 'new_v_w_ada': 'new_v', 'new_v_b_ada': 'new_v', 'new_v_g_pre': 'new_v', 'new_v_g_post': 'new_v', 'new_v_w_ff_in': 'new_v', 'new_v_w_ff_out': 'new_v', 'new_v_w_in': 'new_v', 'new_v_conv_w': 'new_v', 'new_v_w_conv_out': 'new_v', 'new_v_lam_re': 'new_v', 'new_v_lam_im': 'new_v', 'new_v_log_dt': 'new_v', 'new_v_ssm_b_re': 'new_v', 'new_v_ssm_b_im': 'new_v', 'new_v_ssm_c_re': 'new_v', 'new_v_ssm_c_im': 'new_v', 'new_v_ssm_d': 'new_v', 'new_v_w_glu': 'new_v', 'new_v_w_pool': 'new_v', 'new_v_pool_scale': 'new_v', 'new_v_w_pool_out': 'new_v', 'new_v_w_sb_out': 'new_v', 'new_v_w_out': 'new_v'}


def _forward(args):
    return _fwd_reference(*[args[k] for k in FWD_PARAMS])


def _output_shape():
    out = _jax.eval_shape(lambda: _forward(_fwd_setup_inputs(0)))
    return out.shape, out.dtype

N_MICROBATCH = 1
ADAM_LR = 0.001
ADAM_B1 = 0.9
ADAM_B2 = 0.999
ADAM_EPS = 1e-08
ADAM_WD = 0.01
ADAM_STEP = 10
PER_EXAMPLE_BATCH_AXIS = {'x': 0, 'c': 0, 'loss_target': 0}
SHARED_INPUTS = []
_WEIGHT_DTYPES = {'w_ada': _jnp.float32, 'b_ada': _jnp.float32, 'g_pre': _jnp.float32, 'g_post': _jnp.float32, 'w_ff_in': _jnp.float32, 'w_ff_out': _jnp.float32, 'w_in': _jnp.float32, 'conv_w': _jnp.float32, 'w_conv_out': _jnp.float32, 'lam_re': _jnp.float32, 'lam_im': _jnp.float32, 'log_dt': _jnp.float32, 'ssm_b_re': _jnp.float32, 'ssm_b_im': _jnp.float32, 'ssm_c_re': _jnp.float32, 'ssm_c_im': _jnp.float32, 'ssm_d': _jnp.float32, 'w_glu': _jnp.float32, 'w_pool': _jnp.float32, 'pool_scale': _jnp.float32, 'w_pool_out': _jnp.float32, 'w_sb_out': _jnp.float32, 'w_out': _jnp.float32}
MOMENT_SCALE = {'w_ada': 1.891761e+00, 'b_ada': 5.510548e+00, 'g_pre': 7.925778e-01, 'g_post': 9.798372e+00, 'w_ff_in': 2.659264e-01, 'w_ff_out': 4.506296e-01, 'w_in': 4.327890e-01, 'conv_w': 8.919480e-01, 'w_conv_out': 4.595548e-01, 'lam_re': 2.117273e-02, 'lam_im': 1.630345e-02, 'log_dt': 9.759558e+00, 'ssm_b_re': 1.285794e-02, 'ssm_b_im': 1.236326e-02, 'ssm_c_re': 2.373911e-02, 'ssm_c_im': 2.499620e-02, 'ssm_d': 7.733736e-01, 'w_glu': 2.561899e-01, 'w_pool': 1.180987e+00, 'pool_scale': 1.326811e+00, 'w_pool_out': 6.060931e-01, 'w_sb_out': 3.229429e-01, 'w_out': 9.036741e-01}


def _to_microbatches(a, axis):
    t = _jnp.moveaxis(a, axis, 0)
    t = t.reshape((N_MICROBATCH, t.shape[0] // N_MICROBATCH) + t.shape[1:])
    return _jnp.moveaxis(t, 1, axis + 1)


def setup_inputs(seed: int = 0) -> dict:
    inp = _fwd_setup_inputs(seed)
    key = _jax.random.fold_in(_jax.random.key(seed), 7919)
    shape, _ = _output_shape()
    out = dict(inp)
    out["loss_target"] = _jax.random.normal(_jax.random.fold_in(key, 0), shape, _jnp.float32)
    for i, name in enumerate(TWIN_WEIGHTS):
        w = inp[name].astype(_jnp.float32)
        if MOMENT_SCALE is None:
            s = _jnp.sqrt(_jnp.mean(_jnp.square(w)) + 1e-30)
        else:
            s = MOMENT_SCALE[name]
        km, kv = _jax.random.split(_jax.random.fold_in(key, i + 1))
        out[name] = w
        out["m_" + name] = s * _jax.random.normal(km, w.shape, _jnp.float32)
        out["v_" + name] = (s * s) * _jax.random.uniform(kv, w.shape, _jnp.float32, 0.5, 1.5)
    if N_MICROBATCH > 1:
        for name, axis in PER_EXAMPLE_BATCH_AXIS.items():
            out[name] = _to_microbatches(out[name], axis)
    return {'x': out['x'], 'c': out['c'], 'w_ada': out['w_ada'], 'b_ada': out['b_ada'], 'g_pre': out['g_pre'], 'g_post': out['g_post'], 'w_ff_in': out['w_ff_in'], 'w_ff_out': out['w_ff_out'], 'w_in': out['w_in'], 'conv_w': out['conv_w'], 'w_conv_out': out['w_conv_out'], 'lam_re': out['lam_re'], 'lam_im': out['lam_im'], 'log_dt': out['log_dt'], 'ssm_b_re': out['ssm_b_re'], 'ssm_b_im': out['ssm_b_im'], 'ssm_c_re': out['ssm_c_re'], 'ssm_c_im': out['ssm_c_im'], 'ssm_d': out['ssm_d'], 'w_glu': out['w_glu'], 'w_pool': out['w_pool'], 'pool_scale': out['pool_scale'], 'w_pool_out': out['w_pool_out'], 'w_sb_out': out['w_sb_out'], 'w_out': out['w_out'], 'loss_target': out['loss_target'], 'm_w_ada': out['m_w_ada'], 'm_b_ada': out['m_b_ada'], 'm_g_pre': out['m_g_pre'], 'm_g_post': out['m_g_post'], 'm_w_ff_in': out['m_w_ff_in'], 'm_w_ff_out': out['m_w_ff_out'], 'm_w_in': out['m_w_in'], 'm_conv_w': out['m_conv_w'], 'm_w_conv_out': out['m_w_conv_out'], 'm_lam_re': out['m_lam_re'], 'm_lam_im': out['m_lam_im'], 'm_log_dt': out['m_log_dt'], 'm_ssm_b_re': out['m_ssm_b_re'], 'm_ssm_b_im': out['m_ssm_b_im'], 'm_ssm_c_re': out['m_ssm_c_re'], 'm_ssm_c_im': out['m_ssm_c_im'], 'm_ssm_d': out['m_ssm_d'], 'm_w_glu': out['m_w_glu'], 'm_w_pool': out['m_w_pool'], 'm_pool_scale': out['m_pool_scale'], 'm_w_pool_out': out['m_w_pool_out'], 'm_w_sb_out': out['m_w_sb_out'], 'm_w_out': out['m_w_out'], 'v_w_ada': out['v_w_ada'], 'v_b_ada': out['v_b_ada'], 'v_g_pre': out['v_g_pre'], 'v_g_post': out['v_g_post'], 'v_w_ff_in': out['v_w_ff_in'], 'v_w_ff_out': out['v_w_ff_out'], 'v_w_in': out['v_w_in'], 'v_conv_w': out['v_conv_w'], 'v_w_conv_out': out['v_w_conv_out'], 'v_lam_re': out['v_lam_re'], 'v_lam_im': out['v_lam_im'], 'v_log_dt': out['v_log_dt'], 'v_ssm_b_re': out['v_ssm_b_re'], 'v_ssm_b_im': out['v_ssm_b_im'], 'v_ssm_c_re': out['v_ssm_c_re'], 'v_ssm_c_im': out['v_ssm_c_im'], 'v_ssm_d': out['v_ssm_d'], 'v_w_glu': out['v_w_glu'], 'v_w_pool': out['v_w_pool'], 'v_pool_scale': out['v_pool_scale'], 'v_w_pool_out': out['v_w_pool_out'], 'v_w_sb_out': out['v_w_sb_out'], 'v_w_out': out['v_w_out']}


def _loss(weights, diff, rest, loss_target):
    with _jax.named_scope("forward"):
        args = {**rest, TWIN_DIFF_INPUT: diff, **{k: w.astype(_WEIGHT_DTYPES[k]) for k, w in weights.items()}}
        y = _forward(args)
    with _jax.named_scope("loss_head"):
        err = _jnp.square(y.astype(_jnp.float32) - loss_target)
        return 0.5 * _jnp.sum(_jnp.mean(err, axis=-1)) if err.ndim else 0.5 * err


def _adamw(w, g, m, v):
    m = ADAM_B1 * m + (1.0 - ADAM_B1) * g
    v = ADAM_B2 * v + (1.0 - ADAM_B2) * _jnp.square(g)
    m_hat = m / (1.0 - ADAM_B1 ** ADAM_STEP)
    v_hat = v / (1.0 - ADAM_B2 ** ADAM_STEP)
    delta = -ADAM_LR * (m_hat / (_jnp.sqrt(v_hat) + ADAM_EPS) + ADAM_WD * w)
    return delta, m, v


def reference(x, c, w_ada, b_ada, g_pre, g_post, w_ff_in, w_ff_out, w_in, conv_w, w_conv_out, lam_re, lam_im, log_dt, ssm_b_re, ssm_b_im, ssm_c_re, ssm_c_im, ssm_d, w_glu, w_pool, pool_scale, w_pool_out, w_sb_out, w_out, loss_target, m_w_ada, m_b_ada, m_g_pre, m_g_post, m_w_ff_in, m_w_ff_out, m_w_in, m_conv_w, m_w_conv_out, m_lam_re, m_lam_im, m_log_dt, m_ssm_b_re, m_ssm_b_im, m_ssm_c_re, m_ssm_c_im, m_ssm_d, m_w_glu, m_w_pool, m_pool_scale, m_w_pool_out, m_w_sb_out, m_w_out, v_w_ada, v_b_ada, v_g_pre, v_g_post, v_w_ff_in, v_w_ff_out, v_w_in, v_conv_w, v_w_conv_out, v_lam_re, v_lam_im, v_log_dt, v_ssm_b_re, v_ssm_b_im, v_ssm_c_re, v_ssm_c_im, v_ssm_d, v_w_glu, v_w_pool, v_pool_scale, v_w_pool_out, v_w_sb_out, v_w_out):
    given = dict(x=x, c=c, w_ada=w_ada, b_ada=b_ada, g_pre=g_pre, g_post=g_post, w_ff_in=w_ff_in, w_ff_out=w_ff_out, w_in=w_in, conv_w=conv_w, w_conv_out=w_conv_out, lam_re=lam_re, lam_im=lam_im, log_dt=log_dt, ssm_b_re=ssm_b_re, ssm_b_im=ssm_b_im, ssm_c_re=ssm_c_re, ssm_c_im=ssm_c_im, ssm_d=ssm_d, w_glu=w_glu, w_pool=w_pool, pool_scale=pool_scale, w_pool_out=w_pool_out, w_sb_out=w_sb_out, w_out=w_out, loss_target=loss_target, m_w_ada=m_w_ada, m_b_ada=m_b_ada, m_g_pre=m_g_pre, m_g_post=m_g_post, m_w_ff_in=m_w_ff_in, m_w_ff_out=m_w_ff_out, m_w_in=m_w_in, m_conv_w=m_conv_w, m_w_conv_out=m_w_conv_out, m_lam_re=m_lam_re, m_lam_im=m_lam_im, m_log_dt=m_log_dt, m_ssm_b_re=m_ssm_b_re, m_ssm_b_im=m_ssm_b_im, m_ssm_c_re=m_ssm_c_re, m_ssm_c_im=m_ssm_c_im, m_ssm_d=m_ssm_d, m_w_glu=m_w_glu, m_w_pool=m_w_pool, m_pool_scale=m_pool_scale, m_w_pool_out=m_w_pool_out, m_w_sb_out=m_w_sb_out, m_w_out=m_w_out, v_w_ada=v_w_ada, v_b_ada=v_b_ada, v_g_pre=v_g_pre, v_g_post=v_g_post, v_w_ff_in=v_w_ff_in, v_w_ff_out=v_w_ff_out, v_w_in=v_w_in, v_conv_w=v_conv_w, v_w_conv_out=v_w_conv_out, v_lam_re=v_lam_re, v_lam_im=v_lam_im, v_log_dt=v_log_dt, v_ssm_b_re=v_ssm_b_re, v_ssm_b_im=v_ssm_b_im, v_ssm_c_re=v_ssm_c_re, v_ssm_c_im=v_ssm_c_im, v_ssm_d=v_ssm_d, v_w_glu=v_w_glu, v_w_pool=v_w_pool, v_pool_scale=v_pool_scale, v_w_pool_out=v_w_pool_out, v_w_sb_out=v_w_sb_out, v_w_out=v_w_out)
    weights = {n: given[n] for n in TWIN_WEIGHTS}
    shared = {n: given[n] for n in SHARED_INPUTS}
    per_example = {n: given[n] for n in ['x', 'c']}
    grad_fn = _jax.value_and_grad(_loss, argnums=(0, 1))

    def one_microbatch(ex, loss_target):
        ex = dict(ex)
        diff = ex.pop(TWIN_DIFF_INPUT)
        return grad_fn(weights, diff, {**shared, **ex}, loss_target)

    if N_MICROBATCH == 1:
        loss, (grad_w, grad_x) = one_microbatch(per_example, given["loss_target"])
    else:
        def body(carry, xs):
            loss_sum, grad_sum = carry
            l_k, (gw_k, gx_k) = one_microbatch(xs[0], xs[1])
            with _jax.named_scope("update"):
                return (loss_sum + l_k, _jax.tree.map(_jnp.add, grad_sum, gw_k)), gx_k

        init = (_jnp.zeros((), _jnp.float32), _jax.tree.map(_jnp.zeros_like, weights))
        (loss, grad_w), grad_x = _jax.lax.scan(body, init, (per_example, given["loss_target"]))
    with _jax.named_scope("update"):
        delta_w, new_m, new_v = {}, {}, {}
        for n in TWIN_WEIGHTS:
            delta_w[n], new_m[n], new_v[n] = _adamw(weights[n], grad_w[n], given["m_" + n], given["v_" + n])
    return (loss, grad_x, *[grad_w[n] for n in TWIN_WEIGHTS], *[delta_w[n] for n in TWIN_WEIGHTS],
            *[new_m[n] for n in TWIN_WEIGHTS], *[new_v[n] for n in TWIN_WEIGHTS])
```

```python
import functools

import jax
import jax.numpy as jnp
from jax import lax
from jax.experimental import pallas as pl
from jax.experimental.pallas import tpu as pltpu

F32 = jnp.float32
BF16 = jnp.bfloat16

N_DEV = 8
D_MODEL = 1024
D_FF = 2816
FF_BLK = D_FF // 4
N_SUB = 3
MIX_W = 256
IN_COLS = 6144
IN_BLK = IN_COLS // N_DEV
GATE_OFF = 2048
SSM_GROUPS, SSM_GROUP, SSM_STATE = 16, 16, 64
SSM_W = SSM_GROUPS * SSM_STATE
POOL_WINDOWS = (2, 4, 8, 16)
SB_HEAD = 64
EPS = 1e-6
DT_LAMBDA_RE_MAX = -1e-4
ADAM_LR, ADAM_B1, ADAM_B2, ADAM_EPS, ADAM_WD, ADAM_STEP = 0.001, 0.9, 0.999, 1e-08, 0.01, 10

VMEM_LIMIT = 56 * 1024 * 1024

PV_GPRE, PV_SHIFT, PV_SCALE, PV_GPOST, PV_GATE = 0, 1, 2, 3, 4


def _cparams(sem):
    return pltpu.CompilerParams(dimension_semantics=sem, vmem_limit_bytes=VMEM_LIMIT)


def _dot(a, b):
    return jnp.dot(a, b, preferred_element_type=F32)


def _dot_nt(a, b):
    return lax.dot_general(a, b, (((1,), (1,)), ((), ())), preferred_element_type=F32)


def _dot_tn(a, b):
    return lax.dot_general(a, b, (((0,), (0,)), ((), ())), preferred_element_type=F32)


def _rms(x):
    r = lax.rsqrt(jnp.mean(x * x, axis=-1, keepdims=True) + EPS)
    return x * r, r


def _rms_bwd(dn, n, r):
    return r * (dn - n * jnp.mean(dn * n, axis=-1, keepdims=True))


def _sigmoid(x):
    return 1.0 / (1.0 + jnp.exp(-x))


def _colsum(x):
    return jnp.sum(x, axis=0, keepdims=True)


def _prenorm(x, pv_ref):
    n, r = _rms(x)
    hn = n * pv_ref[PV_GPRE:PV_GPRE + 1, :]
    h = hn * (1.0 + pv_ref[PV_SCALE:PV_SCALE + 1, :]) + pv_ref[PV_SHIFT:PV_SHIFT + 1, :]
    return h, n, r, hn


def _prenorm_bwd(dh, dxn, x, pv_ref, pg_ref):
    _, n, r, hn = _prenorm(x, pv_ref)
    pg_ref[PV_SHIFT:PV_SHIFT + 1, :] += _colsum(dh)
    pg_ref[PV_SCALE:PV_SCALE + 1, :] += _colsum(dh * hn)
    dhn = dh * (1.0 + pv_ref[PV_SCALE:PV_SCALE + 1, :])
    pg_ref[PV_GPRE:PV_GPRE + 1, :] += _colsum(dhn * n)
    dn = dhn * pv_ref[PV_GPRE:PV_GPRE + 1, :]
    return dxn + _rms_bwd(dn, n, r)


def _postnorm_res(x, f, pv_ref, coef):
    nf, _ = _rms(f)
    return x + (coef * (1.0 + pv_ref[PV_GATE:PV_GATE + 1, :])) * (nf * pv_ref[PV_GPOST:PV_GPOST + 1, :])


def _postnorm_bwd(dxn, f, pv_ref, pg_ref, coef):
    nf, rf = _rms(f)
    g_post = pv_ref[PV_GPOST:PV_GPOST + 1, :]
    pg_ref[PV_GATE:PV_GATE + 1, :] += _colsum(dxn * (nf * g_post)) * coef
    dnfg = dxn * (coef * (1.0 + pv_ref[PV_GATE:PV_GATE + 1, :]))
    pg_ref[PV_GPOST:PV_GPOST + 1, :] += _colsum(dnfg * nf)
    return _rms_bwd(dnfg * g_post, nf, rf)


def ffn_fwd(lk, x, pv, wg_in, wg_out, tm=512):
    T, D = x.shape
    nj = 4

    def body(lk_ref, x_ref, pv_ref, win_ref, wout_ref, ab_ref, f_ref, xn_ref, h_sc, acc):
        j = pl.program_id(1)

        @pl.when(j == 0)
        def _():
            h, _, _, _ = _prenorm(x_ref[...], pv_ref)
            h_sc[...] = h.astype(BF16)
            acc[...] = jnp.zeros_like(acc)

        h = h_sc[...]
        a = _dot(h, win_ref[0])
        b = _dot(h, win_ref[1])
        ab_ref[0] = a.astype(BF16)
        ab_ref[1] = b.astype(BF16)
        act = (a * _sigmoid(a) * b).astype(BF16)
        acc[...] += _dot(act, wout_ref[...])

        @pl.when(j == nj - 1)
        def _():
            f = acc[...]
            f_ref[...] = f
            xn_ref[...] = _postnorm_res(x_ref[...], f, pv_ref, 0.5)

    grid_spec = pltpu.PrefetchScalarGridSpec(
        num_scalar_prefetch=1, grid=(T // tm, nj),
        in_specs=[
            pl.BlockSpec((tm, D), lambda i, j, lk: (i, 0)),
            pl.BlockSpec((8, D), lambda i, j, lk: (0, 0)),
            pl.BlockSpec((None, None, 2, None, D, FF_BLK), lambda i, j, lk: (lk[0], lk[1], 0, j, 0, 0)),
            pl.BlockSpec((None, None, None, FF_BLK, D), lambda i, j, lk: (lk[0], lk[1], j, 0, 0)),
        ],
        out_specs=[
            pl.BlockSpec((2, None, tm, FF_BLK), lambda i, j, lk: (0, j, i, 0)),
            pl.BlockSpec((tm, D), lambda i, j, lk: (i, 0)),
            pl.BlockSpec((tm, D), lambda i, j, lk: (i, 0)),
        ],
        scratch_shapes=[pltpu.VMEM((tm, D), BF16), pltpu.VMEM((tm, D), F32)],
    )
    return pl.pallas_call(
        body, name="ffn_fwd", grid_spec=grid_spec,
        out_shape=[jax.ShapeDtypeStruct((2, nj, T, FF_BLK), BF16),
                   jax.ShapeDtypeStruct((T, D), F32), jax.ShapeDtypeStruct((T, D), F32)],
        compiler_params=_cparams(("arbitrary", "arbitrary")),
    )(lk, x, pv, wg_in, wg_out)


def ffn_bwd_act(lk, dxn, x, f, pv, ab, wg_in, wg_out, tm=512):
    T, D = x.shape
    nj = 4

    def body(lk_ref, dxn_ref, x_ref, f_ref, pv_ref, ab_ref, win_ref, wout_ref,
             dab_ref, h_ref, df_ref, dx_ref, pg_ref, dacc):
        i, j = pl.program_id(0), pl.program_id(1)

        @pl.when((i == 0) & (j == 0))
        def _():
            pg_ref[...] = jnp.zeros_like(pg_ref)

        @pl.when(j == 0)
        def _():
            df = _postnorm_bwd(dxn_ref[...], f_ref[...], pv_ref, pg_ref, 0.5)
            df_ref[...] = df.astype(BF16)
            h, _, _, _ = _prenorm(x_ref[...], pv_ref)
            h_ref[...] = h.astype(BF16)
            dacc[...] = jnp.zeros_like(dacc)

        dact = _dot_nt(df_ref[...], wout_ref[...])
        a = ab_ref[0].astype(F32)
        b = ab_ref[1].astype(F32)
        sig = _sigmoid(a)
        s = a * sig
        da = (dact * b * (sig * (1.0 + a * (1.0 - sig)))).astype(BF16)
        db = (dact * s).astype(BF16)
        dab_ref[0] = da
        dab_ref[1] = db
        dacc[...] += _dot_nt(da, win_ref[0]) + _dot_nt(db, win_ref[1])

        @pl.when(j == nj - 1)
        def _():
            dx_ref[...] = _prenorm_bwd(dacc[...], dxn_ref[...], x_ref[...], pv_ref, pg_ref)

    tile = pl.BlockSpec((tm, D), lambda i, j, lk: (i, 0))
    grid_spec = pltpu.PrefetchScalarGridSpec(
        num_scalar_prefetch=1, grid=(T // tm, nj),
        in_specs=[
            tile, tile, tile,
            pl.BlockSpec((8, D), lambda i, j, lk: (0, 0)),
            pl.BlockSpec((2, None, tm, FF_BLK), lambda i, j, lk: (0, j, i, 0)),
            pl.BlockSpec((None, None, 2, None, D, FF_BLK), lambda i, j, lk: (lk[0], lk[1], 0, j, 0, 0)),
            pl.BlockSpec((None, None, None, FF_BLK, D), lambda i, j, lk: (lk[0], lk[1], j, 0, 0)),
        ],
        out_specs=[
            pl.BlockSpec((2, None, tm, FF_BLK), lambda i, j, lk: (0, j, i, 0)),
            tile, tile, tile,
            pl.BlockSpec((8, D), lambda i, j, lk: (0, 0)),
        ],
        scratch_shapes=[pltpu.VMEM((tm, D), F32)],
    )
    return pl.pallas_call(
        body, name="ffn_bwd_act", grid_spec=grid_spec,
        out_shape=[jax.ShapeDtypeStruct((2, nj, T, FF_BLK), BF16),
                   jax.ShapeDtypeStruct((T, D), BF16), jax.ShapeDtypeStruct((T, D), BF16),
                   jax.ShapeDtypeStruct((T, D), F32), jax.ShapeDtypeStruct((8, D), F32)],
        compiler_params=_cparams(("arbitrary", "arbitrary")),
    )(lk, dxn, x, f, pv, ab, wg_in, wg_out)


def ffn_bwd_w(h, df, ab, dab, tm=512):
    T, D = h.shape
    nj, ni = 4, T // tm

    def body(h_ref, df_ref, ab_ref, dab_ref, gin_ref, gout_ref, acc_in, acc_out):
        i = pl.program_id(1)

        @pl.when(i == 0)
        def _():
            acc_in[...] = jnp.zeros_like(acc_in)
            acc_out[...] = jnp.zeros_like(acc_out)

        h = h_ref[...]
        acc_in[0] += _dot_tn(h, dab_ref[0])
        acc_in[1] += _dot_tn(h, dab_ref[1])
        a = ab_ref[0].astype(F32)
        b = ab_ref[1].astype(F32)
        act = (a * _sigmoid(a) * b).astype(BF16)
        acc_out[...] += _dot_tn(act, df_ref[...])

        @pl.when(i == ni - 1)
        def _():
            gin_ref[...] = acc_in[...].astype(BF16)
            gout_ref[...] = acc_out[...].astype(BF16)

    tile = pl.BlockSpec((tm, D), lambda j, i: (i, 0))
    blk = pl.BlockSpec((2, None, tm, FF_BLK), lambda j, i: (0, j, i, 0))
    return pl.pallas_call(
        body, name="ffn_bwd_w", grid=(nj, ni),
        in_specs=[tile, tile, blk, blk],
        out_specs=[pl.BlockSpec((2, None, D, FF_BLK), lambda j, i: (0, j, 0, 0)),
                   pl.BlockSpec((None, FF_BLK, D), lambda j, i: (j, 0, 0))],
        out_shape=[jax.ShapeDtypeStruct((2, nj, D, FF_BLK), BF16),
                   jax.ShapeDtypeStruct((nj, FF_BLK, D), BF16)],
        scratch_shapes=[pltpu.VMEM((2, D, FF_BLK), F32), pltpu.VMEM((FF_BLK, D), F32)],
        compiler_params=_cparams(("arbitrary", "arbitrary")),
    )(h, df, ab, dab)


def mix_in_fwd(l, x, pv, wg, tm=512):
    T, D = x.shape

    def body(l_ref, x_ref, pv_ref, w_ref, p_ref, h_sc):
        @pl.when(pl.program_id(1) == 0)
        def _():
            h, _, _, _ = _prenorm(x_ref[...], pv_ref)
            h_sc[...] = h.astype(BF16)

        p_ref[...] = _dot(h_sc[...], w_ref[...])

    grid_spec = pltpu.PrefetchScalarGridSpec(
        num_scalar_prefetch=1, grid=(T // tm, N_DEV),
        in_specs=[pl.BlockSpec((tm, D), lambda i, j, l: (i, 0)),
                  pl.BlockSpec((8, D), lambda i, j, l: (0, 0)),
                  pl.BlockSpec((None, None, D, IN_BLK), lambda i, j, l: (l[0], j, 0, 0))],
        out_specs=pl.BlockSpec((tm, IN_BLK), lambda i, j, l: (i, j)),
        scratch_shapes=[pltpu.VMEM((tm, D), BF16)],
    )
    return pl.pallas_call(
        body, name="mix_in_fwd", grid_spec=grid_spec,
        out_shape=jax.ShapeDtypeStruct((T, IN_COLS), F32),
        compiler_params=_cparams(("arbitrary", "arbitrary")),
    )(l, x, pv, wg)


def mix_in_bwd_act(l, dp, dxn, x, pv, wg, tm=512):
    T, D = x.shape

    def body(l_ref, dp_ref, dxn_ref, x_ref, pv_ref, w_ref, dx_ref, h_ref, pg_ref, dacc):
        i, j = pl.program_id(0), pl.program_id(1)

        @pl.when((i == 0) & (j == 0))
        def _():
            pg_ref[...] = jnp.zeros_like(pg_ref)

        @pl.when(j == 0)
        def _():
            dacc[...] = jnp.zeros_like(dacc)

        dacc[...] += _dot_nt(dp_ref[...], w_ref[...])

        @pl.when(j == N_DEV - 1)
        def _():
            h, _, _, _ = _prenorm(x_ref[...], pv_ref)
            h_ref[...] = h.astype(BF16)
            dx_ref[...] = _prenorm_bwd(dacc[...], dxn_ref[...], x_ref[...], pv_ref, pg_ref)

    tile = pl.BlockSpec((tm, D), lambda i, j, l: (i, 0))
    grid_spec = pltpu.PrefetchScalarGridSpec(
        num_scalar_prefetch=1, grid=(T // tm, N_DEV),
        in_specs=[pl.BlockSpec((tm, IN_BLK), lambda i, j, l: (i, j)), tile, tile,
                  pl.BlockSpec((8, D), lambda i, j, l: (0, 0)),
                  pl.BlockSpec((None, None, D, IN_BLK), lambda i, j, l: (l[0], j, 0, 0))],
        out_specs=[tile, tile, pl.BlockSpec((8, D), lambda i, j, l: (0, 0))],
        scratch_shapes=[pltpu.VMEM((tm, D), F32)],
    )
    return pl.pallas_call(
        body, name="mix_in_bwd_act", grid_spec=grid_spec,
        out_shape=[jax.ShapeDtypeStruct((T, D), F32), jax.ShapeDtypeStruct((T, D), BF16),
                   jax.ShapeDtypeStruct((8, D), F32)],
        compiler_params=_cparams(("arbitrary", "arbitrary")),
    )(l, dp, dxn, x, pv, wg)


def matmul_tn(a, b, tn, tm=512):
    T, M = a.shape
    N = b.shape[1]
    ni = T // tm

    def body(a_ref, b_ref, o_ref, acc):
        i = pl.program_id(1)

        @pl.when(i == 0)
        def _():
            acc[...] = jnp.zeros_like(acc)

        acc[...] += _dot_tn(a_ref[...], b_ref[...])

        @pl.when(i == ni - 1)
        def _():
            o_ref[...] = acc[...].astype(o_ref.dtype)

    return pl.pallas_call(
        body, name="matmul_tn", grid=(N // tn, ni),
        in_specs=[pl.BlockSpec((tm, M), lambda j, i: (i, 0)), pl.BlockSpec((tm, tn), lambda j, i: (i, j))],
        out_specs=pl.BlockSpec((None, M, tn), lambda j, i: (j, 0, 0)),
        out_shape=jax.ShapeDtypeStruct((N // tn, M, tn), BF16),
        scratch_shapes=[pltpu.VMEM((M, tn), F32)],
        compiler_params=_cparams(("arbitrary", "arbitrary")),
    )(a, b)


SEQ_CHUNK = 256
HALO = 16


def _shift_down(ext, d):
    return pltpu.roll(ext, d, 0)


def _shift_up(ext, d):
    return pltpu.roll(ext, ext.shape[0] - d, 0)


def _rows_with_lead(load, c, width):
    t0 = c * SEQ_CHUNK
    if c == 0:
        return jnp.concatenate([jnp.zeros((HALO, width), F32), load(0, SEQ_CHUNK)], axis=0)
    return load(t0 - HALO, SEQ_CHUNK + HALO)


def _rows_with_tail(load, c, n_chunks, width):
    t0 = c * SEQ_CHUNK
    if c == n_chunks - 1:
        return jnp.concatenate([load(t0, SEQ_CHUNK), jnp.zeros((HALO, width), F32)], axis=0)
    return load(t0, SEQ_CHUNK + HALO)


def conv_fwd(l, p, conv_w):
    T = p.shape[0]
    W = MIX_W
    nC = T // SEQ_CHUNK

    def body(l_ref, p_ref, w_ref, za_ref):
        w0, w1, w2 = w_ref[0:1, :], w_ref[1:2, :], w_ref[2:3, :]
        for c in range(nC):
            ext = _rows_with_lead(lambda s, n: p_ref[s:s + n, W:2 * W] * p_ref[s:s + n, 2 * W:3 * W], c, W)
            y = w2 * ext + w1 * _shift_down(ext, 1) + w0 * _shift_down(ext, 2)
            t0 = c * SEQ_CHUNK
            za_ref[t0:t0 + SEQ_CHUNK, :] = (p_ref[t0:t0 + SEQ_CHUNK, 0:W] * y[HALO:]).astype(BF16)

    grid_spec = pltpu.PrefetchScalarGridSpec(
        num_scalar_prefetch=1, grid=(1,),
        in_specs=[pl.BlockSpec((T, 3 * W), lambda i, l: (0, 0)),
                  pl.BlockSpec((None, 8, W), lambda i, l: (l[0], 0, 0))],
        out_specs=pl.BlockSpec((T, W), lambda i, l: (0, 0)),
    )
    return pl.pallas_call(
        body, name="conv_fwd", grid_spec=grid_spec,
        out_shape=jax.ShapeDtypeStruct((T, W), BF16),
        compiler_params=_cparams(("arbitrary",)),
    )(l, p, conv_w)


def conv_bwd(l, p, dza, conv_w):
    T = p.shape[0]
    W = MIX_W
    nC = T // SEQ_CHUNK

    def body(l_ref, p_ref, dza_ref, w_ref, dp_ref, dw_ref):
        w0, w1, w2 = w_ref[0:1, :], w_ref[1:2, :], w_ref[2:3, :]
        dw = [jnp.zeros((1, W), F32) for _ in range(3)]
        for c in range(nC):
            t0 = c * SEQ_CHUNK
            ext = _rows_with_lead(lambda s, n: p_ref[s:s + n, W:2 * W] * p_ref[s:s + n, 2 * W:3 * W], c, W)
            u1, u2 = _shift_down(ext, 1)[HALO:], _shift_down(ext, 2)[HALO:]
            u0 = ext[HALO:]
            y = w2 * u0 + w1 * u1 + w0 * u2
            dza_c = dza_ref[t0:t0 + SEQ_CHUNK, :]
            dy = dza_c * p_ref[t0:t0 + SEQ_CHUNK, 0:W]
            dw[0] += _colsum(dy * u2)
            dw[1] += _colsum(dy * u1)
            dw[2] += _colsum(dy * u0)
            dye = _rows_with_tail(lambda s, n: dza_ref[s:s + n, :] * p_ref[s:s + n, 0:W], c, nC, W)
            du = (w2 * dye + w1 * _shift_up(dye, 1) + w0 * _shift_up(dye, 2))[:SEQ_CHUNK]
            dp_ref[t0:t0 + SEQ_CHUNK, 0:W] = (dza_c * y).astype(BF16)
            dp_ref[t0:t0 + SEQ_CHUNK, W:2 * W] = (du * p_ref[t0:t0 + SEQ_CHUNK, 2 * W:3 * W]).astype(BF16)
            dp_ref[t0:t0 + SEQ_CHUNK, 2 * W:3 * W] = (du * p_ref[t0:t0 + SEQ_CHUNK, W:2 * W]).astype(BF16)
        dw_ref[...] = jnp.concatenate(dw + [jnp.zeros((5, W), F32)], axis=0)

    grid_spec = pltpu.PrefetchScalarGridSpec(
        num_scalar_prefetch=1, grid=(1,),
        in_specs=[pl.BlockSpec((T, 3 * W), lambda i, l: (0, 0)),
                  pl.BlockSpec((T, W), lambda i, l: (0, 0)),
                  pl.BlockSpec((None, 8, W), lambda i, l: (l[0], 0, 0))],
        out_specs=[pl.BlockSpec((T, 3 * W), lambda i, l: (0, 0)), pl.BlockSpec((8, W), lambda i, l: (0, 0))],
    )
    return pl.pallas_call(
        body, name="conv_bwd", grid_spec=grid_spec,
        out_shape=[jax.ShapeDtypeStruct((T, 3 * W), BF16), jax.ShapeDtypeStruct((8, W), F32)],
        compiler_params=_cparams(("arbitrary",)),
    )(l, p, dza, conv_w)


def _pool_consts(rows, t0):
    lane = lax.broadcasted_iota(jnp.int32, (rows, MIX_W), 1)
    t = lax.broadcasted_iota(jnp.int32, (rows, MIX_W), 0) + t0
    win = jnp.where(lane < 64, 2, jnp.where(lane < 128, 4, jnp.where(lane < 192, 8, 16)))
    inv = 1.0 / jnp.minimum(t + 1, win).astype(F32)
    return lane, inv


def _pick_window(lane, s2, s4, s8, s16):
    return jnp.where(lane < 64, s2, jnp.where(lane < 128, s4, jnp.where(lane < 192, s8, s16)))


def _pooled_chunk(u_ref, c):
    ext = _rows_with_lead(lambda s, n: u_ref[s:s + n, :], c, MIX_W)
    s2 = ext + _shift_down(ext, 1)
    s4 = s2 + _shift_down(s2, 2)
    s8 = s4 + _shift_down(s4, 4)
    s16 = s8 + _shift_down(s8, 8)
    lane, inv = _pool_consts(SEQ_CHUNK, c * SEQ_CHUNK)
    return _pick_window(lane, s2[HALO:], s4[HALO:], s8[HALO:], s16[HALO:]) * inv - ext[HALO:]


def pool_fwd(l, p, w_bd, scale):
    T = p.shape[0]
    W = MIX_W
    nC = T // SEQ_CHUNK

    def body(l_ref, u_ref, w_ref, sc_ref, z_ref):
        for c in range(nC):
            pooled = _pooled_chunk(u_ref, c)
            mixed = _dot(pooled.astype(BF16), w_ref[...])
            z_ref[c * SEQ_CHUNK:(c + 1) * SEQ_CHUNK, :] = (mixed * sc_ref[0:1, :]).astype(BF16)

    grid_spec = pltpu.PrefetchScalarGridSpec(
        num_scalar_prefetch=1, grid=(1,),
        in_specs=[pl.BlockSpec((T, W), lambda i, l: (0, 4)),
                  pl.BlockSpec((None, W, W), lambda i, l: (l[0], 0, 0)),
                  pl.BlockSpec((None, 8, W), lambda i, l: (l[0], 0, 0))],
        out_specs=pl.BlockSpec((T, W), lambda i, l: (0, 0)),
    )
    return pl.pallas_call(
        body, name="pool_fwd", grid_spec=grid_spec,
        out_shape=jax.ShapeDtypeStruct((T, W), BF16),
        compiler_params=_cparams(("arbitrary",)),
    )(l, p, w_bd, scale)


def pool_bwd(l, p, dz, w_bd, scale):
    T = p.shape[0]
    W = MIX_W
    nC = T // SEQ_CHUNK

    def body(l_ref, u_ref, dz_ref, w_ref, sc_ref, du_ref, dw_ref, dsc_ref, e_sc, dpl_sc):
        dw = jnp.zeros((W, W), F32)
        dsc = jnp.zeros((1, W), F32)
        for c in range(nC):
            t0 = c * SEQ_CHUNK
            pooled = _pooled_chunk(u_ref, c).astype(BF16)
            mixed = _dot(pooled, w_ref[...])
            dz_c = dz_ref[t0:t0 + SEQ_CHUNK, :]
            dsc += _colsum(dz_c * mixed)
            dmixed = (dz_c * sc_ref[0:1, :]).astype(BF16)
            dw += _dot_tn(pooled, dmixed)
            dpooled = _dot_nt(dmixed, w_ref[...])
            _, inv = _pool_consts(SEQ_CHUNK, t0)
            dpl_sc[t0:t0 + SEQ_CHUNK, :] = dpooled
            e_sc[t0:t0 + SEQ_CHUNK, :] = dpooled * inv
        for c in range(nC):
            t0 = c * SEQ_CHUNK
            ext = _rows_with_tail(lambda s, n: e_sc[s:s + n, :], c, nC, W)
            s2 = ext + _shift_up(ext, 1)
            s4 = s2 + _shift_up(s2, 2)
            s8 = s4 + _shift_up(s4, 4)
            s16 = s8 + _shift_up(s8, 8)
            lane, _ = _pool_consts(SEQ_CHUNK, t0)
            n = SEQ_CHUNK
            du = _pick_window(lane, s2[:n], s4[:n], s8[:n], s16[:n]) - dpl_sc[t0:t0 + SEQ_CHUNK, :]
            du_ref[t0:t0 + SEQ_CHUNK, :] = du.astype(BF16)
        dw_ref[...] = dw
        dsc_ref[...] = jnp.concatenate([dsc, jnp.zeros((7, W), F32)], axis=0)

    grid_spec = pltpu.PrefetchScalarGridSpec(
        num_scalar_prefetch=1, grid=(1,),
        in_specs=[pl.BlockSpec((T, W), lambda i, l: (0, 4)),
                  pl.BlockSpec((T, W), lambda i, l: (0, 0)),
                  pl.BlockSpec((None, W, W), lambda i, l: (l[0], 0, 0)),
                  pl.BlockSpec((None, 8, W), lambda i, l: (l[0], 0, 0))],
        out_specs=[pl.BlockSpec((T, W), lambda i, l: (0, 0)), pl.BlockSpec((W, W), lambda i, l: (0, 0)),
                   pl.BlockSpec((8, W), lambda i, l: (0, 0))],
        scratch_shapes=[pltpu.VMEM((T, W), F32), pltpu.VMEM((T, W), F32)],
    )
    return pl.pallas_call(
        body, name="pool_bwd", grid_spec=grid_spec,
        out_shape=[jax.ShapeDtypeStruct((T, W), BF16), jax.ShapeDtypeStruct((W, W), F32),
                   jax.ShapeDtypeStruct((8, W), F32)],
        compiler_params=_cparams(("arbitrary",)),
    )(l, p, dz, w_bd, scale)


def _s5_disc(lre, lim, ldt):
    lr = jnp.minimum(lre, DT_LAMBDA_RE_MAX)
    dt = jnp.exp(ldt)
    mag = jnp.exp(lr * dt)
    a_re = mag * jnp.cos(lim * dt)
    a_im = mag * jnp.sin(lim * dt)
    den = lr * lr + lim * lim
    nr = a_re - 1.0
    return a_re, a_im, (nr * lr + a_im * lim) / den, (a_im * lr - nr * lim) / den


def _bd_mask(shape, row_blk, col_blk):
    r = lax.broadcasted_iota(jnp.int32, shape, 0) >> (row_blk.bit_length() - 1)
    c = lax.broadcasted_iota(jnp.int32, shape, 1) >> (col_blk.bit_length() - 1)
    return r == c


def s5_params(lam, b_t, c_t):
    L = lam.shape[0]

    def body(lam_ref, b_ref, c_ref, a_ref, bbd_ref, cbd_ref):
        a_re, a_im, f_re, f_im = _s5_disc(lam_ref[0:1, :], lam_ref[1:2, :], lam_ref[2:3, :])
        a_ref[...] = jnp.concatenate([a_re, a_im, jnp.zeros((6, SSM_W), F32)], axis=0)
        mb = _bd_mask((MIX_W, SSM_W), SSM_GROUP, SSM_STATE)
        bbd_ref[0] = jnp.where(mb, f_re * b_ref[0] - f_im * b_ref[1], 0.0).astype(BF16)
        bbd_ref[1] = jnp.where(mb, f_re * b_ref[1] + f_im * b_ref[0], 0.0).astype(BF16)
        mc = _bd_mask((SSM_W, MIX_W), SSM_STATE, SSM_GROUP)
        cbd_ref[0] = jnp.where(mc, c_ref[0], 0.0).astype(BF16)
        cbd_ref[1] = jnp.where(mc, c_ref[1], 0.0).astype(BF16)

    return pl.pallas_call(
        body, name="s5_params", grid=(L,),
        in_specs=[pl.BlockSpec((None, 8, SSM_W), lambda l: (l, 0, 0)),
                  pl.BlockSpec((None, 2, MIX_W, SSM_W), lambda l: (l, 0, 0, 0)),
                  pl.BlockSpec((None, 2, SSM_W, MIX_W), lambda l: (l, 0, 0, 0))],
        out_specs=[pl.BlockSpec((None, 8, SSM_W), lambda l: (l, 0, 0)),
                   pl.BlockSpec((None, 2, MIX_W, SSM_W), lambda l: (l, 0, 0, 0)),
                   pl.BlockSpec((None, 2, SSM_W, MIX_W), lambda l: (l, 0, 0, 0))],
        out_shape=[jax.ShapeDtypeStruct((L, 8, SSM_W), F32),
                   jax.ShapeDtypeStruct((L, 2, MIX_W, SSM_W), BF16),
                   jax.ShapeDtypeStruct((L, 2, SSM_W, MIX_W), BF16)],
        compiler_params=_cparams(("arbitrary",)),
    )(lam, b_t, c_t)


def s5_params_bwd(lam, b_t, gb, gc, da):
    L = lam.shape[0]

    def body(lam_ref, b_ref, gb_ref, gc_ref, da_ref, dlam_ref, db_ref, dc_ref, dgrp_ref):
        lre, lim, ldt = lam_ref[0:1, :], lam_ref[1:2, :], lam_ref[2:3, :]
        (a_re, a_im, f_re, f_im), vjp = jax.vjp(_s5_disc, lre, lim, ldt)
        mb = _bd_mask((MIX_W, SSM_W), SSM_GROUP, SSM_STATE)
        gbr = jnp.where(mb, gb_ref[0], 0.0)
        gbi = jnp.where(mb, gb_ref[1], 0.0)
        df_re = _colsum(gbr * b_ref[0] + gbi * b_ref[1])
        df_im = _colsum(gbi * b_ref[0] - gbr * b_ref[1])
        db_ref[0] = f_re * gbr + f_im * gbi
        db_ref[1] = f_re * gbi - f_im * gbr
        mc = _bd_mask((SSM_W, MIX_W), SSM_STATE, SSM_GROUP)
        dc_ref[0] = jnp.where(mc, gc_ref[0], 0.0)
        dc_ref[1] = jnp.where(mc, gc_ref[1], 0.0)
        dlre, dlim, dldt = vjp((da_ref[0:1, :], da_ref[1:2, :], df_re, df_im))
        dl = jnp.concatenate([dlre, dlim, dldt, jnp.zeros((5, SSM_W), F32)], axis=0)
        dlam_ref[...] = dl
        grp = jnp.where(_bd_mask((SSM_W, 128), SSM_STATE, 1), 1.0, 0.0)
        dgrp_ref[...] = jnp.dot(dl, grp, preferred_element_type=F32, precision=lax.Precision.HIGHEST)

    vec = pl.BlockSpec((None, 8, SSM_W), lambda l: (l, 0, 0))
    bsp = pl.BlockSpec((None, 2, MIX_W, SSM_W), lambda l: (l, 0, 0, 0))
    csp = pl.BlockSpec((None, 2, SSM_W, MIX_W), lambda l: (l, 0, 0, 0))
    return pl.pallas_call(
        body, name="s5_params_bwd", grid=(L,),
        in_specs=[vec, bsp, bsp, csp, vec],
        out_specs=[vec, bsp, csp, pl.BlockSpec((None, 8, 128), lambda l: (l, 0, 0))],
        out_shape=[jax.ShapeDtypeStruct((L, 8, SSM_W), F32),
                   jax.ShapeDtypeStruct((L, 2, MIX_W, SSM_W), F32),
                   jax.ShapeDtypeStruct((L, 2, SSM_W, MIX_W), F32),
                   jax.ShapeDtypeStruct((L, 8, 128), F32)],
        compiler_params=_cparams(("arbitrary",)),
    )(lam, b_t, gb, gc, da)


def s5_bu(l, p, b_bd, tm=512):
    T = p.shape[0]

    def body(l_ref, u_ref, b_ref, bu_ref):
        u = u_ref[...].astype(BF16)
        bu_ref[0] = _dot(u, b_ref[0])
        bu_ref[1] = _dot(u, b_ref[1])

    grid_spec = pltpu.PrefetchScalarGridSpec(
        num_scalar_prefetch=1, grid=(T // tm,),
        in_specs=[pl.BlockSpec((tm, MIX_W), lambda i, l: (i, 3)),
                  pl.BlockSpec((None, 2, MIX_W, SSM_W), lambda i, l: (l[0], 0, 0, 0))],
        out_specs=pl.BlockSpec((2, tm, SSM_W), lambda i, l: (0, i, 0)),
    )
    return pl.pallas_call(
        body, name="s5_bu", grid_spec=grid_spec,
        out_shape=jax.ShapeDtypeStruct((2, T, SSM_W), F32),
        compiler_params=_cparams(("arbitrary",)),
    )(l, p, b_bd)


def s5_scan(l, avec, xs, reverse):
    T = xs.shape[1]
    CH = SEQ_CHUNK
    nC = T // CH
    LW = 128
    n_steps = CH.bit_length() - 1

    def body(l_ref, a_ref, x_ref, s_ref):
        ar = a_ref[0:1, :]
        ai = -a_ref[1:2, :] if reverse else a_ref[1:2, :]
        pows = [(ar, ai)]
        for _ in range(n_steps - 1):
            r, i = pows[-1]
            pows.append((r * r - i * i, 2.0 * r * i))
        row = lax.broadcasted_iota(jnp.int32, (CH, LW), 0)

        def local_scan(re, im):
            for k in range(n_steps):
                d = 1 << k
                pr, pi = pows[k]
                if reverse:
                    keep = row < CH - d
                    sr, si = _shift_up(re, d), _shift_up(im, d)
                else:
                    keep = row >= d
                    sr, si = _shift_down(re, d), _shift_down(im, d)
                sr = jnp.where(keep, sr, 0.0)
                si = jnp.where(keep, si, 0.0)
                re, im = re + pr * sr - pi * si, im + pr * si + pi * sr
            return re, im

        edge = CH - 1 if reverse else 0
        pw_re, pw_im = local_scan(jnp.where(row == edge, ar, 0.0), jnp.where(row == edge, ai, 0.0))
        last = 0 if reverse else CH - 1

        def chunk(c, carry):
            cr, ci = carry
            cc = nC - 1 - c if reverse else c
            t0 = pl.multiple_of(cc * CH, CH)
            re, im = local_scan(x_ref[0, pl.ds(t0, CH), :], x_ref[1, pl.ds(t0, CH), :])
            re2 = re + pw_re * cr - pw_im * ci
            im2 = im + pw_re * ci + pw_im * cr
            s_ref[0, pl.ds(t0, CH), :] = re2
            s_ref[1, pl.ds(t0, CH), :] = im2
            return re2[last:last + 1, :], im2[last:last + 1, :]

        lax.fori_loop(0, nC, chunk, (jnp.zeros((1, LW), F32), jnp.zeros((1, LW), F32)))

    grid_spec = pltpu.PrefetchScalarGridSpec(
        num_scalar_prefetch=1, grid=(SSM_W // LW,),
        in_specs=[pl.BlockSpec((None, 8, LW), lambda g, l: (l[0], 0, g)),
                  pl.BlockSpec((2, T, LW), lambda g, l: (0, 0, g))],
        out_specs=pl.BlockSpec((2, T, LW), lambda g, l: (0, 0, g)),
    )
    return pl.pallas_call(
        body, name="s5_scan_rev" if reverse else "s5_scan_fwd", grid_spec=grid_spec,
        out_shape=jax.ShapeDtypeStruct((2, T, SSM_W), F32),
        compiler_params=_cparams(("arbitrary",)),
    )(l, avec, xs)


_GELU_C = 0.7978845608028654
_GELU_K = 0.044715


def _s5_y(u, s_ref, c_ref, d_row):
    y = _dot(s_ref[0].astype(BF16), c_ref[0]) - _dot(s_ref[1].astype(BF16), c_ref[1])
    return y + d_row * u


def s5_out(l, p, s, c_bd, ssm_d, tm=512):
    T = p.shape[0]

    def body(l_ref, u_ref, s_ref, c_ref, d_ref, yg_ref):
        y = _s5_y(u_ref[...], s_ref, c_ref, d_ref[0:1, :])
        th = jnp.tanh(_GELU_C * (y + _GELU_K * y * y * y))
        yg_ref[...] = (0.5 * y * (1.0 + th)).astype(BF16)

    grid_spec = pltpu.PrefetchScalarGridSpec(
        num_scalar_prefetch=1, grid=(T // tm,),
        in_specs=[pl.BlockSpec((tm, MIX_W), lambda i, l: (i, 3)),
                  pl.BlockSpec((2, tm, SSM_W), lambda i, l: (0, i, 0)),
                  pl.BlockSpec((None, 2, SSM_W, MIX_W), lambda i, l: (l[0], 0, 0, 0)),
                  pl.BlockSpec((None, 8, MIX_W), lambda i, l: (l[0], 0, 0))],
        out_specs=pl.BlockSpec((tm, MIX_W), lambda i, l: (i, 0)),
    )
    return pl.pallas_call(
        body, name="s5_out", grid_spec=grid_spec,
        out_shape=jax.ShapeDtypeStruct((T, MIX_W), BF16),
        compiler_params=_cparams(("arbitrary",)),
    )(l, p, s, c_bd, ssm_d)


def s5_bwd_y(l, p, s, dyg, c_bd, ssm_d, tm=512):
    T = p.shape[0]

    def body(l_ref, u_ref, s_ref, dyg_ref, c_ref, d_ref, ds_ref, du_ref, gc_ref, dd_ref):
        @pl.when(pl.program_id(0) == 0)
        def _():
            gc_ref[...] = jnp.zeros_like(gc_ref)
            dd_ref[...] = jnp.zeros_like(dd_ref)

        u = u_ref[...]
        y = _s5_y(u, s_ref, c_ref, d_ref[0:1, :])
        inner = _GELU_C * (y + _GELU_K * y * y * y)
        th = jnp.tanh(inner)
        dgelu = 0.5 * (1.0 + th) + 0.5 * y * (1.0 - th * th) * (_GELU_C * (1.0 + 3.0 * _GELU_K * y * y))
        dy = dyg_ref[...] * dgelu
        dd_ref[0:1, :] += _colsum(dy * u)
        du_ref[...] = dy * d_ref[0:1, :]
        dyb = dy.astype(BF16)
        ds_ref[0] = _dot_nt(dyb, c_ref[0])
        ds_ref[1] = -_dot_nt(dyb, c_ref[1])
        gc_ref[0] += _dot_tn(s_ref[0].astype(BF16), dyb)
        gc_ref[1] -= _dot_tn(s_ref[1].astype(BF16), dyb)

    grid_spec = pltpu.PrefetchScalarGridSpec(
        num_scalar_prefetch=1, grid=(T // tm,),
        in_specs=[pl.BlockSpec((tm, MIX_W), lambda i, l: (i, 3)),
                  pl.BlockSpec((2, tm, SSM_W), lambda i, l: (0, i, 0)),
                  pl.BlockSpec((tm, MIX_W), lambda i, l: (i, 0)),
                  pl.BlockSpec((None, 2, SSM_W, MIX_W), lambda i, l: (l[0], 0, 0, 0)),
                  pl.BlockSpec((None, 8, MIX_W), lambda i, l: (l[0], 0, 0))],
        out_specs=[pl.BlockSpec((2, tm, SSM_W), lambda i, l: (0, i, 0)),
                   pl.BlockSpec((tm, MIX_W), lambda i, l: (i, 0)),
                   pl.BlockSpec((2, SSM_W, MIX_W), lambda i, l: (0, 0, 0)),
                   pl.BlockSpec((8, MIX_W), lambda i, l: (0, 0))],
    )
    return pl.pallas_call(
        body, name="s5_bwd_y", grid_spec=grid_spec,
        out_shape=[jax.ShapeDtypeStruct((2, T, SSM_W), F32), jax.ShapeDtypeStruct((T, MIX_W), F32),
                   jax.ShapeDtypeStruct((2, SSM_W, MIX_W), F32), jax.ShapeDtypeStruct((8, MIX_W), F32)],
        compiler_params=_cparams(("arbitrary",)),
    )(l, p, s, dyg, c_bd, ssm_d)


def s5_bwd_u(l, p, lam_s, du_skip, b_bd, tm=512):
    T = p.shape[0]

    def body(l_ref, u_ref, ls_ref, dus_ref, b_ref, du_ref, gb_ref):
        @pl.when(pl.program_id(0) == 0)
        def _():
            gb_ref[...] = jnp.zeros_like(gb_ref)

        u = u_ref[...].astype(BF16)
        lr = ls_ref[0].astype(BF16)
        li = ls_ref[1].astype(BF16)
        gb_ref[0] += _dot_tn(u, lr)
        gb_ref[1] += _dot_tn(u, li)
        du_ref[...] = (dus_ref[...] + _dot_nt(lr, b_ref[0]) + _dot_nt(li, b_ref[1])).astype(BF16)

    grid_spec = pltpu.PrefetchScalarGridSpec(
        num_scalar_prefetch=1, grid=(T // tm,),
        in_specs=[pl.BlockSpec((tm, MIX_W), lambda i, l: (i, 3)),
                  pl.BlockSpec((2, tm, SSM_W), lambda i, l: (0, i, 0)),
                  pl.BlockSpec((tm, MIX_W), lambda i, l: (i, 0)),
                  pl.BlockSpec((None, 2, MIX_W, SSM_W), lambda i, l: (l[0], 0, 0, 0))],
        out_specs=[pl.BlockSpec((tm, MIX_W), lambda i, l: (i, 0)),
                   pl.BlockSpec((2, MIX_W, SSM_W), lambda i, l: (0, 0, 0))],
    )
    return pl.pallas_call(
        body, name="s5_bwd_u", grid_spec=grid_spec,
        out_shape=[jax.ShapeDtypeStruct((T, MIX_W), BF16), jax.ShapeDtypeStruct((2, MIX_W, SSM_W), F32)],
        compiler_params=_cparams(("arbitrary",)),
    )(l, p, lam_s, du_skip, b_bd)


def s5_bwd_a(s, lam_s):
    T = s.shape[1]
    nC = T // SEQ_CHUNK
    LW = 128

    def body(s_ref, ls_ref, da_ref):
        dre = jnp.zeros((1, LW), F32)
        dim = jnp.zeros((1, LW), F32)
        for c in range(nC):
            t0 = c * SEQ_CHUNK
            sr = _shift_down(_rows_with_lead(lambda a, n: s_ref[0, a:a + n, :], c, LW), 1)[HALO:]
            si = _shift_down(_rows_with_lead(lambda a, n: s_ref[1, a:a + n, :], c, LW), 1)[HALO:]
            lr = ls_ref[0, t0:t0 + SEQ_CHUNK, :]
            li = ls_ref[1, t0:t0 + SEQ_CHUNK, :]
            dre += _colsum(sr * lr + si * li)
            dim += _colsum(sr * li - si * lr)
        da_ref[...] = jnp.concatenate([dre, dim, jnp.zeros((6, LW), F32)], axis=0)

    blk = pl.BlockSpec((2, T, LW), lambda g: (0, 0, g))
    return pl.pallas_call(
        body, name="s5_bwd_a", grid=(SSM_W // LW,),
        in_specs=[blk, blk],
        out_specs=pl.BlockSpec((8, LW), lambda g: (0, g)),
        out_shape=jax.ShapeDtypeStruct((8, SSM_W), F32),
        compiler_params=_cparams(("arbitrary",)),
    )(s, lam_s)


SB_BLK = 128
SB_SCALE = SB_HEAD ** -0.5


def _split_bf16(x):
    hi = x.astype(BF16)
    return hi, (x - hi.astype(F32)).astype(BF16)


def _dot_split(x, tri):
    hi, lo = _split_bf16(x)
    return _dot(hi, tri) + _dot(lo, tri)


def _sb_logits(q, k_blk, r0, c0):
    z = _dot_nt(q, k_blk)
    row = lax.broadcasted_iota(jnp.int32, z.shape, 0) + r0
    col = lax.broadcasted_iota(jnp.int32, z.shape, 1) + c0
    valid = col < row
    sp = jnp.log(1.0 + jnp.exp(-jnp.abs(z)))
    ls_pos = jnp.minimum(z, 0.0) - sp
    lk = jnp.where(valid, jnp.minimum(-z, 0.0) - sp, 0.0)
    return z, valid, ls_pos, lk


def _tri(lower):
    r = lax.broadcasted_iota(jnp.int32, (SB_BLK, SB_BLK), 0)
    c = lax.broadcasted_iota(jnp.int32, (SB_BLK, SB_BLK), 1)
    return jnp.where(r > c if lower else r < c, 1.0, 0.0).astype(BF16)


def sb_fwd(p):
    T = p.shape[0]
    W = MIX_W
    nB = T // SB_BLK

    def body(q_ref, k_ref, v_ref, o_ref):
        tri = _tri(True)
        lane = lax.broadcasted_iota(jnp.int32, (SB_BLK, SB_BLK), 1)
        for slab in range(W // SB_BLK):
            ls = slice(slab * SB_BLK, (slab + 1) * SB_BLK)
            for hh in range(2):
                hmask = (lane < SB_HEAD) if hh == 0 else (lane >= SB_HEAD)

                def qblock(i, _, ls=ls, hmask=hmask, hh=hh):
                    r0 = pl.multiple_of(i * SB_BLK, SB_BLK)
                    q = (jnp.where(hmask, q_ref[pl.ds(r0, SB_BLK), ls], 0.0) * SB_SCALE).astype(BF16)

                    def kblock(jj, carry):
                        acc, run = carry
                        c0 = pl.multiple_of((i - jj) * SB_BLK, SB_BLK)
                        kb = k_ref[pl.ds(c0, SB_BLK), ls].astype(BF16)
                        _, valid, ls_pos, lk = _sb_logits(q, kb, r0, c0)
                        logw = ls_pos + _dot_split(lk, tri) + run
                        a = jnp.where(valid, jnp.exp(logw), 0.0).astype(BF16)
                        acc = acc + _dot(a, v_ref[pl.ds(c0, SB_BLK), ls].astype(BF16))
                        return acc, run + jnp.sum(lk, axis=1, keepdims=True)

                    acc, _ = lax.fori_loop(0, i + 1, kblock,
                                           (jnp.zeros((SB_BLK, SB_BLK), F32), jnp.zeros((SB_BLK, 1), F32)))
                    acc = jnp.where(hmask, acc, 0.0)
                    if hh == 0:
                        o_ref[pl.ds(r0, SB_BLK), ls] = acc.astype(BF16)
                    else:
                        o_ref[pl.ds(r0, SB_BLK), ls] = (o_ref[pl.ds(r0, SB_BLK), ls].astype(F32) + acc).astype(BF16)
                    return 0

                lax.fori_loop(0, nB, qblock, 0)

    return pl.pallas_call(
        body, name="sb_fwd", grid=(1,),
        in_specs=[pl.BlockSpec((T, W), lambda i: (0, 5)), pl.BlockSpec((T, W), lambda i: (0, 6)),
                  pl.BlockSpec((T, W), lambda i: (0, 7))],
        out_specs=pl.BlockSpec((T, W), lambda i: (0, 0)),
        out_shape=jax.ShapeDtypeStruct((T, W), BF16),
        compiler_params=_cparams(("arbitrary",)),
    )(p, p, p)


def sb_bwd(p, do):
    T = p.shape[0]
    W = MIX_W
    nB = T // SB_BLK

    def body(q_ref, k_ref, v_ref, do_ref, dqkv_ref, dq_sc, dk_sc, dv_sc, run_sc):
        tri_gt = _tri(True)
        tri_lt = _tri(False)
        lane = lax.broadcasted_iota(jnp.int32, (SB_BLK, SB_BLK), 1)
        dq_sc[...] = jnp.zeros_like(dq_sc)
        dk_sc[...] = jnp.zeros_like(dk_sc)
        dv_sc[...] = jnp.zeros_like(dv_sc)
        for slab in range(W // SB_BLK):
            ls = slice(slab * SB_BLK, (slab + 1) * SB_BLK)
            for hh in range(2):
                hmask = (lane < SB_HEAD) if hh == 0 else (lane >= SB_HEAD)

                def qblock(i, _, ls=ls, hmask=hmask):
                    r0 = pl.multiple_of(i * SB_BLK, SB_BLK)
                    q = (jnp.where(hmask, q_ref[pl.ds(r0, SB_BLK), ls], 0.0) * SB_SCALE).astype(BF16)
                    dob = jnp.where(hmask, do_ref[pl.ds(r0, SB_BLK), ls], 0.0).astype(BF16)

                    def suffix(jj, run):
                        j = i - jj
                        c0 = pl.multiple_of(j * SB_BLK, SB_BLK)
                        kb = k_ref[pl.ds(c0, SB_BLK), ls].astype(BF16)
                        _, _, _, lk = _sb_logits(q, kb, r0, c0)
                        run_sc[pl.ds(c0, SB_BLK), :] = jnp.broadcast_to(run, (SB_BLK, SB_BLK))
                        return run + jnp.sum(lk, axis=1, keepdims=True)

                    lax.fori_loop(0, i + 1, suffix, jnp.zeros((SB_BLK, 1), F32))

                    def kblock(j, carry):
                        dq_acc, pre = carry
                        c0 = pl.multiple_of(j * SB_BLK, SB_BLK)
                        kb = k_ref[pl.ds(c0, SB_BLK), ls].astype(BF16)
                        vb = v_ref[pl.ds(c0, SB_BLK), ls].astype(BF16)
                        z, valid, ls_pos, lk = _sb_logits(q, kb, r0, c0)
                        logw = ls_pos + _dot_split(lk, tri_gt) + run_sc[pl.ds(c0, SB_BLK), 0:1]
                        a = jnp.where(valid, jnp.exp(logw), 0.0)
                        dlw = _dot_nt(dob, vb) * a
                        g = pre + _dot_split(dlw, tri_lt)
                        sig = _sigmoid(z)
                        dz = jnp.where(valid, dlw * (1.0 - sig) - g * sig, 0.0).astype(BF16)
                        dk_sc[pl.ds(c0, SB_BLK), ls] += _dot_tn(dz, q)
                        dv_sc[pl.ds(c0, SB_BLK), ls] += _dot_tn(a.astype(BF16), dob)
                        return dq_acc + _dot(dz, kb), pre + jnp.sum(dlw, axis=1, keepdims=True)

                    dq_acc, _ = lax.fori_loop(0, i + 1, kblock,
                                              (jnp.zeros((SB_BLK, SB_BLK), F32), jnp.zeros((SB_BLK, 1), F32)))
                    dq_sc[pl.ds(r0, SB_BLK), ls] += jnp.where(hmask, dq_acc * SB_SCALE, 0.0)
                    return 0

                lax.fori_loop(0, nB, qblock, 0)
        dqkv_ref[:, 0:W] = dq_sc[...].astype(BF16)
        dqkv_ref[:, W:2 * W] = dk_sc[...].astype(BF16)
        dqkv_ref[:, 2 * W:3 * W] = dv_sc[...].astype(BF16)

    return pl.pallas_call(
        body, name="sb_bwd", grid=(1,),
        in_specs=[pl.BlockSpec((T, W), lambda i: (0, 5)), pl.BlockSpec((T, W), lambda i: (0, 6)),
                  pl.BlockSpec((T, W), lambda i: (0, 7)), pl.BlockSpec((T, W), lambda i: (0, 0))],
        out_specs=pl.BlockSpec((T, 3 * W), lambda i: (0, 0)),
        out_shape=jax.ShapeDtypeStruct((T, 3 * W), BF16),
        scratch_shapes=[pltpu.VMEM((T, W), F32), pltpu.VMEM((T, W), F32), pltpu.VMEM((T, W), F32),
                        pltpu.VMEM((T, SB_BLK), F32)],
        compiler_params=_cparams(("arbitrary",)),
    )(p, p, p, do)


def _dot_cols(a, w_ref):
    return jnp.concatenate([_dot(a, w_ref[j]) for j in range(N_DEV)], axis=1)


def _dot_cols_nt(dy, w_ref):
    n = w_ref.shape[2]
    out = _dot_nt(dy[:, 0:n], w_ref[0])
    for j in range(1, N_DEV):
        out += _dot_nt(dy[:, j * n:(j + 1) * n], w_ref[j])
    return out


def _acc_cols_tn(acc_ref, a, dy):
    n = acc_ref.shape[2]
    for j in range(N_DEV):
        acc_ref[j] += _dot_tn(a, dy[:, j * n:(j + 1) * n])


def _merge_branches(za_ref, yg_ref, z_ref, o_ref, gate_refs, wc_ref, wglu_ref, wp_ref, ws_ref):
    D = D_MODEL
    glu = _dot_cols(yg_ref[...], wglu_ref)
    glu_a, sg = glu[:, :D], _sigmoid(glu[:, D:])
    ys = [_dot_cols(za_ref[...], wc_ref), glu_a * sg, _dot_cols(z_ref[...], wp_ref), _dot_cols(o_ref[...], ws_ref)]
    gs = [_sigmoid(g[...]) for g in gate_refs]
    merged = gs[0] * ys[0] + gs[1] * ys[1] + gs[2] * ys[2] + gs[3] * ys[3]
    return ys, gs, glu_a, sg, merged


def _merge_specs(tm, D):
    W = MIX_W
    br = pl.BlockSpec((tm, W), lambda i, l: (i, 0))
    gates = [pl.BlockSpec((tm, D), functools.partial(lambda i, l, b: (i, 2 + b), b=b)) for b in range(4)]
    wsm = pl.BlockSpec((None, N_DEV, W, D // N_DEV), lambda i, l: (l[0], 0, 0, 0))
    weights = [wsm, pl.BlockSpec((None, N_DEV, W, 2 * D // N_DEV), lambda i, l: (l[0], 0, 0, 0)), wsm, wsm,
               pl.BlockSpec((None, D, D), lambda i, l: (l[0], 0, 0))]
    return [br] * 4 + gates, weights


def merge_fwd(l, p, za, yg, z, o, x, pv, wc, wglu, wp, ws, wo, tm=256):
    T, D = x.shape

    def body(l_ref, za_ref, yg_ref, z_ref, o_ref, g0, g1, g2, g3, x_ref, pv_ref,
             wc_ref, wglu_ref, wp_ref, ws_ref, wo_ref, xn_ref, m_ref):
        _, _, _, _, merged = _merge_branches(za_ref, yg_ref, z_ref, o_ref, (g0, g1, g2, g3),
                                             wc_ref, wglu_ref, wp_ref, ws_ref)
        m = _dot(merged.astype(BF16), wo_ref[...])
        m_ref[...] = m
        xn_ref[...] = _postnorm_res(x_ref[...], m, pv_ref, 1.0)

    acts, weights = _merge_specs(tm, D)
    tile = pl.BlockSpec((tm, D), lambda i, l: (i, 0))
    grid_spec = pltpu.PrefetchScalarGridSpec(
        num_scalar_prefetch=1, grid=(T // tm,),
        in_specs=acts + [tile, pl.BlockSpec((8, D), lambda i, l: (0, 0))] + weights,
        out_specs=[tile, tile],
    )
    return pl.pallas_call(
        body, name="merge_fwd", grid_spec=grid_spec,
        out_shape=[jax.ShapeDtypeStruct((T, D), F32), jax.ShapeDtypeStruct((T, D), F32)],
        compiler_params=_cparams(("arbitrary",)),
    )(l, za, yg, z, o, p, p, p, p, x, pv, wc, wglu, wp, ws, wo)


def merge_bwd(l, p, za, yg, z, o, m, dxn, pv, wc, wglu, wp, ws, wo, tm=128):
    T, D = m.shape
    W = MIX_W
    ni = T // tm

    def body(l_ref, za_ref, yg_ref, z_ref, o_ref, g0, g1, g2, g3, m_ref, dxn_ref, pv_ref,
             wc_ref, wglu_ref, wp_ref, ws_ref, wo_ref,
             dza_ref, dyg_ref, dz_ref, do_ref, dg_ref, pg_ref, gwc_ref, gwglu_ref, gwp_ref, gws_ref, gwo_ref,
             awc, awglu, awp, aws, awo):
        i = pl.program_id(0)

        @pl.when(i == 0)
        def _():
            pg_ref[...] = jnp.zeros_like(pg_ref)
            for a in (awc, awglu, awp, aws, awo):
                a[...] = jnp.zeros_like(a)

        ys, gs, glu_a, sg, merged = _merge_branches(za_ref, yg_ref, z_ref, o_ref, (g0, g1, g2, g3),
                                                    wc_ref, wglu_ref, wp_ref, ws_ref)
        dm = _postnorm_bwd(dxn_ref[...], m_ref[...], pv_ref, pg_ref, 1.0).astype(BF16)
        awo[...] += _dot_tn(merged.astype(BF16), dm)
        dmerged = _dot_nt(dm, wo_ref[...])
        for b in range(4):
            dg_ref[:, b * D:(b + 1) * D] = (dmerged * ys[b] * gs[b] * (1.0 - gs[b])).astype(BF16)
        dya = (dmerged * gs[0]).astype(BF16)
        _acc_cols_tn(awc, za_ref[...], dya)
        dza_ref[...] = _dot_cols_nt(dya, wc_ref)
        dyc = (dmerged * gs[2]).astype(BF16)
        _acc_cols_tn(awp, z_ref[...], dyc)
        dz_ref[...] = _dot_cols_nt(dyc, wp_ref)
        dyd = (dmerged * gs[3]).astype(BF16)
        _acc_cols_tn(aws, o_ref[...], dyd)
        do_ref[...] = _dot_cols_nt(dyd, ws_ref)
        dyb = dmerged * gs[1]
        dglu = jnp.concatenate([dyb * sg, dyb * glu_a * sg * (1.0 - sg)], axis=1).astype(BF16)
        _acc_cols_tn(awglu, yg_ref[...], dglu)
        dyg_ref[...] = _dot_cols_nt(dglu, wglu_ref)

        @pl.when(i == ni - 1)
        def _():
            gwc_ref[...] = awc[...].astype(BF16)
            gwglu_ref[...] = awglu[...].astype(BF16)
            gwp_ref[...] = awp[...].astype(BF16)
            gws_ref[...] = aws[...].astype(BF16)
            gwo_ref[...] = awo[...].astype(BF16)

    acts, weights = _merge_specs(tm, D)
    tile = pl.BlockSpec((tm, D), lambda i, l: (i, 0))
    br = pl.BlockSpec((tm, W), lambda i, l: (i, 0))
    full = lambda *s: pl.BlockSpec(s, lambda i, l: (0,) * len(s))
    sm, glu_s = (N_DEV, W, D // N_DEV), (N_DEV, W, 2 * D // N_DEV)
    grid_spec = pltpu.PrefetchScalarGridSpec(
        num_scalar_prefetch=1, grid=(ni,),
        in_specs=acts + [tile, tile, pl.BlockSpec((8, D), lambda i, l: (0, 0))] + weights,
        out_specs=[br, br, br, br, pl.BlockSpec((tm, 4 * D), lambda i, l: (i, 0)), full(8, D),
                   full(*sm), full(*glu_s), full(*sm), full(*sm), full(D, D)],
        scratch_shapes=[pltpu.VMEM(sm, F32), pltpu.VMEM(glu_s, F32), pltpu.VMEM(sm, F32),
                        pltpu.VMEM(sm, F32), pltpu.VMEM((D, D), F32)],
    )
    f32br = jax.ShapeDtypeStruct((T, W), F32)
    return pl.pallas_call(
        body, name="merge_bwd", grid_spec=grid_spec,
        out_shape=[f32br, f32br, f32br, f32br, jax.ShapeDtypeStruct((T, 4 * D), BF16),
                   jax.ShapeDtypeStruct((8, D), F32),
                   jax.ShapeDtypeStruct(sm, BF16), jax.ShapeDtypeStruct(glu_s, BF16),
                   jax.ShapeDtypeStruct(sm, BF16), jax.ShapeDtypeStruct(sm, BF16),
                   jax.ShapeDtypeStruct((D, D), BF16)],
        compiler_params=_cparams(("arbitrary",)),
    )(l, za, yg, z, o, p, p, p, p, m, dxn, pv, wc, wglu, wp, ws, wo)


def dp_assemble(d_conv, d_ssm, d_pool, d_qkv, d_gates, tm=512):
    T = d_conv.shape[0]
    W = MIX_W

    def body(c_ref, s_ref, p_ref, q_ref, g_ref, dp_ref):
        dp_ref[:, 0:3 * W] = c_ref[...]
        dp_ref[:, 3 * W:4 * W] = s_ref[...]
        dp_ref[:, 4 * W:5 * W] = p_ref[...]
        dp_ref[:, 5 * W:8 * W] = q_ref[...]
        dp_ref[:, GATE_OFF:] = g_ref[...]

    row = lambda w: pl.BlockSpec((tm, w), lambda i: (i, 0))
    return pl.pallas_call(
        body, name="dp_assemble", grid=(T // tm,),
        in_specs=[row(3 * W), row(W), row(W), row(3 * W), row(4 * D_MODEL)],
        out_specs=row(IN_COLS),
        out_shape=jax.ShapeDtypeStruct((T, IN_COLS), BF16),
        compiler_params=_cparams(("arbitrary",)),
    )(d_conv, d_ssm, d_pool, d_qkv, d_gates)


def loss_head(y, target, tm=512):
    T, D = y.shape

    def body(y_ref, t_ref, dy_ref, loss_ref):
        @pl.when(pl.program_id(0) == 0)
        def _():
            loss_ref[...] = jnp.zeros_like(loss_ref)

        err = y_ref[...] - t_ref[...]
        dy_ref[...] = err * (1.0 / D)
        loss_ref[...] += jnp.sum(err * err) * (0.5 / D)

    tile = pl.BlockSpec((tm, D), lambda i: (i, 0))
    return pl.pallas_call(
        body, name="loss_head", grid=(T // tm,),
        in_specs=[tile, tile],
        out_specs=[tile, pl.BlockSpec((8, 128), lambda i: (0, 0))],
        out_shape=[jax.ShapeDtypeStruct((T, D), F32), jax.ShapeDtypeStruct((8, 128), F32)],
        compiler_params=_cparams(("arbitrary",)),
    )(y, target)


def cast_weights(ws):
    L = ws[0].shape[0]

    def body(*refs):
        n = len(refs) // 2
        for src, dst in zip(refs[:n], refs[n:]):
            dst[...] = src[...].astype(BF16)

    def spec(w):
        nd = w.ndim
        return pl.BlockSpec((None,) + w.shape[1:], lambda l, nd=nd: (l,) + (0,) * (nd - 1))

    return pl.pallas_call(
        body, name="cast_weights", grid=(L,),
        in_specs=[spec(w) for w in ws], out_specs=[spec(w) for w in ws],
        out_shape=[jax.ShapeDtypeStruct(w.shape, BF16) for w in ws],
        compiler_params=_cparams(("arbitrary",)),
    )(*ws)


def _silu(x):
    return x * _sigmoid(x)


def ada_fwd(c_all, w_ada, b_cols):
    L, D, n = w_ada.shape

    def body(c_ref, w_ref, b_ref, o_ref):
        c_act = _silu(c_ref[...]).astype(BF16)
        o_ref[...] = _dot(c_act, w_ref[...].astype(BF16)) + b_ref[...]

    return pl.pallas_call(
        body, name="ada_fwd", grid=(L,),
        in_specs=[pl.BlockSpec((N_DEV, D), lambda l: (0, 0)), pl.BlockSpec((None, D, n), lambda l: (l, 0, 0)),
                  pl.BlockSpec((None, 1, n), lambda l: (l, 0, 0))],
        out_specs=pl.BlockSpec((None, N_DEV, n), lambda l: (l, 0, 0)),
        out_shape=jax.ShapeDtypeStruct((L, N_DEV, n), F32),
        compiler_params=_cparams(("arbitrary",)),
    )(c_all, w_ada, b_cols)


def _adamw(w, g, m, v):
    m = ADAM_B1 * m + (1.0 - ADAM_B1) * g
    v = ADAM_B2 * v + (1.0 - ADAM_B2) * (g * g)
    m_hat = m / (1.0 - ADAM_B1 ** ADAM_STEP)
    v_hat = v / (1.0 - ADAM_B2 ** ADAM_STEP)
    delta = -ADAM_LR * (m_hat / (jnp.sqrt(v_hat) + ADAM_EPS) + ADAM_WD * w)
    return delta, m, v


def ada_update(c_all, dada_cols, w, m, v, rb=256):
    L, D, n = w.shape

    def body(c_ref, d_ref, w_ref, m_ref, v_ref, g_ref, dl_ref, nm_ref, nv_ref):
        c_act = _silu(c_ref[...]).astype(BF16)
        g = _dot_tn(c_act, d_ref[...].astype(BF16))
        g_ref[...] = g
        dl_ref[...], nm_ref[...], nv_ref[...] = _adamw(w_ref[...], g, m_ref[...], v_ref[...])

    blk = pl.BlockSpec((None, rb, n), lambda l, i: (l, i, 0))
    out = jax.ShapeDtypeStruct((L, D, n), F32)
    return pl.pallas_call(
        body, name="ada_update", grid=(L, D // rb),
        in_specs=[pl.BlockSpec((N_DEV, rb), lambda l, i: (0, i)),
                  pl.BlockSpec((None, N_DEV, n), lambda l, i: (l, 0, 0)), blk, blk, blk],
        out_specs=[blk, blk, blk, blk], out_shape=[out, out, out, out],
        compiler_params=_cparams(("arbitrary", "arbitrary")),
    )(c_all, dada_cols, w, m, v)


SUM_UPDATE_RECV_BYTES = 12 * 1024 * 1024


def sum_update(recvs, w, m, v):
    S, R, C = w.shape
    assert len(recvs) == S
    rb_max = SUM_UPDATE_RECV_BYTES // (S * N_DEV * C * 2)
    rb = max(r for r in range(8, R + 1, 8) if R % r == 0 and (r <= rb_max or r == 8))

    def body(*refs):
        r_refs = refs[:S]
        w_ref, m_ref, v_ref, g_ref, dl_ref, nm_ref, nv_ref = refs[S:]
        for s in range(S):
            @pl.when(pl.program_id(0) == s)
            def _(s=s):
                g = r_refs[s][0].astype(F32)
                for d in range(1, N_DEV):
                    g += r_refs[s][d].astype(F32)
                g_ref[...] = g
                dl_ref[...], nm_ref[...], nv_ref[...] = _adamw(w_ref[...], g, m_ref[...], v_ref[...])

    def rspec(s):
        last = R // rb - 1
        return pl.BlockSpec((N_DEV, rb, C), lambda sl, i: (0, jnp.where(sl == s, i, jnp.where(sl < s, 0, last)), 0))

    blk = pl.BlockSpec((None, rb, C), lambda sl, i: (sl, i, 0))
    out = jax.ShapeDtypeStruct((S, R, C), F32)
    return pl.pallas_call(
        body, name="sum_update", grid=(S, R // rb),
        in_specs=[rspec(s) for s in range(S)] + [blk, blk, blk],
        out_specs=[blk, blk, blk, blk], out_shape=[out, out, out, out],
        compiler_params=_cparams(("arbitrary", "arbitrary")),
    )(*recvs, w, m, v)


def small_sum(gathered):
    _, R, C = gathered.shape

    def body(g_ref, o_ref):
        acc = g_ref[0]
        for d in range(1, N_DEV):
            acc += g_ref[d]
        o_ref[...] = acc

    return pl.pallas_call(
        body, name="small_sum", grid=(1,),
        in_specs=[pl.BlockSpec((N_DEV, R, C), lambda i: (0, 0, 0))],
        out_specs=pl.BlockSpec((R, C), lambda i: (0, 0)),
        out_shape=jax.ShapeDtypeStruct((R, C), F32),
        compiler_params=_cparams(("arbitrary",)),
    )(gathered)


def small_update(w, g, m, v):
    def body(w_ref, g_ref, m_ref, v_ref, dl_ref, nm_ref, nv_ref):
        dl_ref[...], nm_ref[...], nv_ref[...] = _adamw(w_ref[...], g_ref[...], m_ref[...], v_ref[...])

    blk = pl.BlockSpec(w.shape, lambda i: (0, 0))
    out = jax.ShapeDtypeStruct(w.shape, F32)
    return pl.pallas_call(
        body, name="small_update", grid=(1,),
        in_specs=[blk] * 4, out_specs=[blk] * 3, out_shape=[out] * 3,
        compiler_params=_cparams(("arbitrary",)),
    )(w, g, m, v)


MESH = pl.DeviceIdType.MESH
ANY = pl.BlockSpec(memory_space=pl.ANY)


def _coords():
    return lax.axis_index("x"), lax.axis_index("y"), lax.axis_index("c")


def _dev_index(x, y, c):
    return 4 * x + 2 * y + c


def _at_dev(ref, p, dev):
    return ref.at[(slice(None),) * p + (dev,)]


def all_gather(arrays, ps):
    n = len(arrays)

    def body(*refs):
        ins, outs = refs[:n], refs[n:2 * n]
        send_sems, recv_sems, local_sems = refs[2 * n:]
        x, y, c = _coords()
        me, sibling = (x, y, c), (x, y, 1 - c)
        chips = [(1 - x, y), (x, 1 - y), (1 - x, 1 - y)]

        def copy(a, k, block, to, src=None):
            dst = _at_dev(outs[a], ps[a], _dev_index(*block))
            return pltpu.make_async_remote_copy(
                src_ref=dst if src is None else src, dst_ref=dst,
                send_sem=send_sems.at[a, k], recv_sem=recv_sems.at[a, k], device_id=to, device_id_type=MESH)

        mine = [pltpu.make_async_copy(ins[a], _at_dev(outs[a], ps[a], _dev_index(*me)), local_sems.at[a])
                for a in range(n)]
        for cp in mine:
            cp.start()
        first = []
        for a in range(n):
            first.append(copy(a, 0, me, sibling, src=ins[a]))
            first += [copy(a, 1 + j, me, (*chip, c), src=ins[a]) for j, chip in enumerate(chips)]
        for cp in first:
            cp.start()
        passed = []
        for j, chip in enumerate(chips):
            for a in range(n):
                copy(a, 1 + j, (*chip, c), me).wait_recv()
                fwd = copy(a, 4 + j, (*chip, c), sibling)
                fwd.start()
                passed.append(fwd)
        for a in range(n):
            copy(a, 0, sibling, me).wait_recv()
            for j, chip in enumerate(chips):
                copy(a, 4 + j, (*chip, 1 - c), me).wait_recv()
        for cp in first + passed:
            cp.wait_send()
        for cp in mine:
            cp.wait()

    out_shape = [jax.ShapeDtypeStruct(a.shape[:p] + (N_DEV,) + a.shape[p:], a.dtype) for a, p in zip(arrays, ps)]
    return pl.pallas_call(
        body, name="all_gather", in_specs=[ANY] * n, out_specs=[ANY] * n, out_shape=out_shape,
        scratch_shapes=[pltpu.SemaphoreType.DMA((n, 7)), pltpu.SemaphoreType.DMA((n, 7)),
                        pltpu.SemaphoreType.DMA((n,))],
        compiler_params=pltpu.CompilerParams(has_side_effects=True),
    )(*arrays)


def exchange_blocks(arrays, ps):
    n = len(arrays)

    def body(*refs):
        ins, outs = refs[:n], refs[n:2 * n]
        send_sems, recv_sems, local_sems = refs[2 * n:]
        x, y, c = _coords()
        me = _dev_index(x, y, c)
        peers = []
        for k in range(1, N_DEV):
            px = 1 - x if k & 4 else x
            py = 1 - y if k & 2 else y
            pc = 1 - c if k & 1 else c
            peers.append((px, py, pc))
        mine = [pltpu.make_async_copy(_at_dev(ins[a], ps[a], me), outs[a].at[me], local_sems.at[a])
                for a in range(n)]
        for cp in mine:
            cp.start()
        def copy(a, k, peer, slot):
            return pltpu.make_async_remote_copy(
                src_ref=_at_dev(ins[a], ps[a], _dev_index(*peer)), dst_ref=outs[a].at[slot],
                send_sem=send_sems.at[a, k], recv_sem=recv_sems.at[a, k], device_id=peer, device_id_type=MESH)

        sends = [copy(a, k, peer, me) for k, peer in enumerate(peers) for a in range(n)]
        for cp in sends:
            cp.start()
        for k, peer in enumerate(peers):
            for a in range(n):
                copy(a, k, peer, _dev_index(*peer)).wait_recv()
        for cp in sends:
            cp.wait_send()
        for cp in mine:
            cp.wait()

    out_shape = [jax.ShapeDtypeStruct((N_DEV,) + a.shape[:p] + a.shape[p + 1:], a.dtype) for a, p in zip(arrays, ps)]
    return pl.pallas_call(
        body, name="exchange_blocks", in_specs=[ANY] * n, out_specs=[ANY] * n, out_shape=out_shape,
        scratch_shapes=[pltpu.SemaphoreType.DMA((n, 7)), pltpu.SemaphoreType.DMA((n, 7)),
                        pltpu.SemaphoreType.DMA((n,))],
        compiler_params=pltpu.CompilerParams(has_side_effects=True),
    )(*arrays)


WEIGHT_NAMES = ("w_ada", "b_ada", "g_pre", "g_post", "w_ff_in", "w_ff_out", "w_in", "conv_w", "w_conv_out",
                "lam_re", "lam_im", "log_dt", "ssm_b_re", "ssm_b_im", "ssm_c_re", "ssm_c_im", "ssm_d", "w_glu",
                "w_pool", "pool_scale", "w_pool_out", "w_sb_out", "w_out")
BIG_NAMES = ("w_ff_in", "w_ff_out", "w_in", "w_conv_out", "w_glu", "w_pool_out", "w_sb_out", "w_out")
SMALL_NAMES = ("b_ada", "g_pre", "g_post", "conv_w", "lam_re", "lam_im", "log_dt", "ssm_b_re", "ssm_b_im",
               "ssm_c_re", "ssm_c_im", "ssm_d", "w_pool", "pool_scale")
PACK_LANES = 128
PACK_ROWS = 8


def _pack(arrays):
    flat = jnp.concatenate([a.reshape(-1) for a in arrays])
    unit = PACK_LANES * PACK_ROWS
    flat = jnp.pad(flat, (0, -flat.shape[0] % unit))
    return flat.reshape(-1, PACK_LANES)


def _unpack(pack, shapes):
    flat = pack.reshape(-1)
    out, off = [], 0
    for s in shapes:
        n = 1
        for d in s:
            n *= d
        out.append(flat[off:off + n].reshape(s))
        off += n
    return out


def _pad_rows(a, rows=8):
    return jnp.pad(a, ((0, 0), (0, rows - a.shape[1]), (0, 0)))


def _tile_b(b):
    L = b.shape[0]
    return jnp.tile(b.transpose(0, 3, 1, 2).reshape(L, SSM_GROUP, SSM_W), (1, SSM_GROUPS, 1))


def _tile_c(c):
    L = c.shape[0]
    return jnp.tile(c.transpose(0, 3, 1, 2).reshape(L, SSM_STATE, MIX_W), (1, SSM_GROUPS, 1))


def _step(x, c, target, W, M, V):
    T, D = x.shape[1], x.shape[2]
    L = W["w_ada"].shape[0]
    x = x[0]
    target = target[0]
    ax, ay, ac = _coords()
    dev = _dev_index(ax, ay, ac)
    n_ada = W["w_ada"].shape[2]

    cast = cast_weights([W[k] for k in BIG_NAMES])
    gathered = all_gather(list(cast) + [W["g_pre"], W["g_post"], W["conv_w"], c],
                          [2, 2, 1, 1, 1, 1, 1, 1, 0, 0, 0, 0])
    wg_ff_in = gathered[0].reshape(L, 2, 2, 4, D, FF_BLK)
    wg_ff_out = gathered[1].reshape(L, 2, 4, FF_BLK, D)
    wg_in, wg_conv, wg_glu, wg_pool, wg_sb = gathered[2:7]
    wg_out = gathered[7].reshape(L, D, D)
    g_pre = gathered[8].transpose(1, 2, 0, 3).reshape(L, N_SUB, D)
    g_post = gathered[9].transpose(1, 2, 0, 3).reshape(L, N_SUB, D)
    conv_w = _pad_rows(gathered[10].transpose(1, 2, 0, 3).reshape(L, 3, MIX_W))
    c_all = gathered[11].reshape(N_DEV, D)

    b_cols = lax.dynamic_slice_in_dim(W["b_ada"], dev * n_ada, n_ada, axis=1)[:, None, :]
    ada_cols = ada_fwd(c_all, W["w_ada"], b_cols)
    ada_all = all_gather([ada_cols], [0])[0]
    ada = lax.dynamic_index_in_dim(ada_all, dev, axis=2, keepdims=False)
    ada = ada.transpose(1, 0, 2).reshape(L, N_SUB, 3, D)
    zeros = jnp.zeros((L, N_SUB, D), F32)
    pv_all = jnp.stack([g_pre, ada[:, :, 0], ada[:, :, 1], g_post, ada[:, :, 2], zeros, zeros, zeros], axis=2)

    lam = jnp.stack([W["lam_re"].reshape(L, SSM_W), W["lam_im"].reshape(L, SSM_W),
                     jnp.repeat(W["log_dt"], SSM_STATE, axis=1)], axis=1)
    lam = _pad_rows(lam)
    b_t = jnp.stack([_tile_b(W["ssm_b_re"]), _tile_b(W["ssm_b_im"])], axis=1)
    c_t = jnp.stack([_tile_c(W["ssm_c_re"]), _tile_c(W["ssm_c_im"])], axis=1)
    avec, b_bd, c_bd = s5_params(lam, b_t, c_t)
    ssm_d = _pad_rows(W["ssm_d"][:, None, :])
    pool_scale = _pad_rows(W["pool_scale"][:, None, :])
    eye4 = jnp.eye(len(POOL_WINDOWS), dtype=F32)
    w_bd = jnp.einsum("lgcd,gh->lgchd", W["w_pool"], eye4).reshape(L, MIX_W, MIX_W).astype(BF16)

    saved = []
    for l in range(L):
        li = jnp.array([l], jnp.int32)
        x0 = x
        ab0, f0, x1 = ffn_fwd(jnp.array([l, 0], jnp.int32), x0, pv_all[l, 0], wg_ff_in, wg_ff_out)
        p = mix_in_fwd(li, x1, pv_all[l, 1], wg_in)
        za = conv_fwd(li, p, conv_w)
        z = pool_fwd(li, p, w_bd, pool_scale)
        s = s5_scan(li, avec, s5_bu(li, p, b_bd), False)
        yg = s5_out(li, p, s, c_bd, ssm_d)
        o = sb_fwd(p)
        x2, m = merge_fwd(li, p, za, yg, z, o, x1, pv_all[l, 1], wg_conv, wg_glu, wg_pool, wg_sb, wg_out)
        ab1, f1, x = ffn_fwd(jnp.array([l, 1], jnp.int32), x2, pv_all[l, 2], wg_ff_in, wg_ff_out)
        saved.append((x0, ab0, f0, x1, p, za, z, s, yg, o, m, x2, ab1, f1))

    dx, loss_blk = loss_head(x, target)
    loss = lax.psum(loss_blk[0, 0], ("x", "y", "c"))

    recvs = [None] * L
    pgs = [None] * L
    small = {k: [None] * L for k in ("conv_w", "w_bd", "pool_scale", "ssm_d", "gb", "gc", "da")}
    for l in reversed(range(L)):
        li = jnp.array([l], jnp.int32)
        x0, ab0, f0, x1, p, za, z, s, yg, o, m, x2, ab1, f1 = saved[l]
        dab, h, df, dx, pg2 = ffn_bwd_act(jnp.array([l, 1], jnp.int32), dx, x2, f1, pv_all[l, 2], ab1,
                                          wg_ff_in, wg_ff_out)
        g_in1, g_out1 = ffn_bwd_w(h, df, ab1, dab)
        (dza, dyg, dz, do, dgates, pg1m, g_conv, g_glu, g_pool, g_sb, g_wo) = merge_bwd(
            li, p, za, yg, z, o, m, dx, pv_all[l, 1], wg_conv, wg_glu, wg_pool, wg_sb, wg_out)
        d_conv, small["conv_w"][l] = conv_bwd(li, p, dza, conv_w)
        d_pool, small["w_bd"][l], small["pool_scale"][l] = pool_bwd(li, p, dz, w_bd, pool_scale)
        ds, du_skip, small["gc"][l], small["ssm_d"][l] = s5_bwd_y(li, p, s, dyg, c_bd, ssm_d)
        lam_s = s5_scan(li, avec, ds, True)
        d_ssm, small["gb"][l] = s5_bwd_u(li, p, lam_s, du_skip, b_bd)
        small["da"][l] = s5_bwd_a(s, lam_s)
        d_qkv = sb_bwd(p, do)
        dp = dp_assemble(d_conv, d_ssm, d_pool, d_qkv, dgates)
        dx, h, pg1i = mix_in_bwd_act(li, dp, dx, x1, pv_all[l, 1], wg_in)
        g_win = matmul_tn(h, dp, IN_BLK)
        dab, h, df, dx, pg0 = ffn_bwd_act(jnp.array([l, 0], jnp.int32), dx, x0, f0, pv_all[l, 0], ab0,
                                          wg_ff_in, wg_ff_out)
        g_in0, g_out0 = ffn_bwd_w(h, df, ab0, dab)
        pgs[l] = jnp.stack([pg0, pg1m + pg1i, pg2])
        blocks = [g_in0.reshape(N_DEV, D, FF_BLK), g_in1.reshape(N_DEV, D, FF_BLK),
                  g_out0.reshape(N_DEV, D_FF // N_DEV, D), g_out1.reshape(N_DEV, D_FF // N_DEV, D),
                  g_win, g_conv, g_glu, g_pool, g_sb, g_wo.reshape(N_DEV, D // N_DEV, D)]
        recvs[l] = exchange_blocks(blocks, [0] * len(blocks))

    dlam, db_t, dc_t, dldt = s5_params_bwd(lam, b_t, jnp.stack(small["gb"]), jnp.stack(small["gc"]),
                                           jnp.stack(small["da"]))
    pg = jnp.stack(pgs)
    d_ada = jnp.stack([pg[:, :, PV_SHIFT], pg[:, :, PV_SCALE], pg[:, :, PV_GATE]], axis=2).reshape(L, N_SUB * 3 * D)
    db = db_t.reshape(L, 2, SSM_GROUPS, SSM_GROUP, SSM_GROUPS, SSM_STATE)
    db = jnp.einsum("lrghgp->lrgph", db)
    dc = dc_t.reshape(L, 2, SSM_GROUPS, SSM_STATE, SSM_GROUPS, SSM_GROUP)
    dc = jnp.einsum("lrgpgh->lrghp", dc)
    d_wpool = jnp.einsum("lgcgd->lgcd", jnp.stack(small["w_bd"]).reshape(L, 4, 64, 4, 64))
    contrib = {
        "b_ada": d_ada, "g_pre": pg[:, :, PV_GPRE], "g_post": pg[:, :, PV_GPOST],
        "conv_w": jnp.stack(small["conv_w"])[:, :3], "lam_re": dlam[:, 0].reshape(L, SSM_GROUPS, SSM_STATE),
        "lam_im": dlam[:, 1].reshape(L, SSM_GROUPS, SSM_STATE), "log_dt": dldt[:, 2, :SSM_GROUPS],
        "ssm_b_re": db[:, 0], "ssm_b_im": db[:, 1], "ssm_c_re": dc[:, 0], "ssm_c_im": dc[:, 1],
        "ssm_d": jnp.stack(small["ssm_d"])[:, 0], "w_pool": d_wpool,
        "pool_scale": jnp.stack(small["pool_scale"])[:, 0],
    }
    contrib_shapes = [contrib[k].shape for k in SMALL_NAMES]
    pack_all = all_gather([_pack([contrib[k] for k in SMALL_NAMES])], [0])[0]
    total = dict(zip(SMALL_NAMES, _unpack(small_sum(pack_all), contrib_shapes)))
    d_ada_all = pack_all.reshape(N_DEV, -1)[:, :L * N_SUB * 3 * D].reshape(N_DEV, L, N_SUB * 3 * D)
    dada_cols = lax.dynamic_slice_in_dim(d_ada_all, dev * n_ada, n_ada, axis=2).transpose(1, 0, 2)
    n_g = D // N_DEV
    grads = {}
    for k in SMALL_NAMES:
        g = total[k]
        if k in ("g_pre", "g_post"):
            g = lax.dynamic_slice_in_dim(g, dev * n_g, n_g, axis=2)
        elif k == "conv_w":
            g = lax.dynamic_slice_in_dim(g, dev * (MIX_W // N_DEV), MIX_W // N_DEV, axis=2)
        grads[k] = g

    delta, new_m, new_v = {}, {}, {}
    shapes = [W[k].shape for k in SMALL_NAMES]
    dl, nm, nv = small_update(_pack([W[k] for k in SMALL_NAMES]), _pack([grads[k] for k in SMALL_NAMES]),
                              _pack([M[k] for k in SMALL_NAMES]), _pack([V[k] for k in SMALL_NAMES]))
    for k, a, b, cc in zip(SMALL_NAMES, _unpack(dl, shapes), _unpack(nm, shapes), _unpack(nv, shapes)):
        delta[k], new_m[k], new_v[k] = a, b, cc
    grads["w_ada"], delta["w_ada"], new_m["w_ada"], new_v["w_ada"] = ada_update(
        c_all, dada_cols, W["w_ada"], M["w_ada"], V["w_ada"])

    def big(name, slots):
        shp = W[name].shape
        flat = (-1,) + shp[-2:]
        outs = sum_update(slots, W[name].reshape(flat), M[name].reshape(flat), V[name].reshape(flat))
        grads[name], delta[name], new_m[name], new_v[name] = [o.reshape(shp) for o in outs]

    big("w_ff_in", [recvs[l][k] for l in range(L) for k in (0, 1)])
    big("w_ff_out", [recvs[l][2 + k] for l in range(L) for k in (0, 1)])
    for i, name in enumerate(("w_in", "w_conv_out", "w_glu", "w_pool_out", "w_sb_out", "w_out")):
        big(name, [recvs[l][4 + i] for l in range(L)])

    return (loss, dx[None], *[grads[k] for k in WEIGHT_NAMES], *[delta[k] for k in WEIGHT_NAMES],
            *[new_m[k] for k in WEIGHT_NAMES], *[new_v[k] for k in WEIGHT_NAMES])


def kernel(x, c, w_ada, b_ada, g_pre, g_post, w_ff_in, w_ff_out, w_in, conv_w, w_conv_out, lam_re, lam_im, log_dt, ssm_b_re, ssm_b_im, ssm_c_re, ssm_c_im, ssm_d, w_glu, w_pool, pool_scale, w_pool_out, w_sb_out, w_out, loss_target, m_w_ada, m_b_ada, m_g_pre, m_g_post, m_w_ff_in, m_w_ff_out, m_w_in, m_conv_w, m_w_conv_out, m_lam_re, m_lam_im, m_log_dt, m_ssm_b_re, m_ssm_b_im, m_ssm_c_re, m_ssm_c_im, m_ssm_d, m_w_glu, m_w_pool, m_pool_scale, m_w_pool_out, m_w_sb_out, m_w_out, v_w_ada, v_b_ada, v_g_pre, v_g_post, v_w_ff_in, v_w_ff_out, v_w_in, v_conv_w, v_w_conv_out, v_lam_re, v_lam_im, v_log_dt, v_ssm_b_re, v_ssm_b_im, v_ssm_c_re, v_ssm_c_im, v_ssm_d, v_w_glu, v_w_pool, v_pool_scale, v_w_pool_out, v_w_sb_out, v_w_out):
    w = (w_ada, b_ada, g_pre, g_post, w_ff_in, w_ff_out, w_in, conv_w, w_conv_out, lam_re, lam_im, log_dt, ssm_b_re, ssm_b_im, ssm_c_re, ssm_c_im, ssm_d, w_glu, w_pool, pool_scale, w_pool_out, w_sb_out, w_out)
    m = (m_w_ada, m_b_ada, m_g_pre, m_g_post, m_w_ff_in, m_w_ff_out, m_w_in, m_conv_w, m_w_conv_out, m_lam_re, m_lam_im, m_log_dt, m_ssm_b_re, m_ssm_b_im, m_ssm_c_re, m_ssm_c_im, m_ssm_d, m_w_glu, m_w_pool, m_pool_scale, m_w_pool_out, m_w_sb_out, m_w_out)
    v = (v_w_ada, v_b_ada, v_g_pre, v_g_post, v_w_ff_in, v_w_ff_out, v_w_in, v_conv_w, v_w_conv_out, v_lam_re, v_lam_im, v_log_dt, v_ssm_b_re, v_ssm_b_im, v_ssm_c_re, v_ssm_c_im, v_ssm_d, v_w_glu, v_w_pool, v_pool_scale, v_w_pool_out, v_w_sb_out, v_w_out)
    return _step(x, c, loss_target, dict(zip(WEIGHT_NAMES, w)), dict(zip(WEIGHT_NAMES, m)), dict(zip(WEIGHT_NAMES, v)))
```

```python
import functools

import jax
import jax.numpy as jnp
from jax import lax
from jax.experimental import pallas as pl
from jax.experimental.pallas import tpu as pltpu

F32 = jnp.float32
BF16 = jnp.bfloat16

N_DEV = 8
D_MODEL = 1024
D_FF = 2816
FF_BLK = D_FF // 4
N_SUB = 3
MIX_W = 256
IN_COLS = 6144
IN_BLK = IN_COLS // N_DEV
GATE_OFF = 2048
SSM_GROUPS, SSM_GROUP, SSM_STATE = 16, 16, 64
SSM_W = SSM_GROUPS * SSM_STATE
POOL_WINDOWS = (2, 4, 8, 16)
SB_HEAD = 64
EPS = 1e-6
DT_LAMBDA_RE_MAX = -1e-4
ADAM_LR, ADAM_B1, ADAM_B2, ADAM_EPS, ADAM_WD, ADAM_STEP = 0.001, 0.9, 0.999, 1e-08, 0.01, 10

VMEM_LIMIT = 56 * 1024 * 1024

PV_GPRE, PV_SHIFT, PV_SCALE, PV_GPOST, PV_GATE = 0, 1, 2, 3, 4


def _cparams(sem):
    return pltpu.CompilerParams(dimension_semantics=sem, vmem_limit_bytes=VMEM_LIMIT)


def _dot(a, b):
    return jnp.dot(a, b, preferred_element_type=F32)


def _dot_nt(a, b):
    return lax.dot_general(a, b, (((1,), (1,)), ((), ())), preferred_element_type=F32)


def _dot_tn(a, b):
    return lax.dot_general(a, b, (((0,), (0,)), ((), ())), preferred_element_type=F32)


def _rms(x):
    r = lax.rsqrt(jnp.mean(x * x, axis=-1, keepdims=True) + EPS)
    return x * r, r


def _rms_bwd(dn, n, r):
    return r * (dn - n * jnp.mean(dn * n, axis=-1, keepdims=True))


def _sigmoid(x):
    return 1.0 / (1.0 + jnp.exp(-x))


def _colsum(x):
    return jnp.sum(x, axis=0, keepdims=True)


def _prenorm(x, pv_ref):
    n, r = _rms(x)
    hn = n * pv_ref[PV_GPRE:PV_GPRE + 1, :]
    h = hn * (1.0 + pv_ref[PV_SCALE:PV_SCALE + 1, :]) + pv_ref[PV_SHIFT:PV_SHIFT + 1, :]
    return h, n, r, hn


def _prenorm_bwd(dh, dxn, x, pv_ref, pg_ref):
    _, n, r, hn = _prenorm(x, pv_ref)
    pg_ref[PV_SHIFT:PV_SHIFT + 1, :] += _colsum(dh)
    pg_ref[PV_SCALE:PV_SCALE + 1, :] += _colsum(dh * hn)
    dhn = dh * (1.0 + pv_ref[PV_SCALE:PV_SCALE + 1, :])
    pg_ref[PV_GPRE:PV_GPRE + 1, :] += _colsum(dhn * n)
    dn = dhn * pv_ref[PV_GPRE:PV_GPRE + 1, :]
    return dxn + _rms_bwd(dn, n, r)


def _postnorm_res(x, f, pv_ref, coef):
    nf, _ = _rms(f)
    return x + (coef * (1.0 + pv_ref[PV_GATE:PV_GATE + 1, :])) * (nf * pv_ref[PV_GPOST:PV_GPOST + 1, :])


def _postnorm_bwd(dxn, f, pv_ref, pg_ref, coef):
    nf, rf = _rms(f)
    g_post = pv_ref[PV_GPOST:PV_GPOST + 1, :]
    pg_ref[PV_GATE:PV_GATE + 1, :] += _colsum(dxn * (nf * g_post)) * coef
    dnfg = dxn * (coef * (1.0 + pv_ref[PV_GATE:PV_GATE + 1, :]))
    pg_ref[PV_GPOST:PV_GPOST + 1, :] += _colsum(dnfg * nf)
    return _rms_bwd(dnfg * g_post, nf, rf)


def ffn_fwd(lk, x, pv, wg_in, wg_out, tm=512):
    T, D = x.shape
    nj = 4

    def body(lk_ref, x_ref, pv_ref, win_ref, wout_ref, ab_ref, f_ref, xn_ref, h_sc, acc):
        j = pl.program_id(1)

        @pl.when(j == 0)
        def _():
            h, _, _, _ = _prenorm(x_ref[...], pv_ref)
            h_sc[...] = h.astype(BF16)
            acc[...] = jnp.zeros_like(acc)

        h = h_sc[...]
        a = _dot(h, win_ref[0])
        b = _dot(h, win_ref[1])
        ab_ref[0] = a.astype(BF16)
        ab_ref[1] = b.astype(BF16)
        act = (a * _sigmoid(a) * b).astype(BF16)
        acc[...] += _dot(act, wout_ref[...])

        @pl.when(j == nj - 1)
        def _():
            f = acc[...]
            f_ref[...] = f
            xn_ref[...] = _postnorm_res(x_ref[...], f, pv_ref, 0.5)

    grid_spec = pltpu.PrefetchScalarGridSpec(
        num_scalar_prefetch=1, grid=(T // tm, nj),
        in_specs=[
            pl.BlockSpec((tm, D), lambda i, j, lk: (i, 0)),
            pl.BlockSpec((8, D), lambda i, j, lk: (0, 0)),
            pl.BlockSpec((None, None, 2, None, D, FF_BLK), lambda i, j, lk: (lk[0], lk[1], 0, j, 0, 0)),
            pl.BlockSpec((None, None, None, FF_BLK, D), lambda i, j, lk: (lk[0], lk[1], j, 0, 0)),
        ],
        out_specs=[
            pl.BlockSpec((2, None, tm, FF_BLK), lambda i, j, lk: (0, j, i, 0)),
            pl.BlockSpec((tm, D), lambda i, j, lk: (i, 0)),
            pl.BlockSpec((tm, D), lambda i, j, lk: (i, 0)),
        ],
        scratch_shapes=[pltpu.VMEM((tm, D), BF16), pltpu.VMEM((tm, D), F32)],
    )
    return pl.pallas_call(
        body, name="ffn_fwd", grid_spec=grid_spec,
        out_shape=[jax.ShapeDtypeStruct((2, nj, T, FF_BLK), BF16),
                   jax.ShapeDtypeStruct((T, D), F32), jax.ShapeDtypeStruct((T, D), F32)],
        compiler_params=_cparams(("arbitrary", "arbitrary")),
    )(lk, x, pv, wg_in, wg_out)


def ffn_bwd_act(lk, dxn, x, f, pv, ab, wg_in, wg_out, tm=512):
    T, D = x.shape
    nj = 4

    def body(lk_ref, dxn_ref, x_ref, f_ref, pv_ref, ab_ref, win_ref, wout_ref,
             dab_ref, h_ref, df_ref, dx_ref, pg_ref, dacc):
        i, j = pl.program_id(0), pl.program_id(1)

        @pl.when((i == 0) & (j == 0))
        def _():
            pg_ref[...] = jnp.zeros_like(pg_ref)

        @pl.when(j == 0)
        def _():
            df = _postnorm_bwd(dxn_ref[...], f_ref[...], pv_ref, pg_ref, 0.5)
            df_ref[...] = df.astype(BF16)
            h, _, _, _ = _prenorm(x_ref[...], pv_ref)
            h_ref[...] = h.astype(BF16)
            dacc[...] = jnp.zeros_like(dacc)

        dact = _dot_nt(df_ref[...], wout_ref[...])
        a = ab_ref[0].astype(F32)
        b = ab_ref[1].astype(F32)
        sig = _sigmoid(a)
        s = a * sig
        da = (dact * b * (sig * (1.0 + a * (1.0 - sig)))).astype(BF16)
        db = (dact * s).astype(BF16)
        dab_ref[0] = da
        dab_ref[1] = db
        dacc[...] += _dot_nt(da, win_ref[0]) + _dot_nt(db, win_ref[1])

        @pl.when(j == nj - 1)
        def _():
            dx_ref[...] = _prenorm_bwd(dacc[...], dxn_ref[...], x_ref[...], pv_ref, pg_ref)

    tile = pl.BlockSpec((tm, D), lambda i, j, lk: (i, 0))
    grid_spec = pltpu.PrefetchScalarGridSpec(
        num_scalar_prefetch=1, grid=(T // tm, nj),
        in_specs=[
            tile, tile, tile,
            pl.BlockSpec((8, D), lambda i, j, lk: (0, 0)),
            pl.BlockSpec((2, None, tm, FF_BLK), lambda i, j, lk: (0, j, i, 0)),
            pl.BlockSpec((None, None, 2, None, D, FF_BLK), lambda i, j, lk: (lk[0], lk[1], 0, j, 0, 0)),
            pl.BlockSpec((None, None, None, FF_BLK, D), lambda i, j, lk: (lk[0], lk[1], j, 0, 0)),
        ],
        out_specs=[
            pl.BlockSpec((2, None, tm, FF_BLK), lambda i, j, lk: (0, j, i, 0)),
            tile, tile, tile,
            pl.BlockSpec((8, D), lambda i, j, lk: (0, 0)),
        ],
        scratch_shapes=[pltpu.VMEM((tm, D), F32)],
    )
    return pl.pallas_call(
        body, name="ffn_bwd_act", grid_spec=grid_spec,
        out_shape=[jax.ShapeDtypeStruct((2, nj, T, FF_BLK), BF16),
                   jax.ShapeDtypeStruct((T, D), BF16), jax.ShapeDtypeStruct((T, D), BF16),
                   jax.ShapeDtypeStruct((T, D), F32), jax.ShapeDtypeStruct((8, D), F32)],
        compiler_params=_cparams(("arbitrary", "arbitrary")),
    )(lk, dxn, x, f, pv, ab, wg_in, wg_out)


def ffn_bwd_w(h, df, ab, dab, tm=512):
    T, D = h.shape
    nj, ni = 4, T // tm

    def body(h_ref, df_ref, ab_ref, dab_ref, gin_ref, gout_ref, acc_in, acc_out):
        i = pl.program_id(1)

        @pl.when(i == 0)
        def _():
            acc_in[...] = jnp.zeros_like(acc_in)
            acc_out[...] = jnp.zeros_like(acc_out)

        h = h_ref[...]
        acc_in[0] += _dot_tn(h, dab_ref[0])
        acc_in[1] += _dot_tn(h, dab_ref[1])
        a = ab_ref[0].astype(F32)
        b = ab_ref[1].astype(F32)
        act = (a * _sigmoid(a) * b).astype(BF16)
        acc_out[...] += _dot_tn(act, df_ref[...])

        @pl.when(i == ni - 1)
        def _():
            gin_ref[...] = acc_in[...].astype(BF16)
            gout_ref[...] = acc_out[...].astype(BF16)

    tile = pl.BlockSpec((tm, D), lambda j, i: (i, 0))
    blk = pl.BlockSpec((2, None, tm, FF_BLK), lambda j, i: (0, j, i, 0))
    return pl.pallas_call(
        body, name="ffn_bwd_w", grid=(nj, ni),
        in_specs=[tile, tile, blk, blk],
        out_specs=[pl.BlockSpec((2, None, D, FF_BLK), lambda j, i: (0, j, 0, 0)),
                   pl.BlockSpec((None, FF_BLK, D), lambda j, i: (j, 0, 0))],
        out_shape=[jax.ShapeDtypeStruct((2, nj, D, FF_BLK), BF16),
                   jax.ShapeDtypeStruct((nj, FF_BLK, D), BF16)],
        scratch_shapes=[pltpu.VMEM((2, D, FF_BLK), F32), pltpu.VMEM((FF_BLK, D), F32)],
        compiler_params=_cparams(("arbitrary", "arbitrary")),
    )(h, df, ab, dab)


def mix_in_fwd(l, x, pv, wg, tm=512):
    T, D = x.shape

    def body(l_ref, x_ref, pv_ref, w_ref, p_ref, h_sc):
        @pl.when(pl.program_id(1) == 0)
        def _():
            h, _, _, _ = _prenorm(x_ref[...], pv_ref)
            h_sc[...] = h.astype(BF16)

        p_ref[...] = _dot(h_sc[...], w_ref[...])

    grid_spec = pltpu.PrefetchScalarGridSpec(
        num_scalar_prefetch=1, grid=(T // tm, N_DEV),
        in_specs=[pl.BlockSpec((tm, D), lambda i, j, l: (i, 0)),
                  pl.BlockSpec((8, D), lambda i, j, l: (0, 0)),
                  pl.BlockSpec((None, None, D, IN_BLK), lambda i, j, l: (l[0], j, 0, 0))],
        out_specs=pl.BlockSpec((tm, IN_BLK), lambda i, j, l: (i, j)),
        scratch_shapes=[pltpu.VMEM((tm, D), BF16)],
    )
    return pl.pallas_call(
        body, name="mix_in_fwd", grid_spec=grid_spec,
        out_shape=jax.ShapeDtypeStruct((T, IN_COLS), F32),
        compiler_params=_cparams(("arbitrary", "arbitrary")),
    )(l, x, pv, wg)


def mix_in_bwd_act(l, dp, dxn, x, pv, wg, tm=512):
    T, D = x.shape

    def body(l_ref, dp_ref, dxn_ref, x_ref, pv_ref, w_ref, dx_ref, h_ref, pg_ref, dacc):
        i, j = pl.program_id(0), pl.program_id(1)

        @pl.when((i == 0) & (j == 0))
        def _():
            pg_ref[...] = jnp.zeros_like(pg_ref)

        @pl.when(j == 0)
        def _():
            dacc[...] = jnp.zeros_like(dacc)

        dacc[...] += _dot_nt(dp_ref[...], w_ref[...])

        @pl.when(j == N_DEV - 1)
        def _():
            h, _, _, _ = _prenorm(x_ref[...], pv_ref)
            h_ref[...] = h.astype(BF16)
            dx_ref[...] = _prenorm_bwd(dacc[...], dxn_ref[...], x_ref[...], pv_ref, pg_ref)

    tile = pl.BlockSpec((tm, D), lambda i, j, l: (i, 0))
    grid_spec = pltpu.PrefetchScalarGridSpec(
        num_scalar_prefetch=1, grid=(T // tm, N_DEV),
        in_specs=[pl.BlockSpec((tm, IN_BLK), lambda i, j, l: (i, j)), tile, tile,
                  pl.BlockSpec((8, D), lambda i, j, l: (0, 0)),
                  pl.BlockSpec((None, None, D, IN_BLK), lambda i, j, l: (l[0], j, 0, 0))],
        out_specs=[tile, tile, pl.BlockSpec((8, D), lambda i, j, l: (0, 0))],
        scratch_shapes=[pltpu.VMEM((tm, D), F32)],
    )
    return pl.pallas_call(
        body, name="mix_in_bwd_act", grid_spec=grid_spec,
        out_shape=[jax.ShapeDtypeStruct((T, D), F32), jax.ShapeDtypeStruct((T, D), BF16),
                   jax.ShapeDtypeStruct((8, D), F32)],
        compiler_params=_cparams(("arbitrary", "arbitrary")),
    )(l, dp, dxn, x, pv, wg)


def matmul_tn(a, b, tn, tm=512):
    T, M = a.shape
    N = b.shape[1]
    ni = T // tm

    def body(a_ref, b_ref, o_ref, acc):
        i = pl.program_id(1)

        @pl.when(i == 0)
        def _():
            acc[...] = jnp.zeros_like(acc)

        acc[...] += _dot_tn(a_ref[...], b_ref[...])

        @pl.when(i == ni - 1)
        def _():
            o_ref[...] = acc[...].astype(o_ref.dtype)

    return pl.pallas_call(
        body, name="matmul_tn", grid=(N // tn, ni),
        in_specs=[pl.BlockSpec((tm, M), lambda j, i: (i, 0)), pl.BlockSpec((tm, tn), lambda j, i: (i, j))],
        out_specs=pl.BlockSpec((None, M, tn), lambda j, i: (j, 0, 0)),
        out_shape=jax.ShapeDtypeStruct((N // tn, M, tn), BF16),
        scratch_shapes=[pltpu.VMEM((M, tn), F32)],
        compiler_params=_cparams(("arbitrary", "arbitrary")),
    )(a, b)


SEQ_CHUNK = 256
HALO = 16


def _shift_down(ext, d):
    return pltpu.roll(ext, d, 0)


def _shift_up(ext, d):
    return pltpu.roll(ext, ext.shape[0] - d, 0)


def _rows_with_lead(load, c, width):
    t0 = c * SEQ_CHUNK
    if c == 0:
        return jnp.concatenate([jnp.zeros((HALO, width), F32), load(0, SEQ_CHUNK)], axis=0)
    return load(t0 - HALO, SEQ_CHUNK + HALO)


def _rows_with_tail(load, c, n_chunks, width):
    t0 = c * SEQ_CHUNK
    if c == n_chunks - 1:
        return jnp.concatenate([load(t0, SEQ_CHUNK), jnp.zeros((HALO, width), F32)], axis=0)
    return load(t0, SEQ_CHUNK + HALO)


def conv_fwd(l, p, conv_w):
    T = p.shape[0]
    W = MIX_W
    nC = T // SEQ_CHUNK

    def body(l_ref, p_ref, w_ref, za_ref):
        w0, w1, w2 = w_ref[0:1, :], w_ref[1:2, :], w_ref[2:3, :]
        for c in range(nC):
            ext = _rows_with_lead(lambda s, n: p_ref[s:s + n, W:2 * W] * p_ref[s:s + n, 2 * W:3 * W], c, W)
            y = w2 * ext + w1 * _shift_down(ext, 1) + w0 * _shift_down(ext, 2)
            t0 = c * SEQ_CHUNK
            za_ref[t0:t0 + SEQ_CHUNK, :] = (p_ref[t0:t0 + SEQ_CHUNK, 0:W] * y[HALO:]).astype(BF16)

    grid_spec = pltpu.PrefetchScalarGridSpec(
        num_scalar_prefetch=1, grid=(1,),
        in_specs=[pl.BlockSpec((T, 3 * W), lambda i, l: (0, 0)),
                  pl.BlockSpec((None, 8, W), lambda i, l: (l[0], 0, 0))],
        out_specs=pl.BlockSpec((T, W), lambda i, l: (0, 0)),
    )
    return pl.pallas_call(
        body, name="conv_fwd", grid_spec=grid_spec,
        out_shape=jax.ShapeDtypeStruct((T, W), BF16),
        compiler_params=_cparams(("arbitrary",)),
    )(l, p, conv_w)


def conv_bwd(l, p, dza, conv_w):
    T = p.shape[0]
    W = MIX_W
    nC = T // SEQ_CHUNK

    def body(l_ref, p_ref, dza_ref, w_ref, dp_ref, dw_ref):
        w0, w1, w2 = w_ref[0:1, :], w_ref[1:2, :], w_ref[2:3, :]
        dw = [jnp.zeros((1, W), F32) for _ in range(3)]
        for c in range(nC):
            t0 = c * SEQ_CHUNK
            ext = _rows_with_lead(lambda s, n: p_ref[s:s + n, W:2 * W] * p_ref[s:s + n, 2 * W:3 * W], c, W)
            u1, u2 = _shift_down(ext, 1)[HALO:], _shift_down(ext, 2)[HALO:]
            u0 = ext[HALO:]
            y = w2 * u0 + w1 * u1 + w0 * u2
            dza_c = dza_ref[t0:t0 + SEQ_CHUNK, :]
            dy = dza_c * p_ref[t0:t0 + SEQ_CHUNK, 0:W]
            dw[0] += _colsum(dy * u2)
            dw[1] += _colsum(dy * u1)
            dw[2] += _colsum(dy * u0)
            dye = _rows_with_tail(lambda s, n: dza_ref[s:s + n, :] * p_ref[s:s + n, 0:W], c, nC, W)
            du = (w2 * dye + w1 * _shift_up(dye, 1) + w0 * _shift_up(dye, 2))[:SEQ_CHUNK]
            dp_ref[t0:t0 + SEQ_CHUNK, 0:W] = (dza_c * y).astype(BF16)
            dp_ref[t0:t0 + SEQ_CHUNK, W:2 * W] = (du * p_ref[t0:t0 + SEQ_CHUNK, 2 * W:3 * W]).astype(BF16)
            dp_ref[t0:t0 + SEQ_CHUNK, 2 * W:3 * W] = (du * p_ref[t0:t0 + SEQ_CHUNK, W:2 * W]).astype(BF16)
        dw_ref[...] = jnp.concatenate(dw + [jnp.zeros((5, W), F32)], axis=0)

    grid_spec = pltpu.PrefetchScalarGridSpec(
        num_scalar_prefetch=1, grid=(1,),
        in_specs=[pl.BlockSpec((T, 3 * W), lambda i, l: (0, 0)),
                  pl.BlockSpec((T, W), lambda i, l: (0, 0)),
                  pl.BlockSpec((None, 8, W), lambda i, l: (l[0], 0, 0))],
        out_specs=[pl.BlockSpec((T, 3 * W), lambda i, l: (0, 0)), pl.BlockSpec((8, W), lambda i, l: (0, 0))],
    )
    return pl.pallas_call(
        body, name="conv_bwd", grid_spec=grid_spec,
        out_shape=[jax.ShapeDtypeStruct((T, 3 * W), BF16), jax.ShapeDtypeStruct((8, W), F32)],
        compiler_params=_cparams(("arbitrary",)),
    )(l, p, dza, conv_w)


def _pool_consts(rows, t0):
    lane = lax.broadcasted_iota(jnp.int32, (rows, MIX_W), 1)
    t = lax.broadcasted_iota(jnp.int32, (rows, MIX_W), 0) + t0
    win = jnp.where(lane < 64, 2, jnp.where(lane < 128, 4, jnp.where(lane < 192, 8, 16)))
    inv = 1.0 / jnp.minimum(t + 1, win).astype(F32)
    return lane, inv


def _pick_window(lane, s2, s4, s8, s16):
    return jnp.where(lane < 64, s2, jnp.where(lane < 128, s4, jnp.where(lane < 192, s8, s16)))


def _pooled_chunk(u_ref, c):
    ext = _rows_with_lead(lambda s, n: u_ref[s:s + n, :], c, MIX_W)
    s2 = ext + _shift_down(ext, 1)
    s4 = s2 + _shift_down(s2, 2)
    s8 = s4 + _shift_down(s4, 4)
    s16 = s8 + _shift_down(s8, 8)
    lane, inv = _pool_consts(SEQ_CHUNK, c * SEQ_CHUNK)
    return _pick_window(lane, s2[HALO:], s4[HALO:], s8[HALO:], s16[HALO:]) * inv - ext[HALO:]


def pool_fwd(l, p, w_bd, scale):
    T = p.shape[0]
    W = MIX_W
    nC = T // SEQ_CHUNK

    def body(l_ref, u_ref, w_ref, sc_ref, z_ref):
        for c in range(nC):
            pooled = _pooled_chunk(u_ref, c)
            mixed = _dot(pooled.astype(BF16), w_ref[...])
            z_ref[c * SEQ_CHUNK:(c + 1) * SEQ_CHUNK, :] = (mixed * sc_ref[0:1, :]).astype(BF16)

    grid_spec = pltpu.PrefetchScalarGridSpec(
        num_scalar_prefetch=1, grid=(1,),
        in_specs=[pl.BlockSpec((T, W), lambda i, l: (0, 4)),
                  pl.BlockSpec((None, W, W), lambda i, l: (l[0], 0, 0)),
                  pl.BlockSpec((None, 8, W), lambda i, l: (l[0], 0, 0))],
        out_specs=pl.BlockSpec((T, W), lambda i, l: (0, 0)),
    )
    return pl.pallas_call(
        body, name="pool_fwd", grid_spec=grid_spec,
        out_shape=jax.ShapeDtypeStruct((T, W), BF16),
        compiler_params=_cparams(("arbitrary",)),
    )(l, p, w_bd, scale)


def pool_bwd(l, p, dz, w_bd, scale):
    T = p.shape[0]
    W = MIX_W
    nC = T // SEQ_CHUNK

    def body(l_ref, u_ref, dz_ref, w_ref, sc_ref, du_ref, dw_ref, dsc_ref, e_sc, dpl_sc):
        dw = jnp.zeros((W, W), F32)
        dsc = jnp.zeros((1, W), F32)
        for c in range(nC):
            t0 = c * SEQ_CHUNK
            pooled = _pooled_chunk(u_ref, c).astype(BF16)
            mixed = _dot(pooled, w_ref[...])
            dz_c = dz_ref[t0:t0 + SEQ_CHUNK, :]
            dsc += _colsum(dz_c * mixed)
            dmixed = (dz_c * sc_ref[0:1, :]).astype(BF16)
            dw += _dot_tn(pooled, dmixed)
            dpooled = _dot_nt(dmixed, w_ref[...])
            _, inv = _pool_consts(SEQ_CHUNK, t0)
            dpl_sc[t0:t0 + SEQ_CHUNK, :] = dpooled
            e_sc[t0:t0 + SEQ_CHUNK, :] = dpooled * inv
        for c in range(nC):
            t0 = c * SEQ_CHUNK
            ext = _rows_with_tail(lambda s, n: e_sc[s:s + n, :], c, nC, W)
            s2 = ext + _shift_up(ext, 1)
            s4 = s2 + _shift_up(s2, 2)
            s8 = s4 + _shift_up(s4, 4)
            s16 = s8 + _shift_up(s8, 8)
            lane, _ = _pool_consts(SEQ_CHUNK, t0)
            n = SEQ_CHUNK
            du = _pick_window(lane, s2[:n], s4[:n], s8[:n], s16[:n]) - dpl_sc[t0:t0 + SEQ_CHUNK, :]
            du_ref[t0:t0 + SEQ_CHUNK, :] = du.astype(BF16)
        dw_ref[...] = dw
        dsc_ref[...] = jnp.concatenate([dsc, jnp.zeros((7, W), F32)], axis=0)

    grid_spec = pltpu.PrefetchScalarGridSpec(
        num_scalar_prefetch=1, grid=(1,),
        in_specs=[pl.BlockSpec((T, W), lambda i, l: (0, 4)),
                  pl.BlockSpec((T, W), lambda i, l: (0, 0)),
                  pl.BlockSpec((None, W, W), lambda i, l: (l[0], 0, 0)),
                  pl.BlockSpec((None, 8, W), lambda i, l: (l[0], 0, 0))],
        out_specs=[pl.BlockSpec((T, W), lambda i, l: (0, 0)), pl.BlockSpec((W, W), lambda i, l: (0, 0)),
                   pl.BlockSpec((8, W), lambda i, l: (0, 0))],
        scratch_shapes=[pltpu.VMEM((T, W), F32), pltpu.VMEM((T, W), F32)],
    )
    return pl.pallas_call(
        body, name="pool_bwd", grid_spec=grid_spec,
        out_shape=[jax.ShapeDtypeStruct((T, W), BF16), jax.ShapeDtypeStruct((W, W), F32),
                   jax.ShapeDtypeStruct((8, W), F32)],
        compiler_params=_cparams(("arbitrary",)),
    )(l, p, dz, w_bd, scale)


def _s5_disc(lre, lim, ldt):
    lr = jnp.minimum(lre, DT_LAMBDA_RE_MAX)
    dt = jnp.exp(ldt)
    mag = jnp.exp(lr * dt)
    a_re = mag * jnp.cos(lim * dt)
    a_im = mag * jnp.sin(lim * dt)
    den = lr * lr + lim * lim
    nr = a_re - 1.0
    return a_re, a_im, (nr * lr + a_im * lim) / den, (a_im * lr - nr * lim) / den


def _bd_mask(shape, row_blk, col_blk):
    r = lax.broadcasted_iota(jnp.int32, shape, 0) >> (row_blk.bit_length() - 1)
    c = lax.broadcasted_iota(jnp.int32, shape, 1) >> (col_blk.bit_length() - 1)
    return r == c


def s5_params(lam, b_t, c_t):
    L = lam.shape[0]

    def body(lam_ref, b_ref, c_ref, a_ref, bbd_ref, cbd_ref):
        a_re, a_im, f_re, f_im = _s5_disc(lam_ref[0:1, :], lam_ref[1:2, :], lam_ref[2:3, :])
        a_ref[...] = jnp.concatenate([a_re, a_im, jnp.zeros((6, SSM_W), F32)], axis=0)
        mb = _bd_mask((MIX_W, SSM_W), SSM_GROUP, SSM_STATE)
        bbd_ref[0] = jnp.where(mb, f_re * b_ref[0] - f_im * b_ref[1], 0.0).astype(BF16)
        bbd_ref[1] = jnp.where(mb, f_re * b_ref[1] + f_im * b_ref[0], 0.0).astype(BF16)
        mc = _bd_mask((SSM_W, MIX_W), SSM_STATE, SSM_GROUP)
        cbd_ref[0] = jnp.where(mc, c_ref[0], 0.0).astype(BF16)
        cbd_ref[1] = jnp.where(mc, c_ref[1], 0.0).astype(BF16)

    return pl.pallas_call(
        body, name="s5_params", grid=(L,),
        in_specs=[pl.BlockSpec((None, 8, SSM_W), lambda l: (l, 0, 0)),
                  pl.BlockSpec((None, 2, MIX_W, SSM_W), lambda l: (l, 0, 0, 0)),
                  pl.BlockSpec((None, 2, SSM_W, MIX_W), lambda l: (l, 0, 0, 0))],
        out_specs=[pl.BlockSpec((None, 8, SSM_W), lambda l: (l, 0, 0)),
                   pl.BlockSpec((None, 2, MIX_W, SSM_W), lambda l: (l, 0, 0, 0)),
                   pl.BlockSpec((None, 2, SSM_W, MIX_W), lambda l: (l, 0, 0, 0))],
        out_shape=[jax.ShapeDtypeStruct((L, 8, SSM_W), F32),
                   jax.ShapeDtypeStruct((L, 2, MIX_W, SSM_W), BF16),
                   jax.ShapeDtypeStruct((L, 2, SSM_W, MIX_W), BF16)],
        compiler_params=_cparams(("arbitrary",)),
    )(lam, b_t, c_t)


def s5_params_bwd(lam, b_t, gb, gc, da):
    L = lam.shape[0]

    def body(lam_ref, b_ref, gb_ref, gc_ref, da_ref, dlam_ref, db_ref, dc_ref, dgrp_ref):
        lre, lim, ldt = lam_ref[0:1, :], lam_ref[1:2, :], lam_ref[2:3, :]
        (a_re, a_im, f_re, f_im), vjp = jax.vjp(_s5_disc, lre, lim, ldt)
        mb = _bd_mask((MIX_W, SSM_W), SSM_GROUP, SSM_STATE)
        gbr = jnp.where(mb, gb_ref[0], 0.0)
        gbi = jnp.where(mb, gb_ref[1], 0.0)
        df_re = _colsum(gbr * b_ref[0] + gbi * b_ref[1])
        df_im = _colsum(gbi * b_ref[0] - gbr * b_ref[1])
        db_ref[0] = f_re * gbr + f_im * gbi
        db_ref[1] = f_re * gbi - f_im * gbr
        mc = _bd_mask((SSM_W, MIX_W), SSM_STATE, SSM_GROUP)
        dc_ref[0] = jnp.where(mc, gc_ref[0], 0.0)
        dc_ref[1] = jnp.where(mc, gc_ref[1], 0.0)
        dlre, dlim, dldt = vjp((da_ref[0:1, :], da_ref[1:2, :], df_re, df_im))
        dl = jnp.concatenate([dlre, dlim, dldt, jnp.zeros((5, SSM_W), F32)], axis=0)
        dlam_ref[...] = dl
        grp = jnp.where(_bd_mask((SSM_W, 128), SSM_STATE, 1), 1.0, 0.0)
        dgrp_ref[...] = jnp.dot(dl, grp, preferred_element_type=F32, precision=lax.Precision.HIGHEST)

    vec = pl.BlockSpec((None, 8, SSM_W), lambda l: (l, 0, 0))
    bsp = pl.BlockSpec((None, 2, MIX_W, SSM_W), lambda l: (l, 0, 0, 0))
    csp = pl.BlockSpec((None, 2, SSM_W, MIX_W), lambda l: (l, 0, 0, 0))
    return pl.pallas_call(
        body, name="s5_params_bwd", grid=(L,),
        in_specs=[vec, bsp, bsp, csp, vec],
        out_specs=[vec, bsp, csp, pl.BlockSpec((None, 8, 128), lambda l: (l, 0, 0))],
        out_shape=[jax.ShapeDtypeStruct((L, 8, SSM_W), F32),
                   jax.ShapeDtypeStruct((L, 2, MIX_W, SSM_W), F32),
                   jax.ShapeDtypeStruct((L, 2, SSM_W, MIX_W), F32),
                   jax.ShapeDtypeStruct((L, 8, 128), F32)],
        compiler_params=_cparams(("arbitrary",)),
    )(lam, b_t, gb, gc, da)


def s5_bu(l, p, b_bd, tm=512):
    T = p.shape[0]

    def body(l_ref, u_ref, b_ref, bu_ref):
        u = u_ref[...].astype(BF16)
        bu_ref[0] = _dot(u, b_ref[0])
        bu_ref[1] = _dot(u, b_ref[1])

    grid_spec = pltpu.PrefetchScalarGridSpec(
        num_scalar_prefetch=1, grid=(T // tm,),
        in_specs=[pl.BlockSpec((tm, MIX_W), lambda i, l: (i, 3)),
                  pl.BlockSpec((None, 2, MIX_W, SSM_W), lambda i, l: (l[0], 0, 0, 0))],
        out_specs=pl.BlockSpec((2, tm, SSM_W), lambda i, l: (0, i, 0)),
    )
    return pl.pallas_call(
        body, name="s5_bu", grid_spec=grid_spec,
        out_shape=jax.ShapeDtypeStruct((2, T, SSM_W), F32),
        compiler_params=_cparams(("arbitrary",)),
    )(l, p, b_bd)


def s5_scan(l, avec, xs, reverse):
    T = xs.shape[1]
    CH = SEQ_CHUNK
    nC = T // CH
    LW = 128
    n_steps = CH.bit_length() - 1

    def body(l_ref, a_ref, x_ref, s_ref):
        ar = a_ref[0:1, :]
        ai = -a_ref[1:2, :] if reverse else a_ref[1:2, :]
        pows = [(ar, ai)]
        for _ in range(n_steps - 1):
            r, i = pows[-1]
            pows.append((r * r - i * i, 2.0 * r * i))
        row = lax.broadcasted_iota(jnp.int32, (CH, LW), 0)

        def local_scan(re, im):
            for k in range(n_steps):
                d = 1 << k
                pr, pi = pows[k]
                if reverse:
                    keep = row < CH - d
                    sr, si = _shift_up(re, d), _shift_up(im, d)
                else:
                    keep = row >= d
                    sr, si = _shift_down(re, d), _shift_down(im, d)
                sr = jnp.where(keep, sr, 0.0)
                si = jnp.where(keep, si, 0.0)
                re, im = re + pr * sr - pi * si, im + pr * si + pi * sr
            return re, im

        edge = CH - 1 if reverse else 0
        pw_re, pw_im = local_scan(jnp.where(row == edge, ar, 0.0), jnp.where(row == edge, ai, 0.0))
        last = 0 if reverse else CH - 1

        def chunk(c, carry):
            cr, ci = carry
            cc = nC - 1 - c if reverse else c
            t0 = pl.multiple_of(cc * CH, CH)
            re, im = local_scan(x_ref[0, pl.ds(t0, CH), :], x_ref[1, pl.ds(t0, CH), :])
            re2 = re + pw_re * cr - pw_im * ci
            im2 = im + pw_re * ci + pw_im * cr
            s_ref[0, pl.ds(t0, CH), :] = re2
            s_ref[1, pl.ds(t0, CH), :] = im2
            return re2[last:last + 1, :], im2[last:last + 1, :]

        lax.fori_loop(0, nC, chunk, (jnp.zeros((1, LW), F32), jnp.zeros((1, LW), F32)))

    grid_spec = pltpu.PrefetchScalarGridSpec(
        num_scalar_prefetch=1, grid=(SSM_W // LW,),
        in_specs=[pl.BlockSpec((None, 8, LW), lambda g, l: (l[0], 0, g)),
                  pl.BlockSpec((2, T, LW), lambda g, l: (0, 0, g))],
        out_specs=pl.BlockSpec((2, T, LW), lambda g, l: (0, 0, g)),
    )
    return pl.pallas_call(
        body, name="s5_scan_rev" if reverse else "s5_scan_fwd", grid_spec=grid_spec,
        out_shape=jax.ShapeDtypeStruct((2, T, SSM_W), F32),
        compiler_params=_cparams(("arbitrary",)),
    )(l, avec, xs)


_GELU_C = 0.7978845608028654
_GELU_K = 0.044715


def _s5_y(u, s_ref, c_ref, d_row):
    y = _dot(s_ref[0].astype(BF16), c_ref[0]) - _dot(s_ref[1].astype(BF16), c_ref[1])
    return y + d_row * u


def s5_out(l, p, s, c_bd, ssm_d, tm=512):
    T = p.shape[0]

    def body(l_ref, u_ref, s_ref, c_ref, d_ref, yg_ref):
        y = _s5_y(u_ref[...], s_ref, c_ref, d_ref[0:1, :])
        th = jnp.tanh(_GELU_C * (y + _GELU_K * y * y * y))
        yg_ref[...] = (0.5 * y * (1.0 + th)).astype(BF16)

    grid_spec = pltpu.PrefetchScalarGridSpec(
        num_scalar_prefetch=1, grid=(T // tm,),
        in_specs=[pl.BlockSpec((tm, MIX_W), lambda i, l: (i, 3)),
                  pl.BlockSpec((2, tm, SSM_W), lambda i, l: (0, i, 0)),
                  pl.BlockSpec((None, 2, SSM_W, MIX_W), lambda i, l: (l[0], 0, 0, 0)),
                  pl.BlockSpec((None, 8, MIX_W), lambda i, l: (l[0], 0, 0))],
        out_specs=pl.BlockSpec((tm, MIX_W), lambda i, l: (i, 0)),
    )
    return pl.pallas_call(
        body, name="s5_out", grid_spec=grid_spec,
        out_shape=jax.ShapeDtypeStruct((T, MIX_W), BF16),
        compiler_params=_cparams(("arbitrary",)),
    )(l, p, s, c_bd, ssm_d)


def s5_bwd_y(l, p, s, dyg, c_bd, ssm_d, tm=512):
    T = p.shape[0]

    def body(l_ref, u_ref, s_ref, dyg_ref, c_ref, d_ref, ds_ref, du_ref, gc_ref, dd_ref):
        @pl.when(pl.program_id(0) == 0)
        def _():
            gc_ref[...] = jnp.zeros_like(gc_ref)
            dd_ref[...] = jnp.zeros_like(dd_ref)

        u = u_ref[...]
        y = _s5_y(u, s_ref, c_ref, d_ref[0:1, :])
        inner = _GELU_C * (y + _GELU_K * y * y * y)
        th = jnp.tanh(inner)
        dgelu = 0.5 * (1.0 + th) + 0.5 * y * (1.0 - th * th) * (_GELU_C * (1.0 + 3.0 * _GELU_K * y * y))
        dy = dyg_ref[...] * dgelu
        dd_ref[0:1, :] += _colsum(dy * u)
        du_ref[...] = dy * d_ref[0:1, :]
        dyb = dy.astype(BF16)
        ds_ref[0] = _dot_nt(dyb, c_ref[0])
        ds_ref[1] = -_dot_nt(dyb, c_ref[1])
        gc_ref[0] += _dot_tn(s_ref[0].astype(BF16), dyb)
        gc_ref[1] -= _dot_tn(s_ref[1].astype(BF16), dyb)

    grid_spec = pltpu.PrefetchScalarGridSpec(
        num_scalar_prefetch=1, grid=(T // tm,),
        in_specs=[pl.BlockSpec((tm, MIX_W), lambda i, l: (i, 3)),
                  pl.BlockSpec((2, tm, SSM_W), lambda i, l: (0, i, 0)),
                  pl.BlockSpec((tm, MIX_W), lambda i, l: (i, 0)),
                  pl.BlockSpec((None, 2, SSM_W, MIX_W), lambda i, l: (l[0], 0, 0, 0)),
                  pl.BlockSpec((None, 8, MIX_W), lambda i, l: (l[0], 0, 0))],
        out_specs=[pl.BlockSpec((2, tm, SSM_W), lambda i, l: (0, i, 0)),
                   pl.BlockSpec((tm, MIX_W), lambda i, l: (i, 0)),
                   pl.BlockSpec((2, SSM_W, MIX_W), lambda i, l: (0, 0, 0)),
                   pl.BlockSpec((8, MIX_W), lambda i, l: (0, 0))],
    )
    return pl.pallas_call(
        body, name="s5_bwd_y", grid_spec=grid_spec,
        out_shape=[jax.ShapeDtypeStruct((2, T, SSM_W), F32), jax.ShapeDtypeStruct((T, MIX_W), F32),
                   jax.ShapeDtypeStruct((2, SSM_W, MIX_W), F32), jax.ShapeDtypeStruct((8, MIX_W), F32)],
        compiler_params=_cparams(("arbitrary",)),
    )(l, p, s, dyg, c_bd, ssm_d)


def s5_bwd_u(l, p, lam_s, du_skip, b_bd, tm=512):
    T = p.shape[0]

    def body(l_ref, u_ref, ls_ref, dus_ref, b_ref, du_ref, gb_ref):
        @pl.when(pl.program_id(0) == 0)
        def _():
            gb_ref[...] = jnp.zeros_like(gb_ref)

        u = u_ref[...].astype(BF16)
        lr = ls_ref[0].astype(BF16)
        li = ls_ref[1].astype(BF16)
        gb_ref[0] += _dot_tn(u, lr)
        gb_ref[1] += _dot_tn(u, li)
        du_ref[...] = (dus_ref[...] + _dot_nt(lr, b_ref[0]) + _dot_nt(li, b_ref[1])).astype(BF16)

    grid_spec = pltpu.PrefetchScalarGridSpec(
        num_scalar_prefetch=1, grid=(T // tm,),
        in_specs=[pl.BlockSpec((tm, MIX_W), lambda i, l: (i, 3)),
                  pl.BlockSpec((2, tm, SSM_W), lambda i, l: (0, i, 0)),
                  pl.BlockSpec((tm, MIX_W), lambda i, l: (i, 0)),
                  pl.BlockSpec((None, 2, MIX_W, SSM_W), lambda i, l: (l[0], 0, 0, 0))],
        out_specs=[pl.BlockSpec((tm, MIX_W), lambda i, l: (i, 0)),
                   pl.BlockSpec((2, MIX_W, SSM_W), lambda i, l: (0, 0, 0))],
    )
    return pl.pallas_call(
        body, name="s5_bwd_u", grid_spec=grid_spec,
        out_shape=[jax.ShapeDtypeStruct((T, MIX_W), BF16), jax.ShapeDtypeStruct((2, MIX_W, SSM_W), F32)],
        compiler_params=_cparams(("arbitrary",)),
    )(l, p, lam_s, du_skip, b_bd)


def s5_bwd_a(s, lam_s):
    T = s.shape[1]
    nC = T // SEQ_CHUNK
    LW = 128

    def body(s_ref, ls_ref, da_ref):
        dre = jnp.zeros((1, LW), F32)
        dim = jnp.zeros((1, LW), F32)
        for c in range(nC):
            t0 = c * SEQ_CHUNK
            sr = _shift_down(_rows_with_lead(lambda a, n: s_ref[0, a:a + n, :], c, LW), 1)[HALO:]
            si = _shift_down(_rows_with_lead(lambda a, n: s_ref[1, a:a + n, :], c, LW), 1)[HALO:]
            lr = ls_ref[0, t0:t0 + SEQ_CHUNK, :]
            li = ls_ref[1, t0:t0 + SEQ_CHUNK, :]
            dre += _colsum(sr * lr + si * li)
            dim += _colsum(sr * li - si * lr)
        da_ref[...] = jnp.concatenate([dre, dim, jnp.zeros((6, LW), F32)], axis=0)

    blk = pl.BlockSpec((2, T, LW), lambda g: (0, 0, g))
    return pl.pallas_call(
        body, name="s5_bwd_a", grid=(SSM_W // LW,),
        in_specs=[blk, blk],
        out_specs=pl.BlockSpec((8, LW), lambda g: (0, g)),
        out_shape=jax.ShapeDtypeStruct((8, SSM_W), F32),
        compiler_params=_cparams(("arbitrary",)),
    )(s, lam_s)


SB_BLK = 128
SB_SCALE = SB_HEAD ** -0.5


def _split_bf16(x):
    hi = x.astype(BF16)
    return hi, (x - hi.astype(F32)).astype(BF16)


def _dot_split(x, tri):
    hi, lo = _split_bf16(x)
    return _dot(hi, tri) + _dot(lo, tri)


def _sb_valid(r0, c0):
    row = lax.broadcasted_iota(jnp.int32, (SB_BLK, SB_BLK), 0) + r0
    col = lax.broadcasted_iota(jnp.int32, (SB_BLK, SB_BLK), 1) + c0
    return col < row


def _sb_logits(q, k_blk, valid):
    z = _dot_nt(q, k_blk)
    sp = jnp.log(1.0 + jnp.exp(-jnp.abs(z)))
    ls_pos = jnp.minimum(z, 0.0) - sp
    lk = jnp.where(valid, jnp.minimum(-z, 0.0) - sp, 0.0)
    return z, ls_pos, lk


def _sb_heads():
    lane = lax.broadcasted_iota(jnp.int32, (SB_BLK, SB_BLK), 1)
    masks = (lane < SB_HEAD, lane >= SB_HEAD)
    return [(slice(s * SB_BLK, (s + 1) * SB_BLK), masks[hh]) for s in range(MIX_W // SB_BLK) for hh in range(2)]


def _tri(lower):
    r = lax.broadcasted_iota(jnp.int32, (SB_BLK, SB_BLK), 0)
    c = lax.broadcasted_iota(jnp.int32, (SB_BLK, SB_BLK), 1)
    return jnp.where(r > c if lower else r < c, 1.0, 0.0).astype(BF16)


def sb_fwd(p):
    T = p.shape[0]
    W = MIX_W
    nB = T // SB_BLK

    def body(q_ref, k_ref, v_ref, o_ref, acc_sc):
        tri = _tri(True)
        heads = _sb_heads()

        def qblock(i, _):
            r0 = pl.multiple_of(i * SB_BLK, SB_BLK)
            qs = [(jnp.where(hm, q_ref[pl.ds(r0, SB_BLK), ls], 0.0) * SB_SCALE).astype(BF16) for ls, hm in heads]
            acc_sc[...] = jnp.zeros_like(acc_sc)

            def kblock(jj, runs):
                c0 = pl.multiple_of((i - jj) * SB_BLK, SB_BLK)
                valid = _sb_valid(r0, c0)
                new_runs = []
                for h, (ls, hm) in enumerate(heads):
                    kb = k_ref[pl.ds(c0, SB_BLK), ls].astype(BF16)
                    vb = jnp.where(hm, v_ref[pl.ds(c0, SB_BLK), ls], 0.0).astype(BF16)
                    _, ls_pos, lk = _sb_logits(qs[h], kb, valid)
                    logw = ls_pos + _dot_split(lk, tri) + runs[h]
                    a = jnp.where(valid, jnp.exp(logw), 0.0).astype(BF16)
                    acc_sc[:, ls] += _dot(a, vb)
                    new_runs.append(runs[h] + jnp.sum(lk, axis=1, keepdims=True))
                return tuple(new_runs)

            lax.fori_loop(0, i + 1, kblock, tuple(jnp.zeros((SB_BLK, 1), F32) for _ in heads))
            o_ref[pl.ds(r0, SB_BLK), :] = acc_sc[...].astype(BF16)
            return 0

        lax.fori_loop(0, nB, qblock, 0)

    return pl.pallas_call(
        body, name="sb_fwd", grid=(1,),
        in_specs=[pl.BlockSpec((T, W), lambda i: (0, 5)), pl.BlockSpec((T, W), lambda i: (0, 6)),
                  pl.BlockSpec((T, W), lambda i: (0, 7))],
        out_specs=pl.BlockSpec((T, W), lambda i: (0, 0)),
        out_shape=jax.ShapeDtypeStruct((T, W), BF16),
        scratch_shapes=[pltpu.VMEM((SB_BLK, W), F32)],
        compiler_params=_cparams(("arbitrary",)),
    )(p, p, p)


def sb_bwd(p, do):
    T = p.shape[0]
    W = MIX_W
    nB = T // SB_BLK

    def body(q_ref, k_ref, v_ref, do_ref, dqkv_ref, dq_sc, dk_sc, dv_sc, run_sc):
        tri_gt = _tri(True)
        tri_lt = _tri(False)
        heads = _sb_heads()
        nH = len(heads)
        dq_sc[...] = jnp.zeros_like(dq_sc)
        dk_sc[...] = jnp.zeros_like(dk_sc)
        dv_sc[...] = jnp.zeros_like(dv_sc)
        zcol = tuple(jnp.zeros((SB_BLK, 1), F32) for _ in heads)

        def qblock(i, _):
            r0 = pl.multiple_of(i * SB_BLK, SB_BLK)
            qs = [(jnp.where(hm, q_ref[pl.ds(r0, SB_BLK), ls], 0.0) * SB_SCALE).astype(BF16) for ls, hm in heads]
            dobs = [jnp.where(hm, do_ref[pl.ds(r0, SB_BLK), ls], 0.0).astype(BF16) for ls, hm in heads]

            def suffix(jj, runs):
                c0 = pl.multiple_of((i - jj) * SB_BLK, SB_BLK)
                valid = _sb_valid(r0, c0)
                new_runs = []
                for h, (ls, hm) in enumerate(heads):
                    kb = k_ref[pl.ds(c0, SB_BLK), ls].astype(BF16)
                    _, _, lk = _sb_logits(qs[h], kb, valid)
                    run_sc[h, pl.ds(c0, SB_BLK), :] = jnp.broadcast_to(runs[h], (SB_BLK, SB_BLK))
                    new_runs.append(runs[h] + jnp.sum(lk, axis=1, keepdims=True))
                return tuple(new_runs)

            lax.fori_loop(0, i + 1, suffix, zcol)

            def kblock(j, pres):
                c0 = pl.multiple_of(j * SB_BLK, SB_BLK)
                valid = _sb_valid(r0, c0)
                new_pres = []
                for h, (ls, hm) in enumerate(heads):
                    kf = k_ref[pl.ds(c0, SB_BLK), ls]
                    kb = kf.astype(BF16)
                    kbm = jnp.where(hm, kf, 0.0).astype(BF16)
                    vb = v_ref[pl.ds(c0, SB_BLK), ls].astype(BF16)
                    z, ls_pos, lk = _sb_logits(qs[h], kb, valid)
                    logw = ls_pos + _dot_split(lk, tri_gt) + run_sc[h, pl.ds(c0, SB_BLK), 0:1]
                    a = jnp.where(valid, jnp.exp(logw), 0.0)
                    dlw = _dot_nt(dobs[h], vb) * a
                    g = pres[h] + _dot_split(dlw, tri_lt)
                    sig = _sigmoid(z)
                    dz = jnp.where(valid, dlw * (1.0 - sig) - g * sig, 0.0).astype(BF16)
                    dk_sc[pl.ds(c0, SB_BLK), ls] += _dot_tn(dz, qs[h])
                    dv_sc[pl.ds(c0, SB_BLK), ls] += _dot_tn(a.astype(BF16), dobs[h])
                    dq_sc[pl.ds(r0, SB_BLK), ls] += _dot(dz, kbm)
                    new_pres.append(pres[h] + jnp.sum(dlw, axis=1, keepdims=True))
                return tuple(new_pres)

            lax.fori_loop(0, i + 1, kblock, zcol)
            return 0

        lax.fori_loop(0, nB, qblock, 0)
        dqkv_ref[:, 0:W] = (dq_sc[...] * SB_SCALE).astype(BF16)
        dqkv_ref[:, W:2 * W] = dk_sc[...].astype(BF16)
        dqkv_ref[:, 2 * W:3 * W] = dv_sc[...].astype(BF16)

    return pl.pallas_call(
        body, name="sb_bwd", grid=(1,),
        in_specs=[pl.BlockSpec((T, W), lambda i: (0, 5)), pl.BlockSpec((T, W), lambda i: (0, 6)),
                  pl.BlockSpec((T, W), lambda i: (0, 7)), pl.BlockSpec((T, W), lambda i: (0, 0))],
        out_specs=pl.BlockSpec((T, 3 * W), lambda i: (0, 0)),
        out_shape=jax.ShapeDtypeStruct((T, 3 * W), BF16),
        scratch_shapes=[pltpu.VMEM((T, W), F32), pltpu.VMEM((T, W), F32), pltpu.VMEM((T, W), F32),
                        pltpu.VMEM((W // SB_HEAD, T, SB_BLK), F32)],
        compiler_params=_cparams(("arbitrary",)),
    )(p, p, p, do)


def _dot_cols(a, w_ref):
    return jnp.concatenate([_dot(a, w_ref[j]) for j in range(N_DEV)], axis=1)


def _dot_cols_nt(dy, w_ref):
    n = w_ref.shape[2]
    out = _dot_nt(dy[:, 0:n], w_ref[0])
    for j in range(1, N_DEV):
        out += _dot_nt(dy[:, j * n:(j + 1) * n], w_ref[j])
    return out


def _acc_cols_tn(acc_ref, a, dy):
    n = acc_ref.shape[2]
    for j in range(N_DEV):
        acc_ref[j] += _dot_tn(a, dy[:, j * n:(j + 1) * n])


def _merge_branches(za_ref, yg_ref, z_ref, o_ref, gate_refs, wc_ref, wglu_ref, wp_ref, ws_ref):
    D = D_MODEL
    glu = _dot_cols(yg_ref[...], wglu_ref)
    glu_a, sg = glu[:, :D], _sigmoid(glu[:, D:])
    ys = [_dot_cols(za_ref[...], wc_ref), glu_a * sg, _dot_cols(z_ref[...], wp_ref), _dot_cols(o_ref[...], ws_ref)]
    gs = [_sigmoid(g[...]) for g in gate_refs]
    merged = gs[0] * ys[0] + gs[1] * ys[1] + gs[2] * ys[2] + gs[3] * ys[3]
    return ys, gs, glu_a, sg, merged


def _merge_specs(tm, D):
    W = MIX_W
    br = pl.BlockSpec((tm, W), lambda i, l: (i, 0))
    gates = [pl.BlockSpec((tm, D), functools.partial(lambda i, l, b: (i, 2 + b), b=b)) for b in range(4)]
    wsm = pl.BlockSpec((None, N_DEV, W, D // N_DEV), lambda i, l: (l[0], 0, 0, 0))
    weights = [wsm, pl.BlockSpec((None, N_DEV, W, 2 * D // N_DEV), lambda i, l: (l[0], 0, 0, 0)), wsm, wsm,
               pl.BlockSpec((None, D, D), lambda i, l: (l[0], 0, 0))]
    return [br] * 4 + gates, weights


def merge_fwd(l, p, za, yg, z, o, x, pv, wc, wglu, wp, ws, wo, tm=256):
    T, D = x.shape

    def body(l_ref, za_ref, yg_ref, z_ref, o_ref, g0, g1, g2, g3, x_ref, pv_ref,
             wc_ref, wglu_ref, wp_ref, ws_ref, wo_ref, xn_ref, m_ref):
        _, _, _, _, merged = _merge_branches(za_ref, yg_ref, z_ref, o_ref, (g0, g1, g2, g3),
                                             wc_ref, wglu_ref, wp_ref, ws_ref)
        m = _dot(merged.astype(BF16), wo_ref[...])
        m_ref[...] = m
        xn_ref[...] = _postnorm_res(x_ref[...], m, pv_ref, 1.0)

    acts, weights = _merge_specs(tm, D)
    tile = pl.BlockSpec((tm, D), lambda i, l: (i, 0))
    grid_spec = pltpu.PrefetchScalarGridSpec(
        num_scalar_prefetch=1, grid=(T // tm,),
        in_specs=acts + [tile, pl.BlockSpec((8, D), lambda i, l: (0, 0))] + weights,
        out_specs=[tile, tile],
    )
    return pl.pallas_call(
        body, name="merge_fwd", grid_spec=grid_spec,
        out_shape=[jax.ShapeDtypeStruct((T, D), F32), jax.ShapeDtypeStruct((T, D), F32)],
        compiler_params=_cparams(("arbitrary",)),
    )(l, za, yg, z, o, p, p, p, p, x, pv, wc, wglu, wp, ws, wo)


def merge_bwd(l, p, za, yg, z, o, m, dxn, pv, wc, wglu, wp, ws, wo, tm=128):
    T, D = m.shape
    W = MIX_W
    ni = T // tm

    def body(l_ref, za_ref, yg_ref, z_ref, o_ref, g0, g1, g2, g3, m_ref, dxn_ref, pv_ref,
             wc_ref, wglu_ref, wp_ref, ws_ref, wo_ref,
             dza_ref, dyg_ref, dz_ref, do_ref, dg_ref, pg_ref, gwc_ref, gwglu_ref, gwp_ref, gws_ref, gwo_ref,
             awc, awglu, awp, aws, awo):
        i = pl.program_id(0)

        @pl.when(i == 0)
        def _():
            pg_ref[...] = jnp.zeros_like(pg_ref)
            for a in (awc, awglu, awp, aws, awo):
                a[...] = jnp.zeros_like(a)

        ys, gs, glu_a, sg, merged = _merge_branches(za_ref, yg_ref, z_ref, o_ref, (g0, g1, g2, g3),
                                                    wc_ref, wglu_ref, wp_ref, ws_ref)
        dm = _postnorm_bwd(dxn_ref[...], m_ref[...], pv_ref, pg_ref, 1.0).astype(BF16)
        awo[...] += _dot_tn(merged.astype(BF16), dm)
        dmerged = _dot_nt(dm, wo_ref[...])
        for b in range(4):
            dg_ref[:, b * D:(b + 1) * D] = (dmerged * ys[b] * gs[b] * (1.0 - gs[b])).astype(BF16)
        dya = (dmerged * gs[0]).astype(BF16)
        _acc_cols_tn(awc, za_ref[...], dya)
        dza_ref[...] = _dot_cols_nt(dya, wc_ref)
        dyc = (dmerged * gs[2]).astype(BF16)
        _acc_cols_tn(awp, z_ref[...], dyc)
        dz_ref[...] = _dot_cols_nt(dyc, wp_ref)
        dyd = (dmerged * gs[3]).astype(BF16)
        _acc_cols_tn(aws, o_ref[...], dyd)
        do_ref[...] = _dot_cols_nt(dyd, ws_ref)
        dyb = dmerged * gs[1]
        dglu = jnp.concatenate([dyb * sg, dyb * glu_a * sg * (1.0 - sg)], axis=1).astype(BF16)
        _acc_cols_tn(awglu, yg_ref[...], dglu)
        dyg_ref[...] = _dot_cols_nt(dglu, wglu_ref)

        @pl.when(i == ni - 1)
        def _():
            gwc_ref[...] = awc[...].astype(BF16)
            gwglu_ref[...] = awglu[...].astype(BF16)
            gwp_ref[...] = awp[...].astype(BF16)
            gws_ref[...] = aws[...].astype(BF16)
            gwo_ref[...] = awo[...].astype(BF16)

    acts, weights = _merge_specs(tm, D)
    tile = pl.BlockSpec((tm, D), lambda i, l: (i, 0))
    br = pl.BlockSpec((tm, W), lambda i, l: (i, 0))
    full = lambda *s: pl.BlockSpec(s, lambda i, l: (0,) * len(s))
    sm, glu_s = (N_DEV, W, D // N_DEV), (N_DEV, W, 2 * D // N_DEV)
    grid_spec = pltpu.PrefetchScalarGridSpec(
        num_scalar_prefetch=1, grid=(ni,),
        in_specs=acts + [tile, tile, pl.BlockSpec((8, D), lambda i, l: (0, 0))] + weights,
        out_specs=[br, br, br, br, pl.BlockSpec((tm, 4 * D), lambda i, l: (i, 0)), full(8, D),
                   full(*sm), full(*glu_s), full(*sm), full(*sm), full(D, D)],
        scratch_shapes=[pltpu.VMEM(sm, F32), pltpu.VMEM(glu_s, F32), pltpu.VMEM(sm, F32),
                        pltpu.VMEM(sm, F32), pltpu.VMEM((D, D), F32)],
    )
    f32br = jax.ShapeDtypeStruct((T, W), F32)
    return pl.pallas_call(
        body, name="merge_bwd", grid_spec=grid_spec,
        out_shape=[f32br, f32br, f32br, f32br, jax.ShapeDtypeStruct((T, 4 * D), BF16),
                   jax.ShapeDtypeStruct((8, D), F32),
                   jax.ShapeDtypeStruct(sm, BF16), jax.ShapeDtypeStruct(glu_s, BF16),
                   jax.ShapeDtypeStruct(sm, BF16), jax.ShapeDtypeStruct(sm, BF16),
                   jax.ShapeDtypeStruct((D, D), BF16)],
        compiler_params=_cparams(("arbitrary",)),
    )(l, za, yg, z, o, p, p, p, p, m, dxn, pv, wc, wglu, wp, ws, wo)


def dp_assemble(d_conv, d_ssm, d_pool, d_qkv, d_gates, tm=512):
    T = d_conv.shape[0]
    W = MIX_W

    def body(c_ref, s_ref, p_ref, q_ref, g_ref, dp_ref):
        dp_ref[:, 0:3 * W] = c_ref[...]
        dp_ref[:, 3 * W:4 * W] = s_ref[...]
        dp_ref[:, 4 * W:5 * W] = p_ref[...]
        dp_ref[:, 5 * W:8 * W] = q_ref[...]
        dp_ref[:, GATE_OFF:] = g_ref[...]

    row = lambda w: pl.BlockSpec((tm, w), lambda i: (i, 0))
    return pl.pallas_call(
        body, name="dp_assemble", grid=(T // tm,),
        in_specs=[row(3 * W), row(W), row(W), row(3 * W), row(4 * D_MODEL)],
        out_specs=row(IN_COLS),
        out_shape=jax.ShapeDtypeStruct((T, IN_COLS), BF16),
        compiler_params=_cparams(("arbitrary",)),
    )(d_conv, d_ssm, d_pool, d_qkv, d_gates)


def loss_head(y, target, tm=512):
    T, D = y.shape

    def body(y_ref, t_ref, dy_ref, loss_ref):
        @pl.when(pl.program_id(0) == 0)
        def _():
            loss_ref[...] = jnp.zeros_like(loss_ref)

        err = y_ref[...] - t_ref[...]
        dy_ref[...] = err * (1.0 / D)
        loss_ref[...] += jnp.sum(err * err) * (0.5 / D)

    tile = pl.BlockSpec((tm, D), lambda i: (i, 0))
    return pl.pallas_call(
        body, name="loss_head", grid=(T // tm,),
        in_specs=[tile, tile],
        out_specs=[tile, pl.BlockSpec((8, 128), lambda i: (0, 0))],
        out_shape=[jax.ShapeDtypeStruct((T, D), F32), jax.ShapeDtypeStruct((8, 128), F32)],
        compiler_params=_cparams(("arbitrary",)),
    )(y, target)


def cast_weights(ws):
    L = ws[0].shape[0]

    def body(*refs):
        n = len(refs) // 2
        for src, dst in zip(refs[:n], refs[n:]):
            dst[...] = src[...].astype(BF16)

    def spec(w):
        nd = w.ndim
        return pl.BlockSpec((None,) + w.shape[1:], lambda l, nd=nd: (l,) + (0,) * (nd - 1))

    return pl.pallas_call(
        body, name="cast_weights", grid=(L,),
        in_specs=[spec(w) for w in ws], out_specs=[spec(w) for w in ws],
        out_shape=[jax.ShapeDtypeStruct(w.shape, BF16) for w in ws],
        compiler_params=_cparams(("arbitrary",)),
    )(*ws)


def _silu(x):
    return x * _sigmoid(x)


def ada_fwd(c_all, w_ada, b_cols):
    L, D, n = w_ada.shape

    def body(c_ref, w_ref, b_ref, o_ref):
        c_act = _silu(c_ref[...]).astype(BF16)
        o_ref[...] = _dot(c_act, w_ref[...].astype(BF16)) + b_ref[...]

    return pl.pallas_call(
        body, name="ada_fwd", grid=(L,),
        in_specs=[pl.BlockSpec((N_DEV, D), lambda l: (0, 0)), pl.BlockSpec((None, D, n), lambda l: (l, 0, 0)),
                  pl.BlockSpec((None, 1, n), lambda l: (l, 0, 0))],
        out_specs=pl.BlockSpec((None, N_DEV, n), lambda l: (l, 0, 0)),
        out_shape=jax.ShapeDtypeStruct((L, N_DEV, n), F32),
        compiler_params=_cparams(("arbitrary",)),
    )(c_all, w_ada, b_cols)


def _adamw(w, g, m, v):
    m = ADAM_B1 * m + (1.0 - ADAM_B1) * g
    v = ADAM_B2 * v + (1.0 - ADAM_B2) * (g * g)
    m_hat = m / (1.0 - ADAM_B1 ** ADAM_STEP)
    v_hat = v / (1.0 - ADAM_B2 ** ADAM_STEP)
    delta = -ADAM_LR * (m_hat / (jnp.sqrt(v_hat) + ADAM_EPS) + ADAM_WD * w)
    return delta, m, v


def ada_update(c_all, dada_cols, w, m, v, rb=256):
    L, D, n = w.shape

    def body(c_ref, d_ref, w_ref, m_ref, v_ref, g_ref, dl_ref, nm_ref, nv_ref):
        c_act = _silu(c_ref[...]).astype(BF16)
        g = _dot_tn(c_act, d_ref[...].astype(BF16))
        g_ref[...] = g
        dl_ref[...], nm_ref[...], nv_ref[...] = _adamw(w_ref[...], g, m_ref[...], v_ref[...])

    blk = pl.BlockSpec((None, rb, n), lambda l, i: (l, i, 0))
    out = jax.ShapeDtypeStruct((L, D, n), F32)
    return pl.pallas_call(
        body, name="ada_update", grid=(L, D // rb),
        in_specs=[pl.BlockSpec((N_DEV, rb), lambda l, i: (0, i)),
                  pl.BlockSpec((None, N_DEV, n), lambda l, i: (l, 0, 0)), blk, blk, blk],
        out_specs=[blk, blk, blk, blk], out_shape=[out, out, out, out],
        compiler_params=_cparams(("arbitrary", "arbitrary")),
    )(c_all, dada_cols, w, m, v)


SUM_UPDATE_RECV_BYTES = 12 * 1024 * 1024


def sum_update(dev, recvs, owns, w, m, v):
    S, R, C = w.shape
    assert len(recvs) == S and len(owns) == S
    rb_max = SUM_UPDATE_RECV_BYTES // (S * N_DEV * C * 2)
    rb = max(r for r in range(8, R + 1, 8) if R % r == 0 and (r <= rb_max or r == 8))
    last = R // rb - 1

    def body(dev_ref, *refs):
        r_refs, o_refs = refs[:S], refs[S:2 * S]
        w_ref, m_ref, v_ref, g_ref, dl_ref, nm_ref, nv_ref = refs[2 * S:]
        me = dev_ref[0]
        for s in range(S):
            @pl.when(pl.program_id(0) == s)
            def _(s=s):
                g = jnp.zeros((rb, C), F32)
                for d in range(N_DEV):
                    g += jnp.where(me == d, o_refs[s][...], r_refs[s][d]).astype(F32)
                g_ref[...] = g
                dl_ref[...], nm_ref[...], nv_ref[...] = _adamw(w_ref[...], g, m_ref[...], v_ref[...])

    def row(sl, i, s):
        return jnp.where(sl == s, i, jnp.where(sl < s, 0, last))

    def rspec(s):
        return pl.BlockSpec((N_DEV, rb, C), lambda sl, i, dev: (0, row(sl, i, s), 0))

    def ospec(s):
        return pl.BlockSpec((None, rb, C), lambda sl, i, dev: (dev[0], row(sl, i, s), 0))

    blk = pl.BlockSpec((None, rb, C), lambda sl, i, dev: (sl, i, 0))
    out = jax.ShapeDtypeStruct((S, R, C), F32)
    grid_spec = pltpu.PrefetchScalarGridSpec(
        num_scalar_prefetch=1, grid=(S, R // rb),
        in_specs=[rspec(s) for s in range(S)] + [ospec(s) for s in range(S)] + [blk, blk, blk],
        out_specs=[blk, blk, blk, blk],
    )
    return pl.pallas_call(
        body, name="sum_update", grid_spec=grid_spec, out_shape=[out, out, out, out],
        compiler_params=_cparams(("arbitrary", "arbitrary")),
    )(dev, *recvs, *owns, w, m, v)


def small_sum(gathered):
    _, R, C = gathered.shape

    def body(g_ref, o_ref):
        acc = g_ref[0]
        for d in range(1, N_DEV):
            acc += g_ref[d]
        o_ref[...] = acc

    return pl.pallas_call(
        body, name="small_sum", grid=(1,),
        in_specs=[pl.BlockSpec((N_DEV, R, C), lambda i: (0, 0, 0))],
        out_specs=pl.BlockSpec((R, C), lambda i: (0, 0)),
        out_shape=jax.ShapeDtypeStruct((R, C), F32),
        compiler_params=_cparams(("arbitrary",)),
    )(gathered)


def small_update(w, g, m, v):
    def body(w_ref, g_ref, m_ref, v_ref, dl_ref, nm_ref, nv_ref):
        dl_ref[...], nm_ref[...], nv_ref[...] = _adamw(w_ref[...], g_ref[...], m_ref[...], v_ref[...])

    blk = pl.BlockSpec(w.shape, lambda i: (0, 0))
    out = jax.ShapeDtypeStruct(w.shape, F32)
    return pl.pallas_call(
        body, name="small_update", grid=(1,),
        in_specs=[blk] * 4, out_specs=[blk] * 3, out_shape=[out] * 3,
        compiler_params=_cparams(("arbitrary",)),
    )(w, g, m, v)


MESH = pl.DeviceIdType.MESH
ANY = pl.BlockSpec(memory_space=pl.ANY)


def _coords():
    return lax.axis_index("x"), lax.axis_index("y"), lax.axis_index("c")


def _dev_index(x, y, c):
    return 4 * x + 2 * y + c


def _at_dev(ref, p, dev):
    return ref.at[(slice(None),) * p + (dev,)]


def all_gather(arrays, ps):
    n = len(arrays)

    def body(*refs):
        ins, outs = refs[:n], refs[n:2 * n]
        send_sems, recv_sems, local_sems = refs[2 * n:]
        x, y, c = _coords()
        me, sibling = (x, y, c), (x, y, 1 - c)
        chips = [(1 - x, y), (x, 1 - y), (1 - x, 1 - y)]

        def copy(a, k, block, to, src=None):
            dst = _at_dev(outs[a], ps[a], _dev_index(*block))
            return pltpu.make_async_remote_copy(
                src_ref=dst if src is None else src, dst_ref=dst,
                send_sem=send_sems.at[a, k], recv_sem=recv_sems.at[a, k], device_id=to, device_id_type=MESH)

        mine = [pltpu.make_async_copy(ins[a], _at_dev(outs[a], ps[a], _dev_index(*me)), local_sems.at[a])
                for a in range(n)]
        for cp in mine:
            cp.start()
        first = []
        for a in range(n):
            first.append(copy(a, 0, me, sibling, src=ins[a]))
            first += [copy(a, 1 + j, me, (*chip, c), src=ins[a]) for j, chip in enumerate(chips)]
        for cp in first:
            cp.start()
        passed = []
        for j, chip in enumerate(chips):
            for a in range(n):
                copy(a, 1 + j, (*chip, c), me).wait_recv()
                fwd = copy(a, 4 + j, (*chip, c), sibling)
                fwd.start()
                passed.append(fwd)
        for a in range(n):
            copy(a, 0, sibling, me).wait_recv()
            for j, chip in enumerate(chips):
                copy(a, 4 + j, (*chip, 1 - c), me).wait_recv()
        for cp in first + passed:
            cp.wait_send()
        for cp in mine:
            cp.wait()

    out_shape = [jax.ShapeDtypeStruct(a.shape[:p] + (N_DEV,) + a.shape[p:], a.dtype) for a, p in zip(arrays, ps)]
    return pl.pallas_call(
        body, name="all_gather", in_specs=[ANY] * n, out_specs=[ANY] * n, out_shape=out_shape,
        scratch_shapes=[pltpu.SemaphoreType.DMA((n, 7)), pltpu.SemaphoreType.DMA((n, 7)),
                        pltpu.SemaphoreType.DMA((n,))],
        compiler_params=pltpu.CompilerParams(has_side_effects=True),
    )(*arrays)


HBM = pl.BlockSpec(memory_space=pltpu.HBM)
SEM = pl.BlockSpec(memory_space=pltpu.SEMAPHORE)
EFFECT = pltpu.SideEffectType.DATAFLOW_SIDE_EFFECTING


def _peers(x, y, c):
    out = []
    for k in range(1, N_DEV):
        out.append((1 - x if k & 4 else x, 1 - y if k & 2 else y, 1 - c if k & 1 else c))
    return out


def _exchange_copy(blocks, lands, send_sems, recv_sems, a, k, peer, slot):
    i = a * (N_DEV - 1) + k
    return pltpu.make_async_remote_copy(
        src_ref=blocks[a].at[_dev_index(*peer)], dst_ref=lands[a].at[slot],
        send_sem=send_sems[i], recv_sem=recv_sems[i], device_id=peer, device_id_type=MESH)


def exchange_start(name, blocks, carry):
    n = len(blocks)
    ns = n * (N_DEV - 1)

    def body(*refs):
        blk, land = refs[:n], refs[n:2 * n]
        send_sems, recv_sems = refs[2 * n + 1:2 * n + 1 + ns], refs[2 * n + 1 + ns:2 * n + 1 + 2 * ns]
        x, y, c = _coords()
        me = _dev_index(x, y, c)
        for k, peer in enumerate(_peers(x, y, c)):
            for a in range(n):
                _exchange_copy(blk, land, send_sems, recv_sems, a, k, peer, me).start()

    hbm = lambda a: pltpu.HBM(a.shape, a.dtype)
    operands = list(blocks) + [lax.empty(a.shape, a.dtype) for a in blocks] + [carry]
    operands = [pltpu.with_memory_space_constraint(a, pltpu.HBM) for a in operands]
    m = len(operands)
    outs = pl.pallas_call(
        body, name=name,
        out_shape=[pltpu.SemaphoreType.DMA(())] * (2 * ns) + [hbm(a) for a in operands],
        in_specs=[HBM] * m, out_specs=[SEM] * (2 * ns) + [HBM] * m,
        input_output_aliases={i: 2 * ns + i for i in range(m)},
        compiler_params=pltpu.CompilerParams(has_side_effects=EFFECT),
    )(*operands)
    rest = outs[2 * ns:]
    return outs[:ns], outs[ns:2 * ns], rest[:n], rest[n:2 * n], rest[-1]


def exchange_wait(name, send_sems, recv_sems, blocks, lands, after):
    n = len(blocks)
    ns = n * (N_DEV - 1)

    def body(*refs):
        blk, land = refs[:n], refs[n:2 * n]
        send_sems, recv_sems = refs[2 * n:2 * n + ns], refs[2 * n + ns:2 * n + 2 * ns]
        x, y, c = _coords()
        for k, peer in enumerate(_peers(x, y, c)):
            for a in range(n):
                cp = _exchange_copy(blk, land, send_sems, recv_sems, a, k, peer, _dev_index(*peer))
                cp.wait_send()
                cp.wait_recv()

    hbm = lambda a: pltpu.HBM(a.shape, a.dtype)
    outs = pl.pallas_call(
        body, name=name,
        out_shape=[hbm(a) for a in blocks] + [hbm(a) for a in lands],
        in_specs=[HBM] * (2 * n) + [SEM] * (2 * ns) + [ANY], out_specs=[HBM] * (2 * n),
        input_output_aliases={i: i for i in range(2 * n)},
        compiler_params=pltpu.CompilerParams(has_side_effects=EFFECT),
    )(*blocks, *lands, *send_sems, *recv_sems, after)
    return outs[:n], outs[n:]


WEIGHT_NAMES = ("w_ada", "b_ada", "g_pre", "g_post", "w_ff_in", "w_ff_out", "w_in", "conv_w", "w_conv_out",
                "lam_re", "lam_im", "log_dt", "ssm_b_re", "ssm_b_im", "ssm_c_re", "ssm_c_im", "ssm_d", "w_glu",
                "w_pool", "pool_scale", "w_pool_out", "w_sb_out", "w_out")
BIG_NAMES = ("w_ff_in", "w_ff_out", "w_in", "w_conv_out", "w_glu", "w_pool_out", "w_sb_out", "w_out")
SMALL_NAMES = ("b_ada", "g_pre", "g_post", "conv_w", "lam_re", "lam_im", "log_dt", "ssm_b_re", "ssm_b_im",
               "ssm_c_re", "ssm_c_im", "ssm_d", "w_pool", "pool_scale")
PACK_LANES = 128
PACK_ROWS = 8


def _pack(arrays):
    flat = jnp.concatenate([a.reshape(-1) for a in arrays])
    unit = PACK_LANES * PACK_ROWS
    flat = jnp.pad(flat, (0, -flat.shape[0] % unit))
    return flat.reshape(-1, PACK_LANES)


def _unpack(pack, shapes):
    flat = pack.reshape(-1)
    out, off = [], 0
    for s in shapes:
        n = 1
        for d in s:
            n *= d
        out.append(flat[off:off + n].reshape(s))
        off += n
    return out


def _pad_rows(a, rows=8):
    return jnp.pad(a, ((0, 0), (0, rows - a.shape[1]), (0, 0)))


def _tile_b(b):
    L = b.shape[0]
    return jnp.tile(b.transpose(0, 3, 1, 2).reshape(L, SSM_GROUP, SSM_W), (1, SSM_GROUPS, 1))


def _tile_c(c):
    L = c.shape[0]
    return jnp.tile(c.transpose(0, 3, 1, 2).reshape(L, SSM_STATE, MIX_W), (1, SSM_GROUPS, 1))


def _step(x, c, target, W, M, V):
    T, D = x.shape[1], x.shape[2]
    L = W["w_ada"].shape[0]
    x = x[0]
    target = target[0]
    ax, ay, ac = _coords()
    dev = _dev_index(ax, ay, ac)
    n_ada = W["w_ada"].shape[2]

    cast = cast_weights([W[k] for k in BIG_NAMES])
    gathered = all_gather(list(cast) + [W["g_pre"], W["g_post"], W["conv_w"], c],
                          [2, 2, 1, 1, 1, 1, 1, 1, 0, 0, 0, 0])
    wg_ff_in = gathered[0].reshape(L, 2, 2, 4, D, FF_BLK)
    wg_ff_out = gathered[1].reshape(L, 2, 4, FF_BLK, D)
    wg_in, wg_conv, wg_glu, wg_pool, wg_sb = gathered[2:7]
    wg_out = gathered[7].reshape(L, D, D)
    g_pre = gathered[8].transpose(1, 2, 0, 3).reshape(L, N_SUB, D)
    g_post = gathered[9].transpose(1, 2, 0, 3).reshape(L, N_SUB, D)
    conv_w = _pad_rows(gathered[10].transpose(1, 2, 0, 3).reshape(L, 3, MIX_W))
    c_all = gathered[11].reshape(N_DEV, D)

    b_cols = lax.dynamic_slice_in_dim(W["b_ada"], dev * n_ada, n_ada, axis=1)[:, None, :]
    ada_cols = ada_fwd(c_all, W["w_ada"], b_cols)
    ada_all = all_gather([ada_cols], [0])[0]
    ada = lax.dynamic_index_in_dim(ada_all, dev, axis=2, keepdims=False)
    ada = ada.transpose(1, 0, 2).reshape(L, N_SUB, 3, D)
    zeros = jnp.zeros((L, N_SUB, D), F32)
    pv_all = jnp.stack([g_pre, ada[:, :, 0], ada[:, :, 1], g_post, ada[:, :, 2], zeros, zeros, zeros], axis=2)

    lam = jnp.stack([W["lam_re"].reshape(L, SSM_W), W["lam_im"].reshape(L, SSM_W),
                     jnp.repeat(W["log_dt"], SSM_STATE, axis=1)], axis=1)
    lam = _pad_rows(lam)
    b_t = jnp.stack([_tile_b(W["ssm_b_re"]), _tile_b(W["ssm_b_im"])], axis=1)
    c_t = jnp.stack([_tile_c(W["ssm_c_re"]), _tile_c(W["ssm_c_im"])], axis=1)
    avec, b_bd, c_bd = s5_params(lam, b_t, c_t)
    ssm_d = _pad_rows(W["ssm_d"][:, None, :])
    pool_scale = _pad_rows(W["pool_scale"][:, None, :])
    eye4 = jnp.eye(len(POOL_WINDOWS), dtype=F32)
    w_bd = jnp.einsum("lgcd,gh->lgchd", W["w_pool"], eye4).reshape(L, MIX_W, MIX_W).astype(BF16)

    saved = []
    for l in range(L):
        li = jnp.array([l], jnp.int32)
        x0 = x
        ab0, f0, x1 = ffn_fwd(jnp.array([l, 0], jnp.int32), x0, pv_all[l, 0], wg_ff_in, wg_ff_out)
        p = mix_in_fwd(li, x1, pv_all[l, 1], wg_in)
        za = conv_fwd(li, p, conv_w)
        z = pool_fwd(li, p, w_bd, pool_scale)
        s = s5_scan(li, avec, s5_bu(li, p, b_bd), False)
        yg = s5_out(li, p, s, c_bd, ssm_d)
        o = sb_fwd(p)
        x2, m = merge_fwd(li, p, za, yg, z, o, x1, pv_all[l, 1], wg_conv, wg_glu, wg_pool, wg_sb, wg_out)
        ab1, f1, x = ffn_fwd(jnp.array([l, 1], jnp.int32), x2, pv_all[l, 2], wg_ff_in, wg_ff_out)
        saved.append((x0, ab0, f0, x1, p, za, z, s, yg, o, m, x2, ab1, f1))

    dx, loss_blk = loss_head(x, target)
    loss = lax.psum(loss_blk[0, 0], ("x", "y", "c"))

    recvs, owns, in_flight = [None] * L, [None] * L, None
    pgs = [None] * L
    small = {k: [None] * L for k in ("conv_w", "w_bd", "pool_scale", "ssm_d", "gb", "gc", "da")}
    for l in reversed(range(L)):
        li = jnp.array([l], jnp.int32)
        x0, ab0, f0, x1, p, za, z, s, yg, o, m, x2, ab1, f1 = saved[l]
        dab, h, df, dx, pg2 = ffn_bwd_act(jnp.array([l, 1], jnp.int32), dx, x2, f1, pv_all[l, 2], ab1,
                                          wg_ff_in, wg_ff_out)
        g_in1, g_out1 = ffn_bwd_w(h, df, ab1, dab)
        (dza, dyg, dz, do, dgates, pg1m, g_conv, g_glu, g_pool, g_sb, g_wo) = merge_bwd(
            li, p, za, yg, z, o, m, dx, pv_all[l, 1], wg_conv, wg_glu, wg_pool, wg_sb, wg_out)
        d_conv, small["conv_w"][l] = conv_bwd(li, p, dza, conv_w)
        d_pool, small["w_bd"][l], small["pool_scale"][l] = pool_bwd(li, p, dz, w_bd, pool_scale)
        ds, du_skip, small["gc"][l], small["ssm_d"][l] = s5_bwd_y(li, p, s, dyg, c_bd, ssm_d)
        lam_s = s5_scan(li, avec, ds, True)
        d_ssm, small["gb"][l] = s5_bwd_u(li, p, lam_s, du_skip, b_bd)
        small["da"][l] = s5_bwd_a(s, lam_s)
        d_qkv = sb_bwd(p, do)
        dp = dp_assemble(d_conv, d_ssm, d_pool, d_qkv, dgates)
        dx, h, pg1i = mix_in_bwd_act(li, dp, dx, x1, pv_all[l, 1], wg_in)
        g_win = matmul_tn(h, dp, IN_BLK)
        dab, h, df, dx, pg0 = ffn_bwd_act(jnp.array([l, 0], jnp.int32), dx, x0, f0, pv_all[l, 0], ab0,
                                          wg_ff_in, wg_ff_out)
        g_in0, g_out0 = ffn_bwd_w(h, df, ab0, dab)
        pgs[l] = jnp.stack([pg0, pg1m + pg1i, pg2])
        blocks = [g_in0.reshape(N_DEV, D, FF_BLK), g_in1.reshape(N_DEV, D, FF_BLK),
                  g_out0.reshape(N_DEV, D_FF // N_DEV, D), g_out1.reshape(N_DEV, D_FF // N_DEV, D),
                  g_win, g_conv, g_glu, g_pool, g_sb, g_wo.reshape(N_DEV, D // N_DEV, D)]
        if in_flight is not None:
            owns[in_flight[0]], recvs[in_flight[0]] = exchange_wait(f"exchange_wait_{in_flight[0]}", *in_flight[1:], dx)
        s_sem, r_sem, blocks, lands, dx = exchange_start(f"exchange_start_{l}", blocks, dx)
        in_flight = (l, s_sem, r_sem, blocks, lands)

    dlam, db_t, dc_t, dldt = s5_params_bwd(lam, b_t, jnp.stack(small["gb"]), jnp.stack(small["gc"]),
                                           jnp.stack(small["da"]))
    pg = jnp.stack(pgs)
    d_ada = jnp.stack([pg[:, :, PV_SHIFT], pg[:, :, PV_SCALE], pg[:, :, PV_GATE]], axis=2).reshape(L, N_SUB * 3 * D)
    db = db_t.reshape(L, 2, SSM_GROUPS, SSM_GROUP, SSM_GROUPS, SSM_STATE)
    db = jnp.einsum("lrghgp->lrgph", db)
    dc = dc_t.reshape(L, 2, SSM_GROUPS, SSM_STATE, SSM_GROUPS, SSM_GROUP)
    dc = jnp.einsum("lrgpgh->lrghp", dc)
    d_wpool = jnp.einsum("lgcgd->lgcd", jnp.stack(small["w_bd"]).reshape(L, 4, 64, 4, 64))
    contrib = {
        "b_ada": d_ada, "g_pre": pg[:, :, PV_GPRE], "g_post": pg[:, :, PV_GPOST],
        "conv_w": jnp.stack(small["conv_w"])[:, :3], "lam_re": dlam[:, 0].reshape(L, SSM_GROUPS, SSM_STATE),
        "lam_im": dlam[:, 1].reshape(L, SSM_GROUPS, SSM_STATE), "log_dt": dldt[:, 2, :SSM_GROUPS],
        "ssm_b_re": db[:, 0], "ssm_b_im": db[:, 1], "ssm_c_re": dc[:, 0], "ssm_c_im": dc[:, 1],
        "ssm_d": jnp.stack(small["ssm_d"])[:, 0], "w_pool": d_wpool,
        "pool_scale": jnp.stack(small["pool_scale"])[:, 0],
    }
    contrib_shapes = [contrib[k].shape for k in SMALL_NAMES]
    pack_all = all_gather([_pack([contrib[k] for k in SMALL_NAMES])], [0])[0]
    total = dict(zip(SMALL_NAMES, _unpack(small_sum(pack_all), contrib_shapes)))
    d_ada_all = pack_all.reshape(N_DEV, -1)[:, :L * N_SUB * 3 * D].reshape(N_DEV, L, N_SUB * 3 * D)
    dada_cols = lax.dynamic_slice_in_dim(d_ada_all, dev * n_ada, n_ada, axis=2).transpose(1, 0, 2)
    n_g = D // N_DEV
    grads = {}
    for k in SMALL_NAMES:
        g = total[k]
        if k in ("g_pre", "g_post"):
            g = lax.dynamic_slice_in_dim(g, dev * n_g, n_g, axis=2)
        elif k == "conv_w":
            g = lax.dynamic_slice_in_dim(g, dev * (MIX_W // N_DEV), MIX_W // N_DEV, axis=2)
        grads[k] = g

    delta, new_m, new_v = {}, {}, {}
    shapes = [W[k].shape for k in SMALL_NAMES]
    dl, nm, nv = small_update(_pack([W[k] for k in SMALL_NAMES]), _pack([grads[k] for k in SMALL_NAMES]),
                              _pack([M[k] for k in SMALL_NAMES]), _pack([V[k] for k in SMALL_NAMES]))
    for k, a, b, cc in zip(SMALL_NAMES, _unpack(dl, shapes), _unpack(nm, shapes), _unpack(nv, shapes)):
        delta[k], new_m[k], new_v[k] = a, b, cc
    grads["w_ada"], delta["w_ada"], new_m["w_ada"], new_v["w_ada"] = ada_update(
        c_all, dada_cols, W["w_ada"], M["w_ada"], V["w_ada"])

    owns[0], recvs[0] = exchange_wait("exchange_wait_0", *in_flight[1:], dl)
    dev_s = jnp.reshape(dev, (1,)).astype(jnp.int32)

    def big(name, idx):
        shp = W[name].shape
        flat = (-1,) + shp[-2:]
        outs = sum_update(dev_s, [recvs[l][i] for l in range(L) for i in idx], [owns[l][i] for l in range(L) for i in idx],
                          W[name].reshape(flat), M[name].reshape(flat), V[name].reshape(flat))
        grads[name], delta[name], new_m[name], new_v[name] = [o.reshape(shp) for o in outs]

    big("w_ff_in", (0, 1))
    big("w_ff_out", (2, 3))
    for i, name in enumerate(("w_in", "w_conv_out", "w_glu", "w_pool_out", "w_sb_out", "w_out")):
        big(name, (4 + i,))

    return (loss, dx[None], *[grads[k] for k in WEIGHT_NAMES], *[delta[k] for k in WEIGHT_NAMES],
            *[new_m[k] for k in WEIGHT_NAMES], *[new_v[k] for k in WEIGHT_NAMES])


def kernel(x, c, w_ada, b_ada, g_pre, g_post, w_ff_in, w_ff_out, w_in, conv_w, w_conv_out, lam_re, lam_im, log_dt, ssm_b_re, ssm_b_im, ssm_c_re, ssm_c_im, ssm_d, w_glu, w_pool, pool_scale, w_pool_out, w_sb_out, w_out, loss_target, m_w_ada, m_b_ada, m_g_pre, m_g_post, m_w_ff_in, m_w_ff_out, m_w_in, m_conv_w, m_w_conv_out, m_lam_re, m_lam_im, m_log_dt, m_ssm_b_re, m_ssm_b_im, m_ssm_c_re, m_ssm_c_im, m_ssm_d, m_w_glu, m_w_pool, m_pool_scale, m_w_pool_out, m_w_sb_out, m_w_out, v_w_ada, v_b_ada, v_g_pre, v_g_post, v_w_ff_in, v_w_ff_out, v_w_in, v_conv_w, v_w_conv_out, v_lam_re, v_lam_im, v_log_dt, v_ssm_b_re, v_ssm_b_im, v_ssm_c_re, v_ssm_c_im, v_ssm_d, v_w_glu, v_w_pool, v_pool_scale, v_w_pool_out, v_w_sb_out, v_w_out):
    w = (w_ada, b_ada, g_pre, g_post, w_ff_in, w_ff_out, w_in, conv_w, w_conv_out, lam_re, lam_im, log_dt, ssm_b_re, ssm_b_im, ssm_c_re, ssm_c_im, ssm_d, w_glu, w_pool, pool_scale, w_pool_out, w_sb_out, w_out)
    m = (m_w_ada, m_b_ada, m_g_pre, m_g_post, m_w_ff_in, m_w_ff_out, m_w_in, m_conv_w, m_w_conv_out, m_lam_re, m_lam_im, m_log_dt, m_ssm_b_re, m_ssm_b_im, m_ssm_c_re, m_ssm_c_im, m_ssm_d, m_w_glu, m_w_pool, m_pool_scale, m_w_pool_out, m_w_sb_out, m_w_out)
    v = (v_w_ada, v_b_ada, v_g_pre, v_g_post, v_w_ff_in, v_w_ff_out, v_w_in, v_conv_w, v_w_conv_out, v_lam_re, v_lam_im, v_log_dt, v_ssm_b_re, v_ssm_b_im, v_ssm_c_re, v_ssm_c_im, v_ssm_d, v_w_glu, v_w_pool, v_pool_scale, v_w_pool_out, v_w_sb_out, v_w_out)
    return _step(x, c, loss_target, dict(zip(WEIGHT_NAMES, w)), dict(zip(WEIGHT_NAMES, m)), dict(zip(WEIGHT_NAMES, v)))
```

```python
import functools

import jax
import jax.numpy as jnp
from jax import lax
from jax.experimental import pallas as pl
from jax.experimental.pallas import tpu as pltpu

F32 = jnp.float32
BF16 = jnp.bfloat16

N_DEV = 8
D_MODEL = 1024
D_FF = 2816
FF_BLK = D_FF // 4
N_SUB = 3
MIX_W = 256
IN_COLS = 6144
IN_BLK = IN_COLS // N_DEV
GATE_OFF = 2048
SSM_GROUPS, SSM_GROUP, SSM_STATE = 16, 16, 64
SSM_W = SSM_GROUPS * SSM_STATE
POOL_WINDOWS = (2, 4, 8, 16)
SB_HEAD = 64
EPS = 1e-6
DT_LAMBDA_RE_MAX = -1e-4
ADAM_LR, ADAM_B1, ADAM_B2, ADAM_EPS, ADAM_WD, ADAM_STEP = 0.001, 0.9, 0.999, 1e-08, 0.01, 10

VMEM_LIMIT = 56 * 1024 * 1024

PV_GPRE, PV_SHIFT, PV_SCALE, PV_GPOST, PV_GATE = 0, 1, 2, 3, 4


def _cparams(sem):
    return pltpu.CompilerParams(dimension_semantics=sem, vmem_limit_bytes=VMEM_LIMIT)


def _dot(a, b):
    return jnp.dot(a, b, preferred_element_type=F32)


def _dot_nt(a, b):
    return lax.dot_general(a, b, (((1,), (1,)), ((), ())), preferred_element_type=F32)


def _dot_tn(a, b):
    return lax.dot_general(a, b, (((0,), (0,)), ((), ())), preferred_element_type=F32)


def _rms(x):
    r = lax.rsqrt(jnp.mean(x * x, axis=-1, keepdims=True) + EPS)
    return x * r, r


def _rms_bwd(dn, n, r):
    return r * (dn - n * jnp.mean(dn * n, axis=-1, keepdims=True))


def _sigmoid(x):
    return 1.0 / (1.0 + jnp.exp(-x))


def _colsum(x):
    return jnp.sum(x, axis=0, keepdims=True)


def _prenorm(x, pv_ref):
    n, r = _rms(x)
    hn = n * pv_ref[PV_GPRE:PV_GPRE + 1, :]
    h = hn * (1.0 + pv_ref[PV_SCALE:PV_SCALE + 1, :]) + pv_ref[PV_SHIFT:PV_SHIFT + 1, :]
    return h, n, r, hn


def _prenorm_bwd(dh, dxn, x, pv_ref, pg_ref):
    _, n, r, hn = _prenorm(x, pv_ref)
    pg_ref[PV_SHIFT:PV_SHIFT + 1, :] += _colsum(dh)
    pg_ref[PV_SCALE:PV_SCALE + 1, :] += _colsum(dh * hn)
    dhn = dh * (1.0 + pv_ref[PV_SCALE:PV_SCALE + 1, :])
    pg_ref[PV_GPRE:PV_GPRE + 1, :] += _colsum(dhn * n)
    dn = dhn * pv_ref[PV_GPRE:PV_GPRE + 1, :]
    return dxn + _rms_bwd(dn, n, r)


def _postnorm_res(x, f, pv_ref, coef):
    nf, _ = _rms(f)
    return x + (coef * (1.0 + pv_ref[PV_GATE:PV_GATE + 1, :])) * (nf * pv_ref[PV_GPOST:PV_GPOST + 1, :])


def _postnorm_bwd(dxn, f, pv_ref, pg_ref, coef):
    nf, rf = _rms(f)
    g_post = pv_ref[PV_GPOST:PV_GPOST + 1, :]
    pg_ref[PV_GATE:PV_GATE + 1, :] += _colsum(dxn * (nf * g_post)) * coef
    dnfg = dxn * (coef * (1.0 + pv_ref[PV_GATE:PV_GATE + 1, :]))
    pg_ref[PV_GPOST:PV_GPOST + 1, :] += _colsum(dnfg * nf)
    return _rms_bwd(dnfg * g_post, nf, rf)


def ffn_fwd(lk, x, pv, wg_in, wg_out, tm=512):
    T, D = x.shape
    nj = 4

    def body(lk_ref, x_ref, pv_ref, win_ref, wout_ref, ab_ref, f_ref, xn_ref, h_sc, acc):
        j = pl.program_id(1)

        @pl.when(j == 0)
        def _():
            h, _, _, _ = _prenorm(x_ref[...], pv_ref)
            h_sc[...] = h.astype(BF16)
            acc[...] = jnp.zeros_like(acc)

        h = h_sc[...]
        a = _dot(h, win_ref[0])
        b = _dot(h, win_ref[1])
        ab_ref[0] = a.astype(BF16)
        ab_ref[1] = b.astype(BF16)
        act = (a * _sigmoid(a) * b).astype(BF16)
        acc[...] += _dot(act, wout_ref[...])

        @pl.when(j == nj - 1)
        def _():
            f = acc[...]
            f_ref[...] = f
            xn_ref[...] = _postnorm_res(x_ref[...], f, pv_ref, 0.5)

    grid_spec = pltpu.PrefetchScalarGridSpec(
        num_scalar_prefetch=1, grid=(T // tm, nj),
        in_specs=[
            pl.BlockSpec((tm, D), lambda i, j, lk: (i, 0)),
            pl.BlockSpec((8, D), lambda i, j, lk: (0, 0)),
            pl.BlockSpec((None, None, 2, None, D, FF_BLK), lambda i, j, lk: (lk[0], lk[1], 0, j, 0, 0)),
            pl.BlockSpec((None, None, None, FF_BLK, D), lambda i, j, lk: (lk[0], lk[1], j, 0, 0)),
        ],
        out_specs=[
            pl.BlockSpec((2, None, tm, FF_BLK), lambda i, j, lk: (0, j, i, 0)),
            pl.BlockSpec((tm, D), lambda i, j, lk: (i, 0)),
            pl.BlockSpec((tm, D), lambda i, j, lk: (i, 0)),
        ],
        scratch_shapes=[pltpu.VMEM((tm, D), BF16), pltpu.VMEM((tm, D), F32)],
    )
    return pl.pallas_call(
        body, name="ffn_fwd", grid_spec=grid_spec,
        out_shape=[jax.ShapeDtypeStruct((2, nj, T, FF_BLK), BF16),
                   jax.ShapeDtypeStruct((T, D), F32), jax.ShapeDtypeStruct((T, D), F32)],
        compiler_params=_cparams(("arbitrary", "arbitrary")),
    )(lk, x, pv, wg_in, wg_out)


def ffn_bwd_act(lk, dxn, x, f, pv, ab, wg_in, wg_out, tm=512):
    T, D = x.shape
    nj = 4

    def body(lk_ref, dxn_ref, x_ref, f_ref, pv_ref, ab_ref, win_ref, wout_ref,
             dab_ref, h_ref, df_ref, dx_ref, pg_ref, dacc):
        i, j = pl.program_id(0), pl.program_id(1)

        @pl.when((i == 0) & (j == 0))
        def _():
            pg_ref[...] = jnp.zeros_like(pg_ref)

        @pl.when(j == 0)
        def _():
            df = _postnorm_bwd(dxn_ref[...], f_ref[...], pv_ref, pg_ref, 0.5)
            df_ref[...] = df.astype(BF16)
            h, _, _, _ = _prenorm(x_ref[...], pv_ref)
            h_ref[...] = h.astype(BF16)
            dacc[...] = jnp.zeros_like(dacc)

        dact = _dot_nt(df_ref[...], wout_ref[...])
        a = ab_ref[0].astype(F32)
        b = ab_ref[1].astype(F32)
        sig = _sigmoid(a)
        s = a * sig
        da = (dact * b * (sig * (1.0 + a * (1.0 - sig)))).astype(BF16)
        db = (dact * s).astype(BF16)
        dab_ref[0] = da
        dab_ref[1] = db
        dacc[...] += _dot_nt(da, win_ref[0]) + _dot_nt(db, win_ref[1])

        @pl.when(j == nj - 1)
        def _():
            dx_ref[...] = _prenorm_bwd(dacc[...], dxn_ref[...], x_ref[...], pv_ref, pg_ref)

    tile = pl.BlockSpec((tm, D), lambda i, j, lk: (i, 0))
    grid_spec = pltpu.PrefetchScalarGridSpec(
        num_scalar_prefetch=1, grid=(T // tm, nj),
        in_specs=[
            tile, tile, tile,
            pl.BlockSpec((8, D), lambda i, j, lk: (0, 0)),
            pl.BlockSpec((2, None, tm, FF_BLK), lambda i, j, lk: (0, j, i, 0)),
            pl.BlockSpec((None, None, 2, None, D, FF_BLK), lambda i, j, lk: (lk[0], lk[1], 0, j, 0, 0)),
            pl.BlockSpec((None, None, None, FF_BLK, D), lambda i, j, lk: (lk[0], lk[1], j, 0, 0)),
        ],
        out_specs=[
            pl.BlockSpec((2, None, tm, FF_BLK), lambda i, j, lk: (0, j, i, 0)),
            tile, tile, tile,
            pl.BlockSpec((8, D), lambda i, j, lk: (0, 0)),
        ],
        scratch_shapes=[pltpu.VMEM((tm, D), F32)],
    )
    return pl.pallas_call(
        body, name="ffn_bwd_act", grid_spec=grid_spec,
        out_shape=[jax.ShapeDtypeStruct((2, nj, T, FF_BLK), BF16),
                   jax.ShapeDtypeStruct((T, D), BF16), jax.ShapeDtypeStruct((T, D), BF16),
                   jax.ShapeDtypeStruct((T, D), F32), jax.ShapeDtypeStruct((8, D), F32)],
        compiler_params=_cparams(("arbitrary", "arbitrary")),
    )(lk, dxn, x, f, pv, ab, wg_in, wg_out)


def ffn_bwd_w(h, df, ab, dab, tm=512):
    T, D = h.shape
    nj, ni = 4, T // tm

    def body(h_ref, df_ref, ab_ref, dab_ref, gin_ref, gout_ref, acc_in, acc_out):
        i = pl.program_id(1)

        @pl.when(i == 0)
        def _():
            acc_in[...] = jnp.zeros_like(acc_in)
            acc_out[...] = jnp.zeros_like(acc_out)

        h = h_ref[...]
        acc_in[0] += _dot_tn(h, dab_ref[0])
        acc_in[1] += _dot_tn(h, dab_ref[1])
        a = ab_ref[0].astype(F32)
        b = ab_ref[1].astype(F32)
        act = (a * _sigmoid(a) * b).astype(BF16)
        acc_out[...] += _dot_tn(act, df_ref[...])

        @pl.when(i == ni - 1)
        def _():
            gin_ref[...] = acc_in[...].astype(BF16)
            gout_ref[...] = acc_out[...].astype(BF16)

    tile = pl.BlockSpec((tm, D), lambda j, i: (i, 0))
    blk = pl.BlockSpec((2, None, tm, FF_BLK), lambda j, i: (0, j, i, 0))
    return pl.pallas_call(
        body, name="ffn_bwd_w", grid=(nj, ni),
        in_specs=[tile, tile, blk, blk],
        out_specs=[pl.BlockSpec((2, None, D, FF_BLK), lambda j, i: (0, j, 0, 0)),
                   pl.BlockSpec((None, FF_BLK, D), lambda j, i: (j, 0, 0))],
        out_shape=[jax.ShapeDtypeStruct((2, nj, D, FF_BLK), BF16),
                   jax.ShapeDtypeStruct((nj, FF_BLK, D), BF16)],
        scratch_shapes=[pltpu.VMEM((2, D, FF_BLK), F32), pltpu.VMEM((FF_BLK, D), F32)],
        compiler_params=_cparams(("arbitrary", "arbitrary")),
    )(h, df, ab, dab)


def mix_in_fwd(l, x, pv, wg, tm=512):
    T, D = x.shape

    def body(l_ref, x_ref, pv_ref, w_ref, p_ref, h_sc):
        @pl.when(pl.program_id(1) == 0)
        def _():
            h, _, _, _ = _prenorm(x_ref[...], pv_ref)
            h_sc[...] = h.astype(BF16)

        p_ref[...] = _dot(h_sc[...], w_ref[...])

    grid_spec = pltpu.PrefetchScalarGridSpec(
        num_scalar_prefetch=1, grid=(T // tm, N_DEV),
        in_specs=[pl.BlockSpec((tm, D), lambda i, j, l: (i, 0)),
                  pl.BlockSpec((8, D), lambda i, j, l: (0, 0)),
                  pl.BlockSpec((None, None, D, IN_BLK), lambda i, j, l: (l[0], j, 0, 0))],
        out_specs=pl.BlockSpec((tm, IN_BLK), lambda i, j, l: (i, j)),
        scratch_shapes=[pltpu.VMEM((tm, D), BF16)],
    )
    return pl.pallas_call(
        body, name="mix_in_fwd", grid_spec=grid_spec,
        out_shape=jax.ShapeDtypeStruct((T, IN_COLS), F32),
        compiler_params=_cparams(("arbitrary", "arbitrary")),
    )(l, x, pv, wg)


def mix_in_bwd_act(l, dp, dxn, x, pv, wg, tm=512):
    T, D = x.shape

    def body(l_ref, dp_ref, dxn_ref, x_ref, pv_ref, w_ref, dx_ref, h_ref, pg_ref, dacc):
        i, j = pl.program_id(0), pl.program_id(1)

        @pl.when((i == 0) & (j == 0))
        def _():
            pg_ref[...] = jnp.zeros_like(pg_ref)

        @pl.when(j == 0)
        def _():
            dacc[...] = jnp.zeros_like(dacc)

        dacc[...] += _dot_nt(dp_ref[...], w_ref[...])

        @pl.when(j == N_DEV - 1)
        def _():
            h, _, _, _ = _prenorm(x_ref[...], pv_ref)
            h_ref[...] = h.astype(BF16)
            dx_ref[...] = _prenorm_bwd(dacc[...], dxn_ref[...], x_ref[...], pv_ref, pg_ref)

    tile = pl.BlockSpec((tm, D), lambda i, j, l: (i, 0))
    grid_spec = pltpu.PrefetchScalarGridSpec(
        num_scalar_prefetch=1, grid=(T // tm, N_DEV),
        in_specs=[pl.BlockSpec((tm, IN_BLK), lambda i, j, l: (i, j)), tile, tile,
                  pl.BlockSpec((8, D), lambda i, j, l: (0, 0)),
                  pl.BlockSpec((None, None, D, IN_BLK), lambda i, j, l: (l[0], j, 0, 0))],
        out_specs=[tile, tile, pl.BlockSpec((8, D), lambda i, j, l: (0, 0))],
        scratch_shapes=[pltpu.VMEM((tm, D), F32)],
    )
    return pl.pallas_call(
        body, name="mix_in_bwd_act", grid_spec=grid_spec,
        out_shape=[jax.ShapeDtypeStruct((T, D), F32), jax.ShapeDtypeStruct((T, D), BF16),
                   jax.ShapeDtypeStruct((8, D), F32)],
        compiler_params=_cparams(("arbitrary", "arbitrary")),
    )(l, dp, dxn, x, pv, wg)


def matmul_tn(a, b, tn, tm=512):
    T, M = a.shape
    N = b.shape[1]
    ni = T // tm

    def body(a_ref, b_ref, o_ref, acc):
        i = pl.program_id(1)

        @pl.when(i == 0)
        def _():
            acc[...] = jnp.zeros_like(acc)

        acc[...] += _dot_tn(a_ref[...], b_ref[...])

        @pl.when(i == ni - 1)
        def _():
            o_ref[...] = acc[...].astype(o_ref.dtype)

    return pl.pallas_call(
        body, name="matmul_tn", grid=(N // tn, ni),
        in_specs=[pl.BlockSpec((tm, M), lambda j, i: (i, 0)), pl.BlockSpec((tm, tn), lambda j, i: (i, j))],
        out_specs=pl.BlockSpec((None, M, tn), lambda j, i: (j, 0, 0)),
        out_shape=jax.ShapeDtypeStruct((N // tn, M, tn), BF16),
        scratch_shapes=[pltpu.VMEM((M, tn), F32)],
        compiler_params=_cparams(("arbitrary", "arbitrary")),
    )(a, b)


SEQ_CHUNK = 256
HALO = 16


def _shift_down(ext, d):
    return pltpu.roll(ext, d, 0)


def _shift_up(ext, d):
    return pltpu.roll(ext, ext.shape[0] - d, 0)


def _rows_with_lead(load, c, width):
    t0 = c * SEQ_CHUNK
    if c == 0:
        return jnp.concatenate([jnp.zeros((HALO, width), F32), load(0, SEQ_CHUNK)], axis=0)
    return load(t0 - HALO, SEQ_CHUNK + HALO)


def _rows_with_tail(load, c, n_chunks, width):
    t0 = c * SEQ_CHUNK
    if c == n_chunks - 1:
        return jnp.concatenate([load(t0, SEQ_CHUNK), jnp.zeros((HALO, width), F32)], axis=0)
    return load(t0, SEQ_CHUNK + HALO)


def conv_fwd(l, p, conv_w):
    T = p.shape[0]
    W = MIX_W
    nC = T // SEQ_CHUNK

    def body(l_ref, p_ref, w_ref, za_ref):
        w0, w1, w2 = w_ref[0:1, :], w_ref[1:2, :], w_ref[2:3, :]
        for c in range(nC):
            ext = _rows_with_lead(lambda s, n: p_ref[s:s + n, W:2 * W] * p_ref[s:s + n, 2 * W:3 * W], c, W)
            y = w2 * ext + w1 * _shift_down(ext, 1) + w0 * _shift_down(ext, 2)
            t0 = c * SEQ_CHUNK
            za_ref[t0:t0 + SEQ_CHUNK, :] = (p_ref[t0:t0 + SEQ_CHUNK, 0:W] * y[HALO:]).astype(BF16)

    grid_spec = pltpu.PrefetchScalarGridSpec(
        num_scalar_prefetch=1, grid=(1,),
        in_specs=[pl.BlockSpec((T, 3 * W), lambda i, l: (0, 0)),
                  pl.BlockSpec((None, 8, W), lambda i, l: (l[0], 0, 0))],
        out_specs=pl.BlockSpec((T, W), lambda i, l: (0, 0)),
    )
    return pl.pallas_call(
        body, name="conv_fwd", grid_spec=grid_spec,
        out_shape=jax.ShapeDtypeStruct((T, W), BF16),
        compiler_params=_cparams(("arbitrary",)),
    )(l, p, conv_w)


def conv_bwd(l, p, dza, conv_w):
    T = p.shape[0]
    W = MIX_W
    nC = T // SEQ_CHUNK

    def body(l_ref, p_ref, dza_ref, w_ref, dp_ref, dw_ref):
        w0, w1, w2 = w_ref[0:1, :], w_ref[1:2, :], w_ref[2:3, :]
        dw = [jnp.zeros((1, W), F32) for _ in range(3)]
        for c in range(nC):
            t0 = c * SEQ_CHUNK
            ext = _rows_with_lead(lambda s, n: p_ref[s:s + n, W:2 * W] * p_ref[s:s + n, 2 * W:3 * W], c, W)
            u1, u2 = _shift_down(ext, 1)[HALO:], _shift_down(ext, 2)[HALO:]
            u0 = ext[HALO:]
            y = w2 * u0 + w1 * u1 + w0 * u2
            dza_c = dza_ref[t0:t0 + SEQ_CHUNK, :]
            dy = dza_c * p_ref[t0:t0 + SEQ_CHUNK, 0:W]
            dw[0] += _colsum(dy * u2)
            dw[1] += _colsum(dy * u1)
            dw[2] += _colsum(dy * u0)
            dye = _rows_with_tail(lambda s, n: dza_ref[s:s + n, :] * p_ref[s:s + n, 0:W], c, nC, W)
            du = (w2 * dye + w1 * _shift_up(dye, 1) + w0 * _shift_up(dye, 2))[:SEQ_CHUNK]
            dp_ref[t0:t0 + SEQ_CHUNK, 0:W] = (dza_c * y).astype(BF16)
            dp_ref[t0:t0 + SEQ_CHUNK, W:2 * W] = (du * p_ref[t0:t0 + SEQ_CHUNK, 2 * W:3 * W]).astype(BF16)
            dp_ref[t0:t0 + SEQ_CHUNK, 2 * W:3 * W] = (du * p_ref[t0:t0 + SEQ_CHUNK, W:2 * W]).astype(BF16)
        dw_ref[...] = jnp.concatenate(dw + [jnp.zeros((5, W), F32)], axis=0)

    grid_spec = pltpu.PrefetchScalarGridSpec(
        num_scalar_prefetch=1, grid=(1,),
        in_specs=[pl.BlockSpec((T, 3 * W), lambda i, l: (0, 0)),
                  pl.BlockSpec((T, W), lambda i, l: (0, 0)),
                  pl.BlockSpec((None, 8, W), lambda i, l: (l[0], 0, 0))],
        out_specs=[pl.BlockSpec((T, 3 * W), lambda i, l: (0, 0)), pl.BlockSpec((8, W), lambda i, l: (0, 0))],
    )
    return pl.pallas_call(
        body, name="conv_bwd", grid_spec=grid_spec,
        out_shape=[jax.ShapeDtypeStruct((T, 3 * W), BF16), jax.ShapeDtypeStruct((8, W), F32)],
        compiler_params=_cparams(("arbitrary",)),
    )(l, p, dza, conv_w)


def _pool_consts(rows, t0):
    lane = lax.broadcasted_iota(jnp.int32, (rows, MIX_W), 1)
    t = lax.broadcasted_iota(jnp.int32, (rows, MIX_W), 0) + t0
    win = jnp.where(lane < 64, 2, jnp.where(lane < 128, 4, jnp.where(lane < 192, 8, 16)))
    inv = 1.0 / jnp.minimum(t + 1, win).astype(F32)
    return lane, inv


def _pick_window(lane, s2, s4, s8, s16):
    return jnp.where(lane < 64, s2, jnp.where(lane < 128, s4, jnp.where(lane < 192, s8, s16)))


def _pooled_chunk(u_ref, c):
    ext = _rows_with_lead(lambda s, n: u_ref[s:s + n, :], c, MIX_W)
    s2 = ext + _shift_down(ext, 1)
    s4 = s2 + _shift_down(s2, 2)
    s8 = s4 + _shift_down(s4, 4)
    s16 = s8 + _shift_down(s8, 8)
    lane, inv = _pool_consts(SEQ_CHUNK, c * SEQ_CHUNK)
    return _pick_window(lane, s2[HALO:], s4[HALO:], s8[HALO:], s16[HALO:]) * inv - ext[HALO:]


def pool_fwd(l, p, w_bd, scale):
    T = p.shape[0]
    W = MIX_W
    nC = T // SEQ_CHUNK

    def body(l_ref, u_ref, w_ref, sc_ref, z_ref):
        for c in range(nC):
            pooled = _pooled_chunk(u_ref, c)
            mixed = _dot(pooled.astype(BF16), w_ref[...])
            z_ref[c * SEQ_CHUNK:(c + 1) * SEQ_CHUNK, :] = (mixed * sc_ref[0:1, :]).astype(BF16)

    grid_spec = pltpu.PrefetchScalarGridSpec(
        num_scalar_prefetch=1, grid=(1,),
        in_specs=[pl.BlockSpec((T, W), lambda i, l: (0, 4)),
                  pl.BlockSpec((None, W, W), lambda i, l: (l[0], 0, 0)),
                  pl.BlockSpec((None, 8, W), lambda i, l: (l[0], 0, 0))],
        out_specs=pl.BlockSpec((T, W), lambda i, l: (0, 0)),
    )
    return pl.pallas_call(
        body, name="pool_fwd", grid_spec=grid_spec,
        out_shape=jax.ShapeDtypeStruct((T, W), BF16),
        compiler_params=_cparams(("arbitrary",)),
    )(l, p, w_bd, scale)


def pool_bwd(l, p, dz, w_bd, scale):
    T = p.shape[0]
    W = MIX_W
    nC = T // SEQ_CHUNK

    def body(l_ref, u_ref, dz_ref, w_ref, sc_ref, du_ref, dw_ref, dsc_ref, e_sc, dpl_sc):
        dw = jnp.zeros((W, W), F32)
        dsc = jnp.zeros((1, W), F32)
        for c in range(nC):
            t0 = c * SEQ_CHUNK
            pooled = _pooled_chunk(u_ref, c).astype(BF16)
            mixed = _dot(pooled, w_ref[...])
            dz_c = dz_ref[t0:t0 + SEQ_CHUNK, :]
            dsc += _colsum(dz_c * mixed)
            dmixed = (dz_c * sc_ref[0:1, :]).astype(BF16)
            dw += _dot_tn(pooled, dmixed)
            dpooled = _dot_nt(dmixed, w_ref[...])
            _, inv = _pool_consts(SEQ_CHUNK, t0)
            dpl_sc[t0:t0 + SEQ_CHUNK, :] = dpooled
            e_sc[t0:t0 + SEQ_CHUNK, :] = dpooled * inv
        for c in range(nC):
            t0 = c * SEQ_CHUNK
            ext = _rows_with_tail(lambda s, n: e_sc[s:s + n, :], c, nC, W)
            s2 = ext + _shift_up(ext, 1)
            s4 = s2 + _shift_up(s2, 2)
            s8 = s4 + _shift_up(s4, 4)
            s16 = s8 + _shift_up(s8, 8)
            lane, _ = _pool_consts(SEQ_CHUNK, t0)
            n = SEQ_CHUNK
            du = _pick_window(lane, s2[:n], s4[:n], s8[:n], s16[:n]) - dpl_sc[t0:t0 + SEQ_CHUNK, :]
            du_ref[t0:t0 + SEQ_CHUNK, :] = du.astype(BF16)
        dw_ref[...] = dw
        dsc_ref[...] = jnp.concatenate([dsc, jnp.zeros((7, W), F32)], axis=0)

    grid_spec = pltpu.PrefetchScalarGridSpec(
        num_scalar_prefetch=1, grid=(1,),
        in_specs=[pl.BlockSpec((T, W), lambda i, l: (0, 4)),
                  pl.BlockSpec((T, W), lambda i, l: (0, 0)),
                  pl.BlockSpec((None, W, W), lambda i, l: (l[0], 0, 0)),
                  pl.BlockSpec((None, 8, W), lambda i, l: (l[0], 0, 0))],
        out_specs=[pl.BlockSpec((T, W), lambda i, l: (0, 0)), pl.BlockSpec((W, W), lambda i, l: (0, 0)),
                   pl.BlockSpec((8, W), lambda i, l: (0, 0))],
        scratch_shapes=[pltpu.VMEM((T, W), F32), pltpu.VMEM((T, W), F32)],
    )
    return pl.pallas_call(
        body, name="pool_bwd", grid_spec=grid_spec,
        out_shape=[jax.ShapeDtypeStruct((T, W), BF16), jax.ShapeDtypeStruct((W, W), F32),
                   jax.ShapeDtypeStruct((8, W), F32)],
        compiler_params=_cparams(("arbitrary",)),
    )(l, p, dz, w_bd, scale)


def _s5_disc(lre, lim, ldt):
    lr = jnp.minimum(lre, DT_LAMBDA_RE_MAX)
    dt = jnp.exp(ldt)
    mag = jnp.exp(lr * dt)
    a_re = mag * jnp.cos(lim * dt)
    a_im = mag * jnp.sin(lim * dt)
    den = lr * lr + lim * lim
    nr = a_re - 1.0
    return a_re, a_im, (nr * lr + a_im * lim) / den, (a_im * lr - nr * lim) / den


def _bd_mask(shape, row_blk, col_blk):
    r = lax.broadcasted_iota(jnp.int32, shape, 0) >> (row_blk.bit_length() - 1)
    c = lax.broadcasted_iota(jnp.int32, shape, 1) >> (col_blk.bit_length() - 1)
    return r == c


def s5_params(lam, b_t, c_t):
    L = lam.shape[0]

    def body(lam_ref, b_ref, c_ref, a_ref, bbd_ref, cbd_ref):
        a_re, a_im, f_re, f_im = _s5_disc(lam_ref[0:1, :], lam_ref[1:2, :], lam_ref[2:3, :])
        a_ref[...] = jnp.concatenate([a_re, a_im, jnp.zeros((6, SSM_W), F32)], axis=0)
        mb = _bd_mask((MIX_W, SSM_W), SSM_GROUP, SSM_STATE)
        bbd_ref[0] = jnp.where(mb, f_re * b_ref[0] - f_im * b_ref[1], 0.0).astype(BF16)
        bbd_ref[1] = jnp.where(mb, f_re * b_ref[1] + f_im * b_ref[0], 0.0).astype(BF16)
        mc = _bd_mask((SSM_W, MIX_W), SSM_STATE, SSM_GROUP)
        cbd_ref[0] = jnp.where(mc, c_ref[0], 0.0).astype(BF16)
        cbd_ref[1] = jnp.where(mc, c_ref[1], 0.0).astype(BF16)

    return pl.pallas_call(
        body, name="s5_params", grid=(L,),
        in_specs=[pl.BlockSpec((None, 8, SSM_W), lambda l: (l, 0, 0)),
                  pl.BlockSpec((None, 2, MIX_W, SSM_W), lambda l: (l, 0, 0, 0)),
                  pl.BlockSpec((None, 2, SSM_W, MIX_W), lambda l: (l, 0, 0, 0))],
        out_specs=[pl.BlockSpec((None, 8, SSM_W), lambda l: (l, 0, 0)),
                   pl.BlockSpec((None, 2, MIX_W, SSM_W), lambda l: (l, 0, 0, 0)),
                   pl.BlockSpec((None, 2, SSM_W, MIX_W), lambda l: (l, 0, 0, 0))],
        out_shape=[jax.ShapeDtypeStruct((L, 8, SSM_W), F32),
                   jax.ShapeDtypeStruct((L, 2, MIX_W, SSM_W), BF16),
                   jax.ShapeDtypeStruct((L, 2, SSM_W, MIX_W), BF16)],
        compiler_params=_cparams(("arbitrary",)),
    )(lam, b_t, c_t)


def s5_params_bwd(lam, b_t, gb, gc, da):
    L = lam.shape[0]

    def body(lam_ref, b_ref, gb_ref, gc_ref, da_ref, dlam_ref, db_ref, dc_ref, dgrp_ref):
        lre, lim, ldt = lam_ref[0:1, :], lam_ref[1:2, :], lam_ref[2:3, :]
        (a_re, a_im, f_re, f_im), vjp = jax.vjp(_s5_disc, lre, lim, ldt)
        mb = _bd_mask((MIX_W, SSM_W), SSM_GROUP, SSM_STATE)
        gbr = jnp.where(mb, gb_ref[0], 0.0)
        gbi = jnp.where(mb, gb_ref[1], 0.0)
        df_re = _colsum(gbr * b_ref[0] + gbi * b_ref[1])
        df_im = _colsum(gbi * b_ref[0] - gbr * b_ref[1])
        db_ref[0] = f_re * gbr + f_im * gbi
        db_ref[1] = f_re * gbi - f_im * gbr
        mc = _bd_mask((SSM_W, MIX_W), SSM_STATE, SSM_GROUP)
        dc_ref[0] = jnp.where(mc, gc_ref[0], 0.0)
        dc_ref[1] = jnp.where(mc, gc_ref[1], 0.0)
        dlre, dlim, dldt = vjp((da_ref[0:1, :], da_ref[1:2, :], df_re, df_im))
        dl = jnp.concatenate([dlre, dlim, dldt, jnp.zeros((5, SSM_W), F32)], axis=0)
        dlam_ref[...] = dl
        grp = jnp.where(_bd_mask((SSM_W, 128), SSM_STATE, 1), 1.0, 0.0)
        dgrp_ref[...] = jnp.dot(dl, grp, preferred_element_type=F32, precision=lax.Precision.HIGHEST)

    vec = pl.BlockSpec((None, 8, SSM_W), lambda l: (l, 0, 0))
    bsp = pl.BlockSpec((None, 2, MIX_W, SSM_W), lambda l: (l, 0, 0, 0))
    csp = pl.BlockSpec((None, 2, SSM_W, MIX_W), lambda l: (l, 0, 0, 0))
    return pl.pallas_call(
        body, name="s5_params_bwd", grid=(L,),
        in_specs=[vec, bsp, bsp, csp, vec],
        out_specs=[vec, bsp, csp, pl.BlockSpec((None, 8, 128), lambda l: (l, 0, 0))],
        out_shape=[jax.ShapeDtypeStruct((L, 8, SSM_W), F32),
                   jax.ShapeDtypeStruct((L, 2, MIX_W, SSM_W), F32),
                   jax.ShapeDtypeStruct((L, 2, SSM_W, MIX_W), F32),
                   jax.ShapeDtypeStruct((L, 8, 128), F32)],
        compiler_params=_cparams(("arbitrary",)),
    )(lam, b_t, gb, gc, da)


def s5_bu(l, p, b_bd, tm=512):
    T = p.shape[0]

    def body(l_ref, u_ref, b_ref, bu_ref):
        u = u_ref[...].astype(BF16)
        bu_ref[0] = _dot(u, b_ref[0])
        bu_ref[1] = _dot(u, b_ref[1])

    grid_spec = pltpu.PrefetchScalarGridSpec(
        num_scalar_prefetch=1, grid=(T // tm,),
        in_specs=[pl.BlockSpec((tm, MIX_W), lambda i, l: (i, 3)),
                  pl.BlockSpec((None, 2, MIX_W, SSM_W), lambda i, l: (l[0], 0, 0, 0))],
        out_specs=pl.BlockSpec((2, tm, SSM_W), lambda i, l: (0, i, 0)),
    )
    return pl.pallas_call(
        body, name="s5_bu", grid_spec=grid_spec,
        out_shape=jax.ShapeDtypeStruct((2, T, SSM_W), F32),
        compiler_params=_cparams(("arbitrary",)),
    )(l, p, b_bd)


def s5_scan(l, avec, xs, reverse):
    T = xs.shape[1]
    CH = SEQ_CHUNK
    nC = T // CH
    LW = 128
    n_steps = CH.bit_length() - 1

    def body(l_ref, a_ref, x_ref, s_ref):
        ar = a_ref[0:1, :]
        ai = -a_ref[1:2, :] if reverse else a_ref[1:2, :]
        pows = [(ar, ai)]
        for _ in range(n_steps - 1):
            r, i = pows[-1]
            pows.append((r * r - i * i, 2.0 * r * i))
        row = lax.broadcasted_iota(jnp.int32, (CH, LW), 0)

        def local_scan(re, im):
            for k in range(n_steps):
                d = 1 << k
                pr, pi = pows[k]
                if reverse:
                    keep = row < CH - d
                    sr, si = _shift_up(re, d), _shift_up(im, d)
                else:
                    keep = row >= d
                    sr, si = _shift_down(re, d), _shift_down(im, d)
                sr = jnp.where(keep, sr, 0.0)
                si = jnp.where(keep, si, 0.0)
                re, im = re + pr * sr - pi * si, im + pr * si + pi * sr
            return re, im

        edge = CH - 1 if reverse else 0
        pw_re, pw_im = local_scan(jnp.where(row == edge, ar, 0.0), jnp.where(row == edge, ai, 0.0))
        last = 0 if reverse else CH - 1

        def chunk(c, carry):
            cr, ci = carry
            cc = nC - 1 - c if reverse else c
            t0 = pl.multiple_of(cc * CH, CH)
            re, im = local_scan(x_ref[0, pl.ds(t0, CH), :], x_ref[1, pl.ds(t0, CH), :])
            re2 = re + pw_re * cr - pw_im * ci
            im2 = im + pw_re * ci + pw_im * cr
            s_ref[0, pl.ds(t0, CH), :] = re2
            s_ref[1, pl.ds(t0, CH), :] = im2
            return re2[last:last + 1, :], im2[last:last + 1, :]

        lax.fori_loop(0, nC, chunk, (jnp.zeros((1, LW), F32), jnp.zeros((1, LW), F32)))

    grid_spec = pltpu.PrefetchScalarGridSpec(
        num_scalar_prefetch=1, grid=(SSM_W // LW,),
        in_specs=[pl.BlockSpec((None, 8, LW), lambda g, l: (l[0], 0, g)),
                  pl.BlockSpec((2, T, LW), lambda g, l: (0, 0, g))],
        out_specs=pl.BlockSpec((2, T, LW), lambda g, l: (0, 0, g)),
    )
    return pl.pallas_call(
        body, name="s5_scan_rev" if reverse else "s5_scan_fwd", grid_spec=grid_spec,
        out_shape=jax.ShapeDtypeStruct((2, T, SSM_W), F32),
        compiler_params=_cparams(("arbitrary",)),
    )(l, avec, xs)


_GELU_C = 0.7978845608028654
_GELU_K = 0.044715


def _s5_y(u, s_ref, c_ref, d_row):
    y = _dot(s_ref[0].astype(BF16), c_ref[0]) - _dot(s_ref[1].astype(BF16), c_ref[1])
    return y + d_row * u


def s5_out(l, p, s, c_bd, ssm_d, tm=512):
    T = p.shape[0]

    def body(l_ref, u_ref, s_ref, c_ref, d_ref, yg_ref):
        y = _s5_y(u_ref[...], s_ref, c_ref, d_ref[0:1, :])
        th = jnp.tanh(_GELU_C * (y + _GELU_K * y * y * y))
        yg_ref[...] = (0.5 * y * (1.0 + th)).astype(BF16)

    grid_spec = pltpu.PrefetchScalarGridSpec(
        num_scalar_prefetch=1, grid=(T // tm,),
        in_specs=[pl.BlockSpec((tm, MIX_W), lambda i, l: (i, 3)),
                  pl.BlockSpec((2, tm, SSM_W), lambda i, l: (0, i, 0)),
                  pl.BlockSpec((None, 2, SSM_W, MIX_W), lambda i, l: (l[0], 0, 0, 0)),
                  pl.BlockSpec((None, 8, MIX_W), lambda i, l: (l[0], 0, 0))],
        out_specs=pl.BlockSpec((tm, MIX_W), lambda i, l: (i, 0)),
    )
    return pl.pallas_call(
        body, name="s5_out", grid_spec=grid_spec,
        out_shape=jax.ShapeDtypeStruct((T, MIX_W), BF16),
        compiler_params=_cparams(("arbitrary",)),
    )(l, p, s, c_bd, ssm_d)


def s5_bwd_y(l, p, s, dyg, c_bd, ssm_d, tm=512):
    T = p.shape[0]

    def body(l_ref, u_ref, s_ref, dyg_ref, c_ref, d_ref, ds_ref, du_ref, gc_ref, dd_ref):
        @pl.when(pl.program_id(0) == 0)
        def _():
            gc_ref[...] = jnp.zeros_like(gc_ref)
            dd_ref[...] = jnp.zeros_like(dd_ref)

        u = u_ref[...]
        y = _s5_y(u, s_ref, c_ref, d_ref[0:1, :])
        inner = _GELU_C * (y + _GELU_K * y * y * y)
        th = jnp.tanh(inner)
        dgelu = 0.5 * (1.0 + th) + 0.5 * y * (1.0 - th * th) * (_GELU_C * (1.0 + 3.0 * _GELU_K * y * y))
        dy = dyg_ref[...] * dgelu
        dd_ref[0:1, :] += _colsum(dy * u)
        du_ref[...] = dy * d_ref[0:1, :]
        dyb = dy.astype(BF16)
        ds_ref[0] = _dot_nt(dyb, c_ref[0])
        ds_ref[1] = -_dot_nt(dyb, c_ref[1])
        gc_ref[0] += _dot_tn(s_ref[0].astype(BF16), dyb)
        gc_ref[1] -= _dot_tn(s_ref[1].astype(BF16), dyb)

    grid_spec = pltpu.PrefetchScalarGridSpec(
        num_scalar_prefetch=1, grid=(T // tm,),
        in_specs=[pl.BlockSpec((tm, MIX_W), lambda i, l: (i, 3)),
                  pl.BlockSpec((2, tm, SSM_W), lambda i, l: (0, i, 0)),
                  pl.BlockSpec((tm, MIX_W), lambda i, l: (i, 0)),
                  pl.BlockSpec((None, 2, SSM_W, MIX_W), lambda i, l: (l[0], 0, 0, 0)),
                  pl.BlockSpec((None, 8, MIX_W), lambda i, l: (l[0], 0, 0))],
        out_specs=[pl.BlockSpec((2, tm, SSM_W), lambda i, l: (0, i, 0)),
                   pl.BlockSpec((tm, MIX_W), lambda i, l: (i, 0)),
                   pl.BlockSpec((2, SSM_W, MIX_W), lambda i, l: (0, 0, 0)),
                   pl.BlockSpec((8, MIX_W), lambda i, l: (0, 0))],
    )
    return pl.pallas_call(
        body, name="s5_bwd_y", grid_spec=grid_spec,
        out_shape=[jax.ShapeDtypeStruct((2, T, SSM_W), F32), jax.ShapeDtypeStruct((T, MIX_W), F32),
                   jax.ShapeDtypeStruct((2, SSM_W, MIX_W), F32), jax.ShapeDtypeStruct((8, MIX_W), F32)],
        compiler_params=_cparams(("arbitrary",)),
    )(l, p, s, dyg, c_bd, ssm_d)


def s5_bwd_u(l, p, lam_s, du_skip, b_bd, tm=512):
    T = p.shape[0]

    def body(l_ref, u_ref, ls_ref, dus_ref, b_ref, du_ref, gb_ref):
        @pl.when(pl.program_id(0) == 0)
        def _():
            gb_ref[...] = jnp.zeros_like(gb_ref)

        u = u_ref[...].astype(BF16)
        lr = ls_ref[0].astype(BF16)
        li = ls_ref[1].astype(BF16)
        gb_ref[0] += _dot_tn(u, lr)
        gb_ref[1] += _dot_tn(u, li)
        du_ref[...] = (dus_ref[...] + _dot_nt(lr, b_ref[0]) + _dot_nt(li, b_ref[1])).astype(BF16)

    grid_spec = pltpu.PrefetchScalarGridSpec(
        num_scalar_prefetch=1, grid=(T // tm,),
        in_specs=[pl.BlockSpec((tm, MIX_W), lambda i, l: (i, 3)),
                  pl.BlockSpec((2, tm, SSM_W), lambda i, l: (0, i, 0)),
                  pl.BlockSpec((tm, MIX_W), lambda i, l: (i, 0)),
                  pl.BlockSpec((None, 2, MIX_W, SSM_W), lambda i, l: (l[0], 0, 0, 0))],
        out_specs=[pl.BlockSpec((tm, MIX_W), lambda i, l: (i, 0)),
                   pl.BlockSpec((2, MIX_W, SSM_W), lambda i, l: (0, 0, 0))],
    )
    return pl.pallas_call(
        body, name="s5_bwd_u", grid_spec=grid_spec,
        out_shape=[jax.ShapeDtypeStruct((T, MIX_W), BF16), jax.ShapeDtypeStruct((2, MIX_W, SSM_W), F32)],
        compiler_params=_cparams(("arbitrary",)),
    )(l, p, lam_s, du_skip, b_bd)


def s5_bwd_a(s, lam_s):
    T = s.shape[1]
    nC = T // SEQ_CHUNK
    LW = 128

    def body(s_ref, ls_ref, da_ref):
        dre = jnp.zeros((1, LW), F32)
        dim = jnp.zeros((1, LW), F32)
        for c in range(nC):
            t0 = c * SEQ_CHUNK
            sr = _shift_down(_rows_with_lead(lambda a, n: s_ref[0, a:a + n, :], c, LW), 1)[HALO:]
            si = _shift_down(_rows_with_lead(lambda a, n: s_ref[1, a:a + n, :], c, LW), 1)[HALO:]
            lr = ls_ref[0, t0:t0 + SEQ_CHUNK, :]
            li = ls_ref[1, t0:t0 + SEQ_CHUNK, :]
            dre += _colsum(sr * lr + si * li)
            dim += _colsum(sr * li - si * lr)
        da_ref[...] = jnp.concatenate([dre, dim, jnp.zeros((6, LW), F32)], axis=0)

    blk = pl.BlockSpec((2, T, LW), lambda g: (0, 0, g))
    return pl.pallas_call(
        body, name="s5_bwd_a", grid=(SSM_W // LW,),
        in_specs=[blk, blk],
        out_specs=pl.BlockSpec((8, LW), lambda g: (0, g)),
        out_shape=jax.ShapeDtypeStruct((8, SSM_W), F32),
        compiler_params=_cparams(("arbitrary",)),
    )(s, lam_s)


SB_BLK = 128
SB_SCALE = SB_HEAD ** -0.5


def _split_bf16(x):
    hi = x.astype(BF16)
    return hi, (x - hi.astype(F32)).astype(BF16)


def _dot_split(x, tri):
    hi, lo = _split_bf16(x)
    return _dot(hi, tri) + _dot(lo, tri)


def _sb_valid(r0, c0):
    row = lax.broadcasted_iota(jnp.int32, (SB_BLK, SB_BLK), 0) + r0
    col = lax.broadcasted_iota(jnp.int32, (SB_BLK, SB_BLK), 1) + c0
    return col < row


def _sb_logits(q, k_blk, valid):
    z = _dot_nt(q, k_blk)
    sp = jnp.log(1.0 + jnp.exp(-jnp.abs(z)))
    ls_pos = jnp.minimum(z, 0.0) - sp
    lk = jnp.where(valid, jnp.minimum(-z, 0.0) - sp, 0.0)
    return z, ls_pos, lk


def _sb_heads():
    lane = lax.broadcasted_iota(jnp.int32, (SB_BLK, SB_BLK), 1)
    masks = (lane < SB_HEAD, lane >= SB_HEAD)
    return [(slice(s * SB_BLK, (s + 1) * SB_BLK), masks[hh]) for s in range(MIX_W // SB_BLK) for hh in range(2)]


def _tri(lower):
    r = lax.broadcasted_iota(jnp.int32, (SB_BLK, SB_BLK), 0)
    c = lax.broadcasted_iota(jnp.int32, (SB_BLK, SB_BLK), 1)
    return jnp.where(r > c if lower else r < c, 1.0, 0.0).astype(BF16)


def sb_fwd(p):
    T = p.shape[0]
    W = MIX_W
    nB = T // SB_BLK

    def body(q_ref, k_ref, v_ref, o_ref, acc_sc):
        tri = _tri(True)
        heads = _sb_heads()

        def qblock(i, _):
            r0 = pl.multiple_of(i * SB_BLK, SB_BLK)
            qs = [(jnp.where(hm, q_ref[pl.ds(r0, SB_BLK), ls], 0.0) * SB_SCALE).astype(BF16) for ls, hm in heads]
            acc_sc[...] = jnp.zeros_like(acc_sc)

            def kblock(jj, runs):
                c0 = pl.multiple_of((i - jj) * SB_BLK, SB_BLK)
                valid = _sb_valid(r0, c0)
                new_runs = []
                for h, (ls, hm) in enumerate(heads):
                    kb = k_ref[pl.ds(c0, SB_BLK), ls].astype(BF16)
                    vb = jnp.where(hm, v_ref[pl.ds(c0, SB_BLK), ls], 0.0).astype(BF16)
                    _, ls_pos, lk = _sb_logits(qs[h], kb, valid)
                    logw = ls_pos + _dot_split(lk, tri) + runs[h]
                    a = jnp.where(valid, jnp.exp(logw), 0.0).astype(BF16)
                    acc_sc[:, ls] += _dot(a, vb)
                    new_runs.append(runs[h] + jnp.sum(lk, axis=1, keepdims=True))
                return tuple(new_runs)

            lax.fori_loop(0, i + 1, kblock, tuple(jnp.zeros((SB_BLK, 1), F32) for _ in heads))
            o_ref[pl.ds(r0, SB_BLK), :] = acc_sc[...].astype(BF16)
            return 0

        lax.fori_loop(0, nB, qblock, 0)

    return pl.pallas_call(
        body, name="sb_fwd", grid=(1,),
        in_specs=[pl.BlockSpec((T, W), lambda i: (0, 5)), pl.BlockSpec((T, W), lambda i: (0, 6)),
                  pl.BlockSpec((T, W), lambda i: (0, 7))],
        out_specs=pl.BlockSpec((T, W), lambda i: (0, 0)),
        out_shape=jax.ShapeDtypeStruct((T, W), BF16),
        scratch_shapes=[pltpu.VMEM((SB_BLK, W), F32)],
        compiler_params=_cparams(("arbitrary",)),
    )(p, p, p)


def sb_bwd(p, do):
    T = p.shape[0]
    W = MIX_W
    nB = T // SB_BLK

    def body(q_ref, k_ref, v_ref, do_ref, dqkv_ref, dq_sc, dk_sc, dv_sc, run_sc):
        tri_gt = _tri(True)
        tri_lt = _tri(False)
        heads = _sb_heads()
        nH = len(heads)
        dq_sc[...] = jnp.zeros_like(dq_sc)
        dk_sc[...] = jnp.zeros_like(dk_sc)
        dv_sc[...] = jnp.zeros_like(dv_sc)
        zcol = tuple(jnp.zeros((SB_BLK, 1), F32) for _ in heads)

        def qblock(i, _):
            r0 = pl.multiple_of(i * SB_BLK, SB_BLK)
            qs = [(jnp.where(hm, q_ref[pl.ds(r0, SB_BLK), ls], 0.0) * SB_SCALE).astype(BF16) for ls, hm in heads]
            dobs = [jnp.where(hm, do_ref[pl.ds(r0, SB_BLK), ls], 0.0).astype(BF16) for ls, hm in heads]

            def suffix(jj, runs):
                c0 = pl.multiple_of((i - jj) * SB_BLK, SB_BLK)
                valid = _sb_valid(r0, c0)
                new_runs = []
                for h, (ls, hm) in enumerate(heads):
                    kb = k_ref[pl.ds(c0, SB_BLK), ls].astype(BF16)
                    _, _, lk = _sb_logits(qs[h], kb, valid)
                    run_sc[h, pl.ds(c0, SB_BLK), :] = jnp.broadcast_to(runs[h], (SB_BLK, SB_BLK))
                    new_runs.append(runs[h] + jnp.sum(lk, axis=1, keepdims=True))
                return tuple(new_runs)

            lax.fori_loop(0, i + 1, suffix, zcol)

            def kblock(j, pres):
                c0 = pl.multiple_of(j * SB_BLK, SB_BLK)
                valid = _sb_valid(r0, c0)
                new_pres = []
                for h, (ls, hm) in enumerate(heads):
                    kf = k_ref[pl.ds(c0, SB_BLK), ls]
                    kb = kf.astype(BF16)
                    kbm = jnp.where(hm, kf, 0.0).astype(BF16)
                    vb = v_ref[pl.ds(c0, SB_BLK), ls].astype(BF16)
                    z, ls_pos, lk = _sb_logits(qs[h], kb, valid)
                    logw = ls_pos + _dot_split(lk, tri_gt) + run_sc[h, pl.ds(c0, SB_BLK), 0:1]
                    a = jnp.where(valid, jnp.exp(logw), 0.0)
                    dlw = _dot_nt(dobs[h], vb) * a
                    g = pres[h] + _dot_split(dlw, tri_lt)
                    sig = _sigmoid(z)
                    dz = jnp.where(valid, dlw * (1.0 - sig) - g * sig, 0.0).astype(BF16)
                    dk_sc[pl.ds(c0, SB_BLK), ls] += _dot_tn(dz, qs[h])
                    dv_sc[pl.ds(c0, SB_BLK), ls] += _dot_tn(a.astype(BF16), dobs[h])
                    dq_sc[pl.ds(r0, SB_BLK), ls] += _dot(dz, kbm)
                    new_pres.append(pres[h] + jnp.sum(dlw, axis=1, keepdims=True))
                return tuple(new_pres)

            lax.fori_loop(0, i + 1, kblock, zcol)
            return 0

        lax.fori_loop(0, nB, qblock, 0)
        dqkv_ref[:, 0:W] = (dq_sc[...] * SB_SCALE).astype(BF16)
        dqkv_ref[:, W:2 * W] = dk_sc[...].astype(BF16)
        dqkv_ref[:, 2 * W:3 * W] = dv_sc[...].astype(BF16)

    return pl.pallas_call(
        body, name="sb_bwd", grid=(1,),
        in_specs=[pl.BlockSpec((T, W), lambda i: (0, 5)), pl.BlockSpec((T, W), lambda i: (0, 6)),
                  pl.BlockSpec((T, W), lambda i: (0, 7)), pl.BlockSpec((T, W), lambda i: (0, 0))],
        out_specs=pl.BlockSpec((T, 3 * W), lambda i: (0, 0)),
        out_shape=jax.ShapeDtypeStruct((T, 3 * W), BF16),
        scratch_shapes=[pltpu.VMEM((T, W), F32), pltpu.VMEM((T, W), F32), pltpu.VMEM((T, W), F32),
                        pltpu.VMEM((W // SB_HEAD, T, SB_BLK), F32)],
        compiler_params=_cparams(("arbitrary",)),
    )(p, p, p, do)


def _dot_cols(a, w_ref):
    return jnp.concatenate([_dot(a, w_ref[j]) for j in range(N_DEV)], axis=1)


def _dot_cols_nt(dy, w_ref):
    n = w_ref.shape[2]
    out = _dot_nt(dy[:, 0:n], w_ref[0])
    for j in range(1, N_DEV):
        out += _dot_nt(dy[:, j * n:(j + 1) * n], w_ref[j])
    return out


def _acc_cols_tn(acc_ref, a, dy):
    n = acc_ref.shape[2]
    for j in range(N_DEV):
        acc_ref[j] += _dot_tn(a, dy[:, j * n:(j + 1) * n])


def _merge_branches(za_ref, yg_ref, z_ref, o_ref, gate_refs, wc_ref, wglu_ref, wp_ref, ws_ref):
    D = D_MODEL
    glu = _dot_cols(yg_ref[...], wglu_ref)
    glu_a, sg = glu[:, :D], _sigmoid(glu[:, D:])
    ys = [_dot_cols(za_ref[...], wc_ref), glu_a * sg, _dot_cols(z_ref[...], wp_ref), _dot_cols(o_ref[...], ws_ref)]
    gs = [_sigmoid(g[...]) for g in gate_refs]
    merged = gs[0] * ys[0] + gs[1] * ys[1] + gs[2] * ys[2] + gs[3] * ys[3]
    return ys, gs, glu_a, sg, merged


def _merge_specs(tm, D):
    W = MIX_W
    br = pl.BlockSpec((tm, W), lambda i, l: (i, 0))
    gates = [pl.BlockSpec((tm, D), functools.partial(lambda i, l, b: (i, 2 + b), b=b)) for b in range(4)]
    wsm = pl.BlockSpec((None, N_DEV, W, D // N_DEV), lambda i, l: (l[0], 0, 0, 0))
    weights = [wsm, pl.BlockSpec((None, N_DEV, W, 2 * D // N_DEV), lambda i, l: (l[0], 0, 0, 0)), wsm, wsm,
               pl.BlockSpec((None, D, D), lambda i, l: (l[0], 0, 0))]
    return [br] * 4 + gates, weights


def merge_fwd(l, p, za, yg, z, o, x, pv, wc, wglu, wp, ws, wo, tm=256):
    T, D = x.shape

    def body(l_ref, za_ref, yg_ref, z_ref, o_ref, g0, g1, g2, g3, x_ref, pv_ref,
             wc_ref, wglu_ref, wp_ref, ws_ref, wo_ref, xn_ref, m_ref):
        _, _, _, _, merged = _merge_branches(za_ref, yg_ref, z_ref, o_ref, (g0, g1, g2, g3),
                                             wc_ref, wglu_ref, wp_ref, ws_ref)
        m = _dot(merged.astype(BF16), wo_ref[...])
        m_ref[...] = m
        xn_ref[...] = _postnorm_res(x_ref[...], m, pv_ref, 1.0)

    acts, weights = _merge_specs(tm, D)
    tile = pl.BlockSpec((tm, D), lambda i, l: (i, 0))
    grid_spec = pltpu.PrefetchScalarGridSpec(
        num_scalar_prefetch=1, grid=(T // tm,),
        in_specs=acts + [tile, pl.BlockSpec((8, D), lambda i, l: (0, 0))] + weights,
        out_specs=[tile, tile],
    )
    return pl.pallas_call(
        body, name="merge_fwd", grid_spec=grid_spec,
        out_shape=[jax.ShapeDtypeStruct((T, D), F32), jax.ShapeDtypeStruct((T, D), F32)],
        compiler_params=_cparams(("arbitrary",)),
    )(l, za, yg, z, o, p, p, p, p, x, pv, wc, wglu, wp, ws, wo)


def merge_bwd(l, p, za, yg, z, o, m, dxn, pv, wc, wglu, wp, ws, wo, tm=128):
    T, D = m.shape
    W = MIX_W
    ni = T // tm

    def body(l_ref, za_ref, yg_ref, z_ref, o_ref, g0, g1, g2, g3, m_ref, dxn_ref, pv_ref,
             wc_ref, wglu_ref, wp_ref, ws_ref, wo_ref,
             dza_ref, dyg_ref, dz_ref, do_ref, dg_ref, pg_ref, gwc_ref, gwglu_ref, gwp_ref, gws_ref, gwo_ref,
             awc, awglu, awp, aws, awo):
        i = pl.program_id(0)

        @pl.when(i == 0)
        def _():
            pg_ref[...] = jnp.zeros_like(pg_ref)
            for a in (awc, awglu, awp, aws, awo):
                a[...] = jnp.zeros_like(a)

        ys, gs, glu_a, sg, merged = _merge_branches(za_ref, yg_ref, z_ref, o_ref, (g0, g1, g2, g3),
                                                    wc_ref, wglu_ref, wp_ref, ws_ref)
        dm = _postnorm_bwd(dxn_ref[...], m_ref[...], pv_ref, pg_ref, 1.0).astype(BF16)
        awo[...] += _dot_tn(merged.astype(BF16), dm)
        dmerged = _dot_nt(dm, wo_ref[...])
        for b in range(4):
            dg_ref[:, b * D:(b + 1) * D] = (dmerged * ys[b] * gs[b] * (1.0 - gs[b])).astype(BF16)
        dya = (dmerged * gs[0]).astype(BF16)
        _acc_cols_tn(awc, za_ref[...], dya)
        dza_ref[...] = _dot_cols_nt(dya, wc_ref)
        dyc = (dmerged * gs[2]).astype(BF16)
        _acc_cols_tn(awp, z_ref[...], dyc)
        dz_ref[...] = _dot_cols_nt(dyc, wp_ref)
        dyd = (dmerged * gs[3]).astype(BF16)
        _acc_cols_tn(aws, o_ref[...], dyd)
        do_ref[...] = _dot_cols_nt(dyd, ws_ref)
        dyb = dmerged * gs[1]
        dglu = jnp.concatenate([dyb * sg, dyb * glu_a * sg * (1.0 - sg)], axis=1).astype(BF16)
        _acc_cols_tn(awglu, yg_ref[...], dglu)
        dyg_ref[...] = _dot_cols_nt(dglu, wglu_ref)

        @pl.when(i == ni - 1)
        def _():
            gwc_ref[...] = awc[...].astype(BF16)
            gwglu_ref[...] = awglu[...].astype(BF16)
            gwp_ref[...] = awp[...].astype(BF16)
            gws_ref[...] = aws[...].astype(BF16)
            gwo_ref[...] = awo[...].astype(BF16)

    acts, weights = _merge_specs(tm, D)
    tile = pl.BlockSpec((tm, D), lambda i, l: (i, 0))
    br = pl.BlockSpec((tm, W), lambda i, l: (i, 0))
    full = lambda *s: pl.BlockSpec(s, lambda i, l: (0,) * len(s))
    sm, glu_s = (N_DEV, W, D // N_DEV), (N_DEV, W, 2 * D // N_DEV)
    grid_spec = pltpu.PrefetchScalarGridSpec(
        num_scalar_prefetch=1, grid=(ni,),
        in_specs=acts + [tile, tile, pl.BlockSpec((8, D), lambda i, l: (0, 0))] + weights,
        out_specs=[br, br, br, br, pl.BlockSpec((tm, 4 * D), lambda i, l: (i, 0)), full(8, D),
                   full(*sm), full(*glu_s), full(*sm), full(*sm), full(D, D)],
        scratch_shapes=[pltpu.VMEM(sm, F32), pltpu.VMEM(glu_s, F32), pltpu.VMEM(sm, F32),
                        pltpu.VMEM(sm, F32), pltpu.VMEM((D, D), F32)],
    )
    f32br = jax.ShapeDtypeStruct((T, W), F32)
    return pl.pallas_call(
        body, name="merge_bwd", grid_spec=grid_spec,
        out_shape=[f32br, f32br, f32br, f32br, jax.ShapeDtypeStruct((T, 4 * D), BF16),
                   jax.ShapeDtypeStruct((8, D), F32),
                   jax.ShapeDtypeStruct(sm, BF16), jax.ShapeDtypeStruct(glu_s, BF16),
                   jax.ShapeDtypeStruct(sm, BF16), jax.ShapeDtypeStruct(sm, BF16),
                   jax.ShapeDtypeStruct((D, D), BF16)],
        compiler_params=_cparams(("arbitrary",)),
    )(l, za, yg, z, o, p, p, p, p, m, dxn, pv, wc, wglu, wp, ws, wo)


def dp_assemble(d_conv, d_ssm, d_pool, d_qkv, d_gates, tm=512):
    T = d_conv.shape[0]
    W = MIX_W

    def body(c_ref, s_ref, p_ref, q_ref, g_ref, dp_ref):
        dp_ref[:, 0:3 * W] = c_ref[...]
        dp_ref[:, 3 * W:4 * W] = s_ref[...]
        dp_ref[:, 4 * W:5 * W] = p_ref[...]
        dp_ref[:, 5 * W:8 * W] = q_ref[...]
        dp_ref[:, GATE_OFF:] = g_ref[...]

    row = lambda w: pl.BlockSpec((tm, w), lambda i: (i, 0))
    return pl.pallas_call(
        body, name="dp_assemble", grid=(T // tm,),
        in_specs=[row(3 * W), row(W), row(W), row(3 * W), row(4 * D_MODEL)],
        out_specs=row(IN_COLS),
        out_shape=jax.ShapeDtypeStruct((T, IN_COLS), BF16),
        compiler_params=_cparams(("arbitrary",)),
    )(d_conv, d_ssm, d_pool, d_qkv, d_gates)


def loss_head(y, target, tm=512):
    T, D = y.shape

    def body(y_ref, t_ref, dy_ref, loss_ref):
        @pl.when(pl.program_id(0) == 0)
        def _():
            loss_ref[...] = jnp.zeros_like(loss_ref)

        err = y_ref[...] - t_ref[...]
        dy_ref[...] = err * (1.0 / D)
        loss_ref[...] += jnp.sum(err * err) * (0.5 / D)

    tile = pl.BlockSpec((tm, D), lambda i: (i, 0))
    return pl.pallas_call(
        body, name="loss_head", grid=(T // tm,),
        in_specs=[tile, tile],
        out_specs=[tile, pl.BlockSpec((8, 128), lambda i: (0, 0))],
        out_shape=[jax.ShapeDtypeStruct((T, D), F32), jax.ShapeDtypeStruct((8, 128), F32)],
        compiler_params=_cparams(("arbitrary",)),
    )(y, target)


def cast_layer(ld, ws, ps):
    def body(ld_ref, *refs):
        n = len(refs) // 2
        for src, dst in zip(refs[:n], refs[n:]):
            dst[...] = src[...].astype(BF16)

    def in_spec(w):
        nd = w.ndim
        return pl.BlockSpec((None,) + w.shape[1:], lambda i, ld, nd=nd: (ld[0],) + (0,) * (nd - 1))

    def out_spec(w, p):
        shard = w.shape[1:]
        block = (None,) + shard[:p - 1] + (None,) + shard[p - 1:]
        return pl.BlockSpec(block, lambda i, ld, p=p, nd=len(block): (0,) * p + (ld[1],) + (0,) * (nd - p - 1))

    def out_shape(w, p):
        shard = w.shape[1:]
        return jax.ShapeDtypeStruct((1,) + shard[:p - 1] + (N_DEV,) + shard[p - 1:], BF16)

    grid_spec = pltpu.PrefetchScalarGridSpec(
        num_scalar_prefetch=1, grid=(1,),
        in_specs=[in_spec(w) for w in ws], out_specs=[out_spec(w, p) for w, p in zip(ws, ps)])
    return pl.pallas_call(
        body, name="cast_layer", grid_spec=grid_spec,
        out_shape=[out_shape(w, p) for w, p in zip(ws, ps)],
        compiler_params=_cparams(("arbitrary",)),
    )(ld, *ws)


def _silu(x):
    return x * _sigmoid(x)


def ada_fwd(c_all, w_ada, b_cols):
    L, D, n = w_ada.shape

    def body(c_ref, w_ref, b_ref, o_ref):
        c_act = _silu(c_ref[...]).astype(BF16)
        o_ref[...] = _dot(c_act, w_ref[...].astype(BF16)) + b_ref[...]

    return pl.pallas_call(
        body, name="ada_fwd", grid=(L,),
        in_specs=[pl.BlockSpec((N_DEV, D), lambda l: (0, 0)), pl.BlockSpec((None, D, n), lambda l: (l, 0, 0)),
                  pl.BlockSpec((None, 1, n), lambda l: (l, 0, 0))],
        out_specs=pl.BlockSpec((None, N_DEV, n), lambda l: (l, 0, 0)),
        out_shape=jax.ShapeDtypeStruct((L, N_DEV, n), F32),
        compiler_params=_cparams(("arbitrary",)),
    )(c_all, w_ada, b_cols)


def _adamw(w, g, m, v):
    m = ADAM_B1 * m + (1.0 - ADAM_B1) * g
    v = ADAM_B2 * v + (1.0 - ADAM_B2) * (g * g)
    m_hat = m / (1.0 - ADAM_B1 ** ADAM_STEP)
    v_hat = v / (1.0 - ADAM_B2 ** ADAM_STEP)
    delta = -ADAM_LR * (m_hat / (jnp.sqrt(v_hat) + ADAM_EPS) + ADAM_WD * w)
    return delta, m, v


def ada_update(c_all, dada_cols, w, m, v, rb=256):
    L, D, n = w.shape

    def body(c_ref, d_ref, w_ref, m_ref, v_ref, g_ref, dl_ref, nm_ref, nv_ref):
        c_act = _silu(c_ref[...]).astype(BF16)
        g = _dot_tn(c_act, d_ref[...].astype(BF16))
        g_ref[...] = g
        dl_ref[...], nm_ref[...], nv_ref[...] = _adamw(w_ref[...], g, m_ref[...], v_ref[...])

    blk = pl.BlockSpec((None, rb, n), lambda l, i: (l, i, 0))
    out = jax.ShapeDtypeStruct((L, D, n), F32)
    return pl.pallas_call(
        body, name="ada_update", grid=(L, D // rb),
        in_specs=[pl.BlockSpec((N_DEV, rb), lambda l, i: (0, i)),
                  pl.BlockSpec((None, N_DEV, n), lambda l, i: (l, 0, 0)), blk, blk, blk],
        out_specs=[blk, blk, blk, blk], out_shape=[out, out, out, out],
        compiler_params=_cparams(("arbitrary", "arbitrary")),
    )(c_all, dada_cols, w, m, v)


SUM_UPDATE_RECV_BYTES = 12 * 1024 * 1024


def sum_update(dev, first, recvs, owns, w, m, v, prev=None):
    n_slots, R, C = w.shape
    S = len(recvs)
    assert len(owns) == S and first + S <= n_slots
    rb_max = SUM_UPDATE_RECV_BYTES // (S * N_DEV * C * 2)
    rb = max(r for r in range(8, R + 1, 8) if R % r == 0 and (r <= rb_max or r == 8))
    last = R // rb - 1
    n_prev = 0 if prev is None else 4

    def body(dev_ref, *refs):
        r_refs, o_refs = refs[:S], refs[S:2 * S]
        w_ref, m_ref, v_ref = refs[2 * S:2 * S + 3]
        g_ref, dl_ref, nm_ref, nv_ref = refs[2 * S + 3 + n_prev:]
        me = dev_ref[0]
        for s in range(S):
            @pl.when(pl.program_id(0) == s)
            def _(s=s):
                g = jnp.zeros((rb, C), F32)
                for d in range(N_DEV):
                    g += jnp.where(me == d, o_refs[s][...], r_refs[s][d]).astype(F32)
                g_ref[...] = g
                dl_ref[...], nm_ref[...], nv_ref[...] = _adamw(w_ref[...], g, m_ref[...], v_ref[...])

    def row(sl, i, s):
        return jnp.where(sl == s, i, jnp.where(sl < s, 0, last))

    def rspec(s):
        return pl.BlockSpec((N_DEV, rb, C), lambda sl, i, dev: (0, row(sl, i, s), 0))

    def ospec(s):
        return pl.BlockSpec((None, rb, C), lambda sl, i, dev: (dev[0], row(sl, i, s), 0))

    blk = pl.BlockSpec((None, rb, C), lambda sl, i, dev: (first + sl, i, 0))
    out = jax.ShapeDtypeStruct((n_slots, R, C), F32)
    grid_spec = pltpu.PrefetchScalarGridSpec(
        num_scalar_prefetch=1, grid=(S, R // rb),
        in_specs=[rspec(s) for s in range(S)] + [ospec(s) for s in range(S)] + [blk, blk, blk] + [ANY] * n_prev,
        out_specs=[blk, blk, blk, blk],
    )
    n_in = 1 + 2 * S + 3
    return pl.pallas_call(
        body, name="sum_update", grid_spec=grid_spec, out_shape=[out, out, out, out],
        input_output_aliases={n_in + i: i for i in range(n_prev)},
        compiler_params=_cparams(("arbitrary", "arbitrary")),
    )(dev, *recvs, *owns, w, m, v, *(prev or ()))


def small_sum(gathered):
    _, R, C = gathered.shape

    def body(g_ref, o_ref):
        acc = g_ref[0]
        for d in range(1, N_DEV):
            acc += g_ref[d]
        o_ref[...] = acc

    return pl.pallas_call(
        body, name="small_sum", grid=(1,),
        in_specs=[pl.BlockSpec((N_DEV, R, C), lambda i: (0, 0, 0))],
        out_specs=pl.BlockSpec((R, C), lambda i: (0, 0)),
        out_shape=jax.ShapeDtypeStruct((R, C), F32),
        compiler_params=_cparams(("arbitrary",)),
    )(gathered)


def small_update(w, g, m, v):
    def body(w_ref, g_ref, m_ref, v_ref, dl_ref, nm_ref, nv_ref):
        dl_ref[...], nm_ref[...], nv_ref[...] = _adamw(w_ref[...], g_ref[...], m_ref[...], v_ref[...])

    blk = pl.BlockSpec(w.shape, lambda i: (0, 0))
    out = jax.ShapeDtypeStruct(w.shape, F32)
    return pl.pallas_call(
        body, name="small_update", grid=(1,),
        in_specs=[blk] * 4, out_specs=[blk] * 3, out_shape=[out] * 3,
        compiler_params=_cparams(("arbitrary",)),
    )(w, g, m, v)


MESH = pl.DeviceIdType.MESH
ANY = pl.BlockSpec(memory_space=pl.ANY)


def _coords():
    return lax.axis_index("x"), lax.axis_index("y"), lax.axis_index("c")


def _dev_index(x, y, c):
    return 4 * x + 2 * y + c


def _at_dev(ref, p, dev):
    return ref.at[(slice(None),) * p + (dev,)]


def all_gather(arrays, ps):
    n = len(arrays)

    def body(*refs):
        ins, outs = refs[:n], refs[n:2 * n]
        send_sems, recv_sems, local_sems = refs[2 * n:]
        x, y, c = _coords()
        me, sibling = (x, y, c), (x, y, 1 - c)
        chips = [(1 - x, y), (x, 1 - y), (1 - x, 1 - y)]

        def copy(a, k, block, to, src=None):
            dst = _at_dev(outs[a], ps[a], _dev_index(*block))
            return pltpu.make_async_remote_copy(
                src_ref=dst if src is None else src, dst_ref=dst,
                send_sem=send_sems.at[a, k], recv_sem=recv_sems.at[a, k], device_id=to, device_id_type=MESH)

        mine = [pltpu.make_async_copy(ins[a], _at_dev(outs[a], ps[a], _dev_index(*me)), local_sems.at[a])
                for a in range(n)]
        for cp in mine:
            cp.start()
        first = []
        for a in range(n):
            first.append(copy(a, 0, me, sibling, src=ins[a]))
            first += [copy(a, 1 + j, me, (*chip, c), src=ins[a]) for j, chip in enumerate(chips)]
        for cp in first:
            cp.start()
        passed = []
        for j, chip in enumerate(chips):
            for a in range(n):
                copy(a, 1 + j, (*chip, c), me).wait_recv()
                fwd = copy(a, 4 + j, (*chip, c), sibling)
                fwd.start()
                passed.append(fwd)
        for a in range(n):
            copy(a, 0, sibling, me).wait_recv()
            for j, chip in enumerate(chips):
                copy(a, 4 + j, (*chip, 1 - c), me).wait_recv()
        for cp in first + passed:
            cp.wait_send()
        for cp in mine:
            cp.wait()

    out_shape = [jax.ShapeDtypeStruct(a.shape[:p] + (N_DEV,) + a.shape[p:], a.dtype) for a, p in zip(arrays, ps)]
    return pl.pallas_call(
        body, name="all_gather", in_specs=[ANY] * n, out_specs=[ANY] * n, out_shape=out_shape,
        scratch_shapes=[pltpu.SemaphoreType.DMA((n, 7)), pltpu.SemaphoreType.DMA((n, 7)),
                        pltpu.SemaphoreType.DMA((n,))],
        compiler_params=pltpu.CompilerParams(has_side_effects=True),
    )(*arrays)


HBM = pl.BlockSpec(memory_space=pltpu.HBM)
SEM = pl.BlockSpec(memory_space=pltpu.SEMAPHORE)
EFFECT = pltpu.SideEffectType.DATAFLOW_SIDE_EFFECTING


def _peers(x, y, c):
    out = []
    for k in range(1, N_DEV):
        out.append((1 - x if k & 4 else x, 1 - y if k & 2 else y, 1 - c if k & 1 else c))
    return out


def _exchange_plan(n):
    def plan(refs, x, y, c):
        blocks, lands = refs[:n], refs[n:2 * n]
        me = _dev_index(x, y, c)
        moves = []
        for peer in _peers(x, y, c):
            q = _dev_index(*peer)
            moves += [(blocks[a].at[q], lands[a].at[me], peer, lands[a].at[q]) for a in range(n)]
        return moves
    return plan


def _gather_plan(ps, second):
    def plan(refs, x, y, c):
        me, sibling = (x, y, c), (x, y, 1 - c)
        chips = [(1 - x, y), (x, 1 - y), (1 - x, 1 - y)]
        if second:
            trips = [((*ch, c), sibling, (*ch, 1 - c)) for ch in chips]
        else:
            trips = [(me, sibling, sibling)] + [(me, (*ch, c), (*ch, c)) for ch in chips]
        moves = []
        for sent, to, arriving in trips:
            for ref, p in zip(refs, ps):
                blk = _at_dev(ref, p, _dev_index(*sent))
                moves.append((blk, blk, to, _at_dev(ref, p, _dev_index(*arriving))))
        return moves
    return plan


def copies_start(name, plan, n_moves, arrays, carry):
    n = len(arrays)

    def body(*refs):
        sems = refs[n + 1:n + 1 + 2 * n_moves]
        moves = plan(refs[:n], *_coords())
        assert len(moves) == n_moves
        for i, (src, dst, to, _) in enumerate(moves):
            pltpu.make_async_remote_copy(src_ref=src, dst_ref=dst, send_sem=sems[i], recv_sem=sems[n_moves + i],
                                         device_id=to, device_id_type=MESH).start()

    operands = [pltpu.with_memory_space_constraint(a, pltpu.HBM) for a in list(arrays) + [carry]]
    outs = pl.pallas_call(
        body, name=name,
        out_shape=[pltpu.SemaphoreType.DMA(())] * (2 * n_moves) + [pltpu.HBM(a.shape, a.dtype) for a in operands],
        in_specs=[HBM] * (n + 1), out_specs=[SEM] * (2 * n_moves) + [HBM] * (n + 1),
        input_output_aliases={i: 2 * n_moves + i for i in range(n + 1)},
        compiler_params=pltpu.CompilerParams(has_side_effects=EFFECT),
    )(*operands)
    return outs[:n_moves], outs[n_moves:2 * n_moves], outs[2 * n_moves:-1], outs[-1]


def copies_wait(name, plan, send_sems, recv_sems, arrays, after):
    n, n_moves = len(arrays), len(send_sems)

    def body(*refs):
        sems = refs[n:n + 2 * n_moves]
        for i, (src, _, to, arriving) in enumerate(plan(refs[:n], *_coords())):
            cp = pltpu.make_async_remote_copy(src_ref=src, dst_ref=arriving, send_sem=sems[i],
                                              recv_sem=sems[n_moves + i], device_id=to, device_id_type=MESH)
            cp.wait_send()
            cp.wait_recv()

    return pl.pallas_call(
        body, name=name,
        out_shape=[pltpu.HBM(a.shape, a.dtype) for a in arrays],
        in_specs=[HBM] * n + [SEM] * (2 * n_moves) + [ANY], out_specs=[HBM] * n,
        input_output_aliases={i: i for i in range(n)},
        compiler_params=pltpu.CompilerParams(has_side_effects=EFFECT),
    )(*arrays, *send_sems, *recv_sems, after)


WEIGHT_NAMES = ("w_ada", "b_ada", "g_pre", "g_post", "w_ff_in", "w_ff_out", "w_in", "conv_w", "w_conv_out",
                "lam_re", "lam_im", "log_dt", "ssm_b_re", "ssm_b_im", "ssm_c_re", "ssm_c_im", "ssm_d", "w_glu",
                "w_pool", "pool_scale", "w_pool_out", "w_sb_out", "w_out")
BIG_NAMES = ("w_ff_in", "w_ff_out", "w_in", "w_conv_out", "w_glu", "w_pool_out", "w_sb_out", "w_out")
GATHER_PS = (2, 2, 1, 1, 1, 1, 1, 1)
SMALL_NAMES = ("b_ada", "g_pre", "g_post", "conv_w", "lam_re", "lam_im", "log_dt", "ssm_b_re", "ssm_b_im",
               "ssm_c_re", "ssm_c_im", "ssm_d", "w_pool", "pool_scale")
PACK_LANES = 128
PACK_ROWS = 8


def _pack(arrays):
    flat = jnp.concatenate([a.reshape(-1) for a in arrays])
    unit = PACK_LANES * PACK_ROWS
    flat = jnp.pad(flat, (0, -flat.shape[0] % unit))
    return flat.reshape(-1, PACK_LANES)


def _unpack(pack, shapes):
    flat = pack.reshape(-1)
    out, off = [], 0
    for s in shapes:
        n = 1
        for d in s:
            n *= d
        out.append(flat[off:off + n].reshape(s))
        off += n
    return out


def _pad_rows(a, rows=8):
    return jnp.pad(a, ((0, 0), (0, rows - a.shape[1]), (0, 0)))


def _tile_b(b):
    L = b.shape[0]
    return jnp.tile(b.transpose(0, 3, 1, 2).reshape(L, SSM_GROUP, SSM_W), (1, SSM_GROUPS, 1))


def _tile_c(c):
    L = c.shape[0]
    return jnp.tile(c.transpose(0, 3, 1, 2).reshape(L, SSM_STATE, MIX_W), (1, SSM_GROUPS, 1))


def _step(x, c, target, W, M, V):
    T, D = x.shape[1], x.shape[2]
    L = W["w_ada"].shape[0]
    x = x[0]
    target = target[0]
    ax, ay, ac = _coords()
    dev = _dev_index(ax, ay, ac)
    n_ada = W["w_ada"].shape[2]

    dev_s = jnp.reshape(dev, (1,)).astype(jnp.int32)
    big_ws = [W[k] for k in BIG_NAMES]
    bufs = [cast_layer(jnp.concatenate([jnp.array([l], jnp.int32), dev_s]), big_ws, GATHER_PS) for l in range(L)]
    plan_a, plan_b = _gather_plan(GATHER_PS, False), _gather_plan(GATHER_PS, True)
    n_a, n_b = 4 * len(BIG_NAMES), 3 * len(BIG_NAMES)
    sem_s, sem_r, bufs[0], x = copies_start("gather_a_start_0", plan_a, n_a, bufs[0], x)

    gathered = all_gather([W["g_pre"], W["g_post"], W["conv_w"], c], [0, 0, 0, 0])
    g_pre = gathered[0].transpose(1, 2, 0, 3).reshape(L, N_SUB, D)
    g_post = gathered[1].transpose(1, 2, 0, 3).reshape(L, N_SUB, D)
    conv_w = _pad_rows(gathered[2].transpose(1, 2, 0, 3).reshape(L, 3, MIX_W))
    c_all = gathered[3].reshape(N_DEV, D)

    b_cols = lax.dynamic_slice_in_dim(W["b_ada"], dev * n_ada, n_ada, axis=1)[:, None, :]
    ada_cols = ada_fwd(c_all, W["w_ada"], b_cols)
    ada_all = all_gather([ada_cols], [0])[0]
    ada = lax.dynamic_index_in_dim(ada_all, dev, axis=2, keepdims=False)
    ada = ada.transpose(1, 0, 2).reshape(L, N_SUB, 3, D)
    zeros = jnp.zeros((L, N_SUB, D), F32)
    pv_all = jnp.stack([g_pre, ada[:, :, 0], ada[:, :, 1], g_post, ada[:, :, 2], zeros, zeros, zeros], axis=2)

    lam = jnp.stack([W["lam_re"].reshape(L, SSM_W), W["lam_im"].reshape(L, SSM_W),
                     jnp.repeat(W["log_dt"], SSM_STATE, axis=1)], axis=1)
    lam = _pad_rows(lam)
    b_t = jnp.stack([_tile_b(W["ssm_b_re"]), _tile_b(W["ssm_b_im"])], axis=1)
    c_t = jnp.stack([_tile_c(W["ssm_c_re"]), _tile_c(W["ssm_c_im"])], axis=1)
    avec, b_bd, c_bd = s5_params(lam, b_t, c_t)
    ssm_d = _pad_rows(W["ssm_d"][:, None, :])
    pool_scale = _pad_rows(W["pool_scale"][:, None, :])
    eye4 = jnp.eye(len(POOL_WINDOWS), dtype=F32)
    w_bd = jnp.einsum("lgcd,gh->lgchd", W["w_pool"], eye4).reshape(L, MIX_W, MIX_W).astype(BF16)

    bufs[0] = copies_wait("gather_a_wait_0", plan_a, sem_s, sem_r, bufs[0], pv_all)
    sem_s, sem_r, bufs[0], x = copies_start("gather_b_start_0", plan_b, n_b, bufs[0], x)
    bufs[0] = copies_wait("gather_b_wait_0", plan_b, sem_s, sem_r, bufs[0], x)

    def layer_weights(l):
        b = bufs[l]
        return (b[0].reshape(1, 2, 2, 4, D, FF_BLK), b[1].reshape(1, 2, 4, FF_BLK, D), b[2], b[3], b[4], b[5], b[6],
                b[7].reshape(1, D, D))

    l0 = jnp.array([0], jnp.int32)
    k0, k1 = jnp.array([0, 0], jnp.int32), jnp.array([0, 1], jnp.int32)
    saved = []
    for l in range(L):
        li = jnp.array([l], jnp.int32)
        wg_ff_in, wg_ff_out, wg_in, wg_conv, wg_glu, wg_pool, wg_sb, wg_out = layer_weights(l)
        nxt = l + 1 < L
        if nxt:
            sem_s, sem_r, bufs[l + 1], x = copies_start(f"gather_a_start_{l + 1}", plan_a, n_a, bufs[l + 1], x)
        x0 = x
        ab0, f0, x1 = ffn_fwd(k0, x0, pv_all[l, 0], wg_ff_in, wg_ff_out)
        p = mix_in_fwd(l0, x1, pv_all[l, 1], wg_in)
        za = conv_fwd(li, p, conv_w)
        z = pool_fwd(li, p, w_bd, pool_scale)
        s = s5_scan(li, avec, s5_bu(li, p, b_bd), False)
        yg = s5_out(li, p, s, c_bd, ssm_d)
        o = sb_fwd(p)
        x2, m = merge_fwd(l0, p, za, yg, z, o, x1, pv_all[l, 1], wg_conv, wg_glu, wg_pool, wg_sb, wg_out)
        if nxt:
            bufs[l + 1] = copies_wait(f"gather_a_wait_{l + 1}", plan_a, sem_s, sem_r, bufs[l + 1], x2)
            sem_s, sem_r, bufs[l + 1], x2 = copies_start(f"gather_b_start_{l + 1}", plan_b, n_b, bufs[l + 1], x2)
        ab1, f1, x = ffn_fwd(k1, x2, pv_all[l, 2], wg_ff_in, wg_ff_out)
        if nxt:
            bufs[l + 1] = copies_wait(f"gather_b_wait_{l + 1}", plan_b, sem_s, sem_r, bufs[l + 1], x)
        saved.append((x0, ab0, f0, x1, p, za, z, s, yg, o, m, x2, ab1, f1))

    dx, loss_blk = loss_head(x, target)
    loss = lax.psum(loss_blk[0, 0], ("x", "y", "c"))

    recvs, owns, in_flight = [None] * L, [None] * L, None
    n_blocks = 10
    plan_x, n_x = _exchange_plan(n_blocks), (N_DEV - 1) * n_blocks

    def settle(flight, after):
        layer, s_sem, r_sem, arrays = flight
        arrays = copies_wait(f"exchange_wait_{layer}", plan_x, s_sem, r_sem, arrays, after)
        owns[layer], recvs[layer] = arrays[:n_blocks], arrays[n_blocks:]

    pgs = [None] * L
    small = {k: [None] * L for k in ("conv_w", "w_bd", "pool_scale", "ssm_d", "gb", "gc", "da")}
    for l in reversed(range(L)):
        li = jnp.array([l], jnp.int32)
        x0, ab0, f0, x1, p, za, z, s, yg, o, m, x2, ab1, f1 = saved[l]
        wg_ff_in, wg_ff_out, wg_in, wg_conv, wg_glu, wg_pool, wg_sb, wg_out = layer_weights(l)
        dab, h, df, dx, pg2 = ffn_bwd_act(k1, dx, x2, f1, pv_all[l, 2], ab1, wg_ff_in, wg_ff_out)
        g_in1, g_out1 = ffn_bwd_w(h, df, ab1, dab)
        (dza, dyg, dz, do, dgates, pg1m, g_conv, g_glu, g_pool, g_sb, g_wo) = merge_bwd(
            l0, p, za, yg, z, o, m, dx, pv_all[l, 1], wg_conv, wg_glu, wg_pool, wg_sb, wg_out)
        d_conv, small["conv_w"][l] = conv_bwd(li, p, dza, conv_w)
        d_pool, small["w_bd"][l], small["pool_scale"][l] = pool_bwd(li, p, dz, w_bd, pool_scale)
        ds, du_skip, small["gc"][l], small["ssm_d"][l] = s5_bwd_y(li, p, s, dyg, c_bd, ssm_d)
        lam_s = s5_scan(li, avec, ds, True)
        d_ssm, small["gb"][l] = s5_bwd_u(li, p, lam_s, du_skip, b_bd)
        small["da"][l] = s5_bwd_a(s, lam_s)
        d_qkv = sb_bwd(p, do)
        dp = dp_assemble(d_conv, d_ssm, d_pool, d_qkv, dgates)
        dx, h, pg1i = mix_in_bwd_act(l0, dp, dx, x1, pv_all[l, 1], wg_in)
        g_win = matmul_tn(h, dp, IN_BLK)
        dab, h, df, dx, pg0 = ffn_bwd_act(k0, dx, x0, f0, pv_all[l, 0], ab0, wg_ff_in, wg_ff_out)
        g_in0, g_out0 = ffn_bwd_w(h, df, ab0, dab)
        pgs[l] = jnp.stack([pg0, pg1m + pg1i, pg2])
        blocks = [g_in0.reshape(N_DEV, D, FF_BLK), g_in1.reshape(N_DEV, D, FF_BLK),
                  g_out0.reshape(N_DEV, D_FF // N_DEV, D), g_out1.reshape(N_DEV, D_FF // N_DEV, D),
                  g_win, g_conv, g_glu, g_pool, g_sb, g_wo.reshape(N_DEV, D // N_DEV, D)]
        if in_flight is not None:
            settle(in_flight, dx)
        arrays = blocks + [lax.empty(a.shape, a.dtype) for a in blocks]
        s_sem, r_sem, arrays, dx = copies_start(f"exchange_start_{l}", plan_x, n_x, arrays, dx)
        in_flight = (l, s_sem, r_sem, arrays)

    dlam, db_t, dc_t, dldt = s5_params_bwd(lam, b_t, jnp.stack(small["gb"]), jnp.stack(small["gc"]),
                                           jnp.stack(small["da"]))
    pg = jnp.stack(pgs)
    d_ada = jnp.stack([pg[:, :, PV_SHIFT], pg[:, :, PV_SCALE], pg[:, :, PV_GATE]], axis=2).reshape(L, N_SUB * 3 * D)
    db = db_t.reshape(L, 2, SSM_GROUPS, SSM_GROUP, SSM_GROUPS, SSM_STATE)
    db = jnp.einsum("lrghgp->lrgph", db)
    dc = dc_t.reshape(L, 2, SSM_GROUPS, SSM_STATE, SSM_GROUPS, SSM_GROUP)
    dc = jnp.einsum("lrgpgh->lrghp", dc)
    d_wpool = jnp.einsum("lgcgd->lgcd", jnp.stack(small["w_bd"]).reshape(L, 4, 64, 4, 64))
    contrib = {
        "b_ada": d_ada, "g_pre": pg[:, :, PV_GPRE], "g_post": pg[:, :, PV_GPOST],
        "conv_w": jnp.stack(small["conv_w"])[:, :3], "lam_re": dlam[:, 0].reshape(L, SSM_GROUPS, SSM_STATE),
        "lam_im": dlam[:, 1].reshape(L, SSM_GROUPS, SSM_STATE), "log_dt": dldt[:, 2, :SSM_GROUPS],
        "ssm_b_re": db[:, 0], "ssm_b_im": db[:, 1], "ssm_c_re": dc[:, 0], "ssm_c_im": dc[:, 1],
        "ssm_d": jnp.stack(small["ssm_d"])[:, 0], "w_pool": d_wpool,
        "pool_scale": jnp.stack(small["pool_scale"])[:, 0],
    }
    contrib_shapes = [contrib[k].shape for k in SMALL_NAMES]
    pack_all = all_gather([_pack([contrib[k] for k in SMALL_NAMES])], [0])[0]
    total = dict(zip(SMALL_NAMES, _unpack(small_sum(pack_all), contrib_shapes)))
    d_ada_all = pack_all.reshape(N_DEV, -1)[:, :L * N_SUB * 3 * D].reshape(N_DEV, L, N_SUB * 3 * D)
    dada_cols = lax.dynamic_slice_in_dim(d_ada_all, dev * n_ada, n_ada, axis=2).transpose(1, 0, 2)
    n_g = D // N_DEV
    grads = {}
    for k in SMALL_NAMES:
        g = total[k]
        if k in ("g_pre", "g_post"):
            g = lax.dynamic_slice_in_dim(g, dev * n_g, n_g, axis=2)
        elif k == "conv_w":
            g = lax.dynamic_slice_in_dim(g, dev * (MIX_W // N_DEV), MIX_W // N_DEV, axis=2)
        grads[k] = g

    delta, new_m, new_v = {}, {}, {}
    shapes = [W[k].shape for k in SMALL_NAMES]
    dl, nm, nv = small_update(_pack([W[k] for k in SMALL_NAMES]), _pack([grads[k] for k in SMALL_NAMES]),
                              _pack([M[k] for k in SMALL_NAMES]), _pack([V[k] for k in SMALL_NAMES]))
    for k, a, b, cc in zip(SMALL_NAMES, _unpack(dl, shapes), _unpack(nm, shapes), _unpack(nv, shapes)):
        delta[k], new_m[k], new_v[k] = a, b, cc
    grads["w_ada"], delta["w_ada"], new_m["w_ada"], new_v["w_ada"] = ada_update(
        c_all, dada_cols, W["w_ada"], M["w_ada"], V["w_ada"])

    big_idx = {"w_ff_in": (0, 1), "w_ff_out": (2, 3), "w_in": (4,), "w_conv_out": (5,), "w_glu": (6,),
               "w_pool_out": (7,), "w_sb_out": (8,), "w_out": (9,)}

    def big(name, layers, prev):
        idx = big_idx[name]
        flat = (-1,) + W[name].shape[-2:]
        return sum_update(dev_s, layers[0] * len(idx), [recvs[l][i] for l in layers for i in idx],
                          [owns[l][i] for l in layers for i in idx],
                          W[name].reshape(flat), M[name].reshape(flat), V[name].reshape(flat), prev)

    partial = {name: big(name, list(range(1, L)), None) for name in BIG_NAMES} if L > 1 else {}
    after = partial["w_out"][0] if L > 1 else dl
    settle(in_flight, after)
    for name in BIG_NAMES:
        outs = big(name, [0], partial.get(name))
        grads[name], delta[name], new_m[name], new_v[name] = [o.reshape(W[name].shape) for o in outs]

    return (loss, dx[None], *[grads[k] for k in WEIGHT_NAMES], *[delta[k] for k in WEIGHT_NAMES],
            *[new_m[k] for k in WEIGHT_NAMES], *[new_v[k] for k in WEIGHT_NAMES])


def kernel(x, c, w_ada, b_ada, g_pre, g_post, w_ff_in, w_ff_out, w_in, conv_w, w_conv_out, lam_re, lam_im, log_dt, ssm_b_re, ssm_b_im, ssm_c_re, ssm_c_im, ssm_d, w_glu, w_pool, pool_scale, w_pool_out, w_sb_out, w_out, loss_target, m_w_ada, m_b_ada, m_g_pre, m_g_post, m_w_ff_in, m_w_ff_out, m_w_in, m_conv_w, m_w_conv_out, m_lam_re, m_lam_im, m_log_dt, m_ssm_b_re, m_ssm_b_im, m_ssm_c_re, m_ssm_c_im, m_ssm_d, m_w_glu, m_w_pool, m_pool_scale, m_w_pool_out, m_w_sb_out, m_w_out, v_w_ada, v_b_ada, v_g_pre, v_g_post, v_w_ff_in, v_w_ff_out, v_w_in, v_conv_w, v_w_conv_out, v_lam_re, v_lam_im, v_log_dt, v_ssm_b_re, v_ssm_b_im, v_ssm_c_re, v_ssm_c_im, v_ssm_d, v_w_glu, v_w_pool, v_pool_scale, v_w_pool_out, v_w_sb_out, v_w_out):
    w = (w_ada, b_ada, g_pre, g_post, w_ff_in, w_ff_out, w_in, conv_w, w_conv_out, lam_re, lam_im, log_dt, ssm_b_re, ssm_b_im, ssm_c_re, ssm_c_im, ssm_d, w_glu, w_pool, pool_scale, w_pool_out, w_sb_out, w_out)
    m = (m_w_ada, m_b_ada, m_g_pre, m_g_post, m_w_ff_in, m_w_ff_out, m_w_in, m_conv_w, m_w_conv_out, m_lam_re, m_lam_im, m_log_dt, m_ssm_b_re, m_ssm_b_im, m_ssm_c_re, m_ssm_c_im, m_ssm_d, m_w_glu, m_w_pool, m_pool_scale, m_w_pool_out, m_w_sb_out, m_w_out)
    v = (v_w_ada, v_b_ada, v_g_pre, v_g_post, v_w_ff_in, v_w_ff_out, v_w_in, v_conv_w, v_w_conv_out, v_lam_re, v_lam_im, v_log_dt, v_ssm_b_re, v_ssm_b_im, v_ssm_c_re, v_ssm_c_im, v_ssm_d, v_w_glu, v_w_pool, v_pool_scale, v_w_pool_out, v_w_sb_out, v_w_out)
    return _step(x, c, loss_target, dict(zip(WEIGHT_NAMES, w)), dict(zip(WEIGHT_NAMES, m)), dict(zip(WEIGHT_NAMES, v)))
```

```python
import functools

import jax
import jax.numpy as jnp
from jax import lax
from jax.experimental import pallas as pl
from jax.experimental.pallas import tpu as pltpu

F32 = jnp.float32
BF16 = jnp.bfloat16

N_DEV = 8
D_MODEL = 1024
D_FF = 2816
FF_BLK = D_FF // 4
N_SUB = 3
MIX_W = 256
IN_COLS = 6144
IN_BLK = IN_COLS // N_DEV
GATE_OFF = 2048
SSM_GROUPS, SSM_GROUP, SSM_STATE = 16, 16, 64
SSM_W = SSM_GROUPS * SSM_STATE
POOL_WINDOWS = (2, 4, 8, 16)
SB_HEAD = 64
EPS = 1e-6
DT_LAMBDA_RE_MAX = -1e-4
ADAM_LR, ADAM_B1, ADAM_B2, ADAM_EPS, ADAM_WD, ADAM_STEP = 0.001, 0.9, 0.999, 1e-08, 0.01, 10

VMEM_LIMIT = 56 * 1024 * 1024

PV_GPRE, PV_SHIFT, PV_SCALE, PV_GPOST, PV_GATE = 0, 1, 2, 3, 4


def _cparams(sem):
    return pltpu.CompilerParams(dimension_semantics=sem, vmem_limit_bytes=VMEM_LIMIT)


def _dot(a, b):
    return jnp.dot(a, b, preferred_element_type=F32)


def _dot_nt(a, b):
    return lax.dot_general(a, b, (((1,), (1,)), ((), ())), preferred_element_type=F32)


def _dot_tn(a, b):
    return lax.dot_general(a, b, (((0,), (0,)), ((), ())), preferred_element_type=F32)


def _rms(x):
    r = lax.rsqrt(jnp.mean(x * x, axis=-1, keepdims=True) + EPS)
    return x * r, r


def _rms_bwd(dn, n, r):
    return r * (dn - n * jnp.mean(dn * n, axis=-1, keepdims=True))


def _sigmoid(x):
    return 1.0 / (1.0 + jnp.exp(-x))


def _colsum(x):
    return jnp.sum(x, axis=0, keepdims=True)


def _prenorm(x, pv_ref):
    n, r = _rms(x)
    hn = n * pv_ref[PV_GPRE:PV_GPRE + 1, :]
    h = hn * (1.0 + pv_ref[PV_SCALE:PV_SCALE + 1, :]) + pv_ref[PV_SHIFT:PV_SHIFT + 1, :]
    return h, n, r, hn


def _prenorm_bwd(dh, dxn, x, pv_ref, pg_ref):
    _, n, r, hn = _prenorm(x, pv_ref)
    pg_ref[PV_SHIFT:PV_SHIFT + 1, :] += _colsum(dh)
    pg_ref[PV_SCALE:PV_SCALE + 1, :] += _colsum(dh * hn)
    dhn = dh * (1.0 + pv_ref[PV_SCALE:PV_SCALE + 1, :])
    pg_ref[PV_GPRE:PV_GPRE + 1, :] += _colsum(dhn * n)
    dn = dhn * pv_ref[PV_GPRE:PV_GPRE + 1, :]
    return dxn + _rms_bwd(dn, n, r)


def _postnorm_res(x, f, pv_ref, coef):
    nf, _ = _rms(f)
    return x + (coef * (1.0 + pv_ref[PV_GATE:PV_GATE + 1, :])) * (nf * pv_ref[PV_GPOST:PV_GPOST + 1, :])


def _postnorm_bwd(dxn, f, pv_ref, pg_ref, coef):
    nf, rf = _rms(f)
    g_post = pv_ref[PV_GPOST:PV_GPOST + 1, :]
    pg_ref[PV_GATE:PV_GATE + 1, :] += _colsum(dxn * (nf * g_post)) * coef
    dnfg = dxn * (coef * (1.0 + pv_ref[PV_GATE:PV_GATE + 1, :]))
    pg_ref[PV_GPOST:PV_GPOST + 1, :] += _colsum(dnfg * nf)
    return _rms_bwd(dnfg * g_post, nf, rf)


def ffn_fwd(lk, x, pv, wg_in, wg_out, tm=512):
    T, D = x.shape
    nj = 4

    def body(lk_ref, x_ref, pv_ref, win_ref, wout_ref, ab_ref, f_ref, xn_ref, h_sc, acc):
        j = pl.program_id(1)

        @pl.when(j == 0)
        def _():
            h, _, _, _ = _prenorm(x_ref[...], pv_ref)
            h_sc[...] = h.astype(BF16)
            acc[...] = jnp.zeros_like(acc)

        h = h_sc[...]
        a = _dot(h, win_ref[0])
        b = _dot(h, win_ref[1])
        ab_ref[0] = a.astype(BF16)
        ab_ref[1] = b.astype(BF16)
        act = (a * _sigmoid(a) * b).astype(BF16)
        acc[...] += _dot(act, wout_ref[...])

        @pl.when(j == nj - 1)
        def _():
            f = acc[...]
            f_ref[...] = f
            xn_ref[...] = _postnorm_res(x_ref[...], f, pv_ref, 0.5)

    grid_spec = pltpu.PrefetchScalarGridSpec(
        num_scalar_prefetch=1, grid=(T // tm, nj),
        in_specs=[
            pl.BlockSpec((tm, D), lambda i, j, lk: (i, 0)),
            pl.BlockSpec((8, D), lambda i, j, lk: (0, 0)),
            pl.BlockSpec((None, None, 2, None, D, FF_BLK), lambda i, j, lk: (lk[0], lk[1], 0, j, 0, 0)),
            pl.BlockSpec((None, None, None, FF_BLK, D), lambda i, j, lk: (lk[0], lk[1], j, 0, 0)),
        ],
        out_specs=[
            pl.BlockSpec((2, None, tm, FF_BLK), lambda i, j, lk: (0, j, i, 0)),
            pl.BlockSpec((tm, D), lambda i, j, lk: (i, 0)),
            pl.BlockSpec((tm, D), lambda i, j, lk: (i, 0)),
        ],
        scratch_shapes=[pltpu.VMEM((tm, D), BF16), pltpu.VMEM((tm, D), F32)],
    )
    return pl.pallas_call(
        body, name="ffn_fwd", grid_spec=grid_spec,
        out_shape=[jax.ShapeDtypeStruct((2, nj, T, FF_BLK), BF16),
                   jax.ShapeDtypeStruct((T, D), F32), jax.ShapeDtypeStruct((T, D), F32)],
        compiler_params=_cparams(("arbitrary", "arbitrary")),
    )(lk, x, pv, wg_in, wg_out)


def ffn_bwd_act(lk, dxn, x, f, pv, ab, wg_in, wg_out, tm=512):
    T, D = x.shape
    nj = 4

    def body(lk_ref, dxn_ref, x_ref, f_ref, pv_ref, ab_ref, win_ref, wout_ref,
             dab_ref, h_ref, df_ref, dx_ref, pg_ref, dacc):
        i, j = pl.program_id(0), pl.program_id(1)

        @pl.when((i == 0) & (j == 0))
        def _():
            pg_ref[...] = jnp.zeros_like(pg_ref)

        @pl.when(j == 0)
        def _():
            df = _postnorm_bwd(dxn_ref[...], f_ref[...], pv_ref, pg_ref, 0.5)
            df_ref[...] = df.astype(BF16)
            h, _, _, _ = _prenorm(x_ref[...], pv_ref)
            h_ref[...] = h.astype(BF16)
            dacc[...] = jnp.zeros_like(dacc)

        dact = _dot_nt(df_ref[...], wout_ref[...])
        a = ab_ref[0].astype(F32)
        b = ab_ref[1].astype(F32)
        sig = _sigmoid(a)
        s = a * sig
        da = (dact * b * (sig * (1.0 + a * (1.0 - sig)))).astype(BF16)
        db = (dact * s).astype(BF16)
        dab_ref[0] = da
        dab_ref[1] = db
        dacc[...] += _dot_nt(da, win_ref[0]) + _dot_nt(db, win_ref[1])

        @pl.when(j == nj - 1)
        def _():
            dx_ref[...] = _prenorm_bwd(dacc[...], dxn_ref[...], x_ref[...], pv_ref, pg_ref)

    tile = pl.BlockSpec((tm, D), lambda i, j, lk: (i, 0))
    grid_spec = pltpu.PrefetchScalarGridSpec(
        num_scalar_prefetch=1, grid=(T // tm, nj),
        in_specs=[
            tile, tile, tile,
            pl.BlockSpec((8, D), lambda i, j, lk: (0, 0)),
            pl.BlockSpec((2, None, tm, FF_BLK), lambda i, j, lk: (0, j, i, 0)),
            pl.BlockSpec((None, None, 2, None, D, FF_BLK), lambda i, j, lk: (lk[0], lk[1], 0, j, 0, 0)),
            pl.BlockSpec((None, None, None, FF_BLK, D), lambda i, j, lk: (lk[0], lk[1], j, 0, 0)),
        ],
        out_specs=[
            pl.BlockSpec((2, None, tm, FF_BLK), lambda i, j, lk: (0, j, i, 0)),
            tile, tile, tile,
            pl.BlockSpec((8, D), lambda i, j, lk: (0, 0)),
        ],
        scratch_shapes=[pltpu.VMEM((tm, D), F32)],
    )
    return pl.pallas_call(
        body, name="ffn_bwd_act", grid_spec=grid_spec,
        out_shape=[jax.ShapeDtypeStruct((2, nj, T, FF_BLK), BF16),
                   jax.ShapeDtypeStruct((T, D), BF16), jax.ShapeDtypeStruct((T, D), BF16),
                   jax.ShapeDtypeStruct((T, D), F32), jax.ShapeDtypeStruct((8, D), F32)],
        compiler_params=_cparams(("arbitrary", "arbitrary")),
    )(lk, dxn, x, f, pv, ab, wg_in, wg_out)


def ffn_bwd_w(h, df, ab, dab, tm=512):
    T, D = h.shape
    nj, ni = 4, T // tm

    def body(h_ref, df_ref, ab_ref, dab_ref, gin_ref, gout_ref, acc_in, acc_out):
        i = pl.program_id(1)

        @pl.when(i == 0)
        def _():
            acc_in[...] = jnp.zeros_like(acc_in)
            acc_out[...] = jnp.zeros_like(acc_out)

        h = h_ref[...]
        acc_in[0] += _dot_tn(h, dab_ref[0])
        acc_in[1] += _dot_tn(h, dab_ref[1])
        a = ab_ref[0].astype(F32)
        b = ab_ref[1].astype(F32)
        act = (a * _sigmoid(a) * b).astype(BF16)
        acc_out[...] += _dot_tn(act, df_ref[...])

        @pl.when(i == ni - 1)
        def _():
            gin_ref[...] = acc_in[...].astype(BF16)
            gout_ref[...] = acc_out[...].astype(BF16)

    tile = pl.BlockSpec((tm, D), lambda j, i: (i, 0))
    blk = pl.BlockSpec((2, None, tm, FF_BLK), lambda j, i: (0, j, i, 0))
    return pl.pallas_call(
        body, name="ffn_bwd_w", grid=(nj, ni),
        in_specs=[tile, tile, blk, blk],
        out_specs=[pl.BlockSpec((2, None, D, FF_BLK), lambda j, i: (0, j, 0, 0)),
                   pl.BlockSpec((None, FF_BLK, D), lambda j, i: (j, 0, 0))],
        out_shape=[jax.ShapeDtypeStruct((2, nj, D, FF_BLK), BF16),
                   jax.ShapeDtypeStruct((nj, FF_BLK, D), BF16)],
        scratch_shapes=[pltpu.VMEM((2, D, FF_BLK), F32), pltpu.VMEM((FF_BLK, D), F32)],
        compiler_params=_cparams(("arbitrary", "arbitrary")),
    )(h, df, ab, dab)


def mix_in_fwd(l, x, pv, wg, tm=512):
    T, D = x.shape

    def body(l_ref, x_ref, pv_ref, w_ref, p_ref, h_sc):
        @pl.when(pl.program_id(1) == 0)
        def _():
            h, _, _, _ = _prenorm(x_ref[...], pv_ref)
            h_sc[...] = h.astype(BF16)

        p_ref[...] = _dot(h_sc[...], w_ref[...])

    grid_spec = pltpu.PrefetchScalarGridSpec(
        num_scalar_prefetch=1, grid=(T // tm, N_DEV),
        in_specs=[pl.BlockSpec((tm, D), lambda i, j, l: (i, 0)),
                  pl.BlockSpec((8, D), lambda i, j, l: (0, 0)),
                  pl.BlockSpec((None, None, D, IN_BLK), lambda i, j, l: (l[0], j, 0, 0))],
        out_specs=pl.BlockSpec((tm, IN_BLK), lambda i, j, l: (i, j)),
        scratch_shapes=[pltpu.VMEM((tm, D), BF16)],
    )
    return pl.pallas_call(
        body, name="mix_in_fwd", grid_spec=grid_spec,
        out_shape=jax.ShapeDtypeStruct((T, IN_COLS), F32),
        compiler_params=_cparams(("arbitrary", "arbitrary")),
    )(l, x, pv, wg)


def mix_in_bwd_act(l, dp, dxn, x, pv, wg, tm=512):
    T, D = x.shape

    def body(l_ref, dp_ref, dxn_ref, x_ref, pv_ref, w_ref, dx_ref, h_ref, pg_ref, dacc):
        i, j = pl.program_id(0), pl.program_id(1)

        @pl.when((i == 0) & (j == 0))
        def _():
            pg_ref[...] = jnp.zeros_like(pg_ref)

        @pl.when(j == 0)
        def _():
            dacc[...] = jnp.zeros_like(dacc)

        dacc[...] += _dot_nt(dp_ref[...], w_ref[...])

        @pl.when(j == N_DEV - 1)
        def _():
            h, _, _, _ = _prenorm(x_ref[...], pv_ref)
            h_ref[...] = h.astype(BF16)
            dx_ref[...] = _prenorm_bwd(dacc[...], dxn_ref[...], x_ref[...], pv_ref, pg_ref)

    tile = pl.BlockSpec((tm, D), lambda i, j, l: (i, 0))
    grid_spec = pltpu.PrefetchScalarGridSpec(
        num_scalar_prefetch=1, grid=(T // tm, N_DEV),
        in_specs=[pl.BlockSpec((tm, IN_BLK), lambda i, j, l: (i, j)), tile, tile,
                  pl.BlockSpec((8, D), lambda i, j, l: (0, 0)),
                  pl.BlockSpec((None, None, D, IN_BLK), lambda i, j, l: (l[0], j, 0, 0))],
        out_specs=[tile, tile, pl.BlockSpec((8, D), lambda i, j, l: (0, 0))],
        scratch_shapes=[pltpu.VMEM((tm, D), F32)],
    )
    return pl.pallas_call(
        body, name="mix_in_bwd_act", grid_spec=grid_spec,
        out_shape=[jax.ShapeDtypeStruct((T, D), F32), jax.ShapeDtypeStruct((T, D), BF16),
                   jax.ShapeDtypeStruct((8, D), F32)],
        compiler_params=_cparams(("arbitrary", "arbitrary")),
    )(l, dp, dxn, x, pv, wg)


def matmul_tn(a, b, tn, tm=512):
    T, M = a.shape
    N = b.shape[1]
    ni = T // tm

    def body(a_ref, b_ref, o_ref, acc):
        i = pl.program_id(1)

        @pl.when(i == 0)
        def _():
            acc[...] = jnp.zeros_like(acc)

        acc[...] += _dot_tn(a_ref[...], b_ref[...])

        @pl.when(i == ni - 1)
        def _():
            o_ref[...] = acc[...].astype(o_ref.dtype)

    return pl.pallas_call(
        body, name="matmul_tn", grid=(N // tn, ni),
        in_specs=[pl.BlockSpec((tm, M), lambda j, i: (i, 0)), pl.BlockSpec((tm, tn), lambda j, i: (i, j))],
        out_specs=pl.BlockSpec((None, M, tn), lambda j, i: (j, 0, 0)),
        out_shape=jax.ShapeDtypeStruct((N // tn, M, tn), BF16),
        scratch_shapes=[pltpu.VMEM((M, tn), F32)],
        compiler_params=_cparams(("arbitrary", "arbitrary")),
    )(a, b)


SEQ_CHUNK = 256
HALO = 16


def _shift_down(ext, d):
    return pltpu.roll(ext, d, 0)


def _shift_up(ext, d):
    return pltpu.roll(ext, ext.shape[0] - d, 0)


def _rows_with_lead(load, c, width):
    t0 = c * SEQ_CHUNK
    if c == 0:
        return jnp.concatenate([jnp.zeros((HALO, width), F32), load(0, SEQ_CHUNK)], axis=0)
    return load(t0 - HALO, SEQ_CHUNK + HALO)


def _rows_with_tail(load, c, n_chunks, width):
    t0 = c * SEQ_CHUNK
    if c == n_chunks - 1:
        return jnp.concatenate([load(t0, SEQ_CHUNK), jnp.zeros((HALO, width), F32)], axis=0)
    return load(t0, SEQ_CHUNK + HALO)


def conv_fwd(l, p, conv_w):
    T = p.shape[0]
    W = MIX_W
    nC = T // SEQ_CHUNK

    def body(l_ref, p_ref, w_ref, za_ref):
        w0, w1, w2 = w_ref[0:1, :], w_ref[1:2, :], w_ref[2:3, :]
        for c in range(nC):
            ext = _rows_with_lead(lambda s, n: p_ref[s:s + n, W:2 * W] * p_ref[s:s + n, 2 * W:3 * W], c, W)
            y = w2 * ext + w1 * _shift_down(ext, 1) + w0 * _shift_down(ext, 2)
            t0 = c * SEQ_CHUNK
            za_ref[t0:t0 + SEQ_CHUNK, :] = (p_ref[t0:t0 + SEQ_CHUNK, 0:W] * y[HALO:]).astype(BF16)

    grid_spec = pltpu.PrefetchScalarGridSpec(
        num_scalar_prefetch=1, grid=(1,),
        in_specs=[pl.BlockSpec((T, 3 * W), lambda i, l: (0, 0)),
                  pl.BlockSpec((None, 8, W), lambda i, l: (l[0], 0, 0))],
        out_specs=pl.BlockSpec((T, W), lambda i, l: (0, 0)),
    )
    return pl.pallas_call(
        body, name="conv_fwd", grid_spec=grid_spec,
        out_shape=jax.ShapeDtypeStruct((T, W), BF16),
        compiler_params=_cparams(("arbitrary",)),
    )(l, p, conv_w)


def conv_bwd(l, p, dza, conv_w):
    T = p.shape[0]
    W = MIX_W
    nC = T // SEQ_CHUNK

    def body(l_ref, p_ref, dza_ref, w_ref, dp_ref, dw_ref):
        w0, w1, w2 = w_ref[0:1, :], w_ref[1:2, :], w_ref[2:3, :]
        dw = [jnp.zeros((1, W), F32) for _ in range(3)]
        for c in range(nC):
            t0 = c * SEQ_CHUNK
            ext = _rows_with_lead(lambda s, n: p_ref[s:s + n, W:2 * W] * p_ref[s:s + n, 2 * W:3 * W], c, W)
            u1, u2 = _shift_down(ext, 1)[HALO:], _shift_down(ext, 2)[HALO:]
            u0 = ext[HALO:]
            y = w2 * u0 + w1 * u1 + w0 * u2
            dza_c = dza_ref[t0:t0 + SEQ_CHUNK, :]
            dy = dza_c * p_ref[t0:t0 + SEQ_CHUNK, 0:W]
            dw[0] += _colsum(dy * u2)
            dw[1] += _colsum(dy * u1)
            dw[2] += _colsum(dy * u0)
            dye = _rows_with_tail(lambda s, n: dza_ref[s:s + n, :] * p_ref[s:s + n, 0:W], c, nC, W)
            du = (w2 * dye + w1 * _shift_up(dye, 1) + w0 * _shift_up(dye, 2))[:SEQ_CHUNK]
            dp_ref[t0:t0 + SEQ_CHUNK, 0:W] = (dza_c * y).astype(BF16)
            dp_ref[t0:t0 + SEQ_CHUNK, W:2 * W] = (du * p_ref[t0:t0 + SEQ_CHUNK, 2 * W:3 * W]).astype(BF16)
            dp_ref[t0:t0 + SEQ_CHUNK, 2 * W:3 * W] = (du * p_ref[t0:t0 + SEQ_CHUNK, W:2 * W]).astype(BF16)
        dw_ref[...] = jnp.concatenate(dw + [jnp.zeros((5, W), F32)], axis=0)

    grid_spec = pltpu.PrefetchScalarGridSpec(
        num_scalar_prefetch=1, grid=(1,),
        in_specs=[pl.BlockSpec((T, 3 * W), lambda i, l: (0, 0)),
                  pl.BlockSpec((T, W), lambda i, l: (0, 0)),
                  pl.BlockSpec((None, 8, W), lambda i, l: (l[0], 0, 0))],
        out_specs=[pl.BlockSpec((T, 3 * W), lambda i, l: (0, 0)), pl.BlockSpec((8, W), lambda i, l: (0, 0))],
    )
    return pl.pallas_call(
        body, name="conv_bwd", grid_spec=grid_spec,
        out_shape=[jax.ShapeDtypeStruct((T, 3 * W), BF16), jax.ShapeDtypeStruct((8, W), F32)],
        compiler_params=_cparams(("arbitrary",)),
    )(l, p, dza, conv_w)


def _pool_consts(rows, t0):
    lane = lax.broadcasted_iota(jnp.int32, (rows, MIX_W), 1)
    t = lax.broadcasted_iota(jnp.int32, (rows, MIX_W), 0) + t0
    win = jnp.where(lane < 64, 2, jnp.where(lane < 128, 4, jnp.where(lane < 192, 8, 16)))
    inv = 1.0 / jnp.minimum(t + 1, win).astype(F32)
    return lane, inv


def _pick_window(lane, s2, s4, s8, s16):
    return jnp.where(lane < 64, s2, jnp.where(lane < 128, s4, jnp.where(lane < 192, s8, s16)))


def _pooled_chunk(u_ref, c):
    ext = _rows_with_lead(lambda s, n: u_ref[s:s + n, :], c, MIX_W)
    s2 = ext + _shift_down(ext, 1)
    s4 = s2 + _shift_down(s2, 2)
    s8 = s4 + _shift_down(s4, 4)
    s16 = s8 + _shift_down(s8, 8)
    lane, inv = _pool_consts(SEQ_CHUNK, c * SEQ_CHUNK)
    return _pick_window(lane, s2[HALO:], s4[HALO:], s8[HALO:], s16[HALO:]) * inv - ext[HALO:]


def pool_fwd(l, p, w_bd, scale):
    T = p.shape[0]
    W = MIX_W
    nC = T // SEQ_CHUNK

    def body(l_ref, u_ref, w_ref, sc_ref, z_ref):
        for c in range(nC):
            pooled = _pooled_chunk(u_ref, c)
            mixed = _dot(pooled.astype(BF16), w_ref[...])
            z_ref[c * SEQ_CHUNK:(c + 1) * SEQ_CHUNK, :] = (mixed * sc_ref[0:1, :]).astype(BF16)

    grid_spec = pltpu.PrefetchScalarGridSpec(
        num_scalar_prefetch=1, grid=(1,),
        in_specs=[pl.BlockSpec((T, W), lambda i, l: (0, 4)),
                  pl.BlockSpec((None, W, W), lambda i, l: (l[0], 0, 0)),
                  pl.BlockSpec((None, 8, W), lambda i, l: (l[0], 0, 0))],
        out_specs=pl.BlockSpec((T, W), lambda i, l: (0, 0)),
    )
    return pl.pallas_call(
        body, name="pool_fwd", grid_spec=grid_spec,
        out_shape=jax.ShapeDtypeStruct((T, W), BF16),
        compiler_params=_cparams(("arbitrary",)),
    )(l, p, w_bd, scale)


def pool_bwd(l, p, dz, w_bd, scale):
    T = p.shape[0]
    W = MIX_W
    nC = T // SEQ_CHUNK

    def body(l_ref, u_ref, dz_ref, w_ref, sc_ref, du_ref, dw_ref, dsc_ref, e_sc, dpl_sc):
        dw = jnp.zeros((W, W), F32)
        dsc = jnp.zeros((1, W), F32)
        for c in range(nC):
            t0 = c * SEQ_CHUNK
            pooled = _pooled_chunk(u_ref, c).astype(BF16)
            mixed = _dot(pooled, w_ref[...])
            dz_c = dz_ref[t0:t0 + SEQ_CHUNK, :]
            dsc += _colsum(dz_c * mixed)
            dmixed = (dz_c * sc_ref[0:1, :]).astype(BF16)
            dw += _dot_tn(pooled, dmixed)
            dpooled = _dot_nt(dmixed, w_ref[...])
            _, inv = _pool_consts(SEQ_CHUNK, t0)
            dpl_sc[t0:t0 + SEQ_CHUNK, :] = dpooled
            e_sc[t0:t0 + SEQ_CHUNK, :] = dpooled * inv
        for c in range(nC):
            t0 = c * SEQ_CHUNK
            ext = _rows_with_tail(lambda s, n: e_sc[s:s + n, :], c, nC, W)
            s2 = ext + _shift_up(ext, 1)
            s4 = s2 + _shift_up(s2, 2)
            s8 = s4 + _shift_up(s4, 4)
            s16 = s8 + _shift_up(s8, 8)
            lane, _ = _pool_consts(SEQ_CHUNK, t0)
            n = SEQ_CHUNK
            du = _pick_window(lane, s2[:n], s4[:n], s8[:n], s16[:n]) - dpl_sc[t0:t0 + SEQ_CHUNK, :]
            du_ref[t0:t0 + SEQ_CHUNK, :] = du.astype(BF16)
        dw_ref[...] = dw
        dsc_ref[...] = jnp.concatenate([dsc, jnp.zeros((7, W), F32)], axis=0)

    grid_spec = pltpu.PrefetchScalarGridSpec(
        num_scalar_prefetch=1, grid=(1,),
        in_specs=[pl.BlockSpec((T, W), lambda i, l: (0, 4)),
                  pl.BlockSpec((T, W), lambda i, l: (0, 0)),
                  pl.BlockSpec((None, W, W), lambda i, l: (l[0], 0, 0)),
                  pl.BlockSpec((None, 8, W), lambda i, l: (l[0], 0, 0))],
        out_specs=[pl.BlockSpec((T, W), lambda i, l: (0, 0)), pl.BlockSpec((W, W), lambda i, l: (0, 0)),
                   pl.BlockSpec((8, W), lambda i, l: (0, 0))],
        scratch_shapes=[pltpu.VMEM((T, W), F32), pltpu.VMEM((T, W), F32)],
    )
    return pl.pallas_call(
        body, name="pool_bwd", grid_spec=grid_spec,
        out_shape=[jax.ShapeDtypeStruct((T, W), BF16), jax.ShapeDtypeStruct((W, W), F32),
                   jax.ShapeDtypeStruct((8, W), F32)],
        compiler_params=_cparams(("arbitrary",)),
    )(l, p, dz, w_bd, scale)


def _s5_disc(lre, lim, ldt):
    lr = jnp.minimum(lre, DT_LAMBDA_RE_MAX)
    dt = jnp.exp(ldt)
    mag = jnp.exp(lr * dt)
    a_re = mag * jnp.cos(lim * dt)
    a_im = mag * jnp.sin(lim * dt)
    den = lr * lr + lim * lim
    nr = a_re - 1.0
    return a_re, a_im, (nr * lr + a_im * lim) / den, (a_im * lr - nr * lim) / den


def _bd_mask(shape, row_blk, col_blk):
    r = lax.broadcasted_iota(jnp.int32, shape, 0) >> (row_blk.bit_length() - 1)
    c = lax.broadcasted_iota(jnp.int32, shape, 1) >> (col_blk.bit_length() - 1)
    return r == c


def s5_params(lam, b_t, c_t):
    L = lam.shape[0]

    def body(lam_ref, b_ref, c_ref, a_ref, bbd_ref, cbd_ref):
        a_re, a_im, f_re, f_im = _s5_disc(lam_ref[0:1, :], lam_ref[1:2, :], lam_ref[2:3, :])
        a_ref[...] = jnp.concatenate([a_re, a_im, jnp.zeros((6, SSM_W), F32)], axis=0)
        mb = _bd_mask((MIX_W, SSM_W), SSM_GROUP, SSM_STATE)
        bbd_ref[0] = jnp.where(mb, f_re * b_ref[0] - f_im * b_ref[1], 0.0).astype(BF16)
        bbd_ref[1] = jnp.where(mb, f_re * b_ref[1] + f_im * b_ref[0], 0.0).astype(BF16)
        mc = _bd_mask((SSM_W, MIX_W), SSM_STATE, SSM_GROUP)
        cbd_ref[0] = jnp.where(mc, c_ref[0], 0.0).astype(BF16)
        cbd_ref[1] = jnp.where(mc, c_ref[1], 0.0).astype(BF16)

    return pl.pallas_call(
        body, name="s5_params", grid=(L,),
        in_specs=[pl.BlockSpec((None, 8, SSM_W), lambda l: (l, 0, 0)),
                  pl.BlockSpec((None, 2, MIX_W, SSM_W), lambda l: (l, 0, 0, 0)),
                  pl.BlockSpec((None, 2, SSM_W, MIX_W), lambda l: (l, 0, 0, 0))],
        out_specs=[pl.BlockSpec((None, 8, SSM_W), lambda l: (l, 0, 0)),
                   pl.BlockSpec((None, 2, MIX_W, SSM_W), lambda l: (l, 0, 0, 0)),
                   pl.BlockSpec((None, 2, SSM_W, MIX_W), lambda l: (l, 0, 0, 0))],
        out_shape=[jax.ShapeDtypeStruct((L, 8, SSM_W), F32),
                   jax.ShapeDtypeStruct((L, 2, MIX_W, SSM_W), BF16),
                   jax.ShapeDtypeStruct((L, 2, SSM_W, MIX_W), BF16)],
        compiler_params=_cparams(("arbitrary",)),
    )(lam, b_t, c_t)


def s5_params_bwd(lam, b_t, gb, gc, da):
    L = lam.shape[0]

    def body(lam_ref, b_ref, gb_ref, gc_ref, da_ref, dlam_ref, db_ref, dc_ref, dgrp_ref):
        lre, lim, ldt = lam_ref[0:1, :], lam_ref[1:2, :], lam_ref[2:3, :]
        (a_re, a_im, f_re, f_im), vjp = jax.vjp(_s5_disc, lre, lim, ldt)
        mb = _bd_mask((MIX_W, SSM_W), SSM_GROUP, SSM_STATE)
        gbr = jnp.where(mb, gb_ref[0], 0.0)
        gbi = jnp.where(mb, gb_ref[1], 0.0)
        df_re = _colsum(gbr * b_ref[0] + gbi * b_ref[1])
        df_im = _colsum(gbi * b_ref[0] - gbr * b_ref[1])
        db_ref[0] = f_re * gbr + f_im * gbi
        db_ref[1] = f_re * gbi - f_im * gbr
        mc = _bd_mask((SSM_W, MIX_W), SSM_STATE, SSM_GROUP)
        dc_ref[0] = jnp.where(mc, gc_ref[0], 0.0)
        dc_ref[1] = jnp.where(mc, gc_ref[1], 0.0)
        dlre, dlim, dldt = vjp((da_ref[0:1, :], da_ref[1:2, :], df_re, df_im))
        dl = jnp.concatenate([dlre, dlim, dldt, jnp.zeros((5, SSM_W), F32)], axis=0)
        dlam_ref[...] = dl
        grp = jnp.where(_bd_mask((SSM_W, 128), SSM_STATE, 1), 1.0, 0.0)
        dgrp_ref[...] = jnp.dot(dl, grp, preferred_element_type=F32, precision=lax.Precision.HIGHEST)

    vec = pl.BlockSpec((None, 8, SSM_W), lambda l: (l, 0, 0))
    bsp = pl.BlockSpec((None, 2, MIX_W, SSM_W), lambda l: (l, 0, 0, 0))
    csp = pl.BlockSpec((None, 2, SSM_W, MIX_W), lambda l: (l, 0, 0, 0))
    return pl.pallas_call(
        body, name="s5_params_bwd", grid=(L,),
        in_specs=[vec, bsp, bsp, csp, vec],
        out_specs=[vec, bsp, csp, pl.BlockSpec((None, 8, 128), lambda l: (l, 0, 0))],
        out_shape=[jax.ShapeDtypeStruct((L, 8, SSM_W), F32),
                   jax.ShapeDtypeStruct((L, 2, MIX_W, SSM_W), F32),
                   jax.ShapeDtypeStruct((L, 2, SSM_W, MIX_W), F32),
                   jax.ShapeDtypeStruct((L, 8, 128), F32)],
        compiler_params=_cparams(("arbitrary",)),
    )(lam, b_t, gb, gc, da)


def s5_bu(l, p, b_bd, tm=512):
    T = p.shape[0]

    def body(l_ref, u_ref, b_ref, bu_ref):
        u = u_ref[...].astype(BF16)
        bu_ref[0] = _dot(u, b_ref[0])
        bu_ref[1] = _dot(u, b_ref[1])

    grid_spec = pltpu.PrefetchScalarGridSpec(
        num_scalar_prefetch=1, grid=(T // tm,),
        in_specs=[pl.BlockSpec((tm, MIX_W), lambda i, l: (i, 3)),
                  pl.BlockSpec((None, 2, MIX_W, SSM_W), lambda i, l: (l[0], 0, 0, 0))],
        out_specs=pl.BlockSpec((2, tm, SSM_W), lambda i, l: (0, i, 0)),
    )
    return pl.pallas_call(
        body, name="s5_bu", grid_spec=grid_spec,
        out_shape=jax.ShapeDtypeStruct((2, T, SSM_W), F32),
        compiler_params=_cparams(("arbitrary",)),
    )(l, p, b_bd)


def s5_scan(l, avec, xs, reverse):
    T = xs.shape[1]
    CH = SEQ_CHUNK
    nC = T // CH
    LW = 128
    n_steps = CH.bit_length() - 1

    def body(l_ref, a_ref, x_ref, s_ref):
        ar = a_ref[0:1, :]
        ai = -a_ref[1:2, :] if reverse else a_ref[1:2, :]
        pows = [(ar, ai)]
        for _ in range(n_steps - 1):
            r, i = pows[-1]
            pows.append((r * r - i * i, 2.0 * r * i))
        row = lax.broadcasted_iota(jnp.int32, (CH, LW), 0)

        def local_scan(re, im):
            for k in range(n_steps):
                d = 1 << k
                pr, pi = pows[k]
                if reverse:
                    keep = row < CH - d
                    sr, si = _shift_up(re, d), _shift_up(im, d)
                else:
                    keep = row >= d
                    sr, si = _shift_down(re, d), _shift_down(im, d)
                sr = jnp.where(keep, sr, 0.0)
                si = jnp.where(keep, si, 0.0)
                re, im = re + pr * sr - pi * si, im + pr * si + pi * sr
            return re, im

        edge = CH - 1 if reverse else 0
        pw_re, pw_im = local_scan(jnp.where(row == edge, ar, 0.0), jnp.where(row == edge, ai, 0.0))
        last = 0 if reverse else CH - 1

        def chunk(c, carry):
            cr, ci = carry
            cc = nC - 1 - c if reverse else c
            t0 = pl.multiple_of(cc * CH, CH)
            re, im = local_scan(x_ref[0, pl.ds(t0, CH), :], x_ref[1, pl.ds(t0, CH), :])
            re2 = re + pw_re * cr - pw_im * ci
            im2 = im + pw_re * ci + pw_im * cr
            s_ref[0, pl.ds(t0, CH), :] = re2
            s_ref[1, pl.ds(t0, CH), :] = im2
            return re2[last:last + 1, :], im2[last:last + 1, :]

        lax.fori_loop(0, nC, chunk, (jnp.zeros((1, LW), F32), jnp.zeros((1, LW), F32)))

    grid_spec = pltpu.PrefetchScalarGridSpec(
        num_scalar_prefetch=1, grid=(SSM_W // LW,),
        in_specs=[pl.BlockSpec((None, 8, LW), lambda g, l: (l[0], 0, g)),
                  pl.BlockSpec((2, T, LW), lambda g, l: (0, 0, g))],
        out_specs=pl.BlockSpec((2, T, LW), lambda g, l: (0, 0, g)),
    )
    return pl.pallas_call(
        body, name="s5_scan_rev" if reverse else "s5_scan_fwd", grid_spec=grid_spec,
        out_shape=jax.ShapeDtypeStruct((2, T, SSM_W), F32),
        compiler_params=_cparams(("arbitrary",)),
    )(l, avec, xs)


_GELU_C = 0.7978845608028654
_GELU_K = 0.044715


def _s5_y(u, s_ref, c_ref, d_row):
    y = _dot(s_ref[0].astype(BF16), c_ref[0]) - _dot(s_ref[1].astype(BF16), c_ref[1])
    return y + d_row * u


def s5_out(l, p, s, c_bd, ssm_d, tm=512):
    T = p.shape[0]

    def body(l_ref, u_ref, s_ref, c_ref, d_ref, yg_ref):
        y = _s5_y(u_ref[...], s_ref, c_ref, d_ref[0:1, :])
        th = jnp.tanh(_GELU_C * (y + _GELU_K * y * y * y))
        yg_ref[...] = (0.5 * y * (1.0 + th)).astype(BF16)

    grid_spec = pltpu.PrefetchScalarGridSpec(
        num_scalar_prefetch=1, grid=(T // tm,),
        in_specs=[pl.BlockSpec((tm, MIX_W), lambda i, l: (i, 3)),
                  pl.BlockSpec((2, tm, SSM_W), lambda i, l: (0, i, 0)),
                  pl.BlockSpec((None, 2, SSM_W, MIX_W), lambda i, l: (l[0], 0, 0, 0)),
                  pl.BlockSpec((None, 8, MIX_W), lambda i, l: (l[0], 0, 0))],
        out_specs=pl.BlockSpec((tm, MIX_W), lambda i, l: (i, 0)),
    )
    return pl.pallas_call(
        body, name="s5_out", grid_spec=grid_spec,
        out_shape=jax.ShapeDtypeStruct((T, MIX_W), BF16),
        compiler_params=_cparams(("arbitrary",)),
    )(l, p, s, c_bd, ssm_d)


def s5_bwd_y(l, p, s, dyg, c_bd, ssm_d, tm=512):
    T = p.shape[0]

    def body(l_ref, u_ref, s_ref, dyg_ref, c_ref, d_ref, ds_ref, du_ref, gc_ref, dd_ref):
        @pl.when(pl.program_id(0) == 0)
        def _():
            gc_ref[...] = jnp.zeros_like(gc_ref)
            dd_ref[...] = jnp.zeros_like(dd_ref)

        u = u_ref[...]
        y = _s5_y(u, s_ref, c_ref, d_ref[0:1, :])
        inner = _GELU_C * (y + _GELU_K * y * y * y)
        th = jnp.tanh(inner)
        dgelu = 0.5 * (1.0 + th) + 0.5 * y * (1.0 - th * th) * (_GELU_C * (1.0 + 3.0 * _GELU_K * y * y))
        dy = dyg_ref[...] * dgelu
        dd_ref[0:1, :] += _colsum(dy * u)
        du_ref[...] = dy * d_ref[0:1, :]
        dyb = dy.astype(BF16)
        ds_ref[0] = _dot_nt(dyb, c_ref[0])
        ds_ref[1] = -_dot_nt(dyb, c_ref[1])
        gc_ref[0] += _dot_tn(s_ref[0].astype(BF16), dyb)
        gc_ref[1] -= _dot_tn(s_ref[1].astype(BF16), dyb)

    grid_spec = pltpu.PrefetchScalarGridSpec(
        num_scalar_prefetch=1, grid=(T // tm,),
        in_specs=[pl.BlockSpec((tm, MIX_W), lambda i, l: (i, 3)),
                  pl.BlockSpec((2, tm, SSM_W), lambda i, l: (0, i, 0)),
                  pl.BlockSpec((tm, MIX_W), lambda i, l: (i, 0)),
                  pl.BlockSpec((None, 2, SSM_W, MIX_W), lambda i, l: (l[0], 0, 0, 0)),
                  pl.BlockSpec((None, 8, MIX_W), lambda i, l: (l[0], 0, 0))],
        out_specs=[pl.BlockSpec((2, tm, SSM_W), lambda i, l: (0, i, 0)),
                   pl.BlockSpec((tm, MIX_W), lambda i, l: (i, 0)),
                   pl.BlockSpec((2, SSM_W, MIX_W), lambda i, l: (0, 0, 0)),
                   pl.BlockSpec((8, MIX_W), lambda i, l: (0, 0))],
    )
    return pl.pallas_call(
        body, name="s5_bwd_y", grid_spec=grid_spec,
        out_shape=[jax.ShapeDtypeStruct((2, T, SSM_W), F32), jax.ShapeDtypeStruct((T, MIX_W), F32),
                   jax.ShapeDtypeStruct((2, SSM_W, MIX_W), F32), jax.ShapeDtypeStruct((8, MIX_W), F32)],
        compiler_params=_cparams(("arbitrary",)),
    )(l, p, s, dyg, c_bd, ssm_d)


def s5_bwd_u(l, p, lam_s, du_skip, b_bd, tm=512):
    T = p.shape[0]

    def body(l_ref, u_ref, ls_ref, dus_ref, b_ref, du_ref, gb_ref):
        @pl.when(pl.program_id(0) == 0)
        def _():
            gb_ref[...] = jnp.zeros_like(gb_ref)

        u = u_ref[...].astype(BF16)
        lr = ls_ref[0].astype(BF16)
        li = ls_ref[1].astype(BF16)
        gb_ref[0] += _dot_tn(u, lr)
        gb_ref[1] += _dot_tn(u, li)
        du_ref[...] = (dus_ref[...] + _dot_nt(lr, b_ref[0]) + _dot_nt(li, b_ref[1])).astype(BF16)

    grid_spec = pltpu.PrefetchScalarGridSpec(
        num_scalar_prefetch=1, grid=(T // tm,),
        in_specs=[pl.BlockSpec((tm, MIX_W), lambda i, l: (i, 3)),
                  pl.BlockSpec((2, tm, SSM_W), lambda i, l: (0, i, 0)),
                  pl.BlockSpec((tm, MIX_W), lambda i, l: (i, 0)),
                  pl.BlockSpec((None, 2, MIX_W, SSM_W), lambda i, l: (l[0], 0, 0, 0))],
        out_specs=[pl.BlockSpec((tm, MIX_W), lambda i, l: (i, 0)),
                   pl.BlockSpec((2, MIX_W, SSM_W), lambda i, l: (0, 0, 0))],
    )
    return pl.pallas_call(
        body, name="s5_bwd_u", grid_spec=grid_spec,
        out_shape=[jax.ShapeDtypeStruct((T, MIX_W), BF16), jax.ShapeDtypeStruct((2, MIX_W, SSM_W), F32)],
        compiler_params=_cparams(("arbitrary",)),
    )(l, p, lam_s, du_skip, b_bd)


def s5_bwd_a(s, lam_s):
    T = s.shape[1]
    nC = T // SEQ_CHUNK
    LW = 128

    def body(s_ref, ls_ref, da_ref):
        dre = jnp.zeros((1, LW), F32)
        dim = jnp.zeros((1, LW), F32)
        for c in range(nC):
            t0 = c * SEQ_CHUNK
            sr = _shift_down(_rows_with_lead(lambda a, n: s_ref[0, a:a + n, :], c, LW), 1)[HALO:]
            si = _shift_down(_rows_with_lead(lambda a, n: s_ref[1, a:a + n, :], c, LW), 1)[HALO:]
            lr = ls_ref[0, t0:t0 + SEQ_CHUNK, :]
            li = ls_ref[1, t0:t0 + SEQ_CHUNK, :]
            dre += _colsum(sr * lr + si * li)
            dim += _colsum(sr * li - si * lr)
        da_ref[...] = jnp.concatenate([dre, dim, jnp.zeros((6, LW), F32)], axis=0)

    blk = pl.BlockSpec((2, T, LW), lambda g: (0, 0, g))
    return pl.pallas_call(
        body, name="s5_bwd_a", grid=(SSM_W // LW,),
        in_specs=[blk, blk],
        out_specs=pl.BlockSpec((8, LW), lambda g: (0, g)),
        out_shape=jax.ShapeDtypeStruct((8, SSM_W), F32),
        compiler_params=_cparams(("arbitrary",)),
    )(s, lam_s)


SB_BLK = 128
SB_SCALE = SB_HEAD ** -0.5


def _split_bf16(x):
    hi = x.astype(BF16)
    return hi, (x - hi.astype(F32)).astype(BF16)


def _dot_split(x, tri):
    hi, lo = _split_bf16(x)
    return _dot(hi, tri) + _dot(lo, tri)


SB_SLABS = MIX_W // SB_BLK
SB_STACK = 2 * SB_SLABS * SB_BLK
SB_PAIR = 2 * SB_BLK


def _sb_valid(r0, c0):
    row = (lax.broadcasted_iota(jnp.int32, (SB_STACK, SB_BLK), 0) & (SB_BLK - 1)) + r0
    col = lax.broadcasted_iota(jnp.int32, (SB_STACK, SB_BLK), 1) + c0
    return col < row


def _sb_stack(ref, r0, scale):
    lane = lax.broadcasted_iota(jnp.int32, (SB_BLK, SB_BLK), 1)
    parts = []
    for s in range(SB_SLABS):
        blk = ref[pl.ds(r0, SB_BLK), s * SB_BLK:(s + 1) * SB_BLK] * scale
        parts += [jnp.where(lane < SB_HEAD, blk, 0.0), jnp.where(lane >= SB_HEAD, blk, 0.0)]
    return jnp.concatenate(parts, axis=0).astype(BF16)


def _sb_rows_nt(stack, ref, c0):
    return jnp.concatenate(
        [_dot_nt(stack[s * SB_PAIR:(s + 1) * SB_PAIR], ref[pl.ds(c0, SB_BLK), s * SB_BLK:(s + 1) * SB_BLK].astype(BF16))
         for s in range(SB_SLABS)], axis=0)


def _sb_wide(stack, s):
    return jnp.concatenate([stack[s * SB_PAIR:s * SB_PAIR + SB_BLK], stack[s * SB_PAIR + SB_BLK:(s + 1) * SB_PAIR]],
                           axis=1)


def _sb_logits(q_stack, k_ref, c0, valid):
    z = _sb_rows_nt(q_stack, k_ref, c0)
    sp = jnp.log(1.0 + jnp.exp(-jnp.abs(z)))
    ls_pos = jnp.minimum(z, 0.0) - sp
    lk = jnp.where(valid, jnp.minimum(-z, 0.0) - sp, 0.0)
    return z, ls_pos, lk


def _tri(lower):
    r = lax.broadcasted_iota(jnp.int32, (SB_BLK, SB_BLK), 0)
    c = lax.broadcasted_iota(jnp.int32, (SB_BLK, SB_BLK), 1)
    return jnp.where(r > c if lower else r < c, 1.0, 0.0).astype(BF16)


def sb_fwd(p):
    T = p.shape[0]
    W = MIX_W
    nB = T // SB_BLK

    def body(q_ref, k_ref, v_ref, o_ref, tot_ref, acc_sc):
        tri = _tri(True)

        def qblock(i, _):
            r0 = pl.multiple_of(i * SB_BLK, SB_BLK)
            q = _sb_stack(q_ref, r0, SB_SCALE)
            acc_sc[...] = jnp.zeros_like(acc_sc)

            def kblock(jj, run):
                c0 = pl.multiple_of((i - jj) * SB_BLK, SB_BLK)
                valid = _sb_valid(r0, c0)
                _, ls_pos, lk = _sb_logits(q, k_ref, c0, valid)
                logw = ls_pos + _dot_split(lk, tri) + run
                a = jnp.where(valid, jnp.exp(logw), 0.0).astype(BF16)
                v = _sb_stack(v_ref, c0, 1.0)
                for s in range(SB_SLABS):
                    acc_sc[:, s * SB_BLK:(s + 1) * SB_BLK] += _dot(_sb_wide(a, s), v[s * SB_PAIR:(s + 1) * SB_PAIR])
                return run + jnp.sum(lk, axis=1, keepdims=True)

            total = lax.fori_loop(0, i + 1, kblock, jnp.zeros((SB_STACK, 1), F32))
            o_ref[pl.ds(r0, SB_BLK), :] = acc_sc[...].astype(BF16)
            tot_ref[pl.ds(pl.multiple_of(i * SB_STACK, SB_STACK), SB_STACK), :] = jnp.broadcast_to(total, (SB_STACK, SB_BLK))
            return 0

        lax.fori_loop(0, nB, qblock, 0)

    return pl.pallas_call(
        body, name="sb_fwd", grid=(1,),
        in_specs=[pl.BlockSpec((T, W), lambda i: (0, 5)), pl.BlockSpec((T, W), lambda i: (0, 6)),
                  pl.BlockSpec((T, W), lambda i: (0, 7))],
        out_specs=[pl.BlockSpec((T, W), lambda i: (0, 0)), pl.BlockSpec((nB * SB_STACK, SB_BLK), lambda i: (0, 0))],
        out_shape=[jax.ShapeDtypeStruct((T, W), BF16), jax.ShapeDtypeStruct((nB * SB_STACK, SB_BLK), F32)],
        scratch_shapes=[pltpu.VMEM((SB_BLK, W), F32)],
        compiler_params=_cparams(("arbitrary",)),
    )(p, p, p)


def sb_bwd(p, do, tot):
    T = p.shape[0]
    W = MIX_W
    nB = T // SB_BLK

    def body(q_ref, k_ref, v_ref, do_ref, tot_ref, dqkv_ref, dq_sc, dk_sc, dv_sc):
        tri_gt = _tri(True)
        tri_lt = _tri(False)
        dq_sc[...] = jnp.zeros_like(dq_sc)
        dk_sc[...] = jnp.zeros_like(dk_sc)
        dv_sc[...] = jnp.zeros_like(dv_sc)
        zcol = jnp.zeros((SB_STACK, 1), F32)

        def qblock(i, _):
            r0 = pl.multiple_of(i * SB_BLK, SB_BLK)
            q = _sb_stack(q_ref, r0, SB_SCALE)
            dob = _sb_stack(do_ref, r0, 1.0)

            total = tot_ref[pl.ds(pl.multiple_of(i * SB_STACK, SB_STACK), SB_STACK), 0:1]

            def kblock(j, carry):
                pre, seen = carry
                c0 = pl.multiple_of(j * SB_BLK, SB_BLK)
                valid = _sb_valid(r0, c0)
                z, ls_pos, lk = _sb_logits(q, k_ref, c0, valid)
                seen = seen + jnp.sum(lk, axis=1, keepdims=True)
                logw = ls_pos + _dot_split(lk, tri_gt) + (total - seen)
                a = jnp.where(valid, jnp.exp(logw), 0.0)
                dlw = _sb_rows_nt(dob, v_ref, c0) * a
                g = pre + _dot_split(dlw, tri_lt)
                sig = _sigmoid(z)
                dz = jnp.where(valid, dlw * (1.0 - sig) - g * sig, 0.0).astype(BF16)
                ab = a.astype(BF16)
                km = _sb_stack(k_ref, c0, 1.0)
                for s in range(SB_SLABS):
                    pair = slice(s * SB_PAIR, (s + 1) * SB_PAIR)
                    ls = slice(s * SB_BLK, (s + 1) * SB_BLK)
                    dk_sc[pl.ds(c0, SB_BLK), ls] += _dot_tn(dz[pair], q[pair])
                    dv_sc[pl.ds(c0, SB_BLK), ls] += _dot_tn(ab[pair], dob[pair])
                    dq_sc[pl.ds(r0, SB_BLK), ls] += _dot(_sb_wide(dz, s), km[pair])
                return pre + jnp.sum(dlw, axis=1, keepdims=True), seen

            lax.fori_loop(0, i + 1, kblock, (zcol, zcol))
            return 0

        lax.fori_loop(0, nB, qblock, 0)
        dqkv_ref[:, 0:W] = (dq_sc[...] * SB_SCALE).astype(BF16)
        dqkv_ref[:, W:2 * W] = dk_sc[...].astype(BF16)
        dqkv_ref[:, 2 * W:3 * W] = dv_sc[...].astype(BF16)

    return pl.pallas_call(
        body, name="sb_bwd", grid=(1,),
        in_specs=[pl.BlockSpec((T, W), lambda i: (0, 5)), pl.BlockSpec((T, W), lambda i: (0, 6)),
                  pl.BlockSpec((T, W), lambda i: (0, 7)), pl.BlockSpec((T, W), lambda i: (0, 0)),
                  pl.BlockSpec((nB * SB_STACK, SB_BLK), lambda i: (0, 0))],
        out_specs=pl.BlockSpec((T, 3 * W), lambda i: (0, 0)),
        out_shape=jax.ShapeDtypeStruct((T, 3 * W), BF16),
        scratch_shapes=[pltpu.VMEM((T, W), F32), pltpu.VMEM((T, W), F32), pltpu.VMEM((T, W), F32)],
        compiler_params=_cparams(("arbitrary",)),
    )(p, p, p, do, tot)


def _dot_cols(a, w_ref):
    return jnp.concatenate([_dot(a, w_ref[j]) for j in range(N_DEV)], axis=1)


def _dot_cols_nt(dy, w_ref):
    n = w_ref.shape[2]
    out = _dot_nt(dy[:, 0:n], w_ref[0])
    for j in range(1, N_DEV):
        out += _dot_nt(dy[:, j * n:(j + 1) * n], w_ref[j])
    return out


def _acc_cols_tn(acc_ref, a, dy):
    n = acc_ref.shape[2]
    for j in range(N_DEV):
        acc_ref[j] += _dot_tn(a, dy[:, j * n:(j + 1) * n])


def _merge_branches(za_ref, yg_ref, z_ref, o_ref, gate_refs, wc_ref, wglu_ref, wp_ref, ws_ref):
    D = D_MODEL
    glu = _dot_cols(yg_ref[...], wglu_ref)
    glu_a, sg = glu[:, :D], _sigmoid(glu[:, D:])
    ys = [_dot_cols(za_ref[...], wc_ref), glu_a * sg, _dot_cols(z_ref[...], wp_ref), _dot_cols(o_ref[...], ws_ref)]
    gs = [_sigmoid(g[...]) for g in gate_refs]
    merged = gs[0] * ys[0] + gs[1] * ys[1] + gs[2] * ys[2] + gs[3] * ys[3]
    return ys, gs, glu_a, sg, merged


def _merge_specs(tm, D):
    W = MIX_W
    br = pl.BlockSpec((tm, W), lambda i, l: (i, 0))
    gates = [pl.BlockSpec((tm, D), functools.partial(lambda i, l, b: (i, 2 + b), b=b)) for b in range(4)]
    wsm = pl.BlockSpec((None, N_DEV, W, D // N_DEV), lambda i, l: (l[0], 0, 0, 0))
    weights = [wsm, pl.BlockSpec((None, N_DEV, W, 2 * D // N_DEV), lambda i, l: (l[0], 0, 0, 0)), wsm, wsm,
               pl.BlockSpec((None, D, D), lambda i, l: (l[0], 0, 0))]
    return [br] * 4 + gates, weights


def merge_fwd(l, p, za, yg, z, o, x, pv, wc, wglu, wp, ws, wo, tm=256):
    T, D = x.shape

    def body(l_ref, za_ref, yg_ref, z_ref, o_ref, g0, g1, g2, g3, x_ref, pv_ref,
             wc_ref, wglu_ref, wp_ref, ws_ref, wo_ref, xn_ref, m_ref):
        _, _, _, _, merged = _merge_branches(za_ref, yg_ref, z_ref, o_ref, (g0, g1, g2, g3),
                                             wc_ref, wglu_ref, wp_ref, ws_ref)
        m = _dot(merged.astype(BF16), wo_ref[...])
        m_ref[...] = m
        xn_ref[...] = _postnorm_res(x_ref[...], m, pv_ref, 1.0)

    acts, weights = _merge_specs(tm, D)
    tile = pl.BlockSpec((tm, D), lambda i, l: (i, 0))
    grid_spec = pltpu.PrefetchScalarGridSpec(
        num_scalar_prefetch=1, grid=(T // tm,),
        in_specs=acts + [tile, pl.BlockSpec((8, D), lambda i, l: (0, 0))] + weights,
        out_specs=[tile, tile],
    )
    return pl.pallas_call(
        body, name="merge_fwd", grid_spec=grid_spec,
        out_shape=[jax.ShapeDtypeStruct((T, D), F32), jax.ShapeDtypeStruct((T, D), F32)],
        compiler_params=_cparams(("arbitrary",)),
    )(l, za, yg, z, o, p, p, p, p, x, pv, wc, wglu, wp, ws, wo)


def merge_bwd(l, p, za, yg, z, o, m, dxn, pv, wc, wglu, wp, ws, wo, tm=128):
    T, D = m.shape
    W = MIX_W
    ni = T // tm

    def body(l_ref, za_ref, yg_ref, z_ref, o_ref, g0, g1, g2, g3, m_ref, dxn_ref, pv_ref,
             wc_ref, wglu_ref, wp_ref, ws_ref, wo_ref,
             dza_ref, dyg_ref, dz_ref, do_ref, dg_ref, pg_ref, gwc_ref, gwglu_ref, gwp_ref, gws_ref, gwo_ref,
             awc, awglu, awp, aws, awo):
        i = pl.program_id(0)

        @pl.when(i == 0)
        def _():
            pg_ref[...] = jnp.zeros_like(pg_ref)
            for a in (awc, awglu, awp, aws, awo):
                a[...] = jnp.zeros_like(a)

        ys, gs, glu_a, sg, merged = _merge_branches(za_ref, yg_ref, z_ref, o_ref, (g0, g1, g2, g3),
                                                    wc_ref, wglu_ref, wp_ref, ws_ref)
        dm = _postnorm_bwd(dxn_ref[...], m_ref[...], pv_ref, pg_ref, 1.0).astype(BF16)
        awo[...] += _dot_tn(merged.astype(BF16), dm)
        dmerged = _dot_nt(dm, wo_ref[...])
        for b in range(4):
            dg_ref[:, b * D:(b + 1) * D] = (dmerged * ys[b] * gs[b] * (1.0 - gs[b])).astype(BF16)
        dya = (dmerged * gs[0]).astype(BF16)
        _acc_cols_tn(awc, za_ref[...], dya)
        dza_ref[...] = _dot_cols_nt(dya, wc_ref)
        dyc = (dmerged * gs[2]).astype(BF16)
        _acc_cols_tn(awp, z_ref[...], dyc)
        dz_ref[...] = _dot_cols_nt(dyc, wp_ref)
        dyd = (dmerged * gs[3]).astype(BF16)
        _acc_cols_tn(aws, o_ref[...], dyd)
        do_ref[...] = _dot_cols_nt(dyd, ws_ref)
        dyb = dmerged * gs[1]
        dglu = jnp.concatenate([dyb * sg, dyb * glu_a * sg * (1.0 - sg)], axis=1).astype(BF16)
        _acc_cols_tn(awglu, yg_ref[...], dglu)
        dyg_ref[...] = _dot_cols_nt(dglu, wglu_ref)

        @pl.when(i == ni - 1)
        def _():
            gwc_ref[...] = awc[...].astype(BF16)
            gwglu_ref[...] = awglu[...].astype(BF16)
            gwp_ref[...] = awp[...].astype(BF16)
            gws_ref[...] = aws[...].astype(BF16)
            gwo_ref[...] = awo[...].astype(BF16)

    acts, weights = _merge_specs(tm, D)
    tile = pl.BlockSpec((tm, D), lambda i, l: (i, 0))
    br = pl.BlockSpec((tm, W), lambda i, l: (i, 0))
    full = lambda *s: pl.BlockSpec(s, lambda i, l: (0,) * len(s))
    sm, glu_s = (N_DEV, W, D // N_DEV), (N_DEV, W, 2 * D // N_DEV)
    grid_spec = pltpu.PrefetchScalarGridSpec(
        num_scalar_prefetch=1, grid=(ni,),
        in_specs=acts + [tile, tile, pl.BlockSpec((8, D), lambda i, l: (0, 0))] + weights,
        out_specs=[br, br, br, br, pl.BlockSpec((tm, 4 * D), lambda i, l: (i, 0)), full(8, D),
                   full(*sm), full(*glu_s), full(*sm), full(*sm), full(D, D)],
        scratch_shapes=[pltpu.VMEM(sm, F32), pltpu.VMEM(glu_s, F32), pltpu.VMEM(sm, F32),
                        pltpu.VMEM(sm, F32), pltpu.VMEM((D, D), F32)],
    )
    f32br = jax.ShapeDtypeStruct((T, W), F32)
    return pl.pallas_call(
        body, name="merge_bwd", grid_spec=grid_spec,
        out_shape=[f32br, f32br, f32br, f32br, jax.ShapeDtypeStruct((T, 4 * D), BF16),
                   jax.ShapeDtypeStruct((8, D), F32),
                   jax.ShapeDtypeStruct(sm, BF16), jax.ShapeDtypeStruct(glu_s, BF16),
                   jax.ShapeDtypeStruct(sm, BF16), jax.ShapeDtypeStruct(sm, BF16),
                   jax.ShapeDtypeStruct((D, D), BF16)],
        compiler_params=_cparams(("arbitrary",)),
    )(l, za, yg, z, o, p, p, p, p, m, dxn, pv, wc, wglu, wp, ws, wo)


def dp_assemble(d_conv, d_ssm, d_pool, d_qkv, d_gates, tm=512):
    T = d_conv.shape[0]
    W = MIX_W

    def body(c_ref, s_ref, p_ref, q_ref, g_ref, dp_ref):
        dp_ref[:, 0:3 * W] = c_ref[...]
        dp_ref[:, 3 * W:4 * W] = s_ref[...]
        dp_ref[:, 4 * W:5 * W] = p_ref[...]
        dp_ref[:, 5 * W:8 * W] = q_ref[...]
        dp_ref[:, GATE_OFF:] = g_ref[...]

    row = lambda w: pl.BlockSpec((tm, w), lambda i: (i, 0))
    return pl.pallas_call(
        body, name="dp_assemble", grid=(T // tm,),
        in_specs=[row(3 * W), row(W), row(W), row(3 * W), row(4 * D_MODEL)],
        out_specs=row(IN_COLS),
        out_shape=jax.ShapeDtypeStruct((T, IN_COLS), BF16),
        compiler_params=_cparams(("arbitrary",)),
    )(d_conv, d_ssm, d_pool, d_qkv, d_gates)


def loss_head(y, target, tm=512):
    T, D = y.shape

    def body(y_ref, t_ref, dy_ref, loss_ref):
        @pl.when(pl.program_id(0) == 0)
        def _():
            loss_ref[...] = jnp.zeros_like(loss_ref)

        err = y_ref[...] - t_ref[...]
        dy_ref[...] = err * (1.0 / D)
        loss_ref[...] += jnp.sum(err * err) * (0.5 / D)

    tile = pl.BlockSpec((tm, D), lambda i: (i, 0))
    return pl.pallas_call(
        body, name="loss_head", grid=(T // tm,),
        in_specs=[tile, tile],
        out_specs=[tile, pl.BlockSpec((8, 128), lambda i: (0, 0))],
        out_shape=[jax.ShapeDtypeStruct((T, D), F32), jax.ShapeDtypeStruct((8, 128), F32)],
        compiler_params=_cparams(("arbitrary",)),
    )(y, target)


def cast_layer(ld, ws, ps):
    def body(ld_ref, *refs):
        n = len(refs) // 2
        for src, dst in zip(refs[:n], refs[n:]):
            dst[...] = src[...].astype(BF16)

    def in_spec(w):
        nd = w.ndim
        return pl.BlockSpec((None,) + w.shape[1:], lambda i, ld, nd=nd: (ld[0],) + (0,) * (nd - 1))

    def out_spec(w, p):
        shard = w.shape[1:]
        block = (None,) + shard[:p - 1] + (None,) + shard[p - 1:]
        return pl.BlockSpec(block, lambda i, ld, p=p, nd=len(block): (0,) * p + (ld[1],) + (0,) * (nd - p - 1))

    def out_shape(w, p):
        shard = w.shape[1:]
        return jax.ShapeDtypeStruct((1,) + shard[:p - 1] + (N_DEV,) + shard[p - 1:], BF16)

    grid_spec = pltpu.PrefetchScalarGridSpec(
        num_scalar_prefetch=1, grid=(1,),
        in_specs=[in_spec(w) for w in ws], out_specs=[out_spec(w, p) for w, p in zip(ws, ps)])
    return pl.pallas_call(
        body, name="cast_layer", grid_spec=grid_spec,
        out_shape=[out_shape(w, p) for w, p in zip(ws, ps)],
        compiler_params=_cparams(("arbitrary",)),
    )(ld, *ws)


def _silu(x):
    return x * _sigmoid(x)


def ada_fwd(c_all, w_ada, b_cols):
    L, D, n = w_ada.shape

    def body(c_ref, w_ref, b_ref, o_ref):
        c_act = _silu(c_ref[...]).astype(BF16)
        o_ref[...] = _dot(c_act, w_ref[...].astype(BF16)) + b_ref[...]

    return pl.pallas_call(
        body, name="ada_fwd", grid=(L,),
        in_specs=[pl.BlockSpec((N_DEV, D), lambda l: (0, 0)), pl.BlockSpec((None, D, n), lambda l: (l, 0, 0)),
                  pl.BlockSpec((None, 1, n), lambda l: (l, 0, 0))],
        out_specs=pl.BlockSpec((None, N_DEV, n), lambda l: (l, 0, 0)),
        out_shape=jax.ShapeDtypeStruct((L, N_DEV, n), F32),
        compiler_params=_cparams(("arbitrary",)),
    )(c_all, w_ada, b_cols)


def _adamw(w, g, m, v):
    m = ADAM_B1 * m + (1.0 - ADAM_B1) * g
    v = ADAM_B2 * v + (1.0 - ADAM_B2) * (g * g)
    m_hat = m / (1.0 - ADAM_B1 ** ADAM_STEP)
    v_hat = v / (1.0 - ADAM_B2 ** ADAM_STEP)
    delta = -ADAM_LR * (m_hat / (jnp.sqrt(v_hat) + ADAM_EPS) + ADAM_WD * w)
    return delta, m, v


def ada_update(c_all, dada_cols, w, m, v, rb=256):
    L, D, n = w.shape

    def body(c_ref, d_ref, w_ref, m_ref, v_ref, g_ref, dl_ref, nm_ref, nv_ref):
        c_act = _silu(c_ref[...]).astype(BF16)
        g = _dot_tn(c_act, d_ref[...].astype(BF16))
        g_ref[...] = g
        dl_ref[...], nm_ref[...], nv_ref[...] = _adamw(w_ref[...], g, m_ref[...], v_ref[...])

    blk = pl.BlockSpec((None, rb, n), lambda l, i: (l, i, 0))
    out = jax.ShapeDtypeStruct((L, D, n), F32)
    return pl.pallas_call(
        body, name="ada_update", grid=(L, D // rb),
        in_specs=[pl.BlockSpec((N_DEV, rb), lambda l, i: (0, i)),
                  pl.BlockSpec((None, N_DEV, n), lambda l, i: (l, 0, 0)), blk, blk, blk],
        out_specs=[blk, blk, blk, blk], out_shape=[out, out, out, out],
        compiler_params=_cparams(("arbitrary", "arbitrary")),
    )(c_all, dada_cols, w, m, v)


SUM_UPDATE_RECV_BYTES = 12 * 1024 * 1024


def sum_update(dev, first, recvs, owns, w, m, v, prev=None, after=None):
    n_slots, R, C = w.shape
    S = len(recvs)
    assert len(owns) == S and first + S <= n_slots
    rb_max = SUM_UPDATE_RECV_BYTES // (S * N_DEV * C * 2)
    rb = max(r for r in range(8, R + 1, 8) if R % r == 0 and (r <= rb_max or r == 8))
    last = R // rb - 1
    n_prev = 0 if prev is None else 4
    extra = list(prev or ()) + ([] if after is None else [after])

    def body(dev_ref, *refs):
        r_refs, o_refs = refs[:S], refs[S:2 * S]
        w_ref, m_ref, v_ref = refs[2 * S:2 * S + 3]
        g_ref, dl_ref, nm_ref, nv_ref = refs[2 * S + 3 + len(extra):]
        me = dev_ref[0]
        for s in range(S):
            @pl.when(pl.program_id(0) == s)
            def _(s=s):
                g = jnp.zeros((rb, C), F32)
                for d in range(N_DEV):
                    g += jnp.where(me == d, o_refs[s][...], r_refs[s][d]).astype(F32)
                g_ref[...] = g
                dl_ref[...], nm_ref[...], nv_ref[...] = _adamw(w_ref[...], g, m_ref[...], v_ref[...])

    def row(sl, i, s):
        return jnp.where(sl == s, i, jnp.where(sl < s, 0, last))

    def rspec(s):
        return pl.BlockSpec((N_DEV, rb, C), lambda sl, i, dev: (0, row(sl, i, s), 0))

    def ospec(s):
        return pl.BlockSpec((None, rb, C), lambda sl, i, dev: (dev[0], row(sl, i, s), 0))

    blk = pl.BlockSpec((None, rb, C), lambda sl, i, dev: (first + sl, i, 0))
    out = jax.ShapeDtypeStruct((n_slots, R, C), F32)
    grid_spec = pltpu.PrefetchScalarGridSpec(
        num_scalar_prefetch=1, grid=(S, R // rb),
        in_specs=[rspec(s) for s in range(S)] + [ospec(s) for s in range(S)] + [blk, blk, blk] + [ANY] * len(extra),
        out_specs=[blk, blk, blk, blk],
    )
    n_in = 1 + 2 * S + 3
    return pl.pallas_call(
        body, name="sum_update", grid_spec=grid_spec, out_shape=[out, out, out, out],
        input_output_aliases={n_in + i: i for i in range(n_prev)},
        compiler_params=_cparams(("arbitrary", "arbitrary")),
    )(dev, *recvs, *owns, w, m, v, *extra)


def small_sum(gathered):
    _, R, C = gathered.shape

    def body(g_ref, o_ref):
        acc = g_ref[0]
        for d in range(1, N_DEV):
            acc += g_ref[d]
        o_ref[...] = acc

    return pl.pallas_call(
        body, name="small_sum", grid=(1,),
        in_specs=[pl.BlockSpec((N_DEV, R, C), lambda i: (0, 0, 0))],
        out_specs=pl.BlockSpec((R, C), lambda i: (0, 0)),
        out_shape=jax.ShapeDtypeStruct((R, C), F32),
        compiler_params=_cparams(("arbitrary",)),
    )(gathered)


def small_update(w, g, m, v):
    def body(w_ref, g_ref, m_ref, v_ref, dl_ref, nm_ref, nv_ref):
        dl_ref[...], nm_ref[...], nv_ref[...] = _adamw(w_ref[...], g_ref[...], m_ref[...], v_ref[...])

    blk = pl.BlockSpec(w.shape, lambda i: (0, 0))
    out = jax.ShapeDtypeStruct(w.shape, F32)
    return pl.pallas_call(
        body, name="small_update", grid=(1,),
        in_specs=[blk] * 4, out_specs=[blk] * 3, out_shape=[out] * 3,
        compiler_params=_cparams(("arbitrary",)),
    )(w, g, m, v)


MESH = pl.DeviceIdType.MESH
ANY = pl.BlockSpec(memory_space=pl.ANY)


def _coords():
    return lax.axis_index("x"), lax.axis_index("y"), lax.axis_index("c")


def _dev_index(x, y, c):
    return 4 * x + 2 * y + c


def _at_dev(ref, p, dev):
    return ref.at[(slice(None),) * p + (dev,)]


def all_gather(arrays, ps):
    n = len(arrays)

    def body(*refs):
        ins, outs = refs[:n], refs[n:2 * n]
        send_sems, recv_sems, local_sems = refs[2 * n:]
        x, y, c = _coords()
        me, sibling = (x, y, c), (x, y, 1 - c)
        chips = [(1 - x, y), (x, 1 - y), (1 - x, 1 - y)]

        def copy(a, k, block, to, src=None):
            dst = _at_dev(outs[a], ps[a], _dev_index(*block))
            return pltpu.make_async_remote_copy(
                src_ref=dst if src is None else src, dst_ref=dst,
                send_sem=send_sems.at[a, k], recv_sem=recv_sems.at[a, k], device_id=to, device_id_type=MESH)

        mine = [pltpu.make_async_copy(ins[a], _at_dev(outs[a], ps[a], _dev_index(*me)), local_sems.at[a])
                for a in range(n)]
        for cp in mine:
            cp.start()
        first = []
        for a in range(n):
            first.append(copy(a, 0, me, sibling, src=ins[a]))
            first += [copy(a, 1 + j, me, (*chip, c), src=ins[a]) for j, chip in enumerate(chips)]
        for cp in first:
            cp.start()
        passed = []
        for j, chip in enumerate(chips):
            for a in range(n):
                copy(a, 1 + j, (*chip, c), me).wait_recv()
                fwd = copy(a, 4 + j, (*chip, c), sibling)
                fwd.start()
                passed.append(fwd)
        for a in range(n):
            copy(a, 0, sibling, me).wait_recv()
            for j, chip in enumerate(chips):
                copy(a, 4 + j, (*chip, 1 - c), me).wait_recv()
        for cp in first + passed:
            cp.wait_send()
        for cp in mine:
            cp.wait()

    out_shape = [jax.ShapeDtypeStruct(a.shape[:p] + (N_DEV,) + a.shape[p:], a.dtype) for a, p in zip(arrays, ps)]
    return pl.pallas_call(
        body, name="all_gather", in_specs=[ANY] * n, out_specs=[ANY] * n, out_shape=out_shape,
        scratch_shapes=[pltpu.SemaphoreType.DMA((n, 7)), pltpu.SemaphoreType.DMA((n, 7)),
                        pltpu.SemaphoreType.DMA((n,))],
        compiler_params=pltpu.CompilerParams(has_side_effects=True),
    )(*arrays)


HBM = pl.BlockSpec(memory_space=pltpu.HBM)
SEM = pl.BlockSpec(memory_space=pltpu.SEMAPHORE)
EFFECT = pltpu.SideEffectType.DATAFLOW_SIDE_EFFECTING


def _peers(x, y, c):
    out = []
    for k in range(1, N_DEV):
        out.append((1 - x if k & 4 else x, 1 - y if k & 2 else y, 1 - c if k & 1 else c))
    return out


def _exchange_plan(n):
    def plan(refs, x, y, c):
        blocks, lands = refs[:n], refs[n:2 * n]
        me = _dev_index(x, y, c)
        moves = []
        for peer in _peers(x, y, c):
            q = _dev_index(*peer)
            moves += [(blocks[a].at[q], lands[a].at[me], peer, lands[a].at[q]) for a in range(n)]
        return moves
    return plan


def _gather_plan(ps, second):
    def plan(refs, x, y, c):
        me, sibling = (x, y, c), (x, y, 1 - c)
        chips = [(1 - x, y), (x, 1 - y), (1 - x, 1 - y)]
        if second:
            trips = [((*ch, c), sibling, (*ch, 1 - c)) for ch in chips]
        else:
            trips = [(me, sibling, sibling)] + [(me, (*ch, c), (*ch, c)) for ch in chips]
        moves = []
        for sent, to, arriving in trips:
            for ref, p in zip(refs, ps):
                blk = _at_dev(ref, p, _dev_index(*sent))
                moves.append((blk, blk, to, _at_dev(ref, p, _dev_index(*arriving))))
        return moves
    return plan


def copies_start(name, plan, n_moves, arrays, carry):
    n = len(arrays)

    def body(*refs):
        sems = refs[n + 1:n + 1 + 2 * n_moves]
        moves = plan(refs[:n], *_coords())
        assert len(moves) == n_moves
        for i, (src, dst, to, _) in enumerate(moves):
            pltpu.make_async_remote_copy(src_ref=src, dst_ref=dst, send_sem=sems[i], recv_sem=sems[n_moves + i],
                                         device_id=to, device_id_type=MESH).start()

    operands = [pltpu.with_memory_space_constraint(a, pltpu.HBM) for a in list(arrays) + [carry]]
    outs = pl.pallas_call(
        body, name=name,
        out_shape=[pltpu.SemaphoreType.DMA(())] * (2 * n_moves) + [pltpu.HBM(a.shape, a.dtype) for a in operands],
        in_specs=[HBM] * (n + 1), out_specs=[SEM] * (2 * n_moves) + [HBM] * (n + 1),
        input_output_aliases={i: 2 * n_moves + i for i in range(n + 1)},
        compiler_params=pltpu.CompilerParams(has_side_effects=EFFECT),
    )(*operands)
    return outs[:n_moves], outs[n_moves:2 * n_moves], outs[2 * n_moves:-1], outs[-1]


def copies_wait(name, plan, send_sems, recv_sems, arrays, after):
    n, n_moves = len(arrays), len(send_sems)

    def body(*refs):
        sems = refs[n:n + 2 * n_moves]
        for i, (src, _, to, arriving) in enumerate(plan(refs[:n], *_coords())):
            cp = pltpu.make_async_remote_copy(src_ref=src, dst_ref=arriving, send_sem=sems[i],
                                              recv_sem=sems[n_moves + i], device_id=to, device_id_type=MESH)
            cp.wait_send()
            cp.wait_recv()

    return pl.pallas_call(
        body, name=name,
        out_shape=[pltpu.HBM(a.shape, a.dtype) for a in arrays],
        in_specs=[HBM] * n + [SEM] * (2 * n_moves) + [ANY], out_specs=[HBM] * n,
        input_output_aliases={i: i for i in range(n)},
        compiler_params=pltpu.CompilerParams(has_side_effects=EFFECT),
    )(*arrays, *send_sems, *recv_sems, after)


WEIGHT_NAMES = ("w_ada", "b_ada", "g_pre", "g_post", "w_ff_in", "w_ff_out", "w_in", "conv_w", "w_conv_out",
                "lam_re", "lam_im", "log_dt", "ssm_b_re", "ssm_b_im", "ssm_c_re", "ssm_c_im", "ssm_d", "w_glu",
                "w_pool", "pool_scale", "w_pool_out", "w_sb_out", "w_out")
BIG_NAMES = ("w_ff_in", "w_ff_out", "w_in", "w_conv_out", "w_glu", "w_pool_out", "w_sb_out", "w_out")
GATHER_PS = (2, 2, 1, 1, 1, 1, 1, 1)
SMALL_NAMES = ("b_ada", "g_pre", "g_post", "conv_w", "lam_re", "lam_im", "log_dt", "ssm_b_re", "ssm_b_im",
               "ssm_c_re", "ssm_c_im", "ssm_d", "w_pool", "pool_scale")
PACK_LANES = 128
PACK_ROWS = 8


def _pack(arrays):
    flat = jnp.concatenate([a.reshape(-1) for a in arrays])
    unit = PACK_LANES * PACK_ROWS
    flat = jnp.pad(flat, (0, -flat.shape[0] % unit))
    return flat.reshape(-1, PACK_LANES)


def _unpack(pack, shapes):
    flat = pack.reshape(-1)
    out, off = [], 0
    for s in shapes:
        n = 1
        for d in s:
            n *= d
        out.append(flat[off:off + n].reshape(s))
        off += n
    return out


def _pad_rows(a, rows=8):
    return jnp.pad(a, ((0, 0), (0, rows - a.shape[1]), (0, 0)))


def _tile_b(b):
    L = b.shape[0]
    return jnp.tile(b.transpose(0, 3, 1, 2).reshape(L, SSM_GROUP, SSM_W), (1, SSM_GROUPS, 1))


def _tile_c(c):
    L = c.shape[0]
    return jnp.tile(c.transpose(0, 3, 1, 2).reshape(L, SSM_STATE, MIX_W), (1, SSM_GROUPS, 1))


def _step(x, c, target, W, M, V):
    T, D = x.shape[1], x.shape[2]
    L = W["w_ada"].shape[0]
    x = x[0]
    target = target[0]
    ax, ay, ac = _coords()
    dev = _dev_index(ax, ay, ac)
    n_ada = W["w_ada"].shape[2]

    dev_s = jnp.reshape(dev, (1,)).astype(jnp.int32)
    big_ws = [W[k] for k in BIG_NAMES]
    bufs = [cast_layer(jnp.concatenate([jnp.array([l], jnp.int32), dev_s]), big_ws, GATHER_PS) for l in range(L)]
    plan_a, plan_b = _gather_plan(GATHER_PS, False), _gather_plan(GATHER_PS, True)
    n_a, n_b = 4 * len(BIG_NAMES), 3 * len(BIG_NAMES)
    sem_s, sem_r, bufs[0], x = copies_start("gather_a_start_0", plan_a, n_a, bufs[0], x)

    gathered = all_gather([W["g_pre"], W["g_post"], W["conv_w"], c], [0, 0, 0, 0])
    g_pre = gathered[0].transpose(1, 2, 0, 3).reshape(L, N_SUB, D)
    g_post = gathered[1].transpose(1, 2, 0, 3).reshape(L, N_SUB, D)
    conv_w = _pad_rows(gathered[2].transpose(1, 2, 0, 3).reshape(L, 3, MIX_W))
    c_all = gathered[3].reshape(N_DEV, D)

    b_cols = lax.dynamic_slice_in_dim(W["b_ada"], dev * n_ada, n_ada, axis=1)[:, None, :]
    ada_cols = ada_fwd(c_all, W["w_ada"], b_cols)
    ada_all = all_gather([ada_cols], [0])[0]
    ada = lax.dynamic_index_in_dim(ada_all, dev, axis=2, keepdims=False)
    ada = ada.transpose(1, 0, 2).reshape(L, N_SUB, 3, D)
    zeros = jnp.zeros((L, N_SUB, D), F32)
    pv_all = jnp.stack([g_pre, ada[:, :, 0], ada[:, :, 1], g_post, ada[:, :, 2], zeros, zeros, zeros], axis=2)

    lam = jnp.stack([W["lam_re"].reshape(L, SSM_W), W["lam_im"].reshape(L, SSM_W),
                     jnp.repeat(W["log_dt"], SSM_STATE, axis=1)], axis=1)
    lam = _pad_rows(lam)
    b_t = jnp.stack([_tile_b(W["ssm_b_re"]), _tile_b(W["ssm_b_im"])], axis=1)
    c_t = jnp.stack([_tile_c(W["ssm_c_re"]), _tile_c(W["ssm_c_im"])], axis=1)
    avec, b_bd, c_bd = s5_params(lam, b_t, c_t)
    ssm_d = _pad_rows(W["ssm_d"][:, None, :])
    pool_scale = _pad_rows(W["pool_scale"][:, None, :])
    eye4 = jnp.eye(len(POOL_WINDOWS), dtype=F32)
    w_bd = jnp.einsum("lgcd,gh->lgchd", W["w_pool"], eye4).reshape(L, MIX_W, MIX_W).astype(BF16)

    bufs[0] = copies_wait("gather_a_wait_0", plan_a, sem_s, sem_r, bufs[0], pv_all)
    sem_s, sem_r, bufs[0], x = copies_start("gather_b_start_0", plan_b, n_b, bufs[0], x)
    bufs[0] = copies_wait("gather_b_wait_0", plan_b, sem_s, sem_r, bufs[0], x)

    def layer_weights(l):
        b = bufs[l]
        return (b[0].reshape(1, 2, 2, 4, D, FF_BLK), b[1].reshape(1, 2, 4, FF_BLK, D), b[2], b[3], b[4], b[5], b[6],
                b[7].reshape(1, D, D))

    l0 = jnp.array([0], jnp.int32)
    k0, k1 = jnp.array([0, 0], jnp.int32), jnp.array([0, 1], jnp.int32)
    saved = []
    for l in range(L):
        li = jnp.array([l], jnp.int32)
        wg_ff_in, wg_ff_out, wg_in, wg_conv, wg_glu, wg_pool, wg_sb, wg_out = layer_weights(l)
        nxt = l + 1 < L
        if nxt:
            sem_s, sem_r, bufs[l + 1], x = copies_start(f"gather_a_start_{l + 1}", plan_a, n_a, bufs[l + 1], x)
        x0 = x
        ab0, f0, x1 = ffn_fwd(k0, x0, pv_all[l, 0], wg_ff_in, wg_ff_out)
        p = mix_in_fwd(l0, x1, pv_all[l, 1], wg_in)
        za = conv_fwd(li, p, conv_w)
        z = pool_fwd(li, p, w_bd, pool_scale)
        s = s5_scan(li, avec, s5_bu(li, p, b_bd), False)
        yg = s5_out(li, p, s, c_bd, ssm_d)
        o, sb_tot = sb_fwd(p)
        x2, m = merge_fwd(l0, p, za, yg, z, o, x1, pv_all[l, 1], wg_conv, wg_glu, wg_pool, wg_sb, wg_out)
        if nxt:
            bufs[l + 1] = copies_wait(f"gather_a_wait_{l + 1}", plan_a, sem_s, sem_r, bufs[l + 1], x2)
            sem_s, sem_r, bufs[l + 1], x2 = copies_start(f"gather_b_start_{l + 1}", plan_b, n_b, bufs[l + 1], x2)
        ab1, f1, x = ffn_fwd(k1, x2, pv_all[l, 2], wg_ff_in, wg_ff_out)
        if nxt:
            bufs[l + 1] = copies_wait(f"gather_b_wait_{l + 1}", plan_b, sem_s, sem_r, bufs[l + 1], x)
        saved.append((x0, ab0, f0, x1, p, za, z, s, yg, o, sb_tot, m, x2, ab1, f1))

    dx, loss_blk = loss_head(x, target)
    loss = lax.psum(loss_blk[0, 0], ("x", "y", "c"))

    recvs, owns, in_flight = [None] * L, [None] * L, None
    n_blocks = 10
    plan_x, n_x = _exchange_plan(n_blocks), (N_DEV - 1) * n_blocks

    def settle(flight, after):
        layer, s_sem, r_sem, arrays = flight
        arrays = copies_wait(f"exchange_wait_{layer}", plan_x, s_sem, r_sem, arrays, after)
        owns[layer], recvs[layer] = arrays[:n_blocks], arrays[n_blocks:]

    pgs = [None] * L
    small = {k: [None] * L for k in ("conv_w", "w_bd", "pool_scale", "ssm_d", "gb", "gc", "da")}
    for l in reversed(range(L)):
        li = jnp.array([l], jnp.int32)
        x0, ab0, f0, x1, p, za, z, s, yg, o, sb_tot, m, x2, ab1, f1 = saved[l]
        wg_ff_in, wg_ff_out, wg_in, wg_conv, wg_glu, wg_pool, wg_sb, wg_out = layer_weights(l)
        dab, h, df, dx, pg2 = ffn_bwd_act(k1, dx, x2, f1, pv_all[l, 2], ab1, wg_ff_in, wg_ff_out)
        g_in1, g_out1 = ffn_bwd_w(h, df, ab1, dab)
        (dza, dyg, dz, do, dgates, pg1m, g_conv, g_glu, g_pool, g_sb, g_wo) = merge_bwd(
            l0, p, za, yg, z, o, m, dx, pv_all[l, 1], wg_conv, wg_glu, wg_pool, wg_sb, wg_out)
        d_conv, small["conv_w"][l] = conv_bwd(li, p, dza, conv_w)
        d_pool, small["w_bd"][l], small["pool_scale"][l] = pool_bwd(li, p, dz, w_bd, pool_scale)
        ds, du_skip, small["gc"][l], small["ssm_d"][l] = s5_bwd_y(li, p, s, dyg, c_bd, ssm_d)
        lam_s = s5_scan(li, avec, ds, True)
        d_ssm, small["gb"][l] = s5_bwd_u(li, p, lam_s, du_skip, b_bd)
        small["da"][l] = s5_bwd_a(s, lam_s)
        d_qkv = sb_bwd(p, do, sb_tot)
        dp = dp_assemble(d_conv, d_ssm, d_pool, d_qkv, dgates)
        dx, h, pg1i = mix_in_bwd_act(l0, dp, dx, x1, pv_all[l, 1], wg_in)
        g_win = matmul_tn(h, dp, IN_BLK)
        dab, h, df, dx, pg0 = ffn_bwd_act(k0, dx, x0, f0, pv_all[l, 0], ab0, wg_ff_in, wg_ff_out)
        g_in0, g_out0 = ffn_bwd_w(h, df, ab0, dab)
        pgs[l] = jnp.stack([pg0, pg1m + pg1i, pg2])
        blocks = [g_in0.reshape(N_DEV, D, FF_BLK), g_in1.reshape(N_DEV, D, FF_BLK),
                  g_out0.reshape(N_DEV, D_FF // N_DEV, D), g_out1.reshape(N_DEV, D_FF // N_DEV, D),
                  g_win, g_conv, g_glu, g_pool, g_sb, g_wo.reshape(N_DEV, D // N_DEV, D)]
        if in_flight is not None:
            settle(in_flight, dx)
        arrays = blocks + [lax.empty(a.shape, a.dtype) for a in blocks]
        s_sem, r_sem, arrays, dx = copies_start(f"exchange_start_{l}", plan_x, n_x, arrays, dx)
        in_flight = (l, s_sem, r_sem, arrays)

    dlam, db_t, dc_t, dldt = s5_params_bwd(lam, b_t, jnp.stack(small["gb"]), jnp.stack(small["gc"]),
                                           jnp.stack(small["da"]))
    pg = jnp.stack(pgs)
    d_ada = jnp.stack([pg[:, :, PV_SHIFT], pg[:, :, PV_SCALE], pg[:, :, PV_GATE]], axis=2).reshape(L, N_SUB * 3 * D)
    db = db_t.reshape(L, 2, SSM_GROUPS, SSM_GROUP, SSM_GROUPS, SSM_STATE)
    db = jnp.einsum("lrghgp->lrgph", db)
    dc = dc_t.reshape(L, 2, SSM_GROUPS, SSM_STATE, SSM_GROUPS, SSM_GROUP)
    dc = jnp.einsum("lrgpgh->lrghp", dc)
    d_wpool = jnp.einsum("lgcgd->lgcd", jnp.stack(small["w_bd"]).reshape(L, 4, 64, 4, 64))
    contrib = {
        "b_ada": d_ada, "g_pre": pg[:, :, PV_GPRE], "g_post": pg[:, :, PV_GPOST],
        "conv_w": jnp.stack(small["conv_w"])[:, :3], "lam_re": dlam[:, 0].reshape(L, SSM_GROUPS, SSM_STATE),
        "lam_im": dlam[:, 1].reshape(L, SSM_GROUPS, SSM_STATE), "log_dt": dldt[:, 2, :SSM_GROUPS],
        "ssm_b_re": db[:, 0], "ssm_b_im": db[:, 1], "ssm_c_re": dc[:, 0], "ssm_c_im": dc[:, 1],
        "ssm_d": jnp.stack(small["ssm_d"])[:, 0], "w_pool": d_wpool,
        "pool_scale": jnp.stack(small["pool_scale"])[:, 0],
    }
    contrib_shapes = [contrib[k].shape for k in SMALL_NAMES]
    pack_all = all_gather([_pack([contrib[k] for k in SMALL_NAMES])], [0])[0]
    total = dict(zip(SMALL_NAMES, _unpack(small_sum(pack_all), contrib_shapes)))
    d_ada_all = pack_all.reshape(N_DEV, -1)[:, :L * N_SUB * 3 * D].reshape(N_DEV, L, N_SUB * 3 * D)
    dada_cols = lax.dynamic_slice_in_dim(d_ada_all, dev * n_ada, n_ada, axis=2).transpose(1, 0, 2)
    n_g = D // N_DEV
    grads = {}
    for k in SMALL_NAMES:
        g = total[k]
        if k in ("g_pre", "g_post"):
            g = lax.dynamic_slice_in_dim(g, dev * n_g, n_g, axis=2)
        elif k == "conv_w":
            g = lax.dynamic_slice_in_dim(g, dev * (MIX_W // N_DEV), MIX_W // N_DEV, axis=2)
        grads[k] = g

    delta, new_m, new_v = {}, {}, {}
    shapes = [W[k].shape for k in SMALL_NAMES]
    dl, nm, nv = small_update(_pack([W[k] for k in SMALL_NAMES]), _pack([grads[k] for k in SMALL_NAMES]),
                              _pack([M[k] for k in SMALL_NAMES]), _pack([V[k] for k in SMALL_NAMES]))
    for k, a, b, cc in zip(SMALL_NAMES, _unpack(dl, shapes), _unpack(nm, shapes), _unpack(nv, shapes)):
        delta[k], new_m[k], new_v[k] = a, b, cc
    grads["w_ada"], delta["w_ada"], new_m["w_ada"], new_v["w_ada"] = ada_update(
        c_all, dada_cols, W["w_ada"], M["w_ada"], V["w_ada"])

    big_idx = {"w_ff_in": (0, 1), "w_ff_out": (2, 3), "w_in": (4,), "w_conv_out": (5,), "w_glu": (6,),
               "w_pool_out": (7,), "w_sb_out": (8,), "w_out": (9,)}

    def big(name, layers, prev, after=None):
        idx = big_idx[name]
        flat = (-1,) + W[name].shape[-2:]
        return sum_update(dev_s, layers[0] * len(idx), [recvs[l][i] for l in layers for i in idx],
                          [owns[l][i] for l in layers for i in idx],
                          W[name].reshape(flat), M[name].reshape(flat), V[name].reshape(flat), prev, after)

    partial, after = {}, dx
    if L > 1:
        for name in BIG_NAMES:
            partial[name] = big(name, list(range(1, L)), None, after)
            after = partial[name][0]
    settle(in_flight, after)
    for name in BIG_NAMES:
        outs = big(name, [0], partial.get(name))
        grads[name], delta[name], new_m[name], new_v[name] = [o.reshape(W[name].shape) for o in outs]

    return (loss, dx[None], *[grads[k] for k in WEIGHT_NAMES], *[delta[k] for k in WEIGHT_NAMES],
            *[new_m[k] for k in WEIGHT_NAMES], *[new_v[k] for k in WEIGHT_NAMES])


def kernel(x, c, w_ada, b_ada, g_pre, g_post, w_ff_in, w_ff_out, w_in, conv_w, w_conv_out, lam_re, lam_im, log_dt, ssm_b_re, ssm_b_im, ssm_c_re, ssm_c_im, ssm_d, w_glu, w_pool, pool_scale, w_pool_out, w_sb_out, w_out, loss_target, m_w_ada, m_b_ada, m_g_pre, m_g_post, m_w_ff_in, m_w_ff_out, m_w_in, m_conv_w, m_w_conv_out, m_lam_re, m_lam_im, m_log_dt, m_ssm_b_re, m_ssm_b_im, m_ssm_c_re, m_ssm_c_im, m_ssm_d, m_w_glu, m_w_pool, m_pool_scale, m_w_pool_out, m_w_sb_out, m_w_out, v_w_ada, v_b_ada, v_g_pre, v_g_post, v_w_ff_in, v_w_ff_out, v_w_in, v_conv_w, v_w_conv_out, v_lam_re, v_lam_im, v_log_dt, v_ssm_b_re, v_ssm_b_im, v_ssm_c_re, v_ssm_c_im, v_ssm_d, v_w_glu, v_w_pool, v_pool_scale, v_w_pool_out, v_w_sb_out, v_w_out):
    w = (w_ada, b_ada, g_pre, g_post, w_ff_in, w_ff_out, w_in, conv_w, w_conv_out, lam_re, lam_im, log_dt, ssm_b_re, ssm_b_im, ssm_c_re, ssm_c_im, ssm_d, w_glu, w_pool, pool_scale, w_pool_out, w_sb_out, w_out)
    m = (m_w_ada, m_b_ada, m_g_pre, m_g_post, m_w_ff_in, m_w_ff_out, m_w_in, m_conv_w, m_w_conv_out, m_lam_re, m_lam_im, m_log_dt, m_ssm_b_re, m_ssm_b_im, m_ssm_c_re, m_ssm_c_im, m_ssm_d, m_w_glu, m_w_pool, m_pool_scale, m_w_pool_out, m_w_sb_out, m_w_out)
    v = (v_w_ada, v_b_ada, v_g_pre, v_g_post, v_w_ff_in, v_w_ff_out, v_w_in, v_conv_w, v_w_conv_out, v_lam_re, v_lam_im, v_log_dt, v_ssm_b_re, v_ssm_b_im, v_ssm_c_re, v_ssm_c_im, v_ssm_d, v_w_glu, v_w_pool, v_pool_scale, v_w_pool_out, v_w_sb_out, v_w_out)
    return _step(x, c, loss_target, dict(zip(WEIGHT_NAMES, w)), dict(zip(WEIGHT_NAMES, m)), dict(zip(WEIGHT_NAMES, v)))
```

```python
import functools

import jax
import jax.numpy as jnp
from jax import lax
from jax.experimental import pallas as pl
from jax.experimental.pallas import tpu as pltpu

F32 = jnp.float32
BF16 = jnp.bfloat16

N_DEV = 8
D_MODEL = 1024
D_FF = 2816
FF_BLK = D_FF // 4
N_SUB = 3
MIX_W = 256
IN_COLS = 6144
IN_BLK = IN_COLS // N_DEV
GATE_OFF = 2048
SSM_GROUPS, SSM_GROUP, SSM_STATE = 16, 16, 64
SSM_W = SSM_GROUPS * SSM_STATE
POOL_WINDOWS = (2, 4, 8, 16)
SB_HEAD = 64
EPS = 1e-6
DT_LAMBDA_RE_MAX = -1e-4
ADAM_LR, ADAM_B1, ADAM_B2, ADAM_EPS, ADAM_WD, ADAM_STEP = 0.001, 0.9, 0.999, 1e-08, 0.01, 10

VMEM_LIMIT = 56 * 1024 * 1024

PV_GPRE, PV_SHIFT, PV_SCALE, PV_GPOST, PV_GATE = 0, 1, 2, 3, 4


def _cparams(sem):
    return pltpu.CompilerParams(dimension_semantics=sem, vmem_limit_bytes=VMEM_LIMIT)


def _dot(a, b):
    return jnp.dot(a, b, preferred_element_type=F32)


def _dot_nt(a, b):
    return lax.dot_general(a, b, (((1,), (1,)), ((), ())), preferred_element_type=F32)


def _dot_tn(a, b):
    return lax.dot_general(a, b, (((0,), (0,)), ((), ())), preferred_element_type=F32)


def _rms(x):
    r = lax.rsqrt(jnp.mean(x * x, axis=-1, keepdims=True) + EPS)
    return x * r, r


def _rms_bwd(dn, n, r):
    return r * (dn - n * jnp.mean(dn * n, axis=-1, keepdims=True))


def _sigmoid(x):
    return 1.0 / (1.0 + jnp.exp(-x))


def _colsum(x):
    return jnp.sum(x, axis=0, keepdims=True)


def _prenorm(x, pv_ref):
    n, r = _rms(x)
    hn = n * pv_ref[PV_GPRE:PV_GPRE + 1, :]
    h = hn * (1.0 + pv_ref[PV_SCALE:PV_SCALE + 1, :]) + pv_ref[PV_SHIFT:PV_SHIFT + 1, :]
    return h, n, r, hn


def _prenorm_bwd(dh, dxn, x, pv_ref, pg_ref):
    _, n, r, hn = _prenorm(x, pv_ref)
    pg_ref[PV_SHIFT:PV_SHIFT + 1, :] += _colsum(dh)
    pg_ref[PV_SCALE:PV_SCALE + 1, :] += _colsum(dh * hn)
    dhn = dh * (1.0 + pv_ref[PV_SCALE:PV_SCALE + 1, :])
    pg_ref[PV_GPRE:PV_GPRE + 1, :] += _colsum(dhn * n)
    dn = dhn * pv_ref[PV_GPRE:PV_GPRE + 1, :]
    return dxn + _rms_bwd(dn, n, r)


def _postnorm_res(x, f, pv_ref, coef):
    nf, _ = _rms(f)
    return x + (coef * (1.0 + pv_ref[PV_GATE:PV_GATE + 1, :])) * (nf * pv_ref[PV_GPOST:PV_GPOST + 1, :])


def _postnorm_bwd(dxn, f, pv_ref, pg_ref, coef):
    nf, rf = _rms(f)
    g_post = pv_ref[PV_GPOST:PV_GPOST + 1, :]
    pg_ref[PV_GATE:PV_GATE + 1, :] += _colsum(dxn * (nf * g_post)) * coef
    dnfg = dxn * (coef * (1.0 + pv_ref[PV_GATE:PV_GATE + 1, :]))
    pg_ref[PV_GPOST:PV_GPOST + 1, :] += _colsum(dnfg * nf)
    return _rms_bwd(dnfg * g_post, nf, rf)


def ffn_fwd(lk, x, pv, wg_in, wg_out, tm=512):
    T, D = x.shape
    nj = 4

    def body(lk_ref, x_ref, pv_ref, win_ref, wout_ref, ab_ref, f_ref, xn_ref, h_sc, acc):
        j = pl.program_id(1)

        @pl.when(j == 0)
        def _():
            h, _, _, _ = _prenorm(x_ref[...], pv_ref)
            h_sc[...] = h.astype(BF16)
            acc[...] = jnp.zeros_like(acc)

        h = h_sc[...]
        a = _dot(h, win_ref[0])
        b = _dot(h, win_ref[1])
        ab_ref[0] = a.astype(BF16)
        ab_ref[1] = b.astype(BF16)
        act = (a * _sigmoid(a) * b).astype(BF16)
        acc[...] += _dot(act, wout_ref[...])

        @pl.when(j == nj - 1)
        def _():
            f = acc[...]
            f_ref[...] = f
            xn_ref[...] = _postnorm_res(x_ref[...], f, pv_ref, 0.5)

    grid_spec = pltpu.PrefetchScalarGridSpec(
        num_scalar_prefetch=1, grid=(T // tm, nj),
        in_specs=[
            pl.BlockSpec((tm, D), lambda i, j, lk: (i, 0)),
            pl.BlockSpec((8, D), lambda i, j, lk: (0, 0)),
            pl.BlockSpec((None, None, 2, None, D, FF_BLK), lambda i, j, lk: (lk[0], lk[1], 0, j, 0, 0)),
            pl.BlockSpec((None, None, None, FF_BLK, D), lambda i, j, lk: (lk[0], lk[1], j, 0, 0)),
        ],
        out_specs=[
            pl.BlockSpec((2, None, tm, FF_BLK), lambda i, j, lk: (0, j, i, 0)),
            pl.BlockSpec((tm, D), lambda i, j, lk: (i, 0)),
            pl.BlockSpec((tm, D), lambda i, j, lk: (i, 0)),
        ],
        scratch_shapes=[pltpu.VMEM((tm, D), BF16), pltpu.VMEM((tm, D), F32)],
    )
    return pl.pallas_call(
        body, name="ffn_fwd", grid_spec=grid_spec,
        out_shape=[jax.ShapeDtypeStruct((2, nj, T, FF_BLK), BF16),
                   jax.ShapeDtypeStruct((T, D), F32), jax.ShapeDtypeStruct((T, D), F32)],
        compiler_params=_cparams(("arbitrary", "arbitrary")),
    )(lk, x, pv, wg_in, wg_out)


def ffn_bwd_act(lk, dxn, x, f, pv, ab, wg_in, wg_out, tm=512):
    T, D = x.shape
    nj = 4

    def body(lk_ref, dxn_ref, x_ref, f_ref, pv_ref, ab_ref, win_ref, wout_ref,
             dab_ref, h_ref, df_ref, dx_ref, pg_ref, dacc):
        i, j = pl.program_id(0), pl.program_id(1)

        @pl.when((i == 0) & (j == 0))
        def _():
            pg_ref[...] = jnp.zeros_like(pg_ref)

        @pl.when(j == 0)
        def _():
            df = _postnorm_bwd(dxn_ref[...], f_ref[...], pv_ref, pg_ref, 0.5)
            df_ref[...] = df.astype(BF16)
            h, _, _, _ = _prenorm(x_ref[...], pv_ref)
            h_ref[...] = h.astype(BF16)
            dacc[...] = jnp.zeros_like(dacc)

        dact = _dot_nt(df_ref[...], wout_ref[...])
        a = ab_ref[0].astype(F32)
        b = ab_ref[1].astype(F32)
        sig = _sigmoid(a)
        s = a * sig
        da = (dact * b * (sig * (1.0 + a * (1.0 - sig)))).astype(BF16)
        db = (dact * s).astype(BF16)
        dab_ref[0] = da
        dab_ref[1] = db
        dacc[...] += _dot_nt(da, win_ref[0]) + _dot_nt(db, win_ref[1])

        @pl.when(j == nj - 1)
        def _():
            dx_ref[...] = _prenorm_bwd(dacc[...], dxn_ref[...], x_ref[...], pv_ref, pg_ref)

    tile = pl.BlockSpec((tm, D), lambda i, j, lk: (i, 0))
    grid_spec = pltpu.PrefetchScalarGridSpec(
        num_scalar_prefetch=1, grid=(T // tm, nj),
        in_specs=[
            tile, tile, tile,
            pl.BlockSpec((8, D), lambda i, j, lk: (0, 0)),
            pl.BlockSpec((2, None, tm, FF_BLK), lambda i, j, lk: (0, j, i, 0)),
            pl.BlockSpec((None, None, 2, None, D, FF_BLK), lambda i, j, lk: (lk[0], lk[1], 0, j, 0, 0)),
            pl.BlockSpec((None, None, None, FF_BLK, D), lambda i, j, lk: (lk[0], lk[1], j, 0, 0)),
        ],
        out_specs=[
            pl.BlockSpec((2, None, tm, FF_BLK), lambda i, j, lk: (0, j, i, 0)),
            tile, tile, tile,
            pl.BlockSpec((8, D), lambda i, j, lk: (0, 0)),
        ],
        scratch_shapes=[pltpu.VMEM((tm, D), F32)],
    )
    return pl.pallas_call(
        body, name="ffn_bwd_act", grid_spec=grid_spec,
        out_shape=[jax.ShapeDtypeStruct((2, nj, T, FF_BLK), BF16),
                   jax.ShapeDtypeStruct((T, D), BF16), jax.ShapeDtypeStruct((T, D), BF16),
                   jax.ShapeDtypeStruct((T, D), F32), jax.ShapeDtypeStruct((8, D), F32)],
        compiler_params=_cparams(("arbitrary", "arbitrary")),
    )(lk, dxn, x, f, pv, ab, wg_in, wg_out)


def ffn_bwd_w(h, df, ab, dab, tm=512):
    T, D = h.shape
    nj, ni = 4, T // tm

    def body(h_ref, df_ref, ab_ref, dab_ref, gin_ref, gout_ref, acc_in, acc_out):
        i = pl.program_id(1)

        @pl.when(i == 0)
        def _():
            acc_in[...] = jnp.zeros_like(acc_in)
            acc_out[...] = jnp.zeros_like(acc_out)

        h = h_ref[...]
        acc_in[0] += _dot_tn(h, dab_ref[0])
        acc_in[1] += _dot_tn(h, dab_ref[1])
        a = ab_ref[0].astype(F32)
        b = ab_ref[1].astype(F32)
        act = (a * _sigmoid(a) * b).astype(BF16)
        acc_out[...] += _dot_tn(act, df_ref[...])

        @pl.when(i == ni - 1)
        def _():
            gin_ref[...] = acc_in[...].astype(BF16)
            gout_ref[...] = acc_out[...].astype(BF16)

    tile = pl.BlockSpec((tm, D), lambda j, i: (i, 0))
    blk = pl.BlockSpec((2, None, tm, FF_BLK), lambda j, i: (0, j, i, 0))
    return pl.pallas_call(
        body, name="ffn_bwd_w", grid=(nj, ni),
        in_specs=[tile, tile, blk, blk],
        out_specs=[pl.BlockSpec((2, None, D, FF_BLK), lambda j, i: (0, j, 0, 0)),
                   pl.BlockSpec((None, FF_BLK, D), lambda j, i: (j, 0, 0))],
        out_shape=[jax.ShapeDtypeStruct((2, nj, D, FF_BLK), BF16),
                   jax.ShapeDtypeStruct((nj, FF_BLK, D), BF16)],
        scratch_shapes=[pltpu.VMEM((2, D, FF_BLK), F32), pltpu.VMEM((FF_BLK, D), F32)],
        compiler_params=_cparams(("arbitrary", "arbitrary")),
    )(h, df, ab, dab)


def mix_in_fwd(l, x, pv, wg, tm=1024):
    T, D = x.shape
    tm = min(tm, T)

    def body(l_ref, x_ref, pv_ref, w_ref, p_ref, h_sc):
        @pl.when(pl.program_id(1) == 0)
        def _():
            h, _, _, _ = _prenorm(x_ref[...], pv_ref)
            h_sc[...] = h.astype(BF16)

        p_ref[...] = _dot(h_sc[...], w_ref[...])

    grid_spec = pltpu.PrefetchScalarGridSpec(
        num_scalar_prefetch=1, grid=(T // tm, N_DEV),
        in_specs=[pl.BlockSpec((tm, D), lambda i, j, l: (i, 0)),
                  pl.BlockSpec((8, D), lambda i, j, l: (0, 0)),
                  pl.BlockSpec((None, None, D, IN_BLK), lambda i, j, l: (l[0], j, 0, 0))],
        out_specs=pl.BlockSpec((tm, IN_BLK), lambda i, j, l: (i, j)),
        scratch_shapes=[pltpu.VMEM((tm, D), BF16)],
    )
    return pl.pallas_call(
        body, name="mix_in_fwd", grid_spec=grid_spec,
        out_shape=jax.ShapeDtypeStruct((T, IN_COLS), F32),
        compiler_params=_cparams(("arbitrary", "arbitrary")),
    )(l, x, pv, wg)


def mix_in_bwd_act(l, dp, dxn, x, pv, wg, tm=1024):
    T, D = x.shape
    tm = min(tm, T)

    def body(l_ref, dp_ref, dxn_ref, x_ref, pv_ref, w_ref, dx_ref, h_ref, pg_ref, dacc):
        i, j = pl.program_id(0), pl.program_id(1)

        @pl.when((i == 0) & (j == 0))
        def _():
            pg_ref[...] = jnp.zeros_like(pg_ref)

        @pl.when(j == 0)
        def _():
            dacc[...] = jnp.zeros_like(dacc)

        dacc[...] += _dot_nt(dp_ref[...], w_ref[...])

        @pl.when(j == N_DEV - 1)
        def _():
            h, _, _, _ = _prenorm(x_ref[...], pv_ref)
            h_ref[...] = h.astype(BF16)
            dx_ref[...] = _prenorm_bwd(dacc[...], dxn_ref[...], x_ref[...], pv_ref, pg_ref)

    tile = pl.BlockSpec((tm, D), lambda i, j, l: (i, 0))
    grid_spec = pltpu.PrefetchScalarGridSpec(
        num_scalar_prefetch=1, grid=(T // tm, N_DEV),
        in_specs=[pl.BlockSpec((tm, IN_BLK), lambda i, j, l: (i, j)), tile, tile,
                  pl.BlockSpec((8, D), lambda i, j, l: (0, 0)),
                  pl.BlockSpec((None, None, D, IN_BLK), lambda i, j, l: (l[0], j, 0, 0))],
        out_specs=[tile, tile, pl.BlockSpec((8, D), lambda i, j, l: (0, 0))],
        scratch_shapes=[pltpu.VMEM((tm, D), F32)],
    )
    return pl.pallas_call(
        body, name="mix_in_bwd_act", grid_spec=grid_spec,
        out_shape=[jax.ShapeDtypeStruct((T, D), F32), jax.ShapeDtypeStruct((T, D), BF16),
                   jax.ShapeDtypeStruct((8, D), F32)],
        compiler_params=_cparams(("arbitrary", "arbitrary")),
    )(l, dp, dxn, x, pv, wg)


def matmul_tn(a, b, tn, tm=1024):
    T, M = a.shape
    tm = min(tm, T)
    N = b.shape[1]
    ni = T // tm

    def body(a_ref, b_ref, o_ref, acc):
        i = pl.program_id(1)

        @pl.when(i == 0)
        def _():
            acc[...] = jnp.zeros_like(acc)

        acc[...] += _dot_tn(a_ref[...], b_ref[...])

        @pl.when(i == ni - 1)
        def _():
            o_ref[...] = acc[...].astype(o_ref.dtype)

    return pl.pallas_call(
        body, name="matmul_tn", grid=(N // tn, ni),
        in_specs=[pl.BlockSpec((tm, M), lambda j, i: (i, 0)), pl.BlockSpec((tm, tn), lambda j, i: (i, j))],
        out_specs=pl.BlockSpec((None, M, tn), lambda j, i: (j, 0, 0)),
        out_shape=jax.ShapeDtypeStruct((N // tn, M, tn), BF16),
        scratch_shapes=[pltpu.VMEM((M, tn), F32)],
        compiler_params=_cparams(("arbitrary", "arbitrary")),
    )(a, b)


SEQ_CHUNK = 256
HALO = 16


def _shift_down(ext, d):
    return pltpu.roll(ext, d, 0)


def _shift_up(ext, d):
    return pltpu.roll(ext, ext.shape[0] - d, 0)


def _rows_with_lead(load, c, width):
    t0 = c * SEQ_CHUNK
    if c == 0:
        return jnp.concatenate([jnp.zeros((HALO, width), F32), load(0, SEQ_CHUNK)], axis=0)
    return load(t0 - HALO, SEQ_CHUNK + HALO)


def _rows_with_tail(load, c, n_chunks, width):
    t0 = c * SEQ_CHUNK
    if c == n_chunks - 1:
        return jnp.concatenate([load(t0, SEQ_CHUNK), jnp.zeros((HALO, width), F32)], axis=0)
    return load(t0, SEQ_CHUNK + HALO)


def conv_fwd(l, p, conv_w):
    T = p.shape[0]
    W = MIX_W
    nC = T // SEQ_CHUNK

    def body(l_ref, p_ref, w_ref, za_ref):
        w0, w1, w2 = w_ref[0:1, :], w_ref[1:2, :], w_ref[2:3, :]
        for c in range(nC):
            ext = _rows_with_lead(lambda s, n: p_ref[s:s + n, W:2 * W] * p_ref[s:s + n, 2 * W:3 * W], c, W)
            y = w2 * ext + w1 * _shift_down(ext, 1) + w0 * _shift_down(ext, 2)
            t0 = c * SEQ_CHUNK
            za_ref[t0:t0 + SEQ_CHUNK, :] = (p_ref[t0:t0 + SEQ_CHUNK, 0:W] * y[HALO:]).astype(BF16)

    grid_spec = pltpu.PrefetchScalarGridSpec(
        num_scalar_prefetch=1, grid=(1,),
        in_specs=[pl.BlockSpec((T, 3 * W), lambda i, l: (0, 0)),
                  pl.BlockSpec((None, 8, W), lambda i, l: (l[0], 0, 0))],
        out_specs=pl.BlockSpec((T, W), lambda i, l: (0, 0)),
    )
    return pl.pallas_call(
        body, name="conv_fwd", grid_spec=grid_spec,
        out_shape=jax.ShapeDtypeStruct((T, W), BF16),
        compiler_params=_cparams(("arbitrary",)),
    )(l, p, conv_w)


def conv_bwd(l, p, dza, conv_w):
    T = p.shape[0]
    W = MIX_W
    nC = T // SEQ_CHUNK

    def body(l_ref, p_ref, dza_ref, w_ref, dp_ref, dw_ref):
        w0, w1, w2 = w_ref[0:1, :], w_ref[1:2, :], w_ref[2:3, :]
        dw = [jnp.zeros((1, W), F32) for _ in range(3)]
        for c in range(nC):
            t0 = c * SEQ_CHUNK
            ext = _rows_with_lead(lambda s, n: p_ref[s:s + n, W:2 * W] * p_ref[s:s + n, 2 * W:3 * W], c, W)
            u1, u2 = _shift_down(ext, 1)[HALO:], _shift_down(ext, 2)[HALO:]
            u0 = ext[HALO:]
            y = w2 * u0 + w1 * u1 + w0 * u2
            dza_c = dza_ref[t0:t0 + SEQ_CHUNK, :]
            dy = dza_c * p_ref[t0:t0 + SEQ_CHUNK, 0:W]
            dw[0] += _colsum(dy * u2)
            dw[1] += _colsum(dy * u1)
            dw[2] += _colsum(dy * u0)
            dye = _rows_with_tail(lambda s, n: dza_ref[s:s + n, :] * p_ref[s:s + n, 0:W], c, nC, W)
            du = (w2 * dye + w1 * _shift_up(dye, 1) + w0 * _shift_up(dye, 2))[:SEQ_CHUNK]
            dp_ref[t0:t0 + SEQ_CHUNK, 0:W] = (dza_c * y).astype(BF16)
            dp_ref[t0:t0 + SEQ_CHUNK, W:2 * W] = (du * p_ref[t0:t0 + SEQ_CHUNK, 2 * W:3 * W]).astype(BF16)
            dp_ref[t0:t0 + SEQ_CHUNK, 2 * W:3 * W] = (du * p_ref[t0:t0 + SEQ_CHUNK, W:2 * W]).astype(BF16)
        dw_ref[...] = jnp.concatenate(dw + [jnp.zeros((5, W), F32)], axis=0)

    grid_spec = pltpu.PrefetchScalarGridSpec(
        num_scalar_prefetch=1, grid=(1,),
        in_specs=[pl.BlockSpec((T, 3 * W), lambda i, l: (0, 0)),
                  pl.BlockSpec((T, W), lambda i, l: (0, 0)),
                  pl.BlockSpec((None, 8, W), lambda i, l: (l[0], 0, 0))],
        out_specs=[pl.BlockSpec((T, 3 * W), lambda i, l: (0, 0)), pl.BlockSpec((8, W), lambda i, l: (0, 0))],
    )
    return pl.pallas_call(
        body, name="conv_bwd", grid_spec=grid_spec,
        out_shape=[jax.ShapeDtypeStruct((T, 3 * W), BF16), jax.ShapeDtypeStruct((8, W), F32)],
        compiler_params=_cparams(("arbitrary",)),
    )(l, p, dza, conv_w)


def _pool_consts(rows, t0):
    lane = lax.broadcasted_iota(jnp.int32, (rows, MIX_W), 1)
    t = lax.broadcasted_iota(jnp.int32, (rows, MIX_W), 0) + t0
    win = jnp.where(lane < 64, 2, jnp.where(lane < 128, 4, jnp.where(lane < 192, 8, 16)))
    inv = 1.0 / jnp.minimum(t + 1, win).astype(F32)
    return lane, inv


def _pick_window(lane, s2, s4, s8, s16):
    return jnp.where(lane < 64, s2, jnp.where(lane < 128, s4, jnp.where(lane < 192, s8, s16)))


def _pooled_chunk(u_ref, c):
    ext = _rows_with_lead(lambda s, n: u_ref[s:s + n, :], c, MIX_W)
    s2 = ext + _shift_down(ext, 1)
    s4 = s2 + _shift_down(s2, 2)
    s8 = s4 + _shift_down(s4, 4)
    s16 = s8 + _shift_down(s8, 8)
    lane, inv = _pool_consts(SEQ_CHUNK, c * SEQ_CHUNK)
    return _pick_window(lane, s2[HALO:], s4[HALO:], s8[HALO:], s16[HALO:]) * inv - ext[HALO:]


def pool_fwd(l, p, w_bd, scale):
    T = p.shape[0]
    W = MIX_W
    nC = T // SEQ_CHUNK

    def body(l_ref, u_ref, w_ref, sc_ref, z_ref):
        for c in range(nC):
            pooled = _pooled_chunk(u_ref, c)
            mixed = _dot(pooled.astype(BF16), w_ref[...])
            z_ref[c * SEQ_CHUNK:(c + 1) * SEQ_CHUNK, :] = (mixed * sc_ref[0:1, :]).astype(BF16)

    grid_spec = pltpu.PrefetchScalarGridSpec(
        num_scalar_prefetch=1, grid=(1,),
        in_specs=[pl.BlockSpec((T, W), lambda i, l: (0, 4)),
                  pl.BlockSpec((None, W, W), lambda i, l: (l[0], 0, 0)),
                  pl.BlockSpec((None, 8, W), lambda i, l: (l[0], 0, 0))],
        out_specs=pl.BlockSpec((T, W), lambda i, l: (0, 0)),
    )
    return pl.pallas_call(
        body, name="pool_fwd", grid_spec=grid_spec,
        out_shape=jax.ShapeDtypeStruct((T, W), BF16),
        compiler_params=_cparams(("arbitrary",)),
    )(l, p, w_bd, scale)


def pool_bwd(l, p, dz, w_bd, scale):
    T = p.shape[0]
    W = MIX_W
    nC = T // SEQ_CHUNK

    def body(l_ref, u_ref, dz_ref, w_ref, sc_ref, du_ref, dw_ref, dsc_ref, e_sc, dpl_sc):
        dw = jnp.zeros((W, W), F32)
        dsc = jnp.zeros((1, W), F32)
        for c in range(nC):
            t0 = c * SEQ_CHUNK
            pooled = _pooled_chunk(u_ref, c).astype(BF16)
            mixed = _dot(pooled, w_ref[...])
            dz_c = dz_ref[t0:t0 + SEQ_CHUNK, :]
            dsc += _colsum(dz_c * mixed)
            dmixed = (dz_c * sc_ref[0:1, :]).astype(BF16)
            dw += _dot_tn(pooled, dmixed)
            dpooled = _dot_nt(dmixed, w_ref[...])
            _, inv = _pool_consts(SEQ_CHUNK, t0)
            dpl_sc[t0:t0 + SEQ_CHUNK, :] = dpooled
            e_sc[t0:t0 + SEQ_CHUNK, :] = dpooled * inv
        for c in range(nC):
            t0 = c * SEQ_CHUNK
            ext = _rows_with_tail(lambda s, n: e_sc[s:s + n, :], c, nC, W)
            s2 = ext + _shift_up(ext, 1)
            s4 = s2 + _shift_up(s2, 2)
            s8 = s4 + _shift_up(s4, 4)
            s16 = s8 + _shift_up(s8, 8)
            lane, _ = _pool_consts(SEQ_CHUNK, t0)
            n = SEQ_CHUNK
            du = _pick_window(lane, s2[:n], s4[:n], s8[:n], s16[:n]) - dpl_sc[t0:t0 + SEQ_CHUNK, :]
            du_ref[t0:t0 + SEQ_CHUNK, :] = du.astype(BF16)
        dw_ref[...] = dw
        dsc_ref[...] = jnp.concatenate([dsc, jnp.zeros((7, W), F32)], axis=0)

    grid_spec = pltpu.PrefetchScalarGridSpec(
        num_scalar_prefetch=1, grid=(1,),
        in_specs=[pl.BlockSpec((T, W), lambda i, l: (0, 4)),
                  pl.BlockSpec((T, W), lambda i, l: (0, 0)),
                  pl.BlockSpec((None, W, W), lambda i, l: (l[0], 0, 0)),
                  pl.BlockSpec((None, 8, W), lambda i, l: (l[0], 0, 0))],
        out_specs=[pl.BlockSpec((T, W), lambda i, l: (0, 0)), pl.BlockSpec((W, W), lambda i, l: (0, 0)),
                   pl.BlockSpec((8, W), lambda i, l: (0, 0))],
        scratch_shapes=[pltpu.VMEM((T, W), F32), pltpu.VMEM((T, W), F32)],
    )
    return pl.pallas_call(
        body, name="pool_bwd", grid_spec=grid_spec,
        out_shape=[jax.ShapeDtypeStruct((T, W), BF16), jax.ShapeDtypeStruct((W, W), F32),
                   jax.ShapeDtypeStruct((8, W), F32)],
        compiler_params=_cparams(("arbitrary",)),
    )(l, p, dz, w_bd, scale)


def _s5_disc(lre, lim, ldt):
    lr = jnp.minimum(lre, DT_LAMBDA_RE_MAX)
    dt = jnp.exp(ldt)
    mag = jnp.exp(lr * dt)
    a_re = mag * jnp.cos(lim * dt)
    a_im = mag * jnp.sin(lim * dt)
    den = lr * lr + lim * lim
    nr = a_re - 1.0
    return a_re, a_im, (nr * lr + a_im * lim) / den, (a_im * lr - nr * lim) / den


def _bd_mask(shape, row_blk, col_blk):
    r = lax.broadcasted_iota(jnp.int32, shape, 0) >> (row_blk.bit_length() - 1)
    c = lax.broadcasted_iota(jnp.int32, shape, 1) >> (col_blk.bit_length() - 1)
    return r == c


def s5_params(lam, b_t, c_t):
    L = lam.shape[0]

    def body(lam_ref, b_ref, c_ref, a_ref, bbd_ref, cbd_ref):
        a_re, a_im, f_re, f_im = _s5_disc(lam_ref[0:1, :], lam_ref[1:2, :], lam_ref[2:3, :])
        a_ref[...] = jnp.concatenate([a_re, a_im, jnp.zeros((6, SSM_W), F32)], axis=0)
        mb = _bd_mask((MIX_W, SSM_W), SSM_GROUP, SSM_STATE)
        bbd_ref[0] = jnp.where(mb, f_re * b_ref[0] - f_im * b_ref[1], 0.0).astype(BF16)
        bbd_ref[1] = jnp.where(mb, f_re * b_ref[1] + f_im * b_ref[0], 0.0).astype(BF16)
        mc = _bd_mask((SSM_W, MIX_W), SSM_STATE, SSM_GROUP)
        cbd_ref[0] = jnp.where(mc, c_ref[0], 0.0).astype(BF16)
        cbd_ref[1] = jnp.where(mc, c_ref[1], 0.0).astype(BF16)

    return pl.pallas_call(
        body, name="s5_params", grid=(L,),
        in_specs=[pl.BlockSpec((None, 8, SSM_W), lambda l: (l, 0, 0)),
                  pl.BlockSpec((None, 2, MIX_W, SSM_W), lambda l: (l, 0, 0, 0)),
                  pl.BlockSpec((None, 2, SSM_W, MIX_W), lambda l: (l, 0, 0, 0))],
        out_specs=[pl.BlockSpec((None, 8, SSM_W), lambda l: (l, 0, 0)),
                   pl.BlockSpec((None, 2, MIX_W, SSM_W), lambda l: (l, 0, 0, 0)),
                   pl.BlockSpec((None, 2, SSM_W, MIX_W), lambda l: (l, 0, 0, 0))],
        out_shape=[jax.ShapeDtypeStruct((L, 8, SSM_W), F32),
                   jax.ShapeDtypeStruct((L, 2, MIX_W, SSM_W), BF16),
                   jax.ShapeDtypeStruct((L, 2, SSM_W, MIX_W), BF16)],
        compiler_params=_cparams(("arbitrary",)),
    )(lam, b_t, c_t)


def s5_params_bwd(lam, b_t, gb, gc, da):
    L = lam.shape[0]

    def body(lam_ref, b_ref, gb_ref, gc_ref, da_ref, dlam_ref, db_ref, dc_ref, dgrp_ref):
        lre, lim, ldt = lam_ref[0:1, :], lam_ref[1:2, :], lam_ref[2:3, :]
        (a_re, a_im, f_re, f_im), vjp = jax.vjp(_s5_disc, lre, lim, ldt)
        mb = _bd_mask((MIX_W, SSM_W), SSM_GROUP, SSM_STATE)
        gbr = jnp.where(mb, gb_ref[0], 0.0)
        gbi = jnp.where(mb, gb_ref[1], 0.0)
        df_re = _colsum(gbr * b_ref[0] + gbi * b_ref[1])
        df_im = _colsum(gbi * b_ref[0] - gbr * b_ref[1])
        db_ref[0] = f_re * gbr + f_im * gbi
        db_ref[1] = f_re * gbi - f_im * gbr
        mc = _bd_mask((SSM_W, MIX_W), SSM_STATE, SSM_GROUP)
        dc_ref[0] = jnp.where(mc, gc_ref[0], 0.0)
        dc_ref[1] = jnp.where(mc, gc_ref[1], 0.0)
        dlre, dlim, dldt = vjp((da_ref[0:1, :], da_ref[1:2, :], df_re, df_im))
        dl = jnp.concatenate([dlre, dlim, dldt, jnp.zeros((5, SSM_W), F32)], axis=0)
        dlam_ref[...] = dl
        grp = jnp.where(_bd_mask((SSM_W, 128), SSM_STATE, 1), 1.0, 0.0)
        dgrp_ref[...] = jnp.dot(dl, grp, preferred_element_type=F32, precision=lax.Precision.HIGHEST)

    vec = pl.BlockSpec((None, 8, SSM_W), lambda l: (l, 0, 0))
    bsp = pl.BlockSpec((None, 2, MIX_W, SSM_W), lambda l: (l, 0, 0, 0))
    csp = pl.BlockSpec((None, 2, SSM_W, MIX_W), lambda l: (l, 0, 0, 0))
    return pl.pallas_call(
        body, name="s5_params_bwd", grid=(L,),
        in_specs=[vec, bsp, bsp, csp, vec],
        out_specs=[vec, bsp, csp, pl.BlockSpec((None, 8, 128), lambda l: (l, 0, 0))],
        out_shape=[jax.ShapeDtypeStruct((L, 8, SSM_W), F32),
                   jax.ShapeDtypeStruct((L, 2, MIX_W, SSM_W), F32),
                   jax.ShapeDtypeStruct((L, 2, SSM_W, MIX_W), F32),
                   jax.ShapeDtypeStruct((L, 8, 128), F32)],
        compiler_params=_cparams(("arbitrary",)),
    )(lam, b_t, gb, gc, da)


def s5_bu(l, p, b_bd, tm=512):
    T = p.shape[0]

    def body(l_ref, u_ref, b_ref, bu_ref):
        u = u_ref[...].astype(BF16)
        bu_ref[0] = _dot(u, b_ref[0])
        bu_ref[1] = _dot(u, b_ref[1])

    grid_spec = pltpu.PrefetchScalarGridSpec(
        num_scalar_prefetch=1, grid=(T // tm,),
        in_specs=[pl.BlockSpec((tm, MIX_W), lambda i, l: (i, 3)),
                  pl.BlockSpec((None, 2, MIX_W, SSM_W), lambda i, l: (l[0], 0, 0, 0))],
        out_specs=pl.BlockSpec((2, tm, SSM_W), lambda i, l: (0, i, 0)),
    )
    return pl.pallas_call(
        body, name="s5_bu", grid_spec=grid_spec,
        out_shape=jax.ShapeDtypeStruct((2, T, SSM_W), F32),
        compiler_params=_cparams(("arbitrary",)),
    )(l, p, b_bd)


def s5_scan(l, avec, xs, reverse):
    T = xs.shape[1]
    CH = SEQ_CHUNK
    nC = T // CH
    LW = 128
    n_steps = CH.bit_length() - 1

    def body(l_ref, a_ref, x_ref, s_ref):
        ar = a_ref[0:1, :]
        ai = -a_ref[1:2, :] if reverse else a_ref[1:2, :]
        pows = [(ar, ai)]
        for _ in range(n_steps - 1):
            r, i = pows[-1]
            pows.append((r * r - i * i, 2.0 * r * i))
        row = lax.broadcasted_iota(jnp.int32, (CH, LW), 0)

        def local_scan(re, im):
            for k in range(n_steps):
                d = 1 << k
                pr, pi = pows[k]
                if reverse:
                    keep = row < CH - d
                    sr, si = _shift_up(re, d), _shift_up(im, d)
                else:
                    keep = row >= d
                    sr, si = _shift_down(re, d), _shift_down(im, d)
                sr = jnp.where(keep, sr, 0.0)
                si = jnp.where(keep, si, 0.0)
                re, im = re + pr * sr - pi * si, im + pr * si + pi * sr
            return re, im

        edge = CH - 1 if reverse else 0
        pw_re, pw_im = local_scan(jnp.where(row == edge, ar, 0.0), jnp.where(row == edge, ai, 0.0))
        last = 0 if reverse else CH - 1

        def chunk(c, carry):
            cr, ci = carry
            cc = nC - 1 - c if reverse else c
            t0 = pl.multiple_of(cc * CH, CH)
            re, im = local_scan(x_ref[0, pl.ds(t0, CH), :], x_ref[1, pl.ds(t0, CH), :])
            re2 = re + pw_re * cr - pw_im * ci
            im2 = im + pw_re * ci + pw_im * cr
            s_ref[0, pl.ds(t0, CH), :] = re2
            s_ref[1, pl.ds(t0, CH), :] = im2
            return re2[last:last + 1, :], im2[last:last + 1, :]

        lax.fori_loop(0, nC, chunk, (jnp.zeros((1, LW), F32), jnp.zeros((1, LW), F32)))

    grid_spec = pltpu.PrefetchScalarGridSpec(
        num_scalar_prefetch=1, grid=(SSM_W // LW,),
        in_specs=[pl.BlockSpec((None, 8, LW), lambda g, l: (l[0], 0, g)),
                  pl.BlockSpec((2, T, LW), lambda g, l: (0, 0, g))],
        out_specs=pl.BlockSpec((2, T, LW), lambda g, l: (0, 0, g)),
    )
    return pl.pallas_call(
        body, name="s5_scan_rev" if reverse else "s5_scan_fwd", grid_spec=grid_spec,
        out_shape=jax.ShapeDtypeStruct((2, T, SSM_W), F32),
        compiler_params=_cparams(("arbitrary",)),
    )(l, avec, xs)


_GELU_C = 0.7978845608028654
_GELU_K = 0.044715


def _s5_y(u, s_ref, c_ref, d_row):
    y = _dot(s_ref[0].astype(BF16), c_ref[0]) - _dot(s_ref[1].astype(BF16), c_ref[1])
    return y + d_row * u


def s5_out(l, p, s, c_bd, ssm_d, tm=512):
    T = p.shape[0]

    def body(l_ref, u_ref, s_ref, c_ref, d_ref, yg_ref):
        y = _s5_y(u_ref[...], s_ref, c_ref, d_ref[0:1, :])
        th = jnp.tanh(_GELU_C * (y + _GELU_K * y * y * y))
        yg_ref[...] = (0.5 * y * (1.0 + th)).astype(BF16)

    grid_spec = pltpu.PrefetchScalarGridSpec(
        num_scalar_prefetch=1, grid=(T // tm,),
        in_specs=[pl.BlockSpec((tm, MIX_W), lambda i, l: (i, 3)),
                  pl.BlockSpec((2, tm, SSM_W), lambda i, l: (0, i, 0)),
                  pl.BlockSpec((None, 2, SSM_W, MIX_W), lambda i, l: (l[0], 0, 0, 0)),
                  pl.BlockSpec((None, 8, MIX_W), lambda i, l: (l[0], 0, 0))],
        out_specs=pl.BlockSpec((tm, MIX_W), lambda i, l: (i, 0)),
    )
    return pl.pallas_call(
        body, name="s5_out", grid_spec=grid_spec,
        out_shape=jax.ShapeDtypeStruct((T, MIX_W), BF16),
        compiler_params=_cparams(("arbitrary",)),
    )(l, p, s, c_bd, ssm_d)


def s5_bwd_y(l, p, s, dyg, c_bd, ssm_d, tm=512):
    T = p.shape[0]

    def body(l_ref, u_ref, s_ref, dyg_ref, c_ref, d_ref, ds_ref, du_ref, gc_ref, dd_ref):
        @pl.when(pl.program_id(0) == 0)
        def _():
            gc_ref[...] = jnp.zeros_like(gc_ref)
            dd_ref[...] = jnp.zeros_like(dd_ref)

        u = u_ref[...]
        y = _s5_y(u, s_ref, c_ref, d_ref[0:1, :])
        inner = _GELU_C * (y + _GELU_K * y * y * y)
        th = jnp.tanh(inner)
        dgelu = 0.5 * (1.0 + th) + 0.5 * y * (1.0 - th * th) * (_GELU_C * (1.0 + 3.0 * _GELU_K * y * y))
        dy = dyg_ref[...] * dgelu
        dd_ref[0:1, :] += _colsum(dy * u)
        du_ref[...] = dy * d_ref[0:1, :]
        dyb = dy.astype(BF16)
        ds_ref[0] = _dot_nt(dyb, c_ref[0])
        ds_ref[1] = -_dot_nt(dyb, c_ref[1])
        gc_ref[0] += _dot_tn(s_ref[0].astype(BF16), dyb)
        gc_ref[1] -= _dot_tn(s_ref[1].astype(BF16), dyb)

    grid_spec = pltpu.PrefetchScalarGridSpec(
        num_scalar_prefetch=1, grid=(T // tm,),
        in_specs=[pl.BlockSpec((tm, MIX_W), lambda i, l: (i, 3)),
                  pl.BlockSpec((2, tm, SSM_W), lambda i, l: (0, i, 0)),
                  pl.BlockSpec((tm, MIX_W), lambda i, l: (i, 0)),
                  pl.BlockSpec((None, 2, SSM_W, MIX_W), lambda i, l: (l[0], 0, 0, 0)),
                  pl.BlockSpec((None, 8, MIX_W), lambda i, l: (l[0], 0, 0))],
        out_specs=[pl.BlockSpec((2, tm, SSM_W), lambda i, l: (0, i, 0)),
                   pl.BlockSpec((tm, MIX_W), lambda i, l: (i, 0)),
                   pl.BlockSpec((2, SSM_W, MIX_W), lambda i, l: (0, 0, 0)),
                   pl.BlockSpec((8, MIX_W), lambda i, l: (0, 0))],
    )
    return pl.pallas_call(
        body, name="s5_bwd_y", grid_spec=grid_spec,
        out_shape=[jax.ShapeDtypeStruct((2, T, SSM_W), F32), jax.ShapeDtypeStruct((T, MIX_W), F32),
                   jax.ShapeDtypeStruct((2, SSM_W, MIX_W), F32), jax.ShapeDtypeStruct((8, MIX_W), F32)],
        compiler_params=_cparams(("arbitrary",)),
    )(l, p, s, dyg, c_bd, ssm_d)


def s5_bwd_u(l, p, lam_s, du_skip, b_bd, tm=512):
    T = p.shape[0]

    def body(l_ref, u_ref, ls_ref, dus_ref, b_ref, du_ref, gb_ref):
        @pl.when(pl.program_id(0) == 0)
        def _():
            gb_ref[...] = jnp.zeros_like(gb_ref)

        u = u_ref[...].astype(BF16)
        lr = ls_ref[0].astype(BF16)
        li = ls_ref[1].astype(BF16)
        gb_ref[0] += _dot_tn(u, lr)
        gb_ref[1] += _dot_tn(u, li)
        du_ref[...] = (dus_ref[...] + _dot_nt(lr, b_ref[0]) + _dot_nt(li, b_ref[1])).astype(BF16)

    grid_spec = pltpu.PrefetchScalarGridSpec(
        num_scalar_prefetch=1, grid=(T // tm,),
        in_specs=[pl.BlockSpec((tm, MIX_W), lambda i, l: (i, 3)),
                  pl.BlockSpec((2, tm, SSM_W), lambda i, l: (0, i, 0)),
                  pl.BlockSpec((tm, MIX_W), lambda i, l: (i, 0)),
                  pl.BlockSpec((None, 2, MIX_W, SSM_W), lambda i, l: (l[0], 0, 0, 0))],
        out_specs=[pl.BlockSpec((tm, MIX_W), lambda i, l: (i, 0)),
                   pl.BlockSpec((2, MIX_W, SSM_W), lambda i, l: (0, 0, 0))],
    )
    return pl.pallas_call(
        body, name="s5_bwd_u", grid_spec=grid_spec,
        out_shape=[jax.ShapeDtypeStruct((T, MIX_W), BF16), jax.ShapeDtypeStruct((2, MIX_W, SSM_W), F32)],
        compiler_params=_cparams(("arbitrary",)),
    )(l, p, lam_s, du_skip, b_bd)


def s5_bwd_a(s, lam_s):
    T = s.shape[1]
    nC = T // SEQ_CHUNK
    LW = 128

    def body(s_ref, ls_ref, da_ref):
        dre = jnp.zeros((1, LW), F32)
        dim = jnp.zeros((1, LW), F32)
        for c in range(nC):
            t0 = c * SEQ_CHUNK
            sr = _shift_down(_rows_with_lead(lambda a, n: s_ref[0, a:a + n, :], c, LW), 1)[HALO:]
            si = _shift_down(_rows_with_lead(lambda a, n: s_ref[1, a:a + n, :], c, LW), 1)[HALO:]
            lr = ls_ref[0, t0:t0 + SEQ_CHUNK, :]
            li = ls_ref[1, t0:t0 + SEQ_CHUNK, :]
            dre += _colsum(sr * lr + si * li)
            dim += _colsum(sr * li - si * lr)
        da_ref[...] = jnp.concatenate([dre, dim, jnp.zeros((6, LW), F32)], axis=0)

    blk = pl.BlockSpec((2, T, LW), lambda g: (0, 0, g))
    return pl.pallas_call(
        body, name="s5_bwd_a", grid=(SSM_W // LW,),
        in_specs=[blk, blk],
        out_specs=pl.BlockSpec((8, LW), lambda g: (0, g)),
        out_shape=jax.ShapeDtypeStruct((8, SSM_W), F32),
        compiler_params=_cparams(("arbitrary",)),
    )(s, lam_s)


SB_BLK = 128
SB_SCALE = SB_HEAD ** -0.5


def _split_bf16(x):
    hi = x.astype(BF16)
    return hi, (x - hi.astype(F32)).astype(BF16)


def _dot_split(x, tri):
    hi, lo = _split_bf16(x)
    return _dot(hi, tri) + _dot(lo, tri)


SB_SLABS = MIX_W // SB_BLK
SB_STACK = 2 * SB_SLABS * SB_BLK
SB_PAIR = 2 * SB_BLK


def _sb_valid(r0, c0):
    row = (lax.broadcasted_iota(jnp.int32, (SB_STACK, SB_BLK), 0) & (SB_BLK - 1)) + r0
    col = lax.broadcasted_iota(jnp.int32, (SB_STACK, SB_BLK), 1) + c0
    return col < row


def _sb_stack(ref, r0, scale):
    lane = lax.broadcasted_iota(jnp.int32, (SB_BLK, SB_BLK), 1)
    parts = []
    for s in range(SB_SLABS):
        blk = ref[pl.ds(r0, SB_BLK), s * SB_BLK:(s + 1) * SB_BLK] * scale
        parts += [jnp.where(lane < SB_HEAD, blk, 0.0), jnp.where(lane >= SB_HEAD, blk, 0.0)]
    return jnp.concatenate(parts, axis=0).astype(BF16)


def _sb_rows_nt(stack, ref, c0):
    return jnp.concatenate(
        [_dot_nt(stack[s * SB_PAIR:(s + 1) * SB_PAIR], ref[pl.ds(c0, SB_BLK), s * SB_BLK:(s + 1) * SB_BLK].astype(BF16))
         for s in range(SB_SLABS)], axis=0)


def _sb_wide(stack, s):
    return jnp.concatenate([stack[s * SB_PAIR:s * SB_PAIR + SB_BLK], stack[s * SB_PAIR + SB_BLK:(s + 1) * SB_PAIR]],
                           axis=1)


def _sb_logits(q_stack, k_ref, c0, valid):
    z = _sb_rows_nt(q_stack, k_ref, c0)
    sp = jnp.log(1.0 + jnp.exp(-jnp.abs(z)))
    ls_pos = jnp.minimum(z, 0.0) - sp
    lk = jnp.where(valid, jnp.minimum(-z, 0.0) - sp, 0.0)
    return z, ls_pos, lk


def _tri(lower):
    r = lax.broadcasted_iota(jnp.int32, (SB_BLK, SB_BLK), 0)
    c = lax.broadcasted_iota(jnp.int32, (SB_BLK, SB_BLK), 1)
    return jnp.where(r > c if lower else r < c, 1.0, 0.0).astype(BF16)


def sb_fwd(p):
    T = p.shape[0]
    W = MIX_W
    nB = T // SB_BLK

    def body(q_ref, k_ref, v_ref, o_ref, tot_ref, acc_sc):
        tri = _tri(True)

        def qblock(i, _):
            r0 = pl.multiple_of(i * SB_BLK, SB_BLK)
            q = _sb_stack(q_ref, r0, SB_SCALE)
            acc_sc[...] = jnp.zeros_like(acc_sc)

            def kblock(jj, run):
                c0 = pl.multiple_of((i - jj) * SB_BLK, SB_BLK)
                valid = _sb_valid(r0, c0)
                _, ls_pos, lk = _sb_logits(q, k_ref, c0, valid)
                logw = ls_pos + _dot_split(lk, tri) + run
                a = jnp.where(valid, jnp.exp(logw), 0.0).astype(BF16)
                v = _sb_stack(v_ref, c0, 1.0)
                for s in range(SB_SLABS):
                    acc_sc[:, s * SB_BLK:(s + 1) * SB_BLK] += _dot(_sb_wide(a, s), v[s * SB_PAIR:(s + 1) * SB_PAIR])
                return run + jnp.sum(lk, axis=1, keepdims=True)

            total = lax.fori_loop(0, i + 1, kblock, jnp.zeros((SB_STACK, 1), F32))
            o_ref[pl.ds(r0, SB_BLK), :] = acc_sc[...].astype(BF16)
            tot_ref[pl.ds(pl.multiple_of(i * SB_STACK, SB_STACK), SB_STACK), :] = jnp.broadcast_to(total, (SB_STACK, SB_BLK))
            return 0

        lax.fori_loop(0, nB, qblock, 0)

    return pl.pallas_call(
        body, name="sb_fwd", grid=(1,),
        in_specs=[pl.BlockSpec((T, W), lambda i: (0, 5)), pl.BlockSpec((T, W), lambda i: (0, 6)),
                  pl.BlockSpec((T, W), lambda i: (0, 7))],
        out_specs=[pl.BlockSpec((T, W), lambda i: (0, 0)), pl.BlockSpec((nB * SB_STACK, SB_BLK), lambda i: (0, 0))],
        out_shape=[jax.ShapeDtypeStruct((T, W), BF16), jax.ShapeDtypeStruct((nB * SB_STACK, SB_BLK), F32)],
        scratch_shapes=[pltpu.VMEM((SB_BLK, W), F32)],
        compiler_params=_cparams(("arbitrary",)),
    )(p, p, p)


def sb_bwd(p, do, tot):
    T = p.shape[0]
    W = MIX_W
    nB = T // SB_BLK

    def body(q_ref, k_ref, v_ref, do_ref, tot_ref, dqkv_ref, dq_sc, dk_sc, dv_sc):
        tri_gt = _tri(True)
        tri_lt = _tri(False)
        dq_sc[...] = jnp.zeros_like(dq_sc)
        dk_sc[...] = jnp.zeros_like(dk_sc)
        dv_sc[...] = jnp.zeros_like(dv_sc)
        zcol = jnp.zeros((SB_STACK, 1), F32)

        def qblock(i, _):
            r0 = pl.multiple_of(i * SB_BLK, SB_BLK)
            q = _sb_stack(q_ref, r0, SB_SCALE)
            dob = _sb_stack(do_ref, r0, 1.0)

            total = tot_ref[pl.ds(pl.multiple_of(i * SB_STACK, SB_STACK), SB_STACK), 0:1]

            def kblock(j, carry):
                pre, seen = carry
                c0 = pl.multiple_of(j * SB_BLK, SB_BLK)
                valid = _sb_valid(r0, c0)
                z, ls_pos, lk = _sb_logits(q, k_ref, c0, valid)
                seen = seen + jnp.sum(lk, axis=1, keepdims=True)
                logw = ls_pos + _dot_split(lk, tri_gt) + (total - seen)
                a = jnp.where(valid, jnp.exp(logw), 0.0)
                dlw = _sb_rows_nt(dob, v_ref, c0) * a
                g = pre + _dot_split(dlw, tri_lt)
                sig = _sigmoid(z)
                dz = jnp.where(valid, dlw * (1.0 - sig) - g * sig, 0.0).astype(BF16)
                ab = a.astype(BF16)
                km = _sb_stack(k_ref, c0, 1.0)
                for s in range(SB_SLABS):
                    pair = slice(s * SB_PAIR, (s + 1) * SB_PAIR)
                    ls = slice(s * SB_BLK, (s + 1) * SB_BLK)
                    dk_sc[pl.ds(c0, SB_BLK), ls] += _dot_tn(dz[pair], q[pair])
                    dv_sc[pl.ds(c0, SB_BLK), ls] += _dot_tn(ab[pair], dob[pair])
                    dq_sc[pl.ds(r0, SB_BLK), ls] += _dot(_sb_wide(dz, s), km[pair])
                return pre + jnp.sum(dlw, axis=1, keepdims=True), seen

            lax.fori_loop(0, i + 1, kblock, (zcol, zcol))
            return 0

        lax.fori_loop(0, nB, qblock, 0)
        dqkv_ref[:, 0:W] = (dq_sc[...] * SB_SCALE).astype(BF16)
        dqkv_ref[:, W:2 * W] = dk_sc[...].astype(BF16)
        dqkv_ref[:, 2 * W:3 * W] = dv_sc[...].astype(BF16)

    return pl.pallas_call(
        body, name="sb_bwd", grid=(1,),
        in_specs=[pl.BlockSpec((T, W), lambda i: (0, 5)), pl.BlockSpec((T, W), lambda i: (0, 6)),
                  pl.BlockSpec((T, W), lambda i: (0, 7)), pl.BlockSpec((T, W), lambda i: (0, 0)),
                  pl.BlockSpec((nB * SB_STACK, SB_BLK), lambda i: (0, 0))],
        out_specs=pl.BlockSpec((T, 3 * W), lambda i: (0, 0)),
        out_shape=jax.ShapeDtypeStruct((T, 3 * W), BF16),
        scratch_shapes=[pltpu.VMEM((T, W), F32), pltpu.VMEM((T, W), F32), pltpu.VMEM((T, W), F32)],
        compiler_params=_cparams(("arbitrary",)),
    )(p, p, p, do, tot)


def _dot_cols(a, w_ref):
    return jnp.concatenate([_dot(a, w_ref[j]) for j in range(N_DEV)], axis=1)


def _dot_cols_nt(dy, w_ref):
    n = w_ref.shape[2]
    out = _dot_nt(dy[:, 0:n], w_ref[0])
    for j in range(1, N_DEV):
        out += _dot_nt(dy[:, j * n:(j + 1) * n], w_ref[j])
    return out


def _acc_cols_tn(acc_ref, a, dy):
    n = acc_ref.shape[2]
    for j in range(N_DEV):
        acc_ref[j] += _dot_tn(a, dy[:, j * n:(j + 1) * n])


def _merge_branches(za_ref, yg_ref, z_ref, o_ref, gate_refs, wc_ref, wglu_ref, wp_ref, ws_ref):
    D = D_MODEL
    glu = _dot_cols(yg_ref[...], wglu_ref)
    glu_a, sg = glu[:, :D], _sigmoid(glu[:, D:])
    ys = [_dot_cols(za_ref[...], wc_ref), glu_a * sg, _dot_cols(z_ref[...], wp_ref), _dot_cols(o_ref[...], ws_ref)]
    gs = [_sigmoid(g[...]) for g in gate_refs]
    merged = gs[0] * ys[0] + gs[1] * ys[1] + gs[2] * ys[2] + gs[3] * ys[3]
    return ys, gs, glu_a, sg, merged


def _merge_specs(tm, D):
    W = MIX_W
    br = pl.BlockSpec((tm, W), lambda i, l: (i, 0))
    gates = [pl.BlockSpec((tm, D), functools.partial(lambda i, l, b: (i, 2 + b), b=b)) for b in range(4)]
    wsm = pl.BlockSpec((None, N_DEV, W, D // N_DEV), lambda i, l: (l[0], 0, 0, 0))
    weights = [wsm, pl.BlockSpec((None, N_DEV, W, 2 * D // N_DEV), lambda i, l: (l[0], 0, 0, 0)), wsm, wsm,
               pl.BlockSpec((None, D, D), lambda i, l: (l[0], 0, 0))]
    return [br] * 4 + gates, weights


def merge_fwd(l, p, za, yg, z, o, x, pv, wc, wglu, wp, ws, wo, tm=256):
    T, D = x.shape

    def body(l_ref, za_ref, yg_ref, z_ref, o_ref, g0, g1, g2, g3, x_ref, pv_ref,
             wc_ref, wglu_ref, wp_ref, ws_ref, wo_ref, xn_ref, m_ref):
        _, _, _, _, merged = _merge_branches(za_ref, yg_ref, z_ref, o_ref, (g0, g1, g2, g3),
                                             wc_ref, wglu_ref, wp_ref, ws_ref)
        m = _dot(merged.astype(BF16), wo_ref[...])
        m_ref[...] = m
        xn_ref[...] = _postnorm_res(x_ref[...], m, pv_ref, 1.0)

    acts, weights = _merge_specs(tm, D)
    tile = pl.BlockSpec((tm, D), lambda i, l: (i, 0))
    grid_spec = pltpu.PrefetchScalarGridSpec(
        num_scalar_prefetch=1, grid=(T // tm,),
        in_specs=acts + [tile, pl.BlockSpec((8, D), lambda i, l: (0, 0))] + weights,
        out_specs=[tile, tile],
    )
    return pl.pallas_call(
        body, name="merge_fwd", grid_spec=grid_spec,
        out_shape=[jax.ShapeDtypeStruct((T, D), F32), jax.ShapeDtypeStruct((T, D), F32)],
        compiler_params=_cparams(("arbitrary",)),
    )(l, za, yg, z, o, p, p, p, p, x, pv, wc, wglu, wp, ws, wo)


def merge_bwd(l, p, za, yg, z, o, m, dxn, pv, wc, wglu, wp, ws, wo, tm=128):
    T, D = m.shape
    W = MIX_W
    ni = T // tm

    def body(l_ref, za_ref, yg_ref, z_ref, o_ref, g0, g1, g2, g3, m_ref, dxn_ref, pv_ref,
             wc_ref, wglu_ref, wp_ref, ws_ref, wo_ref,
             dza_ref, dyg_ref, dz_ref, do_ref, dg_ref, pg_ref, gwc_ref, gwglu_ref, gwp_ref, gws_ref, gwo_ref,
             awc, awglu, awp, aws, awo):
        i = pl.program_id(0)

        @pl.when(i == 0)
        def _():
            pg_ref[...] = jnp.zeros_like(pg_ref)
            for a in (awc, awglu, awp, aws, awo):
                a[...] = jnp.zeros_like(a)

        ys, gs, glu_a, sg, merged = _merge_branches(za_ref, yg_ref, z_ref, o_ref, (g0, g1, g2, g3),
                                                    wc_ref, wglu_ref, wp_ref, ws_ref)
        dm = _postnorm_bwd(dxn_ref[...], m_ref[...], pv_ref, pg_ref, 1.0).astype(BF16)
        awo[...] += _dot_tn(merged.astype(BF16), dm)
        dmerged = _dot_nt(dm, wo_ref[...])
        for b in range(4):
            dg_ref[:, b * D:(b + 1) * D] = (dmerged * ys[b] * gs[b] * (1.0 - gs[b])).astype(BF16)
        dya = (dmerged * gs[0]).astype(BF16)
        _acc_cols_tn(awc, za_ref[...], dya)
        dza_ref[...] = _dot_cols_nt(dya, wc_ref)
        dyc = (dmerged * gs[2]).astype(BF16)
        _acc_cols_tn(awp, z_ref[...], dyc)
        dz_ref[...] = _dot_cols_nt(dyc, wp_ref)
        dyd = (dmerged * gs[3]).astype(BF16)
        _acc_cols_tn(aws, o_ref[...], dyd)
        do_ref[...] = _dot_cols_nt(dyd, ws_ref)
        dyb = dmerged * gs[1]
        dglu = jnp.concatenate([dyb * sg, dyb * glu_a * sg * (1.0 - sg)], axis=1).astype(BF16)
        _acc_cols_tn(awglu, yg_ref[...], dglu)
        dyg_ref[...] = _dot_cols_nt(dglu, wglu_ref)

        @pl.when(i == ni - 1)
        def _():
            gwc_ref[...] = awc[...].astype(BF16)
            gwglu_ref[...] = awglu[...].astype(BF16)
            gwp_ref[...] = awp[...].astype(BF16)
            gws_ref[...] = aws[...].astype(BF16)
            gwo_ref[...] = awo[...].astype(BF16)

    acts, weights = _merge_specs(tm, D)
    tile = pl.BlockSpec((tm, D), lambda i, l: (i, 0))
    br = pl.BlockSpec((tm, W), lambda i, l: (i, 0))
    full = lambda *s: pl.BlockSpec(s, lambda i, l: (0,) * len(s))
    sm, glu_s = (N_DEV, W, D // N_DEV), (N_DEV, W, 2 * D // N_DEV)
    grid_spec = pltpu.PrefetchScalarGridSpec(
        num_scalar_prefetch=1, grid=(ni,),
        in_specs=acts + [tile, tile, pl.BlockSpec((8, D), lambda i, l: (0, 0))] + weights,
        out_specs=[br, br, br, br, pl.BlockSpec((tm, 4 * D), lambda i, l: (i, 0)), full(8, D),
                   full(*sm), full(*glu_s), full(*sm), full(*sm), full(D, D)],
        scratch_shapes=[pltpu.VMEM(sm, F32), pltpu.VMEM(glu_s, F32), pltpu.VMEM(sm, F32),
                        pltpu.VMEM(sm, F32), pltpu.VMEM((D, D), F32)],
    )
    f32br = jax.ShapeDtypeStruct((T, W), F32)
    return pl.pallas_call(
        body, name="merge_bwd", grid_spec=grid_spec,
        out_shape=[f32br, f32br, f32br, f32br, jax.ShapeDtypeStruct((T, 4 * D), BF16),
                   jax.ShapeDtypeStruct((8, D), F32),
                   jax.ShapeDtypeStruct(sm, BF16), jax.ShapeDtypeStruct(glu_s, BF16),
                   jax.ShapeDtypeStruct(sm, BF16), jax.ShapeDtypeStruct(sm, BF16),
                   jax.ShapeDtypeStruct((D, D), BF16)],
        compiler_params=_cparams(("arbitrary",)),
    )(l, za, yg, z, o, p, p, p, p, m, dxn, pv, wc, wglu, wp, ws, wo)


def dp_assemble(d_conv, d_ssm, d_pool, d_qkv, d_gates, tm=512):
    T = d_conv.shape[0]
    W = MIX_W

    def body(c_ref, s_ref, p_ref, q_ref, g_ref, dp_ref):
        dp_ref[:, 0:3 * W] = c_ref[...]
        dp_ref[:, 3 * W:4 * W] = s_ref[...]
        dp_ref[:, 4 * W:5 * W] = p_ref[...]
        dp_ref[:, 5 * W:8 * W] = q_ref[...]
        dp_ref[:, GATE_OFF:] = g_ref[...]

    row = lambda w: pl.BlockSpec((tm, w), lambda i: (i, 0))
    return pl.pallas_call(
        body, name="dp_assemble", grid=(T // tm,),
        in_specs=[row(3 * W), row(W), row(W), row(3 * W), row(4 * D_MODEL)],
        out_specs=row(IN_COLS),
        out_shape=jax.ShapeDtypeStruct((T, IN_COLS), BF16),
        compiler_params=_cparams(("arbitrary",)),
    )(d_conv, d_ssm, d_pool, d_qkv, d_gates)


def loss_head(y, target, tm=512):
    T, D = y.shape

    def body(y_ref, t_ref, dy_ref, loss_ref):
        @pl.when(pl.program_id(0) == 0)
        def _():
            loss_ref[...] = jnp.zeros_like(loss_ref)

        err = y_ref[...] - t_ref[...]
        dy_ref[...] = err * (1.0 / D)
        loss_ref[...] += jnp.sum(err * err) * (0.5 / D)

    tile = pl.BlockSpec((tm, D), lambda i: (i, 0))
    return pl.pallas_call(
        body, name="loss_head", grid=(T // tm,),
        in_specs=[tile, tile],
        out_specs=[tile, pl.BlockSpec((8, 128), lambda i: (0, 0))],
        out_shape=[jax.ShapeDtypeStruct((T, D), F32), jax.ShapeDtypeStruct((8, 128), F32)],
        compiler_params=_cparams(("arbitrary",)),
    )(y, target)


def cast_layer(ld, items):
    def body(ld_ref, *refs):
        n = len(refs) // 2
        for src, dst in zip(refs[:n], refs[n:]):
            dst[...] = src[...].astype(BF16)

    def shard(w, k):
        return w.shape[1:] if k is None else w.shape[2:]

    def in_spec(w, k):
        sh = shard(w, k)
        if k is None:
            return pl.BlockSpec((None,) + sh, lambda i, ld, n=len(sh): (ld[0],) + (0,) * n)
        return pl.BlockSpec((None, None) + sh, lambda i, ld, n=len(sh), k=k: (ld[0], k) + (0,) * n)

    def out_spec(w, k):
        sh = shard(w, k)
        return pl.BlockSpec((None, None) + sh, lambda i, ld, n=len(sh): (0, ld[1]) + (0,) * n)

    grid_spec = pltpu.PrefetchScalarGridSpec(
        num_scalar_prefetch=1, grid=(1,),
        in_specs=[in_spec(w, k) for w, k in items], out_specs=[out_spec(w, k) for w, k in items])
    return pl.pallas_call(
        body, name="cast_layer", grid_spec=grid_spec,
        out_shape=[jax.ShapeDtypeStruct((1, N_DEV) + shard(w, k), BF16) for w, k in items],
        compiler_params=_cparams(("arbitrary",)),
    )(ld, *[w for w, _ in items])


def _silu(x):
    return x * _sigmoid(x)


def ada_fwd(c_all, w_ada, b_cols):
    L, D, n = w_ada.shape

    def body(c_ref, w_ref, b_ref, o_ref):
        c_act = _silu(c_ref[...]).astype(BF16)
        o_ref[...] = _dot(c_act, w_ref[...].astype(BF16)) + b_ref[...]

    return pl.pallas_call(
        body, name="ada_fwd", grid=(L,),
        in_specs=[pl.BlockSpec((N_DEV, D), lambda l: (0, 0)), pl.BlockSpec((None, D, n), lambda l: (l, 0, 0)),
                  pl.BlockSpec((None, 1, n), lambda l: (l, 0, 0))],
        out_specs=pl.BlockSpec((None, N_DEV, n), lambda l: (l, 0, 0)),
        out_shape=jax.ShapeDtypeStruct((L, N_DEV, n), F32),
        compiler_params=_cparams(("arbitrary",)),
    )(c_all, w_ada, b_cols)


def _adamw(w, g, m, v):
    m = ADAM_B1 * m + (1.0 - ADAM_B1) * g
    v = ADAM_B2 * v + (1.0 - ADAM_B2) * (g * g)
    m_hat = m / (1.0 - ADAM_B1 ** ADAM_STEP)
    v_hat = v / (1.0 - ADAM_B2 ** ADAM_STEP)
    delta = -ADAM_LR * (m_hat / (jnp.sqrt(v_hat) + ADAM_EPS) + ADAM_WD * w)
    return delta, m, v


def ada_update(c_all, dada_cols, w, m, v, rb=256):
    L, D, n = w.shape

    def body(c_ref, d_ref, w_ref, m_ref, v_ref, g_ref, dl_ref, nm_ref, nv_ref):
        c_act = _silu(c_ref[...]).astype(BF16)
        g = _dot_tn(c_act, d_ref[...].astype(BF16))
        g_ref[...] = g
        dl_ref[...], nm_ref[...], nv_ref[...] = _adamw(w_ref[...], g, m_ref[...], v_ref[...])

    blk = pl.BlockSpec((None, rb, n), lambda l, i: (l, i, 0))
    out = jax.ShapeDtypeStruct((L, D, n), F32)
    return pl.pallas_call(
        body, name="ada_update", grid=(L, D // rb),
        in_specs=[pl.BlockSpec((N_DEV, rb), lambda l, i: (0, i)),
                  pl.BlockSpec((None, N_DEV, n), lambda l, i: (l, 0, 0)), blk, blk, blk],
        out_specs=[blk, blk, blk, blk], out_shape=[out, out, out, out],
        compiler_params=_cparams(("arbitrary", "arbitrary")),
    )(c_all, dada_cols, w, m, v)


SUM_UPDATE_RECV_BYTES = 12 * 1024 * 1024


def sum_update(dev, first, recvs, owns, w, m, v, prev=None, after=None):
    n_slots, R, C = w.shape
    S = len(recvs)
    assert len(owns) == S and first + S <= n_slots
    rb_max = SUM_UPDATE_RECV_BYTES // (S * N_DEV * C * 2)
    rb = max(r for r in range(8, R + 1, 8) if R % r == 0 and (r <= rb_max or r == 8))
    last = R // rb - 1
    n_prev = 0 if prev is None else 4
    extra = list(prev or ()) + ([] if after is None else [after])

    def body(dev_ref, *refs):
        r_refs, o_refs = refs[:S], refs[S:2 * S]
        w_ref, m_ref, v_ref = refs[2 * S:2 * S + 3]
        g_ref, dl_ref, nm_ref, nv_ref = refs[2 * S + 3 + len(extra):]
        me = dev_ref[0]
        for s in range(S):
            @pl.when(pl.program_id(0) == s)
            def _(s=s):
                g = jnp.zeros((rb, C), F32)
                for d in range(N_DEV):
                    g += jnp.where(me == d, o_refs[s][...], r_refs[s][d]).astype(F32)
                g_ref[...] = g
                dl_ref[...], nm_ref[...], nv_ref[...] = _adamw(w_ref[...], g, m_ref[...], v_ref[...])

    def row(sl, i, s):
        return jnp.where(sl == s, i, jnp.where(sl < s, 0, last))

    def rspec(s):
        return pl.BlockSpec((N_DEV, rb, C), lambda sl, i, dev: (0, row(sl, i, s), 0))

    def ospec(s):
        return pl.BlockSpec((None, rb, C), lambda sl, i, dev: (dev[0], row(sl, i, s), 0))

    blk = pl.BlockSpec((None, rb, C), lambda sl, i, dev: (first + sl, i, 0))
    out = jax.ShapeDtypeStruct((n_slots, R, C), F32)
    grid_spec = pltpu.PrefetchScalarGridSpec(
        num_scalar_prefetch=1, grid=(S, R // rb),
        in_specs=[rspec(s) for s in range(S)] + [ospec(s) for s in range(S)] + [blk, blk, blk] + [ANY] * len(extra),
        out_specs=[blk, blk, blk, blk],
    )
    n_in = 1 + 2 * S + 3
    return pl.pallas_call(
        body, name="sum_update", grid_spec=grid_spec, out_shape=[out, out, out, out],
        input_output_aliases={n_in + i: i for i in range(n_prev)},
        compiler_params=_cparams(("arbitrary", "arbitrary")),
    )(dev, *recvs, *owns, w, m, v, *extra)


def small_sum(gathered):
    _, R, C = gathered.shape

    def body(g_ref, o_ref):
        acc = g_ref[0]
        for d in range(1, N_DEV):
            acc += g_ref[d]
        o_ref[...] = acc

    return pl.pallas_call(
        body, name="small_sum", grid=(1,),
        in_specs=[pl.BlockSpec((N_DEV, R, C), lambda i: (0, 0, 0))],
        out_specs=pl.BlockSpec((R, C), lambda i: (0, 0)),
        out_shape=jax.ShapeDtypeStruct((R, C), F32),
        compiler_params=_cparams(("arbitrary",)),
    )(gathered)


def small_update(w, g, m, v):
    def body(w_ref, g_ref, m_ref, v_ref, dl_ref, nm_ref, nv_ref):
        dl_ref[...], nm_ref[...], nv_ref[...] = _adamw(w_ref[...], g_ref[...], m_ref[...], v_ref[...])

    blk = pl.BlockSpec(w.shape, lambda i: (0, 0))
    out = jax.ShapeDtypeStruct(w.shape, F32)
    return pl.pallas_call(
        body, name="small_update", grid=(1,),
        in_specs=[blk] * 4, out_specs=[blk] * 3, out_shape=[out] * 3,
        compiler_params=_cparams(("arbitrary",)),
    )(w, g, m, v)


MESH = pl.DeviceIdType.MESH
ANY = pl.BlockSpec(memory_space=pl.ANY)


def _coords():
    return lax.axis_index("x"), lax.axis_index("y"), lax.axis_index("c")


def _dev_index(x, y, c):
    return 4 * x + 2 * y + c


def _at_dev(ref, p, dev):
    return ref.at[(slice(None),) * p + (dev,)]


def all_gather(arrays, ps):
    n = len(arrays)

    def body(*refs):
        ins, outs = refs[:n], refs[n:2 * n]
        send_sems, recv_sems, local_sems = refs[2 * n:]
        x, y, c = _coords()
        me, sibling = (x, y, c), (x, y, 1 - c)
        chips = [(1 - x, y), (x, 1 - y), (1 - x, 1 - y)]

        def copy(a, k, block, to, src=None):
            dst = _at_dev(outs[a], ps[a], _dev_index(*block))
            return pltpu.make_async_remote_copy(
                src_ref=dst if src is None else src, dst_ref=dst,
                send_sem=send_sems.at[a, k], recv_sem=recv_sems.at[a, k], device_id=to, device_id_type=MESH)

        mine = [pltpu.make_async_copy(ins[a], _at_dev(outs[a], ps[a], _dev_index(*me)), local_sems.at[a])
                for a in range(n)]
        for cp in mine:
            cp.start()
        first = []
        for a in range(n):
            first.append(copy(a, 0, me, sibling, src=ins[a]))
            first += [copy(a, 1 + j, me, (*chip, c), src=ins[a]) for j, chip in enumerate(chips)]
        for cp in first:
            cp.start()
        passed = []
        for j, chip in enumerate(chips):
            for a in range(n):
                copy(a, 1 + j, (*chip, c), me).wait_recv()
                fwd = copy(a, 4 + j, (*chip, c), sibling)
                fwd.start()
                passed.append(fwd)
        for a in range(n):
            copy(a, 0, sibling, me).wait_recv()
            for j, chip in enumerate(chips):
                copy(a, 4 + j, (*chip, 1 - c), me).wait_recv()
        for cp in first + passed:
            cp.wait_send()
        for cp in mine:
            cp.wait()

    out_shape = [jax.ShapeDtypeStruct(a.shape[:p] + (N_DEV,) + a.shape[p:], a.dtype) for a, p in zip(arrays, ps)]
    return pl.pallas_call(
        body, name="all_gather", in_specs=[ANY] * n, out_specs=[ANY] * n, out_shape=out_shape,
        scratch_shapes=[pltpu.SemaphoreType.DMA((n, 7)), pltpu.SemaphoreType.DMA((n, 7)),
                        pltpu.SemaphoreType.DMA((n,))],
        compiler_params=pltpu.CompilerParams(has_side_effects=True),
    )(*arrays)


HBM = pl.BlockSpec(memory_space=pltpu.HBM)
SEM = pl.BlockSpec(memory_space=pltpu.SEMAPHORE)
EFFECT = pltpu.SideEffectType.DATAFLOW_SIDE_EFFECTING


def _peers(x, y, c):
    out = []
    for k in range(1, N_DEV):
        out.append((1 - x if k & 4 else x, 1 - y if k & 2 else y, 1 - c if k & 1 else c))
    return out


def _exchange_plan(n):
    def plan(refs, x, y, c):
        blocks, lands = refs[:n], refs[n:2 * n]
        me = _dev_index(x, y, c)
        moves = []
        for peer in _peers(x, y, c):
            q = _dev_index(*peer)
            moves += [(blocks[a].at[q], lands[a].at[me], peer, lands[a].at[q]) for a in range(n)]
        return moves
    return plan


def _gather_plan(ps, second):
    def plan(refs, x, y, c):
        me, sibling = (x, y, c), (x, y, 1 - c)
        chips = [(1 - x, y), (x, 1 - y), (1 - x, 1 - y)]
        if second:
            trips = [((*ch, c), sibling, (*ch, 1 - c)) for ch in chips]
        else:
            trips = [(me, sibling, sibling)] + [(me, (*ch, c), (*ch, c)) for ch in chips]
        moves = []
        for sent, to, arriving in trips:
            for ref, p in zip(refs, ps):
                blk = _at_dev(ref, p, _dev_index(*sent))
                moves.append((blk, blk, to, _at_dev(ref, p, _dev_index(*arriving))))
        return moves
    return plan


def copies_start(name, plan, n_moves, arrays, carry):
    n = len(arrays)

    def body(*refs):
        sems = refs[n + 1:n + 1 + 2 * n_moves]
        moves = plan(refs[:n], *_coords())
        assert len(moves) == n_moves
        for i, (src, dst, to, _) in enumerate(moves):
            pltpu.make_async_remote_copy(src_ref=src, dst_ref=dst, send_sem=sems[i], recv_sem=sems[n_moves + i],
                                         device_id=to, device_id_type=MESH).start()

    operands = [pltpu.with_memory_space_constraint(a, pltpu.HBM) for a in list(arrays) + [carry]]
    outs = pl.pallas_call(
        body, name=name,
        out_shape=[pltpu.SemaphoreType.DMA(())] * (2 * n_moves) + [pltpu.HBM(a.shape, a.dtype) for a in operands],
        in_specs=[HBM] * (n + 1), out_specs=[SEM] * (2 * n_moves) + [HBM] * (n + 1),
        input_output_aliases={i: 2 * n_moves + i for i in range(n + 1)},
        compiler_params=pltpu.CompilerParams(has_side_effects=EFFECT),
    )(*operands)
    return outs[:n_moves], outs[n_moves:2 * n_moves], outs[2 * n_moves:-1], outs[-1]


def copies_wait(name, plan, send_sems, recv_sems, arrays, after):
    n, n_moves = len(arrays), len(send_sems)

    def body(*refs):
        sems = refs[n:n + 2 * n_moves]
        for i, (src, _, to, arriving) in enumerate(plan(refs[:n], *_coords())):
            cp = pltpu.make_async_remote_copy(src_ref=src, dst_ref=arriving, send_sem=sems[i],
                                              recv_sem=sems[n_moves + i], device_id=to, device_id_type=MESH)
            cp.wait_send()
            cp.wait_recv()

    return pl.pallas_call(
        body, name=name,
        out_shape=[pltpu.HBM(a.shape, a.dtype) for a in arrays],
        in_specs=[HBM] * n + [SEM] * (2 * n_moves) + [ANY], out_specs=[HBM] * n,
        input_output_aliases={i: i for i in range(n)},
        compiler_params=pltpu.CompilerParams(has_side_effects=EFFECT),
    )(*arrays, *send_sems, *recv_sems, after)


WEIGHT_NAMES = ("w_ada", "b_ada", "g_pre", "g_post", "w_ff_in", "w_ff_out", "w_in", "conv_w", "w_conv_out",
                "lam_re", "lam_im", "log_dt", "ssm_b_re", "ssm_b_im", "ssm_c_re", "ssm_c_im", "ssm_d", "w_glu",
                "w_pool", "pool_scale", "w_pool_out", "w_sb_out", "w_out")
BIG_NAMES = ("w_ff_in", "w_ff_out", "w_in", "w_conv_out", "w_glu", "w_pool_out", "w_sb_out", "w_out")
SMALL_NAMES = ("b_ada", "g_pre", "g_post", "conv_w", "lam_re", "lam_im", "log_dt", "ssm_b_re", "ssm_b_im",
               "ssm_c_re", "ssm_c_im", "ssm_d", "w_pool", "pool_scale")
PACK_LANES = 128
PACK_ROWS = 8


def _pack(arrays):
    flat = jnp.concatenate([a.reshape(-1) for a in arrays])
    unit = PACK_LANES * PACK_ROWS
    flat = jnp.pad(flat, (0, -flat.shape[0] % unit))
    return flat.reshape(-1, PACK_LANES)


def _unpack(pack, shapes):
    flat = pack.reshape(-1)
    out, off = [], 0
    for s in shapes:
        n = 1
        for d in s:
            n *= d
        out.append(flat[off:off + n].reshape(s))
        off += n
    return out


def _pad_rows(a, rows=8):
    return jnp.pad(a, ((0, 0), (0, rows - a.shape[1]), (0, 0)))


def _tile_b(b):
    L = b.shape[0]
    return jnp.tile(b.transpose(0, 3, 1, 2).reshape(L, SSM_GROUP, SSM_W), (1, SSM_GROUPS, 1))


def _tile_c(c):
    L = c.shape[0]
    return jnp.tile(c.transpose(0, 3, 1, 2).reshape(L, SSM_STATE, MIX_W), (1, SSM_GROUPS, 1))


def _step(x, c, target, W, M, V):
    T, D = x.shape[1], x.shape[2]
    L = W["w_ada"].shape[0]
    x = x[0]
    target = target[0]
    ax, ay, ac = _coords()
    dev = _dev_index(ax, ay, ac)
    n_ada = W["w_ada"].shape[2]

    dev_s = jnp.reshape(dev, (1,)).astype(jnp.int32)
    items = ([(W["w_ff_in"], 0), (W["w_ff_in"], 1), (W["w_ff_out"], 0), (W["w_ff_out"], 1)]
             + [(W[k], None) for k in BIG_NAMES[2:]])
    bufs = [list(cast_layer(jnp.concatenate([jnp.array([l], jnp.int32), dev_s]), items)) for l in range(L)]
    ffn1_w, mixer_w, ffn2_w = (0, 2), (4, 5, 6, 7, 8, 9), (1, 3)
    all_w = tuple(range(len(items)))

    def gather_start(tag, second, l, idx, carry):
        plan = _gather_plan((1,) * len(idx), second)
        n_moves = (3 if second else 4) * len(idx)
        s_sem, r_sem, arrs, carry = copies_start(f"gather_{'b' if second else 'a'}_start_{tag}", plan, n_moves,
                                                 [bufs[l][i] for i in idx], carry)
        for i, a in zip(idx, arrs):
            bufs[l][i] = a
        return (plan, s_sem, r_sem), carry

    def gather_wait(tag, second, l, idx, flight, after):
        arrs = copies_wait(f"gather_{'b' if second else 'a'}_wait_{tag}", *flight, [bufs[l][i] for i in idx], after)
        for i, a in zip(idx, arrs):
            bufs[l][i] = a

    def gather_finish(tag, l, idx, flight, after, carry):
        gather_wait(tag, False, l, idx, flight, after)
        flight, carry = gather_start(tag, True, l, idx, carry)
        gather_wait(tag, True, l, idx, flight, carry)
        return carry

    first = []
    for g, idx in enumerate((ffn1_w, mixer_w, ffn2_w)):
        flight, x = gather_start(f"0_{g}", False, 0, idx, x)
        first.append(flight)

    gathered = all_gather([W["g_pre"], W["g_post"], W["conv_w"], c], [0, 0, 0, 0])
    g_pre = gathered[0].transpose(1, 2, 0, 3).reshape(L, N_SUB, D)
    g_post = gathered[1].transpose(1, 2, 0, 3).reshape(L, N_SUB, D)
    conv_w = _pad_rows(gathered[2].transpose(1, 2, 0, 3).reshape(L, 3, MIX_W))
    c_all = gathered[3].reshape(N_DEV, D)

    b_cols = lax.dynamic_slice_in_dim(W["b_ada"], dev * n_ada, n_ada, axis=1)[:, None, :]
    ada_cols = ada_fwd(c_all, W["w_ada"], b_cols)
    ada_all = all_gather([ada_cols], [0])[0]
    ada = lax.dynamic_index_in_dim(ada_all, dev, axis=2, keepdims=False)
    ada = ada.transpose(1, 0, 2).reshape(L, N_SUB, 3, D)
    zeros = jnp.zeros((L, N_SUB, D), F32)
    pv_all = jnp.stack([g_pre, ada[:, :, 0], ada[:, :, 1], g_post, ada[:, :, 2], zeros, zeros, zeros], axis=2)

    lam = jnp.stack([W["lam_re"].reshape(L, SSM_W), W["lam_im"].reshape(L, SSM_W),
                     jnp.repeat(W["log_dt"], SSM_STATE, axis=1)], axis=1)
    lam = _pad_rows(lam)
    b_t = jnp.stack([_tile_b(W["ssm_b_re"]), _tile_b(W["ssm_b_im"])], axis=1)
    c_t = jnp.stack([_tile_c(W["ssm_c_re"]), _tile_c(W["ssm_c_im"])], axis=1)
    avec, b_bd, c_bd = s5_params(lam, b_t, c_t)
    ssm_d = _pad_rows(W["ssm_d"][:, None, :])
    pool_scale = _pad_rows(W["pool_scale"][:, None, :])
    eye4 = jnp.eye(len(POOL_WINDOWS), dtype=F32)
    w_bd = jnp.einsum("lgcd,gh->lgchd", W["w_pool"], eye4).reshape(L, MIX_W, MIX_W).astype(BF16)

    x = gather_finish("0_0", 0, ffn1_w, first[0], pv_all, x)

    def ffn_weights(l, k):
        b = bufs[l]
        return b[k].reshape(1, 1, 2, 4, D, FF_BLK), b[2 + k].reshape(1, 1, 4, FF_BLK, D)

    def mixer_weights(l):
        b = bufs[l]
        return b[4], b[5], b[6], b[7], b[8], b[9].reshape(1, D, D)

    l0 = jnp.array([0], jnp.int32)
    k0 = jnp.array([0, 0], jnp.int32)
    saved = []
    for l in range(L):
        li = jnp.array([l], jnp.int32)
        nxt = l + 1 < L
        if nxt:
            flight, x = gather_start(f"{l + 1}", False, l + 1, all_w, x)
        x0 = x
        ab0, f0, x1 = ffn_fwd(k0, x0, pv_all[l, 0], *ffn_weights(l, 0))
        if l == 0:
            x1 = gather_finish("0_1", 0, mixer_w, first[1], x1, x1)
        wg_in, wg_conv, wg_glu, wg_pool, wg_sb, wg_out = mixer_weights(l)
        p = mix_in_fwd(l0, x1, pv_all[l, 1], wg_in)
        za = conv_fwd(li, p, conv_w)
        z = pool_fwd(li, p, w_bd, pool_scale)
        s = s5_scan(li, avec, s5_bu(li, p, b_bd), False)
        yg = s5_out(li, p, s, c_bd, ssm_d)
        o, sb_tot = sb_fwd(p)
        x2, m = merge_fwd(l0, p, za, yg, z, o, x1, pv_all[l, 1], wg_conv, wg_glu, wg_pool, wg_sb, wg_out)
        if l == 0:
            x2 = gather_finish("0_2", 0, ffn2_w, first[2], x2, x2)
        if nxt:
            gather_wait(f"{l + 1}", False, l + 1, all_w, flight, x2)
            flight, x2 = gather_start(f"{l + 1}", True, l + 1, all_w, x2)
        ab1, f1, x = ffn_fwd(k0, x2, pv_all[l, 2], *ffn_weights(l, 1))
        if nxt:
            gather_wait(f"{l + 1}", True, l + 1, all_w, flight, x)
        saved.append((x0, ab0, f0, x1, p, za, z, s, yg, o, sb_tot, m, x2, ab1, f1))

    dx, loss_blk = loss_head(x, target)
    loss = lax.psum(loss_blk[0, 0], ("x", "y", "c"))

    n_blocks = 10
    late_idx, early_idx = (0, 2), (1, 3, 4, 5, 6, 7, 8, 9)
    recvs, owns, in_flight = [[None] * n_blocks for _ in range(L)], [[None] * n_blocks for _ in range(L)], []

    def exchange_start(tag, layer, idx, blocks, carry):
        n = len(idx)
        plan = _exchange_plan(n)
        arrays = list(blocks) + [lax.empty(a.shape, a.dtype) for a in blocks]
        s_sem, r_sem, arrays, carry = copies_start(f"exchange_start_{tag}", plan, (N_DEV - 1) * n, arrays, carry)
        in_flight.append((tag, layer, idx, plan, s_sem, r_sem, arrays))
        return carry

    def settle(after):
        while in_flight:
            tag, layer, idx, plan, s_sem, r_sem, arrays = in_flight.pop(0)
            arrays = copies_wait(f"exchange_wait_{tag}", plan, s_sem, r_sem, arrays, after)
            for j, i in enumerate(idx):
                owns[layer][i], recvs[layer][i] = arrays[j], arrays[len(idx) + j]

    pgs = [None] * L
    small = {k: [None] * L for k in ("conv_w", "w_bd", "pool_scale", "ssm_d", "gb", "gc", "da")}
    for l in reversed(range(L)):
        li = jnp.array([l], jnp.int32)
        x0, ab0, f0, x1, p, za, z, s, yg, o, sb_tot, m, x2, ab1, f1 = saved[l]
        wg_in, wg_conv, wg_glu, wg_pool, wg_sb, wg_out = mixer_weights(l)
        dab, h, df, dx, pg2 = ffn_bwd_act(k0, dx, x2, f1, pv_all[l, 2], ab1, *ffn_weights(l, 1))
        g_in1, g_out1 = ffn_bwd_w(h, df, ab1, dab)
        (dza, dyg, dz, do, dgates, pg1m, g_conv, g_glu, g_pool, g_sb, g_wo) = merge_bwd(
            l0, p, za, yg, z, o, m, dx, pv_all[l, 1], wg_conv, wg_glu, wg_pool, wg_sb, wg_out)
        d_conv, small["conv_w"][l] = conv_bwd(li, p, dza, conv_w)
        d_pool, small["w_bd"][l], small["pool_scale"][l] = pool_bwd(li, p, dz, w_bd, pool_scale)
        ds, du_skip, small["gc"][l], small["ssm_d"][l] = s5_bwd_y(li, p, s, dyg, c_bd, ssm_d)
        lam_s = s5_scan(li, avec, ds, True)
        d_ssm, small["gb"][l] = s5_bwd_u(li, p, lam_s, du_skip, b_bd)
        small["da"][l] = s5_bwd_a(s, lam_s)
        d_qkv = sb_bwd(p, do, sb_tot)
        dp = dp_assemble(d_conv, d_ssm, d_pool, d_qkv, dgates)
        dx, h, pg1i = mix_in_bwd_act(l0, dp, dx, x1, pv_all[l, 1], wg_in)
        g_win = matmul_tn(h, dp, IN_BLK)
        settle(dx)
        dx = exchange_start(f"{l}_early", l, early_idx,
                            [g_in1.reshape(N_DEV, D, FF_BLK), g_out1.reshape(N_DEV, D_FF // N_DEV, D), g_win, g_conv,
                             g_glu, g_pool, g_sb, g_wo.reshape(N_DEV, D // N_DEV, D)], dx)
        dab, h, df, dx, pg0 = ffn_bwd_act(k0, dx, x0, f0, pv_all[l, 0], ab0, *ffn_weights(l, 0))
        g_in0, g_out0 = ffn_bwd_w(h, df, ab0, dab)
        pgs[l] = jnp.stack([pg0, pg1m + pg1i, pg2])
        dx = exchange_start(f"{l}_late", l, late_idx,
                            [g_in0.reshape(N_DEV, D, FF_BLK), g_out0.reshape(N_DEV, D_FF // N_DEV, D)], dx)

    dlam, db_t, dc_t, dldt = s5_params_bwd(lam, b_t, jnp.stack(small["gb"]), jnp.stack(small["gc"]),
                                           jnp.stack(small["da"]))
    pg = jnp.stack(pgs)
    d_ada = jnp.stack([pg[:, :, PV_SHIFT], pg[:, :, PV_SCALE], pg[:, :, PV_GATE]], axis=2).reshape(L, N_SUB * 3 * D)
    db = db_t.reshape(L, 2, SSM_GROUPS, SSM_GROUP, SSM_GROUPS, SSM_STATE)
    db = jnp.einsum("lrghgp->lrgph", db)
    dc = dc_t.reshape(L, 2, SSM_GROUPS, SSM_STATE, SSM_GROUPS, SSM_GROUP)
    dc = jnp.einsum("lrgpgh->lrghp", dc)
    d_wpool = jnp.einsum("lgcgd->lgcd", jnp.stack(small["w_bd"]).reshape(L, 4, 64, 4, 64))
    contrib = {
        "b_ada": d_ada, "g_pre": pg[:, :, PV_GPRE], "g_post": pg[:, :, PV_GPOST],
        "conv_w": jnp.stack(small["conv_w"])[:, :3], "lam_re": dlam[:, 0].reshape(L, SSM_GROUPS, SSM_STATE),
        "lam_im": dlam[:, 1].reshape(L, SSM_GROUPS, SSM_STATE), "log_dt": dldt[:, 2, :SSM_GROUPS],
        "ssm_b_re": db[:, 0], "ssm_b_im": db[:, 1], "ssm_c_re": dc[:, 0], "ssm_c_im": dc[:, 1],
        "ssm_d": jnp.stack(small["ssm_d"])[:, 0], "w_pool": d_wpool,
        "pool_scale": jnp.stack(small["pool_scale"])[:, 0],
    }
    contrib_shapes = [contrib[k].shape for k in SMALL_NAMES]
    pack_all = all_gather([_pack([contrib[k] for k in SMALL_NAMES])], [0])[0]
    total = dict(zip(SMALL_NAMES, _unpack(small_sum(pack_all), contrib_shapes)))
    d_ada_all = pack_all.reshape(N_DEV, -1)[:, :L * N_SUB * 3 * D].reshape(N_DEV, L, N_SUB * 3 * D)
    dada_cols = lax.dynamic_slice_in_dim(d_ada_all, dev * n_ada, n_ada, axis=2).transpose(1, 0, 2)
    n_g = D // N_DEV
    grads = {}
    for k in SMALL_NAMES:
        g = total[k]
        if k in ("g_pre", "g_post"):
            g = lax.dynamic_slice_in_dim(g, dev * n_g, n_g, axis=2)
        elif k == "conv_w":
            g = lax.dynamic_slice_in_dim(g, dev * (MIX_W // N_DEV), MIX_W // N_DEV, axis=2)
        grads[k] = g

    delta, new_m, new_v = {}, {}, {}
    shapes = [W[k].shape for k in SMALL_NAMES]
    dl, nm, nv = small_update(_pack([W[k] for k in SMALL_NAMES]), _pack([grads[k] for k in SMALL_NAMES]),
                              _pack([M[k] for k in SMALL_NAMES]), _pack([V[k] for k in SMALL_NAMES]))
    for k, a, b, cc in zip(SMALL_NAMES, _unpack(dl, shapes), _unpack(nm, shapes), _unpack(nv, shapes)):
        delta[k], new_m[k], new_v[k] = a, b, cc
    grads["w_ada"], delta["w_ada"], new_m["w_ada"], new_v["w_ada"] = ada_update(
        c_all, dada_cols, W["w_ada"], M["w_ada"], V["w_ada"])

    big_idx = {"w_ff_in": (0, 1), "w_ff_out": (2, 3), "w_in": (4,), "w_conv_out": (5,), "w_glu": (6,),
               "w_pool_out": (7,), "w_sb_out": (8,), "w_out": (9,)}

    def big(name, layers, prev, after=None):
        idx = big_idx[name]
        flat = (-1,) + W[name].shape[-2:]
        return sum_update(dev_s, layers[0] * len(idx), [recvs[l][i] for l in layers for i in idx],
                          [owns[l][i] for l in layers for i in idx],
                          W[name].reshape(flat), M[name].reshape(flat), V[name].reshape(flat), prev, after)

    partial, after = {}, dx
    if L > 1:
        for name in BIG_NAMES:
            partial[name] = big(name, list(range(1, L)), None, after)
            after = partial[name][0]
    settle(after)
    for name in BIG_NAMES:
        outs = big(name, [0], partial.get(name))
        grads[name], delta[name], new_m[name], new_v[name] = [o.reshape(W[name].shape) for o in outs]

    return (loss, dx[None], *[grads[k] for k in WEIGHT_NAMES], *[delta[k] for k in WEIGHT_NAMES],
            *[new_m[k] for k in WEIGHT_NAMES], *[new_v[k] for k in WEIGHT_NAMES])


def kernel(x, c, w_ada, b_ada, g_pre, g_post, w_ff_in, w_ff_out, w_in, conv_w, w_conv_out, lam_re, lam_im, log_dt, ssm_b_re, ssm_b_im, ssm_c_re, ssm_c_im, ssm_d, w_glu, w_pool, pool_scale, w_pool_out, w_sb_out, w_out, loss_target, m_w_ada, m_b_ada, m_g_pre, m_g_post, m_w_ff_in, m_w_ff_out, m_w_in, m_conv_w, m_w_conv_out, m_lam_re, m_lam_im, m_log_dt, m_ssm_b_re, m_ssm_b_im, m_ssm_c_re, m_ssm_c_im, m_ssm_d, m_w_glu, m_w_pool, m_pool_scale, m_w_pool_out, m_w_sb_out, m_w_out, v_w_ada, v_b_ada, v_g_pre, v_g_post, v_w_ff_in, v_w_ff_out, v_w_in, v_conv_w, v_w_conv_out, v_lam_re, v_lam_im, v_log_dt, v_ssm_b_re, v_ssm_b_im, v_ssm_c_re, v_ssm_c_im, v_ssm_d, v_w_glu, v_w_pool, v_pool_scale, v_w_pool_out, v_w_sb_out, v_w_out):
    w = (w_ada, b_ada, g_pre, g_post, w_ff_in, w_ff_out, w_in, conv_w, w_conv_out, lam_re, lam_im, log_dt, ssm_b_re, ssm_b_im, ssm_c_re, ssm_c_im, ssm_d, w_glu, w_pool, pool_scale, w_pool_out, w_sb_out, w_out)
    m = (m_w_ada, m_b_ada, m_g_pre, m_g_post, m_w_ff_in, m_w_ff_out, m_w_in, m_conv_w, m_w_conv_out, m_lam_re, m_lam_im, m_log_dt, m_ssm_b_re, m_ssm_b_im, m_ssm_c_re, m_ssm_c_im, m_ssm_d, m_w_glu, m_w_pool, m_pool_scale, m_w_pool_out, m_w_sb_out, m_w_out)
    v = (v_w_ada, v_b_ada, v_g_pre, v_g_post, v_w_ff_in, v_w_ff_out, v_w_in, v_conv_w, v_w_conv_out, v_lam_re, v_lam_im, v_log_dt, v_ssm_b_re, v_ssm_b_im, v_ssm_c_re, v_ssm_c_im, v_ssm_d, v_w_glu, v_w_pool, v_pool_scale, v_w_pool_out, v_w_sb_out, v_w_out)
    return _step(x, c, loss_target, dict(zip(WEIGHT_NAMES, w)), dict(zip(WEIGHT_NAMES, m)), dict(zip(WEIGHT_NAMES, v)))
```

```python
import functools

import jax
import jax.numpy as jnp
from jax import lax
from jax.experimental import pallas as pl
from jax.experimental.pallas import tpu as pltpu

F32 = jnp.float32
BF16 = jnp.bfloat16

N_DEV = 8
D_MODEL = 1024
D_FF = 2816
FF_BLK = D_FF // 4
N_SUB = 3
MIX_W = 256
IN_COLS = 6144
IN_BLK = IN_COLS // N_DEV
GATE_OFF = 2048
SSM_GROUPS, SSM_GROUP, SSM_STATE = 16, 16, 64
SSM_W = SSM_GROUPS * SSM_STATE
POOL_WINDOWS = (2, 4, 8, 16)
SB_HEAD = 64
EPS = 1e-6
DT_LAMBDA_RE_MAX = -1e-4
ADAM_LR, ADAM_B1, ADAM_B2, ADAM_EPS, ADAM_WD, ADAM_STEP = 0.001, 0.9, 0.999, 1e-08, 0.01, 10

VMEM_LIMIT = 56 * 1024 * 1024

PV_GPRE, PV_SHIFT, PV_SCALE, PV_GPOST, PV_GATE = 0, 1, 2, 3, 4


def _cparams(sem):
    return pltpu.CompilerParams(dimension_semantics=sem, vmem_limit_bytes=VMEM_LIMIT)


def _dot(a, b):
    return jnp.dot(a, b, preferred_element_type=F32)


def _dot_nt(a, b):
    return lax.dot_general(a, b, (((1,), (1,)), ((), ())), preferred_element_type=F32)


def _dot_tn(a, b):
    return lax.dot_general(a, b, (((0,), (0,)), ((), ())), preferred_element_type=F32)


def _rms(x):
    r = lax.rsqrt(jnp.mean(x * x, axis=-1, keepdims=True) + EPS)
    return x * r, r


def _rms_bwd(dn, n, r):
    return r * (dn - n * jnp.mean(dn * n, axis=-1, keepdims=True))


def _sigmoid(x):
    return 1.0 / (1.0 + jnp.exp(-x))


def _colsum(x):
    return jnp.sum(x, axis=0, keepdims=True)


def _prenorm(x, pv_ref):
    n, r = _rms(x)
    hn = n * pv_ref[PV_GPRE:PV_GPRE + 1, :]
    h = hn * (1.0 + pv_ref[PV_SCALE:PV_SCALE + 1, :]) + pv_ref[PV_SHIFT:PV_SHIFT + 1, :]
    return h, n, r, hn


def _prenorm_bwd(dh, dxn, x, pv_ref, pg_ref):
    _, n, r, hn = _prenorm(x, pv_ref)
    pg_ref[PV_SHIFT:PV_SHIFT + 1, :] += _colsum(dh)
    pg_ref[PV_SCALE:PV_SCALE + 1, :] += _colsum(dh * hn)
    dhn = dh * (1.0 + pv_ref[PV_SCALE:PV_SCALE + 1, :])
    pg_ref[PV_GPRE:PV_GPRE + 1, :] += _colsum(dhn * n)
    dn = dhn * pv_ref[PV_GPRE:PV_GPRE + 1, :]
    return dxn + _rms_bwd(dn, n, r)


def _postnorm_res(x, f, pv_ref, coef):
    nf, _ = _rms(f)
    return x + (coef * (1.0 + pv_ref[PV_GATE:PV_GATE + 1, :])) * (nf * pv_ref[PV_GPOST:PV_GPOST + 1, :])


def _postnorm_bwd(dxn, f, pv_ref, pg_ref, coef):
    nf, rf = _rms(f)
    g_post = pv_ref[PV_GPOST:PV_GPOST + 1, :]
    pg_ref[PV_GATE:PV_GATE + 1, :] += _colsum(dxn * (nf * g_post)) * coef
    dnfg = dxn * (coef * (1.0 + pv_ref[PV_GATE:PV_GATE + 1, :]))
    pg_ref[PV_GPOST:PV_GPOST + 1, :] += _colsum(dnfg * nf)
    return _rms_bwd(dnfg * g_post, nf, rf)


def ffn_fwd(lk, x, pv, wg_in, wg_out, tm=1024):
    T, D = x.shape
    tm = min(tm, T)
    nj = 4

    def body(lk_ref, x_ref, pv_ref, win_ref, wout_ref, ab_ref, f_ref, xn_ref, h_sc, acc):
        j = pl.program_id(1)

        @pl.when(j == 0)
        def _():
            h, _, _, _ = _prenorm(x_ref[...], pv_ref)
            h_sc[...] = h.astype(BF16)
            acc[...] = jnp.zeros_like(acc)

        h = h_sc[...]
        a = _dot(h, win_ref[0])
        b = _dot(h, win_ref[1])
        ab_ref[0] = a.astype(BF16)
        ab_ref[1] = b.astype(BF16)
        act = (a * _sigmoid(a) * b).astype(BF16)
        acc[...] += _dot(act, wout_ref[...])

        @pl.when(j == nj - 1)
        def _():
            f = acc[...]
            f_ref[...] = f
            xn_ref[...] = _postnorm_res(x_ref[...], f, pv_ref, 0.5)

    grid_spec = pltpu.PrefetchScalarGridSpec(
        num_scalar_prefetch=1, grid=(T // tm, nj),
        in_specs=[
            pl.BlockSpec((tm, D), lambda i, j, lk: (i, 0)),
            pl.BlockSpec((8, D), lambda i, j, lk: (0, 0)),
            pl.BlockSpec((None, None, 2, None, D, FF_BLK), lambda i, j, lk: (lk[0], lk[1], 0, j, 0, 0)),
            pl.BlockSpec((None, None, None, FF_BLK, D), lambda i, j, lk: (lk[0], lk[1], j, 0, 0)),
        ],
        out_specs=[
            pl.BlockSpec((2, None, tm, FF_BLK), lambda i, j, lk: (0, j, i, 0)),
            pl.BlockSpec((tm, D), lambda i, j, lk: (i, 0)),
            pl.BlockSpec((tm, D), lambda i, j, lk: (i, 0)),
        ],
        scratch_shapes=[pltpu.VMEM((tm, D), BF16), pltpu.VMEM((tm, D), F32)],
    )
    return pl.pallas_call(
        body, name="ffn_fwd", grid_spec=grid_spec,
        out_shape=[jax.ShapeDtypeStruct((2, nj, T, FF_BLK), BF16),
                   jax.ShapeDtypeStruct((T, D), F32), jax.ShapeDtypeStruct((T, D), F32)],
        compiler_params=_cparams(("arbitrary", "arbitrary")),
    )(lk, x, pv, wg_in, wg_out)


def ffn_bwd_hidden(lk, dxn, x, f, pv, ab, wg_out, tm=512):
    T, D = x.shape
    tm = min(tm, T)
    nj = 4

    def body(lk_ref, dxn_ref, x_ref, f_ref, pv_ref, ab_ref, wout_ref, dab_ref, h_ref, df_ref, pg_ref):
        i, j = pl.program_id(0), pl.program_id(1)

        @pl.when((i == 0) & (j == 0))
        def _():
            pg_ref[...] = jnp.zeros_like(pg_ref)

        @pl.when(j == 0)
        def _():
            df = _postnorm_bwd(dxn_ref[...], f_ref[...], pv_ref, pg_ref, 0.5)
            df_ref[...] = df.astype(BF16)
            h, _, _, _ = _prenorm(x_ref[...], pv_ref)
            h_ref[...] = h.astype(BF16)

        dact = _dot_nt(df_ref[...], wout_ref[...])
        a = ab_ref[0].astype(F32)
        b = ab_ref[1].astype(F32)
        sig = _sigmoid(a)
        s = a * sig
        dab_ref[0] = (dact * b * (sig * (1.0 + a * (1.0 - sig)))).astype(BF16)
        dab_ref[1] = (dact * s).astype(BF16)

    tile = pl.BlockSpec((tm, D), lambda i, j, lk: (i, 0))
    blk = pl.BlockSpec((2, None, tm, FF_BLK), lambda i, j, lk: (0, j, i, 0))
    grid_spec = pltpu.PrefetchScalarGridSpec(
        num_scalar_prefetch=1, grid=(T // tm, nj),
        in_specs=[tile, tile, tile, pl.BlockSpec((8, D), lambda i, j, lk: (0, 0)), blk,
                  pl.BlockSpec((None, None, None, FF_BLK, D), lambda i, j, lk: (lk[0], lk[1], j, 0, 0))],
        out_specs=[blk, tile, tile, pl.BlockSpec((8, D), lambda i, j, lk: (0, 0))],
    )
    return pl.pallas_call(
        body, name="ffn_bwd_hidden", grid_spec=grid_spec,
        out_shape=[jax.ShapeDtypeStruct((2, nj, T, FF_BLK), BF16), jax.ShapeDtypeStruct((T, D), BF16),
                   jax.ShapeDtypeStruct((T, D), BF16), jax.ShapeDtypeStruct((8, D), F32)],
        compiler_params=_cparams(("arbitrary", "arbitrary")),
    )(lk, dxn, x, f, pv, ab, wg_out)


def ffn_bwd_in(lk, dab, dxn, x, pv, wg_in, tm=1024):
    T, D = x.shape
    tm = min(tm, T)
    nj = 4

    def body(lk_ref, dab_ref, dxn_ref, x_ref, pv_ref, win_ref, dx_ref, pg_ref, dacc):
        i, j = pl.program_id(0), pl.program_id(1)

        @pl.when((i == 0) & (j == 0))
        def _():
            pg_ref[...] = jnp.zeros_like(pg_ref)

        @pl.when(j == 0)
        def _():
            dacc[...] = jnp.zeros_like(dacc)

        dacc[...] += _dot_nt(dab_ref[0], win_ref[0]) + _dot_nt(dab_ref[1], win_ref[1])

        @pl.when(j == nj - 1)
        def _():
            dx_ref[...] = _prenorm_bwd(dacc[...], dxn_ref[...], x_ref[...], pv_ref, pg_ref)

    tile = pl.BlockSpec((tm, D), lambda i, j, lk: (i, 0))
    grid_spec = pltpu.PrefetchScalarGridSpec(
        num_scalar_prefetch=1, grid=(T // tm, nj),
        in_specs=[pl.BlockSpec((2, None, tm, FF_BLK), lambda i, j, lk: (0, j, i, 0)), tile, tile,
                  pl.BlockSpec((8, D), lambda i, j, lk: (0, 0)),
                  pl.BlockSpec((None, None, 2, None, D, FF_BLK), lambda i, j, lk: (lk[0], lk[1], 0, j, 0, 0))],
        out_specs=[tile, pl.BlockSpec((8, D), lambda i, j, lk: (0, 0))],
        scratch_shapes=[pltpu.VMEM((tm, D), F32)],
    )
    return pl.pallas_call(
        body, name="ffn_bwd_in", grid_spec=grid_spec,
        out_shape=[jax.ShapeDtypeStruct((T, D), F32), jax.ShapeDtypeStruct((8, D), F32)],
        compiler_params=_cparams(("arbitrary", "arbitrary")),
    )(lk, dab, dxn, x, pv, wg_in)


def ffn_bwd_act(lk, dxn, x, f, pv, ab, wg_in, wg_out):
    dab, h, df, pg_post = ffn_bwd_hidden(lk, dxn, x, f, pv, ab, wg_out)
    dx, pg_pre = ffn_bwd_in(lk, dab, dxn, x, pv, wg_in)
    return dab, h, df, dx, pg_post + pg_pre


def ffn_bwd_w(h, df, ab, dab, tm=1024):
    T, D = h.shape
    tm = min(tm, T)
    nj, ni = 4, T // tm

    def body(h_ref, df_ref, ab_ref, dab_ref, gin_ref, gout_ref, acc_in, acc_out):
        i = pl.program_id(1)

        @pl.when(i == 0)
        def _():
            acc_in[...] = jnp.zeros_like(acc_in)
            acc_out[...] = jnp.zeros_like(acc_out)

        h = h_ref[...]
        acc_in[0] += _dot_tn(h, dab_ref[0])
        acc_in[1] += _dot_tn(h, dab_ref[1])
        a = ab_ref[0].astype(F32)
        b = ab_ref[1].astype(F32)
        act = (a * _sigmoid(a) * b).astype(BF16)
        acc_out[...] += _dot_tn(act, df_ref[...])

        @pl.when(i == ni - 1)
        def _():
            gin_ref[...] = acc_in[...].astype(BF16)
            gout_ref[...] = acc_out[...].astype(BF16)

    tile = pl.BlockSpec((tm, D), lambda j, i: (i, 0))
    blk = pl.BlockSpec((2, None, tm, FF_BLK), lambda j, i: (0, j, i, 0))
    return pl.pallas_call(
        body, name="ffn_bwd_w", grid=(nj, ni),
        in_specs=[tile, tile, blk, blk],
        out_specs=[pl.BlockSpec((2, None, D, FF_BLK), lambda j, i: (0, j, 0, 0)),
                   pl.BlockSpec((None, FF_BLK, D), lambda j, i: (j, 0, 0))],
        out_shape=[jax.ShapeDtypeStruct((2, nj, D, FF_BLK), BF16),
                   jax.ShapeDtypeStruct((nj, FF_BLK, D), BF16)],
        scratch_shapes=[pltpu.VMEM((2, D, FF_BLK), F32), pltpu.VMEM((FF_BLK, D), F32)],
        compiler_params=_cparams(("arbitrary", "arbitrary")),
    )(h, df, ab, dab)


def mix_in_fwd(l, x, pv, wg, tm=1024):
    T, D = x.shape
    tm = min(tm, T)

    def body(l_ref, x_ref, pv_ref, w_ref, p_ref, h_sc):
        @pl.when(pl.program_id(1) == 0)
        def _():
            h, _, _, _ = _prenorm(x_ref[...], pv_ref)
            h_sc[...] = h.astype(BF16)

        p_ref[...] = _dot(h_sc[...], w_ref[...])

    grid_spec = pltpu.PrefetchScalarGridSpec(
        num_scalar_prefetch=1, grid=(T // tm, N_DEV),
        in_specs=[pl.BlockSpec((tm, D), lambda i, j, l: (i, 0)),
                  pl.BlockSpec((8, D), lambda i, j, l: (0, 0)),
                  pl.BlockSpec((None, None, D, IN_BLK), lambda i, j, l: (l[0], j, 0, 0))],
        out_specs=pl.BlockSpec((tm, IN_BLK), lambda i, j, l: (i, j)),
        scratch_shapes=[pltpu.VMEM((tm, D), BF16)],
    )
    return pl.pallas_call(
        body, name="mix_in_fwd", grid_spec=grid_spec,
        out_shape=jax.ShapeDtypeStruct((T, IN_COLS), F32),
        compiler_params=_cparams(("arbitrary", "arbitrary")),
    )(l, x, pv, wg)


def mix_in_bwd_act(l, dp, dxn, x, pv, wg, tm=1024):
    T, D = x.shape
    tm = min(tm, T)

    def body(l_ref, dp_ref, dxn_ref, x_ref, pv_ref, w_ref, dx_ref, h_ref, pg_ref, dacc):
        i, j = pl.program_id(0), pl.program_id(1)

        @pl.when((i == 0) & (j == 0))
        def _():
            pg_ref[...] = jnp.zeros_like(pg_ref)

        @pl.when(j == 0)
        def _():
            dacc[...] = jnp.zeros_like(dacc)

        dacc[...] += _dot_nt(dp_ref[...], w_ref[...])

        @pl.when(j == N_DEV - 1)
        def _():
            h, _, _, _ = _prenorm(x_ref[...], pv_ref)
            h_ref[...] = h.astype(BF16)
            dx_ref[...] = _prenorm_bwd(dacc[...], dxn_ref[...], x_ref[...], pv_ref, pg_ref)

    tile = pl.BlockSpec((tm, D), lambda i, j, l: (i, 0))
    grid_spec = pltpu.PrefetchScalarGridSpec(
        num_scalar_prefetch=1, grid=(T // tm, N_DEV),
        in_specs=[pl.BlockSpec((tm, IN_BLK), lambda i, j, l: (i, j)), tile, tile,
                  pl.BlockSpec((8, D), lambda i, j, l: (0, 0)),
                  pl.BlockSpec((None, None, D, IN_BLK), lambda i, j, l: (l[0], j, 0, 0))],
        out_specs=[tile, tile, pl.BlockSpec((8, D), lambda i, j, l: (0, 0))],
        scratch_shapes=[pltpu.VMEM((tm, D), F32)],
    )
    return pl.pallas_call(
        body, name="mix_in_bwd_act", grid_spec=grid_spec,
        out_shape=[jax.ShapeDtypeStruct((T, D), F32), jax.ShapeDtypeStruct((T, D), BF16),
                   jax.ShapeDtypeStruct((8, D), F32)],
        compiler_params=_cparams(("arbitrary", "arbitrary")),
    )(l, dp, dxn, x, pv, wg)


def matmul_tn(a, b, tn, tm=1024):
    T, M = a.shape
    tm = min(tm, T)
    N = b.shape[1]
    ni = T // tm

    def body(a_ref, b_ref, o_ref, acc):
        i = pl.program_id(1)

        @pl.when(i == 0)
        def _():
            acc[...] = jnp.zeros_like(acc)

        acc[...] += _dot_tn(a_ref[...], b_ref[...])

        @pl.when(i == ni - 1)
        def _():
            o_ref[...] = acc[...].astype(o_ref.dtype)

    return pl.pallas_call(
        body, name="matmul_tn", grid=(N // tn, ni),
        in_specs=[pl.BlockSpec((tm, M), lambda j, i: (i, 0)), pl.BlockSpec((tm, tn), lambda j, i: (i, j))],
        out_specs=pl.BlockSpec((None, M, tn), lambda j, i: (j, 0, 0)),
        out_shape=jax.ShapeDtypeStruct((N // tn, M, tn), BF16),
        scratch_shapes=[pltpu.VMEM((M, tn), F32)],
        compiler_params=_cparams(("arbitrary", "arbitrary")),
    )(a, b)


SEQ_CHUNK = 256
HALO = 16


def _shift_down(ext, d):
    return pltpu.roll(ext, d, 0)


def _shift_up(ext, d):
    return pltpu.roll(ext, ext.shape[0] - d, 0)


def _rows_with_lead(load, c, width):
    t0 = c * SEQ_CHUNK
    if c == 0:
        return jnp.concatenate([jnp.zeros((HALO, width), F32), load(0, SEQ_CHUNK)], axis=0)
    return load(t0 - HALO, SEQ_CHUNK + HALO)


def _rows_with_tail(load, c, n_chunks, width):
    t0 = c * SEQ_CHUNK
    if c == n_chunks - 1:
        return jnp.concatenate([load(t0, SEQ_CHUNK), jnp.zeros((HALO, width), F32)], axis=0)
    return load(t0, SEQ_CHUNK + HALO)


def conv_fwd(l, p, conv_w):
    T = p.shape[0]
    W = MIX_W
    nC = T // SEQ_CHUNK

    def body(l_ref, p_ref, w_ref, za_ref):
        w0, w1, w2 = w_ref[0:1, :], w_ref[1:2, :], w_ref[2:3, :]
        for c in range(nC):
            ext = _rows_with_lead(lambda s, n: p_ref[s:s + n, W:2 * W] * p_ref[s:s + n, 2 * W:3 * W], c, W)
            y = w2 * ext + w1 * _shift_down(ext, 1) + w0 * _shift_down(ext, 2)
            t0 = c * SEQ_CHUNK
            za_ref[t0:t0 + SEQ_CHUNK, :] = (p_ref[t0:t0 + SEQ_CHUNK, 0:W] * y[HALO:]).astype(BF16)

    grid_spec = pltpu.PrefetchScalarGridSpec(
        num_scalar_prefetch=1, grid=(1,),
        in_specs=[pl.BlockSpec((T, 3 * W), lambda i, l: (0, 0)),
                  pl.BlockSpec((None, 8, W), lambda i, l: (l[0], 0, 0))],
        out_specs=pl.BlockSpec((T, W), lambda i, l: (0, 0)),
    )
    return pl.pallas_call(
        body, name="conv_fwd", grid_spec=grid_spec,
        out_shape=jax.ShapeDtypeStruct((T, W), BF16),
        compiler_params=_cparams(("arbitrary",)),
    )(l, p, conv_w)


def conv_bwd(l, p, dza, conv_w):
    T = p.shape[0]
    W = MIX_W
    nC = T // SEQ_CHUNK

    def body(l_ref, p_ref, dza_ref, w_ref, dp_ref, dw_ref):
        w0, w1, w2 = w_ref[0:1, :], w_ref[1:2, :], w_ref[2:3, :]
        dw = [jnp.zeros((1, W), F32) for _ in range(3)]
        for c in range(nC):
            t0 = c * SEQ_CHUNK
            ext = _rows_with_lead(lambda s, n: p_ref[s:s + n, W:2 * W] * p_ref[s:s + n, 2 * W:3 * W], c, W)
            u1, u2 = _shift_down(ext, 1)[HALO:], _shift_down(ext, 2)[HALO:]
            u0 = ext[HALO:]
            y = w2 * u0 + w1 * u1 + w0 * u2
            dza_c = dza_ref[t0:t0 + SEQ_CHUNK, :]
            dy = dza_c * p_ref[t0:t0 + SEQ_CHUNK, 0:W]
            dw[0] += _colsum(dy * u2)
            dw[1] += _colsum(dy * u1)
            dw[2] += _colsum(dy * u0)
            dye = _rows_with_tail(lambda s, n: dza_ref[s:s + n, :] * p_ref[s:s + n, 0:W], c, nC, W)
            du = (w2 * dye + w1 * _shift_up(dye, 1) + w0 * _shift_up(dye, 2))[:SEQ_CHUNK]
            dp_ref[t0:t0 + SEQ_CHUNK, 0:W] = (dza_c * y).astype(BF16)
            dp_ref[t0:t0 + SEQ_CHUNK, W:2 * W] = (du * p_ref[t0:t0 + SEQ_CHUNK, 2 * W:3 * W]).astype(BF16)
            dp_ref[t0:t0 + SEQ_CHUNK, 2 * W:3 * W] = (du * p_ref[t0:t0 + SEQ_CHUNK, W:2 * W]).astype(BF16)
        dw_ref[...] = jnp.concatenate(dw + [jnp.zeros((5, W), F32)], axis=0)

    grid_spec = pltpu.PrefetchScalarGridSpec(
        num_scalar_prefetch=1, grid=(1,),
        in_specs=[pl.BlockSpec((T, 3 * W), lambda i, l: (0, 0)),
                  pl.BlockSpec((T, W), lambda i, l: (0, 0)),
                  pl.BlockSpec((None, 8, W), lambda i, l: (l[0], 0, 0))],
        out_specs=[pl.BlockSpec((T, 3 * W), lambda i, l: (0, 0)), pl.BlockSpec((8, W), lambda i, l: (0, 0))],
    )
    return pl.pallas_call(
        body, name="conv_bwd", grid_spec=grid_spec,
        out_shape=[jax.ShapeDtypeStruct((T, 3 * W), BF16), jax.ShapeDtypeStruct((8, W), F32)],
        compiler_params=_cparams(("arbitrary",)),
    )(l, p, dza, conv_w)


def _pool_consts(rows, t0):
    lane = lax.broadcasted_iota(jnp.int32, (rows, MIX_W), 1)
    t = lax.broadcasted_iota(jnp.int32, (rows, MIX_W), 0) + t0
    win = jnp.where(lane < 64, 2, jnp.where(lane < 128, 4, jnp.where(lane < 192, 8, 16)))
    inv = 1.0 / jnp.minimum(t + 1, win).astype(F32)
    return lane, inv


def _pick_window(lane, s2, s4, s8, s16):
    return jnp.where(lane < 64, s2, jnp.where(lane < 128, s4, jnp.where(lane < 192, s8, s16)))


def _pooled_chunk(u_ref, c):
    ext = _rows_with_lead(lambda s, n: u_ref[s:s + n, :], c, MIX_W)
    s2 = ext + _shift_down(ext, 1)
    s4 = s2 + _shift_down(s2, 2)
    s8 = s4 + _shift_down(s4, 4)
    s16 = s8 + _shift_down(s8, 8)
    lane, inv = _pool_consts(SEQ_CHUNK, c * SEQ_CHUNK)
    return _pick_window(lane, s2[HALO:], s4[HALO:], s8[HALO:], s16[HALO:]) * inv - ext[HALO:]


def pool_fwd(l, p, w_bd, scale):
    T = p.shape[0]
    W = MIX_W
    nC = T // SEQ_CHUNK

    def body(l_ref, u_ref, w_ref, sc_ref, z_ref):
        for c in range(nC):
            pooled = _pooled_chunk(u_ref, c)
            mixed = _dot(pooled.astype(BF16), w_ref[...])
            z_ref[c * SEQ_CHUNK:(c + 1) * SEQ_CHUNK, :] = (mixed * sc_ref[0:1, :]).astype(BF16)

    grid_spec = pltpu.PrefetchScalarGridSpec(
        num_scalar_prefetch=1, grid=(1,),
        in_specs=[pl.BlockSpec((T, W), lambda i, l: (0, 4)),
                  pl.BlockSpec((None, W, W), lambda i, l: (l[0], 0, 0)),
                  pl.BlockSpec((None, 8, W), lambda i, l: (l[0], 0, 0))],
        out_specs=pl.BlockSpec((T, W), lambda i, l: (0, 0)),
    )
    return pl.pallas_call(
        body, name="pool_fwd", grid_spec=grid_spec,
        out_shape=jax.ShapeDtypeStruct((T, W), BF16),
        compiler_params=_cparams(("arbitrary",)),
    )(l, p, w_bd, scale)


def pool_bwd(l, p, dz, w_bd, scale):
    T = p.shape[0]
    W = MIX_W
    nC = T // SEQ_CHUNK

    def body(l_ref, u_ref, dz_ref, w_ref, sc_ref, du_ref, dw_ref, dsc_ref, e_sc, dpl_sc):
        dw = jnp.zeros((W, W), F32)
        dsc = jnp.zeros((1, W), F32)
        for c in range(nC):
            t0 = c * SEQ_CHUNK
            pooled = _pooled_chunk(u_ref, c).astype(BF16)
            mixed = _dot(pooled, w_ref[...])
            dz_c = dz_ref[t0:t0 + SEQ_CHUNK, :]
            dsc += _colsum(dz_c * mixed)
            dmixed = (dz_c * sc_ref[0:1, :]).astype(BF16)
            dw += _dot_tn(pooled, dmixed)
            dpooled = _dot_nt(dmixed, w_ref[...])
            _, inv = _pool_consts(SEQ_CHUNK, t0)
            dpl_sc[t0:t0 + SEQ_CHUNK, :] = dpooled
            e_sc[t0:t0 + SEQ_CHUNK, :] = dpooled * inv
        for c in range(nC):
            t0 = c * SEQ_CHUNK
            ext = _rows_with_tail(lambda s, n: e_sc[s:s + n, :], c, nC, W)
            s2 = ext + _shift_up(ext, 1)
            s4 = s2 + _shift_up(s2, 2)
            s8 = s4 + _shift_up(s4, 4)
            s16 = s8 + _shift_up(s8, 8)
            lane, _ = _pool_consts(SEQ_CHUNK, t0)
            n = SEQ_CHUNK
            du = _pick_window(lane, s2[:n], s4[:n], s8[:n], s16[:n]) - dpl_sc[t0:t0 + SEQ_CHUNK, :]
            du_ref[t0:t0 + SEQ_CHUNK, :] = du.astype(BF16)
        dw_ref[...] = dw
        dsc_ref[...] = jnp.concatenate([dsc, jnp.zeros((7, W), F32)], axis=0)

    grid_spec = pltpu.PrefetchScalarGridSpec(
        num_scalar_prefetch=1, grid=(1,),
        in_specs=[pl.BlockSpec((T, W), lambda i, l: (0, 4)),
                  pl.BlockSpec((T, W), lambda i, l: (0, 0)),
                  pl.BlockSpec((None, W, W), lambda i, l: (l[0], 0, 0)),
                  pl.BlockSpec((None, 8, W), lambda i, l: (l[0], 0, 0))],
        out_specs=[pl.BlockSpec((T, W), lambda i, l: (0, 0)), pl.BlockSpec((W, W), lambda i, l: (0, 0)),
                   pl.BlockSpec((8, W), lambda i, l: (0, 0))],
        scratch_shapes=[pltpu.VMEM((T, W), F32), pltpu.VMEM((T, W), F32)],
    )
    return pl.pallas_call(
        body, name="pool_bwd", grid_spec=grid_spec,
        out_shape=[jax.ShapeDtypeStruct((T, W), BF16), jax.ShapeDtypeStruct((W, W), F32),
                   jax.ShapeDtypeStruct((8, W), F32)],
        compiler_params=_cparams(("arbitrary",)),
    )(l, p, dz, w_bd, scale)


def _s5_disc(lre, lim, ldt):
    lr = jnp.minimum(lre, DT_LAMBDA_RE_MAX)
    dt = jnp.exp(ldt)
    mag = jnp.exp(lr * dt)
    a_re = mag * jnp.cos(lim * dt)
    a_im = mag * jnp.sin(lim * dt)
    den = lr * lr + lim * lim
    nr = a_re - 1.0
    return a_re, a_im, (nr * lr + a_im * lim) / den, (a_im * lr - nr * lim) / den


def _bd_mask(shape, row_blk, col_blk):
    r = lax.broadcasted_iota(jnp.int32, shape, 0) >> (row_blk.bit_length() - 1)
    c = lax.broadcasted_iota(jnp.int32, shape, 1) >> (col_blk.bit_length() - 1)
    return r == c


def s5_params(lam, b_t, c_t):
    L = lam.shape[0]

    def body(lam_ref, b_ref, c_ref, a_ref, bbd_ref, cbd_ref):
        a_re, a_im, f_re, f_im = _s5_disc(lam_ref[0:1, :], lam_ref[1:2, :], lam_ref[2:3, :])
        a_ref[...] = jnp.concatenate([a_re, a_im, jnp.zeros((6, SSM_W), F32)], axis=0)
        mb = _bd_mask((MIX_W, SSM_W), SSM_GROUP, SSM_STATE)
        bbd_ref[0] = jnp.where(mb, f_re * b_ref[0] - f_im * b_ref[1], 0.0).astype(BF16)
        bbd_ref[1] = jnp.where(mb, f_re * b_ref[1] + f_im * b_ref[0], 0.0).astype(BF16)
        mc = _bd_mask((SSM_W, MIX_W), SSM_STATE, SSM_GROUP)
        cbd_ref[0] = jnp.where(mc, c_ref[0], 0.0).astype(BF16)
        cbd_ref[1] = jnp.where(mc, c_ref[1], 0.0).astype(BF16)

    return pl.pallas_call(
        body, name="s5_params", grid=(L,),
        in_specs=[pl.BlockSpec((None, 8, SSM_W), lambda l: (l, 0, 0)),
                  pl.BlockSpec((None, 2, MIX_W, SSM_W), lambda l: (l, 0, 0, 0)),
                  pl.BlockSpec((None, 2, SSM_W, MIX_W), lambda l: (l, 0, 0, 0))],
        out_specs=[pl.BlockSpec((None, 8, SSM_W), lambda l: (l, 0, 0)),
                   pl.BlockSpec((None, 2, MIX_W, SSM_W), lambda l: (l, 0, 0, 0)),
                   pl.BlockSpec((None, 2, SSM_W, MIX_W), lambda l: (l, 0, 0, 0))],
        out_shape=[jax.ShapeDtypeStruct((L, 8, SSM_W), F32),
                   jax.ShapeDtypeStruct((L, 2, MIX_W, SSM_W), BF16),
                   jax.ShapeDtypeStruct((L, 2, SSM_W, MIX_W), BF16)],
        compiler_params=_cparams(("arbitrary",)),
    )(lam, b_t, c_t)


def s5_params_bwd(lam, b_t, gb, gc, da):
    L = lam.shape[0]

    def body(lam_ref, b_ref, gb_ref, gc_ref, da_ref, dlam_ref, db_ref, dc_ref, dgrp_ref):
        lre, lim, ldt = lam_ref[0:1, :], lam_ref[1:2, :], lam_ref[2:3, :]
        (a_re, a_im, f_re, f_im), vjp = jax.vjp(_s5_disc, lre, lim, ldt)
        mb = _bd_mask((MIX_W, SSM_W), SSM_GROUP, SSM_STATE)
        gbr = jnp.where(mb, gb_ref[0], 0.0)
        gbi = jnp.where(mb, gb_ref[1], 0.0)
        df_re = _colsum(gbr * b_ref[0] + gbi * b_ref[1])
        df_im = _colsum(gbi * b_ref[0] - gbr * b_ref[1])
        db_ref[0] = f_re * gbr + f_im * gbi
        db_ref[1] = f_re * gbi - f_im * gbr
        mc = _bd_mask((SSM_W, MIX_W), SSM_STATE, SSM_GROUP)
        dc_ref[0] = jnp.where(mc, gc_ref[0], 0.0)
        dc_ref[1] = jnp.where(mc, gc_ref[1], 0.0)
        dlre, dlim, dldt = vjp((da_ref[0:1, :], da_ref[1:2, :], df_re, df_im))
        dl = jnp.concatenate([dlre, dlim, dldt, jnp.zeros((5, SSM_W), F32)], axis=0)
        dlam_ref[...] = dl
        grp = jnp.where(_bd_mask((SSM_W, 128), SSM_STATE, 1), 1.0, 0.0)
        dgrp_ref[...] = jnp.dot(dl, grp, preferred_element_type=F32, precision=lax.Precision.HIGHEST)

    vec = pl.BlockSpec((None, 8, SSM_W), lambda l: (l, 0, 0))
    bsp = pl.BlockSpec((None, 2, MIX_W, SSM_W), lambda l: (l, 0, 0, 0))
    csp = pl.BlockSpec((None, 2, SSM_W, MIX_W), lambda l: (l, 0, 0, 0))
    return pl.pallas_call(
        body, name="s5_params_bwd", grid=(L,),
        in_specs=[vec, bsp, bsp, csp, vec],
        out_specs=[vec, bsp, csp, pl.BlockSpec((None, 8, 128), lambda l: (l, 0, 0))],
        out_shape=[jax.ShapeDtypeStruct((L, 8, SSM_W), F32),
                   jax.ShapeDtypeStruct((L, 2, MIX_W, SSM_W), F32),
                   jax.ShapeDtypeStruct((L, 2, SSM_W, MIX_W), F32),
                   jax.ShapeDtypeStruct((L, 8, 128), F32)],
        compiler_params=_cparams(("arbitrary",)),
    )(lam, b_t, gb, gc, da)


def s5_bu(l, p, b_bd, tm=512):
    T = p.shape[0]

    def body(l_ref, u_ref, b_ref, bu_ref):
        u = u_ref[...].astype(BF16)
        bu_ref[0] = _dot(u, b_ref[0])
        bu_ref[1] = _dot(u, b_ref[1])

    grid_spec = pltpu.PrefetchScalarGridSpec(
        num_scalar_prefetch=1, grid=(T // tm,),
        in_specs=[pl.BlockSpec((tm, MIX_W), lambda i, l: (i, 3)),
                  pl.BlockSpec((None, 2, MIX_W, SSM_W), lambda i, l: (l[0], 0, 0, 0))],
        out_specs=pl.BlockSpec((2, tm, SSM_W), lambda i, l: (0, i, 0)),
    )
    return pl.pallas_call(
        body, name="s5_bu", grid_spec=grid_spec,
        out_shape=jax.ShapeDtypeStruct((2, T, SSM_W), F32),
        compiler_params=_cparams(("arbitrary",)),
    )(l, p, b_bd)


def s5_scan(l, avec, xs, reverse):
    T = xs.shape[1]
    CH = SEQ_CHUNK
    nC = T // CH
    LW = 128
    n_steps = CH.bit_length() - 1

    def body(l_ref, a_ref, x_ref, s_ref):
        ar = a_ref[0:1, :]
        ai = -a_ref[1:2, :] if reverse else a_ref[1:2, :]
        pows = [(ar, ai)]
        for _ in range(n_steps - 1):
            r, i = pows[-1]
            pows.append((r * r - i * i, 2.0 * r * i))
        row = lax.broadcasted_iota(jnp.int32, (CH, LW), 0)

        def local_scan(re, im):
            for k in range(n_steps):
                d = 1 << k
                pr, pi = pows[k]
                if reverse:
                    keep = row < CH - d
                    sr, si = _shift_up(re, d), _shift_up(im, d)
                else:
                    keep = row >= d
                    sr, si = _shift_down(re, d), _shift_down(im, d)
                sr = jnp.where(keep, sr, 0.0)
                si = jnp.where(keep, si, 0.0)
                re, im = re + pr * sr - pi * si, im + pr * si + pi * sr
            return re, im

        edge = CH - 1 if reverse else 0
        pw_re, pw_im = local_scan(jnp.where(row == edge, ar, 0.0), jnp.where(row == edge, ai, 0.0))
        last = 0 if reverse else CH - 1

        def chunk(c, carry):
            cr, ci = carry
            cc = nC - 1 - c if reverse else c
            t0 = pl.multiple_of(cc * CH, CH)
            re, im = local_scan(x_ref[0, pl.ds(t0, CH), :], x_ref[1, pl.ds(t0, CH), :])
            re2 = re + pw_re * cr - pw_im * ci
            im2 = im + pw_re * ci + pw_im * cr
            s_ref[0, pl.ds(t0, CH), :] = re2
            s_ref[1, pl.ds(t0, CH), :] = im2
            return re2[last:last + 1, :], im2[last:last + 1, :]

        lax.fori_loop(0, nC, chunk, (jnp.zeros((1, LW), F32), jnp.zeros((1, LW), F32)))

    grid_spec = pltpu.PrefetchScalarGridSpec(
        num_scalar_prefetch=1, grid=(SSM_W // LW,),
        in_specs=[pl.BlockSpec((None, 8, LW), lambda g, l: (l[0], 0, g)),
                  pl.BlockSpec((2, T, LW), lambda g, l: (0, 0, g))],
        out_specs=pl.BlockSpec((2, T, LW), lambda g, l: (0, 0, g)),
    )
    return pl.pallas_call(
        body, name="s5_scan_rev" if reverse else "s5_scan_fwd", grid_spec=grid_spec,
        out_shape=jax.ShapeDtypeStruct((2, T, SSM_W), F32),
        compiler_params=_cparams(("arbitrary",)),
    )(l, avec, xs)


_GELU_C = 0.7978845608028654
_GELU_K = 0.044715


def _s5_y(u, s_ref, c_ref, d_row):
    y = _dot(s_ref[0].astype(BF16), c_ref[0]) - _dot(s_ref[1].astype(BF16), c_ref[1])
    return y + d_row * u


def s5_out(l, p, s, c_bd, ssm_d, tm=512):
    T = p.shape[0]

    def body(l_ref, u_ref, s_ref, c_ref, d_ref, yg_ref):
        y = _s5_y(u_ref[...], s_ref, c_ref, d_ref[0:1, :])
        th = jnp.tanh(_GELU_C * (y + _GELU_K * y * y * y))
        yg_ref[...] = (0.5 * y * (1.0 + th)).astype(BF16)

    grid_spec = pltpu.PrefetchScalarGridSpec(
        num_scalar_prefetch=1, grid=(T // tm,),
        in_specs=[pl.BlockSpec((tm, MIX_W), lambda i, l: (i, 3)),
                  pl.BlockSpec((2, tm, SSM_W), lambda i, l: (0, i, 0)),
                  pl.BlockSpec((None, 2, SSM_W, MIX_W), lambda i, l: (l[0], 0, 0, 0)),
                  pl.BlockSpec((None, 8, MIX_W), lambda i, l: (l[0], 0, 0))],
        out_specs=pl.BlockSpec((tm, MIX_W), lambda i, l: (i, 0)),
    )
    return pl.pallas_call(
        body, name="s5_out", grid_spec=grid_spec,
        out_shape=jax.ShapeDtypeStruct((T, MIX_W), BF16),
        compiler_params=_cparams(("arbitrary",)),
    )(l, p, s, c_bd, ssm_d)


def s5_bwd_y(l, p, s, dyg, c_bd, ssm_d, tm=512):
    T = p.shape[0]

    def body(l_ref, u_ref, s_ref, dyg_ref, c_ref, d_ref, ds_ref, du_ref, gc_ref, dd_ref):
        @pl.when(pl.program_id(0) == 0)
        def _():
            gc_ref[...] = jnp.zeros_like(gc_ref)
            dd_ref[...] = jnp.zeros_like(dd_ref)

        u = u_ref[...]
        y = _s5_y(u, s_ref, c_ref, d_ref[0:1, :])
        inner = _GELU_C * (y + _GELU_K * y * y * y)
        th = jnp.tanh(inner)
        dgelu = 0.5 * (1.0 + th) + 0.5 * y * (1.0 - th * th) * (_GELU_C * (1.0 + 3.0 * _GELU_K * y * y))
        dy = dyg_ref[...] * dgelu
        dd_ref[0:1, :] += _colsum(dy * u)
        du_ref[...] = dy * d_ref[0:1, :]
        dyb = dy.astype(BF16)
        ds_ref[0] = _dot_nt(dyb, c_ref[0])
        ds_ref[1] = -_dot_nt(dyb, c_ref[1])
        gc_ref[0] += _dot_tn(s_ref[0].astype(BF16), dyb)
        gc_ref[1] -= _dot_tn(s_ref[1].astype(BF16), dyb)

    grid_spec = pltpu.PrefetchScalarGridSpec(
        num_scalar_prefetch=1, grid=(T // tm,),
        in_specs=[pl.BlockSpec((tm, MIX_W), lambda i, l: (i, 3)),
                  pl.BlockSpec((2, tm, SSM_W), lambda i, l: (0, i, 0)),
                  pl.BlockSpec((tm, MIX_W), lambda i, l: (i, 0)),
                  pl.BlockSpec((None, 2, SSM_W, MIX_W), lambda i, l: (l[0], 0, 0, 0)),
                  pl.BlockSpec((None, 8, MIX_W), lambda i, l: (l[0], 0, 0))],
        out_specs=[pl.BlockSpec((2, tm, SSM_W), lambda i, l: (0, i, 0)),
                   pl.BlockSpec((tm, MIX_W), lambda i, l: (i, 0)),
                   pl.BlockSpec((2, SSM_W, MIX_W), lambda i, l: (0, 0, 0)),
                   pl.BlockSpec((8, MIX_W), lambda i, l: (0, 0))],
    )
    return pl.pallas_call(
        body, name="s5_bwd_y", grid_spec=grid_spec,
        out_shape=[jax.ShapeDtypeStruct((2, T, SSM_W), F32), jax.ShapeDtypeStruct((T, MIX_W), F32),
                   jax.ShapeDtypeStruct((2, SSM_W, MIX_W), F32), jax.ShapeDtypeStruct((8, MIX_W), F32)],
        compiler_params=_cparams(("arbitrary",)),
    )(l, p, s, dyg, c_bd, ssm_d)


def s5_bwd_u(l, p, lam_s, du_skip, b_bd, tm=512):
    T = p.shape[0]

    def body(l_ref, u_ref, ls_ref, dus_ref, b_ref, du_ref, gb_ref):
        @pl.when(pl.program_id(0) == 0)
        def _():
            gb_ref[...] = jnp.zeros_like(gb_ref)

        u = u_ref[...].astype(BF16)
        lr = ls_ref[0].astype(BF16)
        li = ls_ref[1].astype(BF16)
        gb_ref[0] += _dot_tn(u, lr)
        gb_ref[1] += _dot_tn(u, li)
        du_ref[...] = (dus_ref[...] + _dot_nt(lr, b_ref[0]) + _dot_nt(li, b_ref[1])).astype(BF16)

    grid_spec = pltpu.PrefetchScalarGridSpec(
        num_scalar_prefetch=1, grid=(T // tm,),
        in_specs=[pl.BlockSpec((tm, MIX_W), lambda i, l: (i, 3)),
                  pl.BlockSpec((2, tm, SSM_W), lambda i, l: (0, i, 0)),
                  pl.BlockSpec((tm, MIX_W), lambda i, l: (i, 0)),
                  pl.BlockSpec((None, 2, MIX_W, SSM_W), lambda i, l: (l[0], 0, 0, 0))],
        out_specs=[pl.BlockSpec((tm, MIX_W), lambda i, l: (i, 0)),
                   pl.BlockSpec((2, MIX_W, SSM_W), lambda i, l: (0, 0, 0))],
    )
    return pl.pallas_call(
        body, name="s5_bwd_u", grid_spec=grid_spec,
        out_shape=[jax.ShapeDtypeStruct((T, MIX_W), BF16), jax.ShapeDtypeStruct((2, MIX_W, SSM_W), F32)],
        compiler_params=_cparams(("arbitrary",)),
    )(l, p, lam_s, du_skip, b_bd)


def s5_bwd_a(s, lam_s):
    T = s.shape[1]
    nC = T // SEQ_CHUNK
    LW = 128

    def body(s_ref, ls_ref, da_ref):
        dre = jnp.zeros((1, LW), F32)
        dim = jnp.zeros((1, LW), F32)
        for c in range(nC):
            t0 = c * SEQ_CHUNK
            sr = _shift_down(_rows_with_lead(lambda a, n: s_ref[0, a:a + n, :], c, LW), 1)[HALO:]
            si = _shift_down(_rows_with_lead(lambda a, n: s_ref[1, a:a + n, :], c, LW), 1)[HALO:]
            lr = ls_ref[0, t0:t0 + SEQ_CHUNK, :]
            li = ls_ref[1, t0:t0 + SEQ_CHUNK, :]
            dre += _colsum(sr * lr + si * li)
            dim += _colsum(sr * li - si * lr)
        da_ref[...] = jnp.concatenate([dre, dim, jnp.zeros((6, LW), F32)], axis=0)

    blk = pl.BlockSpec((2, T, LW), lambda g: (0, 0, g))
    return pl.pallas_call(
        body, name="s5_bwd_a", grid=(SSM_W // LW,),
        in_specs=[blk, blk],
        out_specs=pl.BlockSpec((8, LW), lambda g: (0, g)),
        out_shape=jax.ShapeDtypeStruct((8, SSM_W), F32),
        compiler_params=_cparams(("arbitrary",)),
    )(s, lam_s)


SB_BLK = 128
SB_SCALE = SB_HEAD ** -0.5


def _split_bf16(x):
    hi = x.astype(BF16)
    return hi, (x - hi.astype(F32)).astype(BF16)


def _dot_split(x, tri):
    hi, lo = _split_bf16(x)
    return _dot(hi, tri) + _dot(lo, tri)


SB_SLABS = MIX_W // SB_BLK
SB_STACK = 2 * SB_SLABS * SB_BLK
SB_PAIR = 2 * SB_BLK


def _sb_valid(r0, c0):
    row = (lax.broadcasted_iota(jnp.int32, (SB_STACK, SB_BLK), 0) & (SB_BLK - 1)) + r0
    col = lax.broadcasted_iota(jnp.int32, (SB_STACK, SB_BLK), 1) + c0
    return col < row


def _sb_stack(ref, r0, scale):
    lane = lax.broadcasted_iota(jnp.int32, (SB_BLK, SB_BLK), 1)
    parts = []
    for s in range(SB_SLABS):
        blk = ref[pl.ds(r0, SB_BLK), s * SB_BLK:(s + 1) * SB_BLK] * scale
        parts += [jnp.where(lane < SB_HEAD, blk, 0.0), jnp.where(lane >= SB_HEAD, blk, 0.0)]
    return jnp.concatenate(parts, axis=0).astype(BF16)


def _sb_rows_nt(stack, ref, c0):
    return jnp.concatenate(
        [_dot_nt(stack[s * SB_PAIR:(s + 1) * SB_PAIR], ref[pl.ds(c0, SB_BLK), s * SB_BLK:(s + 1) * SB_BLK].astype(BF16))
         for s in range(SB_SLABS)], axis=0)


def _sb_wide(stack, s):
    return jnp.concatenate([stack[s * SB_PAIR:s * SB_PAIR + SB_BLK], stack[s * SB_PAIR + SB_BLK:(s + 1) * SB_PAIR]],
                           axis=1)


def _sb_logits(q_stack, k_ref, c0, valid):
    z = _sb_rows_nt(q_stack, k_ref, c0)
    sp = jnp.log(1.0 + jnp.exp(-jnp.abs(z)))
    ls_pos = jnp.minimum(z, 0.0) - sp
    lk = jnp.where(valid, jnp.minimum(-z, 0.0) - sp, 0.0)
    return z, ls_pos, lk


def _tri(lower):
    r = lax.broadcasted_iota(jnp.int32, (SB_BLK, SB_BLK), 0)
    c = lax.broadcasted_iota(jnp.int32, (SB_BLK, SB_BLK), 1)
    return jnp.where(r > c if lower else r < c, 1.0, 0.0).astype(BF16)


def sb_fwd(p):
    T = p.shape[0]
    W = MIX_W
    nB = T // SB_BLK

    def body(q_ref, k_ref, v_ref, o_ref, tot_ref, acc_sc):
        tri = _tri(True)

        def qblock(i, _):
            r0 = pl.multiple_of(i * SB_BLK, SB_BLK)
            q = _sb_stack(q_ref, r0, SB_SCALE)
            acc_sc[...] = jnp.zeros_like(acc_sc)

            def kblock(jj, run):
                c0 = pl.multiple_of((i - jj) * SB_BLK, SB_BLK)
                valid = _sb_valid(r0, c0)
                _, ls_pos, lk = _sb_logits(q, k_ref, c0, valid)
                logw = ls_pos + _dot_split(lk, tri) + run
                a = jnp.where(valid, jnp.exp(logw), 0.0).astype(BF16)
                v = _sb_stack(v_ref, c0, 1.0)
                for s in range(SB_SLABS):
                    acc_sc[:, s * SB_BLK:(s + 1) * SB_BLK] += _dot(_sb_wide(a, s), v[s * SB_PAIR:(s + 1) * SB_PAIR])
                return run + jnp.sum(lk, axis=1, keepdims=True)

            total = lax.fori_loop(0, i + 1, kblock, jnp.zeros((SB_STACK, 1), F32))
            o_ref[pl.ds(r0, SB_BLK), :] = acc_sc[...].astype(BF16)
            tot_ref[pl.ds(pl.multiple_of(i * SB_STACK, SB_STACK), SB_STACK), :] = jnp.broadcast_to(total, (SB_STACK, SB_BLK))
            return 0

        lax.fori_loop(0, nB, qblock, 0)

    return pl.pallas_call(
        body, name="sb_fwd", grid=(1,),
        in_specs=[pl.BlockSpec((T, W), lambda i: (0, 5)), pl.BlockSpec((T, W), lambda i: (0, 6)),
                  pl.BlockSpec((T, W), lambda i: (0, 7))],
        out_specs=[pl.BlockSpec((T, W), lambda i: (0, 0)), pl.BlockSpec((nB * SB_STACK, SB_BLK), lambda i: (0, 0))],
        out_shape=[jax.ShapeDtypeStruct((T, W), BF16), jax.ShapeDtypeStruct((nB * SB_STACK, SB_BLK), F32)],
        scratch_shapes=[pltpu.VMEM((SB_BLK, W), F32)],
        compiler_params=_cparams(("arbitrary",)),
    )(p, p, p)


def sb_bwd(p, do, tot):
    T = p.shape[0]
    W = MIX_W
    nB = T // SB_BLK

    def body(q_ref, k_ref, v_ref, do_ref, tot_ref, dqkv_ref, dq_sc, dk_sc, dv_sc):
        tri_gt = _tri(True)
        tri_lt = _tri(False)
        dq_sc[...] = jnp.zeros_like(dq_sc)
        dk_sc[...] = jnp.zeros_like(dk_sc)
        dv_sc[...] = jnp.zeros_like(dv_sc)
        zcol = jnp.zeros((SB_STACK, 1), F32)

        def qblock(i, _):
            r0 = pl.multiple_of(i * SB_BLK, SB_BLK)
            q = _sb_stack(q_ref, r0, SB_SCALE)
            dob = _sb_stack(do_ref, r0, 1.0)

            total = tot_ref[pl.ds(pl.multiple_of(i * SB_STACK, SB_STACK), SB_STACK), 0:1]

            def kblock(j, carry):
                pre, seen = carry
                c0 = pl.multiple_of(j * SB_BLK, SB_BLK)
                valid = _sb_valid(r0, c0)
                z, ls_pos, lk = _sb_logits(q, k_ref, c0, valid)
                seen = seen + jnp.sum(lk, axis=1, keepdims=True)
                logw = ls_pos + _dot_split(lk, tri_gt) + (total - seen)
                a = jnp.where(valid, jnp.exp(logw), 0.0)
                dlw = _sb_rows_nt(dob, v_ref, c0) * a
                g = pre + _dot_split(dlw, tri_lt)
                sig = _sigmoid(z)
                dz = jnp.where(valid, dlw * (1.0 - sig) - g * sig, 0.0).astype(BF16)
                ab = a.astype(BF16)
                km = _sb_stack(k_ref, c0, 1.0)
                for s in range(SB_SLABS):
                    pair = slice(s * SB_PAIR, (s + 1) * SB_PAIR)
                    ls = slice(s * SB_BLK, (s + 1) * SB_BLK)
                    dk_sc[pl.ds(c0, SB_BLK), ls] += _dot_tn(dz[pair], q[pair])
                    dv_sc[pl.ds(c0, SB_BLK), ls] += _dot_tn(ab[pair], dob[pair])
                    dq_sc[pl.ds(r0, SB_BLK), ls] += _dot(_sb_wide(dz, s), km[pair])
                return pre + jnp.sum(dlw, axis=1, keepdims=True), seen

            lax.fori_loop(0, i + 1, kblock, (zcol, zcol))
            return 0

        lax.fori_loop(0, nB, qblock, 0)
        dqkv_ref[:, 0:W] = (dq_sc[...] * SB_SCALE).astype(BF16)
        dqkv_ref[:, W:2 * W] = dk_sc[...].astype(BF16)
        dqkv_ref[:, 2 * W:3 * W] = dv_sc[...].astype(BF16)

    return pl.pallas_call(
        body, name="sb_bwd", grid=(1,),
        in_specs=[pl.BlockSpec((T, W), lambda i: (0, 5)), pl.BlockSpec((T, W), lambda i: (0, 6)),
                  pl.BlockSpec((T, W), lambda i: (0, 7)), pl.BlockSpec((T, W), lambda i: (0, 0)),
                  pl.BlockSpec((nB * SB_STACK, SB_BLK), lambda i: (0, 0))],
        out_specs=pl.BlockSpec((T, 3 * W), lambda i: (0, 0)),
        out_shape=jax.ShapeDtypeStruct((T, 3 * W), BF16),
        scratch_shapes=[pltpu.VMEM((T, W), F32), pltpu.VMEM((T, W), F32), pltpu.VMEM((T, W), F32)],
        compiler_params=_cparams(("arbitrary",)),
    )(p, p, p, do, tot)


def _dot_cols(a, w_ref):
    return jnp.concatenate([_dot(a, w_ref[j]) for j in range(N_DEV)], axis=1)


def _dot_cols_nt(dy, w_ref):
    n = w_ref.shape[2]
    out = _dot_nt(dy[:, 0:n], w_ref[0])
    for j in range(1, N_DEV):
        out += _dot_nt(dy[:, j * n:(j + 1) * n], w_ref[j])
    return out


def _acc_cols_tn(acc_ref, a, dy):
    n = acc_ref.shape[2]
    for j in range(N_DEV):
        acc_ref[j] += _dot_tn(a, dy[:, j * n:(j + 1) * n])


def _merge_branches(za_ref, yg_ref, z_ref, o_ref, gate_refs, wc_ref, wglu_ref, wp_ref, ws_ref):
    D = D_MODEL
    glu = _dot_cols(yg_ref[...], wglu_ref)
    glu_a, sg = glu[:, :D], _sigmoid(glu[:, D:])
    ys = [_dot_cols(za_ref[...], wc_ref), glu_a * sg, _dot_cols(z_ref[...], wp_ref), _dot_cols(o_ref[...], ws_ref)]
    gs = [_sigmoid(g[...]) for g in gate_refs]
    merged = gs[0] * ys[0] + gs[1] * ys[1] + gs[2] * ys[2] + gs[3] * ys[3]
    return ys, gs, glu_a, sg, merged


def _merge_specs(tm, D):
    W = MIX_W
    br = pl.BlockSpec((tm, W), lambda i, l: (i, 0))
    gates = [pl.BlockSpec((tm, D), functools.partial(lambda i, l, b: (i, 2 + b), b=b)) for b in range(4)]
    wsm = pl.BlockSpec((None, N_DEV, W, D // N_DEV), lambda i, l: (l[0], 0, 0, 0))
    weights = [wsm, pl.BlockSpec((None, N_DEV, W, 2 * D // N_DEV), lambda i, l: (l[0], 0, 0, 0)), wsm, wsm,
               pl.BlockSpec((None, D, D), lambda i, l: (l[0], 0, 0))]
    return [br] * 4 + gates, weights


def merge_fwd(l, p, za, yg, z, o, x, pv, wc, wglu, wp, ws, wo, tm=512):
    T, D = x.shape
    tm = min(tm, T)

    def body(l_ref, za_ref, yg_ref, z_ref, o_ref, g0, g1, g2, g3, x_ref, pv_ref,
             wc_ref, wglu_ref, wp_ref, ws_ref, wo_ref, xn_ref, m_ref):
        _, _, _, _, merged = _merge_branches(za_ref, yg_ref, z_ref, o_ref, (g0, g1, g2, g3),
                                             wc_ref, wglu_ref, wp_ref, ws_ref)
        m = _dot(merged.astype(BF16), wo_ref[...])
        m_ref[...] = m
        xn_ref[...] = _postnorm_res(x_ref[...], m, pv_ref, 1.0)

    acts, weights = _merge_specs(tm, D)
    tile = pl.BlockSpec((tm, D), lambda i, l: (i, 0))
    grid_spec = pltpu.PrefetchScalarGridSpec(
        num_scalar_prefetch=1, grid=(T // tm,),
        in_specs=acts + [tile, pl.BlockSpec((8, D), lambda i, l: (0, 0))] + weights,
        out_specs=[tile, tile],
    )
    return pl.pallas_call(
        body, name="merge_fwd", grid_spec=grid_spec,
        out_shape=[jax.ShapeDtypeStruct((T, D), F32), jax.ShapeDtypeStruct((T, D), F32)],
        compiler_params=_cparams(("arbitrary",)),
    )(l, za, yg, z, o, p, p, p, p, x, pv, wc, wglu, wp, ws, wo)


def merge_bwd(l, p, za, yg, z, o, m, dxn, pv, wc, wglu, wp, ws, wo, tm=256):
    T, D = m.shape
    W = MIX_W
    tm = min(tm, T)
    ni = T // tm

    def body(l_ref, za_ref, yg_ref, z_ref, o_ref, g0, g1, g2, g3, m_ref, dxn_ref, pv_ref,
             wc_ref, wglu_ref, wp_ref, ws_ref, wo_ref,
             dza_ref, dyg_ref, dz_ref, do_ref, dg_ref, pg_ref, gwc_ref, gwglu_ref, gwp_ref, gws_ref, gwo_ref,
             awc, awglu, awp, aws, awo):
        i = pl.program_id(0)

        @pl.when(i == 0)
        def _():
            pg_ref[...] = jnp.zeros_like(pg_ref)
            for a in (awc, awglu, awp, aws, awo):
                a[...] = jnp.zeros_like(a)

        ys, gs, glu_a, sg, merged = _merge_branches(za_ref, yg_ref, z_ref, o_ref, (g0, g1, g2, g3),
                                                    wc_ref, wglu_ref, wp_ref, ws_ref)
        dm = _postnorm_bwd(dxn_ref[...], m_ref[...], pv_ref, pg_ref, 1.0).astype(BF16)
        awo[...] += _dot_tn(merged.astype(BF16), dm)
        dmerged = _dot_nt(dm, wo_ref[...])
        for b in range(4):
            dg_ref[:, b * D:(b + 1) * D] = (dmerged * ys[b] * gs[b] * (1.0 - gs[b])).astype(BF16)
        dya = (dmerged * gs[0]).astype(BF16)
        _acc_cols_tn(awc, za_ref[...], dya)
        dza_ref[...] = _dot_cols_nt(dya, wc_ref)
        dyc = (dmerged * gs[2]).astype(BF16)
        _acc_cols_tn(awp, z_ref[...], dyc)
        dz_ref[...] = _dot_cols_nt(dyc, wp_ref)
        dyd = (dmerged * gs[3]).astype(BF16)
        _acc_cols_tn(aws, o_ref[...], dyd)
        do_ref[...] = _dot_cols_nt(dyd, ws_ref)
        dyb = dmerged * gs[1]
        dglu = jnp.concatenate([dyb * sg, dyb * glu_a * sg * (1.0 - sg)], axis=1).astype(BF16)
        _acc_cols_tn(awglu, yg_ref[...], dglu)
        dyg_ref[...] = _dot_cols_nt(dglu, wglu_ref)

        @pl.when(i == ni - 1)
        def _():
            gwc_ref[...] = awc[...].astype(BF16)
            gwglu_ref[...] = awglu[...].astype(BF16)
            gwp_ref[...] = awp[...].astype(BF16)
            gws_ref[...] = aws[...].astype(BF16)
            gwo_ref[...] = awo[...].astype(BF16)

    acts, weights = _merge_specs(tm, D)
    tile = pl.BlockSpec((tm, D), lambda i, l: (i, 0))
    br = pl.BlockSpec((tm, W), lambda i, l: (i, 0))
    full = lambda *s: pl.BlockSpec(s, lambda i, l: (0,) * len(s))
    sm, glu_s = (N_DEV, W, D // N_DEV), (N_DEV, W, 2 * D // N_DEV)
    grid_spec = pltpu.PrefetchScalarGridSpec(
        num_scalar_prefetch=1, grid=(ni,),
        in_specs=acts + [tile, tile, pl.BlockSpec((8, D), lambda i, l: (0, 0))] + weights,
        out_specs=[br, br, br, br, pl.BlockSpec((tm, 4 * D), lambda i, l: (i, 0)), full(8, D),
                   full(*sm), full(*glu_s), full(*sm), full(*sm), full(D, D)],
        scratch_shapes=[pltpu.VMEM(sm, F32), pltpu.VMEM(glu_s, F32), pltpu.VMEM(sm, F32),
                        pltpu.VMEM(sm, F32), pltpu.VMEM((D, D), F32)],
    )
    f32br = jax.ShapeDtypeStruct((T, W), F32)
    return pl.pallas_call(
        body, name="merge_bwd", grid_spec=grid_spec,
        out_shape=[f32br, f32br, f32br, f32br, jax.ShapeDtypeStruct((T, 4 * D), BF16),
                   jax.ShapeDtypeStruct((8, D), F32),
                   jax.ShapeDtypeStruct(sm, BF16), jax.ShapeDtypeStruct(glu_s, BF16),
                   jax.ShapeDtypeStruct(sm, BF16), jax.ShapeDtypeStruct(sm, BF16),
                   jax.ShapeDtypeStruct((D, D), BF16)],
        compiler_params=_cparams(("arbitrary",)),
    )(l, za, yg, z, o, p, p, p, p, m, dxn, pv, wc, wglu, wp, ws, wo)


def dp_assemble(d_conv, d_ssm, d_pool, d_qkv, d_gates, tm=512):
    T = d_conv.shape[0]
    W = MIX_W

    def body(c_ref, s_ref, p_ref, q_ref, g_ref, dp_ref):
        dp_ref[:, 0:3 * W] = c_ref[...]
        dp_ref[:, 3 * W:4 * W] = s_ref[...]
        dp_ref[:, 4 * W:5 * W] = p_ref[...]
        dp_ref[:, 5 * W:8 * W] = q_ref[...]
        dp_ref[:, GATE_OFF:] = g_ref[...]

    row = lambda w: pl.BlockSpec((tm, w), lambda i: (i, 0))
    return pl.pallas_call(
        body, name="dp_assemble", grid=(T // tm,),
        in_specs=[row(3 * W), row(W), row(W), row(3 * W), row(4 * D_MODEL)],
        out_specs=row(IN_COLS),
        out_shape=jax.ShapeDtypeStruct((T, IN_COLS), BF16),
        compiler_params=_cparams(("arbitrary",)),
    )(d_conv, d_ssm, d_pool, d_qkv, d_gates)


def loss_head(y, target, tm=512):
    T, D = y.shape

    def body(y_ref, t_ref, dy_ref, loss_ref):
        @pl.when(pl.program_id(0) == 0)
        def _():
            loss_ref[...] = jnp.zeros_like(loss_ref)

        err = y_ref[...] - t_ref[...]
        dy_ref[...] = err * (1.0 / D)
        loss_ref[...] += jnp.sum(err * err) * (0.5 / D)

    tile = pl.BlockSpec((tm, D), lambda i: (i, 0))
    return pl.pallas_call(
        body, name="loss_head", grid=(T // tm,),
        in_specs=[tile, tile],
        out_specs=[tile, pl.BlockSpec((8, 128), lambda i: (0, 0))],
        out_shape=[jax.ShapeDtypeStruct((T, D), F32), jax.ShapeDtypeStruct((8, 128), F32)],
        compiler_params=_cparams(("arbitrary",)),
    )(y, target)


def cast_layer(ld, items):
    def body(ld_ref, *refs):
        n = len(refs) // 2
        for src, dst in zip(refs[:n], refs[n:]):
            dst[...] = src[...].astype(BF16)

    def shard(w, k):
        return w.shape[1:] if k is None else w.shape[2:]

    def in_spec(w, k):
        sh = shard(w, k)
        if k is None:
            return pl.BlockSpec((None,) + sh, lambda i, ld, n=len(sh): (ld[0],) + (0,) * n)
        return pl.BlockSpec((None, None) + sh, lambda i, ld, n=len(sh), k=k: (ld[0], k) + (0,) * n)

    def out_spec(w, k):
        sh = shard(w, k)
        return pl.BlockSpec((None, None) + sh, lambda i, ld, n=len(sh): (0, ld[1]) + (0,) * n)

    grid_spec = pltpu.PrefetchScalarGridSpec(
        num_scalar_prefetch=1, grid=(1,),
        in_specs=[in_spec(w, k) for w, k in items], out_specs=[out_spec(w, k) for w, k in items])
    return pl.pallas_call(
        body, name="cast_layer", grid_spec=grid_spec,
        out_shape=[jax.ShapeDtypeStruct((1, N_DEV) + shard(w, k), BF16) for w, k in items],
        compiler_params=_cparams(("arbitrary",)),
    )(ld, *[w for w, _ in items])


def place_own(dev, a):
    def body(dev_ref, a_ref, o_ref):
        o_ref[...] = a_ref[...]

    grid_spec = pltpu.PrefetchScalarGridSpec(
        num_scalar_prefetch=1, grid=(1,),
        in_specs=[pl.BlockSpec(a.shape, lambda i, dev: (0, 0))],
        out_specs=pl.BlockSpec((None,) + a.shape, lambda i, dev: (dev[0], 0, 0)))
    return pl.pallas_call(
        body, name="place_own", grid_spec=grid_spec,
        out_shape=jax.ShapeDtypeStruct((N_DEV,) + a.shape, a.dtype),
        compiler_params=_cparams(("arbitrary",)),
    )(dev, a)


def _silu(x):
    return x * _sigmoid(x)


def ada_fwd(c_all, w_ada, b_cols):
    L, D, n = w_ada.shape

    def body(c_ref, w_ref, b_ref, o_ref):
        c_act = _silu(c_ref[...]).astype(BF16)
        o_ref[...] = _dot(c_act, w_ref[...].astype(BF16)) + b_ref[...]

    return pl.pallas_call(
        body, name="ada_fwd", grid=(L,),
        in_specs=[pl.BlockSpec((N_DEV, D), lambda l: (0, 0)), pl.BlockSpec((None, D, n), lambda l: (l, 0, 0)),
                  pl.BlockSpec((None, 1, n), lambda l: (l, 0, 0))],
        out_specs=pl.BlockSpec((None, N_DEV, n), lambda l: (l, 0, 0)),
        out_shape=jax.ShapeDtypeStruct((L, N_DEV, n), F32),
        compiler_params=_cparams(("arbitrary",)),
    )(c_all, w_ada, b_cols)


def _adamw(w, g, m, v):
    m = ADAM_B1 * m + (1.0 - ADAM_B1) * g
    v = ADAM_B2 * v + (1.0 - ADAM_B2) * (g * g)
    m_hat = m / (1.0 - ADAM_B1 ** ADAM_STEP)
    v_hat = v / (1.0 - ADAM_B2 ** ADAM_STEP)
    delta = -ADAM_LR * (m_hat / (jnp.sqrt(v_hat) + ADAM_EPS) + ADAM_WD * w)
    return delta, m, v


def ada_update(c_all, dada_cols, w, m, v, rb=256):
    L, D, n = w.shape

    def body(c_ref, d_ref, w_ref, m_ref, v_ref, g_ref, dl_ref, nm_ref, nv_ref):
        c_act = _silu(c_ref[...]).astype(BF16)
        g = _dot_tn(c_act, d_ref[...].astype(BF16))
        g_ref[...] = g
        dl_ref[...], nm_ref[...], nv_ref[...] = _adamw(w_ref[...], g, m_ref[...], v_ref[...])

    blk = pl.BlockSpec((None, rb, n), lambda l, i: (l, i, 0))
    out = jax.ShapeDtypeStruct((L, D, n), F32)
    return pl.pallas_call(
        body, name="ada_update", grid=(L, D // rb),
        in_specs=[pl.BlockSpec((N_DEV, rb), lambda l, i: (0, i)),
                  pl.BlockSpec((None, N_DEV, n), lambda l, i: (l, 0, 0)), blk, blk, blk],
        out_specs=[blk, blk, blk, blk], out_shape=[out, out, out, out],
        compiler_params=_cparams(("arbitrary", "arbitrary")),
    )(c_all, dada_cols, w, m, v)


SUM_UPDATE_RECV_BYTES = 12 * 1024 * 1024


def sum_update(dev, first, recvs, owns, w, m, v, prev=None, after=None):
    n_slots, R, C = w.shape
    S = len(recvs)
    assert len(owns) == S and first + S <= n_slots
    rb_max = SUM_UPDATE_RECV_BYTES // (S * N_DEV * C * 2)
    rb = max(r for r in range(8, R + 1, 8) if R % r == 0 and (r <= rb_max or r == 8))
    last = R // rb - 1
    n_prev = 0 if prev is None else 4
    extra = list(prev or ()) + ([] if after is None else [after])

    def body(dev_ref, *refs):
        r_refs, o_refs = refs[:S], refs[S:2 * S]
        w_ref, m_ref, v_ref = refs[2 * S:2 * S + 3]
        g_ref, dl_ref, nm_ref, nv_ref = refs[2 * S + 3 + len(extra):]
        me = dev_ref[0]
        for s in range(S):
            @pl.when(pl.program_id(0) == s)
            def _(s=s):
                g = jnp.zeros((rb, C), F32)
                for d in range(N_DEV):
                    g += jnp.where(me == d, o_refs[s][...], r_refs[s][d]).astype(F32)
                g_ref[...] = g
                dl_ref[...], nm_ref[...], nv_ref[...] = _adamw(w_ref[...], g, m_ref[...], v_ref[...])

    def row(sl, i, s):
        return jnp.where(sl == s, i, jnp.where(sl < s, 0, last))

    def rspec(s):
        return pl.BlockSpec((N_DEV, rb, C), lambda sl, i, dev: (0, row(sl, i, s), 0))

    def ospec(s):
        return pl.BlockSpec((None, rb, C), lambda sl, i, dev: (dev[0], row(sl, i, s), 0))

    blk = pl.BlockSpec((None, rb, C), lambda sl, i, dev: (first + sl, i, 0))
    out = jax.ShapeDtypeStruct((n_slots, R, C), F32)
    grid_spec = pltpu.PrefetchScalarGridSpec(
        num_scalar_prefetch=1, grid=(S, R // rb),
        in_specs=[rspec(s) for s in range(S)] + [ospec(s) for s in range(S)] + [blk, blk, blk] + [ANY] * len(extra),
        out_specs=[blk, blk, blk, blk],
    )
    n_in = 1 + 2 * S + 3
    return pl.pallas_call(
        body, name="sum_update", grid_spec=grid_spec, out_shape=[out, out, out, out],
        input_output_aliases={n_in + i: i for i in range(n_prev)},
        compiler_params=_cparams(("arbitrary", "arbitrary")),
    )(dev, *recvs, *owns, w, m, v, *extra)


def small_sum(gathered):
    _, R, C = gathered.shape

    def body(g_ref, o_ref):
        acc = g_ref[0]
        for d in range(1, N_DEV):
            acc += g_ref[d]
        o_ref[...] = acc

    return pl.pallas_call(
        body, name="small_sum", grid=(1,),
        in_specs=[pl.BlockSpec((N_DEV, R, C), lambda i: (0, 0, 0))],
        out_specs=pl.BlockSpec((R, C), lambda i: (0, 0)),
        out_shape=jax.ShapeDtypeStruct((R, C), F32),
        compiler_params=_cparams(("arbitrary",)),
    )(gathered)


def small_update(w, g, m, v):
    def body(w_ref, g_ref, m_ref, v_ref, dl_ref, nm_ref, nv_ref):
        dl_ref[...], nm_ref[...], nv_ref[...] = _adamw(w_ref[...], g_ref[...], m_ref[...], v_ref[...])

    blk = pl.BlockSpec(w.shape, lambda i: (0, 0))
    out = jax.ShapeDtypeStruct(w.shape, F32)
    return pl.pallas_call(
        body, name="small_update", grid=(1,),
        in_specs=[blk] * 4, out_specs=[blk] * 3, out_shape=[out] * 3,
        compiler_params=_cparams(("arbitrary",)),
    )(w, g, m, v)


MESH = pl.DeviceIdType.MESH
ANY = pl.BlockSpec(memory_space=pl.ANY)


def _coords():
    return lax.axis_index("x"), lax.axis_index("y"), lax.axis_index("c")


def _dev_index(x, y, c):
    return 4 * x + 2 * y + c


def _at_dev(ref, p, dev):
    return ref.at[(slice(None),) * p + (dev,)]


def all_gather(arrays, ps):
    n = len(arrays)

    def body(*refs):
        ins, outs = refs[:n], refs[n:2 * n]
        send_sems, recv_sems, local_sems = refs[2 * n:]
        x, y, c = _coords()
        me, sibling = (x, y, c), (x, y, 1 - c)
        chips = [(1 - x, y), (x, 1 - y), (1 - x, 1 - y)]

        def copy(a, k, block, to, src=None):
            dst = _at_dev(outs[a], ps[a], _dev_index(*block))
            return pltpu.make_async_remote_copy(
                src_ref=dst if src is None else src, dst_ref=dst,
                send_sem=send_sems.at[a, k], recv_sem=recv_sems.at[a, k], device_id=to, device_id_type=MESH)

        mine = [pltpu.make_async_copy(ins[a], _at_dev(outs[a], ps[a], _dev_index(*me)), local_sems.at[a])
                for a in range(n)]
        for cp in mine:
            cp.start()
        first = []
        for a in range(n):
            first.append(copy(a, 0, me, sibling, src=ins[a]))
            first += [copy(a, 1 + j, me, (*chip, c), src=ins[a]) for j, chip in enumerate(chips)]
        for cp in first:
            cp.start()
        passed = []
        for j, chip in enumerate(chips):
            for a in range(n):
                copy(a, 1 + j, (*chip, c), me).wait_recv()
                fwd = copy(a, 4 + j, (*chip, c), sibling)
                fwd.start()
                passed.append(fwd)
        for a in range(n):
            copy(a, 0, sibling, me).wait_recv()
            for j, chip in enumerate(chips):
                copy(a, 4 + j, (*chip, 1 - c), me).wait_recv()
        for cp in first + passed:
            cp.wait_send()
        for cp in mine:
            cp.wait()

    out_shape = [jax.ShapeDtypeStruct(a.shape[:p] + (N_DEV,) + a.shape[p:], a.dtype) for a, p in zip(arrays, ps)]
    return pl.pallas_call(
        body, name="all_gather", in_specs=[ANY] * n, out_specs=[ANY] * n, out_shape=out_shape,
        scratch_shapes=[pltpu.SemaphoreType.DMA((n, 7)), pltpu.SemaphoreType.DMA((n, 7)),
                        pltpu.SemaphoreType.DMA((n,))],
        compiler_params=pltpu.CompilerParams(has_side_effects=True),
    )(*arrays)


HBM = pl.BlockSpec(memory_space=pltpu.HBM)
SEM = pl.BlockSpec(memory_space=pltpu.SEMAPHORE)
EFFECT = pltpu.SideEffectType.DATAFLOW_SIDE_EFFECTING


def _peers(x, y, c):
    out = []
    for k in range(1, N_DEV):
        out.append((1 - x if k & 4 else x, 1 - y if k & 2 else y, 1 - c if k & 1 else c))
    return out


def _exchange_plan(n):
    def plan(refs, x, y, c):
        blocks, lands = refs[:n], refs[n:2 * n]
        me = _dev_index(x, y, c)
        moves = []
        for peer in _peers(x, y, c):
            q = _dev_index(*peer)
            moves += [(blocks[a].at[q], lands[a].at[me], peer, lands[a].at[q]) for a in range(n)]
        return moves
    return plan


def _gather_plan(ps, second):
    def plan(refs, x, y, c):
        me, sibling = (x, y, c), (x, y, 1 - c)
        chips = [(1 - x, y), (x, 1 - y), (1 - x, 1 - y)]
        if second:
            trips = [((*ch, c), sibling, (*ch, 1 - c)) for ch in chips]
        else:
            trips = [(me, sibling, sibling)] + [(me, (*ch, c), (*ch, c)) for ch in chips]
        moves = []
        for sent, to, arriving in trips:
            for ref, p in zip(refs, ps):
                blk = _at_dev(ref, p, _dev_index(*sent))
                moves.append((blk, blk, to, _at_dev(ref, p, _dev_index(*arriving))))
        return moves
    return plan


def copies_start(name, plan, n_moves, arrays, carry):
    n = len(arrays)

    def body(*refs):
        sems = refs[n + 1:n + 1 + 2 * n_moves]
        moves = plan(refs[:n], *_coords())
        assert len(moves) == n_moves
        for i, (src, dst, to, _) in enumerate(moves):
            pltpu.make_async_remote_copy(src_ref=src, dst_ref=dst, send_sem=sems[i], recv_sem=sems[n_moves + i],
                                         device_id=to, device_id_type=MESH).start()

    operands = [pltpu.with_memory_space_constraint(a, pltpu.HBM) for a in list(arrays) + [carry]]
    outs = pl.pallas_call(
        body, name=name,
        out_shape=[pltpu.SemaphoreType.DMA(())] * (2 * n_moves) + [pltpu.HBM(a.shape, a.dtype) for a in operands],
        in_specs=[HBM] * (n + 1), out_specs=[SEM] * (2 * n_moves) + [HBM] * (n + 1),
        input_output_aliases={i: 2 * n_moves + i for i in range(n + 1)},
        compiler_params=pltpu.CompilerParams(has_side_effects=EFFECT),
    )(*operands)
    return outs[:n_moves], outs[n_moves:2 * n_moves], outs[2 * n_moves:-1], outs[-1]


def copies_wait(name, plan, send_sems, recv_sems, arrays, after):
    n, n_moves = len(arrays), len(send_sems)

    def body(*refs):
        sems = refs[n:n + 2 * n_moves]
        for i, (src, _, to, arriving) in enumerate(plan(refs[:n], *_coords())):
            cp = pltpu.make_async_remote_copy(src_ref=src, dst_ref=arriving, send_sem=sems[i],
                                              recv_sem=sems[n_moves + i], device_id=to, device_id_type=MESH)
            cp.wait_send()
            cp.wait_recv()

    return pl.pallas_call(
        body, name=name,
        out_shape=[pltpu.HBM(a.shape, a.dtype) for a in arrays],
        in_specs=[HBM] * n + [SEM] * (2 * n_moves) + [ANY], out_specs=[HBM] * n,
        input_output_aliases={i: i for i in range(n)},
        compiler_params=pltpu.CompilerParams(has_side_effects=EFFECT),
    )(*arrays, *send_sems, *recv_sems, after)


WEIGHT_NAMES = ("w_ada", "b_ada", "g_pre", "g_post", "w_ff_in", "w_ff_out", "w_in", "conv_w", "w_conv_out",
                "lam_re", "lam_im", "log_dt", "ssm_b_re", "ssm_b_im", "ssm_c_re", "ssm_c_im", "ssm_d", "w_glu",
                "w_pool", "pool_scale", "w_pool_out", "w_sb_out", "w_out")
BIG_NAMES = ("w_ff_in", "w_ff_out", "w_in", "w_conv_out", "w_glu", "w_pool_out", "w_sb_out", "w_out")
SMALL_NAMES = ("b_ada", "g_pre", "g_post", "conv_w", "lam_re", "lam_im", "log_dt", "ssm_b_re", "ssm_b_im",
               "ssm_c_re", "ssm_c_im", "ssm_d", "w_pool", "pool_scale")
PACK_LANES = 128
PACK_ROWS = 8


def _pack(arrays):
    flat = jnp.concatenate([a.reshape(-1) for a in arrays])
    unit = PACK_LANES * PACK_ROWS
    flat = jnp.pad(flat, (0, -flat.shape[0] % unit))
    return flat.reshape(-1, PACK_LANES)


def _unpack(pack, shapes):
    flat = pack.reshape(-1)
    out, off = [], 0
    for s in shapes:
        n = 1
        for d in s:
            n *= d
        out.append(flat[off:off + n].reshape(s))
        off += n
    return out


def _pad_rows(a, rows=8):
    return jnp.pad(a, ((0, 0), (0, rows - a.shape[1]), (0, 0)))


def _tile_b(b):
    L = b.shape[0]
    return jnp.tile(b.transpose(0, 3, 1, 2).reshape(L, SSM_GROUP, SSM_W), (1, SSM_GROUPS, 1))


def _tile_c(c):
    L = c.shape[0]
    return jnp.tile(c.transpose(0, 3, 1, 2).reshape(L, SSM_STATE, MIX_W), (1, SSM_GROUPS, 1))


def _step(x, c, target, W, M, V):
    T, D = x.shape[1], x.shape[2]
    L = W["w_ada"].shape[0]
    x = x[0]
    target = target[0]
    ax, ay, ac = _coords()
    dev = _dev_index(ax, ay, ac)
    n_ada = W["w_ada"].shape[2]

    dev_s = jnp.reshape(dev, (1,)).astype(jnp.int32)
    items = ([(W["w_ff_in"], 0), (W["w_ff_in"], 1), (W["w_ff_out"], 0), (W["w_ff_out"], 1)]
             + [(W[k], None) for k in BIG_NAMES[2:]])
    bufs = [list(cast_layer(jnp.concatenate([jnp.array([l], jnp.int32), dev_s]), items)) for l in range(L)]
    ffn1_w, mixer_w, ffn2_w = (0, 2), (4, 5, 6, 7, 8, 9), (1, 3)
    all_w = tuple(range(len(items)))

    def gather_start(tag, second, l, idx, carry):
        plan = _gather_plan((1,) * len(idx), second)
        n_moves = (3 if second else 4) * len(idx)
        s_sem, r_sem, arrs, carry = copies_start(f"gather_{'b' if second else 'a'}_start_{tag}", plan, n_moves,
                                                 [bufs[l][i] for i in idx], carry)
        for i, a in zip(idx, arrs):
            bufs[l][i] = a
        return (plan, s_sem, r_sem), carry

    def gather_wait(tag, second, l, idx, flight, after):
        arrs = copies_wait(f"gather_{'b' if second else 'a'}_wait_{tag}", *flight, [bufs[l][i] for i in idx], after)
        for i, a in zip(idx, arrs):
            bufs[l][i] = a

    def gather_finish(tag, l, idx, flight, after, carry):
        gather_wait(tag, False, l, idx, flight, after)
        flight, carry = gather_start(tag, True, l, idx, carry)
        gather_wait(tag, True, l, idx, flight, carry)
        return carry

    first = []
    for g, idx in enumerate((ffn1_w, mixer_w, ffn2_w)):
        flight, x = gather_start(f"0_{g}", False, 0, idx, x)
        first.append(flight)

    gathered = all_gather([W["g_pre"], W["g_post"], W["conv_w"], c], [0, 0, 0, 0])
    g_pre = gathered[0].transpose(1, 2, 0, 3).reshape(L, N_SUB, D)
    g_post = gathered[1].transpose(1, 2, 0, 3).reshape(L, N_SUB, D)
    conv_w = _pad_rows(gathered[2].transpose(1, 2, 0, 3).reshape(L, 3, MIX_W))
    c_all = gathered[3].reshape(N_DEV, D)

    b_cols = lax.dynamic_slice_in_dim(W["b_ada"], dev * n_ada, n_ada, axis=1)[:, None, :]
    ada_cols = ada_fwd(c_all, W["w_ada"], b_cols)
    ada_all = all_gather([ada_cols], [0])[0]
    ada = lax.dynamic_index_in_dim(ada_all, dev, axis=2, keepdims=False)
    ada = ada.transpose(1, 0, 2).reshape(L, N_SUB, 3, D)
    zeros = jnp.zeros((L, N_SUB, D), F32)
    pv_all = jnp.stack([g_pre, ada[:, :, 0], ada[:, :, 1], g_post, ada[:, :, 2], zeros, zeros, zeros], axis=2)

    lam = jnp.stack([W["lam_re"].reshape(L, SSM_W), W["lam_im"].reshape(L, SSM_W),
                     jnp.repeat(W["log_dt"], SSM_STATE, axis=1)], axis=1)
    lam = _pad_rows(lam)
    b_t = jnp.stack([_tile_b(W["ssm_b_re"]), _tile_b(W["ssm_b_im"])], axis=1)
    c_t = jnp.stack([_tile_c(W["ssm_c_re"]), _tile_c(W["ssm_c_im"])], axis=1)
    avec, b_bd, c_bd = s5_params(lam, b_t, c_t)
    ssm_d = _pad_rows(W["ssm_d"][:, None, :])
    pool_scale = _pad_rows(W["pool_scale"][:, None, :])
    eye4 = jnp.eye(len(POOL_WINDOWS), dtype=F32)
    w_bd = jnp.einsum("lgcd,gh->lgchd", W["w_pool"], eye4).reshape(L, MIX_W, MIX_W).astype(BF16)

    x = gather_finish("0_0", 0, ffn1_w, first[0], pv_all, x)

    def ffn_weights(l, k):
        b = bufs[l]
        return b[k].reshape(1, 1, 2, 4, D, FF_BLK), b[2 + k].reshape(1, 1, 4, FF_BLK, D)

    def mixer_weights(l):
        b = bufs[l]
        return b[4], b[5], b[6], b[7], b[8], b[9].reshape(1, D, D)

    l0 = jnp.array([0], jnp.int32)
    k0 = jnp.array([0, 0], jnp.int32)
    saved = []
    for l in range(L):
        li = jnp.array([l], jnp.int32)
        nxt = l + 1 < L
        if nxt:
            flight, x = gather_start(f"{l + 1}", False, l + 1, all_w, x)
        x0 = x
        ab0, f0, x1 = ffn_fwd(k0, x0, pv_all[l, 0], *ffn_weights(l, 0))
        if l == 0:
            x1 = gather_finish("0_1", 0, mixer_w, first[1], x1, x1)
        wg_in, wg_conv, wg_glu, wg_pool, wg_sb, wg_out = mixer_weights(l)
        p = mix_in_fwd(l0, x1, pv_all[l, 1], wg_in)
        za = conv_fwd(li, p, conv_w)
        z = pool_fwd(li, p, w_bd, pool_scale)
        s = s5_scan(li, avec, s5_bu(li, p, b_bd), False)
        yg = s5_out(li, p, s, c_bd, ssm_d)
        o, sb_tot = sb_fwd(p)
        x2, m = merge_fwd(l0, p, za, yg, z, o, x1, pv_all[l, 1], wg_conv, wg_glu, wg_pool, wg_sb, wg_out)
        if l == 0:
            x2 = gather_finish("0_2", 0, ffn2_w, first[2], x2, x2)
        if nxt:
            gather_wait(f"{l + 1}", False, l + 1, all_w, flight, x2)
            flight, x2 = gather_start(f"{l + 1}", True, l + 1, all_w, x2)
        ab1, f1, x = ffn_fwd(k0, x2, pv_all[l, 2], *ffn_weights(l, 1))
        if nxt:
            gather_wait(f"{l + 1}", True, l + 1, all_w, flight, x)
        saved.append((x0, ab0, f0, x1, p, za, z, s, yg, o, sb_tot, m, x2, ab1, f1))

    dx, loss_blk = loss_head(x, target)
    loss = lax.psum(loss_blk[0, 0], ("x", "y", "c"))

    n_blocks = 10
    late_idx, early_idx = (0, 2), (1, 3, 4, 5, 6, 7, 8, 9)
    recvs, owns, in_flight = [[None] * n_blocks for _ in range(L)], [[None] * n_blocks for _ in range(L)], []

    def exchange_start(tag, layer, idx, blocks, carry):
        n = len(idx)
        plan = _exchange_plan(n)
        arrays = list(blocks) + [lax.empty(a.shape, a.dtype) for a in blocks]
        s_sem, r_sem, arrays, carry = copies_start(f"exchange_start_{tag}", plan, (N_DEV - 1) * n, arrays, carry)
        in_flight.append((tag, layer, idx, plan, s_sem, r_sem, arrays))
        return carry

    def settle(after):
        while in_flight:
            tag, layer, idx, plan, s_sem, r_sem, arrays = in_flight.pop(0)
            arrays = copies_wait(f"exchange_wait_{tag}", plan, s_sem, r_sem, arrays, after)
            for j, i in enumerate(idx):
                owns[layer][i], recvs[layer][i] = arrays[j], arrays[len(idx) + j]

    pgs = [None] * L
    small = {k: [None] * L for k in ("conv_w", "w_bd", "pool_scale", "ssm_d", "gb", "gc", "da")}
    for l in reversed(range(L)):
        li = jnp.array([l], jnp.int32)
        x0, ab0, f0, x1, p, za, z, s, yg, o, sb_tot, m, x2, ab1, f1 = saved[l]
        wg_in, wg_conv, wg_glu, wg_pool, wg_sb, wg_out = mixer_weights(l)
        dab, h, df, dx, pg2 = ffn_bwd_act(k0, dx, x2, f1, pv_all[l, 2], ab1, *ffn_weights(l, 1))
        g_in1, g_out1 = ffn_bwd_w(h, df, ab1, dab)
        (dza, dyg, dz, do, dgates, pg1m, g_conv, g_glu, g_pool, g_sb, g_wo) = merge_bwd(
            l0, p, za, yg, z, o, m, dx, pv_all[l, 1], wg_conv, wg_glu, wg_pool, wg_sb, wg_out)
        d_conv, small["conv_w"][l] = conv_bwd(li, p, dza, conv_w)
        d_pool, small["w_bd"][l], small["pool_scale"][l] = pool_bwd(li, p, dz, w_bd, pool_scale)
        ds, du_skip, small["gc"][l], small["ssm_d"][l] = s5_bwd_y(li, p, s, dyg, c_bd, ssm_d)
        lam_s = s5_scan(li, avec, ds, True)
        d_ssm, small["gb"][l] = s5_bwd_u(li, p, lam_s, du_skip, b_bd)
        small["da"][l] = s5_bwd_a(s, lam_s)
        d_qkv = sb_bwd(p, do, sb_tot)
        dp = dp_assemble(d_conv, d_ssm, d_pool, d_qkv, dgates)
        dx, h, pg1i = mix_in_bwd_act(l0, dp, dx, x1, pv_all[l, 1], wg_in)
        g_win = matmul_tn(h, dp, IN_BLK)
        settle(dx)
        dx = exchange_start(f"{l}_early", l, early_idx,
                            [g_in1.reshape(N_DEV, D, FF_BLK), g_out1.reshape(N_DEV, D_FF // N_DEV, D), g_win, g_conv,
                             g_glu, g_pool, g_sb, g_wo.reshape(N_DEV, D // N_DEV, D)], dx)
        dab, h, df, dx, pg0 = ffn_bwd_act(k0, dx, x0, f0, pv_all[l, 0], ab0, *ffn_weights(l, 0))
        g_in0, g_out0 = ffn_bwd_w(h, df, ab0, dab)
        pgs[l] = jnp.stack([pg0, pg1m + pg1i, pg2])
        dx = exchange_start(f"{l}_late", l, late_idx,
                            [g_in0.reshape(N_DEV, D, FF_BLK), g_out0.reshape(N_DEV, D_FF // N_DEV, D)], dx)

    dlam, db_t, dc_t, dldt = s5_params_bwd(lam, b_t, jnp.stack(small["gb"]), jnp.stack(small["gc"]),
                                           jnp.stack(small["da"]))
    pg = jnp.stack(pgs)
    d_ada = jnp.stack([pg[:, :, PV_SHIFT], pg[:, :, PV_SCALE], pg[:, :, PV_GATE]], axis=2).reshape(L, N_SUB * 3 * D)
    db = db_t.reshape(L, 2, SSM_GROUPS, SSM_GROUP, SSM_GROUPS, SSM_STATE)
    db = jnp.einsum("lrghgp->lrgph", db)
    dc = dc_t.reshape(L, 2, SSM_GROUPS, SSM_STATE, SSM_GROUPS, SSM_GROUP)
    dc = jnp.einsum("lrgpgh->lrghp", dc)
    d_wpool = jnp.einsum("lgcgd->lgcd", jnp.stack(small["w_bd"]).reshape(L, 4, 64, 4, 64))
    contrib = {
        "b_ada": d_ada, "g_pre": pg[:, :, PV_GPRE], "g_post": pg[:, :, PV_GPOST],
        "conv_w": jnp.stack(small["conv_w"])[:, :3], "lam_re": dlam[:, 0].reshape(L, SSM_GROUPS, SSM_STATE),
        "lam_im": dlam[:, 1].reshape(L, SSM_GROUPS, SSM_STATE), "log_dt": dldt[:, 2, :SSM_GROUPS],
        "ssm_b_re": db[:, 0], "ssm_b_im": db[:, 1], "ssm_c_re": dc[:, 0], "ssm_c_im": dc[:, 1],
        "ssm_d": jnp.stack(small["ssm_d"])[:, 0], "w_pool": d_wpool,
        "pool_scale": jnp.stack(small["pool_scale"])[:, 0],
    }
    contrib_shapes = [contrib[k].shape for k in SMALL_NAMES]

    big_idx = {"w_ff_in": (0, 1), "w_ff_out": (2, 3), "w_in": (4,), "w_conv_out": (5,), "w_glu": (6,),
               "w_pool_out": (7,), "w_sb_out": (8,), "w_out": (9,)}

    def big(name, layers, prev, after=None):
        idx = big_idx[name]
        flat = (-1,) + W[name].shape[-2:]
        return sum_update(dev_s, layers[0] * len(idx), [recvs[l][i] for l in layers for i in idx],
                          [owns[l][i] for l in layers for i in idx],
                          W[name].reshape(flat), M[name].reshape(flat), V[name].reshape(flat), prev, after)

    partial = {}

    def partial_updates(names, after):
        for name in names:
            if L > 1:
                partial[name] = big(name, list(range(1, L)), None, after)
                after = partial[name][0]
        return after

    pack_buf = [place_own(dev_s, _pack([contrib[k] for k in SMALL_NAMES]))]
    plan_a, plan_b = _gather_plan((0,), False), _gather_plan((0,), True)
    s_sem, r_sem, pack_buf, dx = copies_start("small_gather_a_start", plan_a, 4, pack_buf, dx)
    after = partial_updates(BIG_NAMES[:1], dx)
    pack_buf = copies_wait("small_gather_a_wait", plan_a, s_sem, r_sem, pack_buf, after)
    s_sem, r_sem, pack_buf, dx = copies_start("small_gather_b_start", plan_b, 3, pack_buf, dx)
    after = partial_updates(BIG_NAMES[1:], dx)
    pack_all = copies_wait("small_gather_b_wait", plan_b, s_sem, r_sem, pack_buf, after)[0]
    total = dict(zip(SMALL_NAMES, _unpack(small_sum(pack_all), contrib_shapes)))
    d_ada_all = pack_all.reshape(N_DEV, -1)[:, :L * N_SUB * 3 * D].reshape(N_DEV, L, N_SUB * 3 * D)
    dada_cols = lax.dynamic_slice_in_dim(d_ada_all, dev * n_ada, n_ada, axis=2).transpose(1, 0, 2)
    n_g = D // N_DEV
    grads = {}
    for k in SMALL_NAMES:
        g = total[k]
        if k in ("g_pre", "g_post"):
            g = lax.dynamic_slice_in_dim(g, dev * n_g, n_g, axis=2)
        elif k == "conv_w":
            g = lax.dynamic_slice_in_dim(g, dev * (MIX_W // N_DEV), MIX_W // N_DEV, axis=2)
        grads[k] = g

    delta, new_m, new_v = {}, {}, {}
    shapes = [W[k].shape for k in SMALL_NAMES]
    dl, nm, nv = small_update(_pack([W[k] for k in SMALL_NAMES]), _pack([grads[k] for k in SMALL_NAMES]),
                              _pack([M[k] for k in SMALL_NAMES]), _pack([V[k] for k in SMALL_NAMES]))
    for k, a, b, cc in zip(SMALL_NAMES, _unpack(dl, shapes), _unpack(nm, shapes), _unpack(nv, shapes)):
        delta[k], new_m[k], new_v[k] = a, b, cc
    grads["w_ada"], delta["w_ada"], new_m["w_ada"], new_v["w_ada"] = ada_update(
        c_all, dada_cols, W["w_ada"], M["w_ada"], V["w_ada"])

    settle(new_m["w_ada"])
    for name in BIG_NAMES:
        outs = big(name, [0], partial.get(name))
        grads[name], delta[name], new_m[name], new_v[name] = [o.reshape(W[name].shape) for o in outs]

    return (loss, dx[None], *[grads[k] for k in WEIGHT_NAMES], *[delta[k] for k in WEIGHT_NAMES],
            *[new_m[k] for k in WEIGHT_NAMES], *[new_v[k] for k in WEIGHT_NAMES])


def kernel(x, c, w_ada, b_ada, g_pre, g_post, w_ff_in, w_ff_out, w_in, conv_w, w_conv_out, lam_re, lam_im, log_dt, ssm_b_re, ssm_b_im, ssm_c_re, ssm_c_im, ssm_d, w_glu, w_pool, pool_scale, w_pool_out, w_sb_out, w_out, loss_target, m_w_ada, m_b_ada, m_g_pre, m_g_post, m_w_ff_in, m_w_ff_out, m_w_in, m_conv_w, m_w_conv_out, m_lam_re, m_lam_im, m_log_dt, m_ssm_b_re, m_ssm_b_im, m_ssm_c_re, m_ssm_c_im, m_ssm_d, m_w_glu, m_w_pool, m_pool_scale, m_w_pool_out, m_w_sb_out, m_w_out, v_w_ada, v_b_ada, v_g_pre, v_g_post, v_w_ff_in, v_w_ff_out, v_w_in, v_conv_w, v_w_conv_out, v_lam_re, v_lam_im, v_log_dt, v_ssm_b_re, v_ssm_b_im, v_ssm_c_re, v_ssm_c_im, v_ssm_d, v_w_glu, v_w_pool, v_pool_scale, v_w_pool_out, v_w_sb_out, v_w_out):
    w = (w_ada, b_ada, g_pre, g_post, w_ff_in, w_ff_out, w_in, conv_w, w_conv_out, lam_re, lam_im, log_dt, ssm_b_re, ssm_b_im, ssm_c_re, ssm_c_im, ssm_d, w_glu, w_pool, pool_scale, w_pool_out, w_sb_out, w_out)
    m = (m_w_ada, m_b_ada, m_g_pre, m_g_post, m_w_ff_in, m_w_ff_out, m_w_in, m_conv_w, m_w_conv_out, m_lam_re, m_lam_im, m_log_dt, m_ssm_b_re, m_ssm_b_im, m_ssm_c_re, m_ssm_c_im, m_ssm_d, m_w_glu, m_w_pool, m_pool_scale, m_w_pool_out, m_w_sb_out, m_w_out)
    v = (v_w_ada, v_b_ada, v_g_pre, v_g_post, v_w_ff_in, v_w_ff_out, v_w_in, v_conv_w, v_w_conv_out, v_lam_re, v_lam_im, v_log_dt, v_ssm_b_re, v_ssm_b_im, v_ssm_c_re, v_ssm_c_im, v_ssm_d, v_w_glu, v_w_pool, v_pool_scale, v_w_pool_out, v_w_sb_out, v_w_out)
    return _step(x, c, loss_target, dict(zip(WEIGHT_NAMES, w)), dict(zip(WEIGHT_NAMES, m)), dict(zip(WEIGHT_NAMES, v)))
```

```python
import functools

import jax
import jax.numpy as jnp
from jax import lax
from jax.experimental import pallas as pl
from jax.experimental.pallas import tpu as pltpu

F32 = jnp.float32
BF16 = jnp.bfloat16

N_DEV = 8
D_MODEL = 1024
D_FF = 2816
FF_BLK = D_FF // 4
N_SUB = 3
MIX_W = 256
IN_COLS = 6144
IN_BLK = IN_COLS // N_DEV
GATE_OFF = 2048
SSM_GROUPS, SSM_GROUP, SSM_STATE = 16, 16, 64
SSM_W = SSM_GROUPS * SSM_STATE
POOL_WINDOWS = (2, 4, 8, 16)
SB_HEAD = 64
EPS = 1e-6
DT_LAMBDA_RE_MAX = -1e-4
ADAM_LR, ADAM_B1, ADAM_B2, ADAM_EPS, ADAM_WD, ADAM_STEP = 0.001, 0.9, 0.999, 1e-08, 0.01, 10

VMEM_LIMIT = 56 * 1024 * 1024

PV_GPRE, PV_SHIFT, PV_SCALE, PV_GPOST, PV_GATE = 0, 1, 2, 3, 4


def _cparams(sem):
    return pltpu.CompilerParams(dimension_semantics=sem, vmem_limit_bytes=VMEM_LIMIT)


def _dot(a, b):
    return jnp.dot(a, b, preferred_element_type=F32)


def _dot_nt(a, b):
    return lax.dot_general(a, b, (((1,), (1,)), ((), ())), preferred_element_type=F32)


def _dot_tn(a, b):
    return lax.dot_general(a, b, (((0,), (0,)), ((), ())), preferred_element_type=F32)


def _rms(x):
    r = lax.rsqrt(jnp.mean(x * x, axis=-1, keepdims=True) + EPS)
    return x * r, r


def _rms_bwd(dn, n, r):
    return r * (dn - n * jnp.mean(dn * n, axis=-1, keepdims=True))


def _sigmoid(x):
    return 1.0 / (1.0 + jnp.exp(-x))


def _colsum(x):
    return jnp.sum(x, axis=0, keepdims=True)


def _prenorm(x, pv_ref):
    n, r = _rms(x)
    hn = n * pv_ref[PV_GPRE:PV_GPRE + 1, :]
    h = hn * (1.0 + pv_ref[PV_SCALE:PV_SCALE + 1, :]) + pv_ref[PV_SHIFT:PV_SHIFT + 1, :]
    return h, n, r, hn


def _prenorm_bwd(dh, dxn, x, pv_ref, pg_ref):
    _, n, r, hn = _prenorm(x, pv_ref)
    pg_ref[PV_SHIFT:PV_SHIFT + 1, :] += _colsum(dh)
    pg_ref[PV_SCALE:PV_SCALE + 1, :] += _colsum(dh * hn)
    dhn = dh * (1.0 + pv_ref[PV_SCALE:PV_SCALE + 1, :])
    pg_ref[PV_GPRE:PV_GPRE + 1, :] += _colsum(dhn * n)
    dn = dhn * pv_ref[PV_GPRE:PV_GPRE + 1, :]
    return dxn + _rms_bwd(dn, n, r)


def _postnorm_res(x, f, pv_ref, coef):
    nf, _ = _rms(f)
    return x + (coef * (1.0 + pv_ref[PV_GATE:PV_GATE + 1, :])) * (nf * pv_ref[PV_GPOST:PV_GPOST + 1, :])


def _postnorm_bwd(dxn, f, pv_ref, pg_ref, coef):
    nf, rf = _rms(f)
    g_post = pv_ref[PV_GPOST:PV_GPOST + 1, :]
    pg_ref[PV_GATE:PV_GATE + 1, :] += _colsum(dxn * (nf * g_post)) * coef
    dnfg = dxn * (coef * (1.0 + pv_ref[PV_GATE:PV_GATE + 1, :]))
    pg_ref[PV_GPOST:PV_GPOST + 1, :] += _colsum(dnfg * nf)
    return _rms_bwd(dnfg * g_post, nf, rf)


def ffn_fwd(lk, x, pv, wg_in, wg_out, tm=1024):
    T, D = x.shape
    tm = min(tm, T)
    nj = 4

    def body(lk_ref, x_ref, pv_ref, win_ref, wout_ref, ab_ref, f_ref, xn_ref, h_sc, acc):
        j = pl.program_id(1)

        @pl.when(j == 0)
        def _():
            h, _, _, _ = _prenorm(x_ref[...], pv_ref)
            h_sc[...] = h.astype(BF16)
            acc[...] = jnp.zeros_like(acc)

        h = h_sc[...]
        a = _dot(h, win_ref[0])
        b = _dot(h, win_ref[1])
        ab_ref[0] = a.astype(BF16)
        ab_ref[1] = b.astype(BF16)
        act = (a * _sigmoid(a) * b).astype(BF16)
        acc[...] += _dot(act, wout_ref[...])

        @pl.when(j == nj - 1)
        def _():
            f = acc[...]
            f_ref[...] = f
            xn_ref[...] = _postnorm_res(x_ref[...], f, pv_ref, 0.5)

    grid_spec = pltpu.PrefetchScalarGridSpec(
        num_scalar_prefetch=1, grid=(T // tm, nj),
        in_specs=[
            pl.BlockSpec((tm, D), lambda i, j, lk: (i, 0)),
            pl.BlockSpec((8, D), lambda i, j, lk: (0, 0)),
            pl.BlockSpec((None, None, 2, None, D, FF_BLK), lambda i, j, lk: (lk[0], lk[1], 0, j, 0, 0)),
            pl.BlockSpec((None, None, None, FF_BLK, D), lambda i, j, lk: (lk[0], lk[1], j, 0, 0)),
        ],
        out_specs=[
            pl.BlockSpec((2, None, tm, FF_BLK), lambda i, j, lk: (0, j, i, 0)),
            pl.BlockSpec((tm, D), lambda i, j, lk: (i, 0)),
            pl.BlockSpec((tm, D), lambda i, j, lk: (i, 0)),
        ],
        scratch_shapes=[pltpu.VMEM((tm, D), BF16), pltpu.VMEM((tm, D), F32)],
    )
    return pl.pallas_call(
        body, name="ffn_fwd", grid_spec=grid_spec,
        out_shape=[jax.ShapeDtypeStruct((2, nj, T, FF_BLK), BF16),
                   jax.ShapeDtypeStruct((T, D), F32), jax.ShapeDtypeStruct((T, D), F32)],
        compiler_params=_cparams(("arbitrary", "arbitrary")),
    )(lk, x, pv, wg_in, wg_out)


def ffn_bwd_act(lk, dxn, x, f, pv, ab, wg_in, wg_out, tm=512):
    T, D = x.shape
    tm = min(tm, T)
    nj = 4

    def body(lk_ref, dxn_ref, x_ref, f_ref, pv_ref, ab_ref, win_ref, wout_ref,
             dab_ref, h_ref, df_ref, dx_ref, pg_ref, dacc):
        i, j = pl.program_id(0), pl.program_id(1)

        @pl.when((i == 0) & (j == 0))
        def _():
            pg_ref[...] = jnp.zeros_like(pg_ref)

        @pl.when(j == 0)
        def _():
            df = _postnorm_bwd(dxn_ref[...], f_ref[...], pv_ref, pg_ref, 0.5)
            df_ref[...] = df.astype(BF16)
            h, _, _, _ = _prenorm(x_ref[...], pv_ref)
            h_ref[...] = h.astype(BF16)
            dacc[...] = jnp.zeros_like(dacc)

        dact = _dot_nt(df_ref[...], wout_ref[...])
        a = ab_ref[0].astype(F32)
        b = ab_ref[1].astype(F32)
        sig = _sigmoid(a)
        s = a * sig
        da = (dact * b * (sig * (1.0 + a * (1.0 - sig)))).astype(BF16)
        db = (dact * s).astype(BF16)
        dab_ref[0] = da
        dab_ref[1] = db
        dacc[...] += _dot_nt(da, win_ref[0]) + _dot_nt(db, win_ref[1])

        @pl.when(j == nj - 1)
        def _():
            dx_ref[...] = _prenorm_bwd(dacc[...], dxn_ref[...], x_ref[...], pv_ref, pg_ref)

    tile = pl.BlockSpec((tm, D), lambda i, j, lk: (i, 0))
    blk = pl.BlockSpec((2, None, tm, FF_BLK), lambda i, j, lk: (0, j, i, 0))
    grid_spec = pltpu.PrefetchScalarGridSpec(
        num_scalar_prefetch=1, grid=(T // tm, nj),
        in_specs=[tile, tile, tile, pl.BlockSpec((8, D), lambda i, j, lk: (0, 0)), blk,
                  pl.BlockSpec((None, None, 2, None, D, FF_BLK), lambda i, j, lk: (lk[0], lk[1], 0, j, 0, 0)),
                  pl.BlockSpec((None, None, None, FF_BLK, D), lambda i, j, lk: (lk[0], lk[1], j, 0, 0))],
        out_specs=[blk, tile, tile, tile, pl.BlockSpec((8, D), lambda i, j, lk: (0, 0))],
        scratch_shapes=[pltpu.VMEM((tm, D), F32)],
    )
    return pl.pallas_call(
        body, name="ffn_bwd_act", grid_spec=grid_spec,
        out_shape=[jax.ShapeDtypeStruct((2, nj, T, FF_BLK), BF16), jax.ShapeDtypeStruct((T, D), BF16),
                   jax.ShapeDtypeStruct((T, D), BF16), jax.ShapeDtypeStruct((T, D), F32),
                   jax.ShapeDtypeStruct((8, D), F32)],
        compiler_params=_cparams(("arbitrary", "arbitrary")),
    )(lk, dxn, x, f, pv, ab, wg_in, wg_out)


def ffn_bwd_w(h, df, ab, dab, tm=1024):
    T, D = h.shape
    tm = min(tm, T)
    nj, ni = 4, T // tm

    def body(h_ref, df_ref, ab_ref, dab_ref, gin_ref, gout_ref, acc_in, acc_out):
        i = pl.program_id(1)

        @pl.when(i == 0)
        def _():
            acc_in[...] = jnp.zeros_like(acc_in)
            acc_out[...] = jnp.zeros_like(acc_out)

        h = h_ref[...]
        acc_in[0] += _dot_tn(h, dab_ref[0])
        acc_in[1] += _dot_tn(h, dab_ref[1])
        a = ab_ref[0].astype(F32)
        b = ab_ref[1].astype(F32)
        act = (a * _sigmoid(a) * b).astype(BF16)
        acc_out[...] += _dot_tn(act, df_ref[...])

        @pl.when(i == ni - 1)
        def _():
            gin_ref[...] = acc_in[...].astype(BF16)
            gout_ref[...] = acc_out[...].astype(BF16)

    tile = pl.BlockSpec((tm, D), lambda j, i: (i, 0))
    blk = pl.BlockSpec((2, None, tm, FF_BLK), lambda j, i: (0, j, i, 0))
    return pl.pallas_call(
        body, name="ffn_bwd_w", grid=(nj, ni),
        in_specs=[tile, tile, blk, blk],
        out_specs=[pl.BlockSpec((2, None, D, FF_BLK), lambda j, i: (0, j, 0, 0)),
                   pl.BlockSpec((None, FF_BLK, D), lambda j, i: (j, 0, 0))],
        out_shape=[jax.ShapeDtypeStruct((2, nj, D, FF_BLK), BF16),
                   jax.ShapeDtypeStruct((nj, FF_BLK, D), BF16)],
        scratch_shapes=[pltpu.VMEM((2, D, FF_BLK), F32), pltpu.VMEM((FF_BLK, D), F32)],
        compiler_params=_cparams(("arbitrary", "arbitrary")),
    )(h, df, ab, dab)


def mix_in_fwd(l, x, pv, wg, tm=1024):
    T, D = x.shape
    tm = min(tm, T)

    def body(l_ref, x_ref, pv_ref, w_ref, p_ref, h_sc):
        @pl.when(pl.program_id(1) == 0)
        def _():
            h, _, _, _ = _prenorm(x_ref[...], pv_ref)
            h_sc[...] = h.astype(BF16)

        p_ref[...] = _dot(h_sc[...], w_ref[...])

    grid_spec = pltpu.PrefetchScalarGridSpec(
        num_scalar_prefetch=1, grid=(T // tm, N_DEV),
        in_specs=[pl.BlockSpec((tm, D), lambda i, j, l: (i, 0)),
                  pl.BlockSpec((8, D), lambda i, j, l: (0, 0)),
                  pl.BlockSpec((None, None, D, IN_BLK), lambda i, j, l: (l[0], j, 0, 0))],
        out_specs=pl.BlockSpec((tm, IN_BLK), lambda i, j, l: (i, j)),
        scratch_shapes=[pltpu.VMEM((tm, D), BF16)],
    )
    return pl.pallas_call(
        body, name="mix_in_fwd", grid_spec=grid_spec,
        out_shape=jax.ShapeDtypeStruct((T, IN_COLS), F32),
        compiler_params=_cparams(("arbitrary", "arbitrary")),
    )(l, x, pv, wg)


def mix_in_bwd_act(l, dp, dxn, x, pv, wg, tm=1024):
    T, D = x.shape
    tm = min(tm, T)

    def body(l_ref, dp_ref, dxn_ref, x_ref, pv_ref, w_ref, dx_ref, h_ref, pg_ref, dacc):
        i, j = pl.program_id(0), pl.program_id(1)

        @pl.when((i == 0) & (j == 0))
        def _():
            pg_ref[...] = jnp.zeros_like(pg_ref)

        @pl.when(j == 0)
        def _():
            dacc[...] = jnp.zeros_like(dacc)

        dacc[...] += _dot_nt(dp_ref[...], w_ref[...])

        @pl.when(j == N_DEV - 1)
        def _():
            h, _, _, _ = _prenorm(x_ref[...], pv_ref)
            h_ref[...] = h.astype(BF16)
            dx_ref[...] = _prenorm_bwd(dacc[...], dxn_ref[...], x_ref[...], pv_ref, pg_ref)

    tile = pl.BlockSpec((tm, D), lambda i, j, l: (i, 0))
    grid_spec = pltpu.PrefetchScalarGridSpec(
        num_scalar_prefetch=1, grid=(T // tm, N_DEV),
        in_specs=[pl.BlockSpec((tm, IN_BLK), lambda i, j, l: (i, j)), tile, tile,
                  pl.BlockSpec((8, D), lambda i, j, l: (0, 0)),
                  pl.BlockSpec((None, None, D, IN_BLK), lambda i, j, l: (l[0], j, 0, 0))],
        out_specs=[tile, tile, pl.BlockSpec((8, D), lambda i, j, l: (0, 0))],
        scratch_shapes=[pltpu.VMEM((tm, D), F32)],
    )
    return pl.pallas_call(
        body, name="mix_in_bwd_act", grid_spec=grid_spec,
        out_shape=[jax.ShapeDtypeStruct((T, D), F32), jax.ShapeDtypeStruct((T, D), BF16),
                   jax.ShapeDtypeStruct((8, D), F32)],
        compiler_params=_cparams(("arbitrary", "arbitrary")),
    )(l, dp, dxn, x, pv, wg)


def matmul_tn(a, b, tn, tm=1024):
    T, M = a.shape
    tm = min(tm, T)
    N = b.shape[1]
    ni = T // tm

    def body(a_ref, b_ref, o_ref, acc):
        i = pl.program_id(1)

        @pl.when(i == 0)
        def _():
            acc[...] = jnp.zeros_like(acc)

        acc[...] += _dot_tn(a_ref[...], b_ref[...])

        @pl.when(i == ni - 1)
        def _():
            o_ref[...] = acc[...].astype(o_ref.dtype)

    return pl.pallas_call(
        body, name="matmul_tn", grid=(N // tn, ni),
        in_specs=[pl.BlockSpec((tm, M), lambda j, i: (i, 0)), pl.BlockSpec((tm, tn), lambda j, i: (i, j))],
        out_specs=pl.BlockSpec((None, M, tn), lambda j, i: (j, 0, 0)),
        out_shape=jax.ShapeDtypeStruct((N // tn, M, tn), BF16),
        scratch_shapes=[pltpu.VMEM((M, tn), F32)],
        compiler_params=_cparams(("arbitrary", "arbitrary")),
    )(a, b)


SEQ_CHUNK = 256
HALO = 16


def _shift_down(ext, d):
    return pltpu.roll(ext, d, 0)


def _shift_up(ext, d):
    return pltpu.roll(ext, ext.shape[0] - d, 0)


def _rows_with_lead(load, c, width):
    t0 = c * SEQ_CHUNK
    if c == 0:
        return jnp.concatenate([jnp.zeros((HALO, width), F32), load(0, SEQ_CHUNK)], axis=0)
    return load(t0 - HALO, SEQ_CHUNK + HALO)


def _rows_with_tail(load, c, n_chunks, width):
    t0 = c * SEQ_CHUNK
    if c == n_chunks - 1:
        return jnp.concatenate([load(t0, SEQ_CHUNK), jnp.zeros((HALO, width), F32)], axis=0)
    return load(t0, SEQ_CHUNK + HALO)


def conv_fwd(l, p, conv_w):
    T = p.shape[0]
    W = MIX_W
    nC = T // SEQ_CHUNK

    def body(l_ref, p_ref, w_ref, za_ref):
        w0, w1, w2 = w_ref[0:1, :], w_ref[1:2, :], w_ref[2:3, :]
        for c in range(nC):
            ext = _rows_with_lead(lambda s, n: p_ref[s:s + n, W:2 * W] * p_ref[s:s + n, 2 * W:3 * W], c, W)
            y = w2 * ext + w1 * _shift_down(ext, 1) + w0 * _shift_down(ext, 2)
            t0 = c * SEQ_CHUNK
            za_ref[t0:t0 + SEQ_CHUNK, :] = (p_ref[t0:t0 + SEQ_CHUNK, 0:W] * y[HALO:]).astype(BF16)

    grid_spec = pltpu.PrefetchScalarGridSpec(
        num_scalar_prefetch=1, grid=(1,),
        in_specs=[pl.BlockSpec((T, 3 * W), lambda i, l: (0, 0)),
                  pl.BlockSpec((None, 8, W), lambda i, l: (l[0], 0, 0))],
        out_specs=pl.BlockSpec((T, W), lambda i, l: (0, 0)),
    )
    return pl.pallas_call(
        body, name="conv_fwd", grid_spec=grid_spec,
        out_shape=jax.ShapeDtypeStruct((T, W), BF16),
        compiler_params=_cparams(("arbitrary",)),
    )(l, p, conv_w)


def conv_bwd(l, p, dza, conv_w):
    T = p.shape[0]
    W = MIX_W
    nC = T // SEQ_CHUNK

    def body(l_ref, p_ref, dza_ref, w_ref, dp_ref, dw_ref):
        w0, w1, w2 = w_ref[0:1, :], w_ref[1:2, :], w_ref[2:3, :]
        dw = [jnp.zeros((1, W), F32) for _ in range(3)]
        for c in range(nC):
            t0 = c * SEQ_CHUNK
            ext = _rows_with_lead(lambda s, n: p_ref[s:s + n, W:2 * W] * p_ref[s:s + n, 2 * W:3 * W], c, W)
            u1, u2 = _shift_down(ext, 1)[HALO:], _shift_down(ext, 2)[HALO:]
            u0 = ext[HALO:]
            y = w2 * u0 + w1 * u1 + w0 * u2
            dza_c = dza_ref[t0:t0 + SEQ_CHUNK, :]
            dy = dza_c * p_ref[t0:t0 + SEQ_CHUNK, 0:W]
            dw[0] += _colsum(dy * u2)
            dw[1] += _colsum(dy * u1)
            dw[2] += _colsum(dy * u0)
            dye = _rows_with_tail(lambda s, n: dza_ref[s:s + n, :] * p_ref[s:s + n, 0:W], c, nC, W)
            du = (w2 * dye + w1 * _shift_up(dye, 1) + w0 * _shift_up(dye, 2))[:SEQ_CHUNK]
            dp_ref[t0:t0 + SEQ_CHUNK, 0:W] = (dza_c * y).astype(BF16)
            dp_ref[t0:t0 + SEQ_CHUNK, W:2 * W] = (du * p_ref[t0:t0 + SEQ_CHUNK, 2 * W:3 * W]).astype(BF16)
            dp_ref[t0:t0 + SEQ_CHUNK, 2 * W:3 * W] = (du * p_ref[t0:t0 + SEQ_CHUNK, W:2 * W]).astype(BF16)
        dw_ref[...] = jnp.concatenate(dw + [jnp.zeros((5, W), F32)], axis=0)

    grid_spec = pltpu.PrefetchScalarGridSpec(
        num_scalar_prefetch=1, grid=(1,),
        in_specs=[pl.BlockSpec((T, 3 * W), lambda i, l: (0, 0)),
                  pl.BlockSpec((T, W), lambda i, l: (0, 0)),
                  pl.BlockSpec((None, 8, W), lambda i, l: (l[0], 0, 0))],
        out_specs=[pl.BlockSpec((T, 3 * W), lambda i, l: (0, 0)), pl.BlockSpec((8, W), lambda i, l: (0, 0))],
    )
    return pl.pallas_call(
        body, name="conv_bwd", grid_spec=grid_spec,
        out_shape=[jax.ShapeDtypeStruct((T, 3 * W), BF16), jax.ShapeDtypeStruct((8, W), F32)],
        compiler_params=_cparams(("arbitrary",)),
    )(l, p, dza, conv_w)


def _pool_consts(rows, t0):
    lane = lax.broadcasted_iota(jnp.int32, (rows, MIX_W), 1)
    t = lax.broadcasted_iota(jnp.int32, (rows, MIX_W), 0) + t0
    win = jnp.where(lane < 64, 2, jnp.where(lane < 128, 4, jnp.where(lane < 192, 8, 16)))
    inv = 1.0 / jnp.minimum(t + 1, win).astype(F32)
    return lane, inv


def _pick_window(lane, s2, s4, s8, s16):
    return jnp.where(lane < 64, s2, jnp.where(lane < 128, s4, jnp.where(lane < 192, s8, s16)))


def _pooled_chunk(u_ref, c):
    ext = _rows_with_lead(lambda s, n: u_ref[s:s + n, :], c, MIX_W)
    s2 = ext + _shift_down(ext, 1)
    s4 = s2 + _shift_down(s2, 2)
    s8 = s4 + _shift_down(s4, 4)
    s16 = s8 + _shift_down(s8, 8)
    lane, inv = _pool_consts(SEQ_CHUNK, c * SEQ_CHUNK)
    return _pick_window(lane, s2[HALO:], s4[HALO:], s8[HALO:], s16[HALO:]) * inv - ext[HALO:]


def pool_fwd(l, p, w_bd, scale):
    T = p.shape[0]
    W = MIX_W
    nC = T // SEQ_CHUNK

    def body(l_ref, u_ref, w_ref, sc_ref, z_ref):
        for c in range(nC):
            pooled = _pooled_chunk(u_ref, c)
            mixed = _dot(pooled.astype(BF16), w_ref[...])
            z_ref[c * SEQ_CHUNK:(c + 1) * SEQ_CHUNK, :] = (mixed * sc_ref[0:1, :]).astype(BF16)

    grid_spec = pltpu.PrefetchScalarGridSpec(
        num_scalar_prefetch=1, grid=(1,),
        in_specs=[pl.BlockSpec((T, W), lambda i, l: (0, 4)),
                  pl.BlockSpec((None, W, W), lambda i, l: (l[0], 0, 0)),
                  pl.BlockSpec((None, 8, W), lambda i, l: (l[0], 0, 0))],
        out_specs=pl.BlockSpec((T, W), lambda i, l: (0, 0)),
    )
    return pl.pallas_call(
        body, name="pool_fwd", grid_spec=grid_spec,
        out_shape=jax.ShapeDtypeStruct((T, W), BF16),
        compiler_params=_cparams(("arbitrary",)),
    )(l, p, w_bd, scale)


def pool_bwd(l, p, dz, w_bd, scale):
    T = p.shape[0]
    W = MIX_W
    nC = T // SEQ_CHUNK

    def body(l_ref, u_ref, dz_ref, w_ref, sc_ref, du_ref, dw_ref, dsc_ref, e_sc, dpl_sc):
        dw = jnp.zeros((W, W), F32)
        dsc = jnp.zeros((1, W), F32)
        for c in range(nC):
            t0 = c * SEQ_CHUNK
            pooled = _pooled_chunk(u_ref, c).astype(BF16)
            mixed = _dot(pooled, w_ref[...])
            dz_c = dz_ref[t0:t0 + SEQ_CHUNK, :]
            dsc += _colsum(dz_c * mixed)
            dmixed = (dz_c * sc_ref[0:1, :]).astype(BF16)
            dw += _dot_tn(pooled, dmixed)
            dpooled = _dot_nt(dmixed, w_ref[...])
            _, inv = _pool_consts(SEQ_CHUNK, t0)
            dpl_sc[t0:t0 + SEQ_CHUNK, :] = dpooled
            e_sc[t0:t0 + SEQ_CHUNK, :] = dpooled * inv
        for c in range(nC):
            t0 = c * SEQ_CHUNK
            ext = _rows_with_tail(lambda s, n: e_sc[s:s + n, :], c, nC, W)
            s2 = ext + _shift_up(ext, 1)
            s4 = s2 + _shift_up(s2, 2)
            s8 = s4 + _shift_up(s4, 4)
            s16 = s8 + _shift_up(s8, 8)
            lane, _ = _pool_consts(SEQ_CHUNK, t0)
            n = SEQ_CHUNK
            du = _pick_window(lane, s2[:n], s4[:n], s8[:n], s16[:n]) - dpl_sc[t0:t0 + SEQ_CHUNK, :]
            du_ref[t0:t0 + SEQ_CHUNK, :] = du.astype(BF16)
        dw_ref[...] = dw
        dsc_ref[...] = jnp.concatenate([dsc, jnp.zeros((7, W), F32)], axis=0)

    grid_spec = pltpu.PrefetchScalarGridSpec(
        num_scalar_prefetch=1, grid=(1,),
        in_specs=[pl.BlockSpec((T, W), lambda i, l: (0, 4)),
                  pl.BlockSpec((T, W), lambda i, l: (0, 0)),
                  pl.BlockSpec((None, W, W), lambda i, l: (l[0], 0, 0)),
                  pl.BlockSpec((None, 8, W), lambda i, l: (l[0], 0, 0))],
        out_specs=[pl.BlockSpec((T, W), lambda i, l: (0, 0)), pl.BlockSpec((W, W), lambda i, l: (0, 0)),
                   pl.BlockSpec((8, W), lambda i, l: (0, 0))],
        scratch_shapes=[pltpu.VMEM((T, W), F32), pltpu.VMEM((T, W), F32)],
    )
    return pl.pallas_call(
        body, name="pool_bwd", grid_spec=grid_spec,
        out_shape=[jax.ShapeDtypeStruct((T, W), BF16), jax.ShapeDtypeStruct((W, W), F32),
                   jax.ShapeDtypeStruct((8, W), F32)],
        compiler_params=_cparams(("arbitrary",)),
    )(l, p, dz, w_bd, scale)


def _s5_disc(lre, lim, ldt):
    lr = jnp.minimum(lre, DT_LAMBDA_RE_MAX)
    dt = jnp.exp(ldt)
    mag = jnp.exp(lr * dt)
    a_re = mag * jnp.cos(lim * dt)
    a_im = mag * jnp.sin(lim * dt)
    den = lr * lr + lim * lim
    nr = a_re - 1.0
    return a_re, a_im, (nr * lr + a_im * lim) / den, (a_im * lr - nr * lim) / den


def _bd_mask(shape, row_blk, col_blk):
    r = lax.broadcasted_iota(jnp.int32, shape, 0) >> (row_blk.bit_length() - 1)
    c = lax.broadcasted_iota(jnp.int32, shape, 1) >> (col_blk.bit_length() - 1)
    return r == c


def s5_params(lam, b_t, c_t):
    L = lam.shape[0]

    def body(lam_ref, b_ref, c_ref, a_ref, bbd_ref, cbd_ref):
        a_re, a_im, f_re, f_im = _s5_disc(lam_ref[0:1, :], lam_ref[1:2, :], lam_ref[2:3, :])
        a_ref[...] = jnp.concatenate([a_re, a_im, jnp.zeros((6, SSM_W), F32)], axis=0)
        mb = _bd_mask((MIX_W, SSM_W), SSM_GROUP, SSM_STATE)
        bbd_ref[0] = jnp.where(mb, f_re * b_ref[0] - f_im * b_ref[1], 0.0).astype(BF16)
        bbd_ref[1] = jnp.where(mb, f_re * b_ref[1] + f_im * b_ref[0], 0.0).astype(BF16)
        mc = _bd_mask((SSM_W, MIX_W), SSM_STATE, SSM_GROUP)
        cbd_ref[0] = jnp.where(mc, c_ref[0], 0.0).astype(BF16)
        cbd_ref[1] = jnp.where(mc, c_ref[1], 0.0).astype(BF16)

    return pl.pallas_call(
        body, name="s5_params", grid=(L,),
        in_specs=[pl.BlockSpec((None, 8, SSM_W), lambda l: (l, 0, 0)),
                  pl.BlockSpec((None, 2, MIX_W, SSM_W), lambda l: (l, 0, 0, 0)),
                  pl.BlockSpec((None, 2, SSM_W, MIX_W), lambda l: (l, 0, 0, 0))],
        out_specs=[pl.BlockSpec((None, 8, SSM_W), lambda l: (l, 0, 0)),
                   pl.BlockSpec((None, 2, MIX_W, SSM_W), lambda l: (l, 0, 0, 0)),
                   pl.BlockSpec((None, 2, SSM_W, MIX_W), lambda l: (l, 0, 0, 0))],
        out_shape=[jax.ShapeDtypeStruct((L, 8, SSM_W), F32),
                   jax.ShapeDtypeStruct((L, 2, MIX_W, SSM_W), BF16),
                   jax.ShapeDtypeStruct((L, 2, SSM_W, MIX_W), BF16)],
        compiler_params=_cparams(("arbitrary",)),
    )(lam, b_t, c_t)


def s5_params_bwd(lam, b_t, gb, gc, da):
    L = lam.shape[0]

    def body(lam_ref, b_ref, gb_ref, gc_ref, da_ref, dlam_ref, db_ref, dc_ref, dgrp_ref):
        lre, lim, ldt = lam_ref[0:1, :], lam_ref[1:2, :], lam_ref[2:3, :]
        (a_re, a_im, f_re, f_im), vjp = jax.vjp(_s5_disc, lre, lim, ldt)
        mb = _bd_mask((MIX_W, SSM_W), SSM_GROUP, SSM_STATE)
        gbr = jnp.where(mb, gb_ref[0], 0.0)
        gbi = jnp.where(mb, gb_ref[1], 0.0)
        df_re = _colsum(gbr * b_ref[0] + gbi * b_ref[1])
        df_im = _colsum(gbi * b_ref[0] - gbr * b_ref[1])
        db_ref[0] = f_re * gbr + f_im * gbi
        db_ref[1] = f_re * gbi - f_im * gbr
        mc = _bd_mask((SSM_W, MIX_W), SSM_STATE, SSM_GROUP)
        dc_ref[0] = jnp.where(mc, gc_ref[0], 0.0)
        dc_ref[1] = jnp.where(mc, gc_ref[1], 0.0)
        dlre, dlim, dldt = vjp((da_ref[0:1, :], da_ref[1:2, :], df_re, df_im))
        dl = jnp.concatenate([dlre, dlim, dldt, jnp.zeros((5, SSM_W), F32)], axis=0)
        dlam_ref[...] = dl
        grp = jnp.where(_bd_mask((SSM_W, 128), SSM_STATE, 1), 1.0, 0.0)
        dgrp_ref[...] = jnp.dot(dl, grp, preferred_element_type=F32, precision=lax.Precision.HIGHEST)

    vec = pl.BlockSpec((None, 8, SSM_W), lambda l: (l, 0, 0))
    bsp = pl.BlockSpec((None, 2, MIX_W, SSM_W), lambda l: (l, 0, 0, 0))
    csp = pl.BlockSpec((None, 2, SSM_W, MIX_W), lambda l: (l, 0, 0, 0))
    return pl.pallas_call(
        body, name="s5_params_bwd", grid=(L,),
        in_specs=[vec, bsp, bsp, csp, vec],
        out_specs=[vec, bsp, csp, pl.BlockSpec((None, 8, 128), lambda l: (l, 0, 0))],
        out_shape=[jax.ShapeDtypeStruct((L, 8, SSM_W), F32),
                   jax.ShapeDtypeStruct((L, 2, MIX_W, SSM_W), F32),
                   jax.ShapeDtypeStruct((L, 2, SSM_W, MIX_W), F32),
                   jax.ShapeDtypeStruct((L, 8, 128), F32)],
        compiler_params=_cparams(("arbitrary",)),
    )(lam, b_t, gb, gc, da)


def s5_bu(l, p, b_bd, tm=512):
    T = p.shape[0]

    def body(l_ref, u_ref, b_ref, bu_ref):
        u = u_ref[...].astype(BF16)
        bu_ref[0] = _dot(u, b_ref[0])
        bu_ref[1] = _dot(u, b_ref[1])

    grid_spec = pltpu.PrefetchScalarGridSpec(
        num_scalar_prefetch=1, grid=(T // tm,),
        in_specs=[pl.BlockSpec((tm, MIX_W), lambda i, l: (i, 3)),
                  pl.BlockSpec((None, 2, MIX_W, SSM_W), lambda i, l: (l[0], 0, 0, 0))],
        out_specs=pl.BlockSpec((2, tm, SSM_W), lambda i, l: (0, i, 0)),
    )
    return pl.pallas_call(
        body, name="s5_bu", grid_spec=grid_spec,
        out_shape=jax.ShapeDtypeStruct((2, T, SSM_W), F32),
        compiler_params=_cparams(("arbitrary",)),
    )(l, p, b_bd)


def s5_scan(l, avec, xs, reverse):
    T = xs.shape[1]
    CH = SEQ_CHUNK
    nC = T // CH
    LW = 128
    n_steps = CH.bit_length() - 1

    def body(l_ref, a_ref, x_ref, s_ref):
        ar = a_ref[0:1, :]
        ai = -a_ref[1:2, :] if reverse else a_ref[1:2, :]
        pows = [(ar, ai)]
        for _ in range(n_steps - 1):
            r, i = pows[-1]
            pows.append((r * r - i * i, 2.0 * r * i))
        row = lax.broadcasted_iota(jnp.int32, (CH, LW), 0)

        def local_scan(re, im):
            for k in range(n_steps):
                d = 1 << k
                pr, pi = pows[k]
                if reverse:
                    keep = row < CH - d
                    sr, si = _shift_up(re, d), _shift_up(im, d)
                else:
                    keep = row >= d
                    sr, si = _shift_down(re, d), _shift_down(im, d)
                sr = jnp.where(keep, sr, 0.0)
                si = jnp.where(keep, si, 0.0)
                re, im = re + pr * sr - pi * si, im + pr * si + pi * sr
            return re, im

        edge = CH - 1 if reverse else 0
        pw_re, pw_im = local_scan(jnp.where(row == edge, ar, 0.0), jnp.where(row == edge, ai, 0.0))
        last = 0 if reverse else CH - 1

        def chunk(c, carry):
            cr, ci = carry
            cc = nC - 1 - c if reverse else c
            t0 = pl.multiple_of(cc * CH, CH)
            re, im = local_scan(x_ref[0, pl.ds(t0, CH), :], x_ref[1, pl.ds(t0, CH), :])
            re2 = re + pw_re * cr - pw_im * ci
            im2 = im + pw_re * ci + pw_im * cr
            s_ref[0, pl.ds(t0, CH), :] = re2
            s_ref[1, pl.ds(t0, CH), :] = im2
            return re2[last:last + 1, :], im2[last:last + 1, :]

        lax.fori_loop(0, nC, chunk, (jnp.zeros((1, LW), F32), jnp.zeros((1, LW), F32)))

    grid_spec = pltpu.PrefetchScalarGridSpec(
        num_scalar_prefetch=1, grid=(SSM_W // LW,),
        in_specs=[pl.BlockSpec((None, 8, LW), lambda g, l: (l[0], 0, g)),
                  pl.BlockSpec((2, T, LW), lambda g, l: (0, 0, g))],
        out_specs=pl.BlockSpec((2, T, LW), lambda g, l: (0, 0, g)),
    )
    return pl.pallas_call(
        body, name="s5_scan_rev" if reverse else "s5_scan_fwd", grid_spec=grid_spec,
        out_shape=jax.ShapeDtypeStruct((2, T, SSM_W), F32),
        compiler_params=_cparams(("arbitrary",)),
    )(l, avec, xs)


_GELU_C = 0.7978845608028654
_GELU_K = 0.044715


def _s5_y(u, s_ref, c_ref, d_row):
    y = _dot(s_ref[0].astype(BF16), c_ref[0]) - _dot(s_ref[1].astype(BF16), c_ref[1])
    return y + d_row * u


def s5_out(l, p, s, c_bd, ssm_d, tm=512):
    T = p.shape[0]

    def body(l_ref, u_ref, s_ref, c_ref, d_ref, yg_ref):
        y = _s5_y(u_ref[...], s_ref, c_ref, d_ref[0:1, :])
        th = jnp.tanh(_GELU_C * (y + _GELU_K * y * y * y))
        yg_ref[...] = (0.5 * y * (1.0 + th)).astype(BF16)

    grid_spec = pltpu.PrefetchScalarGridSpec(
        num_scalar_prefetch=1, grid=(T // tm,),
        in_specs=[pl.BlockSpec((tm, MIX_W), lambda i, l: (i, 3)),
                  pl.BlockSpec((2, tm, SSM_W), lambda i, l: (0, i, 0)),
                  pl.BlockSpec((None, 2, SSM_W, MIX_W), lambda i, l: (l[0], 0, 0, 0)),
                  pl.BlockSpec((None, 8, MIX_W), lambda i, l: (l[0], 0, 0))],
        out_specs=pl.BlockSpec((tm, MIX_W), lambda i, l: (i, 0)),
    )
    return pl.pallas_call(
        body, name="s5_out", grid_spec=grid_spec,
        out_shape=jax.ShapeDtypeStruct((T, MIX_W), BF16),
        compiler_params=_cparams(("arbitrary",)),
    )(l, p, s, c_bd, ssm_d)


def s5_bwd_y(l, p, s, dyg, c_bd, ssm_d, tm=512):
    T = p.shape[0]

    def body(l_ref, u_ref, s_ref, dyg_ref, c_ref, d_ref, ds_ref, du_ref, gc_ref, dd_ref):
        @pl.when(pl.program_id(0) == 0)
        def _():
            gc_ref[...] = jnp.zeros_like(gc_ref)
            dd_ref[...] = jnp.zeros_like(dd_ref)

        u = u_ref[...]
        y = _s5_y(u, s_ref, c_ref, d_ref[0:1, :])
        inner = _GELU_C * (y + _GELU_K * y * y * y)
        th = jnp.tanh(inner)
        dgelu = 0.5 * (1.0 + th) + 0.5 * y * (1.0 - th * th) * (_GELU_C * (1.0 + 3.0 * _GELU_K * y * y))
        dy = dyg_ref[...] * dgelu
        dd_ref[0:1, :] += _colsum(dy * u)
        du_ref[...] = dy * d_ref[0:1, :]
        dyb = dy.astype(BF16)
        ds_ref[0] = _dot_nt(dyb, c_ref[0])
        ds_ref[1] = -_dot_nt(dyb, c_ref[1])
        gc_ref[0] += _dot_tn(s_ref[0].astype(BF16), dyb)
        gc_ref[1] -= _dot_tn(s_ref[1].astype(BF16), dyb)

    grid_spec = pltpu.PrefetchScalarGridSpec(
        num_scalar_prefetch=1, grid=(T // tm,),
        in_specs=[pl.BlockSpec((tm, MIX_W), lambda i, l: (i, 3)),
                  pl.BlockSpec((2, tm, SSM_W), lambda i, l: (0, i, 0)),
                  pl.BlockSpec((tm, MIX_W), lambda i, l: (i, 0)),
                  pl.BlockSpec((None, 2, SSM_W, MIX_W), lambda i, l: (l[0], 0, 0, 0)),
                  pl.BlockSpec((None, 8, MIX_W), lambda i, l: (l[0], 0, 0))],
        out_specs=[pl.BlockSpec((2, tm, SSM_W), lambda i, l: (0, i, 0)),
                   pl.BlockSpec((tm, MIX_W), lambda i, l: (i, 0)),
                   pl.BlockSpec((2, SSM_W, MIX_W), lambda i, l: (0, 0, 0)),
                   pl.BlockSpec((8, MIX_W), lambda i, l: (0, 0))],
    )
    return pl.pallas_call(
        body, name="s5_bwd_y", grid_spec=grid_spec,
        out_shape=[jax.ShapeDtypeStruct((2, T, SSM_W), F32), jax.ShapeDtypeStruct((T, MIX_W), F32),
                   jax.ShapeDtypeStruct((2, SSM_W, MIX_W), F32), jax.ShapeDtypeStruct((8, MIX_W), F32)],
        compiler_params=_cparams(("arbitrary",)),
    )(l, p, s, dyg, c_bd, ssm_d)


def s5_bwd_u(l, p, lam_s, du_skip, b_bd, tm=512):
    T = p.shape[0]

    def body(l_ref, u_ref, ls_ref, dus_ref, b_ref, du_ref, gb_ref):
        @pl.when(pl.program_id(0) == 0)
        def _():
            gb_ref[...] = jnp.zeros_like(gb_ref)

        u = u_ref[...].astype(BF16)
        lr = ls_ref[0].astype(BF16)
        li = ls_ref[1].astype(BF16)
        gb_ref[0] += _dot_tn(u, lr)
        gb_ref[1] += _dot_tn(u, li)
        du_ref[...] = (dus_ref[...] + _dot_nt(lr, b_ref[0]) + _dot_nt(li, b_ref[1])).astype(BF16)

    grid_spec = pltpu.PrefetchScalarGridSpec(
        num_scalar_prefetch=1, grid=(T // tm,),
        in_specs=[pl.BlockSpec((tm, MIX_W), lambda i, l: (i, 3)),
                  pl.BlockSpec((2, tm, SSM_W), lambda i, l: (0, i, 0)),
                  pl.BlockSpec((tm, MIX_W), lambda i, l: (i, 0)),
                  pl.BlockSpec((None, 2, MIX_W, SSM_W), lambda i, l: (l[0], 0, 0, 0))],
        out_specs=[pl.BlockSpec((tm, MIX_W), lambda i, l: (i, 0)),
                   pl.BlockSpec((2, MIX_W, SSM_W), lambda i, l: (0, 0, 0))],
    )
    return pl.pallas_call(
        body, name="s5_bwd_u", grid_spec=grid_spec,
        out_shape=[jax.ShapeDtypeStruct((T, MIX_W), BF16), jax.ShapeDtypeStruct((2, MIX_W, SSM_W), F32)],
        compiler_params=_cparams(("arbitrary",)),
    )(l, p, lam_s, du_skip, b_bd)


def s5_bwd_a(s, lam_s):
    T = s.shape[1]
    nC = T // SEQ_CHUNK
    LW = 128

    def body(s_ref, ls_ref, da_ref):
        dre = jnp.zeros((1, LW), F32)
        dim = jnp.zeros((1, LW), F32)
        for c in range(nC):
            t0 = c * SEQ_CHUNK
            sr = _shift_down(_rows_with_lead(lambda a, n: s_ref[0, a:a + n, :], c, LW), 1)[HALO:]
            si = _shift_down(_rows_with_lead(lambda a, n: s_ref[1, a:a + n, :], c, LW), 1)[HALO:]
            lr = ls_ref[0, t0:t0 + SEQ_CHUNK, :]
            li = ls_ref[1, t0:t0 + SEQ_CHUNK, :]
            dre += _colsum(sr * lr + si * li)
            dim += _colsum(sr * li - si * lr)
        da_ref[...] = jnp.concatenate([dre, dim, jnp.zeros((6, LW), F32)], axis=0)

    blk = pl.BlockSpec((2, T, LW), lambda g: (0, 0, g))
    return pl.pallas_call(
        body, name="s5_bwd_a", grid=(SSM_W // LW,),
        in_specs=[blk, blk],
        out_specs=pl.BlockSpec((8, LW), lambda g: (0, g)),
        out_shape=jax.ShapeDtypeStruct((8, SSM_W), F32),
        compiler_params=_cparams(("arbitrary",)),
    )(s, lam_s)


SB_BLK = 128
SB_SCALE = SB_HEAD ** -0.5


def _split_bf16(x):
    hi = x.astype(BF16)
    return hi, (x - hi.astype(F32)).astype(BF16)


def _dot_split(x, tri):
    hi, lo = _split_bf16(x)
    return _dot(hi, tri) + _dot(lo, tri)


SB_SLABS = MIX_W // SB_BLK
SB_STACK = 2 * SB_SLABS * SB_BLK
SB_PAIR = 2 * SB_BLK


def _sb_valid(r0, c0):
    row = (lax.broadcasted_iota(jnp.int32, (SB_STACK, SB_BLK), 0) & (SB_BLK - 1)) + r0
    col = lax.broadcasted_iota(jnp.int32, (SB_STACK, SB_BLK), 1) + c0
    return col < row


def _sb_stack(ref, r0, scale):
    lane = lax.broadcasted_iota(jnp.int32, (SB_BLK, SB_BLK), 1)
    parts = []
    for s in range(SB_SLABS):
        blk = ref[pl.ds(r0, SB_BLK), s * SB_BLK:(s + 1) * SB_BLK] * scale
        parts += [jnp.where(lane < SB_HEAD, blk, 0.0), jnp.where(lane >= SB_HEAD, blk, 0.0)]
    return jnp.concatenate(parts, axis=0).astype(BF16)


def _sb_rows_nt(stack, ref, c0):
    return jnp.concatenate(
        [_dot_nt(stack[s * SB_PAIR:(s + 1) * SB_PAIR], ref[pl.ds(c0, SB_BLK), s * SB_BLK:(s + 1) * SB_BLK].astype(BF16))
         for s in range(SB_SLABS)], axis=0)


def _sb_wide(stack, s):
    return jnp.concatenate([stack[s * SB_PAIR:s * SB_PAIR + SB_BLK], stack[s * SB_PAIR + SB_BLK:(s + 1) * SB_PAIR]],
                           axis=1)


def _sb_logits(q_stack, k_ref, c0, valid):
    z = _sb_rows_nt(q_stack, k_ref, c0)
    sp = jnp.log(1.0 + jnp.exp(-jnp.abs(z)))
    ls_pos = jnp.minimum(z, 0.0) - sp
    lk = jnp.minimum(-z, 0.0) - sp
    if valid is not None:
        lk = jnp.where(valid, lk, 0.0)
    return z, ls_pos, lk


def _tri(lower):
    r = lax.broadcasted_iota(jnp.int32, (SB_BLK, SB_BLK), 0)
    c = lax.broadcasted_iota(jnp.int32, (SB_BLK, SB_BLK), 1)
    return jnp.where(r > c if lower else r < c, 1.0, 0.0).astype(BF16)


def sb_fwd(p):
    T = p.shape[0]
    W = MIX_W
    nB = T // SB_BLK

    def body(q_ref, k_ref, v_ref, o_ref, tot_ref, acc_sc):
        tri = _tri(True)

        def qblock(i, _):
            r0 = pl.multiple_of(i * SB_BLK, SB_BLK)
            q = _sb_stack(q_ref, r0, SB_SCALE)
            acc_sc[...] = jnp.zeros_like(acc_sc)

            def kblocks(c0s, run, valid):
                parts = [_sb_logits(q, k_ref, c0, valid) for c0 in c0s]
                for c0, (_, ls_pos, lk) in zip(c0s, parts):
                    a = jnp.exp(ls_pos + _dot_split(lk, tri) + run)
                    if valid is not None:
                        a = jnp.where(valid, a, 0.0)
                    a = a.astype(BF16)
                    v = _sb_stack(v_ref, c0, 1.0)
                    for s in range(SB_SLABS):
                        acc_sc[:, s * SB_BLK:(s + 1) * SB_BLK] += _dot(_sb_wide(a, s), v[s * SB_PAIR:(s + 1) * SB_PAIR])
                    run = run + jnp.sum(lk, axis=1, keepdims=True)
                return run

            def key_block(jj):
                return pl.multiple_of((i - jj) * SB_BLK, SB_BLK)

            run = kblocks([r0], jnp.zeros((SB_STACK, 1), F32), _sb_valid(0, 0))
            odd = i & 1
            run = lax.cond(odd == 1, lambda r: kblocks([key_block(1)], r, None), lambda r: r, run)
            total = lax.fori_loop(
                0, i >> 1, lambda t, r: kblocks([key_block(1 + odd + 2 * t), key_block(2 + odd + 2 * t)], r, None), run)
            o_ref[pl.ds(r0, SB_BLK), :] = acc_sc[...].astype(BF16)
            tot_ref[pl.ds(pl.multiple_of(i * SB_STACK, SB_STACK), SB_STACK), :] = jnp.broadcast_to(total, (SB_STACK, SB_BLK))
            return 0

        lax.fori_loop(0, nB, qblock, 0)

    return pl.pallas_call(
        body, name="sb_fwd", grid=(1,),
        in_specs=[pl.BlockSpec((T, W), lambda i: (0, 5)), pl.BlockSpec((T, W), lambda i: (0, 6)),
                  pl.BlockSpec((T, W), lambda i: (0, 7))],
        out_specs=[pl.BlockSpec((T, W), lambda i: (0, 0)), pl.BlockSpec((nB * SB_STACK, SB_BLK), lambda i: (0, 0))],
        out_shape=[jax.ShapeDtypeStruct((T, W), BF16), jax.ShapeDtypeStruct((nB * SB_STACK, SB_BLK), F32)],
        scratch_shapes=[pltpu.VMEM((SB_BLK, W), F32)],
        compiler_params=_cparams(("arbitrary",)),
    )(p, p, p)


def sb_bwd(p, do, tot):
    T = p.shape[0]
    W = MIX_W
    nB = T // SB_BLK

    def body(q_ref, k_ref, v_ref, do_ref, tot_ref, dqkv_ref, dq_sc, dk_sc, dv_sc):
        tri_gt = _tri(True)
        tri_lt = _tri(False)
        dq_sc[...] = jnp.zeros_like(dq_sc)
        dk_sc[...] = jnp.zeros_like(dk_sc)
        dv_sc[...] = jnp.zeros_like(dv_sc)
        zcol = jnp.zeros((SB_STACK, 1), F32)

        def qblock(i, _):
            r0 = pl.multiple_of(i * SB_BLK, SB_BLK)
            q = _sb_stack(q_ref, r0, SB_SCALE)
            dob = _sb_stack(do_ref, r0, 1.0)

            total = tot_ref[pl.ds(pl.multiple_of(i * SB_STACK, SB_STACK), SB_STACK), 0:1]

            def kblocks(c0s, carry, valid):
                pre, seen = carry
                parts = [_sb_logits(q, k_ref, c0, valid) for c0 in c0s]
                for c0, (z, ls_pos, lk) in zip(c0s, parts):
                    seen = seen + jnp.sum(lk, axis=1, keepdims=True)
                    a = jnp.exp(ls_pos + _dot_split(lk, tri_gt) + (total - seen))
                    if valid is not None:
                        a = jnp.where(valid, a, 0.0)
                    dlw = _sb_rows_nt(dob, v_ref, c0) * a
                    g = pre + _dot_split(dlw, tri_lt)
                    sig = _sigmoid(z)
                    dz = dlw * (1.0 - sig) - g * sig
                    if valid is not None:
                        dz = jnp.where(valid, dz, 0.0)
                    dz = dz.astype(BF16)
                    ab = a.astype(BF16)
                    km = _sb_stack(k_ref, c0, 1.0)
                    for s in range(SB_SLABS):
                        pair = slice(s * SB_PAIR, (s + 1) * SB_PAIR)
                        ls = slice(s * SB_BLK, (s + 1) * SB_BLK)
                        dk_sc[pl.ds(c0, SB_BLK), ls] += _dot_tn(dz[pair], q[pair])
                        dv_sc[pl.ds(c0, SB_BLK), ls] += _dot_tn(ab[pair], dob[pair])
                        dq_sc[pl.ds(r0, SB_BLK), ls] += _dot(_sb_wide(dz, s), km[pair])
                    pre = pre + jnp.sum(dlw, axis=1, keepdims=True)
                return pre, seen

            def key_block(j):
                return pl.multiple_of(j * SB_BLK, SB_BLK)

            carry = lax.fori_loop(
                0, i >> 1, lambda t, c: kblocks([key_block(2 * t), key_block(2 * t + 1)], c, None), (zcol, zcol))
            carry = lax.cond((i & 1) == 1, lambda c: kblocks([key_block(i - 1)], c, None), lambda c: c, carry)
            kblocks([r0], carry, _sb_valid(0, 0))
            return 0

        lax.fori_loop(0, nB, qblock, 0)
        dqkv_ref[:, 0:W] = (dq_sc[...] * SB_SCALE).astype(BF16)
        dqkv_ref[:, W:2 * W] = dk_sc[...].astype(BF16)
        dqkv_ref[:, 2 * W:3 * W] = dv_sc[...].astype(BF16)

    return pl.pallas_call(
        body, name="sb_bwd", grid=(1,),
        in_specs=[pl.BlockSpec((T, W), lambda i: (0, 5)), pl.BlockSpec((T, W), lambda i: (0, 6)),
                  pl.BlockSpec((T, W), lambda i: (0, 7)), pl.BlockSpec((T, W), lambda i: (0, 0)),
                  pl.BlockSpec((nB * SB_STACK, SB_BLK), lambda i: (0, 0))],
        out_specs=pl.BlockSpec((T, 3 * W), lambda i: (0, 0)),
        out_shape=jax.ShapeDtypeStruct((T, 3 * W), BF16),
        scratch_shapes=[pltpu.VMEM((T, W), F32), pltpu.VMEM((T, W), F32), pltpu.VMEM((T, W), F32)],
        compiler_params=_cparams(("arbitrary",)),
    )(p, p, p, do, tot)


def _dot_cols(a, w_ref):
    return jnp.concatenate([_dot(a, w_ref[j]) for j in range(N_DEV)], axis=1)


def _dot_cols_nt(dy, w_ref):
    n = w_ref.shape[2]
    out = _dot_nt(dy[:, 0:n], w_ref[0])
    for j in range(1, N_DEV):
        out += _dot_nt(dy[:, j * n:(j + 1) * n], w_ref[j])
    return out


def _acc_cols_tn(acc_ref, a, dy):
    n = acc_ref.shape[2]
    for j in range(N_DEV):
        acc_ref[j] += _dot_tn(a, dy[:, j * n:(j + 1) * n])


def _merge_branches(za_ref, yg_ref, z_ref, o_ref, gate_refs, wc_ref, wglu_ref, wp_ref, ws_ref):
    D = D_MODEL
    glu = _dot_cols(yg_ref[...], wglu_ref)
    glu_a, sg = glu[:, :D], _sigmoid(glu[:, D:])
    ys = [_dot_cols(za_ref[...], wc_ref), glu_a * sg, _dot_cols(z_ref[...], wp_ref), _dot_cols(o_ref[...], ws_ref)]
    gs = [_sigmoid(g[...]) for g in gate_refs]
    merged = gs[0] * ys[0] + gs[1] * ys[1] + gs[2] * ys[2] + gs[3] * ys[3]
    return ys, gs, glu_a, sg, merged


def _merge_specs(tm, D):
    W = MIX_W
    br = pl.BlockSpec((tm, W), lambda i, l: (i, 0))
    gates = [pl.BlockSpec((tm, D), functools.partial(lambda i, l, b: (i, 2 + b), b=b)) for b in range(4)]
    wsm = pl.BlockSpec((None, N_DEV, W, D // N_DEV), lambda i, l: (l[0], 0, 0, 0))
    weights = [wsm, pl.BlockSpec((None, N_DEV, W, 2 * D // N_DEV), lambda i, l: (l[0], 0, 0, 0)), wsm, wsm,
               pl.BlockSpec((None, D, D), lambda i, l: (l[0], 0, 0))]
    return [br] * 4 + gates, weights


def merge_fwd(l, p, za, yg, z, o, x, pv, wc, wglu, wp, ws, wo, tm=512):
    T, D = x.shape
    tm = min(tm, T)

    def body(l_ref, za_ref, yg_ref, z_ref, o_ref, g0, g1, g2, g3, x_ref, pv_ref,
             wc_ref, wglu_ref, wp_ref, ws_ref, wo_ref, xn_ref, m_ref):
        _, _, _, _, merged = _merge_branches(za_ref, yg_ref, z_ref, o_ref, (g0, g1, g2, g3),
                                             wc_ref, wglu_ref, wp_ref, ws_ref)
        m = _dot(merged.astype(BF16), wo_ref[...])
        m_ref[...] = m
        xn_ref[...] = _postnorm_res(x_ref[...], m, pv_ref, 1.0)

    acts, weights = _merge_specs(tm, D)
    tile = pl.BlockSpec((tm, D), lambda i, l: (i, 0))
    grid_spec = pltpu.PrefetchScalarGridSpec(
        num_scalar_prefetch=1, grid=(T // tm,),
        in_specs=acts + [tile, pl.BlockSpec((8, D), lambda i, l: (0, 0))] + weights,
        out_specs=[tile, tile],
    )
    return pl.pallas_call(
        body, name="merge_fwd", grid_spec=grid_spec,
        out_shape=[jax.ShapeDtypeStruct((T, D), F32), jax.ShapeDtypeStruct((T, D), F32)],
        compiler_params=_cparams(("arbitrary",)),
    )(l, za, yg, z, o, p, p, p, p, x, pv, wc, wglu, wp, ws, wo)


def merge_bwd(l, p, za, yg, z, o, m, dxn, pv, wc, wglu, wp, ws, wo, tm=256):
    T, D = m.shape
    W = MIX_W
    tm = min(tm, T)
    ni = T // tm

    def body(l_ref, za_ref, yg_ref, z_ref, o_ref, g0, g1, g2, g3, m_ref, dxn_ref, pv_ref,
             wc_ref, wglu_ref, wp_ref, ws_ref, wo_ref,
             dza_ref, dyg_ref, dz_ref, do_ref, dg_ref, pg_ref, gwc_ref, gwglu_ref, gwp_ref, gws_ref, gwo_ref,
             awc, awglu, awp, aws, awo):
        i = pl.program_id(0)

        @pl.when(i == 0)
        def _():
            pg_ref[...] = jnp.zeros_like(pg_ref)
            for a in (awc, awglu, awp, aws, awo):
                a[...] = jnp.zeros_like(a)

        ys, gs, glu_a, sg, merged = _merge_branches(za_ref, yg_ref, z_ref, o_ref, (g0, g1, g2, g3),
                                                    wc_ref, wglu_ref, wp_ref, ws_ref)
        dm = _postnorm_bwd(dxn_ref[...], m_ref[...], pv_ref, pg_ref, 1.0).astype(BF16)
        awo[...] += _dot_tn(merged.astype(BF16), dm)
        dmerged = _dot_nt(dm, wo_ref[...])
        for b in range(4):
            dg_ref[:, b * D:(b + 1) * D] = (dmerged * ys[b] * gs[b] * (1.0 - gs[b])).astype(BF16)
        dya = (dmerged * gs[0]).astype(BF16)
        _acc_cols_tn(awc, za_ref[...], dya)
        dza_ref[...] = _dot_cols_nt(dya, wc_ref)
        dyc = (dmerged * gs[2]).astype(BF16)
        _acc_cols_tn(awp, z_ref[...], dyc)
        dz_ref[...] = _dot_cols_nt(dyc, wp_ref)
        dyd = (dmerged * gs[3]).astype(BF16)
        _acc_cols_tn(aws, o_ref[...], dyd)
        do_ref[...] = _dot_cols_nt(dyd, ws_ref)
        dyb = dmerged * gs[1]
        dglu = jnp.concatenate([dyb * sg, dyb * glu_a * sg * (1.0 - sg)], axis=1).astype(BF16)
        _acc_cols_tn(awglu, yg_ref[...], dglu)
        dyg_ref[...] = _dot_cols_nt(dglu, wglu_ref)

        @pl.when(i == ni - 1)
        def _():
            gwc_ref[...] = awc[...].astype(BF16)
            gwglu_ref[...] = awglu[...].astype(BF16)
            gwp_ref[...] = awp[...].astype(BF16)
            gws_ref[...] = aws[...].astype(BF16)
            gwo_ref[...] = awo[...].astype(BF16)

    acts, weights = _merge_specs(tm, D)
    tile = pl.BlockSpec((tm, D), lambda i, l: (i, 0))
    br = pl.BlockSpec((tm, W), lambda i, l: (i, 0))
    full = lambda *s: pl.BlockSpec(s, lambda i, l: (0,) * len(s))
    sm, glu_s = (N_DEV, W, D // N_DEV), (N_DEV, W, 2 * D // N_DEV)
    grid_spec = pltpu.PrefetchScalarGridSpec(
        num_scalar_prefetch=1, grid=(ni,),
        in_specs=acts + [tile, tile, pl.BlockSpec((8, D), lambda i, l: (0, 0))] + weights,
        out_specs=[br, br, br, br, pl.BlockSpec((tm, 4 * D), lambda i, l: (i, 0)), full(8, D),
                   full(*sm), full(*glu_s), full(*sm), full(*sm), full(D, D)],
        scratch_shapes=[pltpu.VMEM(sm, F32), pltpu.VMEM(glu_s, F32), pltpu.VMEM(sm, F32),
                        pltpu.VMEM(sm, F32), pltpu.VMEM((D, D), F32)],
    )
    f32br = jax.ShapeDtypeStruct((T, W), F32)
    return pl.pallas_call(
        body, name="merge_bwd", grid_spec=grid_spec,
        out_shape=[f32br, f32br, f32br, f32br, jax.ShapeDtypeStruct((T, 4 * D), BF16),
                   jax.ShapeDtypeStruct((8, D), F32),
                   jax.ShapeDtypeStruct(sm, BF16), jax.ShapeDtypeStruct(glu_s, BF16),
                   jax.ShapeDtypeStruct(sm, BF16), jax.ShapeDtypeStruct(sm, BF16),
                   jax.ShapeDtypeStruct((D, D), BF16)],
        compiler_params=_cparams(("arbitrary",)),
    )(l, za, yg, z, o, p, p, p, p, m, dxn, pv, wc, wglu, wp, ws, wo)


def dp_assemble(d_conv, d_ssm, d_pool, d_qkv, d_gates, tm=512):
    T = d_conv.shape[0]
    W = MIX_W

    def body(c_ref, s_ref, p_ref, q_ref, g_ref, dp_ref):
        dp_ref[:, 0:3 * W] = c_ref[...]
        dp_ref[:, 3 * W:4 * W] = s_ref[...]
        dp_ref[:, 4 * W:5 * W] = p_ref[...]
        dp_ref[:, 5 * W:8 * W] = q_ref[...]
        dp_ref[:, GATE_OFF:] = g_ref[...]

    row = lambda w: pl.BlockSpec((tm, w), lambda i: (i, 0))
    return pl.pallas_call(
        body, name="dp_assemble", grid=(T // tm,),
        in_specs=[row(3 * W), row(W), row(W), row(3 * W), row(4 * D_MODEL)],
        out_specs=row(IN_COLS),
        out_shape=jax.ShapeDtypeStruct((T, IN_COLS), BF16),
        compiler_params=_cparams(("arbitrary",)),
    )(d_conv, d_ssm, d_pool, d_qkv, d_gates)


def loss_head(y, target, tm=512):
    T, D = y.shape

    def body(y_ref, t_ref, dy_ref, loss_ref):
        @pl.when(pl.program_id(0) == 0)
        def _():
            loss_ref[...] = jnp.zeros_like(loss_ref)

        err = y_ref[...] - t_ref[...]
        dy_ref[...] = err * (1.0 / D)
        loss_ref[...] += jnp.sum(err * err) * (0.5 / D)

    tile = pl.BlockSpec((tm, D), lambda i: (i, 0))
    return pl.pallas_call(
        body, name="loss_head", grid=(T // tm,),
        in_specs=[tile, tile],
        out_specs=[tile, pl.BlockSpec((8, 128), lambda i: (0, 0))],
        out_shape=[jax.ShapeDtypeStruct((T, D), F32), jax.ShapeDtypeStruct((8, 128), F32)],
        compiler_params=_cparams(("arbitrary",)),
    )(y, target)


def cast_layer(ld, items):
    def body(ld_ref, *refs):
        n = len(refs) // 2
        for src, dst in zip(refs[:n], refs[n:]):
            dst[...] = src[...].astype(BF16)

    def shard(w, k):
        return w.shape[1:] if k is None else w.shape[2:]

    def in_spec(w, k):
        sh = shard(w, k)
        if k is None:
            return pl.BlockSpec((None,) + sh, lambda i, ld, n=len(sh): (ld[0],) + (0,) * n)
        return pl.BlockSpec((None, None) + sh, lambda i, ld, n=len(sh), k=k: (ld[0], k) + (0,) * n)

    def out_spec(w, k):
        sh = shard(w, k)
        return pl.BlockSpec((None, None) + sh, lambda i, ld, n=len(sh): (0, ld[1]) + (0,) * n)

    grid_spec = pltpu.PrefetchScalarGridSpec(
        num_scalar_prefetch=1, grid=(1,),
        in_specs=[in_spec(w, k) for w, k in items], out_specs=[out_spec(w, k) for w, k in items])
    return pl.pallas_call(
        body, name="cast_layer", grid_spec=grid_spec,
        out_shape=[jax.ShapeDtypeStruct((1, N_DEV) + shard(w, k), BF16) for w, k in items],
        compiler_params=_cparams(("arbitrary",)),
    )(ld, *[w for w, _ in items])


def place_own(dev, a):
    def body(dev_ref, a_ref, o_ref):
        o_ref[...] = a_ref[...]

    grid_spec = pltpu.PrefetchScalarGridSpec(
        num_scalar_prefetch=1, grid=(1,),
        in_specs=[pl.BlockSpec(a.shape, lambda i, dev: (0, 0))],
        out_specs=pl.BlockSpec((None,) + a.shape, lambda i, dev: (dev[0], 0, 0)))
    return pl.pallas_call(
        body, name="place_own", grid_spec=grid_spec,
        out_shape=jax.ShapeDtypeStruct((N_DEV,) + a.shape, a.dtype),
        compiler_params=_cparams(("arbitrary",)),
    )(dev, a)


def _silu(x):
    return x * _sigmoid(x)


def ada_fwd(c_all, w_ada, b_cols):
    L, D, n = w_ada.shape

    def body(c_ref, w_ref, b_ref, o_ref):
        c_act = _silu(c_ref[...]).astype(BF16)
        o_ref[...] = _dot(c_act, w_ref[...].astype(BF16)) + b_ref[...]

    return pl.pallas_call(
        body, name="ada_fwd", grid=(L,),
        in_specs=[pl.BlockSpec((N_DEV, D), lambda l: (0, 0)), pl.BlockSpec((None, D, n), lambda l: (l, 0, 0)),
                  pl.BlockSpec((None, 1, n), lambda l: (l, 0, 0))],
        out_specs=pl.BlockSpec((None, N_DEV, n), lambda l: (l, 0, 0)),
        out_shape=jax.ShapeDtypeStruct((L, N_DEV, n), F32),
        compiler_params=_cparams(("arbitrary",)),
    )(c_all, w_ada, b_cols)


def _adamw(w, g, m, v):
    m = ADAM_B1 * m + (1.0 - ADAM_B1) * g
    v = ADAM_B2 * v + (1.0 - ADAM_B2) * (g * g)
    m_hat = m / (1.0 - ADAM_B1 ** ADAM_STEP)
    v_hat = v / (1.0 - ADAM_B2 ** ADAM_STEP)
    delta = -ADAM_LR * (m_hat / (jnp.sqrt(v_hat) + ADAM_EPS) + ADAM_WD * w)
    return delta, m, v


def ada_update(c_all, dada_cols, w, m, v, rb=256):
    L, D, n = w.shape

    def body(c_ref, d_ref, w_ref, m_ref, v_ref, g_ref, dl_ref, nm_ref, nv_ref):
        c_act = _silu(c_ref[...]).astype(BF16)
        g = _dot_tn(c_act, d_ref[...].astype(BF16))
        g_ref[...] = g
        dl_ref[...], nm_ref[...], nv_ref[...] = _adamw(w_ref[...], g, m_ref[...], v_ref[...])

    blk = pl.BlockSpec((None, rb, n), lambda l, i: (l, i, 0))
    out = jax.ShapeDtypeStruct((L, D, n), F32)
    return pl.pallas_call(
        body, name="ada_update", grid=(L, D // rb),
        in_specs=[pl.BlockSpec((N_DEV, rb), lambda l, i: (0, i)),
                  pl.BlockSpec((None, N_DEV, n), lambda l, i: (l, 0, 0)), blk, blk, blk],
        out_specs=[blk, blk, blk, blk], out_shape=[out, out, out, out],
        compiler_params=_cparams(("arbitrary", "arbitrary")),
    )(c_all, dada_cols, w, m, v)


SUM_UPDATE_RECV_BYTES = 12 * 1024 * 1024


def sum_update(dev, first, recvs, owns, w, m, v, prev=None, after=None):
    n_slots, R, C = w.shape
    S = len(recvs)
    assert len(owns) == S and first + S <= n_slots
    rb_max = SUM_UPDATE_RECV_BYTES // (S * N_DEV * C * 2)
    rb = max(r for r in range(8, R + 1, 8) if R % r == 0 and (r <= rb_max or r == 8))
    last = R // rb - 1
    n_prev = 0 if prev is None else 4
    extra = list(prev or ()) + ([] if after is None else [after])

    def body(dev_ref, *refs):
        r_refs, o_refs = refs[:S], refs[S:2 * S]
        w_ref, m_ref, v_ref = refs[2 * S:2 * S + 3]
        g_ref, dl_ref, nm_ref, nv_ref = refs[2 * S + 3 + len(extra):]
        me = dev_ref[0]
        for s in range(S):
            @pl.when(pl.program_id(0) == s)
            def _(s=s):
                g = jnp.zeros((rb, C), F32)
                for d in range(N_DEV):
                    g += jnp.where(me == d, o_refs[s][...], r_refs[s][d]).astype(F32)
                g_ref[...] = g
                dl_ref[...], nm_ref[...], nv_ref[...] = _adamw(w_ref[...], g, m_ref[...], v_ref[...])

    def row(sl, i, s):
        return jnp.where(sl == s, i, jnp.where(sl < s, 0, last))

    def rspec(s):
        return pl.BlockSpec((N_DEV, rb, C), lambda sl, i, dev: (0, row(sl, i, s), 0))

    def ospec(s):
        return pl.BlockSpec((None, rb, C), lambda sl, i, dev: (dev[0], row(sl, i, s), 0))

    blk = pl.BlockSpec((None, rb, C), lambda sl, i, dev: (first + sl, i, 0))
    out = jax.ShapeDtypeStruct((n_slots, R, C), F32)
    grid_spec = pltpu.PrefetchScalarGridSpec(
        num_scalar_prefetch=1, grid=(S, R // rb),
        in_specs=[rspec(s) for s in range(S)] + [ospec(s) for s in range(S)] + [blk, blk, blk] + [ANY] * len(extra),
        out_specs=[blk, blk, blk, blk],
    )
    n_in = 1 + 2 * S + 3
    return pl.pallas_call(
        body, name="sum_update", grid_spec=grid_spec, out_shape=[out, out, out, out],
        input_output_aliases={n_in + i: i for i in range(n_prev)},
        compiler_params=_cparams(("arbitrary", "arbitrary")),
    )(dev, *recvs, *owns, w, m, v, *extra)


def small_sum(gathered):
    _, R, C = gathered.shape

    def body(g_ref, o_ref):
        acc = g_ref[0]
        for d in range(1, N_DEV):
            acc += g_ref[d]
        o_ref[...] = acc

    return pl.pallas_call(
        body, name="small_sum", grid=(1,),
        in_specs=[pl.BlockSpec((N_DEV, R, C), lambda i: (0, 0, 0))],
        out_specs=pl.BlockSpec((R, C), lambda i: (0, 0)),
        out_shape=jax.ShapeDtypeStruct((R, C), F32),
        compiler_params=_cparams(("arbitrary",)),
    )(gathered)


def small_update(w, g, m, v):
    def body(w_ref, g_ref, m_ref, v_ref, dl_ref, nm_ref, nv_ref):
        dl_ref[...], nm_ref[...], nv_ref[...] = _adamw(w_ref[...], g_ref[...], m_ref[...], v_ref[...])

    blk = pl.BlockSpec(w.shape, lambda i: (0, 0))
    out = jax.ShapeDtypeStruct(w.shape, F32)
    return pl.pallas_call(
        body, name="small_update", grid=(1,),
        in_specs=[blk] * 4, out_specs=[blk] * 3, out_shape=[out] * 3,
        compiler_params=_cparams(("arbitrary",)),
    )(w, g, m, v)


MESH = pl.DeviceIdType.MESH
ANY = pl.BlockSpec(memory_space=pl.ANY)


def _coords():
    return lax.axis_index("x"), lax.axis_index("y"), lax.axis_index("c")


def _dev_index(x, y, c):
    return 4 * x + 2 * y + c


def _at_dev(ref, p, dev):
    return ref.at[(slice(None),) * p + (dev,)]


def all_gather(arrays, ps):
    n = len(arrays)

    def body(*refs):
        ins, outs = refs[:n], refs[n:2 * n]
        send_sems, recv_sems, local_sems = refs[2 * n:]
        x, y, c = _coords()
        me, sibling = (x, y, c), (x, y, 1 - c)
        chips = [(1 - x, y), (x, 1 - y), (1 - x, 1 - y)]

        def copy(a, k, block, to, src=None):
            dst = _at_dev(outs[a], ps[a], _dev_index(*block))
            return pltpu.make_async_remote_copy(
                src_ref=dst if src is None else src, dst_ref=dst,
                send_sem=send_sems.at[a, k], recv_sem=recv_sems.at[a, k], device_id=to, device_id_type=MESH)

        mine = [pltpu.make_async_copy(ins[a], _at_dev(outs[a], ps[a], _dev_index(*me)), local_sems.at[a])
                for a in range(n)]
        for cp in mine:
            cp.start()
        first = []
        for a in range(n):
            first.append(copy(a, 0, me, sibling, src=ins[a]))
            first += [copy(a, 1 + j, me, (*chip, c), src=ins[a]) for j, chip in enumerate(chips)]
        for cp in first:
            cp.start()
        passed = []
        for j, chip in enumerate(chips):
            for a in range(n):
                copy(a, 1 + j, (*chip, c), me).wait_recv()
                fwd = copy(a, 4 + j, (*chip, c), sibling)
                fwd.start()
                passed.append(fwd)
        for a in range(n):
            copy(a, 0, sibling, me).wait_recv()
            for j, chip in enumerate(chips):
                copy(a, 4 + j, (*chip, 1 - c), me).wait_recv()
        for cp in first + passed:
            cp.wait_send()
        for cp in mine:
            cp.wait()

    out_shape = [jax.ShapeDtypeStruct(a.shape[:p] + (N_DEV,) + a.shape[p:], a.dtype) for a, p in zip(arrays, ps)]
    return pl.pallas_call(
        body, name="all_gather", in_specs=[ANY] * n, out_specs=[ANY] * n, out_shape=out_shape,
        scratch_shapes=[pltpu.SemaphoreType.DMA((n, 7)), pltpu.SemaphoreType.DMA((n, 7)),
                        pltpu.SemaphoreType.DMA((n,))],
        compiler_params=pltpu.CompilerParams(has_side_effects=True),
    )(*arrays)


HBM = pl.BlockSpec(memory_space=pltpu.HBM)
SEM = pl.BlockSpec(memory_space=pltpu.SEMAPHORE)
EFFECT = pltpu.SideEffectType.DATAFLOW_SIDE_EFFECTING


def _peers(x, y, c):
    out = []
    for k in range(1, N_DEV):
        out.append((1 - x if k & 4 else x, 1 - y if k & 2 else y, 1 - c if k & 1 else c))
    return out


def _exchange_plan(n):
    def plan(refs, x, y, c):
        blocks, lands = refs[:n], refs[n:2 * n]
        me = _dev_index(x, y, c)
        moves = []
        for peer in _peers(x, y, c):
            q = _dev_index(*peer)
            moves += [(blocks[a].at[q], lands[a].at[me], peer, lands[a].at[q]) for a in range(n)]
        return moves
    return plan


def _gather_plan(ps, second):
    def plan(refs, x, y, c):
        me, sibling = (x, y, c), (x, y, 1 - c)
        chips = [(1 - x, y), (x, 1 - y), (1 - x, 1 - y)]
        if second:
            trips = [((*ch, c), sibling, (*ch, 1 - c)) for ch in chips]
        else:
            trips = [(me, sibling, sibling)] + [(me, (*ch, c), (*ch, c)) for ch in chips]
        moves = []
        for sent, to, arriving in trips:
            for ref, p in zip(refs, ps):
                blk = _at_dev(ref, p, _dev_index(*sent))
                moves.append((blk, blk, to, _at_dev(ref, p, _dev_index(*arriving))))
        return moves
    return plan


def copies_start(name, plan, n_moves, arrays, carry):
    n = len(arrays)

    def body(*refs):
        sems = refs[n + 1:n + 1 + 2 * n_moves]
        moves = plan(refs[:n], *_coords())
        assert len(moves) == n_moves
        for i, (src, dst, to, _) in enumerate(moves):
            pltpu.make_async_remote_copy(src_ref=src, dst_ref=dst, send_sem=sems[i], recv_sem=sems[n_moves + i],
                                         device_id=to, device_id_type=MESH).start()

    operands = [pltpu.with_memory_space_constraint(a, pltpu.HBM) for a in list(arrays) + [carry]]
    outs = pl.pallas_call(
        body, name=name,
        out_shape=[pltpu.SemaphoreType.DMA(())] * (2 * n_moves) + [pltpu.HBM(a.shape, a.dtype) for a in operands],
        in_specs=[HBM] * (n + 1), out_specs=[SEM] * (2 * n_moves) + [HBM] * (n + 1),
        input_output_aliases={i: 2 * n_moves + i for i in range(n + 1)},
        compiler_params=pltpu.CompilerParams(has_side_effects=EFFECT),
    )(*operands)
    return outs[:n_moves], outs[n_moves:2 * n_moves], outs[2 * n_moves:-1], outs[-1]


def copies_wait(name, plan, send_sems, recv_sems, arrays, after):
    n, n_moves = len(arrays), len(send_sems)

    def body(*refs):
        sems = refs[n:n + 2 * n_moves]
        for i, (src, _, to, arriving) in enumerate(plan(refs[:n], *_coords())):
            cp = pltpu.make_async_remote_copy(src_ref=src, dst_ref=arriving, send_sem=sems[i],
                                              recv_sem=sems[n_moves + i], device_id=to, device_id_type=MESH)
            cp.wait_send()
            cp.wait_recv()

    return pl.pallas_call(
        body, name=name,
        out_shape=[pltpu.HBM(a.shape, a.dtype) for a in arrays],
        in_specs=[HBM] * n + [SEM] * (2 * n_moves) + [ANY], out_specs=[HBM] * n,
        input_output_aliases={i: i for i in range(n)},
        compiler_params=pltpu.CompilerParams(has_side_effects=EFFECT),
    )(*arrays, *send_sems, *recv_sems, after)


WEIGHT_NAMES = ("w_ada", "b_ada", "g_pre", "g_post", "w_ff_in", "w_ff_out", "w_in", "conv_w", "w_conv_out",
                "lam_re", "lam_im", "log_dt", "ssm_b_re", "ssm_b_im", "ssm_c_re", "ssm_c_im", "ssm_d", "w_glu",
                "w_pool", "pool_scale", "w_pool_out", "w_sb_out", "w_out")
BIG_NAMES = ("w_ff_in", "w_ff_out", "w_in", "w_conv_out", "w_glu", "w_pool_out", "w_sb_out", "w_out")
SMALL_NAMES = ("b_ada", "g_pre", "g_post", "conv_w", "lam_re", "lam_im", "log_dt", "ssm_b_re", "ssm_b_im",
               "ssm_c_re", "ssm_c_im", "ssm_d", "w_pool", "pool_scale")
PACK_LANES = 128
PACK_ROWS = 8


def _pack(arrays):
    flat = jnp.concatenate([a.reshape(-1) for a in arrays])
    unit = PACK_LANES * PACK_ROWS
    flat = jnp.pad(flat, (0, -flat.shape[0] % unit))
    return flat.reshape(-1, PACK_LANES)


def _unpack(pack, shapes):
    flat = pack.reshape(-1)
    out, off = [], 0
    for s in shapes:
        n = 1
        for d in s:
            n *= d
        out.append(flat[off:off + n].reshape(s))
        off += n
    return out


def _pad_rows(a, rows=8):
    return jnp.pad(a, ((0, 0), (0, rows - a.shape[1]), (0, 0)))


def _tile_b(b):
    L = b.shape[0]
    return jnp.tile(b.transpose(0, 3, 1, 2).reshape(L, SSM_GROUP, SSM_W), (1, SSM_GROUPS, 1))


def _tile_c(c):
    L = c.shape[0]
    return jnp.tile(c.transpose(0, 3, 1, 2).reshape(L, SSM_STATE, MIX_W), (1, SSM_GROUPS, 1))


def _step(x, c, target, W, M, V):
    T, D = x.shape[1], x.shape[2]
    L = W["w_ada"].shape[0]
    x = x[0]
    target = target[0]
    ax, ay, ac = _coords()
    dev = _dev_index(ax, ay, ac)
    n_ada = W["w_ada"].shape[2]

    dev_s = jnp.reshape(dev, (1,)).astype(jnp.int32)
    items = ([(W["w_ff_in"], 0), (W["w_ff_in"], 1), (W["w_ff_out"], 0), (W["w_ff_out"], 1)]
             + [(W[k], None) for k in BIG_NAMES[2:]])
    bufs = [list(cast_layer(jnp.concatenate([jnp.array([l], jnp.int32), dev_s]), items)) for l in range(L)]
    ffn1_w, mixer_w, ffn2_w = (0, 2), (4, 5, 6, 7, 8, 9), (1, 3)
    all_w = tuple(range(len(items)))

    def gather_start(tag, second, l, idx, carry):
        plan = _gather_plan((1,) * len(idx), second)
        n_moves = (3 if second else 4) * len(idx)
        s_sem, r_sem, arrs, carry = copies_start(f"gather_{'b' if second else 'a'}_start_{tag}", plan, n_moves,
                                                 [bufs[l][i] for i in idx], carry)
        for i, a in zip(idx, arrs):
            bufs[l][i] = a
        return (plan, s_sem, r_sem), carry

    def gather_wait(tag, second, l, idx, flight, after):
        arrs = copies_wait(f"gather_{'b' if second else 'a'}_wait_{tag}", *flight, [bufs[l][i] for i in idx], after)
        for i, a in zip(idx, arrs):
            bufs[l][i] = a

    def gather_finish(tag, l, idx, flight, after, carry):
        gather_wait(tag, False, l, idx, flight, after)
        flight, carry = gather_start(tag, True, l, idx, carry)
        gather_wait(tag, True, l, idx, flight, carry)
        return carry

    first = []
    for g, idx in enumerate((ffn1_w, mixer_w, ffn2_w)):
        flight, x = gather_start(f"0_{g}", False, 0, idx, x)
        first.append(flight)

    gathered = all_gather([W["g_pre"], W["g_post"], W["conv_w"], c], [0, 0, 0, 0])
    g_pre = gathered[0].transpose(1, 2, 0, 3).reshape(L, N_SUB, D)
    g_post = gathered[1].transpose(1, 2, 0, 3).reshape(L, N_SUB, D)
    conv_w = _pad_rows(gathered[2].transpose(1, 2, 0, 3).reshape(L, 3, MIX_W))
    c_all = gathered[3].reshape(N_DEV, D)

    b_cols = lax.dynamic_slice_in_dim(W["b_ada"], dev * n_ada, n_ada, axis=1)[:, None, :]
    ada_cols = ada_fwd(c_all, W["w_ada"], b_cols)
    ada_all = all_gather([ada_cols], [0])[0]
    ada = lax.dynamic_index_in_dim(ada_all, dev, axis=2, keepdims=False)
    ada = ada.transpose(1, 0, 2).reshape(L, N_SUB, 3, D)
    zeros = jnp.zeros((L, N_SUB, D), F32)
    pv_all = jnp.stack([g_pre, ada[:, :, 0], ada[:, :, 1], g_post, ada[:, :, 2], zeros, zeros, zeros], axis=2)

    lam = jnp.stack([W["lam_re"].reshape(L, SSM_W), W["lam_im"].reshape(L, SSM_W),
                     jnp.repeat(W["log_dt"], SSM_STATE, axis=1)], axis=1)
    lam = _pad_rows(lam)
    b_t = jnp.stack([_tile_b(W["ssm_b_re"]), _tile_b(W["ssm_b_im"])], axis=1)
    c_t = jnp.stack([_tile_c(W["ssm_c_re"]), _tile_c(W["ssm_c_im"])], axis=1)
    avec, b_bd, c_bd = s5_params(lam, b_t, c_t)
    ssm_d = _pad_rows(W["ssm_d"][:, None, :])
    pool_scale = _pad_rows(W["pool_scale"][:, None, :])
    eye4 = jnp.eye(len(POOL_WINDOWS), dtype=F32)
    w_bd = jnp.einsum("lgcd,gh->lgchd", W["w_pool"], eye4).reshape(L, MIX_W, MIX_W).astype(BF16)

    x = gather_finish("0_0", 0, ffn1_w, first[0], pv_all, x)

    def ffn_weights(l, k):
        b = bufs[l]
        return b[k].reshape(1, 1, 2, 4, D, FF_BLK), b[2 + k].reshape(1, 1, 4, FF_BLK, D)

    def mixer_weights(l):
        b = bufs[l]
        return b[4], b[5], b[6], b[7], b[8], b[9].reshape(1, D, D)

    l0 = jnp.array([0], jnp.int32)
    k0 = jnp.array([0, 0], jnp.int32)
    saved = []
    for l in range(L):
        li = jnp.array([l], jnp.int32)
        nxt = l + 1 < L
        if nxt:
            flight, x = gather_start(f"{l + 1}", False, l + 1, all_w, x)
        x0 = x
        ab0, f0, x1 = ffn_fwd(k0, x0, pv_all[l, 0], *ffn_weights(l, 0))
        if l == 0:
            x1 = gather_finish("0_1", 0, mixer_w, first[1], x1, x1)
        wg_in, wg_conv, wg_glu, wg_pool, wg_sb, wg_out = mixer_weights(l)
        p = mix_in_fwd(l0, x1, pv_all[l, 1], wg_in)
        za = conv_fwd(li, p, conv_w)
        z = pool_fwd(li, p, w_bd, pool_scale)
        s = s5_scan(li, avec, s5_bu(li, p, b_bd), False)
        yg = s5_out(li, p, s, c_bd, ssm_d)
        o, sb_tot = sb_fwd(p)
        x2, m = merge_fwd(l0, p, za, yg, z, o, x1, pv_all[l, 1], wg_conv, wg_glu, wg_pool, wg_sb, wg_out)
        if l == 0:
            x2 = gather_finish("0_2", 0, ffn2_w, first[2], x2, x2)
        if nxt:
            gather_wait(f"{l + 1}", False, l + 1, all_w, flight, x2)
            flight, x2 = gather_start(f"{l + 1}", True, l + 1, all_w, x2)
        ab1, f1, x = ffn_fwd(k0, x2, pv_all[l, 2], *ffn_weights(l, 1))
        if nxt:
            gather_wait(f"{l + 1}", True, l + 1, all_w, flight, x)
        saved.append((x0, ab0, f0, x1, p, za, z, s, yg, o, sb_tot, m, x2, ab1, f1))

    dx, loss_blk = loss_head(x, target)
    loss = lax.psum(loss_blk[0, 0], ("x", "y", "c"))

    n_blocks = 10
    ffn2_g, mixer_g, ffn1_g = (1, 3), (4, 5, 6, 7, 8, 9), (0, 2)
    recvs, owns, in_flight = [[None] * n_blocks for _ in range(L)], [[None] * n_blocks for _ in range(L)], []

    def exchange_start(tag, layer, idx, blocks, carry):
        n = len(idx)
        plan = _exchange_plan(n)
        arrays = list(blocks) + [lax.empty(a.shape, a.dtype) for a in blocks]
        s_sem, r_sem, arrays, carry = copies_start(f"exchange_start_{tag}", plan, (N_DEV - 1) * n, arrays, carry)
        in_flight.append((tag, layer, idx, plan, s_sem, r_sem, arrays))
        return carry

    def settle(after, upto):
        for flight in [f for f in in_flight if f[1] >= upto]:
            in_flight.remove(flight)
            tag, layer, idx, plan, s_sem, r_sem, arrays = flight
            arrays = copies_wait(f"exchange_wait_{tag}", plan, s_sem, r_sem, arrays, after)
            for j, i in enumerate(idx):
                owns[layer][i], recvs[layer][i] = arrays[j], arrays[len(idx) + j]

    pgs = [None] * L
    small = {k: [None] * L for k in ("conv_w", "w_bd", "pool_scale", "ssm_d", "gb", "gc", "da")}
    for l in reversed(range(L)):
        li = jnp.array([l], jnp.int32)
        x0, ab0, f0, x1, p, za, z, s, yg, o, sb_tot, m, x2, ab1, f1 = saved[l]
        wg_in, wg_conv, wg_glu, wg_pool, wg_sb, wg_out = mixer_weights(l)
        dab, h, df, dx, pg2 = ffn_bwd_act(k0, dx, x2, f1, pv_all[l, 2], ab1, *ffn_weights(l, 1))
        g_in1, g_out1 = ffn_bwd_w(h, df, ab1, dab)
        dx = exchange_start(f"{l}_ffn2", l, ffn2_g,
                            [g_in1.reshape(N_DEV, D, FF_BLK), g_out1.reshape(N_DEV, D_FF // N_DEV, D)], dx)
        (dza, dyg, dz, do, dgates, pg1m, g_conv, g_glu, g_pool, g_sb, g_wo) = merge_bwd(
            l0, p, za, yg, z, o, m, dx, pv_all[l, 1], wg_conv, wg_glu, wg_pool, wg_sb, wg_out)
        d_conv, small["conv_w"][l] = conv_bwd(li, p, dza, conv_w)
        d_pool, small["w_bd"][l], small["pool_scale"][l] = pool_bwd(li, p, dz, w_bd, pool_scale)
        ds, du_skip, small["gc"][l], small["ssm_d"][l] = s5_bwd_y(li, p, s, dyg, c_bd, ssm_d)
        lam_s = s5_scan(li, avec, ds, True)
        d_ssm, small["gb"][l] = s5_bwd_u(li, p, lam_s, du_skip, b_bd)
        small["da"][l] = s5_bwd_a(s, lam_s)
        d_qkv = sb_bwd(p, do, sb_tot)
        dp = dp_assemble(d_conv, d_ssm, d_pool, d_qkv, dgates)
        dx, h, pg1i = mix_in_bwd_act(l0, dp, dx, x1, pv_all[l, 1], wg_in)
        g_win = matmul_tn(h, dp, IN_BLK)
        settle(dx, l + 1)
        dx = exchange_start(f"{l}_mixer", l, mixer_g,
                            [g_win, g_conv, g_glu, g_pool, g_sb, g_wo.reshape(N_DEV, D // N_DEV, D)], dx)
        dab, h, df, dx, pg0 = ffn_bwd_act(k0, dx, x0, f0, pv_all[l, 0], ab0, *ffn_weights(l, 0))
        g_in0, g_out0 = ffn_bwd_w(h, df, ab0, dab)
        pgs[l] = jnp.stack([pg0, pg1m + pg1i, pg2])
        dx = exchange_start(f"{l}_ffn1", l, ffn1_g,
                            [g_in0.reshape(N_DEV, D, FF_BLK), g_out0.reshape(N_DEV, D_FF // N_DEV, D)], dx)

    dlam, db_t, dc_t, dldt = s5_params_bwd(lam, b_t, jnp.stack(small["gb"]), jnp.stack(small["gc"]),
                                           jnp.stack(small["da"]))
    pg = jnp.stack(pgs)
    d_ada = jnp.stack([pg[:, :, PV_SHIFT], pg[:, :, PV_SCALE], pg[:, :, PV_GATE]], axis=2).reshape(L, N_SUB * 3 * D)
    db = db_t.reshape(L, 2, SSM_GROUPS, SSM_GROUP, SSM_GROUPS, SSM_STATE)
    db = jnp.einsum("lrghgp->lrgph", db)
    dc = dc_t.reshape(L, 2, SSM_GROUPS, SSM_STATE, SSM_GROUPS, SSM_GROUP)
    dc = jnp.einsum("lrgpgh->lrghp", dc)
    d_wpool = jnp.einsum("lgcgd->lgcd", jnp.stack(small["w_bd"]).reshape(L, 4, 64, 4, 64))
    contrib = {
        "b_ada": d_ada, "g_pre": pg[:, :, PV_GPRE], "g_post": pg[:, :, PV_GPOST],
        "conv_w": jnp.stack(small["conv_w"])[:, :3], "lam_re": dlam[:, 0].reshape(L, SSM_GROUPS, SSM_STATE),
        "lam_im": dlam[:, 1].reshape(L, SSM_GROUPS, SSM_STATE), "log_dt": dldt[:, 2, :SSM_GROUPS],
        "ssm_b_re": db[:, 0], "ssm_b_im": db[:, 1], "ssm_c_re": dc[:, 0], "ssm_c_im": dc[:, 1],
        "ssm_d": jnp.stack(small["ssm_d"])[:, 0], "w_pool": d_wpool,
        "pool_scale": jnp.stack(small["pool_scale"])[:, 0],
    }
    contrib_shapes = [contrib[k].shape for k in SMALL_NAMES]

    big_idx = {"w_ff_in": (0, 1), "w_ff_out": (2, 3), "w_in": (4,), "w_conv_out": (5,), "w_glu": (6,),
               "w_pool_out": (7,), "w_sb_out": (8,), "w_out": (9,)}

    def big(name, layers, prev, after=None):
        idx = big_idx[name]
        flat = (-1,) + W[name].shape[-2:]
        return sum_update(dev_s, layers[0] * len(idx), [recvs[l][i] for l in layers for i in idx],
                          [owns[l][i] for l in layers for i in idx],
                          W[name].reshape(flat), M[name].reshape(flat), V[name].reshape(flat), prev, after)

    partial = {}

    def partial_updates(names, after):
        for name in names:
            if L > 1:
                partial[name] = big(name, list(range(1, L)), None, after)
                after = partial[name][0]
        return after

    pack_buf = [place_own(dev_s, _pack([contrib[k] for k in SMALL_NAMES]))]
    plan_a, plan_b = _gather_plan((0,), False), _gather_plan((0,), True)
    s_sem, r_sem, pack_buf, dx = copies_start("small_gather_a_start", plan_a, 4, pack_buf, dx)
    after = partial_updates(BIG_NAMES[:1], dx)
    pack_buf = copies_wait("small_gather_a_wait", plan_a, s_sem, r_sem, pack_buf, after)
    s_sem, r_sem, pack_buf, dx = copies_start("small_gather_b_start", plan_b, 3, pack_buf, dx)
    after = partial_updates(BIG_NAMES[1:], dx)
    pack_all = copies_wait("small_gather_b_wait", plan_b, s_sem, r_sem, pack_buf, after)[0]
    total = dict(zip(SMALL_NAMES, _unpack(small_sum(pack_all), contrib_shapes)))
    d_ada_all = pack_all.reshape(N_DEV, -1)[:, :L * N_SUB * 3 * D].reshape(N_DEV, L, N_SUB * 3 * D)
    dada_cols = lax.dynamic_slice_in_dim(d_ada_all, dev * n_ada, n_ada, axis=2).transpose(1, 0, 2)
    n_g = D // N_DEV
    grads = {}
    for k in SMALL_NAMES:
        g = total[k]
        if k in ("g_pre", "g_post"):
            g = lax.dynamic_slice_in_dim(g, dev * n_g, n_g, axis=2)
        elif k == "conv_w":
            g = lax.dynamic_slice_in_dim(g, dev * (MIX_W // N_DEV), MIX_W // N_DEV, axis=2)
        grads[k] = g

    delta, new_m, new_v = {}, {}, {}
    shapes = [W[k].shape for k in SMALL_NAMES]
    dl, nm, nv = small_update(_pack([W[k] for k in SMALL_NAMES]), _pack([grads[k] for k in SMALL_NAMES]),
                              _pack([M[k] for k in SMALL_NAMES]), _pack([V[k] for k in SMALL_NAMES]))
    for k, a, b, cc in zip(SMALL_NAMES, _unpack(dl, shapes), _unpack(nm, shapes), _unpack(nv, shapes)):
        delta[k], new_m[k], new_v[k] = a, b, cc
    grads["w_ada"], delta["w_ada"], new_m["w_ada"], new_v["w_ada"] = ada_update(
        c_all, dada_cols, W["w_ada"], M["w_ada"], V["w_ada"])

    settle(new_m["w_ada"], 0)
    for name in BIG_NAMES:
        outs = big(name, [0], partial.get(name))
        grads[name], delta[name], new_m[name], new_v[name] = [o.reshape(W[name].shape) for o in outs]

    return (loss, dx[None], *[grads[k] for k in WEIGHT_NAMES], *[delta[k] for k in WEIGHT_NAMES],
            *[new_m[k] for k in WEIGHT_NAMES], *[new_v[k] for k in WEIGHT_NAMES])


def kernel(x, c, w_ada, b_ada, g_pre, g_post, w_ff_in, w_ff_out, w_in, conv_w, w_conv_out, lam_re, lam_im, log_dt, ssm_b_re, ssm_b_im, ssm_c_re, ssm_c_im, ssm_d, w_glu, w_pool, pool_scale, w_pool_out, w_sb_out, w_out, loss_target, m_w_ada, m_b_ada, m_g_pre, m_g_post, m_w_ff_in, m_w_ff_out, m_w_in, m_conv_w, m_w_conv_out, m_lam_re, m_lam_im, m_log_dt, m_ssm_b_re, m_ssm_b_im, m_ssm_c_re, m_ssm_c_im, m_ssm_d, m_w_glu, m_w_pool, m_pool_scale, m_w_pool_out, m_w_sb_out, m_w_out, v_w_ada, v_b_ada, v_g_pre, v_g_post, v_w_ff_in, v_w_ff_out, v_w_in, v_conv_w, v_w_conv_out, v_lam_re, v_lam_im, v_log_dt, v_ssm_b_re, v_ssm_b_im, v_ssm_c_re, v_ssm_c_im, v_ssm_d, v_w_glu, v_w_pool, v_pool_scale, v_w_pool_out, v_w_sb_out, v_w_out):
    w = (w_ada, b_ada, g_pre, g_post, w_ff_in, w_ff_out, w_in, conv_w, w_conv_out, lam_re, lam_im, log_dt, ssm_b_re, ssm_b_im, ssm_c_re, ssm_c_im, ssm_d, w_glu, w_pool, pool_scale, w_pool_out, w_sb_out, w_out)
    m = (m_w_ada, m_b_ada, m_g_pre, m_g_post, m_w_ff_in, m_w_ff_out, m_w_in, m_conv_w, m_w_conv_out, m_lam_re, m_lam_im, m_log_dt, m_ssm_b_re, m_ssm_b_im, m_ssm_c_re, m_ssm_c_im, m_ssm_d, m_w_glu, m_w_pool, m_pool_scale, m_w_pool_out, m_w_sb_out, m_w_out)
    v = (v_w_ada, v_b_ada, v_g_pre, v_g_post, v_w_ff_in, v_w_ff_out, v_w_in, v_conv_w, v_w_conv_out, v_lam_re, v_lam_im, v_log_dt, v_ssm_b_re, v_ssm_b_im, v_ssm_c_re, v_ssm_c_im, v_ssm_d, v_w_glu, v_w_pool, v_pool_scale, v_w_pool_out, v_w_sb_out, v_w_out)
    return _step(x, c, loss_target, dict(zip(WEIGHT_NAMES, w)), dict(zip(WEIGHT_NAMES, m)), dict(zip(WEIGHT_NAMES, v)))
```

```python
import functools

import jax
import jax.numpy as jnp
from jax import lax
from jax.experimental import pallas as pl
from jax.experimental.pallas import tpu as pltpu

F32 = jnp.float32
BF16 = jnp.bfloat16

N_DEV = 8
D_MODEL = 1024
D_FF = 2816
FF_BLK = D_FF // 4
N_SUB = 3
MIX_W = 256
IN_COLS = 6144
IN_BLK = IN_COLS // N_DEV
GATE_OFF = 2048
SSM_GROUPS, SSM_GROUP, SSM_STATE = 16, 16, 64
SSM_W = SSM_GROUPS * SSM_STATE
POOL_WINDOWS = (2, 4, 8, 16)
SB_HEAD = 64
EPS = 1e-6
DT_LAMBDA_RE_MAX = -1e-4
ADAM_LR, ADAM_B1, ADAM_B2, ADAM_EPS, ADAM_WD, ADAM_STEP = 0.001, 0.9, 0.999, 1e-08, 0.01, 10

VMEM_LIMIT = 56 * 1024 * 1024
FFN_BWD_VMEM_LIMIT = 60 * 1024 * 1024

PV_GPRE, PV_SHIFT, PV_SCALE, PV_GPOST, PV_GATE = 0, 1, 2, 3, 4


def _cparams(sem):
    return pltpu.CompilerParams(dimension_semantics=sem, vmem_limit_bytes=VMEM_LIMIT)


def _dot(a, b):
    return jnp.dot(a, b, preferred_element_type=F32)


def _dot_nt(a, b):
    return lax.dot_general(a, b, (((1,), (1,)), ((), ())), preferred_element_type=F32)


def _dot_tn(a, b):
    return lax.dot_general(a, b, (((0,), (0,)), ((), ())), preferred_element_type=F32)


def _rms(x):
    r = lax.rsqrt(jnp.mean(x * x, axis=-1, keepdims=True) + EPS)
    return x * r, r


def _rms_bwd(dn, n, r):
    return r * (dn - n * jnp.mean(dn * n, axis=-1, keepdims=True))


def _sigmoid(x):
    return 1.0 / (1.0 + jnp.exp(-x))


def _colsum(x):
    return jnp.sum(x, axis=0, keepdims=True)


def _prenorm(x, pv_ref):
    n, r = _rms(x)
    hn = n * pv_ref[PV_GPRE:PV_GPRE + 1, :]
    h = hn * (1.0 + pv_ref[PV_SCALE:PV_SCALE + 1, :]) + pv_ref[PV_SHIFT:PV_SHIFT + 1, :]
    return h, n, r, hn


def _prenorm_bwd(dh, dxn, x, pv_ref, pg_ref):
    _, n, r, hn = _prenorm(x, pv_ref)
    pg_ref[PV_SHIFT:PV_SHIFT + 1, :] += _colsum(dh)
    pg_ref[PV_SCALE:PV_SCALE + 1, :] += _colsum(dh * hn)
    dhn = dh * (1.0 + pv_ref[PV_SCALE:PV_SCALE + 1, :])
    pg_ref[PV_GPRE:PV_GPRE + 1, :] += _colsum(dhn * n)
    dn = dhn * pv_ref[PV_GPRE:PV_GPRE + 1, :]
    return dxn + _rms_bwd(dn, n, r)


def _postnorm_res(x, f, pv_ref, coef):
    nf, _ = _rms(f)
    return x + (coef * (1.0 + pv_ref[PV_GATE:PV_GATE + 1, :])) * (nf * pv_ref[PV_GPOST:PV_GPOST + 1, :])


def _postnorm_bwd(dxn, f, pv_ref, pg_ref, coef):
    nf, rf = _rms(f)
    g_post = pv_ref[PV_GPOST:PV_GPOST + 1, :]
    pg_ref[PV_GATE:PV_GATE + 1, :] += _colsum(dxn * (nf * g_post)) * coef
    dnfg = dxn * (coef * (1.0 + pv_ref[PV_GATE:PV_GATE + 1, :]))
    pg_ref[PV_GPOST:PV_GPOST + 1, :] += _colsum(dnfg * nf)
    return _rms_bwd(dnfg * g_post, nf, rf)


def ffn_fwd(lk, x, pv, wg_in, wg_out, tm=1024):
    T, D = x.shape
    tm = min(tm, T)
    nj = 4

    def body(lk_ref, x_ref, pv_ref, win_ref, wout_ref, ab_ref, f_ref, xn_ref, h_sc, acc):
        j = pl.program_id(1)

        @pl.when(j == 0)
        def _():
            h, _, _, _ = _prenorm(x_ref[...], pv_ref)
            h_sc[...] = h.astype(BF16)
            acc[...] = jnp.zeros_like(acc)

        h = h_sc[...]
        a = _dot(h, win_ref[0])
        b = _dot(h, win_ref[1])
        ab_ref[0] = a.astype(BF16)
        ab_ref[1] = b.astype(BF16)
        act = (a * _sigmoid(a) * b).astype(BF16)
        acc[...] += _dot(act, wout_ref[...])

        @pl.when(j == nj - 1)
        def _():
            f = acc[...]
            f_ref[...] = f
            xn_ref[...] = _postnorm_res(x_ref[...], f, pv_ref, 0.5)

    grid_spec = pltpu.PrefetchScalarGridSpec(
        num_scalar_prefetch=1, grid=(T // tm, nj),
        in_specs=[
            pl.BlockSpec((tm, D), lambda i, j, lk: (i, 0)),
            pl.BlockSpec((8, D), lambda i, j, lk: (0, 0)),
            pl.BlockSpec((None, None, 2, None, D, FF_BLK), lambda i, j, lk: (lk[0], lk[1], 0, j, 0, 0)),
            pl.BlockSpec((None, None, None, FF_BLK, D), lambda i, j, lk: (lk[0], lk[1], j, 0, 0)),
        ],
        out_specs=[
            pl.BlockSpec((2, None, tm, FF_BLK), lambda i, j, lk: (0, j, i, 0)),
            pl.BlockSpec((tm, D), lambda i, j, lk: (i, 0)),
            pl.BlockSpec((tm, D), lambda i, j, lk: (i, 0)),
        ],
        scratch_shapes=[pltpu.VMEM((tm, D), BF16), pltpu.VMEM((tm, D), F32)],
    )
    return pl.pallas_call(
        body, name="ffn_fwd", grid_spec=grid_spec,
        out_shape=[jax.ShapeDtypeStruct((2, nj, T, FF_BLK), BF16),
                   jax.ShapeDtypeStruct((T, D), F32), jax.ShapeDtypeStruct((T, D), F32)],
        compiler_params=_cparams(("arbitrary", "arbitrary")),
    )(lk, x, pv, wg_in, wg_out)


def ffn_bwd_act(lk, dxn, x, f, pv, ab, wg_in, wg_out, tm=512):
    T, D = x.shape
    tm = min(tm, T)
    nj = 4

    def body(lk_ref, dxn_ref, x_ref, f_ref, pv_ref, ab_ref, win_ref, wout_ref,
             dab_ref, h_ref, df_ref, dx_ref, pg_ref, dacc):
        i, j = pl.program_id(0), pl.program_id(1)

        @pl.when((i == 0) & (j == 0))
        def _():
            pg_ref[...] = jnp.zeros_like(pg_ref)

        @pl.when(j == 0)
        def _():
            df = _postnorm_bwd(dxn_ref[...], f_ref[...], pv_ref, pg_ref, 0.5)
            df_ref[...] = df.astype(BF16)
            h, _, _, _ = _prenorm(x_ref[...], pv_ref)
            h_ref[...] = h.astype(BF16)
            dacc[...] = jnp.zeros_like(dacc)

        dact = _dot_nt(df_ref[...], wout_ref[...])
        a = ab_ref[0].astype(F32)
        b = ab_ref[1].astype(F32)
        sig = _sigmoid(a)
        s = a * sig
        da = (dact * b * (sig * (1.0 + a * (1.0 - sig)))).astype(BF16)
        db = (dact * s).astype(BF16)
        dab_ref[0] = da
        dab_ref[1] = db
        dacc[...] += _dot_nt(da, win_ref[0]) + _dot_nt(db, win_ref[1])

        @pl.when(j == nj - 1)
        def _():
            dx_ref[...] = _prenorm_bwd(dacc[...], dxn_ref[...], x_ref[...], pv_ref, pg_ref)

    tile = pl.BlockSpec((tm, D), lambda i, j, lk: (i, 0))
    blk = pl.BlockSpec((2, None, tm, FF_BLK), lambda i, j, lk: (0, j, i, 0))
    grid_spec = pltpu.PrefetchScalarGridSpec(
        num_scalar_prefetch=1, grid=(T // tm, nj),
        in_specs=[tile, tile, tile, pl.BlockSpec((8, D), lambda i, j, lk: (0, 0)), blk,
                  pl.BlockSpec((None, None, 2, None, D, FF_BLK), lambda i, j, lk: (lk[0], lk[1], 0, j, 0, 0)),
                  pl.BlockSpec((None, None, None, FF_BLK, D), lambda i, j, lk: (lk[0], lk[1], j, 0, 0))],
        out_specs=[blk, tile, tile, tile, pl.BlockSpec((8, D), lambda i, j, lk: (0, 0))],
        scratch_shapes=[pltpu.VMEM((tm, D), F32)],
    )
    return pl.pallas_call(
        body, name="ffn_bwd_act", grid_spec=grid_spec,
        out_shape=[jax.ShapeDtypeStruct((2, nj, T, FF_BLK), BF16), jax.ShapeDtypeStruct((T, D), BF16),
                   jax.ShapeDtypeStruct((T, D), BF16), jax.ShapeDtypeStruct((T, D), F32),
                   jax.ShapeDtypeStruct((8, D), F32)],
        compiler_params=_cparams(("arbitrary", "arbitrary")),
    )(lk, dxn, x, f, pv, ab, wg_in, wg_out)


def ffn_bwd(lk, dxn, x, f, pv, ab, wg_in, wg_out, tm=256):
    T, D = x.shape
    tm = min(tm, T)
    nj, ni = 4, T // tm

    def body(lk_ref, dxn_ref, x_ref, f_ref, pv_ref, ab_ref, win_ref, wout_ref,
             gin_ref, gout_ref, dx_ref, pg_ref, df_all, h_all, dh_all, acc_in, acc_out):
        j, i = pl.program_id(0), pl.program_id(1)
        rows = pl.ds(pl.multiple_of(i * tm, tm), tm)

        @pl.when((i == 0) & (j == 0))
        def _():
            pg_ref[...] = jnp.zeros_like(pg_ref)

        @pl.when(j == 0)
        def _():
            df = _postnorm_bwd(dxn_ref[...], f_ref[...], pv_ref, pg_ref, 0.5)
            df_all[rows, :] = df.astype(BF16)
            h, _, _, _ = _prenorm(x_ref[...], pv_ref)
            h_all[rows, :] = h.astype(BF16)
            dh_all[rows, :] = jnp.zeros((tm, D), F32)

        @pl.when(i == 0)
        def _():
            acc_in[...] = jnp.zeros_like(acc_in)
            acc_out[...] = jnp.zeros_like(acc_out)

        df_t = df_all[rows, :]
        h_t = h_all[rows, :]
        dact = _dot_nt(df_t, wout_ref[...])
        a = ab_ref[0].astype(F32)
        b = ab_ref[1].astype(F32)
        sig = _sigmoid(a)
        s = a * sig
        da = (dact * b * (sig * (1.0 + a * (1.0 - sig)))).astype(BF16)
        db = (dact * s).astype(BF16)
        act = (s * b).astype(BF16)
        dh_all[rows, :] += _dot_nt(da, win_ref[0]) + _dot_nt(db, win_ref[1])
        acc_in[0] += _dot_tn(h_t, da)
        acc_in[1] += _dot_tn(h_t, db)
        acc_out[...] += _dot_tn(act, df_t)

        @pl.when(i == ni - 1)
        def _():
            gin_ref[...] = acc_in[...].astype(BF16)
            gout_ref[...] = acc_out[...].astype(BF16)

        @pl.when(j == nj - 1)
        def _():
            dx_ref[...] = _prenorm_bwd(dh_all[rows, :], dxn_ref[...], x_ref[...], pv_ref, pg_ref)

    ends = lambda j, i, lk: (jnp.where((j == 0) | (j == nj - 1), i, ni - 1), 0)
    first = lambda j, i, lk: (jnp.where(j == 0, i, ni - 1), 0)
    grid_spec = pltpu.PrefetchScalarGridSpec(
        num_scalar_prefetch=1, grid=(nj, ni),
        in_specs=[pl.BlockSpec((tm, D), ends), pl.BlockSpec((tm, D), ends), pl.BlockSpec((tm, D), first),
                  pl.BlockSpec((8, D), lambda j, i, lk: (0, 0)),
                  pl.BlockSpec((2, None, tm, FF_BLK), lambda j, i, lk: (0, j, i, 0)),
                  pl.BlockSpec((None, None, 2, None, D, FF_BLK), lambda j, i, lk: (lk[0], lk[1], 0, j, 0, 0)),
                  pl.BlockSpec((None, None, None, FF_BLK, D), lambda j, i, lk: (lk[0], lk[1], j, 0, 0))],
        out_specs=[pl.BlockSpec((2, None, D, FF_BLK), lambda j, i, lk: (0, j, 0, 0)),
                   pl.BlockSpec((None, FF_BLK, D), lambda j, i, lk: (j, 0, 0)),
                   pl.BlockSpec((tm, D), lambda j, i, lk: (jnp.where(j == nj - 1, i, 0), 0)),
                   pl.BlockSpec((8, D), lambda j, i, lk: (0, 0))],
        scratch_shapes=[pltpu.VMEM((T, D), BF16), pltpu.VMEM((T, D), BF16), pltpu.VMEM((T, D), F32),
                        pltpu.VMEM((2, D, FF_BLK), F32), pltpu.VMEM((FF_BLK, D), F32)],
    )
    return pl.pallas_call(
        body, name="ffn_bwd", grid_spec=grid_spec,
        out_shape=[jax.ShapeDtypeStruct((2, nj, D, FF_BLK), BF16), jax.ShapeDtypeStruct((nj, FF_BLK, D), BF16),
                   jax.ShapeDtypeStruct((T, D), F32), jax.ShapeDtypeStruct((8, D), F32)],
        compiler_params=pltpu.CompilerParams(dimension_semantics=("arbitrary", "arbitrary"),
                                             vmem_limit_bytes=FFN_BWD_VMEM_LIMIT),
    )(lk, dxn, x, f, pv, ab, wg_in, wg_out)


def ffn_bwd_w(h, df, ab, dab, tm=1024):
    T, D = h.shape
    tm = min(tm, T)
    nj, ni = 4, T // tm

    def body(h_ref, df_ref, ab_ref, dab_ref, gin_ref, gout_ref, acc_in, acc_out):
        i = pl.program_id(1)

        @pl.when(i == 0)
        def _():
            acc_in[...] = jnp.zeros_like(acc_in)
            acc_out[...] = jnp.zeros_like(acc_out)

        h = h_ref[...]
        acc_in[0] += _dot_tn(h, dab_ref[0])
        acc_in[1] += _dot_tn(h, dab_ref[1])
        a = ab_ref[0].astype(F32)
        b = ab_ref[1].astype(F32)
        act = (a * _sigmoid(a) * b).astype(BF16)
        acc_out[...] += _dot_tn(act, df_ref[...])

        @pl.when(i == ni - 1)
        def _():
            gin_ref[...] = acc_in[...].astype(BF16)
            gout_ref[...] = acc_out[...].astype(BF16)

    tile = pl.BlockSpec((tm, D), lambda j, i: (i, 0))
    blk = pl.BlockSpec((2, None, tm, FF_BLK), lambda j, i: (0, j, i, 0))
    return pl.pallas_call(
        body, name="ffn_bwd_w", grid=(nj, ni),
        in_specs=[tile, tile, blk, blk],
        out_specs=[pl.BlockSpec((2, None, D, FF_BLK), lambda j, i: (0, j, 0, 0)),
                   pl.BlockSpec((None, FF_BLK, D), lambda j, i: (j, 0, 0))],
        out_shape=[jax.ShapeDtypeStruct((2, nj, D, FF_BLK), BF16),
                   jax.ShapeDtypeStruct((nj, FF_BLK, D), BF16)],
        scratch_shapes=[pltpu.VMEM((2, D, FF_BLK), F32), pltpu.VMEM((FF_BLK, D), F32)],
        compiler_params=_cparams(("arbitrary", "arbitrary")),
    )(h, df, ab, dab)


def mix_in_fwd(l, x, pv, wg, tm=1024):
    T, D = x.shape
    tm = min(tm, T)

    def body(l_ref, x_ref, pv_ref, w_ref, p_ref, h_sc):
        @pl.when(pl.program_id(1) == 0)
        def _():
            h, _, _, _ = _prenorm(x_ref[...], pv_ref)
            h_sc[...] = h.astype(BF16)

        p_ref[...] = _dot(h_sc[...], w_ref[...])

    grid_spec = pltpu.PrefetchScalarGridSpec(
        num_scalar_prefetch=1, grid=(T // tm, N_DEV),
        in_specs=[pl.BlockSpec((tm, D), lambda i, j, l: (i, 0)),
                  pl.BlockSpec((8, D), lambda i, j, l: (0, 0)),
                  pl.BlockSpec((None, None, D, IN_BLK), lambda i, j, l: (l[0], j, 0, 0))],
        out_specs=pl.BlockSpec((tm, IN_BLK), lambda i, j, l: (i, j)),
        scratch_shapes=[pltpu.VMEM((tm, D), BF16)],
    )
    return pl.pallas_call(
        body, name="mix_in_fwd", grid_spec=grid_spec,
        out_shape=jax.ShapeDtypeStruct((T, IN_COLS), F32),
        compiler_params=_cparams(("arbitrary", "arbitrary")),
    )(l, x, pv, wg)


def mix_in_bwd_act(l, dp, dxn, x, pv, wg, tm=1024):
    T, D = x.shape
    tm = min(tm, T)

    def body(l_ref, dp_ref, dxn_ref, x_ref, pv_ref, w_ref, dx_ref, h_ref, pg_ref, dacc):
        i, j = pl.program_id(0), pl.program_id(1)

        @pl.when((i == 0) & (j == 0))
        def _():
            pg_ref[...] = jnp.zeros_like(pg_ref)

        @pl.when(j == 0)
        def _():
            dacc[...] = jnp.zeros_like(dacc)

        dacc[...] += _dot_nt(dp_ref[...], w_ref[...])

        @pl.when(j == N_DEV - 1)
        def _():
            h, _, _, _ = _prenorm(x_ref[...], pv_ref)
            h_ref[...] = h.astype(BF16)
            dx_ref[...] = _prenorm_bwd(dacc[...], dxn_ref[...], x_ref[...], pv_ref, pg_ref)

    tile = pl.BlockSpec((tm, D), lambda i, j, l: (i, 0))
    grid_spec = pltpu.PrefetchScalarGridSpec(
        num_scalar_prefetch=1, grid=(T // tm, N_DEV),
        in_specs=[pl.BlockSpec((tm, IN_BLK), lambda i, j, l: (i, j)), tile, tile,
                  pl.BlockSpec((8, D), lambda i, j, l: (0, 0)),
                  pl.BlockSpec((None, None, D, IN_BLK), lambda i, j, l: (l[0], j, 0, 0))],
        out_specs=[tile, tile, pl.BlockSpec((8, D), lambda i, j, l: (0, 0))],
        scratch_shapes=[pltpu.VMEM((tm, D), F32)],
    )
    return pl.pallas_call(
        body, name="mix_in_bwd_act", grid_spec=grid_spec,
        out_shape=[jax.ShapeDtypeStruct((T, D), F32), jax.ShapeDtypeStruct((T, D), BF16),
                   jax.ShapeDtypeStruct((8, D), F32)],
        compiler_params=_cparams(("arbitrary", "arbitrary")),
    )(l, dp, dxn, x, pv, wg)


def matmul_tn(a, b, tn, tm=1024):
    T, M = a.shape
    tm = min(tm, T)
    N = b.shape[1]
    ni = T // tm

    def body(a_ref, b_ref, o_ref, acc):
        i = pl.program_id(1)

        @pl.when(i == 0)
        def _():
            acc[...] = jnp.zeros_like(acc)

        acc[...] += _dot_tn(a_ref[...], b_ref[...])

        @pl.when(i == ni - 1)
        def _():
            o_ref[...] = acc[...].astype(o_ref.dtype)

    return pl.pallas_call(
        body, name="matmul_tn", grid=(N // tn, ni),
        in_specs=[pl.BlockSpec((tm, M), lambda j, i: (i, 0)), pl.BlockSpec((tm, tn), lambda j, i: (i, j))],
        out_specs=pl.BlockSpec((None, M, tn), lambda j, i: (j, 0, 0)),
        out_shape=jax.ShapeDtypeStruct((N // tn, M, tn), BF16),
        scratch_shapes=[pltpu.VMEM((M, tn), F32)],
        compiler_params=_cparams(("arbitrary", "arbitrary")),
    )(a, b)


SEQ_CHUNK = 256
HALO = 16


def _shift_down(ext, d):
    return pltpu.roll(ext, d, 0)


def _shift_up(ext, d):
    return pltpu.roll(ext, ext.shape[0] - d, 0)


def _rows_with_lead(load, c, width):
    t0 = c * SEQ_CHUNK
    if c == 0:
        return jnp.concatenate([jnp.zeros((HALO, width), F32), load(0, SEQ_CHUNK)], axis=0)
    return load(t0 - HALO, SEQ_CHUNK + HALO)


def _rows_with_tail(load, c, n_chunks, width):
    t0 = c * SEQ_CHUNK
    if c == n_chunks - 1:
        return jnp.concatenate([load(t0, SEQ_CHUNK), jnp.zeros((HALO, width), F32)], axis=0)
    return load(t0, SEQ_CHUNK + HALO)


def conv_fwd(l, p, conv_w):
    T = p.shape[0]
    W = MIX_W
    nC = T // SEQ_CHUNK

    def body(l_ref, p_ref, w_ref, za_ref):
        w0, w1, w2 = w_ref[0:1, :], w_ref[1:2, :], w_ref[2:3, :]
        for c in range(nC):
            ext = _rows_with_lead(lambda s, n: p_ref[s:s + n, W:2 * W] * p_ref[s:s + n, 2 * W:3 * W], c, W)
            y = w2 * ext + w1 * _shift_down(ext, 1) + w0 * _shift_down(ext, 2)
            t0 = c * SEQ_CHUNK
            za_ref[t0:t0 + SEQ_CHUNK, :] = (p_ref[t0:t0 + SEQ_CHUNK, 0:W] * y[HALO:]).astype(BF16)

    grid_spec = pltpu.PrefetchScalarGridSpec(
        num_scalar_prefetch=1, grid=(1,),
        in_specs=[pl.BlockSpec((T, 3 * W), lambda i, l: (0, 0)),
                  pl.BlockSpec((None, 8, W), lambda i, l: (l[0], 0, 0))],
        out_specs=pl.BlockSpec((T, W), lambda i, l: (0, 0)),
    )
    return pl.pallas_call(
        body, name="conv_fwd", grid_spec=grid_spec,
        out_shape=jax.ShapeDtypeStruct((T, W), BF16),
        compiler_params=_cparams(("arbitrary",)),
    )(l, p, conv_w)


def conv_bwd(l, p, dza, conv_w):
    T = p.shape[0]
    W = MIX_W
    nC = T // SEQ_CHUNK

    def body(l_ref, p_ref, dza_ref, w_ref, dp_ref, dw_ref):
        w0, w1, w2 = w_ref[0:1, :], w_ref[1:2, :], w_ref[2:3, :]
        dw = [jnp.zeros((1, W), F32) for _ in range(3)]
        for c in range(nC):
            t0 = c * SEQ_CHUNK
            ext = _rows_with_lead(lambda s, n: p_ref[s:s + n, W:2 * W] * p_ref[s:s + n, 2 * W:3 * W], c, W)
            u1, u2 = _shift_down(ext, 1)[HALO:], _shift_down(ext, 2)[HALO:]
            u0 = ext[HALO:]
            y = w2 * u0 + w1 * u1 + w0 * u2
            dza_c = dza_ref[t0:t0 + SEQ_CHUNK, :]
            dy = dza_c * p_ref[t0:t0 + SEQ_CHUNK, 0:W]
            dw[0] += _colsum(dy * u2)
            dw[1] += _colsum(dy * u1)
            dw[2] += _colsum(dy * u0)
            dye = _rows_with_tail(lambda s, n: dza_ref[s:s + n, :] * p_ref[s:s + n, 0:W], c, nC, W)
            du = (w2 * dye + w1 * _shift_up(dye, 1) + w0 * _shift_up(dye, 2))[:SEQ_CHUNK]
            dp_ref[t0:t0 + SEQ_CHUNK, 0:W] = (dza_c * y).astype(BF16)
            dp_ref[t0:t0 + SEQ_CHUNK, W:2 * W] = (du * p_ref[t0:t0 + SEQ_CHUNK, 2 * W:3 * W]).astype(BF16)
            dp_ref[t0:t0 + SEQ_CHUNK, 2 * W:3 * W] = (du * p_ref[t0:t0 + SEQ_CHUNK, W:2 * W]).astype(BF16)
        dw_ref[...] = jnp.concatenate(dw + [jnp.zeros((5, W), F32)], axis=0)

    grid_spec = pltpu.PrefetchScalarGridSpec(
        num_scalar_prefetch=1, grid=(1,),
        in_specs=[pl.BlockSpec((T, 3 * W), lambda i, l: (0, 0)),
                  pl.BlockSpec((T, W), lambda i, l: (0, 0)),
                  pl.BlockSpec((None, 8, W), lambda i, l: (l[0], 0, 0))],
        out_specs=[pl.BlockSpec((T, 3 * W), lambda i, l: (0, 0)), pl.BlockSpec((8, W), lambda i, l: (0, 0))],
    )
    return pl.pallas_call(
        body, name="conv_bwd", grid_spec=grid_spec,
        out_shape=[jax.ShapeDtypeStruct((T, 3 * W), BF16), jax.ShapeDtypeStruct((8, W), F32)],
        compiler_params=_cparams(("arbitrary",)),
    )(l, p, dza, conv_w)


def _pool_consts(rows, t0):
    lane = lax.broadcasted_iota(jnp.int32, (rows, MIX_W), 1)
    t = lax.broadcasted_iota(jnp.int32, (rows, MIX_W), 0) + t0
    win = jnp.where(lane < 64, 2, jnp.where(lane < 128, 4, jnp.where(lane < 192, 8, 16)))
    inv = 1.0 / jnp.minimum(t + 1, win).astype(F32)
    return lane, inv


def _pick_window(lane, s2, s4, s8, s16):
    return jnp.where(lane < 64, s2, jnp.where(lane < 128, s4, jnp.where(lane < 192, s8, s16)))


def _pooled_chunk(u_ref, c):
    ext = _rows_with_lead(lambda s, n: u_ref[s:s + n, :], c, MIX_W)
    s2 = ext + _shift_down(ext, 1)
    s4 = s2 + _shift_down(s2, 2)
    s8 = s4 + _shift_down(s4, 4)
    s16 = s8 + _shift_down(s8, 8)
    lane, inv = _pool_consts(SEQ_CHUNK, c * SEQ_CHUNK)
    return _pick_window(lane, s2[HALO:], s4[HALO:], s8[HALO:], s16[HALO:]) * inv - ext[HALO:]


def pool_fwd(l, p, w_bd, scale):
    T = p.shape[0]
    W = MIX_W
    nC = T // SEQ_CHUNK

    def body(l_ref, u_ref, w_ref, sc_ref, z_ref):
        for c in range(nC):
            pooled = _pooled_chunk(u_ref, c)
            mixed = _dot(pooled.astype(BF16), w_ref[...])
            z_ref[c * SEQ_CHUNK:(c + 1) * SEQ_CHUNK, :] = (mixed * sc_ref[0:1, :]).astype(BF16)

    grid_spec = pltpu.PrefetchScalarGridSpec(
        num_scalar_prefetch=1, grid=(1,),
        in_specs=[pl.BlockSpec((T, W), lambda i, l: (0, 4)),
                  pl.BlockSpec((None, W, W), lambda i, l: (l[0], 0, 0)),
                  pl.BlockSpec((None, 8, W), lambda i, l: (l[0], 0, 0))],
        out_specs=pl.BlockSpec((T, W), lambda i, l: (0, 0)),
    )
    return pl.pallas_call(
        body, name="pool_fwd", grid_spec=grid_spec,
        out_shape=jax.ShapeDtypeStruct((T, W), BF16),
        compiler_params=_cparams(("arbitrary",)),
    )(l, p, w_bd, scale)


def pool_bwd(l, p, dz, w_bd, scale):
    T = p.shape[0]
    W = MIX_W
    nC = T // SEQ_CHUNK

    def body(l_ref, u_ref, dz_ref, w_ref, sc_ref, du_ref, dw_ref, dsc_ref, e_sc, dpl_sc):
        dw = jnp.zeros((W, W), F32)
        dsc = jnp.zeros((1, W), F32)
        for c in range(nC):
            t0 = c * SEQ_CHUNK
            pooled = _pooled_chunk(u_ref, c).astype(BF16)
            mixed = _dot(pooled, w_ref[...])
            dz_c = dz_ref[t0:t0 + SEQ_CHUNK, :]
            dsc += _colsum(dz_c * mixed)
            dmixed = (dz_c * sc_ref[0:1, :]).astype(BF16)
            dw += _dot_tn(pooled, dmixed)
            dpooled = _dot_nt(dmixed, w_ref[...])
            _, inv = _pool_consts(SEQ_CHUNK, t0)
            dpl_sc[t0:t0 + SEQ_CHUNK, :] = dpooled
            e_sc[t0:t0 + SEQ_CHUNK, :] = dpooled * inv
        for c in range(nC):
            t0 = c * SEQ_CHUNK
            ext = _rows_with_tail(lambda s, n: e_sc[s:s + n, :], c, nC, W)
            s2 = ext + _shift_up(ext, 1)
            s4 = s2 + _shift_up(s2, 2)
            s8 = s4 + _shift_up(s4, 4)
            s16 = s8 + _shift_up(s8, 8)
            lane, _ = _pool_consts(SEQ_CHUNK, t0)
            n = SEQ_CHUNK
            du = _pick_window(lane, s2[:n], s4[:n], s8[:n], s16[:n]) - dpl_sc[t0:t0 + SEQ_CHUNK, :]
            du_ref[t0:t0 + SEQ_CHUNK, :] = du.astype(BF16)
        dw_ref[...] = dw
        dsc_ref[...] = jnp.concatenate([dsc, jnp.zeros((7, W), F32)], axis=0)

    grid_spec = pltpu.PrefetchScalarGridSpec(
        num_scalar_prefetch=1, grid=(1,),
        in_specs=[pl.BlockSpec((T, W), lambda i, l: (0, 4)),
                  pl.BlockSpec((T, W), lambda i, l: (0, 0)),
                  pl.BlockSpec((None, W, W), lambda i, l: (l[0], 0, 0)),
                  pl.BlockSpec((None, 8, W), lambda i, l: (l[0], 0, 0))],
        out_specs=[pl.BlockSpec((T, W), lambda i, l: (0, 0)), pl.BlockSpec((W, W), lambda i, l: (0, 0)),
                   pl.BlockSpec((8, W), lambda i, l: (0, 0))],
        scratch_shapes=[pltpu.VMEM((T, W), F32), pltpu.VMEM((T, W), F32)],
    )
    return pl.pallas_call(
        body, name="pool_bwd", grid_spec=grid_spec,
        out_shape=[jax.ShapeDtypeStruct((T, W), BF16), jax.ShapeDtypeStruct((W, W), F32),
                   jax.ShapeDtypeStruct((8, W), F32)],
        compiler_params=_cparams(("arbitrary",)),
    )(l, p, dz, w_bd, scale)


def _s5_disc(lre, lim, ldt):
    lr = jnp.minimum(lre, DT_LAMBDA_RE_MAX)
    dt = jnp.exp(ldt)
    mag = jnp.exp(lr * dt)
    a_re = mag * jnp.cos(lim * dt)
    a_im = mag * jnp.sin(lim * dt)
    den = lr * lr + lim * lim
    nr = a_re - 1.0
    return a_re, a_im, (nr * lr + a_im * lim) / den, (a_im * lr - nr * lim) / den


def _bd_mask(shape, row_blk, col_blk):
    r = lax.broadcasted_iota(jnp.int32, shape, 0) >> (row_blk.bit_length() - 1)
    c = lax.broadcasted_iota(jnp.int32, shape, 1) >> (col_blk.bit_length() - 1)
    return r == c


def s5_params(lam, b_t, c_t):
    L = lam.shape[0]

    def body(lam_ref, b_ref, c_ref, a_ref, bbd_ref, cbd_ref):
        a_re, a_im, f_re, f_im = _s5_disc(lam_ref[0:1, :], lam_ref[1:2, :], lam_ref[2:3, :])
        a_ref[...] = jnp.concatenate([a_re, a_im, jnp.zeros((6, SSM_W), F32)], axis=0)
        mb = _bd_mask((MIX_W, SSM_W), SSM_GROUP, SSM_STATE)
        bbd_ref[0] = jnp.where(mb, f_re * b_ref[0] - f_im * b_ref[1], 0.0).astype(BF16)
        bbd_ref[1] = jnp.where(mb, f_re * b_ref[1] + f_im * b_ref[0], 0.0).astype(BF16)
        mc = _bd_mask((SSM_W, MIX_W), SSM_STATE, SSM_GROUP)
        cbd_ref[0] = jnp.where(mc, c_ref[0], 0.0).astype(BF16)
        cbd_ref[1] = jnp.where(mc, c_ref[1], 0.0).astype(BF16)

    return pl.pallas_call(
        body, name="s5_params", grid=(L,),
        in_specs=[pl.BlockSpec((None, 8, SSM_W), lambda l: (l, 0, 0)),
                  pl.BlockSpec((None, 2, MIX_W, SSM_W), lambda l: (l, 0, 0, 0)),
                  pl.BlockSpec((None, 2, SSM_W, MIX_W), lambda l: (l, 0, 0, 0))],
        out_specs=[pl.BlockSpec((None, 8, SSM_W), lambda l: (l, 0, 0)),
                   pl.BlockSpec((None, 2, MIX_W, SSM_W), lambda l: (l, 0, 0, 0)),
                   pl.BlockSpec((None, 2, SSM_W, MIX_W), lambda l: (l, 0, 0, 0))],
        out_shape=[jax.ShapeDtypeStruct((L, 8, SSM_W), F32),
                   jax.ShapeDtypeStruct((L, 2, MIX_W, SSM_W), BF16),
                   jax.ShapeDtypeStruct((L, 2, SSM_W, MIX_W), BF16)],
        compiler_params=_cparams(("arbitrary",)),
    )(lam, b_t, c_t)


def s5_params_bwd(lam, b_t, gb, gc, da):
    L = lam.shape[0]

    def body(lam_ref, b_ref, gb_ref, gc_ref, da_ref, dlam_ref, db_ref, dc_ref, dgrp_ref):
        lre, lim, ldt = lam_ref[0:1, :], lam_ref[1:2, :], lam_ref[2:3, :]
        (a_re, a_im, f_re, f_im), vjp = jax.vjp(_s5_disc, lre, lim, ldt)
        mb = _bd_mask((MIX_W, SSM_W), SSM_GROUP, SSM_STATE)
        gbr = jnp.where(mb, gb_ref[0], 0.0)
        gbi = jnp.where(mb, gb_ref[1], 0.0)
        df_re = _colsum(gbr * b_ref[0] + gbi * b_ref[1])
        df_im = _colsum(gbi * b_ref[0] - gbr * b_ref[1])
        db_ref[0] = f_re * gbr + f_im * gbi
        db_ref[1] = f_re * gbi - f_im * gbr
        mc = _bd_mask((SSM_W, MIX_W), SSM_STATE, SSM_GROUP)
        dc_ref[0] = jnp.where(mc, gc_ref[0], 0.0)
        dc_ref[1] = jnp.where(mc, gc_ref[1], 0.0)
        dlre, dlim, dldt = vjp((da_ref[0:1, :], da_ref[1:2, :], df_re, df_im))
        dl = jnp.concatenate([dlre, dlim, dldt, jnp.zeros((5, SSM_W), F32)], axis=0)
        dlam_ref[...] = dl
        grp = jnp.where(_bd_mask((SSM_W, 128), SSM_STATE, 1), 1.0, 0.0)
        dgrp_ref[...] = jnp.dot(dl, grp, preferred_element_type=F32, precision=lax.Precision.HIGHEST)

    vec = pl.BlockSpec((None, 8, SSM_W), lambda l: (l, 0, 0))
    bsp = pl.BlockSpec((None, 2, MIX_W, SSM_W), lambda l: (l, 0, 0, 0))
    csp = pl.BlockSpec((None, 2, SSM_W, MIX_W), lambda l: (l, 0, 0, 0))
    return pl.pallas_call(
        body, name="s5_params_bwd", grid=(L,),
        in_specs=[vec, bsp, bsp, csp, vec],
        out_specs=[vec, bsp, csp, pl.BlockSpec((None, 8, 128), lambda l: (l, 0, 0))],
        out_shape=[jax.ShapeDtypeStruct((L, 8, SSM_W), F32),
                   jax.ShapeDtypeStruct((L, 2, MIX_W, SSM_W), F32),
                   jax.ShapeDtypeStruct((L, 2, SSM_W, MIX_W), F32),
                   jax.ShapeDtypeStruct((L, 8, 128), F32)],
        compiler_params=_cparams(("arbitrary",)),
    )(lam, b_t, gb, gc, da)


def s5_bu(l, p, b_bd, tm=512):
    T = p.shape[0]

    def body(l_ref, u_ref, b_ref, bu_ref):
        u = u_ref[...].astype(BF16)
        bu_ref[0] = _dot(u, b_ref[0])
        bu_ref[1] = _dot(u, b_ref[1])

    grid_spec = pltpu.PrefetchScalarGridSpec(
        num_scalar_prefetch=1, grid=(T // tm,),
        in_specs=[pl.BlockSpec((tm, MIX_W), lambda i, l: (i, 3)),
                  pl.BlockSpec((None, 2, MIX_W, SSM_W), lambda i, l: (l[0], 0, 0, 0))],
        out_specs=pl.BlockSpec((2, tm, SSM_W), lambda i, l: (0, i, 0)),
    )
    return pl.pallas_call(
        body, name="s5_bu", grid_spec=grid_spec,
        out_shape=jax.ShapeDtypeStruct((2, T, SSM_W), F32),
        compiler_params=_cparams(("arbitrary",)),
    )(l, p, b_bd)


def s5_scan(l, avec, xs, reverse):
    T = xs.shape[1]
    CH = SEQ_CHUNK
    nC = T // CH
    LW = 128
    n_steps = CH.bit_length() - 1

    def body(l_ref, a_ref, x_ref, s_ref):
        ar = a_ref[0:1, :]
        ai = -a_ref[1:2, :] if reverse else a_ref[1:2, :]
        pows = [(ar, ai)]
        for _ in range(n_steps - 1):
            r, i = pows[-1]
            pows.append((r * r - i * i, 2.0 * r * i))
        row = lax.broadcasted_iota(jnp.int32, (CH, LW), 0)

        def local_scan(re, im):
            for k in range(n_steps):
                d = 1 << k
                pr, pi = pows[k]
                if reverse:
                    keep = row < CH - d
                    sr, si = _shift_up(re, d), _shift_up(im, d)
                else:
                    keep = row >= d
                    sr, si = _shift_down(re, d), _shift_down(im, d)
                sr = jnp.where(keep, sr, 0.0)
                si = jnp.where(keep, si, 0.0)
                re, im = re + pr * sr - pi * si, im + pr * si + pi * sr
            return re, im

        edge = CH - 1 if reverse else 0
        pw_re, pw_im = local_scan(jnp.where(row == edge, ar, 0.0), jnp.where(row == edge, ai, 0.0))
        last = 0 if reverse else CH - 1

        def chunk(c, carry):
            cr, ci = carry
            cc = nC - 1 - c if reverse else c
            t0 = pl.multiple_of(cc * CH, CH)
            re, im = local_scan(x_ref[0, pl.ds(t0, CH), :], x_ref[1, pl.ds(t0, CH), :])
            re2 = re + pw_re * cr - pw_im * ci
            im2 = im + pw_re * ci + pw_im * cr
            s_ref[0, pl.ds(t0, CH), :] = re2
            s_ref[1, pl.ds(t0, CH), :] = im2
            return re2[last:last + 1, :], im2[last:last + 1, :]

        lax.fori_loop(0, nC, chunk, (jnp.zeros((1, LW), F32), jnp.zeros((1, LW), F32)))

    grid_spec = pltpu.PrefetchScalarGridSpec(
        num_scalar_prefetch=1, grid=(SSM_W // LW,),
        in_specs=[pl.BlockSpec((None, 8, LW), lambda g, l: (l[0], 0, g)),
                  pl.BlockSpec((2, T, LW), lambda g, l: (0, 0, g))],
        out_specs=pl.BlockSpec((2, T, LW), lambda g, l: (0, 0, g)),
    )
    return pl.pallas_call(
        body, name="s5_scan_rev" if reverse else "s5_scan_fwd", grid_spec=grid_spec,
        out_shape=jax.ShapeDtypeStruct((2, T, SSM_W), F32),
        compiler_params=_cparams(("arbitrary",)),
    )(l, avec, xs)


_GELU_C = 0.7978845608028654
_GELU_K = 0.044715


def _s5_y(u, s_ref, c_ref, d_row):
    y = _dot(s_ref[0].astype(BF16), c_ref[0]) - _dot(s_ref[1].astype(BF16), c_ref[1])
    return y + d_row * u


def s5_out(l, p, s, c_bd, ssm_d, tm=512):
    T = p.shape[0]

    def body(l_ref, u_ref, s_ref, c_ref, d_ref, yg_ref):
        y = _s5_y(u_ref[...], s_ref, c_ref, d_ref[0:1, :])
        th = jnp.tanh(_GELU_C * (y + _GELU_K * y * y * y))
        yg_ref[...] = (0.5 * y * (1.0 + th)).astype(BF16)

    grid_spec = pltpu.PrefetchScalarGridSpec(
        num_scalar_prefetch=1, grid=(T // tm,),
        in_specs=[pl.BlockSpec((tm, MIX_W), lambda i, l: (i, 3)),
                  pl.BlockSpec((2, tm, SSM_W), lambda i, l: (0, i, 0)),
                  pl.BlockSpec((None, 2, SSM_W, MIX_W), lambda i, l: (l[0], 0, 0, 0)),
                  pl.BlockSpec((None, 8, MIX_W), lambda i, l: (l[0], 0, 0))],
        out_specs=pl.BlockSpec((tm, MIX_W), lambda i, l: (i, 0)),
    )
    return pl.pallas_call(
        body, name="s5_out", grid_spec=grid_spec,
        out_shape=jax.ShapeDtypeStruct((T, MIX_W), BF16),
        compiler_params=_cparams(("arbitrary",)),
    )(l, p, s, c_bd, ssm_d)


def s5_bwd_y(l, p, s, dyg, c_bd, ssm_d, tm=512):
    T = p.shape[0]

    def body(l_ref, u_ref, s_ref, dyg_ref, c_ref, d_ref, ds_ref, du_ref, gc_ref, dd_ref):
        @pl.when(pl.program_id(0) == 0)
        def _():
            gc_ref[...] = jnp.zeros_like(gc_ref)
            dd_ref[...] = jnp.zeros_like(dd_ref)

        u = u_ref[...]
        y = _s5_y(u, s_ref, c_ref, d_ref[0:1, :])
        inner = _GELU_C * (y + _GELU_K * y * y * y)
        th = jnp.tanh(inner)
        dgelu = 0.5 * (1.0 + th) + 0.5 * y * (1.0 - th * th) * (_GELU_C * (1.0 + 3.0 * _GELU_K * y * y))
        dy = dyg_ref[...] * dgelu
        dd_ref[0:1, :] += _colsum(dy * u)
        du_ref[...] = dy * d_ref[0:1, :]
        dyb = dy.astype(BF16)
        ds_ref[0] = _dot_nt(dyb, c_ref[0])
        ds_ref[1] = -_dot_nt(dyb, c_ref[1])
        gc_ref[0] += _dot_tn(s_ref[0].astype(BF16), dyb)
        gc_ref[1] -= _dot_tn(s_ref[1].astype(BF16), dyb)

    grid_spec = pltpu.PrefetchScalarGridSpec(
        num_scalar_prefetch=1, grid=(T // tm,),
        in_specs=[pl.BlockSpec((tm, MIX_W), lambda i, l: (i, 3)),
                  pl.BlockSpec((2, tm, SSM_W), lambda i, l: (0, i, 0)),
                  pl.BlockSpec((tm, MIX_W), lambda i, l: (i, 0)),
                  pl.BlockSpec((None, 2, SSM_W, MIX_W), lambda i, l: (l[0], 0, 0, 0)),
                  pl.BlockSpec((None, 8, MIX_W), lambda i, l: (l[0], 0, 0))],
        out_specs=[pl.BlockSpec((2, tm, SSM_W), lambda i, l: (0, i, 0)),
                   pl.BlockSpec((tm, MIX_W), lambda i, l: (i, 0)),
                   pl.BlockSpec((2, SSM_W, MIX_W), lambda i, l: (0, 0, 0)),
                   pl.BlockSpec((8, MIX_W), lambda i, l: (0, 0))],
    )
    return pl.pallas_call(
        body, name="s5_bwd_y", grid_spec=grid_spec,
        out_shape=[jax.ShapeDtypeStruct((2, T, SSM_W), F32), jax.ShapeDtypeStruct((T, MIX_W), F32),
                   jax.ShapeDtypeStruct((2, SSM_W, MIX_W), F32), jax.ShapeDtypeStruct((8, MIX_W), F32)],
        compiler_params=_cparams(("arbitrary",)),
    )(l, p, s, dyg, c_bd, ssm_d)


def s5_bwd_u(l, p, lam_s, du_skip, b_bd, tm=512):
    T = p.shape[0]

    def body(l_ref, u_ref, ls_ref, dus_ref, b_ref, du_ref, gb_ref):
        @pl.when(pl.program_id(0) == 0)
        def _():
            gb_ref[...] = jnp.zeros_like(gb_ref)

        u = u_ref[...].astype(BF16)
        lr = ls_ref[0].astype(BF16)
        li = ls_ref[1].astype(BF16)
        gb_ref[0] += _dot_tn(u, lr)
        gb_ref[1] += _dot_tn(u, li)
        du_ref[...] = (dus_ref[...] + _dot_nt(lr, b_ref[0]) + _dot_nt(li, b_ref[1])).astype(BF16)

    grid_spec = pltpu.PrefetchScalarGridSpec(
        num_scalar_prefetch=1, grid=(T // tm,),
        in_specs=[pl.BlockSpec((tm, MIX_W), lambda i, l: (i, 3)),
                  pl.BlockSpec((2, tm, SSM_W), lambda i, l: (0, i, 0)),
                  pl.BlockSpec((tm, MIX_W), lambda i, l: (i, 0)),
                  pl.BlockSpec((None, 2, MIX_W, SSM_W), lambda i, l: (l[0], 0, 0, 0))],
        out_specs=[pl.BlockSpec((tm, MIX_W), lambda i, l: (i, 0)),
                   pl.BlockSpec((2, MIX_W, SSM_W), lambda i, l: (0, 0, 0))],
    )
    return pl.pallas_call(
        body, name="s5_bwd_u", grid_spec=grid_spec,
        out_shape=[jax.ShapeDtypeStruct((T, MIX_W), BF16), jax.ShapeDtypeStruct((2, MIX_W, SSM_W), F32)],
        compiler_params=_cparams(("arbitrary",)),
    )(l, p, lam_s, du_skip, b_bd)


def s5_bwd_a(s, lam_s):
    T = s.shape[1]
    nC = T // SEQ_CHUNK
    LW = 128

    def body(s_ref, ls_ref, da_ref):
        dre = jnp.zeros((1, LW), F32)
        dim = jnp.zeros((1, LW), F32)
        for c in range(nC):
            t0 = c * SEQ_CHUNK
            sr = _shift_down(_rows_with_lead(lambda a, n: s_ref[0, a:a + n, :], c, LW), 1)[HALO:]
            si = _shift_down(_rows_with_lead(lambda a, n: s_ref[1, a:a + n, :], c, LW), 1)[HALO:]
            lr = ls_ref[0, t0:t0 + SEQ_CHUNK, :]
            li = ls_ref[1, t0:t0 + SEQ_CHUNK, :]
            dre += _colsum(sr * lr + si * li)
            dim += _colsum(sr * li - si * lr)
        da_ref[...] = jnp.concatenate([dre, dim, jnp.zeros((6, LW), F32)], axis=0)

    blk = pl.BlockSpec((2, T, LW), lambda g: (0, 0, g))
    return pl.pallas_call(
        body, name="s5_bwd_a", grid=(SSM_W // LW,),
        in_specs=[blk, blk],
        out_specs=pl.BlockSpec((8, LW), lambda g: (0, g)),
        out_shape=jax.ShapeDtypeStruct((8, SSM_W), F32),
        compiler_params=_cparams(("arbitrary",)),
    )(s, lam_s)


SB_BLK = 128
SB_SCALE = SB_HEAD ** -0.5


def _split_bf16(x):
    hi = x.astype(BF16)
    return hi, (x - hi.astype(F32)).astype(BF16)


def _dot_split(x, tri):
    hi, lo = _split_bf16(x)
    return _dot(hi, tri) + _dot(lo, tri)


SB_SLABS = MIX_W // SB_BLK
SB_STACK = 2 * SB_SLABS * SB_BLK
SB_PAIR = 2 * SB_BLK


def _sb_valid(r0, c0):
    row = (lax.broadcasted_iota(jnp.int32, (SB_STACK, SB_BLK), 0) & (SB_BLK - 1)) + r0
    col = lax.broadcasted_iota(jnp.int32, (SB_STACK, SB_BLK), 1) + c0
    return col < row


def _sb_stack(ref, r0, scale):
    lane = lax.broadcasted_iota(jnp.int32, (SB_BLK, SB_BLK), 1)
    parts = []
    for s in range(SB_SLABS):
        blk = ref[pl.ds(r0, SB_BLK), s * SB_BLK:(s + 1) * SB_BLK] * scale
        parts += [jnp.where(lane < SB_HEAD, blk, 0.0), jnp.where(lane >= SB_HEAD, blk, 0.0)]
    return jnp.concatenate(parts, axis=0).astype(BF16)


def _sb_rows_nt(stack, ref, c0):
    return jnp.concatenate(
        [_dot_nt(stack[s * SB_PAIR:(s + 1) * SB_PAIR], ref[pl.ds(c0, SB_BLK), s * SB_BLK:(s + 1) * SB_BLK].astype(BF16))
         for s in range(SB_SLABS)], axis=0)


def _sb_wide(stack, s):
    return jnp.concatenate([stack[s * SB_PAIR:s * SB_PAIR + SB_BLK], stack[s * SB_PAIR + SB_BLK:(s + 1) * SB_PAIR]],
                           axis=1)


def _sb_logits(q_stack, k_ref, c0, valid):
    z = _sb_rows_nt(q_stack, k_ref, c0)
    sp = jnp.log(1.0 + jnp.exp(-jnp.abs(z)))
    ls_pos = jnp.minimum(z, 0.0) - sp
    lk = jnp.minimum(-z, 0.0) - sp
    if valid is not None:
        lk = jnp.where(valid, lk, 0.0)
    return z, ls_pos, lk


def _tri(lower):
    r = lax.broadcasted_iota(jnp.int32, (SB_BLK, SB_BLK), 0)
    c = lax.broadcasted_iota(jnp.int32, (SB_BLK, SB_BLK), 1)
    return jnp.where(r > c if lower else r < c, 1.0, 0.0).astype(BF16)


def sb_fwd(p):
    T = p.shape[0]
    W = MIX_W
    nB = T // SB_BLK

    def body(q_ref, k_ref, v_ref, o_ref, tot_ref, acc_sc):
        tri = _tri(True)

        def qblock(i, _):
            r0 = pl.multiple_of(i * SB_BLK, SB_BLK)
            q = _sb_stack(q_ref, r0, SB_SCALE)
            acc_sc[...] = jnp.zeros_like(acc_sc)

            def kblocks(c0s, run, valid):
                parts = [_sb_logits(q, k_ref, c0, valid) for c0 in c0s]
                for c0, (_, ls_pos, lk) in zip(c0s, parts):
                    a = jnp.exp(ls_pos + _dot_split(lk, tri) + run)
                    if valid is not None:
                        a = jnp.where(valid, a, 0.0)
                    a = a.astype(BF16)
                    v = _sb_stack(v_ref, c0, 1.0)
                    for s in range(SB_SLABS):
                        acc_sc[:, s * SB_BLK:(s + 1) * SB_BLK] += _dot(_sb_wide(a, s), v[s * SB_PAIR:(s + 1) * SB_PAIR])
                    run = run + jnp.sum(lk, axis=1, keepdims=True)
                return run

            def key_block(jj):
                return pl.multiple_of((i - jj) * SB_BLK, SB_BLK)

            run = kblocks([r0], jnp.zeros((SB_STACK, 1), F32), _sb_valid(0, 0))
            odd = i & 1
            run = lax.cond(odd == 1, lambda r: kblocks([key_block(1)], r, None), lambda r: r, run)
            total = lax.fori_loop(
                0, i >> 1, lambda t, r: kblocks([key_block(1 + odd + 2 * t), key_block(2 + odd + 2 * t)], r, None), run)
            o_ref[pl.ds(r0, SB_BLK), :] = acc_sc[...].astype(BF16)
            tot_ref[pl.ds(pl.multiple_of(i * SB_STACK, SB_STACK), SB_STACK), :] = jnp.broadcast_to(total, (SB_STACK, SB_BLK))
            return 0

        lax.fori_loop(0, nB, qblock, 0)

    return pl.pallas_call(
        body, name="sb_fwd", grid=(1,),
        in_specs=[pl.BlockSpec((T, W), lambda i: (0, 5)), pl.BlockSpec((T, W), lambda i: (0, 6)),
                  pl.BlockSpec((T, W), lambda i: (0, 7))],
        out_specs=[pl.BlockSpec((T, W), lambda i: (0, 0)), pl.BlockSpec((nB * SB_STACK, SB_BLK), lambda i: (0, 0))],
        out_shape=[jax.ShapeDtypeStruct((T, W), BF16), jax.ShapeDtypeStruct((nB * SB_STACK, SB_BLK), F32)],
        scratch_shapes=[pltpu.VMEM((SB_BLK, W), F32)],
        compiler_params=_cparams(("arbitrary",)),
    )(p, p, p)


def sb_bwd(p, do, tot):
    T = p.shape[0]
    W = MIX_W
    nB = T // SB_BLK

    def body(q_ref, k_ref, v_ref, do_ref, tot_ref, dqkv_ref, dq_sc, dk_sc, dv_sc):
        tri_gt = _tri(True)
        tri_lt = _tri(False)
        dq_sc[...] = jnp.zeros_like(dq_sc)
        dk_sc[...] = jnp.zeros_like(dk_sc)
        dv_sc[...] = jnp.zeros_like(dv_sc)
        zcol = jnp.zeros((SB_STACK, 1), F32)

        def qblock(i, _):
            r0 = pl.multiple_of(i * SB_BLK, SB_BLK)
            q = _sb_stack(q_ref, r0, SB_SCALE)
            dob = _sb_stack(do_ref, r0, 1.0)

            total = tot_ref[pl.ds(pl.multiple_of(i * SB_STACK, SB_STACK), SB_STACK), 0:1]

            def kblocks(c0s, carry, valid):
                pre, seen = carry
                parts = [_sb_logits(q, k_ref, c0, valid) for c0 in c0s]
                for c0, (z, ls_pos, lk) in zip(c0s, parts):
                    seen = seen + jnp.sum(lk, axis=1, keepdims=True)
                    a = jnp.exp(ls_pos + _dot_split(lk, tri_gt) + (total - seen))
                    if valid is not None:
                        a = jnp.where(valid, a, 0.0)
                    dlw = _sb_rows_nt(dob, v_ref, c0) * a
                    g = pre + _dot_split(dlw, tri_lt)
                    sig = _sigmoid(z)
                    dz = dlw * (1.0 - sig) - g * sig
                    if valid is not None:
                        dz = jnp.where(valid, dz, 0.0)
                    dz = dz.astype(BF16)
                    ab = a.astype(BF16)
                    km = _sb_stack(k_ref, c0, 1.0)
                    for s in range(SB_SLABS):
                        pair = slice(s * SB_PAIR, (s + 1) * SB_PAIR)
                        ls = slice(s * SB_BLK, (s + 1) * SB_BLK)
                        dk_sc[pl.ds(c0, SB_BLK), ls] += _dot_tn(dz[pair], q[pair])
                        dv_sc[pl.ds(c0, SB_BLK), ls] += _dot_tn(ab[pair], dob[pair])
                        dq_sc[pl.ds(r0, SB_BLK), ls] += _dot(_sb_wide(dz, s), km[pair])
                    pre = pre + jnp.sum(dlw, axis=1, keepdims=True)
                return pre, seen

            def key_block(j):
                return pl.multiple_of(j * SB_BLK, SB_BLK)

            carry = lax.fori_loop(
                0, i >> 1, lambda t, c: kblocks([key_block(2 * t), key_block(2 * t + 1)], c, None), (zcol, zcol))
            carry = lax.cond((i & 1) == 1, lambda c: kblocks([key_block(i - 1)], c, None), lambda c: c, carry)
            kblocks([r0], carry, _sb_valid(0, 0))
            return 0

        lax.fori_loop(0, nB, qblock, 0)
        dqkv_ref[:, 0:W] = (dq_sc[...] * SB_SCALE).astype(BF16)
        dqkv_ref[:, W:2 * W] = dk_sc[...].astype(BF16)
        dqkv_ref[:, 2 * W:3 * W] = dv_sc[...].astype(BF16)

    return pl.pallas_call(
        body, name="sb_bwd", grid=(1,),
        in_specs=[pl.BlockSpec((T, W), lambda i: (0, 5)), pl.BlockSpec((T, W), lambda i: (0, 6)),
                  pl.BlockSpec((T, W), lambda i: (0, 7)), pl.BlockSpec((T, W), lambda i: (0, 0)),
                  pl.BlockSpec((nB * SB_STACK, SB_BLK), lambda i: (0, 0))],
        out_specs=pl.BlockSpec((T, 3 * W), lambda i: (0, 0)),
        out_shape=jax.ShapeDtypeStruct((T, 3 * W), BF16),
        scratch_shapes=[pltpu.VMEM((T, W), F32), pltpu.VMEM((T, W), F32), pltpu.VMEM((T, W), F32)],
        compiler_params=_cparams(("arbitrary",)),
    )(p, p, p, do, tot)


def _dot_cols(a, w_ref):
    return jnp.concatenate([_dot(a, w_ref[j]) for j in range(N_DEV)], axis=1)


def _dot_cols_nt(dy, w_ref):
    n = w_ref.shape[2]
    out = _dot_nt(dy[:, 0:n], w_ref[0])
    for j in range(1, N_DEV):
        out += _dot_nt(dy[:, j * n:(j + 1) * n], w_ref[j])
    return out


def _acc_cols_tn(acc_ref, a, dy):
    n = acc_ref.shape[2]
    for j in range(N_DEV):
        acc_ref[j] += _dot_tn(a, dy[:, j * n:(j + 1) * n])


def _merge_branches(za_ref, yg_ref, z_ref, o_ref, gate_refs, wc_ref, wglu_ref, wp_ref, ws_ref):
    D = D_MODEL
    glu = _dot_cols(yg_ref[...], wglu_ref)
    glu_a, sg = glu[:, :D], _sigmoid(glu[:, D:])
    ys = [_dot_cols(za_ref[...], wc_ref), glu_a * sg, _dot_cols(z_ref[...], wp_ref), _dot_cols(o_ref[...], ws_ref)]
    gs = [_sigmoid(g[...]) for g in gate_refs]
    merged = gs[0] * ys[0] + gs[1] * ys[1] + gs[2] * ys[2] + gs[3] * ys[3]
    return ys, gs, glu_a, sg, merged


def _merge_specs(tm, D):
    W = MIX_W
    br = pl.BlockSpec((tm, W), lambda i, l: (i, 0))
    gates = [pl.BlockSpec((tm, D), functools.partial(lambda i, l, b: (i, 2 + b), b=b)) for b in range(4)]
    wsm = pl.BlockSpec((None, N_DEV, W, D // N_DEV), lambda i, l: (l[0], 0, 0, 0))
    weights = [wsm, pl.BlockSpec((None, N_DEV, W, 2 * D // N_DEV), lambda i, l: (l[0], 0, 0, 0)), wsm, wsm,
               pl.BlockSpec((None, D, D), lambda i, l: (l[0], 0, 0))]
    return [br] * 4 + gates, weights


def merge_fwd(l, p, za, yg, z, o, x, pv, wc, wglu, wp, ws, wo, tm=512):
    T, D = x.shape
    tm = min(tm, T)

    def body(l_ref, za_ref, yg_ref, z_ref, o_ref, g0, g1, g2, g3, x_ref, pv_ref,
             wc_ref, wglu_ref, wp_ref, ws_ref, wo_ref, xn_ref, m_ref):
        _, _, _, _, merged = _merge_branches(za_ref, yg_ref, z_ref, o_ref, (g0, g1, g2, g3),
                                             wc_ref, wglu_ref, wp_ref, ws_ref)
        m = _dot(merged.astype(BF16), wo_ref[...])
        m_ref[...] = m
        xn_ref[...] = _postnorm_res(x_ref[...], m, pv_ref, 1.0)

    acts, weights = _merge_specs(tm, D)
    tile = pl.BlockSpec((tm, D), lambda i, l: (i, 0))
    grid_spec = pltpu.PrefetchScalarGridSpec(
        num_scalar_prefetch=1, grid=(T // tm,),
        in_specs=acts + [tile, pl.BlockSpec((8, D), lambda i, l: (0, 0))] + weights,
        out_specs=[tile, tile],
    )
    return pl.pallas_call(
        body, name="merge_fwd", grid_spec=grid_spec,
        out_shape=[jax.ShapeDtypeStruct((T, D), F32), jax.ShapeDtypeStruct((T, D), F32)],
        compiler_params=_cparams(("arbitrary",)),
    )(l, za, yg, z, o, p, p, p, p, x, pv, wc, wglu, wp, ws, wo)


def merge_bwd(l, p, za, yg, z, o, m, dxn, pv, wc, wglu, wp, ws, wo, tm=256):
    T, D = m.shape
    W = MIX_W
    tm = min(tm, T)
    ni = T // tm

    def body(l_ref, za_ref, yg_ref, z_ref, o_ref, g0, g1, g2, g3, m_ref, dxn_ref, pv_ref,
             wc_ref, wglu_ref, wp_ref, ws_ref, wo_ref,
             dza_ref, dyg_ref, dz_ref, do_ref, dg_ref, pg_ref, gwc_ref, gwglu_ref, gwp_ref, gws_ref, gwo_ref,
             awc, awglu, awp, aws, awo):
        i = pl.program_id(0)

        @pl.when(i == 0)
        def _():
            pg_ref[...] = jnp.zeros_like(pg_ref)
            for a in (awc, awglu, awp, aws, awo):
                a[...] = jnp.zeros_like(a)

        ys, gs, glu_a, sg, merged = _merge_branches(za_ref, yg_ref, z_ref, o_ref, (g0, g1, g2, g3),
                                                    wc_ref, wglu_ref, wp_ref, ws_ref)
        dm = _postnorm_bwd(dxn_ref[...], m_ref[...], pv_ref, pg_ref, 1.0).astype(BF16)
        awo[...] += _dot_tn(merged.astype(BF16), dm)
        dmerged = _dot_nt(dm, wo_ref[...])
        for b in range(4):
            dg_ref[:, b * D:(b + 1) * D] = (dmerged * ys[b] * gs[b] * (1.0 - gs[b])).astype(BF16)
        dya = (dmerged * gs[0]).astype(BF16)
        _acc_cols_tn(awc, za_ref[...], dya)
        dza_ref[...] = _dot_cols_nt(dya, wc_ref)
        dyc = (dmerged * gs[2]).astype(BF16)
        _acc_cols_tn(awp, z_ref[...], dyc)
        dz_ref[...] = _dot_cols_nt(dyc, wp_ref)
        dyd = (dmerged * gs[3]).astype(BF16)
        _acc_cols_tn(aws, o_ref[...], dyd)
        do_ref[...] = _dot_cols_nt(dyd, ws_ref)
        dyb = dmerged * gs[1]
        dglu = jnp.concatenate([dyb * sg, dyb * glu_a * sg * (1.0 - sg)], axis=1).astype(BF16)
        _acc_cols_tn(awglu, yg_ref[...], dglu)
        dyg_ref[...] = _dot_cols_nt(dglu, wglu_ref)

        @pl.when(i == ni - 1)
        def _():
            gwc_ref[...] = awc[...].astype(BF16)
            gwglu_ref[...] = awglu[...].astype(BF16)
            gwp_ref[...] = awp[...].astype(BF16)
            gws_ref[...] = aws[...].astype(BF16)
            gwo_ref[...] = awo[...].astype(BF16)

    acts, weights = _merge_specs(tm, D)
    tile = pl.BlockSpec((tm, D), lambda i, l: (i, 0))
    br = pl.BlockSpec((tm, W), lambda i, l: (i, 0))
    full = lambda *s: pl.BlockSpec(s, lambda i, l: (0,) * len(s))
    sm, glu_s = (N_DEV, W, D // N_DEV), (N_DEV, W, 2 * D // N_DEV)
    grid_spec = pltpu.PrefetchScalarGridSpec(
        num_scalar_prefetch=1, grid=(ni,),
        in_specs=acts + [tile, tile, pl.BlockSpec((8, D), lambda i, l: (0, 0))] + weights,
        out_specs=[br, br, br, br, pl.BlockSpec((tm, 4 * D), lambda i, l: (i, 0)), full(8, D),
                   full(*sm), full(*glu_s), full(*sm), full(*sm), full(D, D)],
        scratch_shapes=[pltpu.VMEM(sm, F32), pltpu.VMEM(glu_s, F32), pltpu.VMEM(sm, F32),
                        pltpu.VMEM(sm, F32), pltpu.VMEM((D, D), F32)],
    )
    f32br = jax.ShapeDtypeStruct((T, W), F32)
    return pl.pallas_call(
        body, name="merge_bwd", grid_spec=grid_spec,
        out_shape=[f32br, f32br, f32br, f32br, jax.ShapeDtypeStruct((T, 4 * D), BF16),
                   jax.ShapeDtypeStruct((8, D), F32),
                   jax.ShapeDtypeStruct(sm, BF16), jax.ShapeDtypeStruct(glu_s, BF16),
                   jax.ShapeDtypeStruct(sm, BF16), jax.ShapeDtypeStruct(sm, BF16),
                   jax.ShapeDtypeStruct((D, D), BF16)],
        compiler_params=_cparams(("arbitrary",)),
    )(l, za, yg, z, o, p, p, p, p, m, dxn, pv, wc, wglu, wp, ws, wo)


def dp_assemble(d_conv, d_ssm, d_pool, d_qkv, d_gates, tm=512):
    T = d_conv.shape[0]
    W = MIX_W

    def body(c_ref, s_ref, p_ref, q_ref, g_ref, dp_ref):
        dp_ref[:, 0:3 * W] = c_ref[...]
        dp_ref[:, 3 * W:4 * W] = s_ref[...]
        dp_ref[:, 4 * W:5 * W] = p_ref[...]
        dp_ref[:, 5 * W:8 * W] = q_ref[...]
        dp_ref[:, GATE_OFF:] = g_ref[...]

    row = lambda w: pl.BlockSpec((tm, w), lambda i: (i, 0))
    return pl.pallas_call(
        body, name="dp_assemble", grid=(T // tm,),
        in_specs=[row(3 * W), row(W), row(W), row(3 * W), row(4 * D_MODEL)],
        out_specs=row(IN_COLS),
        out_shape=jax.ShapeDtypeStruct((T, IN_COLS), BF16),
        compiler_params=_cparams(("arbitrary",)),
    )(d_conv, d_ssm, d_pool, d_qkv, d_gates)


def loss_head(y, target, tm=512):
    T, D = y.shape

    def body(y_ref, t_ref, dy_ref, loss_ref):
        @pl.when(pl.program_id(0) == 0)
        def _():
            loss_ref[...] = jnp.zeros_like(loss_ref)

        err = y_ref[...] - t_ref[...]
        dy_ref[...] = err * (1.0 / D)
        loss_ref[...] += jnp.sum(err * err) * (0.5 / D)

    tile = pl.BlockSpec((tm, D), lambda i: (i, 0))
    return pl.pallas_call(
        body, name="loss_head", grid=(T // tm,),
        in_specs=[tile, tile],
        out_specs=[tile, pl.BlockSpec((8, 128), lambda i: (0, 0))],
        out_shape=[jax.ShapeDtypeStruct((T, D), F32), jax.ShapeDtypeStruct((8, 128), F32)],
        compiler_params=_cparams(("arbitrary",)),
    )(y, target)


def cast_layer(ld, items):
    def body(ld_ref, *refs):
        n = len(refs) // 2
        for src, dst in zip(refs[:n], refs[n:]):
            dst[...] = src[...].astype(BF16)

    def shard(w, k):
        return w.shape[1:] if k is None else w.shape[2:]

    def in_spec(w, k):
        sh = shard(w, k)
        if k is None:
            return pl.BlockSpec((None,) + sh, lambda i, ld, n=len(sh): (ld[0],) + (0,) * n)
        return pl.BlockSpec((None, None) + sh, lambda i, ld, n=len(sh), k=k: (ld[0], k) + (0,) * n)

    def out_spec(w, k):
        sh = shard(w, k)
        return pl.BlockSpec((None, None) + sh, lambda i, ld, n=len(sh): (0, ld[1]) + (0,) * n)

    grid_spec = pltpu.PrefetchScalarGridSpec(
        num_scalar_prefetch=1, grid=(1,),
        in_specs=[in_spec(w, k) for w, k in items], out_specs=[out_spec(w, k) for w, k in items])
    return pl.pallas_call(
        body, name="cast_layer", grid_spec=grid_spec,
        out_shape=[jax.ShapeDtypeStruct((1, N_DEV) + shard(w, k), BF16) for w, k in items],
        compiler_params=_cparams(("arbitrary",)),
    )(ld, *[w for w, _ in items])


def place_own(dev, a):
    def body(dev_ref, a_ref, o_ref):
        o_ref[...] = a_ref[...]

    grid_spec = pltpu.PrefetchScalarGridSpec(
        num_scalar_prefetch=1, grid=(1,),
        in_specs=[pl.BlockSpec(a.shape, lambda i, dev: (0, 0))],
        out_specs=pl.BlockSpec((None,) + a.shape, lambda i, dev: (dev[0], 0, 0)))
    return pl.pallas_call(
        body, name="place_own", grid_spec=grid_spec,
        out_shape=jax.ShapeDtypeStruct((N_DEV,) + a.shape, a.dtype),
        compiler_params=_cparams(("arbitrary",)),
    )(dev, a)


def _silu(x):
    return x * _sigmoid(x)


def ada_fwd(c_all, w_ada, b_cols):
    L, D, n = w_ada.shape

    def body(c_ref, w_ref, b_ref, o_ref):
        c_act = _silu(c_ref[...]).astype(BF16)
        o_ref[...] = _dot(c_act, w_ref[...].astype(BF16)) + b_ref[...]

    return pl.pallas_call(
        body, name="ada_fwd", grid=(L,),
        in_specs=[pl.BlockSpec((N_DEV, D), lambda l: (0, 0)), pl.BlockSpec((None, D, n), lambda l: (l, 0, 0)),
                  pl.BlockSpec((None, 1, n), lambda l: (l, 0, 0))],
        out_specs=pl.BlockSpec((None, N_DEV, n), lambda l: (l, 0, 0)),
        out_shape=jax.ShapeDtypeStruct((L, N_DEV, n), F32),
        compiler_params=_cparams(("arbitrary",)),
    )(c_all, w_ada, b_cols)


def _adamw(w, g, m, v):
    m = ADAM_B1 * m + (1.0 - ADAM_B1) * g
    v = ADAM_B2 * v + (1.0 - ADAM_B2) * (g * g)
    m_hat = m / (1.0 - ADAM_B1 ** ADAM_STEP)
    v_hat = v / (1.0 - ADAM_B2 ** ADAM_STEP)
    delta = -ADAM_LR * (m_hat / (jnp.sqrt(v_hat) + ADAM_EPS) + ADAM_WD * w)
    return delta, m, v


def ada_update(c_all, dada_cols, w, m, v, rb=256):
    L, D, n = w.shape

    def body(c_ref, d_ref, w_ref, m_ref, v_ref, g_ref, dl_ref, nm_ref, nv_ref):
        c_act = _silu(c_ref[...]).astype(BF16)
        g = _dot_tn(c_act, d_ref[...].astype(BF16))
        g_ref[...] = g
        dl_ref[...], nm_ref[...], nv_ref[...] = _adamw(w_ref[...], g, m_ref[...], v_ref[...])

    blk = pl.BlockSpec((None, rb, n), lambda l, i: (l, i, 0))
    out = jax.ShapeDtypeStruct((L, D, n), F32)
    return pl.pallas_call(
        body, name="ada_update", grid=(L, D // rb),
        in_specs=[pl.BlockSpec((N_DEV, rb), lambda l, i: (0, i)),
                  pl.BlockSpec((None, N_DEV, n), lambda l, i: (l, 0, 0)), blk, blk, blk],
        out_specs=[blk, blk, blk, blk], out_shape=[out, out, out, out],
        compiler_params=_cparams(("arbitrary", "arbitrary")),
    )(c_all, dada_cols, w, m, v)


SUM_UPDATE_RECV_BYTES = 12 * 1024 * 1024


def sum_update(dev, first, recvs, owns, w, m, v, prev=None, after=None):
    n_slots, R, C = w.shape
    S = len(recvs)
    assert len(owns) == S and first + S <= n_slots
    rb_max = SUM_UPDATE_RECV_BYTES // (S * N_DEV * C * 2)
    rb = max(r for r in range(8, R + 1, 8) if R % r == 0 and (r <= rb_max or r == 8))
    last = R // rb - 1
    n_prev = 0 if prev is None else 4
    extra = list(prev or ()) + ([] if after is None else [after])

    def body(dev_ref, *refs):
        r_refs, o_refs = refs[:S], refs[S:2 * S]
        w_ref, m_ref, v_ref = refs[2 * S:2 * S + 3]
        g_ref, dl_ref, nm_ref, nv_ref = refs[2 * S + 3 + len(extra):]
        me = dev_ref[0]
        for s in range(S):
            @pl.when(pl.program_id(0) == s)
            def _(s=s):
                g = jnp.zeros((rb, C), F32)
                for d in range(N_DEV):
                    g += jnp.where(me == d, o_refs[s][...], r_refs[s][d]).astype(F32)
                g_ref[...] = g
                dl_ref[...], nm_ref[...], nv_ref[...] = _adamw(w_ref[...], g, m_ref[...], v_ref[...])

    def row(sl, i, s):
        return jnp.where(sl == s, i, jnp.where(sl < s, 0, last))

    def rspec(s):
        return pl.BlockSpec((N_DEV, rb, C), lambda sl, i, dev: (0, row(sl, i, s), 0))

    def ospec(s):
        return pl.BlockSpec((None, rb, C), lambda sl, i, dev: (dev[0], row(sl, i, s), 0))

    blk = pl.BlockSpec((None, rb, C), lambda sl, i, dev: (first + sl, i, 0))
    out = jax.ShapeDtypeStruct((n_slots, R, C), F32)
    grid_spec = pltpu.PrefetchScalarGridSpec(
        num_scalar_prefetch=1, grid=(S, R // rb),
        in_specs=[rspec(s) for s in range(S)] + [ospec(s) for s in range(S)] + [blk, blk, blk] + [ANY] * len(extra),
        out_specs=[blk, blk, blk, blk],
    )
    n_in = 1 + 2 * S + 3
    return pl.pallas_call(
        body, name="sum_update", grid_spec=grid_spec, out_shape=[out, out, out, out],
        input_output_aliases={n_in + i: i for i in range(n_prev)},
        compiler_params=_cparams(("arbitrary", "arbitrary")),
    )(dev, *recvs, *owns, w, m, v, *extra)


def small_sum(gathered):
    _, R, C = gathered.shape

    def body(g_ref, o_ref):
        acc = g_ref[0]
        for d in range(1, N_DEV):
            acc += g_ref[d]
        o_ref[...] = acc

    return pl.pallas_call(
        body, name="small_sum", grid=(1,),
        in_specs=[pl.BlockSpec((N_DEV, R, C), lambda i: (0, 0, 0))],
        out_specs=pl.BlockSpec((R, C), lambda i: (0, 0)),
        out_shape=jax.ShapeDtypeStruct((R, C), F32),
        compiler_params=_cparams(("arbitrary",)),
    )(gathered)


def small_update(w, g, m, v):
    def body(w_ref, g_ref, m_ref, v_ref, dl_ref, nm_ref, nv_ref):
        dl_ref[...], nm_ref[...], nv_ref[...] = _adamw(w_ref[...], g_ref[...], m_ref[...], v_ref[...])

    blk = pl.BlockSpec(w.shape, lambda i: (0, 0))
    out = jax.ShapeDtypeStruct(w.shape, F32)
    return pl.pallas_call(
        body, name="small_update", grid=(1,),
        in_specs=[blk] * 4, out_specs=[blk] * 3, out_shape=[out] * 3,
        compiler_params=_cparams(("arbitrary",)),
    )(w, g, m, v)


MESH = pl.DeviceIdType.MESH
ANY = pl.BlockSpec(memory_space=pl.ANY)


def _coords():
    return lax.axis_index("x"), lax.axis_index("y"), lax.axis_index("c")


def _dev_index(x, y, c):
    return 4 * x + 2 * y + c


def _at_dev(ref, p, dev):
    return ref.at[(slice(None),) * p + (dev,)]


def all_gather(arrays, ps):
    n = len(arrays)

    def body(*refs):
        ins, outs = refs[:n], refs[n:2 * n]
        send_sems, recv_sems, local_sems = refs[2 * n:]
        x, y, c = _coords()
        me, sibling = (x, y, c), (x, y, 1 - c)
        chips = [(1 - x, y), (x, 1 - y), (1 - x, 1 - y)]

        def copy(a, k, block, to, src=None):
            dst = _at_dev(outs[a], ps[a], _dev_index(*block))
            return pltpu.make_async_remote_copy(
                src_ref=dst if src is None else src, dst_ref=dst,
                send_sem=send_sems.at[a, k], recv_sem=recv_sems.at[a, k], device_id=to, device_id_type=MESH)

        mine = [pltpu.make_async_copy(ins[a], _at_dev(outs[a], ps[a], _dev_index(*me)), local_sems.at[a])
                for a in range(n)]
        for cp in mine:
            cp.start()
        first = []
        for a in range(n):
            first.append(copy(a, 0, me, sibling, src=ins[a]))
            first += [copy(a, 1 + j, me, (*chip, c), src=ins[a]) for j, chip in enumerate(chips)]
        for cp in first:
            cp.start()
        passed = []
        for j, chip in enumerate(chips):
            for a in range(n):
                copy(a, 1 + j, (*chip, c), me).wait_recv()
                fwd = copy(a, 4 + j, (*chip, c), sibling)
                fwd.start()
                passed.append(fwd)
        for a in range(n):
            copy(a, 0, sibling, me).wait_recv()
            for j, chip in enumerate(chips):
                copy(a, 4 + j, (*chip, 1 - c), me).wait_recv()
        for cp in first + passed:
            cp.wait_send()
        for cp in mine:
            cp.wait()

    out_shape = [jax.ShapeDtypeStruct(a.shape[:p] + (N_DEV,) + a.shape[p:], a.dtype) for a, p in zip(arrays, ps)]
    return pl.pallas_call(
        body, name="all_gather", in_specs=[ANY] * n, out_specs=[ANY] * n, out_shape=out_shape,
        scratch_shapes=[pltpu.SemaphoreType.DMA((n, 7)), pltpu.SemaphoreType.DMA((n, 7)),
                        pltpu.SemaphoreType.DMA((n,))],
        compiler_params=pltpu.CompilerParams(has_side_effects=True),
    )(*arrays)


HBM = pl.BlockSpec(memory_space=pltpu.HBM)
SEM = pl.BlockSpec(memory_space=pltpu.SEMAPHORE)
EFFECT = pltpu.SideEffectType.DATAFLOW_SIDE_EFFECTING


def _peers(x, y, c):
    out = []
    for k in range(1, N_DEV):
        out.append((1 - x if k & 4 else x, 1 - y if k & 2 else y, 1 - c if k & 1 else c))
    return out


def _exchange_plan(n):
    def plan(refs, x, y, c):
        blocks, lands = refs[:n], refs[n:2 * n]
        me = _dev_index(x, y, c)
        moves = []
        for peer in _peers(x, y, c):
            q = _dev_index(*peer)
            moves += [(blocks[a].at[q], lands[a].at[me], peer, lands[a].at[q]) for a in range(n)]
        return moves
    return plan


def _gather_plan(ps, second):
    def plan(refs, x, y, c):
        me, sibling = (x, y, c), (x, y, 1 - c)
        chips = [(1 - x, y), (x, 1 - y), (1 - x, 1 - y)]
        if second:
            trips = [((*ch, c), sibling, (*ch, 1 - c)) for ch in chips]
        else:
            trips = [(me, sibling, sibling)] + [(me, (*ch, c), (*ch, c)) for ch in chips]
        moves = []
        for sent, to, arriving in trips:
            for ref, p in zip(refs, ps):
                blk = _at_dev(ref, p, _dev_index(*sent))
                moves.append((blk, blk, to, _at_dev(ref, p, _dev_index(*arriving))))
        return moves
    return plan


def copies_start(name, plan, n_moves, arrays, carry):
    n = len(arrays)

    def body(*refs):
        sems = refs[n + 1:n + 1 + 2 * n_moves]
        moves = plan(refs[:n], *_coords())
        assert len(moves) == n_moves
        for i, (src, dst, to, _) in enumerate(moves):
            pltpu.make_async_remote_copy(src_ref=src, dst_ref=dst, send_sem=sems[i], recv_sem=sems[n_moves + i],
                                         device_id=to, device_id_type=MESH).start()

    operands = [pltpu.with_memory_space_constraint(a, pltpu.HBM) for a in list(arrays) + [carry]]
    outs = pl.pallas_call(
        body, name=name,
        out_shape=[pltpu.SemaphoreType.DMA(())] * (2 * n_moves) + [pltpu.HBM(a.shape, a.dtype) for a in operands],
        in_specs=[HBM] * (n + 1), out_specs=[SEM] * (2 * n_moves) + [HBM] * (n + 1),
        input_output_aliases={i: 2 * n_moves + i for i in range(n + 1)},
        compiler_params=pltpu.CompilerParams(has_side_effects=EFFECT),
    )(*operands)
    return outs[:n_moves], outs[n_moves:2 * n_moves], outs[2 * n_moves:-1], outs[-1]


def copies_wait(name, plan, send_sems, recv_sems, arrays, after):
    n, n_moves = len(arrays), len(send_sems)

    def body(*refs):
        sems = refs[n:n + 2 * n_moves]
        for i, (src, _, to, arriving) in enumerate(plan(refs[:n], *_coords())):
            cp = pltpu.make_async_remote_copy(src_ref=src, dst_ref=arriving, send_sem=sems[i],
                                              recv_sem=sems[n_moves + i], device_id=to, device_id_type=MESH)
            cp.wait_send()
            cp.wait_recv()

    return pl.pallas_call(
        body, name=name,
        out_shape=[pltpu.HBM(a.shape, a.dtype) for a in arrays],
        in_specs=[HBM] * n + [SEM] * (2 * n_moves) + [ANY], out_specs=[HBM] * n,
        input_output_aliases={i: i for i in range(n)},
        compiler_params=pltpu.CompilerParams(has_side_effects=EFFECT),
    )(*arrays, *send_sems, *recv_sems, after)


WEIGHT_NAMES = ("w_ada", "b_ada", "g_pre", "g_post", "w_ff_in", "w_ff_out", "w_in", "conv_w", "w_conv_out",
                "lam_re", "lam_im", "log_dt", "ssm_b_re", "ssm_b_im", "ssm_c_re", "ssm_c_im", "ssm_d", "w_glu",
                "w_pool", "pool_scale", "w_pool_out", "w_sb_out", "w_out")
BIG_NAMES = ("w_ff_in", "w_ff_out", "w_in", "w_conv_out", "w_glu", "w_pool_out", "w_sb_out", "w_out")
SMALL_NAMES = ("b_ada", "g_pre", "g_post", "conv_w", "lam_re", "lam_im", "log_dt", "ssm_b_re", "ssm_b_im",
               "ssm_c_re", "ssm_c_im", "ssm_d", "w_pool", "pool_scale")
PACK_LANES = 128
PACK_ROWS = 8


def _pack(arrays):
    flat = jnp.concatenate([a.reshape(-1) for a in arrays])
    unit = PACK_LANES * PACK_ROWS
    flat = jnp.pad(flat, (0, -flat.shape[0] % unit))
    return flat.reshape(-1, PACK_LANES)


def _unpack(pack, shapes):
    flat = pack.reshape(-1)
    out, off = [], 0
    for s in shapes:
        n = 1
        for d in s:
            n *= d
        out.append(flat[off:off + n].reshape(s))
        off += n
    return out


def _pad_rows(a, rows=8):
    return jnp.pad(a, ((0, 0), (0, rows - a.shape[1]), (0, 0)))


def _tile_b(b):
    L = b.shape[0]
    return jnp.tile(b.transpose(0, 3, 1, 2).reshape(L, SSM_GROUP, SSM_W), (1, SSM_GROUPS, 1))


def _tile_c(c):
    L = c.shape[0]
    return jnp.tile(c.transpose(0, 3, 1, 2).reshape(L, SSM_STATE, MIX_W), (1, SSM_GROUPS, 1))


def _step(x, c, target, W, M, V):
    T, D = x.shape[1], x.shape[2]
    L = W["w_ada"].shape[0]
    x = x[0]
    target = target[0]
    ax, ay, ac = _coords()
    dev = _dev_index(ax, ay, ac)
    n_ada = W["w_ada"].shape[2]

    dev_s = jnp.reshape(dev, (1,)).astype(jnp.int32)
    items = ([(W["w_ff_in"], 0), (W["w_ff_in"], 1), (W["w_ff_out"], 0), (W["w_ff_out"], 1)]
             + [(W[k], None) for k in BIG_NAMES[2:]])
    bufs = [list(cast_layer(jnp.concatenate([jnp.array([l], jnp.int32), dev_s]), items)) for l in range(L)]
    ffn1_w, mixer_w, ffn2_w = (0, 2), (4, 5, 6, 7, 8, 9), (1, 3)
    all_w = tuple(range(len(items)))

    def gather_start(tag, second, l, idx, carry):
        plan = _gather_plan((1,) * len(idx), second)
        n_moves = (3 if second else 4) * len(idx)
        s_sem, r_sem, arrs, carry = copies_start(f"gather_{'b' if second else 'a'}_start_{tag}", plan, n_moves,
                                                 [bufs[l][i] for i in idx], carry)
        for i, a in zip(idx, arrs):
            bufs[l][i] = a
        return (plan, s_sem, r_sem), carry

    def gather_wait(tag, second, l, idx, flight, after):
        arrs = copies_wait(f"gather_{'b' if second else 'a'}_wait_{tag}", *flight, [bufs[l][i] for i in idx], after)
        for i, a in zip(idx, arrs):
            bufs[l][i] = a

    def gather_finish(tag, l, idx, flight, after, carry):
        gather_wait(tag, False, l, idx, flight, after)
        flight, carry = gather_start(tag, True, l, idx, carry)
        gather_wait(tag, True, l, idx, flight, carry)
        return carry

    first = []
    for g, idx in enumerate((ffn1_w, mixer_w, ffn2_w)):
        flight, x = gather_start(f"0_{g}", False, 0, idx, x)
        first.append(flight)

    gathered = all_gather([W["g_pre"], W["g_post"], W["conv_w"], c], [0, 0, 0, 0])
    g_pre = gathered[0].transpose(1, 2, 0, 3).reshape(L, N_SUB, D)
    g_post = gathered[1].transpose(1, 2, 0, 3).reshape(L, N_SUB, D)
    conv_w = _pad_rows(gathered[2].transpose(1, 2, 0, 3).reshape(L, 3, MIX_W))
    c_all = gathered[3].reshape(N_DEV, D)

    b_cols = lax.dynamic_slice_in_dim(W["b_ada"], dev * n_ada, n_ada, axis=1)[:, None, :]
    ada_cols = ada_fwd(c_all, W["w_ada"], b_cols)
    ada_all = all_gather([ada_cols], [0])[0]
    ada = lax.dynamic_index_in_dim(ada_all, dev, axis=2, keepdims=False)
    ada = ada.transpose(1, 0, 2).reshape(L, N_SUB, 3, D)
    zeros = jnp.zeros((L, N_SUB, D), F32)
    pv_all = jnp.stack([g_pre, ada[:, :, 0], ada[:, :, 1], g_post, ada[:, :, 2], zeros, zeros, zeros], axis=2)

    lam = jnp.stack([W["lam_re"].reshape(L, SSM_W), W["lam_im"].reshape(L, SSM_W),
                     jnp.repeat(W["log_dt"], SSM_STATE, axis=1)], axis=1)
    lam = _pad_rows(lam)
    b_t = jnp.stack([_tile_b(W["ssm_b_re"]), _tile_b(W["ssm_b_im"])], axis=1)
    c_t = jnp.stack([_tile_c(W["ssm_c_re"]), _tile_c(W["ssm_c_im"])], axis=1)
    avec, b_bd, c_bd = s5_params(lam, b_t, c_t)
    ssm_d = _pad_rows(W["ssm_d"][:, None, :])
    pool_scale = _pad_rows(W["pool_scale"][:, None, :])
    eye4 = jnp.eye(len(POOL_WINDOWS), dtype=F32)
    w_bd = jnp.einsum("lgcd,gh->lgchd", W["w_pool"], eye4).reshape(L, MIX_W, MIX_W).astype(BF16)

    x = gather_finish("0_0", 0, ffn1_w, first[0], pv_all, x)

    def ffn_weights(l, k):
        b = bufs[l]
        return b[k].reshape(1, 1, 2, 4, D, FF_BLK), b[2 + k].reshape(1, 1, 4, FF_BLK, D)

    def mixer_weights(l):
        b = bufs[l]
        return b[4], b[5], b[6], b[7], b[8], b[9].reshape(1, D, D)

    l0 = jnp.array([0], jnp.int32)
    k0 = jnp.array([0, 0], jnp.int32)
    saved = []
    for l in range(L):
        li = jnp.array([l], jnp.int32)
        nxt = l + 1 < L
        if nxt:
            flight, x = gather_start(f"{l + 1}", False, l + 1, all_w, x)
        x0 = x
        ab0, f0, x1 = ffn_fwd(k0, x0, pv_all[l, 0], *ffn_weights(l, 0))
        if l == 0:
            x1 = gather_finish("0_1", 0, mixer_w, first[1], x1, x1)
        wg_in, wg_conv, wg_glu, wg_pool, wg_sb, wg_out = mixer_weights(l)
        p = mix_in_fwd(l0, x1, pv_all[l, 1], wg_in)
        za = conv_fwd(li, p, conv_w)
        z = pool_fwd(li, p, w_bd, pool_scale)
        s = s5_scan(li, avec, s5_bu(li, p, b_bd), False)
        yg = s5_out(li, p, s, c_bd, ssm_d)
        o, sb_tot = sb_fwd(p)
        x2, m = merge_fwd(l0, p, za, yg, z, o, x1, pv_all[l, 1], wg_conv, wg_glu, wg_pool, wg_sb, wg_out)
        if l == 0:
            x2 = gather_finish("0_2", 0, ffn2_w, first[2], x2, x2)
        if nxt:
            gather_wait(f"{l + 1}", False, l + 1, all_w, flight, x2)
            flight, x2 = gather_start(f"{l + 1}", True, l + 1, all_w, x2)
        ab1, f1, x = ffn_fwd(k0, x2, pv_all[l, 2], *ffn_weights(l, 1))
        if nxt:
            gather_wait(f"{l + 1}", True, l + 1, all_w, flight, x)
        saved.append((x0, ab0, f0, x1, p, za, z, s, yg, o, sb_tot, m, x2, ab1, f1))

    dx, loss_blk = loss_head(x, target)
    loss = lax.psum(loss_blk[0, 0], ("x", "y", "c"))

    n_blocks = 10
    ffn2_g, mixer_g, ffn1_g = (1, 3), (4, 5, 6, 7, 8, 9), (0, 2)
    recvs, owns, in_flight = [[None] * n_blocks for _ in range(L)], [[None] * n_blocks for _ in range(L)], []

    def exchange_start(tag, layer, idx, blocks, carry):
        n = len(idx)
        plan = _exchange_plan(n)
        arrays = list(blocks) + [lax.empty(a.shape, a.dtype) for a in blocks]
        s_sem, r_sem, arrays, carry = copies_start(f"exchange_start_{tag}", plan, (N_DEV - 1) * n, arrays, carry)
        in_flight.append((tag, layer, idx, plan, s_sem, r_sem, arrays))
        return carry

    def settle(after, upto):
        for flight in [f for f in in_flight if f[1] >= upto]:
            in_flight.remove(flight)
            tag, layer, idx, plan, s_sem, r_sem, arrays = flight
            arrays = copies_wait(f"exchange_wait_{tag}", plan, s_sem, r_sem, arrays, after)
            for j, i in enumerate(idx):
                owns[layer][i], recvs[layer][i] = arrays[j], arrays[len(idx) + j]

    pgs = [None] * L
    small = {k: [None] * L for k in ("conv_w", "w_bd", "pool_scale", "ssm_d", "gb", "gc", "da")}
    for l in reversed(range(L)):
        li = jnp.array([l], jnp.int32)
        x0, ab0, f0, x1, p, za, z, s, yg, o, sb_tot, m, x2, ab1, f1 = saved[l]
        wg_in, wg_conv, wg_glu, wg_pool, wg_sb, wg_out = mixer_weights(l)
        g_in1, g_out1, dx, pg2 = ffn_bwd(k0, dx, x2, f1, pv_all[l, 2], ab1, *ffn_weights(l, 1))
        dx = exchange_start(f"{l}_ffn2", l, ffn2_g,
                            [g_in1.reshape(N_DEV, D, FF_BLK), g_out1.reshape(N_DEV, D_FF // N_DEV, D)], dx)
        (dza, dyg, dz, do, dgates, pg1m, g_conv, g_glu, g_pool, g_sb, g_wo) = merge_bwd(
            l0, p, za, yg, z, o, m, dx, pv_all[l, 1], wg_conv, wg_glu, wg_pool, wg_sb, wg_out)
        d_conv, small["conv_w"][l] = conv_bwd(li, p, dza, conv_w)
        d_pool, small["w_bd"][l], small["pool_scale"][l] = pool_bwd(li, p, dz, w_bd, pool_scale)
        ds, du_skip, small["gc"][l], small["ssm_d"][l] = s5_bwd_y(li, p, s, dyg, c_bd, ssm_d)
        lam_s = s5_scan(li, avec, ds, True)
        d_ssm, small["gb"][l] = s5_bwd_u(li, p, lam_s, du_skip, b_bd)
        small["da"][l] = s5_bwd_a(s, lam_s)
        d_qkv = sb_bwd(p, do, sb_tot)
        dp = dp_assemble(d_conv, d_ssm, d_pool, d_qkv, dgates)
        dx, h, pg1i = mix_in_bwd_act(l0, dp, dx, x1, pv_all[l, 1], wg_in)
        g_win = matmul_tn(h, dp, IN_BLK)
        settle(dx, l + 1)
        dx = exchange_start(f"{l}_mixer", l, mixer_g,
                            [g_win, g_conv, g_glu, g_pool, g_sb, g_wo.reshape(N_DEV, D // N_DEV, D)], dx)
        g_in0, g_out0, dx, pg0 = ffn_bwd(k0, dx, x0, f0, pv_all[l, 0], ab0, *ffn_weights(l, 0))
        pgs[l] = jnp.stack([pg0, pg1m + pg1i, pg2])
        last_ffn1 = [g_in0.reshape(N_DEV, D, FF_BLK), g_out0.reshape(N_DEV, D_FF // N_DEV, D)]
        if l > 0:
            dx = exchange_start(f"{l}_ffn1", l, ffn1_g, last_ffn1, dx)

    dlam, db_t, dc_t, dldt = s5_params_bwd(lam, b_t, jnp.stack(small["gb"]), jnp.stack(small["gc"]),
                                           jnp.stack(small["da"]))
    pg = jnp.stack(pgs)
    d_ada = jnp.stack([pg[:, :, PV_SHIFT], pg[:, :, PV_SCALE], pg[:, :, PV_GATE]], axis=2).reshape(L, N_SUB * 3 * D)
    db = db_t.reshape(L, 2, SSM_GROUPS, SSM_GROUP, SSM_GROUPS, SSM_STATE)
    db = jnp.einsum("lrghgp->lrgph", db)
    dc = dc_t.reshape(L, 2, SSM_GROUPS, SSM_STATE, SSM_GROUPS, SSM_GROUP)
    dc = jnp.einsum("lrgpgh->lrghp", dc)
    d_wpool = jnp.einsum("lgcgd->lgcd", jnp.stack(small["w_bd"]).reshape(L, 4, 64, 4, 64))
    contrib = {
        "b_ada": d_ada, "g_pre": pg[:, :, PV_GPRE], "g_post": pg[:, :, PV_GPOST],
        "conv_w": jnp.stack(small["conv_w"])[:, :3], "lam_re": dlam[:, 0].reshape(L, SSM_GROUPS, SSM_STATE),
        "lam_im": dlam[:, 1].reshape(L, SSM_GROUPS, SSM_STATE), "log_dt": dldt[:, 2, :SSM_GROUPS],
        "ssm_b_re": db[:, 0], "ssm_b_im": db[:, 1], "ssm_c_re": dc[:, 0], "ssm_c_im": dc[:, 1],
        "ssm_d": jnp.stack(small["ssm_d"])[:, 0], "w_pool": d_wpool,
        "pool_scale": jnp.stack(small["pool_scale"])[:, 0],
    }
    contrib_shapes = [contrib[k].shape for k in SMALL_NAMES]

    big_idx = {"w_ff_in": (0, 1), "w_ff_out": (2, 3), "w_in": (4,), "w_conv_out": (5,), "w_glu": (6,),
               "w_pool_out": (7,), "w_sb_out": (8,), "w_out": (9,)}

    def big(name, layers, prev, after=None):
        idx = big_idx[name]
        flat = (-1,) + W[name].shape[-2:]
        return sum_update(dev_s, layers[0] * len(idx), [recvs[l][i] for l in layers for i in idx],
                          [owns[l][i] for l in layers for i in idx],
                          W[name].reshape(flat), M[name].reshape(flat), V[name].reshape(flat), prev, after)

    partial = {}

    def partial_updates(names, after):
        for name in names:
            if L > 1:
                partial[name] = big(name, list(range(1, L)), None, after)
                after = partial[name][0]
        return after

    pack_buf = [place_own(dev_s, _pack([contrib[k] for k in SMALL_NAMES]))]
    plan_a, plan_b = _gather_plan((0,), False), _gather_plan((0,), True)
    s_sem, r_sem, pack_buf, dx = copies_start("small_gather_a_start", plan_a, 4, pack_buf, dx)
    dx = exchange_start("0_ffn1", 0, ffn1_g, last_ffn1, dx)
    after = partial_updates(BIG_NAMES[:1], dx)
    pack_buf = copies_wait("small_gather_a_wait", plan_a, s_sem, r_sem, pack_buf, after)
    s_sem, r_sem, pack_buf, dx = copies_start("small_gather_b_start", plan_b, 3, pack_buf, dx)
    after = partial_updates(BIG_NAMES[1:], dx)
    pack_all = copies_wait("small_gather_b_wait", plan_b, s_sem, r_sem, pack_buf, after)[0]
    total = dict(zip(SMALL_NAMES, _unpack(small_sum(pack_all), contrib_shapes)))
    d_ada_all = pack_all.reshape(N_DEV, -1)[:, :L * N_SUB * 3 * D].reshape(N_DEV, L, N_SUB * 3 * D)
    dada_cols = lax.dynamic_slice_in_dim(d_ada_all, dev * n_ada, n_ada, axis=2).transpose(1, 0, 2)
    n_g = D // N_DEV
    grads = {}
    for k in SMALL_NAMES:
        g = total[k]
        if k in ("g_pre", "g_post"):
            g = lax.dynamic_slice_in_dim(g, dev * n_g, n_g, axis=2)
        elif k == "conv_w":
            g = lax.dynamic_slice_in_dim(g, dev * (MIX_W // N_DEV), MIX_W // N_DEV, axis=2)
        grads[k] = g

    delta, new_m, new_v = {}, {}, {}
    shapes = [W[k].shape for k in SMALL_NAMES]
    dl, nm, nv = small_update(_pack([W[k] for k in SMALL_NAMES]), _pack([grads[k] for k in SMALL_NAMES]),
                              _pack([M[k] for k in SMALL_NAMES]), _pack([V[k] for k in SMALL_NAMES]))
    for k, a, b, cc in zip(SMALL_NAMES, _unpack(dl, shapes), _unpack(nm, shapes), _unpack(nv, shapes)):
        delta[k], new_m[k], new_v[k] = a, b, cc
    grads["w_ada"], delta["w_ada"], new_m["w_ada"], new_v["w_ada"] = ada_update(
        c_all, dada_cols, W["w_ada"], M["w_ada"], V["w_ada"])

    settle(new_m["w_ada"], 0)
    for name in BIG_NAMES:
        outs = big(name, [0], partial.get(name))
        grads[name], delta[name], new_m[name], new_v[name] = [o.reshape(W[name].shape) for o in outs]

    return (loss, dx[None], *[grads[k] for k in WEIGHT_NAMES], *[delta[k] for k in WEIGHT_NAMES],
            *[new_m[k] for k in WEIGHT_NAMES], *[new_v[k] for k in WEIGHT_NAMES])


def kernel(x, c, w_ada, b_ada, g_pre, g_post, w_ff_in, w_ff_out, w_in, conv_w, w_conv_out, lam_re, lam_im, log_dt, ssm_b_re, ssm_b_im, ssm_c_re, ssm_c_im, ssm_d, w_glu, w_pool, pool_scale, w_pool_out, w_sb_out, w_out, loss_target, m_w_ada, m_b_ada, m_g_pre, m_g_post, m_w_ff_in, m_w_ff_out, m_w_in, m_conv_w, m_w_conv_out, m_lam_re, m_lam_im, m_log_dt, m_ssm_b_re, m_ssm_b_im, m_ssm_c_re, m_ssm_c_im, m_ssm_d, m_w_glu, m_w_pool, m_pool_scale, m_w_pool_out, m_w_sb_out, m_w_out, v_w_ada, v_b_ada, v_g_pre, v_g_post, v_w_ff_in, v_w_ff_out, v_w_in, v_conv_w, v_w_conv_out, v_lam_re, v_lam_im, v_log_dt, v_ssm_b_re, v_ssm_b_im, v_ssm_c_re, v_ssm_c_im, v_ssm_d, v_w_glu, v_w_pool, v_pool_scale, v_w_pool_out, v_w_sb_out, v_w_out):
    w = (w_ada, b_ada, g_pre, g_post, w_ff_in, w_ff_out, w_in, conv_w, w_conv_out, lam_re, lam_im, log_dt, ssm_b_re, ssm_b_im, ssm_c_re, ssm_c_im, ssm_d, w_glu, w_pool, pool_scale, w_pool_out, w_sb_out, w_out)
    m = (m_w_ada, m_b_ada, m_g_pre, m_g_post, m_w_ff_in, m_w_ff_out, m_w_in, m_conv_w, m_w_conv_out, m_lam_re, m_lam_im, m_log_dt, m_ssm_b_re, m_ssm_b_im, m_ssm_c_re, m_ssm_c_im, m_ssm_d, m_w_glu, m_w_pool, m_pool_scale, m_w_pool_out, m_w_sb_out, m_w_out)
    v = (v_w_ada, v_b_ada, v_g_pre, v_g_post, v_w_ff_in, v_w_ff_out, v_w_in, v_conv_w, v_w_conv_out, v_lam_re, v_lam_im, v_log_dt, v_ssm_b_re, v_ssm_b_im, v_ssm_c_re, v_ssm_c_im, v_ssm_d, v_w_glu, v_w_pool, v_pool_scale, v_w_pool_out, v_w_sb_out, v_w_out)
    return _step(x, c, loss_target, dict(zip(WEIGHT_NAMES, w)), dict(zip(WEIGHT_NAMES, m)), dict(zip(WEIGHT_NAMES, v)))
```

```python
import functools

import jax
import jax.numpy as jnp
from jax import lax
from jax.experimental import pallas as pl
from jax.experimental.pallas import tpu as pltpu

F32 = jnp.float32
BF16 = jnp.bfloat16

N_DEV = 8
D_MODEL = 1024
D_FF = 2816
FF_BLK = D_FF // 4
N_SUB = 3
MIX_W = 256
IN_COLS = 6144
IN_BLK = IN_COLS // N_DEV
GATE_OFF = 2048
SSM_GROUPS, SSM_GROUP, SSM_STATE = 16, 16, 64
SSM_W = SSM_GROUPS * SSM_STATE
POOL_WINDOWS = (2, 4, 8, 16)
SB_HEAD = 64
EPS = 1e-6
DT_LAMBDA_RE_MAX = -1e-4
ADAM_LR, ADAM_B1, ADAM_B2, ADAM_EPS, ADAM_WD, ADAM_STEP = 0.001, 0.9, 0.999, 1e-08, 0.01, 10

VMEM_LIMIT = 56 * 1024 * 1024
FFN_BWD_VMEM_LIMIT = 60 * 1024 * 1024

PV_GPRE, PV_SHIFT, PV_SCALE, PV_GPOST, PV_GATE = 0, 1, 2, 3, 4


def _cparams(sem):
    return pltpu.CompilerParams(dimension_semantics=sem, vmem_limit_bytes=VMEM_LIMIT)


def _dot(a, b):
    return jnp.dot(a, b, preferred_element_type=F32)


def _dot_nt(a, b):
    return lax.dot_general(a, b, (((1,), (1,)), ((), ())), preferred_element_type=F32)


def _dot_tn(a, b):
    return lax.dot_general(a, b, (((0,), (0,)), ((), ())), preferred_element_type=F32)


def _rms(x):
    r = lax.rsqrt(jnp.mean(x * x, axis=-1, keepdims=True) + EPS)
    return x * r, r


def _rms_bwd(dn, n, r):
    return r * (dn - n * jnp.mean(dn * n, axis=-1, keepdims=True))


def _sigmoid(x):
    return 1.0 / (1.0 + jnp.exp(-x))


def _colsum(x):
    return jnp.sum(x, axis=0, keepdims=True)


def _prenorm(x, pv_ref):
    n, r = _rms(x)
    hn = n * pv_ref[PV_GPRE:PV_GPRE + 1, :]
    h = hn * (1.0 + pv_ref[PV_SCALE:PV_SCALE + 1, :]) + pv_ref[PV_SHIFT:PV_SHIFT + 1, :]
    return h, n, r, hn


def _prenorm_bwd(dh, dxn, x, pv_ref, pg_ref):
    _, n, r, hn = _prenorm(x, pv_ref)
    pg_ref[PV_SHIFT:PV_SHIFT + 1, :] += _colsum(dh)
    pg_ref[PV_SCALE:PV_SCALE + 1, :] += _colsum(dh * hn)
    dhn = dh * (1.0 + pv_ref[PV_SCALE:PV_SCALE + 1, :])
    pg_ref[PV_GPRE:PV_GPRE + 1, :] += _colsum(dhn * n)
    dn = dhn * pv_ref[PV_GPRE:PV_GPRE + 1, :]
    return dxn + _rms_bwd(dn, n, r)


def _postnorm_res(x, f, pv_ref, coef):
    nf, _ = _rms(f)
    return x + (coef * (1.0 + pv_ref[PV_GATE:PV_GATE + 1, :])) * (nf * pv_ref[PV_GPOST:PV_GPOST + 1, :])


def _postnorm_bwd(dxn, f, pv_ref, pg_ref, coef):
    nf, rf = _rms(f)
    g_post = pv_ref[PV_GPOST:PV_GPOST + 1, :]
    pg_ref[PV_GATE:PV_GATE + 1, :] += _colsum(dxn * (nf * g_post)) * coef
    dnfg = dxn * (coef * (1.0 + pv_ref[PV_GATE:PV_GATE + 1, :]))
    pg_ref[PV_GPOST:PV_GPOST + 1, :] += _colsum(dnfg * nf)
    return _rms_bwd(dnfg * g_post, nf, rf)


def ffn_fwd(lk, x, pv, wg_in, wg_out, tm=512):
    T, D = x.shape
    tm = min(tm, T)
    nj, ni = 4, T // tm

    def body(lk_ref, x_ref, pv_ref, win_ref, wout_ref, ab_ref, f_ref, xn_ref, h_all, acc_all):
        j, i = pl.program_id(0), pl.program_id(1)
        rows = pl.ds(pl.multiple_of(i * tm, tm), tm)

        @pl.when(j == 0)
        def _():
            h, _, _, _ = _prenorm(x_ref[...], pv_ref)
            h_all[rows, :] = h.astype(BF16)
            acc_all[rows, :] = jnp.zeros((tm, D), F32)

        h = h_all[rows, :]
        a = _dot(h, win_ref[0])
        b = _dot(h, win_ref[1])
        ab_ref[0] = a.astype(BF16)
        ab_ref[1] = b.astype(BF16)
        act = (a * _sigmoid(a) * b).astype(BF16)
        acc_all[rows, :] += _dot(act, wout_ref[...])

        @pl.when(j == nj - 1)
        def _():
            f = acc_all[rows, :]
            f_ref[...] = f
            xn_ref[...] = _postnorm_res(x_ref[...], f, pv_ref, 0.5)

    ends = lambda j, i, lk: (jnp.where((j == 0) | (j == nj - 1), i, ni - 1), 0)
    last = lambda j, i, lk: (jnp.where(j == nj - 1, i, 0), 0)
    grid_spec = pltpu.PrefetchScalarGridSpec(
        num_scalar_prefetch=1, grid=(nj, ni),
        in_specs=[
            pl.BlockSpec((tm, D), ends),
            pl.BlockSpec((8, D), lambda j, i, lk: (0, 0)),
            pl.BlockSpec((None, None, 2, None, D, FF_BLK), lambda j, i, lk: (lk[0], lk[1], 0, j, 0, 0)),
            pl.BlockSpec((None, None, None, FF_BLK, D), lambda j, i, lk: (lk[0], lk[1], j, 0, 0)),
        ],
        out_specs=[
            pl.BlockSpec((2, None, tm, FF_BLK), lambda j, i, lk: (0, j, i, 0)),
            pl.BlockSpec((tm, D), last),
            pl.BlockSpec((tm, D), last),
        ],
        scratch_shapes=[pltpu.VMEM((T, D), BF16), pltpu.VMEM((T, D), F32)],
    )
    return pl.pallas_call(
        body, name="ffn_fwd", grid_spec=grid_spec,
        out_shape=[jax.ShapeDtypeStruct((2, nj, T, FF_BLK), BF16),
                   jax.ShapeDtypeStruct((T, D), F32), jax.ShapeDtypeStruct((T, D), F32)],
        compiler_params=_cparams(("arbitrary", "arbitrary")),
    )(lk, x, pv, wg_in, wg_out)


def ffn_bwd_act(lk, dxn, x, f, pv, ab, wg_in, wg_out, tm=512):
    T, D = x.shape
    tm = min(tm, T)
    nj = 4

    def body(lk_ref, dxn_ref, x_ref, f_ref, pv_ref, ab_ref, win_ref, wout_ref,
             dab_ref, h_ref, df_ref, dx_ref, pg_ref, dacc):
        i, j = pl.program_id(0), pl.program_id(1)

        @pl.when((i == 0) & (j == 0))
        def _():
            pg_ref[...] = jnp.zeros_like(pg_ref)

        @pl.when(j == 0)
        def _():
            df = _postnorm_bwd(dxn_ref[...], f_ref[...], pv_ref, pg_ref, 0.5)
            df_ref[...] = df.astype(BF16)
            h, _, _, _ = _prenorm(x_ref[...], pv_ref)
            h_ref[...] = h.astype(BF16)
            dacc[...] = jnp.zeros_like(dacc)

        dact = _dot_nt(df_ref[...], wout_ref[...])
        a = ab_ref[0].astype(F32)
        b = ab_ref[1].astype(F32)
        sig = _sigmoid(a)
        s = a * sig
        da = (dact * b * (sig * (1.0 + a * (1.0 - sig)))).astype(BF16)
        db = (dact * s).astype(BF16)
        dab_ref[0] = da
        dab_ref[1] = db
        dacc[...] += _dot_nt(da, win_ref[0]) + _dot_nt(db, win_ref[1])

        @pl.when(j == nj - 1)
        def _():
            dx_ref[...] = _prenorm_bwd(dacc[...], dxn_ref[...], x_ref[...], pv_ref, pg_ref)

    tile = pl.BlockSpec((tm, D), lambda i, j, lk: (i, 0))
    blk = pl.BlockSpec((2, None, tm, FF_BLK), lambda i, j, lk: (0, j, i, 0))
    grid_spec = pltpu.PrefetchScalarGridSpec(
        num_scalar_prefetch=1, grid=(T // tm, nj),
        in_specs=[tile, tile, tile, pl.BlockSpec((8, D), lambda i, j, lk: (0, 0)), blk,
                  pl.BlockSpec((None, None, 2, None, D, FF_BLK), lambda i, j, lk: (lk[0], lk[1], 0, j, 0, 0)),
                  pl.BlockSpec((None, None, None, FF_BLK, D), lambda i, j, lk: (lk[0], lk[1], j, 0, 0))],
        out_specs=[blk, tile, tile, tile, pl.BlockSpec((8, D), lambda i, j, lk: (0, 0))],
        scratch_shapes=[pltpu.VMEM((tm, D), F32)],
    )
    return pl.pallas_call(
        body, name="ffn_bwd_act", grid_spec=grid_spec,
        out_shape=[jax.ShapeDtypeStruct((2, nj, T, FF_BLK), BF16), jax.ShapeDtypeStruct((T, D), BF16),
                   jax.ShapeDtypeStruct((T, D), BF16), jax.ShapeDtypeStruct((T, D), F32),
                   jax.ShapeDtypeStruct((8, D), F32)],
        compiler_params=_cparams(("arbitrary", "arbitrary")),
    )(lk, dxn, x, f, pv, ab, wg_in, wg_out)


def ffn_bwd(lk, dxn, x, f, pv, ab, wg_in, wg_out, tm=256):
    T, D = x.shape
    tm = min(tm, T)
    nj, ni = 4, T // tm

    def body(lk_ref, dxn_ref, x_ref, f_ref, pv_ref, ab_ref, win_ref, wout_ref,
             gin_ref, gout_ref, dx_ref, pg_ref, df_all, h_all, dh_all, acc_in, acc_out):
        j, i = pl.program_id(0), pl.program_id(1)
        rows = pl.ds(pl.multiple_of(i * tm, tm), tm)

        @pl.when((i == 0) & (j == 0))
        def _():
            pg_ref[...] = jnp.zeros_like(pg_ref)

        @pl.when(j == 0)
        def _():
            df = _postnorm_bwd(dxn_ref[...], f_ref[...], pv_ref, pg_ref, 0.5)
            df_all[rows, :] = df.astype(BF16)
            h, _, _, _ = _prenorm(x_ref[...], pv_ref)
            h_all[rows, :] = h.astype(BF16)
            dh_all[rows, :] = jnp.zeros((tm, D), F32)

        @pl.when(i == 0)
        def _():
            acc_in[...] = jnp.zeros_like(acc_in)
            acc_out[...] = jnp.zeros_like(acc_out)

        df_t = df_all[rows, :]
        h_t = h_all[rows, :]
        dact = _dot_nt(df_t, wout_ref[...])
        a = ab_ref[0].astype(F32)
        b = ab_ref[1].astype(F32)
        sig = _sigmoid(a)
        s = a * sig
        da = (dact * b * (sig * (1.0 + a * (1.0 - sig)))).astype(BF16)
        db = (dact * s).astype(BF16)
        act = (s * b).astype(BF16)
        dh_all[rows, :] += _dot_nt(da, win_ref[0]) + _dot_nt(db, win_ref[1])
        acc_in[0] += _dot_tn(h_t, da)
        acc_in[1] += _dot_tn(h_t, db)
        acc_out[...] += _dot_tn(act, df_t)

        @pl.when(i == ni - 1)
        def _():
            gin_ref[...] = acc_in[...].astype(BF16)
            gout_ref[...] = acc_out[...].astype(BF16)

        @pl.when(j == nj - 1)
        def _():
            dx_ref[...] = _prenorm_bwd(dh_all[rows, :], dxn_ref[...], x_ref[...], pv_ref, pg_ref)

    ends = lambda j, i, lk: (jnp.where((j == 0) | (j == nj - 1), i, ni - 1), 0)
    first = lambda j, i, lk: (jnp.where(j == 0, i, ni - 1), 0)
    grid_spec = pltpu.PrefetchScalarGridSpec(
        num_scalar_prefetch=1, grid=(nj, ni),
        in_specs=[pl.BlockSpec((tm, D), ends), pl.BlockSpec((tm, D), ends), pl.BlockSpec((tm, D), first),
                  pl.BlockSpec((8, D), lambda j, i, lk: (0, 0)),
                  pl.BlockSpec((2, None, tm, FF_BLK), lambda j, i, lk: (0, j, i, 0)),
                  pl.BlockSpec((None, None, 2, None, D, FF_BLK), lambda j, i, lk: (lk[0], lk[1], 0, j, 0, 0)),
                  pl.BlockSpec((None, None, None, FF_BLK, D), lambda j, i, lk: (lk[0], lk[1], j, 0, 0))],
        out_specs=[pl.BlockSpec((2, None, D, FF_BLK), lambda j, i, lk: (0, j, 0, 0)),
                   pl.BlockSpec((None, FF_BLK, D), lambda j, i, lk: (j, 0, 0)),
                   pl.BlockSpec((tm, D), lambda j, i, lk: (jnp.where(j == nj - 1, i, 0), 0)),
                   pl.BlockSpec((8, D), lambda j, i, lk: (0, 0))],
        scratch_shapes=[pltpu.VMEM((T, D), BF16), pltpu.VMEM((T, D), BF16), pltpu.VMEM((T, D), F32),
                        pltpu.VMEM((2, D, FF_BLK), F32), pltpu.VMEM((FF_BLK, D), F32)],
    )
    return pl.pallas_call(
        body, name="ffn_bwd", grid_spec=grid_spec,
        out_shape=[jax.ShapeDtypeStruct((2, nj, D, FF_BLK), BF16), jax.ShapeDtypeStruct((nj, FF_BLK, D), BF16),
                   jax.ShapeDtypeStruct((T, D), F32), jax.ShapeDtypeStruct((8, D), F32)],
        compiler_params=pltpu.CompilerParams(dimension_semantics=("arbitrary", "arbitrary"),
                                             vmem_limit_bytes=FFN_BWD_VMEM_LIMIT),
    )(lk, dxn, x, f, pv, ab, wg_in, wg_out)


def ffn_bwd_w(h, df, ab, dab, tm=1024):
    T, D = h.shape
    tm = min(tm, T)
    nj, ni = 4, T // tm

    def body(h_ref, df_ref, ab_ref, dab_ref, gin_ref, gout_ref, acc_in, acc_out):
        i = pl.program_id(1)

        @pl.when(i == 0)
        def _():
            acc_in[...] = jnp.zeros_like(acc_in)
            acc_out[...] = jnp.zeros_like(acc_out)

        h = h_ref[...]
        acc_in[0] += _dot_tn(h, dab_ref[0])
        acc_in[1] += _dot_tn(h, dab_ref[1])
        a = ab_ref[0].astype(F32)
        b = ab_ref[1].astype(F32)
        act = (a * _sigmoid(a) * b).astype(BF16)
        acc_out[...] += _dot_tn(act, df_ref[...])

        @pl.when(i == ni - 1)
        def _():
            gin_ref[...] = acc_in[...].astype(BF16)
            gout_ref[...] = acc_out[...].astype(BF16)

    tile = pl.BlockSpec((tm, D), lambda j, i: (i, 0))
    blk = pl.BlockSpec((2, None, tm, FF_BLK), lambda j, i: (0, j, i, 0))
    return pl.pallas_call(
        body, name="ffn_bwd_w", grid=(nj, ni),
        in_specs=[tile, tile, blk, blk],
        out_specs=[pl.BlockSpec((2, None, D, FF_BLK), lambda j, i: (0, j, 0, 0)),
                   pl.BlockSpec((None, FF_BLK, D), lambda j, i: (j, 0, 0))],
        out_shape=[jax.ShapeDtypeStruct((2, nj, D, FF_BLK), BF16),
                   jax.ShapeDtypeStruct((nj, FF_BLK, D), BF16)],
        scratch_shapes=[pltpu.VMEM((2, D, FF_BLK), F32), pltpu.VMEM((FF_BLK, D), F32)],
        compiler_params=_cparams(("arbitrary", "arbitrary")),
    )(h, df, ab, dab)


def mix_in_fwd(l, x, pv, wg, tm=1024):
    T, D = x.shape
    tm = min(tm, T)

    def body(l_ref, x_ref, pv_ref, w_ref, p_ref, h_sc):
        @pl.when(pl.program_id(1) == 0)
        def _():
            h, _, _, _ = _prenorm(x_ref[...], pv_ref)
            h_sc[...] = h.astype(BF16)

        p_ref[...] = _dot(h_sc[...], w_ref[...])

    grid_spec = pltpu.PrefetchScalarGridSpec(
        num_scalar_prefetch=1, grid=(T // tm, N_DEV),
        in_specs=[pl.BlockSpec((tm, D), lambda i, j, l: (i, 0)),
                  pl.BlockSpec((8, D), lambda i, j, l: (0, 0)),
                  pl.BlockSpec((None, None, D, IN_BLK), lambda i, j, l: (l[0], j, 0, 0))],
        out_specs=pl.BlockSpec((tm, IN_BLK), lambda i, j, l: (i, j)),
        scratch_shapes=[pltpu.VMEM((tm, D), BF16)],
    )
    return pl.pallas_call(
        body, name="mix_in_fwd", grid_spec=grid_spec,
        out_shape=jax.ShapeDtypeStruct((T, IN_COLS), F32),
        compiler_params=_cparams(("arbitrary", "arbitrary")),
    )(l, x, pv, wg)


def mix_in_bwd_act(l, dp, dxn, x, pv, wg, tm=1024):
    T, D = x.shape
    tm = min(tm, T)

    def body(l_ref, dp_ref, dxn_ref, x_ref, pv_ref, w_ref, dx_ref, h_ref, pg_ref, dacc):
        i, j = pl.program_id(0), pl.program_id(1)

        @pl.when((i == 0) & (j == 0))
        def _():
            pg_ref[...] = jnp.zeros_like(pg_ref)

        @pl.when(j == 0)
        def _():
            dacc[...] = jnp.zeros_like(dacc)

        dacc[...] += _dot_nt(dp_ref[...], w_ref[...])

        @pl.when(j == N_DEV - 1)
        def _():
            h, _, _, _ = _prenorm(x_ref[...], pv_ref)
            h_ref[...] = h.astype(BF16)
            dx_ref[...] = _prenorm_bwd(dacc[...], dxn_ref[...], x_ref[...], pv_ref, pg_ref)

    tile = pl.BlockSpec((tm, D), lambda i, j, l: (i, 0))
    grid_spec = pltpu.PrefetchScalarGridSpec(
        num_scalar_prefetch=1, grid=(T // tm, N_DEV),
        in_specs=[pl.BlockSpec((tm, IN_BLK), lambda i, j, l: (i, j)), tile, tile,
                  pl.BlockSpec((8, D), lambda i, j, l: (0, 0)),
                  pl.BlockSpec((None, None, D, IN_BLK), lambda i, j, l: (l[0], j, 0, 0))],
        out_specs=[tile, tile, pl.BlockSpec((8, D), lambda i, j, l: (0, 0))],
        scratch_shapes=[pltpu.VMEM((tm, D), F32)],
    )
    return pl.pallas_call(
        body, name="mix_in_bwd_act", grid_spec=grid_spec,
        out_shape=[jax.ShapeDtypeStruct((T, D), F32), jax.ShapeDtypeStruct((T, D), BF16),
                   jax.ShapeDtypeStruct((8, D), F32)],
        compiler_params=_cparams(("arbitrary", "arbitrary")),
    )(l, dp, dxn, x, pv, wg)


def mix_in_bwd(l, dp, dxn, x, pv, wg, tm=512):
    T, D = x.shape
    tm = min(tm, T)
    nj, ni = N_DEV, T // tm

    def body(l_ref, dp_ref, dxn_ref, x_ref, pv_ref, w_ref, dx_ref, gw_ref, pg_ref, h_all, dh_all, acc):
        j, i = pl.program_id(0), pl.program_id(1)
        rows = pl.ds(pl.multiple_of(i * tm, tm), tm)

        @pl.when((i == 0) & (j == 0))
        def _():
            pg_ref[...] = jnp.zeros_like(pg_ref)

        @pl.when(j == 0)
        def _():
            h, _, _, _ = _prenorm(x_ref[...], pv_ref)
            h_all[rows, :] = h.astype(BF16)
            dh_all[rows, :] = jnp.zeros((tm, D), F32)

        @pl.when(i == 0)
        def _():
            acc[...] = jnp.zeros_like(acc)

        dp_t = dp_ref[...]
        dh_all[rows, :] += _dot_nt(dp_t, w_ref[...])
        acc[...] += _dot_tn(h_all[rows, :], dp_t)

        @pl.when(i == ni - 1)
        def _():
            gw_ref[...] = acc[...].astype(BF16)

        @pl.when(j == nj - 1)
        def _():
            dx_ref[...] = _prenorm_bwd(dh_all[rows, :], dxn_ref[...], x_ref[...], pv_ref, pg_ref)

    ends = lambda j, i, l: (jnp.where((j == 0) | (j == nj - 1), i, ni - 1), 0)
    last = lambda j, i, l: (jnp.where(j == nj - 1, i, 0), 0)
    grid_spec = pltpu.PrefetchScalarGridSpec(
        num_scalar_prefetch=1, grid=(nj, ni),
        in_specs=[pl.BlockSpec((tm, IN_BLK), lambda j, i, l: (i, j)),
                  pl.BlockSpec((tm, D), last), pl.BlockSpec((tm, D), ends),
                  pl.BlockSpec((8, D), lambda j, i, l: (0, 0)),
                  pl.BlockSpec((None, None, D, IN_BLK), lambda j, i, l: (l[0], j, 0, 0))],
        out_specs=[pl.BlockSpec((tm, D), last), pl.BlockSpec((None, D, IN_BLK), lambda j, i, l: (j, 0, 0)),
                   pl.BlockSpec((8, D), lambda j, i, l: (0, 0))],
        scratch_shapes=[pltpu.VMEM((T, D), BF16), pltpu.VMEM((T, D), F32), pltpu.VMEM((D, IN_BLK), F32)],
    )
    return pl.pallas_call(
        body, name="mix_in_bwd", grid_spec=grid_spec,
        out_shape=[jax.ShapeDtypeStruct((T, D), F32), jax.ShapeDtypeStruct((nj, D, IN_BLK), BF16),
                   jax.ShapeDtypeStruct((8, D), F32)],
        compiler_params=_cparams(("arbitrary", "arbitrary")),
    )(l, dp, dxn, x, pv, wg)


def matmul_tn(a, b, tn, tm=1024):
    T, M = a.shape
    tm = min(tm, T)
    N = b.shape[1]
    ni = T // tm

    def body(a_ref, b_ref, o_ref, acc):
        i = pl.program_id(1)

        @pl.when(i == 0)
        def _():
            acc[...] = jnp.zeros_like(acc)

        acc[...] += _dot_tn(a_ref[...], b_ref[...])

        @pl.when(i == ni - 1)
        def _():
            o_ref[...] = acc[...].astype(o_ref.dtype)

    return pl.pallas_call(
        body, name="matmul_tn", grid=(N // tn, ni),
        in_specs=[pl.BlockSpec((tm, M), lambda j, i: (i, 0)), pl.BlockSpec((tm, tn), lambda j, i: (i, j))],
        out_specs=pl.BlockSpec((None, M, tn), lambda j, i: (j, 0, 0)),
        out_shape=jax.ShapeDtypeStruct((N // tn, M, tn), BF16),
        scratch_shapes=[pltpu.VMEM((M, tn), F32)],
        compiler_params=_cparams(("arbitrary", "arbitrary")),
    )(a, b)


SEQ_CHUNK = 256
HALO = 16


def _shift_down(ext, d):
    return pltpu.roll(ext, d, 0)


def _shift_up(ext, d):
    return pltpu.roll(ext, ext.shape[0] - d, 0)


def _rows_with_lead(load, c, width):
    t0 = c * SEQ_CHUNK
    if c == 0:
        return jnp.concatenate([jnp.zeros((HALO, width), F32), load(0, SEQ_CHUNK)], axis=0)
    return load(t0 - HALO, SEQ_CHUNK + HALO)


def _rows_with_tail(load, c, n_chunks, width):
    t0 = c * SEQ_CHUNK
    if c == n_chunks - 1:
        return jnp.concatenate([load(t0, SEQ_CHUNK), jnp.zeros((HALO, width), F32)], axis=0)
    return load(t0, SEQ_CHUNK + HALO)


def conv_fwd(l, p, conv_w):
    T = p.shape[0]
    W = MIX_W
    nC = T // SEQ_CHUNK

    def body(l_ref, p_ref, w_ref, za_ref):
        w0, w1, w2 = w_ref[0:1, :], w_ref[1:2, :], w_ref[2:3, :]
        for c in range(nC):
            ext = _rows_with_lead(lambda s, n: p_ref[s:s + n, W:2 * W] * p_ref[s:s + n, 2 * W:3 * W], c, W)
            y = w2 * ext + w1 * _shift_down(ext, 1) + w0 * _shift_down(ext, 2)
            t0 = c * SEQ_CHUNK
            za_ref[t0:t0 + SEQ_CHUNK, :] = (p_ref[t0:t0 + SEQ_CHUNK, 0:W] * y[HALO:]).astype(BF16)

    grid_spec = pltpu.PrefetchScalarGridSpec(
        num_scalar_prefetch=1, grid=(1,),
        in_specs=[pl.BlockSpec((T, 3 * W), lambda i, l: (0, 0)),
                  pl.BlockSpec((None, 8, W), lambda i, l: (l[0], 0, 0))],
        out_specs=pl.BlockSpec((T, W), lambda i, l: (0, 0)),
    )
    return pl.pallas_call(
        body, name="conv_fwd", grid_spec=grid_spec,
        out_shape=jax.ShapeDtypeStruct((T, W), BF16),
        compiler_params=_cparams(("arbitrary",)),
    )(l, p, conv_w)


def conv_bwd(l, p, dza, conv_w):
    T = p.shape[0]
    W = MIX_W
    nC = T // SEQ_CHUNK

    def body(l_ref, p_ref, dza_ref, w_ref, dp_ref, dw_ref):
        w0, w1, w2 = w_ref[0:1, :], w_ref[1:2, :], w_ref[2:3, :]
        dw = [jnp.zeros((1, W), F32) for _ in range(3)]
        for c in range(nC):
            t0 = c * SEQ_CHUNK
            ext = _rows_with_lead(lambda s, n: p_ref[s:s + n, W:2 * W] * p_ref[s:s + n, 2 * W:3 * W], c, W)
            u1, u2 = _shift_down(ext, 1)[HALO:], _shift_down(ext, 2)[HALO:]
            u0 = ext[HALO:]
            y = w2 * u0 + w1 * u1 + w0 * u2
            dza_c = dza_ref[t0:t0 + SEQ_CHUNK, :]
            dy = dza_c * p_ref[t0:t0 + SEQ_CHUNK, 0:W]
            dw[0] += _colsum(dy * u2)
            dw[1] += _colsum(dy * u1)
            dw[2] += _colsum(dy * u0)
            dye = _rows_with_tail(lambda s, n: dza_ref[s:s + n, :] * p_ref[s:s + n, 0:W], c, nC, W)
            du = (w2 * dye + w1 * _shift_up(dye, 1) + w0 * _shift_up(dye, 2))[:SEQ_CHUNK]
            dp_ref[t0:t0 + SEQ_CHUNK, 0:W] = (dza_c * y).astype(BF16)
            dp_ref[t0:t0 + SEQ_CHUNK, W:2 * W] = (du * p_ref[t0:t0 + SEQ_CHUNK, 2 * W:3 * W]).astype(BF16)
            dp_ref[t0:t0 + SEQ_CHUNK, 2 * W:3 * W] = (du * p_ref[t0:t0 + SEQ_CHUNK, W:2 * W]).astype(BF16)
        dw_ref[...] = jnp.concatenate(dw + [jnp.zeros((5, W), F32)], axis=0)

    grid_spec = pltpu.PrefetchScalarGridSpec(
        num_scalar_prefetch=1, grid=(1,),
        in_specs=[pl.BlockSpec((T, 3 * W), lambda i, l: (0, 0)),
                  pl.BlockSpec((T, W), lambda i, l: (0, 0)),
                  pl.BlockSpec((None, 8, W), lambda i, l: (l[0], 0, 0))],
        out_specs=[pl.BlockSpec((T, 3 * W), lambda i, l: (0, 0)), pl.BlockSpec((8, W), lambda i, l: (0, 0))],
    )
    return pl.pallas_call(
        body, name="conv_bwd", grid_spec=grid_spec,
        out_shape=[jax.ShapeDtypeStruct((T, 3 * W), BF16), jax.ShapeDtypeStruct((8, W), F32)],
        compiler_params=_cparams(("arbitrary",)),
    )(l, p, dza, conv_w)


def _pool_consts(rows, t0):
    lane = lax.broadcasted_iota(jnp.int32, (rows, MIX_W), 1)
    t = lax.broadcasted_iota(jnp.int32, (rows, MIX_W), 0) + t0
    win = jnp.where(lane < 64, 2, jnp.where(lane < 128, 4, jnp.where(lane < 192, 8, 16)))
    inv = 1.0 / jnp.minimum(t + 1, win).astype(F32)
    return lane, inv


def _pick_window(lane, s2, s4, s8, s16):
    return jnp.where(lane < 64, s2, jnp.where(lane < 128, s4, jnp.where(lane < 192, s8, s16)))


def _pooled_chunk(u_ref, c):
    ext = _rows_with_lead(lambda s, n: u_ref[s:s + n, :], c, MIX_W)
    s2 = ext + _shift_down(ext, 1)
    s4 = s2 + _shift_down(s2, 2)
    s8 = s4 + _shift_down(s4, 4)
    s16 = s8 + _shift_down(s8, 8)
    lane, inv = _pool_consts(SEQ_CHUNK, c * SEQ_CHUNK)
    return _pick_window(lane, s2[HALO:], s4[HALO:], s8[HALO:], s16[HALO:]) * inv - ext[HALO:]


def pool_fwd(l, p, w_bd, scale):
    T = p.shape[0]
    W = MIX_W
    nC = T // SEQ_CHUNK

    def body(l_ref, u_ref, w_ref, sc_ref, z_ref):
        for c in range(nC):
            pooled = _pooled_chunk(u_ref, c)
            mixed = _dot(pooled.astype(BF16), w_ref[...])
            z_ref[c * SEQ_CHUNK:(c + 1) * SEQ_CHUNK, :] = (mixed * sc_ref[0:1, :]).astype(BF16)

    grid_spec = pltpu.PrefetchScalarGridSpec(
        num_scalar_prefetch=1, grid=(1,),
        in_specs=[pl.BlockSpec((T, W), lambda i, l: (0, 4)),
                  pl.BlockSpec((None, W, W), lambda i, l: (l[0], 0, 0)),
                  pl.BlockSpec((None, 8, W), lambda i, l: (l[0], 0, 0))],
        out_specs=pl.BlockSpec((T, W), lambda i, l: (0, 0)),
    )
    return pl.pallas_call(
        body, name="pool_fwd", grid_spec=grid_spec,
        out_shape=jax.ShapeDtypeStruct((T, W), BF16),
        compiler_params=_cparams(("arbitrary",)),
    )(l, p, w_bd, scale)


def pool_bwd(l, p, dz, w_bd, scale):
    T = p.shape[0]
    W = MIX_W
    nC = T // SEQ_CHUNK

    def body(l_ref, u_ref, dz_ref, w_ref, sc_ref, du_ref, dw_ref, dsc_ref, e_sc, dpl_sc):
        dw = jnp.zeros((W, W), F32)
        dsc = jnp.zeros((1, W), F32)
        for c in range(nC):
            t0 = c * SEQ_CHUNK
            pooled = _pooled_chunk(u_ref, c).astype(BF16)
            mixed = _dot(pooled, w_ref[...])
            dz_c = dz_ref[t0:t0 + SEQ_CHUNK, :]
            dsc += _colsum(dz_c * mixed)
            dmixed = (dz_c * sc_ref[0:1, :]).astype(BF16)
            dw += _dot_tn(pooled, dmixed)
            dpooled = _dot_nt(dmixed, w_ref[...])
            _, inv = _pool_consts(SEQ_CHUNK, t0)
            dpl_sc[t0:t0 + SEQ_CHUNK, :] = dpooled
            e_sc[t0:t0 + SEQ_CHUNK, :] = dpooled * inv
        for c in range(nC):
            t0 = c * SEQ_CHUNK
            ext = _rows_with_tail(lambda s, n: e_sc[s:s + n, :], c, nC, W)
            s2 = ext + _shift_up(ext, 1)
            s4 = s2 + _shift_up(s2, 2)
            s8 = s4 + _shift_up(s4, 4)
            s16 = s8 + _shift_up(s8, 8)
            lane, _ = _pool_consts(SEQ_CHUNK, t0)
            n = SEQ_CHUNK
            du = _pick_window(lane, s2[:n], s4[:n], s8[:n], s16[:n]) - dpl_sc[t0:t0 + SEQ_CHUNK, :]
            du_ref[t0:t0 + SEQ_CHUNK, :] = du.astype(BF16)
        dw_ref[...] = dw
        dsc_ref[...] = jnp.concatenate([dsc, jnp.zeros((7, W), F32)], axis=0)

    grid_spec = pltpu.PrefetchScalarGridSpec(
        num_scalar_prefetch=1, grid=(1,),
        in_specs=[pl.BlockSpec((T, W), lambda i, l: (0, 4)),
                  pl.BlockSpec((T, W), lambda i, l: (0, 0)),
                  pl.BlockSpec((None, W, W), lambda i, l: (l[0], 0, 0)),
                  pl.BlockSpec((None, 8, W), lambda i, l: (l[0], 0, 0))],
        out_specs=[pl.BlockSpec((T, W), lambda i, l: (0, 0)), pl.BlockSpec((W, W), lambda i, l: (0, 0)),
                   pl.BlockSpec((8, W), lambda i, l: (0, 0))],
        scratch_shapes=[pltpu.VMEM((T, W), F32), pltpu.VMEM((T, W), F32)],
    )
    return pl.pallas_call(
        body, name="pool_bwd", grid_spec=grid_spec,
        out_shape=[jax.ShapeDtypeStruct((T, W), BF16), jax.ShapeDtypeStruct((W, W), F32),
                   jax.ShapeDtypeStruct((8, W), F32)],
        compiler_params=_cparams(("arbitrary",)),
    )(l, p, dz, w_bd, scale)


def _s5_disc(lre, lim, ldt):
    lr = jnp.minimum(lre, DT_LAMBDA_RE_MAX)
    dt = jnp.exp(ldt)
    mag = jnp.exp(lr * dt)
    a_re = mag * jnp.cos(lim * dt)
    a_im = mag * jnp.sin(lim * dt)
    den = lr * lr + lim * lim
    nr = a_re - 1.0
    return a_re, a_im, (nr * lr + a_im * lim) / den, (a_im * lr - nr * lim) / den


def _bd_mask(shape, row_blk, col_blk):
    r = lax.broadcasted_iota(jnp.int32, shape, 0) >> (row_blk.bit_length() - 1)
    c = lax.broadcasted_iota(jnp.int32, shape, 1) >> (col_blk.bit_length() - 1)
    return r == c


def s5_params(lam, b_t, c_t):
    L = lam.shape[0]

    def body(lam_ref, b_ref, c_ref, a_ref, bbd_ref, cbd_ref):
        a_re, a_im, f_re, f_im = _s5_disc(lam_ref[0:1, :], lam_ref[1:2, :], lam_ref[2:3, :])
        a_ref[...] = jnp.concatenate([a_re, a_im, jnp.zeros((6, SSM_W), F32)], axis=0)
        mb = _bd_mask((MIX_W, SSM_W), SSM_GROUP, SSM_STATE)
        bbd_ref[0] = jnp.where(mb, f_re * b_ref[0] - f_im * b_ref[1], 0.0).astype(BF16)
        bbd_ref[1] = jnp.where(mb, f_re * b_ref[1] + f_im * b_ref[0], 0.0).astype(BF16)
        mc = _bd_mask((SSM_W, MIX_W), SSM_STATE, SSM_GROUP)
        cbd_ref[0] = jnp.where(mc, c_ref[0], 0.0).astype(BF16)
        cbd_ref[1] = jnp.where(mc, c_ref[1], 0.0).astype(BF16)

    return pl.pallas_call(
        body, name="s5_params", grid=(L,),
        in_specs=[pl.BlockSpec((None, 8, SSM_W), lambda l: (l, 0, 0)),
                  pl.BlockSpec((None, 2, MIX_W, SSM_W), lambda l: (l, 0, 0, 0)),
                  pl.BlockSpec((None, 2, SSM_W, MIX_W), lambda l: (l, 0, 0, 0))],
        out_specs=[pl.BlockSpec((None, 8, SSM_W), lambda l: (l, 0, 0)),
                   pl.BlockSpec((None, 2, MIX_W, SSM_W), lambda l: (l, 0, 0, 0)),
                   pl.BlockSpec((None, 2, SSM_W, MIX_W), lambda l: (l, 0, 0, 0))],
        out_shape=[jax.ShapeDtypeStruct((L, 8, SSM_W), F32),
                   jax.ShapeDtypeStruct((L, 2, MIX_W, SSM_W), BF16),
                   jax.ShapeDtypeStruct((L, 2, SSM_W, MIX_W), BF16)],
        compiler_params=_cparams(("arbitrary",)),
    )(lam, b_t, c_t)


def s5_params_bwd(lam, b_t, gb, gc, da):
    L = lam.shape[0]

    def body(lam_ref, b_ref, gb_ref, gc_ref, da_ref, dlam_ref, db_ref, dc_ref, dgrp_ref):
        lre, lim, ldt = lam_ref[0:1, :], lam_ref[1:2, :], lam_ref[2:3, :]
        (a_re, a_im, f_re, f_im), vjp = jax.vjp(_s5_disc, lre, lim, ldt)
        mb = _bd_mask((MIX_W, SSM_W), SSM_GROUP, SSM_STATE)
        gbr = jnp.where(mb, gb_ref[0], 0.0)
        gbi = jnp.where(mb, gb_ref[1], 0.0)
        df_re = _colsum(gbr * b_ref[0] + gbi * b_ref[1])
        df_im = _colsum(gbi * b_ref[0] - gbr * b_ref[1])
        db_ref[0] = f_re * gbr + f_im * gbi
        db_ref[1] = f_re * gbi - f_im * gbr
        mc = _bd_mask((SSM_W, MIX_W), SSM_STATE, SSM_GROUP)
        dc_ref[0] = jnp.where(mc, gc_ref[0], 0.0)
        dc_ref[1] = jnp.where(mc, gc_ref[1], 0.0)
        dlre, dlim, dldt = vjp((da_ref[0:1, :], da_ref[1:2, :], df_re, df_im))
        dl = jnp.concatenate([dlre, dlim, dldt, jnp.zeros((5, SSM_W), F32)], axis=0)
        dlam_ref[...] = dl
        grp = jnp.where(_bd_mask((SSM_W, 128), SSM_STATE, 1), 1.0, 0.0)
        dgrp_ref[...] = jnp.dot(dl, grp, preferred_element_type=F32, precision=lax.Precision.HIGHEST)

    vec = pl.BlockSpec((None, 8, SSM_W), lambda l: (l, 0, 0))
    bsp = pl.BlockSpec((None, 2, MIX_W, SSM_W), lambda l: (l, 0, 0, 0))
    csp = pl.BlockSpec((None, 2, SSM_W, MIX_W), lambda l: (l, 0, 0, 0))
    return pl.pallas_call(
        body, name="s5_params_bwd", grid=(L,),
        in_specs=[vec, bsp, bsp, csp, vec],
        out_specs=[vec, bsp, csp, pl.BlockSpec((None, 8, 128), lambda l: (l, 0, 0))],
        out_shape=[jax.ShapeDtypeStruct((L, 8, SSM_W), F32),
                   jax.ShapeDtypeStruct((L, 2, MIX_W, SSM_W), F32),
                   jax.ShapeDtypeStruct((L, 2, SSM_W, MIX_W), F32),
                   jax.ShapeDtypeStruct((L, 8, 128), F32)],
        compiler_params=_cparams(("arbitrary",)),
    )(lam, b_t, gb, gc, da)


def s5_bu(l, p, b_bd, tm=512):
    T = p.shape[0]

    def body(l_ref, u_ref, b_ref, bu_ref):
        u = u_ref[...].astype(BF16)
        bu_ref[0] = _dot(u, b_ref[0])
        bu_ref[1] = _dot(u, b_ref[1])

    grid_spec = pltpu.PrefetchScalarGridSpec(
        num_scalar_prefetch=1, grid=(T // tm,),
        in_specs=[pl.BlockSpec((tm, MIX_W), lambda i, l: (i, 3)),
                  pl.BlockSpec((None, 2, MIX_W, SSM_W), lambda i, l: (l[0], 0, 0, 0))],
        out_specs=pl.BlockSpec((2, tm, SSM_W), lambda i, l: (0, i, 0)),
    )
    return pl.pallas_call(
        body, name="s5_bu", grid_spec=grid_spec,
        out_shape=jax.ShapeDtypeStruct((2, T, SSM_W), F32),
        compiler_params=_cparams(("arbitrary",)),
    )(l, p, b_bd)


def s5_scan(l, avec, xs, reverse):
    T = xs.shape[1]
    CH = SEQ_CHUNK
    nC = T // CH
    LW = 128
    n_steps = CH.bit_length() - 1

    def body(l_ref, a_ref, x_ref, s_ref):
        ar = a_ref[0:1, :]
        ai = -a_ref[1:2, :] if reverse else a_ref[1:2, :]
        pows = [(ar, ai)]
        for _ in range(n_steps - 1):
            r, i = pows[-1]
            pows.append((r * r - i * i, 2.0 * r * i))
        row = lax.broadcasted_iota(jnp.int32, (CH, LW), 0)

        def local_scan(re, im):
            for k in range(n_steps):
                d = 1 << k
                pr, pi = pows[k]
                if reverse:
                    keep = row < CH - d
                    sr, si = _shift_up(re, d), _shift_up(im, d)
                else:
                    keep = row >= d
                    sr, si = _shift_down(re, d), _shift_down(im, d)
                sr = jnp.where(keep, sr, 0.0)
                si = jnp.where(keep, si, 0.0)
                re, im = re + pr * sr - pi * si, im + pr * si + pi * sr
            return re, im

        edge = CH - 1 if reverse else 0
        pw_re, pw_im = local_scan(jnp.where(row == edge, ar, 0.0), jnp.where(row == edge, ai, 0.0))
        last = 0 if reverse else CH - 1

        def chunk(c, carry):
            cr, ci = carry
            cc = nC - 1 - c if reverse else c
            t0 = pl.multiple_of(cc * CH, CH)
            re, im = local_scan(x_ref[0, pl.ds(t0, CH), :], x_ref[1, pl.ds(t0, CH), :])
            re2 = re + pw_re * cr - pw_im * ci
            im2 = im + pw_re * ci + pw_im * cr
            s_ref[0, pl.ds(t0, CH), :] = re2
            s_ref[1, pl.ds(t0, CH), :] = im2
            return re2[last:last + 1, :], im2[last:last + 1, :]

        lax.fori_loop(0, nC, chunk, (jnp.zeros((1, LW), F32), jnp.zeros((1, LW), F32)))

    grid_spec = pltpu.PrefetchScalarGridSpec(
        num_scalar_prefetch=1, grid=(SSM_W // LW,),
        in_specs=[pl.BlockSpec((None, 8, LW), lambda g, l: (l[0], 0, g)),
                  pl.BlockSpec((2, T, LW), lambda g, l: (0, 0, g))],
        out_specs=pl.BlockSpec((2, T, LW), lambda g, l: (0, 0, g)),
    )
    return pl.pallas_call(
        body, name="s5_scan_rev" if reverse else "s5_scan_fwd", grid_spec=grid_spec,
        out_shape=jax.ShapeDtypeStruct((2, T, SSM_W), F32),
        compiler_params=_cparams(("arbitrary",)),
    )(l, avec, xs)


_GELU_C = 0.7978845608028654
_GELU_K = 0.044715


def _s5_y(u, s_ref, c_ref, d_row):
    y = _dot(s_ref[0].astype(BF16), c_ref[0]) - _dot(s_ref[1].astype(BF16), c_ref[1])
    return y + d_row * u


def s5_out(l, p, s, c_bd, ssm_d, tm=512):
    T = p.shape[0]

    def body(l_ref, u_ref, s_ref, c_ref, d_ref, yg_ref):
        y = _s5_y(u_ref[...], s_ref, c_ref, d_ref[0:1, :])
        th = jnp.tanh(_GELU_C * (y + _GELU_K * y * y * y))
        yg_ref[...] = (0.5 * y * (1.0 + th)).astype(BF16)

    grid_spec = pltpu.PrefetchScalarGridSpec(
        num_scalar_prefetch=1, grid=(T // tm,),
        in_specs=[pl.BlockSpec((tm, MIX_W), lambda i, l: (i, 3)),
                  pl.BlockSpec((2, tm, SSM_W), lambda i, l: (0, i, 0)),
                  pl.BlockSpec((None, 2, SSM_W, MIX_W), lambda i, l: (l[0], 0, 0, 0)),
                  pl.BlockSpec((None, 8, MIX_W), lambda i, l: (l[0], 0, 0))],
        out_specs=pl.BlockSpec((tm, MIX_W), lambda i, l: (i, 0)),
    )
    return pl.pallas_call(
        body, name="s5_out", grid_spec=grid_spec,
        out_shape=jax.ShapeDtypeStruct((T, MIX_W), BF16),
        compiler_params=_cparams(("arbitrary",)),
    )(l, p, s, c_bd, ssm_d)


def s5_bwd_y(l, p, s, dyg, c_bd, ssm_d, tm=512):
    T = p.shape[0]

    def body(l_ref, u_ref, s_ref, dyg_ref, c_ref, d_ref, ds_ref, du_ref, gc_ref, dd_ref):
        @pl.when(pl.program_id(0) == 0)
        def _():
            gc_ref[...] = jnp.zeros_like(gc_ref)
            dd_ref[...] = jnp.zeros_like(dd_ref)

        u = u_ref[...]
        y = _s5_y(u, s_ref, c_ref, d_ref[0:1, :])
        inner = _GELU_C * (y + _GELU_K * y * y * y)
        th = jnp.tanh(inner)
        dgelu = 0.5 * (1.0 + th) + 0.5 * y * (1.0 - th * th) * (_GELU_C * (1.0 + 3.0 * _GELU_K * y * y))
        dy = dyg_ref[...] * dgelu
        dd_ref[0:1, :] += _colsum(dy * u)
        du_ref[...] = dy * d_ref[0:1, :]
        dyb = dy.astype(BF16)
        ds_ref[0] = _dot_nt(dyb, c_ref[0])
        ds_ref[1] = -_dot_nt(dyb, c_ref[1])
        gc_ref[0] += _dot_tn(s_ref[0].astype(BF16), dyb)
        gc_ref[1] -= _dot_tn(s_ref[1].astype(BF16), dyb)

    grid_spec = pltpu.PrefetchScalarGridSpec(
        num_scalar_prefetch=1, grid=(T // tm,),
        in_specs=[pl.BlockSpec((tm, MIX_W), lambda i, l: (i, 3)),
                  pl.BlockSpec((2, tm, SSM_W), lambda i, l: (0, i, 0)),
                  pl.BlockSpec((tm, MIX_W), lambda i, l: (i, 0)),
                  pl.BlockSpec((None, 2, SSM_W, MIX_W), lambda i, l: (l[0], 0, 0, 0)),
                  pl.BlockSpec((None, 8, MIX_W), lambda i, l: (l[0], 0, 0))],
        out_specs=[pl.BlockSpec((2, tm, SSM_W), lambda i, l: (0, i, 0)),
                   pl.BlockSpec((tm, MIX_W), lambda i, l: (i, 0)),
                   pl.BlockSpec((2, SSM_W, MIX_W), lambda i, l: (0, 0, 0)),
                   pl.BlockSpec((8, MIX_W), lambda i, l: (0, 0))],
    )
    return pl.pallas_call(
        body, name="s5_bwd_y", grid_spec=grid_spec,
        out_shape=[jax.ShapeDtypeStruct((2, T, SSM_W), F32), jax.ShapeDtypeStruct((T, MIX_W), F32),
                   jax.ShapeDtypeStruct((2, SSM_W, MIX_W), F32), jax.ShapeDtypeStruct((8, MIX_W), F32)],
        compiler_params=_cparams(("arbitrary",)),
    )(l, p, s, dyg, c_bd, ssm_d)


def s5_bwd_u(l, p, lam_s, du_skip, b_bd, tm=512):
    T = p.shape[0]

    def body(l_ref, u_ref, ls_ref, dus_ref, b_ref, du_ref, gb_ref):
        @pl.when(pl.program_id(0) == 0)
        def _():
            gb_ref[...] = jnp.zeros_like(gb_ref)

        u = u_ref[...].astype(BF16)
        lr = ls_ref[0].astype(BF16)
        li = ls_ref[1].astype(BF16)
        gb_ref[0] += _dot_tn(u, lr)
        gb_ref[1] += _dot_tn(u, li)
        du_ref[...] = (dus_ref[...] + _dot_nt(lr, b_ref[0]) + _dot_nt(li, b_ref[1])).astype(BF16)

    grid_spec = pltpu.PrefetchScalarGridSpec(
        num_scalar_prefetch=1, grid=(T // tm,),
        in_specs=[pl.BlockSpec((tm, MIX_W), lambda i, l: (i, 3)),
                  pl.BlockSpec((2, tm, SSM_W), lambda i, l: (0, i, 0)),
                  pl.BlockSpec((tm, MIX_W), lambda i, l: (i, 0)),
                  pl.BlockSpec((None, 2, MIX_W, SSM_W), lambda i, l: (l[0], 0, 0, 0))],
        out_specs=[pl.BlockSpec((tm, MIX_W), lambda i, l: (i, 0)),
                   pl.BlockSpec((2, MIX_W, SSM_W), lambda i, l: (0, 0, 0))],
    )
    return pl.pallas_call(
        body, name="s5_bwd_u", grid_spec=grid_spec,
        out_shape=[jax.ShapeDtypeStruct((T, MIX_W), BF16), jax.ShapeDtypeStruct((2, MIX_W, SSM_W), F32)],
        compiler_params=_cparams(("arbitrary",)),
    )(l, p, lam_s, du_skip, b_bd)


def s5_bwd_a(s, lam_s):
    T = s.shape[1]
    nC = T // SEQ_CHUNK
    LW = 128

    def body(s_ref, ls_ref, da_ref):
        dre = jnp.zeros((1, LW), F32)
        dim = jnp.zeros((1, LW), F32)
        for c in range(nC):
            t0 = c * SEQ_CHUNK
            sr = _shift_down(_rows_with_lead(lambda a, n: s_ref[0, a:a + n, :], c, LW), 1)[HALO:]
            si = _shift_down(_rows_with_lead(lambda a, n: s_ref[1, a:a + n, :], c, LW), 1)[HALO:]
            lr = ls_ref[0, t0:t0 + SEQ_CHUNK, :]
            li = ls_ref[1, t0:t0 + SEQ_CHUNK, :]
            dre += _colsum(sr * lr + si * li)
            dim += _colsum(sr * li - si * lr)
        da_ref[...] = jnp.concatenate([dre, dim, jnp.zeros((6, LW), F32)], axis=0)

    blk = pl.BlockSpec((2, T, LW), lambda g: (0, 0, g))
    return pl.pallas_call(
        body, name="s5_bwd_a", grid=(SSM_W // LW,),
        in_specs=[blk, blk],
        out_specs=pl.BlockSpec((8, LW), lambda g: (0, g)),
        out_shape=jax.ShapeDtypeStruct((8, SSM_W), F32),
        compiler_params=_cparams(("arbitrary",)),
    )(s, lam_s)


SB_BLK = 128
SB_SCALE = SB_HEAD ** -0.5


def _split_bf16(x):
    hi = x.astype(BF16)
    return hi, (x - hi.astype(F32)).astype(BF16)


def _dot_split(x, tri):
    hi, lo = _split_bf16(x)
    return _dot(hi, tri) + _dot(lo, tri)


SB_SLABS = MIX_W // SB_BLK
SB_STACK = 2 * SB_SLABS * SB_BLK
SB_PAIR = 2 * SB_BLK


def _sb_valid(r0, c0):
    row = (lax.broadcasted_iota(jnp.int32, (SB_STACK, SB_BLK), 0) & (SB_BLK - 1)) + r0
    col = lax.broadcasted_iota(jnp.int32, (SB_STACK, SB_BLK), 1) + c0
    return col < row


def _sb_stack(ref, r0, scale):
    lane = lax.broadcasted_iota(jnp.int32, (SB_BLK, SB_BLK), 1)
    parts = []
    for s in range(SB_SLABS):
        blk = ref[pl.ds(r0, SB_BLK), s * SB_BLK:(s + 1) * SB_BLK] * scale
        parts += [jnp.where(lane < SB_HEAD, blk, 0.0), jnp.where(lane >= SB_HEAD, blk, 0.0)]
    return jnp.concatenate(parts, axis=0).astype(BF16)


def _sb_rows_nt(stack, ref, c0):
    return jnp.concatenate(
        [_dot_nt(stack[s * SB_PAIR:(s + 1) * SB_PAIR], ref[pl.ds(c0, SB_BLK), s * SB_BLK:(s + 1) * SB_BLK].astype(BF16))
         for s in range(SB_SLABS)], axis=0)


def _sb_wide(stack, s):
    return jnp.concatenate([stack[s * SB_PAIR:s * SB_PAIR + SB_BLK], stack[s * SB_PAIR + SB_BLK:(s + 1) * SB_PAIR]],
                           axis=1)


def _sb_logits(q_stack, k_ref, c0, valid):
    z = _sb_rows_nt(q_stack, k_ref, c0)
    sp = jnp.log(1.0 + jnp.exp(-jnp.abs(z)))
    ls_pos = jnp.minimum(z, 0.0) - sp
    lk = jnp.minimum(-z, 0.0) - sp
    if valid is not None:
        lk = jnp.where(valid, lk, 0.0)
    return z, ls_pos, lk


def _tri(lower):
    r = lax.broadcasted_iota(jnp.int32, (SB_BLK, SB_BLK), 0)
    c = lax.broadcasted_iota(jnp.int32, (SB_BLK, SB_BLK), 1)
    return jnp.where(r > c if lower else r < c, 1.0, 0.0).astype(BF16)


def sb_fwd(p):
    T = p.shape[0]
    W = MIX_W
    nB = T // SB_BLK

    def body(q_ref, k_ref, v_ref, o_ref, tot_ref, acc_sc):
        tri = _tri(True)

        def qblock(i, _):
            r0 = pl.multiple_of(i * SB_BLK, SB_BLK)
            q = _sb_stack(q_ref, r0, SB_SCALE)
            acc_sc[...] = jnp.zeros_like(acc_sc)

            def kblocks(c0s, run, valid):
                parts = [_sb_logits(q, k_ref, c0, valid) for c0 in c0s]
                for c0, (_, ls_pos, lk) in zip(c0s, parts):
                    a = jnp.exp(ls_pos + _dot_split(lk, tri) + run)
                    if valid is not None:
                        a = jnp.where(valid, a, 0.0)
                    a = a.astype(BF16)
                    v = _sb_stack(v_ref, c0, 1.0)
                    for s in range(SB_SLABS):
                        acc_sc[:, s * SB_BLK:(s + 1) * SB_BLK] += _dot(_sb_wide(a, s), v[s * SB_PAIR:(s + 1) * SB_PAIR])
                    run = run + jnp.sum(lk, axis=1, keepdims=True)
                return run

            def key_block(jj):
                return pl.multiple_of((i - jj) * SB_BLK, SB_BLK)

            run = kblocks([r0], jnp.zeros((SB_STACK, 1), F32), _sb_valid(0, 0))
            odd = i & 1
            run = lax.cond(odd == 1, lambda r: kblocks([key_block(1)], r, None), lambda r: r, run)
            total = lax.fori_loop(
                0, i >> 1, lambda t, r: kblocks([key_block(1 + odd + 2 * t), key_block(2 + odd + 2 * t)], r, None), run)
            o_ref[pl.ds(r0, SB_BLK), :] = acc_sc[...].astype(BF16)
            tot_ref[pl.ds(pl.multiple_of(i * SB_STACK, SB_STACK), SB_STACK), :] = jnp.broadcast_to(total, (SB_STACK, SB_BLK))
            return 0

        lax.fori_loop(0, nB, qblock, 0)

    return pl.pallas_call(
        body, name="sb_fwd", grid=(1,),
        in_specs=[pl.BlockSpec((T, W), lambda i: (0, 5)), pl.BlockSpec((T, W), lambda i: (0, 6)),
                  pl.BlockSpec((T, W), lambda i: (0, 7))],
        out_specs=[pl.BlockSpec((T, W), lambda i: (0, 0)), pl.BlockSpec((nB * SB_STACK, SB_BLK), lambda i: (0, 0))],
        out_shape=[jax.ShapeDtypeStruct((T, W), BF16), jax.ShapeDtypeStruct((nB * SB_STACK, SB_BLK), F32)],
        scratch_shapes=[pltpu.VMEM((SB_BLK, W), F32)],
        compiler_params=_cparams(("arbitrary",)),
    )(p, p, p)


def sb_bwd(p, do, tot):
    T = p.shape[0]
    W = MIX_W
    nB = T // SB_BLK

    def body(q_ref, k_ref, v_ref, do_ref, tot_ref, dqkv_ref, dq_sc, dk_sc, dv_sc):
        tri_gt = _tri(True)
        tri_lt = _tri(False)
        dq_sc[...] = jnp.zeros_like(dq_sc)
        dk_sc[...] = jnp.zeros_like(dk_sc)
        dv_sc[...] = jnp.zeros_like(dv_sc)
        zcol = jnp.zeros((SB_STACK, 1), F32)

        def qblock(i, _):
            r0 = pl.multiple_of(i * SB_BLK, SB_BLK)
            q = _sb_stack(q_ref, r0, SB_SCALE)
            dob = _sb_stack(do_ref, r0, 1.0)

            total = tot_ref[pl.ds(pl.multiple_of(i * SB_STACK, SB_STACK), SB_STACK), 0:1]

            def kblocks(c0s, carry, valid):
                pre, seen = carry
                parts = [_sb_logits(q, k_ref, c0, valid) for c0 in c0s]
                for c0, (z, ls_pos, lk) in zip(c0s, parts):
                    seen = seen + jnp.sum(lk, axis=1, keepdims=True)
                    a = jnp.exp(ls_pos + _dot_split(lk, tri_gt) + (total - seen))
                    if valid is not None:
                        a = jnp.where(valid, a, 0.0)
                    dlw = _sb_rows_nt(dob, v_ref, c0) * a
                    g = pre + _dot_split(dlw, tri_lt)
                    sig = _sigmoid(z)
                    dz = dlw * (1.0 - sig) - g * sig
                    if valid is not None:
                        dz = jnp.where(valid, dz, 0.0)
                    dz = dz.astype(BF16)
                    ab = a.astype(BF16)
                    km = _sb_stack(k_ref, c0, 1.0)
                    for s in range(SB_SLABS):
                        pair = slice(s * SB_PAIR, (s + 1) * SB_PAIR)
                        ls = slice(s * SB_BLK, (s + 1) * SB_BLK)
                        dk_sc[pl.ds(c0, SB_BLK), ls] += _dot_tn(dz[pair], q[pair])
                        dv_sc[pl.ds(c0, SB_BLK), ls] += _dot_tn(ab[pair], dob[pair])
                        dq_sc[pl.ds(r0, SB_BLK), ls] += _dot(_sb_wide(dz, s), km[pair])
                    pre = pre + jnp.sum(dlw, axis=1, keepdims=True)
                return pre, seen

            def key_block(j):
                return pl.multiple_of(j * SB_BLK, SB_BLK)

            carry = lax.fori_loop(
                0, i >> 1, lambda t, c: kblocks([key_block(2 * t), key_block(2 * t + 1)], c, None), (zcol, zcol))
            carry = lax.cond((i & 1) == 1, lambda c: kblocks([key_block(i - 1)], c, None), lambda c: c, carry)
            kblocks([r0], carry, _sb_valid(0, 0))
            return 0

        lax.fori_loop(0, nB, qblock, 0)
        dqkv_ref[:, 0:W] = (dq_sc[...] * SB_SCALE).astype(BF16)
        dqkv_ref[:, W:2 * W] = dk_sc[...].astype(BF16)
        dqkv_ref[:, 2 * W:3 * W] = dv_sc[...].astype(BF16)

    return pl.pallas_call(
        body, name="sb_bwd", grid=(1,),
        in_specs=[pl.BlockSpec((T, W), lambda i: (0, 5)), pl.BlockSpec((T, W), lambda i: (0, 6)),
                  pl.BlockSpec((T, W), lambda i: (0, 7)), pl.BlockSpec((T, W), lambda i: (0, 0)),
                  pl.BlockSpec((nB * SB_STACK, SB_BLK), lambda i: (0, 0))],
        out_specs=pl.BlockSpec((T, 3 * W), lambda i: (0, 0)),
        out_shape=jax.ShapeDtypeStruct((T, 3 * W), BF16),
        scratch_shapes=[pltpu.VMEM((T, W), F32), pltpu.VMEM((T, W), F32), pltpu.VMEM((T, W), F32)],
        compiler_params=_cparams(("arbitrary",)),
    )(p, p, p, do, tot)


def _dot_cols(a, w_ref):
    return jnp.concatenate([_dot(a, w_ref[j]) for j in range(N_DEV)], axis=1)


def _dot_cols_nt(dy, w_ref):
    n = w_ref.shape[2]
    out = _dot_nt(dy[:, 0:n], w_ref[0])
    for j in range(1, N_DEV):
        out += _dot_nt(dy[:, j * n:(j + 1) * n], w_ref[j])
    return out


def _acc_cols_tn(acc_ref, a, dy):
    n = acc_ref.shape[2]
    for j in range(N_DEV):
        acc_ref[j] += _dot_tn(a, dy[:, j * n:(j + 1) * n])


def _merge_branches(za_ref, yg_ref, z_ref, o_ref, gate_refs, wc_ref, wglu_ref, wp_ref, ws_ref):
    D = D_MODEL
    glu = _dot_cols(yg_ref[...], wglu_ref)
    glu_a, sg = glu[:, :D], _sigmoid(glu[:, D:])
    ys = [_dot_cols(za_ref[...], wc_ref), glu_a * sg, _dot_cols(z_ref[...], wp_ref), _dot_cols(o_ref[...], ws_ref)]
    gs = [_sigmoid(g[...]) for g in gate_refs]
    merged = gs[0] * ys[0] + gs[1] * ys[1] + gs[2] * ys[2] + gs[3] * ys[3]
    return ys, gs, glu_a, sg, merged


def _merge_specs(tm, D):
    W = MIX_W
    br = pl.BlockSpec((tm, W), lambda i, l: (i, 0))
    gates = [pl.BlockSpec((tm, D), functools.partial(lambda i, l, b: (i, 2 + b), b=b)) for b in range(4)]
    wsm = pl.BlockSpec((None, N_DEV, W, D // N_DEV), lambda i, l: (l[0], 0, 0, 0))
    weights = [wsm, pl.BlockSpec((None, N_DEV, W, 2 * D // N_DEV), lambda i, l: (l[0], 0, 0, 0)), wsm, wsm,
               pl.BlockSpec((None, D, D), lambda i, l: (l[0], 0, 0))]
    return [br] * 4 + gates, weights


def merge_fwd(l, p, za, yg, z, o, x, pv, wc, wglu, wp, ws, wo, tm=512):
    T, D = x.shape
    tm = min(tm, T)

    def body(l_ref, za_ref, yg_ref, z_ref, o_ref, g0, g1, g2, g3, x_ref, pv_ref,
             wc_ref, wglu_ref, wp_ref, ws_ref, wo_ref, xn_ref, m_ref):
        _, _, _, _, merged = _merge_branches(za_ref, yg_ref, z_ref, o_ref, (g0, g1, g2, g3),
                                             wc_ref, wglu_ref, wp_ref, ws_ref)
        m = _dot(merged.astype(BF16), wo_ref[...])
        m_ref[...] = m
        xn_ref[...] = _postnorm_res(x_ref[...], m, pv_ref, 1.0)

    acts, weights = _merge_specs(tm, D)
    tile = pl.BlockSpec((tm, D), lambda i, l: (i, 0))
    grid_spec = pltpu.PrefetchScalarGridSpec(
        num_scalar_prefetch=1, grid=(T // tm,),
        in_specs=acts + [tile, pl.BlockSpec((8, D), lambda i, l: (0, 0))] + weights,
        out_specs=[tile, tile],
    )
    return pl.pallas_call(
        body, name="merge_fwd", grid_spec=grid_spec,
        out_shape=[jax.ShapeDtypeStruct((T, D), F32), jax.ShapeDtypeStruct((T, D), F32)],
        compiler_params=_cparams(("arbitrary",)),
    )(l, za, yg, z, o, p, p, p, p, x, pv, wc, wglu, wp, ws, wo)


def merge_bwd(l, p, za, yg, z, o, m, dxn, pv, wc, wglu, wp, ws, wo, tm=256):
    T, D = m.shape
    W = MIX_W
    tm = min(tm, T)
    ni = T // tm

    def body(l_ref, za_ref, yg_ref, z_ref, o_ref, g0, g1, g2, g3, m_ref, dxn_ref, pv_ref,
             wc_ref, wglu_ref, wp_ref, ws_ref, wo_ref,
             dza_ref, dyg_ref, dz_ref, do_ref, dg_ref, pg_ref, gwc_ref, gwglu_ref, gwp_ref, gws_ref, gwo_ref,
             awc, awglu, awp, aws, awo):
        i = pl.program_id(0)

        @pl.when(i == 0)
        def _():
            pg_ref[...] = jnp.zeros_like(pg_ref)
            for a in (awc, awglu, awp, aws, awo):
                a[...] = jnp.zeros_like(a)

        ys, gs, glu_a, sg, merged = _merge_branches(za_ref, yg_ref, z_ref, o_ref, (g0, g1, g2, g3),
                                                    wc_ref, wglu_ref, wp_ref, ws_ref)
        dm = _postnorm_bwd(dxn_ref[...], m_ref[...], pv_ref, pg_ref, 1.0).astype(BF16)
        awo[...] += _dot_tn(merged.astype(BF16), dm)
        dmerged = _dot_nt(dm, wo_ref[...])
        for b in range(4):
            dg_ref[:, b * D:(b + 1) * D] = (dmerged * ys[b] * gs[b] * (1.0 - gs[b])).astype(BF16)
        dya = (dmerged * gs[0]).astype(BF16)
        _acc_cols_tn(awc, za_ref[...], dya)
        dza_ref[...] = _dot_cols_nt(dya, wc_ref)
        dyc = (dmerged * gs[2]).astype(BF16)
        _acc_cols_tn(awp, z_ref[...], dyc)
        dz_ref[...] = _dot_cols_nt(dyc, wp_ref)
        dyd = (dmerged * gs[3]).astype(BF16)
        _acc_cols_tn(aws, o_ref[...], dyd)
        do_ref[...] = _dot_cols_nt(dyd, ws_ref)
        dyb = dmerged * gs[1]
        dglu = jnp.concatenate([dyb * sg, dyb * glu_a * sg * (1.0 - sg)], axis=1).astype(BF16)
        _acc_cols_tn(awglu, yg_ref[...], dglu)
        dyg_ref[...] = _dot_cols_nt(dglu, wglu_ref)

        @pl.when(i == ni - 1)
        def _():
            gwc_ref[...] = awc[...].astype(BF16)
            gwglu_ref[...] = awglu[...].astype(BF16)
            gwp_ref[...] = awp[...].astype(BF16)
            gws_ref[...] = aws[...].astype(BF16)
            gwo_ref[...] = awo[...].astype(BF16)

    acts, weights = _merge_specs(tm, D)
    tile = pl.BlockSpec((tm, D), lambda i, l: (i, 0))
    br = pl.BlockSpec((tm, W), lambda i, l: (i, 0))
    full = lambda *s: pl.BlockSpec(s, lambda i, l: (0,) * len(s))
    sm, glu_s = (N_DEV, W, D // N_DEV), (N_DEV, W, 2 * D // N_DEV)
    grid_spec = pltpu.PrefetchScalarGridSpec(
        num_scalar_prefetch=1, grid=(ni,),
        in_specs=acts + [tile, tile, pl.BlockSpec((8, D), lambda i, l: (0, 0))] + weights,
        out_specs=[br, br, br, br, pl.BlockSpec((tm, 4 * D), lambda i, l: (i, 0)), full(8, D),
                   full(*sm), full(*glu_s), full(*sm), full(*sm), full(D, D)],
        scratch_shapes=[pltpu.VMEM(sm, F32), pltpu.VMEM(glu_s, F32), pltpu.VMEM(sm, F32),
                        pltpu.VMEM(sm, F32), pltpu.VMEM((D, D), F32)],
    )
    f32br = jax.ShapeDtypeStruct((T, W), F32)
    return pl.pallas_call(
        body, name="merge_bwd", grid_spec=grid_spec,
        out_shape=[f32br, f32br, f32br, f32br, jax.ShapeDtypeStruct((T, 4 * D), BF16),
                   jax.ShapeDtypeStruct((8, D), F32),
                   jax.ShapeDtypeStruct(sm, BF16), jax.ShapeDtypeStruct(glu_s, BF16),
                   jax.ShapeDtypeStruct(sm, BF16), jax.ShapeDtypeStruct(sm, BF16),
                   jax.ShapeDtypeStruct((D, D), BF16)],
        compiler_params=_cparams(("arbitrary",)),
    )(l, za, yg, z, o, p, p, p, p, m, dxn, pv, wc, wglu, wp, ws, wo)


def dp_assemble(d_conv, d_ssm, d_pool, d_qkv, d_gates, tm=512):
    T = d_conv.shape[0]
    W = MIX_W

    def body(c_ref, s_ref, p_ref, q_ref, g_ref, dp_ref):
        dp_ref[:, 0:3 * W] = c_ref[...]
        dp_ref[:, 3 * W:4 * W] = s_ref[...]
        dp_ref[:, 4 * W:5 * W] = p_ref[...]
        dp_ref[:, 5 * W:8 * W] = q_ref[...]
        dp_ref[:, GATE_OFF:] = g_ref[...]

    row = lambda w: pl.BlockSpec((tm, w), lambda i: (i, 0))
    return pl.pallas_call(
        body, name="dp_assemble", grid=(T // tm,),
        in_specs=[row(3 * W), row(W), row(W), row(3 * W), row(4 * D_MODEL)],
        out_specs=row(IN_COLS),
        out_shape=jax.ShapeDtypeStruct((T, IN_COLS), BF16),
        compiler_params=_cparams(("arbitrary",)),
    )(d_conv, d_ssm, d_pool, d_qkv, d_gates)


def loss_head(y, target, tm=512):
    T, D = y.shape

    def body(y_ref, t_ref, dy_ref, loss_ref):
        @pl.when(pl.program_id(0) == 0)
        def _():
            loss_ref[...] = jnp.zeros_like(loss_ref)

        err = y_ref[...] - t_ref[...]
        dy_ref[...] = err * (1.0 / D)
        loss_ref[...] += jnp.sum(err * err) * (0.5 / D)

    tile = pl.BlockSpec((tm, D), lambda i: (i, 0))
    return pl.pallas_call(
        body, name="loss_head", grid=(T // tm,),
        in_specs=[tile, tile],
        out_specs=[tile, pl.BlockSpec((8, 128), lambda i: (0, 0))],
        out_shape=[jax.ShapeDtypeStruct((T, D), F32), jax.ShapeDtypeStruct((8, 128), F32)],
        compiler_params=_cparams(("arbitrary",)),
    )(y, target)


def cast_layer(ld, items):
    def body(ld_ref, *refs):
        n = len(refs) // 2
        for src, dst in zip(refs[:n], refs[n:]):
            dst[...] = src[...].astype(BF16)

    def shard(w, k):
        return w.shape[1:] if k is None else w.shape[2:]

    def in_spec(w, k):
        sh = shard(w, k)
        if k is None:
            return pl.BlockSpec((None,) + sh, lambda i, ld, n=len(sh): (ld[0],) + (0,) * n)
        return pl.BlockSpec((None, None) + sh, lambda i, ld, n=len(sh), k=k: (ld[0], k) + (0,) * n)

    def out_spec(w, k):
        sh = shard(w, k)
        return pl.BlockSpec((None, None) + sh, lambda i, ld, n=len(sh): (0, ld[1]) + (0,) * n)

    grid_spec = pltpu.PrefetchScalarGridSpec(
        num_scalar_prefetch=1, grid=(1,),
        in_specs=[in_spec(w, k) for w, k in items], out_specs=[out_spec(w, k) for w, k in items])
    return pl.pallas_call(
        body, name="cast_layer", grid_spec=grid_spec,
        out_shape=[jax.ShapeDtypeStruct((1, N_DEV) + shard(w, k), BF16) for w, k in items],
        compiler_params=_cparams(("arbitrary",)),
    )(ld, *[w for w, _ in items])


def place_own(dev, a):
    def body(dev_ref, a_ref, o_ref):
        o_ref[...] = a_ref[...]

    grid_spec = pltpu.PrefetchScalarGridSpec(
        num_scalar_prefetch=1, grid=(1,),
        in_specs=[pl.BlockSpec(a.shape, lambda i, dev: (0, 0))],
        out_specs=pl.BlockSpec((None,) + a.shape, lambda i, dev: (dev[0], 0, 0)))
    return pl.pallas_call(
        body, name="place_own", grid_spec=grid_spec,
        out_shape=jax.ShapeDtypeStruct((N_DEV,) + a.shape, a.dtype),
        compiler_params=_cparams(("arbitrary",)),
    )(dev, a)


def _silu(x):
    return x * _sigmoid(x)


def ada_fwd(c_all, w_ada, b_cols):
    L, D, n = w_ada.shape

    def body(c_ref, w_ref, b_ref, o_ref):
        c_act = _silu(c_ref[...]).astype(BF16)
        o_ref[...] = _dot(c_act, w_ref[...].astype(BF16)) + b_ref[...]

    return pl.pallas_call(
        body, name="ada_fwd", grid=(L,),
        in_specs=[pl.BlockSpec((N_DEV, D), lambda l: (0, 0)), pl.BlockSpec((None, D, n), lambda l: (l, 0, 0)),
                  pl.BlockSpec((None, 1, n), lambda l: (l, 0, 0))],
        out_specs=pl.BlockSpec((None, N_DEV, n), lambda l: (l, 0, 0)),
        out_shape=jax.ShapeDtypeStruct((L, N_DEV, n), F32),
        compiler_params=_cparams(("arbitrary",)),
    )(c_all, w_ada, b_cols)


def _adamw(w, g, m, v):
    m = ADAM_B1 * m + (1.0 - ADAM_B1) * g
    v = ADAM_B2 * v + (1.0 - ADAM_B2) * (g * g)
    m_hat = m / (1.0 - ADAM_B1 ** ADAM_STEP)
    v_hat = v / (1.0 - ADAM_B2 ** ADAM_STEP)
    delta = -ADAM_LR * (m_hat / (jnp.sqrt(v_hat) + ADAM_EPS) + ADAM_WD * w)
    return delta, m, v


def ada_update(c_all, dada_cols, w, m, v, rb=256):
    L, D, n = w.shape

    def body(c_ref, d_ref, w_ref, m_ref, v_ref, g_ref, dl_ref, nm_ref, nv_ref):
        c_act = _silu(c_ref[...]).astype(BF16)
        g = _dot_tn(c_act, d_ref[...].astype(BF16))
        g_ref[...] = g
        dl_ref[...], nm_ref[...], nv_ref[...] = _adamw(w_ref[...], g, m_ref[...], v_ref[...])

    blk = pl.BlockSpec((None, rb, n), lambda l, i: (l, i, 0))
    out = jax.ShapeDtypeStruct((L, D, n), F32)
    return pl.pallas_call(
        body, name="ada_update", grid=(L, D // rb),
        in_specs=[pl.BlockSpec((N_DEV, rb), lambda l, i: (0, i)),
                  pl.BlockSpec((None, N_DEV, n), lambda l, i: (l, 0, 0)), blk, blk, blk],
        out_specs=[blk, blk, blk, blk], out_shape=[out, out, out, out],
        compiler_params=_cparams(("arbitrary", "arbitrary")),
    )(c_all, dada_cols, w, m, v)


SUM_UPDATE_RECV_BYTES = 12 * 1024 * 1024


def sum_update(dev, first, recvs, owns, w, m, v, prev=None, after=None):
    n_slots, R, C = w.shape
    S = len(recvs)
    assert len(owns) == S and first + S <= n_slots
    rb_max = SUM_UPDATE_RECV_BYTES // (S * N_DEV * C * 2)
    rb = max(r for r in range(8, R + 1, 8) if R % r == 0 and (r <= rb_max or r == 8))
    last = R // rb - 1
    n_prev = 0 if prev is None else 4
    extra = list(prev or ()) + ([] if after is None else [after])

    def body(dev_ref, *refs):
        r_refs, o_refs = refs[:S], refs[S:2 * S]
        w_ref, m_ref, v_ref = refs[2 * S:2 * S + 3]
        g_ref, dl_ref, nm_ref, nv_ref = refs[2 * S + 3 + len(extra):]
        me = dev_ref[0]
        for s in range(S):
            @pl.when(pl.program_id(0) == s)
            def _(s=s):
                g = jnp.zeros((rb, C), F32)
                for d in range(N_DEV):
                    g += jnp.where(me == d, o_refs[s][...], r_refs[s][d]).astype(F32)
                g_ref[...] = g
                dl_ref[...], nm_ref[...], nv_ref[...] = _adamw(w_ref[...], g, m_ref[...], v_ref[...])

    def row(sl, i, s):
        return jnp.where(sl == s, i, jnp.where(sl < s, 0, last))

    def rspec(s):
        return pl.BlockSpec((N_DEV, rb, C), lambda sl, i, dev: (0, row(sl, i, s), 0))

    def ospec(s):
        return pl.BlockSpec((None, rb, C), lambda sl, i, dev: (dev[0], row(sl, i, s), 0))

    blk = pl.BlockSpec((None, rb, C), lambda sl, i, dev: (first + sl, i, 0))
    out = jax.ShapeDtypeStruct((n_slots, R, C), F32)
    grid_spec = pltpu.PrefetchScalarGridSpec(
        num_scalar_prefetch=1, grid=(S, R // rb),
        in_specs=[rspec(s) for s in range(S)] + [ospec(s) for s in range(S)] + [blk, blk, blk] + [ANY] * len(extra),
        out_specs=[blk, blk, blk, blk],
    )
    n_in = 1 + 2 * S + 3
    return pl.pallas_call(
        body, name="sum_update", grid_spec=grid_spec, out_shape=[out, out, out, out],
        input_output_aliases={n_in + i: i for i in range(n_prev)},
        compiler_params=_cparams(("arbitrary", "arbitrary")),
    )(dev, *recvs, *owns, w, m, v, *extra)


def small_sum(gathered):
    _, R, C = gathered.shape

    def body(g_ref, o_ref):
        acc = g_ref[0]
        for d in range(1, N_DEV):
            acc += g_ref[d]
        o_ref[...] = acc

    return pl.pallas_call(
        body, name="small_sum", grid=(1,),
        in_specs=[pl.BlockSpec((N_DEV, R, C), lambda i: (0, 0, 0))],
        out_specs=pl.BlockSpec((R, C), lambda i: (0, 0)),
        out_shape=jax.ShapeDtypeStruct((R, C), F32),
        compiler_params=_cparams(("arbitrary",)),
    )(gathered)


def small_update(w, g, m, v):
    def body(w_ref, g_ref, m_ref, v_ref, dl_ref, nm_ref, nv_ref):
        dl_ref[...], nm_ref[...], nv_ref[...] = _adamw(w_ref[...], g_ref[...], m_ref[...], v_ref[...])

    blk = pl.BlockSpec(w.shape, lambda i: (0, 0))
    out = jax.ShapeDtypeStruct(w.shape, F32)
    return pl.pallas_call(
        body, name="small_update", grid=(1,),
        in_specs=[blk] * 4, out_specs=[blk] * 3, out_shape=[out] * 3,
        compiler_params=_cparams(("arbitrary",)),
    )(w, g, m, v)


MESH = pl.DeviceIdType.MESH
ANY = pl.BlockSpec(memory_space=pl.ANY)


def _coords():
    return lax.axis_index("x"), lax.axis_index("y"), lax.axis_index("c")


def _dev_index(x, y, c):
    return 4 * x + 2 * y + c


def _at_dev(ref, p, dev):
    return ref.at[(slice(None),) * p + (dev,)]


def all_gather(arrays, ps):
    n = len(arrays)

    def body(*refs):
        ins, outs = refs[:n], refs[n:2 * n]
        send_sems, recv_sems, local_sems = refs[2 * n:]
        x, y, c = _coords()
        me, sibling = (x, y, c), (x, y, 1 - c)
        chips = [(1 - x, y), (x, 1 - y), (1 - x, 1 - y)]

        def copy(a, k, block, to, src=None):
            dst = _at_dev(outs[a], ps[a], _dev_index(*block))
            return pltpu.make_async_remote_copy(
                src_ref=dst if src is None else src, dst_ref=dst,
                send_sem=send_sems.at[a, k], recv_sem=recv_sems.at[a, k], device_id=to, device_id_type=MESH)

        mine = [pltpu.make_async_copy(ins[a], _at_dev(outs[a], ps[a], _dev_index(*me)), local_sems.at[a])
                for a in range(n)]
        for cp in mine:
            cp.start()
        first = []
        for a in range(n):
            first.append(copy(a, 0, me, sibling, src=ins[a]))
            first += [copy(a, 1 + j, me, (*chip, c), src=ins[a]) for j, chip in enumerate(chips)]
        for cp in first:
            cp.start()
        passed = []
        for j, chip in enumerate(chips):
            for a in range(n):
                copy(a, 1 + j, (*chip, c), me).wait_recv()
                fwd = copy(a, 4 + j, (*chip, c), sibling)
                fwd.start()
                passed.append(fwd)
        for a in range(n):
            copy(a, 0, sibling, me).wait_recv()
            for j, chip in enumerate(chips):
                copy(a, 4 + j, (*chip, 1 - c), me).wait_recv()
        for cp in first + passed:
            cp.wait_send()
        for cp in mine:
            cp.wait()

    out_shape = [jax.ShapeDtypeStruct(a.shape[:p] + (N_DEV,) + a.shape[p:], a.dtype) for a, p in zip(arrays, ps)]
    return pl.pallas_call(
        body, name="all_gather", in_specs=[ANY] * n, out_specs=[ANY] * n, out_shape=out_shape,
        scratch_shapes=[pltpu.SemaphoreType.DMA((n, 7)), pltpu.SemaphoreType.DMA((n, 7)),
                        pltpu.SemaphoreType.DMA((n,))],
        compiler_params=pltpu.CompilerParams(has_side_effects=True),
    )(*arrays)


HBM = pl.BlockSpec(memory_space=pltpu.HBM)
SEM = pl.BlockSpec(memory_space=pltpu.SEMAPHORE)
EFFECT = pltpu.SideEffectType.DATAFLOW_SIDE_EFFECTING


def _peers(x, y, c):
    out = []
    for k in range(1, N_DEV):
        out.append((1 - x if k & 4 else x, 1 - y if k & 2 else y, 1 - c if k & 1 else c))
    return out


def _exchange_plan(n):
    def plan(refs, x, y, c):
        blocks, lands = refs[:n], refs[n:2 * n]
        me = _dev_index(x, y, c)
        moves = []
        for peer in _peers(x, y, c):
            q = _dev_index(*peer)
            moves += [(blocks[a].at[q], lands[a].at[me], peer, lands[a].at[q]) for a in range(n)]
        return moves
    return plan


def _gather_plan(ps, second):
    def plan(refs, x, y, c):
        me, sibling = (x, y, c), (x, y, 1 - c)
        chips = [(1 - x, y), (x, 1 - y), (1 - x, 1 - y)]
        if second:
            trips = [((*ch, c), sibling, (*ch, 1 - c)) for ch in chips]
        else:
            trips = [(me, sibling, sibling)] + [(me, (*ch, c), (*ch, c)) for ch in chips]
        moves = []
        for sent, to, arriving in trips:
            for ref, p in zip(refs, ps):
                blk = _at_dev(ref, p, _dev_index(*sent))
                moves.append((blk, blk, to, _at_dev(ref, p, _dev_index(*arriving))))
        return moves
    return plan


def copies_start(name, plan, n_moves, arrays, carry):
    n = len(arrays)

    def body(*refs):
        sems = refs[n + 1:n + 1 + 2 * n_moves]
        moves = plan(refs[:n], *_coords())
        assert len(moves) == n_moves
        for i, (src, dst, to, _) in enumerate(moves):
            pltpu.make_async_remote_copy(src_ref=src, dst_ref=dst, send_sem=sems[i], recv_sem=sems[n_moves + i],
                                         device_id=to, device_id_type=MESH).start()

    operands = [pltpu.with_memory_space_constraint(a, pltpu.HBM) for a in list(arrays) + [carry]]
    outs = pl.pallas_call(
        body, name=name,
        out_shape=[pltpu.SemaphoreType.DMA(())] * (2 * n_moves) + [pltpu.HBM(a.shape, a.dtype) for a in operands],
        in_specs=[HBM] * (n + 1), out_specs=[SEM] * (2 * n_moves) + [HBM] * (n + 1),
        input_output_aliases={i: 2 * n_moves + i for i in range(n + 1)},
        compiler_params=pltpu.CompilerParams(has_side_effects=EFFECT),
    )(*operands)
    return outs[:n_moves], outs[n_moves:2 * n_moves], outs[2 * n_moves:-1], outs[-1]


def copies_wait(name, plan, send_sems, recv_sems, arrays, after):
    n, n_moves = len(arrays), len(send_sems)

    def body(*refs):
        sems = refs[n:n + 2 * n_moves]
        for i, (src, _, to, arriving) in enumerate(plan(refs[:n], *_coords())):
            cp = pltpu.make_async_remote_copy(src_ref=src, dst_ref=arriving, send_sem=sems[i],
                                              recv_sem=sems[n_moves + i], device_id=to, device_id_type=MESH)
            cp.wait_send()
            cp.wait_recv()

    return pl.pallas_call(
        body, name=name,
        out_shape=[pltpu.HBM(a.shape, a.dtype) for a in arrays],
        in_specs=[HBM] * n + [SEM] * (2 * n_moves) + [ANY], out_specs=[HBM] * n,
        input_output_aliases={i: i for i in range(n)},
        compiler_params=pltpu.CompilerParams(has_side_effects=EFFECT),
    )(*arrays, *send_sems, *recv_sems, after)


WEIGHT_NAMES = ("w_ada", "b_ada", "g_pre", "g_post", "w_ff_in", "w_ff_out", "w_in", "conv_w", "w_conv_out",
                "lam_re", "lam_im", "log_dt", "ssm_b_re", "ssm_b_im", "ssm_c_re", "ssm_c_im", "ssm_d", "w_glu",
                "w_pool", "pool_scale", "w_pool_out", "w_sb_out", "w_out")
BIG_NAMES = ("w_ff_in", "w_ff_out", "w_in", "w_conv_out", "w_glu", "w_pool_out", "w_sb_out", "w_out")
SMALL_NAMES = ("b_ada", "g_pre", "g_post", "conv_w", "lam_re", "lam_im", "log_dt", "ssm_b_re", "ssm_b_im",
               "ssm_c_re", "ssm_c_im", "ssm_d", "w_pool", "pool_scale")
PACK_LANES = 128
PACK_ROWS = 8


def _pack(arrays):
    flat = jnp.concatenate([a.reshape(-1) for a in arrays])
    unit = PACK_LANES * PACK_ROWS
    flat = jnp.pad(flat, (0, -flat.shape[0] % unit))
    return flat.reshape(-1, PACK_LANES)


def _unpack(pack, shapes):
    flat = pack.reshape(-1)
    out, off = [], 0
    for s in shapes:
        n = 1
        for d in s:
            n *= d
        out.append(flat[off:off + n].reshape(s))
        off += n
    return out


def _pad_rows(a, rows=8):
    return jnp.pad(a, ((0, 0), (0, rows - a.shape[1]), (0, 0)))


def _tile_b(b):
    L = b.shape[0]
    return jnp.tile(b.transpose(0, 3, 1, 2).reshape(L, SSM_GROUP, SSM_W), (1, SSM_GROUPS, 1))


def _tile_c(c):
    L = c.shape[0]
    return jnp.tile(c.transpose(0, 3, 1, 2).reshape(L, SSM_STATE, MIX_W), (1, SSM_GROUPS, 1))


def _step(x, c, target, W, M, V):
    T, D = x.shape[1], x.shape[2]
    L = W["w_ada"].shape[0]
    x = x[0]
    target = target[0]
    ax, ay, ac = _coords()
    dev = _dev_index(ax, ay, ac)
    n_ada = W["w_ada"].shape[2]

    dev_s = jnp.reshape(dev, (1,)).astype(jnp.int32)
    items = ([(W["w_ff_in"], 0), (W["w_ff_in"], 1), (W["w_ff_out"], 0), (W["w_ff_out"], 1)]
             + [(W[k], None) for k in BIG_NAMES[2:]])
    bufs = [list(cast_layer(jnp.concatenate([jnp.array([l], jnp.int32), dev_s]), items)) for l in range(L)]
    ffn1_w, mixer_w, ffn2_w = (0, 2), (4, 5, 6, 7, 8, 9), (1, 3)
    all_w = tuple(range(len(items)))

    def gather_start(tag, second, l, idx, carry):
        plan = _gather_plan((1,) * len(idx), second)
        n_moves = (3 if second else 4) * len(idx)
        s_sem, r_sem, arrs, carry = copies_start(f"gather_{'b' if second else 'a'}_start_{tag}", plan, n_moves,
                                                 [bufs[l][i] for i in idx], carry)
        for i, a in zip(idx, arrs):
            bufs[l][i] = a
        return (plan, s_sem, r_sem), carry

    def gather_wait(tag, second, l, idx, flight, after):
        arrs = copies_wait(f"gather_{'b' if second else 'a'}_wait_{tag}", *flight, [bufs[l][i] for i in idx], after)
        for i, a in zip(idx, arrs):
            bufs[l][i] = a

    def gather_finish(tag, l, idx, flight, after, carry):
        gather_wait(tag, False, l, idx, flight, after)
        flight, carry = gather_start(tag, True, l, idx, carry)
        gather_wait(tag, True, l, idx, flight, carry)
        return carry

    first = []
    for g, idx in enumerate((ffn1_w, mixer_w, ffn2_w)):
        flight, x = gather_start(f"0_{g}", False, 0, idx, x)
        first.append(flight)

    gathered = all_gather([W["g_pre"], W["g_post"], W["conv_w"], c], [0, 0, 0, 0])
    g_pre = gathered[0].transpose(1, 2, 0, 3).reshape(L, N_SUB, D)
    g_post = gathered[1].transpose(1, 2, 0, 3).reshape(L, N_SUB, D)
    conv_w = _pad_rows(gathered[2].transpose(1, 2, 0, 3).reshape(L, 3, MIX_W))
    c_all = gathered[3].reshape(N_DEV, D)

    b_cols = lax.dynamic_slice_in_dim(W["b_ada"], dev * n_ada, n_ada, axis=1)[:, None, :]
    ada_cols = ada_fwd(c_all, W["w_ada"], b_cols)
    ada_all = all_gather([ada_cols], [0])[0]
    ada = lax.dynamic_index_in_dim(ada_all, dev, axis=2, keepdims=False)
    ada = ada.transpose(1, 0, 2).reshape(L, N_SUB, 3, D)
    zeros = jnp.zeros((L, N_SUB, D), F32)
    pv_all = jnp.stack([g_pre, ada[:, :, 0], ada[:, :, 1], g_post, ada[:, :, 2], zeros, zeros, zeros], axis=2)

    lam = jnp.stack([W["lam_re"].reshape(L, SSM_W), W["lam_im"].reshape(L, SSM_W),
                     jnp.repeat(W["log_dt"], SSM_STATE, axis=1)], axis=1)
    lam = _pad_rows(lam)
    b_t = jnp.stack([_tile_b(W["ssm_b_re"]), _tile_b(W["ssm_b_im"])], axis=1)
    c_t = jnp.stack([_tile_c(W["ssm_c_re"]), _tile_c(W["ssm_c_im"])], axis=1)
    avec, b_bd, c_bd = s5_params(lam, b_t, c_t)
    ssm_d = _pad_rows(W["ssm_d"][:, None, :])
    pool_scale = _pad_rows(W["pool_scale"][:, None, :])
    eye4 = jnp.eye(len(POOL_WINDOWS), dtype=F32)
    w_bd = jnp.einsum("lgcd,gh->lgchd", W["w_pool"], eye4).reshape(L, MIX_W, MIX_W).astype(BF16)

    x = gather_finish("0_0", 0, ffn1_w, first[0], pv_all, x)

    def ffn_weights(l, k):
        b = bufs[l]
        return b[k].reshape(1, 1, 2, 4, D, FF_BLK), b[2 + k].reshape(1, 1, 4, FF_BLK, D)

    def mixer_weights(l):
        b = bufs[l]
        return b[4], b[5], b[6], b[7], b[8], b[9].reshape(1, D, D)

    l0 = jnp.array([0], jnp.int32)
    k0 = jnp.array([0, 0], jnp.int32)
    saved = []
    for l in range(L):
        li = jnp.array([l], jnp.int32)
        nxt = l + 1 < L
        if nxt:
            flight, x = gather_start(f"{l + 1}", False, l + 1, all_w, x)
        x0 = x
        ab0, f0, x1 = ffn_fwd(k0, x0, pv_all[l, 0], *ffn_weights(l, 0))
        if l == 0:
            x1 = gather_finish("0_1", 0, mixer_w, first[1], x1, x1)
        wg_in, wg_conv, wg_glu, wg_pool, wg_sb, wg_out = mixer_weights(l)
        p = mix_in_fwd(l0, x1, pv_all[l, 1], wg_in)
        za = conv_fwd(li, p, conv_w)
        z = pool_fwd(li, p, w_bd, pool_scale)
        s = s5_scan(li, avec, s5_bu(li, p, b_bd), False)
        yg = s5_out(li, p, s, c_bd, ssm_d)
        o, sb_tot = sb_fwd(p)
        x2, m = merge_fwd(l0, p, za, yg, z, o, x1, pv_all[l, 1], wg_conv, wg_glu, wg_pool, wg_sb, wg_out)
        if l == 0:
            x2 = gather_finish("0_2", 0, ffn2_w, first[2], x2, x2)
        if nxt:
            gather_wait(f"{l + 1}", False, l + 1, all_w, flight, x2)
            flight, x2 = gather_start(f"{l + 1}", True, l + 1, all_w, x2)
        ab1, f1, x = ffn_fwd(k0, x2, pv_all[l, 2], *ffn_weights(l, 1))
        if nxt:
            gather_wait(f"{l + 1}", True, l + 1, all_w, flight, x)
        saved.append((x0, ab0, f0, x1, p, za, z, s, yg, o, sb_tot, m, x2, ab1, f1))

    dx, loss_blk = loss_head(x, target)
    loss = lax.psum(loss_blk[0, 0], ("x", "y", "c"))

    n_blocks = 10
    ffn2_g, mixer_g, ffn1_g = (1, 3), (4, 5, 6, 7, 8, 9), (0, 2)
    recvs, owns, in_flight = [[None] * n_blocks for _ in range(L)], [[None] * n_blocks for _ in range(L)], []

    def exchange_start(tag, layer, idx, blocks, carry):
        n = len(idx)
        plan = _exchange_plan(n)
        arrays = list(blocks) + [lax.empty(a.shape, a.dtype) for a in blocks]
        s_sem, r_sem, arrays, carry = copies_start(f"exchange_start_{tag}", plan, (N_DEV - 1) * n, arrays, carry)
        in_flight.append((tag, layer, idx, plan, s_sem, r_sem, arrays))
        return carry

    def settle(after, upto):
        for flight in [f for f in in_flight if f[1] >= upto]:
            in_flight.remove(flight)
            tag, layer, idx, plan, s_sem, r_sem, arrays = flight
            arrays = copies_wait(f"exchange_wait_{tag}", plan, s_sem, r_sem, arrays, after)
            for j, i in enumerate(idx):
                owns[layer][i], recvs[layer][i] = arrays[j], arrays[len(idx) + j]

    pgs = [None] * L
    small = {k: [None] * L for k in ("conv_w", "w_bd", "pool_scale", "ssm_d", "gb", "gc", "da")}
    for l in reversed(range(L)):
        li = jnp.array([l], jnp.int32)
        x0, ab0, f0, x1, p, za, z, s, yg, o, sb_tot, m, x2, ab1, f1 = saved[l]
        wg_in, wg_conv, wg_glu, wg_pool, wg_sb, wg_out = mixer_weights(l)
        g_in1, g_out1, dx, pg2 = ffn_bwd(k0, dx, x2, f1, pv_all[l, 2], ab1, *ffn_weights(l, 1))
        dx = exchange_start(f"{l}_ffn2", l, ffn2_g,
                            [g_in1.reshape(N_DEV, D, FF_BLK), g_out1.reshape(N_DEV, D_FF // N_DEV, D)], dx)
        (dza, dyg, dz, do, dgates, pg1m, g_conv, g_glu, g_pool, g_sb, g_wo) = merge_bwd(
            l0, p, za, yg, z, o, m, dx, pv_all[l, 1], wg_conv, wg_glu, wg_pool, wg_sb, wg_out)
        d_conv, small["conv_w"][l] = conv_bwd(li, p, dza, conv_w)
        d_pool, small["w_bd"][l], small["pool_scale"][l] = pool_bwd(li, p, dz, w_bd, pool_scale)
        ds, du_skip, small["gc"][l], small["ssm_d"][l] = s5_bwd_y(li, p, s, dyg, c_bd, ssm_d)
        lam_s = s5_scan(li, avec, ds, True)
        d_ssm, small["gb"][l] = s5_bwd_u(li, p, lam_s, du_skip, b_bd)
        small["da"][l] = s5_bwd_a(s, lam_s)
        d_qkv = sb_bwd(p, do, sb_tot)
        dp = dp_assemble(d_conv, d_ssm, d_pool, d_qkv, dgates)
        dx, g_win, pg1i = mix_in_bwd(l0, dp, dx, x1, pv_all[l, 1], wg_in)
        settle(dx, l + 1)
        dx = exchange_start(f"{l}_mixer", l, mixer_g,
                            [g_win, g_conv, g_glu, g_pool, g_sb, g_wo.reshape(N_DEV, D // N_DEV, D)], dx)
        g_in0, g_out0, dx, pg0 = ffn_bwd(k0, dx, x0, f0, pv_all[l, 0], ab0, *ffn_weights(l, 0))
        pgs[l] = jnp.stack([pg0, pg1m + pg1i, pg2])
        last_ffn1 = [g_in0.reshape(N_DEV, D, FF_BLK), g_out0.reshape(N_DEV, D_FF // N_DEV, D)]
        if l > 0:
            dx = exchange_start(f"{l}_ffn1", l, ffn1_g, last_ffn1, dx)

    dlam, db_t, dc_t, dldt = s5_params_bwd(lam, b_t, jnp.stack(small["gb"]), jnp.stack(small["gc"]),
                                           jnp.stack(small["da"]))
    pg = jnp.stack(pgs)
    d_ada = jnp.stack([pg[:, :, PV_SHIFT], pg[:, :, PV_SCALE], pg[:, :, PV_GATE]], axis=2).reshape(L, N_SUB * 3 * D)
    db = db_t.reshape(L, 2, SSM_GROUPS, SSM_GROUP, SSM_GROUPS, SSM_STATE)
    db = jnp.einsum("lrghgp->lrgph", db)
    dc = dc_t.reshape(L, 2, SSM_GROUPS, SSM_STATE, SSM_GROUPS, SSM_GROUP)
    dc = jnp.einsum("lrgpgh->lrghp", dc)
    d_wpool = jnp.einsum("lgcgd->lgcd", jnp.stack(small["w_bd"]).reshape(L, 4, 64, 4, 64))
    contrib = {
        "b_ada": d_ada, "g_pre": pg[:, :, PV_GPRE], "g_post": pg[:, :, PV_GPOST],
        "conv_w": jnp.stack(small["conv_w"])[:, :3], "lam_re": dlam[:, 0].reshape(L, SSM_GROUPS, SSM_STATE),
        "lam_im": dlam[:, 1].reshape(L, SSM_GROUPS, SSM_STATE), "log_dt": dldt[:, 2, :SSM_GROUPS],
        "ssm_b_re": db[:, 0], "ssm_b_im": db[:, 1], "ssm_c_re": dc[:, 0], "ssm_c_im": dc[:, 1],
        "ssm_d": jnp.stack(small["ssm_d"])[:, 0], "w_pool": d_wpool,
        "pool_scale": jnp.stack(small["pool_scale"])[:, 0],
    }
    contrib_shapes = [contrib[k].shape for k in SMALL_NAMES]

    big_idx = {"w_ff_in": (0, 1), "w_ff_out": (2, 3), "w_in": (4,), "w_conv_out": (5,), "w_glu": (6,),
               "w_pool_out": (7,), "w_sb_out": (8,), "w_out": (9,)}

    def big(name, layers, prev, after=None):
        idx = big_idx[name]
        flat = (-1,) + W[name].shape[-2:]
        return sum_update(dev_s, layers[0] * len(idx), [recvs[l][i] for l in layers for i in idx],
                          [owns[l][i] for l in layers for i in idx],
                          W[name].reshape(flat), M[name].reshape(flat), V[name].reshape(flat), prev, after)

    partial = {}

    def partial_updates(names, after):
        for name in names:
            if L > 1:
                partial[name] = big(name, list(range(1, L)), None, after)
                after = partial[name][0]
        return after

    pack_buf = [place_own(dev_s, _pack([contrib[k] for k in SMALL_NAMES]))]
    plan_a, plan_b = _gather_plan((0,), False), _gather_plan((0,), True)
    s_sem, r_sem, pack_buf, dx = copies_start("small_gather_a_start", plan_a, 4, pack_buf, dx)
    dx = exchange_start("0_ffn1", 0, ffn1_g, last_ffn1, dx)
    after = partial_updates(BIG_NAMES[:1], dx)
    pack_buf = copies_wait("small_gather_a_wait", plan_a, s_sem, r_sem, pack_buf, after)
    s_sem, r_sem, pack_buf, dx = copies_start("small_gather_b_start", plan_b, 3, pack_buf, dx)
    after = partial_updates(BIG_NAMES[1:], dx)
    pack_all = copies_wait("small_gather_b_wait", plan_b, s_sem, r_sem, pack_buf, after)[0]
    total = dict(zip(SMALL_NAMES, _unpack(small_sum(pack_all), contrib_shapes)))
    d_ada_all = pack_all.reshape(N_DEV, -1)[:, :L * N_SUB * 3 * D].reshape(N_DEV, L, N_SUB * 3 * D)
    dada_cols = lax.dynamic_slice_in_dim(d_ada_all, dev * n_ada, n_ada, axis=2).transpose(1, 0, 2)
    n_g = D // N_DEV
    grads = {}
    for k in SMALL_NAMES:
        g = total[k]
        if k in ("g_pre", "g_post"):
            g = lax.dynamic_slice_in_dim(g, dev * n_g, n_g, axis=2)
        elif k == "conv_w":
            g = lax.dynamic_slice_in_dim(g, dev * (MIX_W // N_DEV), MIX_W // N_DEV, axis=2)
        grads[k] = g

    delta, new_m, new_v = {}, {}, {}
    shapes = [W[k].shape for k in SMALL_NAMES]
    dl, nm, nv = small_update(_pack([W[k] for k in SMALL_NAMES]), _pack([grads[k] for k in SMALL_NAMES]),
                              _pack([M[k] for k in SMALL_NAMES]), _pack([V[k] for k in SMALL_NAMES]))
    for k, a, b, cc in zip(SMALL_NAMES, _unpack(dl, shapes), _unpack(nm, shapes), _unpack(nv, shapes)):
        delta[k], new_m[k], new_v[k] = a, b, cc
    grads["w_ada"], delta["w_ada"], new_m["w_ada"], new_v["w_ada"] = ada_update(
        c_all, dada_cols, W["w_ada"], M["w_ada"], V["w_ada"])

    settle(new_m["w_ada"], 0)
    for name in BIG_NAMES:
        outs = big(name, [0], partial.get(name))
        grads[name], delta[name], new_m[name], new_v[name] = [o.reshape(W[name].shape) for o in outs]

    return (loss, dx[None], *[grads[k] for k in WEIGHT_NAMES], *[delta[k] for k in WEIGHT_NAMES],
            *[new_m[k] for k in WEIGHT_NAMES], *[new_v[k] for k in WEIGHT_NAMES])


def kernel(x, c, w_ada, b_ada, g_pre, g_post, w_ff_in, w_ff_out, w_in, conv_w, w_conv_out, lam_re, lam_im, log_dt, ssm_b_re, ssm_b_im, ssm_c_re, ssm_c_im, ssm_d, w_glu, w_pool, pool_scale, w_pool_out, w_sb_out, w_out, loss_target, m_w_ada, m_b_ada, m_g_pre, m_g_post, m_w_ff_in, m_w_ff_out, m_w_in, m_conv_w, m_w_conv_out, m_lam_re, m_lam_im, m_log_dt, m_ssm_b_re, m_ssm_b_im, m_ssm_c_re, m_ssm_c_im, m_ssm_d, m_w_glu, m_w_pool, m_pool_scale, m_w_pool_out, m_w_sb_out, m_w_out, v_w_ada, v_b_ada, v_g_pre, v_g_post, v_w_ff_in, v_w_ff_out, v_w_in, v_conv_w, v_w_conv_out, v_lam_re, v_lam_im, v_log_dt, v_ssm_b_re, v_ssm_b_im, v_ssm_c_re, v_ssm_c_im, v_ssm_d, v_w_glu, v_w_pool, v_pool_scale, v_w_pool_out, v_w_sb_out, v_w_out):
    w = (w_ada, b_ada, g_pre, g_post, w_ff_in, w_ff_out, w_in, conv_w, w_conv_out, lam_re, lam_im, log_dt, ssm_b_re, ssm_b_im, ssm_c_re, ssm_c_im, ssm_d, w_glu, w_pool, pool_scale, w_pool_out, w_sb_out, w_out)
    m = (m_w_ada, m_b_ada, m_g_pre, m_g_post, m_w_ff_in, m_w_ff_out, m_w_in, m_conv_w, m_w_conv_out, m_lam_re, m_lam_im, m_log_dt, m_ssm_b_re, m_ssm_b_im, m_ssm_c_re, m_ssm_c_im, m_ssm_d, m_w_glu, m_w_pool, m_pool_scale, m_w_pool_out, m_w_sb_out, m_w_out)
    v = (v_w_ada, v_b_ada, v_g_pre, v_g_post, v_w_ff_in, v_w_ff_out, v_w_in, v_conv_w, v_w_conv_out, v_lam_re, v_lam_im, v_log_dt, v_ssm_b_re, v_ssm_b_im, v_ssm_c_re, v_ssm_c_im, v_ssm_d, v_w_glu, v_w_pool, v_pool_scale, v_w_pool_out, v_w_sb_out, v_w_out)
    return _step(x, c, loss_target, dict(zip(WEIGHT_NAMES, w)), dict(zip(WEIGHT_NAMES, m)), dict(zip(WEIGHT_NAMES, v)))
```

```python
import functools

import jax
import jax.numpy as jnp
from jax import lax
from jax.experimental import pallas as pl
from jax.experimental.pallas import tpu as pltpu

F32 = jnp.float32
BF16 = jnp.bfloat16

N_DEV = 8
D_MODEL = 1024
D_FF = 2816
FF_BLK = D_FF // 4
N_SUB = 3
MIX_W = 256
IN_COLS = 6144
IN_BLK = IN_COLS // N_DEV
GATE_OFF = 2048
SSM_GROUPS, SSM_GROUP, SSM_STATE = 16, 16, 64
SSM_W = SSM_GROUPS * SSM_STATE
POOL_WINDOWS = (2, 4, 8, 16)
SB_HEAD = 64
EPS = 1e-6
DT_LAMBDA_RE_MAX = -1e-4
ADAM_LR, ADAM_B1, ADAM_B2, ADAM_EPS, ADAM_WD, ADAM_STEP = 0.001, 0.9, 0.999, 1e-08, 0.01, 10

VMEM_LIMIT = 56 * 1024 * 1024
FFN_BWD_VMEM_LIMIT = 60 * 1024 * 1024

PV_GPRE, PV_SHIFT, PV_SCALE, PV_GPOST, PV_GATE = 0, 1, 2, 3, 4


def _cparams(sem):
    return pltpu.CompilerParams(dimension_semantics=sem, vmem_limit_bytes=VMEM_LIMIT)


def _dot(a, b):
    return jnp.dot(a, b, preferred_element_type=F32)


def _dot_nt(a, b):
    return lax.dot_general(a, b, (((1,), (1,)), ((), ())), preferred_element_type=F32)


def _dot_tn(a, b):
    return lax.dot_general(a, b, (((0,), (0,)), ((), ())), preferred_element_type=F32)


def _rms(x):
    r = lax.rsqrt(jnp.mean(x * x, axis=-1, keepdims=True) + EPS)
    return x * r, r


def _rms_bwd(dn, n, r):
    return r * (dn - n * jnp.mean(dn * n, axis=-1, keepdims=True))


def _sigmoid(x):
    return 1.0 / (1.0 + jnp.exp(-x))


def _colsum(x):
    return jnp.sum(x, axis=0, keepdims=True)


def _prenorm(x, pv_ref):
    n, r = _rms(x)
    hn = n * pv_ref[PV_GPRE:PV_GPRE + 1, :]
    h = hn * (1.0 + pv_ref[PV_SCALE:PV_SCALE + 1, :]) + pv_ref[PV_SHIFT:PV_SHIFT + 1, :]
    return h, n, r, hn


def _prenorm_bwd(dh, dxn, x, pv_ref, pg_ref):
    _, n, r, hn = _prenorm(x, pv_ref)
    pg_ref[PV_SHIFT:PV_SHIFT + 1, :] += _colsum(dh)
    pg_ref[PV_SCALE:PV_SCALE + 1, :] += _colsum(dh * hn)
    dhn = dh * (1.0 + pv_ref[PV_SCALE:PV_SCALE + 1, :])
    pg_ref[PV_GPRE:PV_GPRE + 1, :] += _colsum(dhn * n)
    dn = dhn * pv_ref[PV_GPRE:PV_GPRE + 1, :]
    return dxn + _rms_bwd(dn, n, r)


def _postnorm_res(x, f, pv_ref, coef):
    nf, _ = _rms(f)
    return x + (coef * (1.0 + pv_ref[PV_GATE:PV_GATE + 1, :])) * (nf * pv_ref[PV_GPOST:PV_GPOST + 1, :])


def _postnorm_bwd(dxn, f, pv_ref, pg_ref, coef):
    nf, rf = _rms(f)
    g_post = pv_ref[PV_GPOST:PV_GPOST + 1, :]
    pg_ref[PV_GATE:PV_GATE + 1, :] += _colsum(dxn * (nf * g_post)) * coef
    dnfg = dxn * (coef * (1.0 + pv_ref[PV_GATE:PV_GATE + 1, :]))
    pg_ref[PV_GPOST:PV_GPOST + 1, :] += _colsum(dnfg * nf)
    return _rms_bwd(dnfg * g_post, nf, rf)


def ffn_fwd(lk, x, pv, wg_in, wg_out, tm=512):
    T, D = x.shape
    tm = min(tm, T)
    nj, ni = 4, T // tm

    def body(lk_ref, x_ref, pv_ref, win_ref, wout_ref, ab_ref, f_ref, xn_ref, h_all, acc_all):
        j, i = pl.program_id(0), pl.program_id(1)
        rows = pl.ds(pl.multiple_of(i * tm, tm), tm)

        @pl.when(j == 0)
        def _():
            h, _, _, _ = _prenorm(x_ref[...], pv_ref)
            h_all[rows, :] = h.astype(BF16)
            acc_all[rows, :] = jnp.zeros((tm, D), F32)

        h = h_all[rows, :]
        a = _dot(h, win_ref[0])
        b = _dot(h, win_ref[1])
        ab_ref[0] = a.astype(BF16)
        ab_ref[1] = b.astype(BF16)
        act = (a * _sigmoid(a) * b).astype(BF16)
        acc_all[rows, :] += _dot(act, wout_ref[...])

        @pl.when(j == nj - 1)
        def _():
            f = acc_all[rows, :]
            f_ref[...] = f
            xn_ref[...] = _postnorm_res(x_ref[...], f, pv_ref, 0.5)

    ends = lambda j, i, lk: (jnp.where((j == 0) | (j == nj - 1), i, ni - 1), 0)
    last = lambda j, i, lk: (jnp.where(j == nj - 1, i, 0), 0)
    grid_spec = pltpu.PrefetchScalarGridSpec(
        num_scalar_prefetch=1, grid=(nj, ni),
        in_specs=[
            pl.BlockSpec((tm, D), ends),
            pl.BlockSpec((8, D), lambda j, i, lk: (0, 0)),
            pl.BlockSpec((None, None, 2, None, D, FF_BLK), lambda j, i, lk: (lk[0], lk[1], 0, j, 0, 0)),
            pl.BlockSpec((None, None, None, FF_BLK, D), lambda j, i, lk: (lk[0], lk[1], j, 0, 0)),
        ],
        out_specs=[
            pl.BlockSpec((2, None, tm, FF_BLK), lambda j, i, lk: (0, j, i, 0)),
            pl.BlockSpec((tm, D), last),
            pl.BlockSpec((tm, D), last),
        ],
        scratch_shapes=[pltpu.VMEM((T, D), BF16), pltpu.VMEM((T, D), F32)],
    )
    return pl.pallas_call(
        body, name="ffn_fwd", grid_spec=grid_spec,
        out_shape=[jax.ShapeDtypeStruct((2, nj, T, FF_BLK), BF16),
                   jax.ShapeDtypeStruct((T, D), F32), jax.ShapeDtypeStruct((T, D), F32)],
        compiler_params=_cparams(("arbitrary", "arbitrary")),
    )(lk, x, pv, wg_in, wg_out)


def ffn_bwd(lk, dxn, x, f, pv, ab, wg_in, wg_out, tm=256):
    T, D = x.shape
    tm = min(tm, T)
    nj, ni = 4, T // tm

    def body(lk_ref, dxn_ref, x_ref, f_ref, pv_ref, ab_ref, win_ref, wout_ref,
             gin_ref, gout_ref, dx_ref, pg_ref, df_all, h_all, dh_all, acc_in, acc_out):
        j, i = pl.program_id(0), pl.program_id(1)
        rows = pl.ds(pl.multiple_of(i * tm, tm), tm)

        @pl.when((i == 0) & (j == 0))
        def _():
            pg_ref[...] = jnp.zeros_like(pg_ref)

        @pl.when(j == 0)
        def _():
            df = _postnorm_bwd(dxn_ref[...], f_ref[...], pv_ref, pg_ref, 0.5)
            df_all[rows, :] = df.astype(BF16)
            h, _, _, _ = _prenorm(x_ref[...], pv_ref)
            h_all[rows, :] = h.astype(BF16)
            dh_all[rows, :] = jnp.zeros((tm, D), F32)

        @pl.when(i == 0)
        def _():
            acc_in[...] = jnp.zeros_like(acc_in)
            acc_out[...] = jnp.zeros_like(acc_out)

        df_t = df_all[rows, :]
        h_t = h_all[rows, :]
        dact = _dot_nt(df_t, wout_ref[...])
        a = ab_ref[0].astype(F32)
        b = ab_ref[1].astype(F32)
        sig = _sigmoid(a)
        s = a * sig
        da = (dact * b * (sig * (1.0 + a * (1.0 - sig)))).astype(BF16)
        db = (dact * s).astype(BF16)
        act = (s * b).astype(BF16)
        dh_all[rows, :] += _dot_nt(da, win_ref[0]) + _dot_nt(db, win_ref[1])
        acc_in[0] += _dot_tn(h_t, da)
        acc_in[1] += _dot_tn(h_t, db)
        acc_out[...] += _dot_tn(act, df_t)

        @pl.when(i == ni - 1)
        def _():
            gin_ref[...] = acc_in[...].astype(BF16)
            gout_ref[...] = acc_out[...].astype(BF16)

        @pl.when(j == nj - 1)
        def _():
            dx_ref[...] = _prenorm_bwd(dh_all[rows, :], dxn_ref[...], x_ref[...], pv_ref, pg_ref)

    ends = lambda j, i, lk: (jnp.where((j == 0) | (j == nj - 1), i, ni - 1), 0)
    first = lambda j, i, lk: (jnp.where(j == 0, i, ni - 1), 0)
    grid_spec = pltpu.PrefetchScalarGridSpec(
        num_scalar_prefetch=1, grid=(nj, ni),
        in_specs=[pl.BlockSpec((tm, D), ends), pl.BlockSpec((tm, D), ends), pl.BlockSpec((tm, D), first),
                  pl.BlockSpec((8, D), lambda j, i, lk: (0, 0)),
                  pl.BlockSpec((2, None, tm, FF_BLK), lambda j, i, lk: (0, j, i, 0)),
                  pl.BlockSpec((None, None, 2, None, D, FF_BLK), lambda j, i, lk: (lk[0], lk[1], 0, j, 0, 0)),
                  pl.BlockSpec((None, None, None, FF_BLK, D), lambda j, i, lk: (lk[0], lk[1], j, 0, 0))],
        out_specs=[pl.BlockSpec((2, None, D, FF_BLK), lambda j, i, lk: (0, j, 0, 0)),
                   pl.BlockSpec((None, FF_BLK, D), lambda j, i, lk: (j, 0, 0)),
                   pl.BlockSpec((tm, D), lambda j, i, lk: (jnp.where(j == nj - 1, i, 0), 0)),
                   pl.BlockSpec((8, D), lambda j, i, lk: (0, 0))],
        scratch_shapes=[pltpu.VMEM((T, D), BF16), pltpu.VMEM((T, D), BF16), pltpu.VMEM((T, D), F32),
                        pltpu.VMEM((2, D, FF_BLK), F32), pltpu.VMEM((FF_BLK, D), F32)],
    )
    return pl.pallas_call(
        body, name="ffn_bwd", grid_spec=grid_spec,
        out_shape=[jax.ShapeDtypeStruct((2, nj, D, FF_BLK), BF16), jax.ShapeDtypeStruct((nj, FF_BLK, D), BF16),
                   jax.ShapeDtypeStruct((T, D), F32), jax.ShapeDtypeStruct((8, D), F32)],
        compiler_params=pltpu.CompilerParams(dimension_semantics=("arbitrary", "arbitrary"),
                                             vmem_limit_bytes=FFN_BWD_VMEM_LIMIT),
    )(lk, dxn, x, f, pv, ab, wg_in, wg_out)


def mix_in_fwd(l, x, pv, wg, tm=1024):
    T, D = x.shape
    tm = min(tm, T)

    def body(l_ref, x_ref, pv_ref, w_ref, p_ref, h_sc):
        @pl.when(pl.program_id(1) == 0)
        def _():
            h, _, _, _ = _prenorm(x_ref[...], pv_ref)
            h_sc[...] = h.astype(BF16)

        p_ref[...] = _dot(h_sc[...], w_ref[...])

    grid_spec = pltpu.PrefetchScalarGridSpec(
        num_scalar_prefetch=1, grid=(T // tm, N_DEV),
        in_specs=[pl.BlockSpec((tm, D), lambda i, j, l: (i, 0)),
                  pl.BlockSpec((8, D), lambda i, j, l: (0, 0)),
                  pl.BlockSpec((None, None, D, IN_BLK), lambda i, j, l: (l[0], j, 0, 0))],
        out_specs=pl.BlockSpec((tm, IN_BLK), lambda i, j, l: (i, j)),
        scratch_shapes=[pltpu.VMEM((tm, D), BF16)],
    )
    return pl.pallas_call(
        body, name="mix_in_fwd", grid_spec=grid_spec,
        out_shape=jax.ShapeDtypeStruct((T, IN_COLS), F32),
        compiler_params=_cparams(("arbitrary", "arbitrary")),
    )(l, x, pv, wg)


def mix_in_bwd(l, dp, dxn, x, pv, wg, tm=512):
    T, D = x.shape
    tm = min(tm, T)
    nj, ni = N_DEV, T // tm

    def body(l_ref, dp_ref, dxn_ref, x_ref, pv_ref, w_ref, dx_ref, gw_ref, pg_ref, h_all, dh_all, acc):
        j, i = pl.program_id(0), pl.program_id(1)
        rows = pl.ds(pl.multiple_of(i * tm, tm), tm)

        @pl.when((i == 0) & (j == 0))
        def _():
            pg_ref[...] = jnp.zeros_like(pg_ref)

        @pl.when(j == 0)
        def _():
            h, _, _, _ = _prenorm(x_ref[...], pv_ref)
            h_all[rows, :] = h.astype(BF16)
            dh_all[rows, :] = jnp.zeros((tm, D), F32)

        @pl.when(i == 0)
        def _():
            acc[...] = jnp.zeros_like(acc)

        dp_t = dp_ref[...]
        dh_all[rows, :] += _dot_nt(dp_t, w_ref[...])
        acc[...] += _dot_tn(h_all[rows, :], dp_t)

        @pl.when(i == ni - 1)
        def _():
            gw_ref[...] = acc[...].astype(BF16)

        @pl.when(j == nj - 1)
        def _():
            dx_ref[...] = _prenorm_bwd(dh_all[rows, :], dxn_ref[...], x_ref[...], pv_ref, pg_ref)

    ends = lambda j, i, l: (jnp.where((j == 0) | (j == nj - 1), i, ni - 1), 0)
    last = lambda j, i, l: (jnp.where(j == nj - 1, i, 0), 0)
    grid_spec = pltpu.PrefetchScalarGridSpec(
        num_scalar_prefetch=1, grid=(nj, ni),
        in_specs=[pl.BlockSpec((tm, IN_BLK), lambda j, i, l: (i, j)),
                  pl.BlockSpec((tm, D), last), pl.BlockSpec((tm, D), ends),
                  pl.BlockSpec((8, D), lambda j, i, l: (0, 0)),
                  pl.BlockSpec((None, None, D, IN_BLK), lambda j, i, l: (l[0], j, 0, 0))],
        out_specs=[pl.BlockSpec((tm, D), last), pl.BlockSpec((None, D, IN_BLK), lambda j, i, l: (j, 0, 0)),
                   pl.BlockSpec((8, D), lambda j, i, l: (0, 0))],
        scratch_shapes=[pltpu.VMEM((T, D), BF16), pltpu.VMEM((T, D), F32), pltpu.VMEM((D, IN_BLK), F32)],
    )
    return pl.pallas_call(
        body, name="mix_in_bwd", grid_spec=grid_spec,
        out_shape=[jax.ShapeDtypeStruct((T, D), F32), jax.ShapeDtypeStruct((nj, D, IN_BLK), BF16),
                   jax.ShapeDtypeStruct((8, D), F32)],
        compiler_params=_cparams(("arbitrary", "arbitrary")),
    )(l, dp, dxn, x, pv, wg)


SEQ_CHUNK = 256
HALO = 16


def _shift_down(ext, d):
    return pltpu.roll(ext, d, 0)


def _shift_up(ext, d):
    return pltpu.roll(ext, ext.shape[0] - d, 0)


def _rows_with_lead(load, c, width):
    t0 = c * SEQ_CHUNK
    if c == 0:
        return jnp.concatenate([jnp.zeros((HALO, width), F32), load(0, SEQ_CHUNK)], axis=0)
    return load(t0 - HALO, SEQ_CHUNK + HALO)


def _rows_with_tail(load, c, n_chunks, width):
    t0 = c * SEQ_CHUNK
    if c == n_chunks - 1:
        return jnp.concatenate([load(t0, SEQ_CHUNK), jnp.zeros((HALO, width), F32)], axis=0)
    return load(t0, SEQ_CHUNK + HALO)


def conv_fwd(l, p, conv_w):
    T = p.shape[0]
    W = MIX_W
    nC = T // SEQ_CHUNK

    def body(l_ref, p_ref, w_ref, za_ref):
        w0, w1, w2 = w_ref[0:1, :], w_ref[1:2, :], w_ref[2:3, :]
        for c in range(nC):
            ext = _rows_with_lead(lambda s, n: p_ref[s:s + n, W:2 * W] * p_ref[s:s + n, 2 * W:3 * W], c, W)
            y = w2 * ext + w1 * _shift_down(ext, 1) + w0 * _shift_down(ext, 2)
            t0 = c * SEQ_CHUNK
            za_ref[t0:t0 + SEQ_CHUNK, :] = (p_ref[t0:t0 + SEQ_CHUNK, 0:W] * y[HALO:]).astype(BF16)

    grid_spec = pltpu.PrefetchScalarGridSpec(
        num_scalar_prefetch=1, grid=(1,),
        in_specs=[pl.BlockSpec((T, 3 * W), lambda i, l: (0, 0)),
                  pl.BlockSpec((None, 8, W), lambda i, l: (l[0], 0, 0))],
        out_specs=pl.BlockSpec((T, W), lambda i, l: (0, 0)),
    )
    return pl.pallas_call(
        body, name="conv_fwd", grid_spec=grid_spec,
        out_shape=jax.ShapeDtypeStruct((T, W), BF16),
        compiler_params=_cparams(("arbitrary",)),
    )(l, p, conv_w)


def conv_bwd(l, p, dza, conv_w):
    T = p.shape[0]
    W = MIX_W
    nC = T // SEQ_CHUNK

    def body(l_ref, p_ref, dza_ref, w_ref, dp_ref, dw_ref):
        w0, w1, w2 = w_ref[0:1, :], w_ref[1:2, :], w_ref[2:3, :]
        dw = [jnp.zeros((1, W), F32) for _ in range(3)]
        for c in range(nC):
            t0 = c * SEQ_CHUNK
            ext = _rows_with_lead(lambda s, n: p_ref[s:s + n, W:2 * W] * p_ref[s:s + n, 2 * W:3 * W], c, W)
            u1, u2 = _shift_down(ext, 1)[HALO:], _shift_down(ext, 2)[HALO:]
            u0 = ext[HALO:]
            y = w2 * u0 + w1 * u1 + w0 * u2
            dza_c = dza_ref[t0:t0 + SEQ_CHUNK, :]
            dy = dza_c * p_ref[t0:t0 + SEQ_CHUNK, 0:W]
            dw[0] += _colsum(dy * u2)
            dw[1] += _colsum(dy * u1)
            dw[2] += _colsum(dy * u0)
            dye = _rows_with_tail(lambda s, n: dza_ref[s:s + n, :] * p_ref[s:s + n, 0:W], c, nC, W)
            du = (w2 * dye + w1 * _shift_up(dye, 1) + w0 * _shift_up(dye, 2))[:SEQ_CHUNK]
            dp_ref[t0:t0 + SEQ_CHUNK, 0:W] = (dza_c * y).astype(BF16)
            dp_ref[t0:t0 + SEQ_CHUNK, W:2 * W] = (du * p_ref[t0:t0 + SEQ_CHUNK, 2 * W:3 * W]).astype(BF16)
            dp_ref[t0:t0 + SEQ_CHUNK, 2 * W:3 * W] = (du * p_ref[t0:t0 + SEQ_CHUNK, W:2 * W]).astype(BF16)
        dw_ref[...] = jnp.concatenate(dw + [jnp.zeros((5, W), F32)], axis=0)

    grid_spec = pltpu.PrefetchScalarGridSpec(
        num_scalar_prefetch=1, grid=(1,),
        in_specs=[pl.BlockSpec((T, 3 * W), lambda i, l: (0, 0)),
                  pl.BlockSpec((T, W), lambda i, l: (0, 0)),
                  pl.BlockSpec((None, 8, W), lambda i, l: (l[0], 0, 0))],
        out_specs=[pl.BlockSpec((T, 3 * W), lambda i, l: (0, 0)), pl.BlockSpec((8, W), lambda i, l: (0, 0))],
    )
    return pl.pallas_call(
        body, name="conv_bwd", grid_spec=grid_spec,
        out_shape=[jax.ShapeDtypeStruct((T, 3 * W), BF16), jax.ShapeDtypeStruct((8, W), F32)],
        compiler_params=_cparams(("arbitrary",)),
    )(l, p, dza, conv_w)


def _pool_consts(rows, t0):
    lane = lax.broadcasted_iota(jnp.int32, (rows, MIX_W), 1)
    t = lax.broadcasted_iota(jnp.int32, (rows, MIX_W), 0) + t0
    win = jnp.where(lane < 64, 2, jnp.where(lane < 128, 4, jnp.where(lane < 192, 8, 16)))
    inv = 1.0 / jnp.minimum(t + 1, win).astype(F32)
    return lane, inv


def _pick_window(lane, s2, s4, s8, s16):
    return jnp.where(lane < 64, s2, jnp.where(lane < 128, s4, jnp.where(lane < 192, s8, s16)))


def _pooled_chunk(u_ref, c):
    ext = _rows_with_lead(lambda s, n: u_ref[s:s + n, :], c, MIX_W)
    s2 = ext + _shift_down(ext, 1)
    s4 = s2 + _shift_down(s2, 2)
    s8 = s4 + _shift_down(s4, 4)
    s16 = s8 + _shift_down(s8, 8)
    lane, inv = _pool_consts(SEQ_CHUNK, c * SEQ_CHUNK)
    return _pick_window(lane, s2[HALO:], s4[HALO:], s8[HALO:], s16[HALO:]) * inv - ext[HALO:]


def pool_fwd(l, p, w_bd, scale):
    T = p.shape[0]
    W = MIX_W
    nC = T // SEQ_CHUNK

    def body(l_ref, u_ref, w_ref, sc_ref, z_ref):
        for c in range(nC):
            pooled = _pooled_chunk(u_ref, c)
            mixed = _dot(pooled.astype(BF16), w_ref[...])
            z_ref[c * SEQ_CHUNK:(c + 1) * SEQ_CHUNK, :] = (mixed * sc_ref[0:1, :]).astype(BF16)

    grid_spec = pltpu.PrefetchScalarGridSpec(
        num_scalar_prefetch=1, grid=(1,),
        in_specs=[pl.BlockSpec((T, W), lambda i, l: (0, 4)),
                  pl.BlockSpec((None, W, W), lambda i, l: (l[0], 0, 0)),
                  pl.BlockSpec((None, 8, W), lambda i, l: (l[0], 0, 0))],
        out_specs=pl.BlockSpec((T, W), lambda i, l: (0, 0)),
    )
    return pl.pallas_call(
        body, name="pool_fwd", grid_spec=grid_spec,
        out_shape=jax.ShapeDtypeStruct((T, W), BF16),
        compiler_params=_cparams(("arbitrary",)),
    )(l, p, w_bd, scale)


def pool_bwd(l, p, dz, w_bd, scale):
    T = p.shape[0]
    W = MIX_W
    nC = T // SEQ_CHUNK

    def body(l_ref, u_ref, dz_ref, w_ref, sc_ref, du_ref, dw_ref, dsc_ref, e_sc, dpl_sc):
        dw = jnp.zeros((W, W), F32)
        dsc = jnp.zeros((1, W), F32)
        for c in range(nC):
            t0 = c * SEQ_CHUNK
            pooled = _pooled_chunk(u_ref, c).astype(BF16)
            mixed = _dot(pooled, w_ref[...])
            dz_c = dz_ref[t0:t0 + SEQ_CHUNK, :]
            dsc += _colsum(dz_c * mixed)
            dmixed = (dz_c * sc_ref[0:1, :]).astype(BF16)
            dw += _dot_tn(pooled, dmixed)
            dpooled = _dot_nt(dmixed, w_ref[...])
            _, inv = _pool_consts(SEQ_CHUNK, t0)
            dpl_sc[t0:t0 + SEQ_CHUNK, :] = dpooled
            e_sc[t0:t0 + SEQ_CHUNK, :] = dpooled * inv
        for c in range(nC):
            t0 = c * SEQ_CHUNK
            ext = _rows_with_tail(lambda s, n: e_sc[s:s + n, :], c, nC, W)
            s2 = ext + _shift_up(ext, 1)
            s4 = s2 + _shift_up(s2, 2)
            s8 = s4 + _shift_up(s4, 4)
            s16 = s8 + _shift_up(s8, 8)
            lane, _ = _pool_consts(SEQ_CHUNK, t0)
            n = SEQ_CHUNK
            du = _pick_window(lane, s2[:n], s4[:n], s8[:n], s16[:n]) - dpl_sc[t0:t0 + SEQ_CHUNK, :]
            du_ref[t0:t0 + SEQ_CHUNK, :] = du.astype(BF16)
        dw_ref[...] = dw
        dsc_ref[...] = jnp.concatenate([dsc, jnp.zeros((7, W), F32)], axis=0)

    grid_spec = pltpu.PrefetchScalarGridSpec(
        num_scalar_prefetch=1, grid=(1,),
        in_specs=[pl.BlockSpec((T, W), lambda i, l: (0, 4)),
                  pl.BlockSpec((T, W), lambda i, l: (0, 0)),
                  pl.BlockSpec((None, W, W), lambda i, l: (l[0], 0, 0)),
                  pl.BlockSpec((None, 8, W), lambda i, l: (l[0], 0, 0))],
        out_specs=[pl.BlockSpec((T, W), lambda i, l: (0, 0)), pl.BlockSpec((W, W), lambda i, l: (0, 0)),
                   pl.BlockSpec((8, W), lambda i, l: (0, 0))],
        scratch_shapes=[pltpu.VMEM((T, W), F32), pltpu.VMEM((T, W), F32)],
    )
    return pl.pallas_call(
        body, name="pool_bwd", grid_spec=grid_spec,
        out_shape=[jax.ShapeDtypeStruct((T, W), BF16), jax.ShapeDtypeStruct((W, W), F32),
                   jax.ShapeDtypeStruct((8, W), F32)],
        compiler_params=_cparams(("arbitrary",)),
    )(l, p, dz, w_bd, scale)


def _s5_disc(lre, lim, ldt):
    lr = jnp.minimum(lre, DT_LAMBDA_RE_MAX)
    dt = jnp.exp(ldt)
    mag = jnp.exp(lr * dt)
    a_re = mag * jnp.cos(lim * dt)
    a_im = mag * jnp.sin(lim * dt)
    den = lr * lr + lim * lim
    nr = a_re - 1.0
    return a_re, a_im, (nr * lr + a_im * lim) / den, (a_im * lr - nr * lim) / den


def _bd_mask(shape, row_blk, col_blk):
    r = lax.broadcasted_iota(jnp.int32, shape, 0) >> (row_blk.bit_length() - 1)
    c = lax.broadcasted_iota(jnp.int32, shape, 1) >> (col_blk.bit_length() - 1)
    return r == c


def s5_params(lam, b_t, c_t):
    L = lam.shape[0]

    def body(lam_ref, b_ref, c_ref, a_ref, bbd_ref, cbd_ref):
        a_re, a_im, f_re, f_im = _s5_disc(lam_ref[0:1, :], lam_ref[1:2, :], lam_ref[2:3, :])
        a_ref[...] = jnp.concatenate([a_re, a_im, jnp.zeros((6, SSM_W), F32)], axis=0)
        mb = _bd_mask((MIX_W, SSM_W), SSM_GROUP, SSM_STATE)
        bbd_ref[0] = jnp.where(mb, f_re * b_ref[0] - f_im * b_ref[1], 0.0).astype(BF16)
        bbd_ref[1] = jnp.where(mb, f_re * b_ref[1] + f_im * b_ref[0], 0.0).astype(BF16)
        mc = _bd_mask((SSM_W, MIX_W), SSM_STATE, SSM_GROUP)
        cbd_ref[0] = jnp.where(mc, c_ref[0], 0.0).astype(BF16)
        cbd_ref[1] = jnp.where(mc, c_ref[1], 0.0).astype(BF16)

    return pl.pallas_call(
        body, name="s5_params", grid=(L,),
        in_specs=[pl.BlockSpec((None, 8, SSM_W), lambda l: (l, 0, 0)),
                  pl.BlockSpec((None, 2, MIX_W, SSM_W), lambda l: (l, 0, 0, 0)),
                  pl.BlockSpec((None, 2, SSM_W, MIX_W), lambda l: (l, 0, 0, 0))],
        out_specs=[pl.BlockSpec((None, 8, SSM_W), lambda l: (l, 0, 0)),
                   pl.BlockSpec((None, 2, MIX_W, SSM_W), lambda l: (l, 0, 0, 0)),
                   pl.BlockSpec((None, 2, SSM_W, MIX_W), lambda l: (l, 0, 0, 0))],
        out_shape=[jax.ShapeDtypeStruct((L, 8, SSM_W), F32),
                   jax.ShapeDtypeStruct((L, 2, MIX_W, SSM_W), BF16),
                   jax.ShapeDtypeStruct((L, 2, SSM_W, MIX_W), BF16)],
        compiler_params=_cparams(("arbitrary",)),
    )(lam, b_t, c_t)


def s5_params_bwd(lam, b_t, gb, gc, da):
    L = lam.shape[0]
    exact = functools.partial(jnp.dot, preferred_element_type=F32, precision=lax.Precision.HIGHEST)

    def fold(shape, period):
        c = lax.broadcasted_iota(jnp.int32, shape, 0) & (period - 1)
        return jnp.where(c == lax.broadcasted_iota(jnp.int32, shape, 1), 1.0, 0.0)

    def body(lam_ref, b_ref, gb_ref, gc_ref, da_ref, dlam_ref, db_ref, dc_ref, dgrp_ref):
        lre, lim, ldt = lam_ref[0:1, :], lam_ref[1:2, :], lam_ref[2:3, :]
        (a_re, a_im, f_re, f_im), vjp = jax.vjp(_s5_disc, lre, lim, ldt)
        mb = _bd_mask((MIX_W, SSM_W), SSM_GROUP, SSM_STATE)
        gbr = jnp.where(mb, gb_ref[0], 0.0)
        gbi = jnp.where(mb, gb_ref[1], 0.0)
        df_re = _colsum(gbr * b_ref[0] + gbi * b_ref[1])
        df_im = _colsum(gbi * b_ref[0] - gbr * b_ref[1])
        fold_b = fold((SSM_W, 128), SSM_STATE)
        db_ref[0] = exact(f_re * gbr + f_im * gbi, fold_b)
        db_ref[1] = exact(f_re * gbi - f_im * gbr, fold_b)
        mc = _bd_mask((SSM_W, MIX_W), SSM_STATE, SSM_GROUP)
        fold_c = fold((MIX_W, 128), SSM_GROUP)
        dc_ref[0] = exact(jnp.where(mc, gc_ref[0], 0.0), fold_c)
        dc_ref[1] = exact(jnp.where(mc, gc_ref[1], 0.0), fold_c)
        dlre, dlim, dldt = vjp((da_ref[0:1, :], da_ref[1:2, :], df_re, df_im))
        dl = jnp.concatenate([dlre, dlim, dldt, jnp.zeros((5, SSM_W), F32)], axis=0)
        dlam_ref[...] = dl
        grp = jnp.where(_bd_mask((SSM_W, 128), SSM_STATE, 1), 1.0, 0.0)
        dgrp_ref[...] = jnp.dot(dl, grp, preferred_element_type=F32, precision=lax.Precision.HIGHEST)

    vec = pl.BlockSpec((None, 8, SSM_W), lambda l: (l, 0, 0))
    bsp = pl.BlockSpec((None, 2, MIX_W, SSM_W), lambda l: (l, 0, 0, 0))
    csp = pl.BlockSpec((None, 2, SSM_W, MIX_W), lambda l: (l, 0, 0, 0))
    return pl.pallas_call(
        body, name="s5_params_bwd", grid=(L,),
        in_specs=[vec, bsp, bsp, csp, vec],
        out_specs=[vec, pl.BlockSpec((None, 2, MIX_W, 128), lambda l: (l, 0, 0, 0)),
                   pl.BlockSpec((None, 2, SSM_W, 128), lambda l: (l, 0, 0, 0)),
                   pl.BlockSpec((None, 8, 128), lambda l: (l, 0, 0))],
        out_shape=[jax.ShapeDtypeStruct((L, 8, SSM_W), F32),
                   jax.ShapeDtypeStruct((L, 2, MIX_W, 128), F32),
                   jax.ShapeDtypeStruct((L, 2, SSM_W, 128), F32),
                   jax.ShapeDtypeStruct((L, 8, 128), F32)],
        compiler_params=_cparams(("arbitrary",)),
    )(lam, b_t, gb, gc, da)


def s5_bu(l, p, b_bd, tm=512):
    T = p.shape[0]

    def body(l_ref, u_ref, b_ref, bu_ref):
        u = u_ref[...].astype(BF16)
        bu_ref[0] = _dot(u, b_ref[0])
        bu_ref[1] = _dot(u, b_ref[1])

    grid_spec = pltpu.PrefetchScalarGridSpec(
        num_scalar_prefetch=1, grid=(T // tm,),
        in_specs=[pl.BlockSpec((tm, MIX_W), lambda i, l: (i, 3)),
                  pl.BlockSpec((None, 2, MIX_W, SSM_W), lambda i, l: (l[0], 0, 0, 0))],
        out_specs=pl.BlockSpec((2, tm, SSM_W), lambda i, l: (0, i, 0)),
    )
    return pl.pallas_call(
        body, name="s5_bu", grid_spec=grid_spec,
        out_shape=jax.ShapeDtypeStruct((2, T, SSM_W), F32),
        compiler_params=_cparams(("arbitrary",)),
    )(l, p, b_bd)


def s5_scan(l, avec, xs, reverse):
    T = xs.shape[1]
    CH = SEQ_CHUNK
    nC = T // CH
    LW = 128
    n_steps = CH.bit_length() - 1

    def body(l_ref, a_ref, x_ref, s_ref):
        ar = a_ref[0:1, :]
        ai = -a_ref[1:2, :] if reverse else a_ref[1:2, :]
        pows = [(ar, ai)]
        for _ in range(n_steps - 1):
            r, i = pows[-1]
            pows.append((r * r - i * i, 2.0 * r * i))
        row = lax.broadcasted_iota(jnp.int32, (CH, LW), 0)

        def local_scan(re, im):
            for k in range(n_steps):
                d = 1 << k
                pr, pi = pows[k]
                if reverse:
                    keep = row < CH - d
                    sr, si = _shift_up(re, d), _shift_up(im, d)
                else:
                    keep = row >= d
                    sr, si = _shift_down(re, d), _shift_down(im, d)
                sr = jnp.where(keep, sr, 0.0)
                si = jnp.where(keep, si, 0.0)
                re, im = re + pr * sr - pi * si, im + pr * si + pi * sr
            return re, im

        edge = CH - 1 if reverse else 0
        pw_re, pw_im = local_scan(jnp.where(row == edge, ar, 0.0), jnp.where(row == edge, ai, 0.0))
        last = 0 if reverse else CH - 1

        def chunk(c, carry):
            cr, ci = carry
            cc = nC - 1 - c if reverse else c
            t0 = pl.multiple_of(cc * CH, CH)
            re, im = local_scan(x_ref[0, pl.ds(t0, CH), :], x_ref[1, pl.ds(t0, CH), :])
            re2 = re + pw_re * cr - pw_im * ci
            im2 = im + pw_re * ci + pw_im * cr
            s_ref[0, pl.ds(t0, CH), :] = re2
            s_ref[1, pl.ds(t0, CH), :] = im2
            return re2[last:last + 1, :], im2[last:last + 1, :]

        lax.fori_loop(0, nC, chunk, (jnp.zeros((1, LW), F32), jnp.zeros((1, LW), F32)))

    grid_spec = pltpu.PrefetchScalarGridSpec(
        num_scalar_prefetch=1, grid=(SSM_W // LW,),
        in_specs=[pl.BlockSpec((None, 8, LW), lambda g, l: (l[0], 0, g)),
                  pl.BlockSpec((2, T, LW), lambda g, l: (0, 0, g))],
        out_specs=pl.BlockSpec((2, T, LW), lambda g, l: (0, 0, g)),
    )
    return pl.pallas_call(
        body, name="s5_scan_rev" if reverse else "s5_scan_fwd", grid_spec=grid_spec,
        out_shape=jax.ShapeDtypeStruct((2, T, SSM_W), F32),
        compiler_params=_cparams(("arbitrary",)),
    )(l, avec, xs)


_GELU_C = 0.7978845608028654
_GELU_K = 0.044715


def _s5_y(u, s_ref, c_ref, d_row):
    y = _dot(s_ref[0].astype(BF16), c_ref[0]) - _dot(s_ref[1].astype(BF16), c_ref[1])
    return y + d_row * u


def s5_out(l, p, s, c_bd, ssm_d, tm=512):
    T = p.shape[0]

    def body(l_ref, u_ref, s_ref, c_ref, d_ref, yg_ref):
        y = _s5_y(u_ref[...], s_ref, c_ref, d_ref[0:1, :])
        th = jnp.tanh(_GELU_C * (y + _GELU_K * y * y * y))
        yg_ref[...] = (0.5 * y * (1.0 + th)).astype(BF16)

    grid_spec = pltpu.PrefetchScalarGridSpec(
        num_scalar_prefetch=1, grid=(T // tm,),
        in_specs=[pl.BlockSpec((tm, MIX_W), lambda i, l: (i, 3)),
                  pl.BlockSpec((2, tm, SSM_W), lambda i, l: (0, i, 0)),
                  pl.BlockSpec((None, 2, SSM_W, MIX_W), lambda i, l: (l[0], 0, 0, 0)),
                  pl.BlockSpec((None, 8, MIX_W), lambda i, l: (l[0], 0, 0))],
        out_specs=pl.BlockSpec((tm, MIX_W), lambda i, l: (i, 0)),
    )
    return pl.pallas_call(
        body, name="s5_out", grid_spec=grid_spec,
        out_shape=jax.ShapeDtypeStruct((T, MIX_W), BF16),
        compiler_params=_cparams(("arbitrary",)),
    )(l, p, s, c_bd, ssm_d)


def s5_bwd_y(l, p, s, dyg, c_bd, ssm_d, tm=512):
    T = p.shape[0]

    def body(l_ref, u_ref, s_ref, dyg_ref, c_ref, d_ref, ds_ref, du_ref, gc_ref, dd_ref):
        @pl.when(pl.program_id(0) == 0)
        def _():
            gc_ref[...] = jnp.zeros_like(gc_ref)
            dd_ref[...] = jnp.zeros_like(dd_ref)

        u = u_ref[...]
        y = _s5_y(u, s_ref, c_ref, d_ref[0:1, :])
        inner = _GELU_C * (y + _GELU_K * y * y * y)
        th = jnp.tanh(inner)
        dgelu = 0.5 * (1.0 + th) + 0.5 * y * (1.0 - th * th) * (_GELU_C * (1.0 + 3.0 * _GELU_K * y * y))
        dy = dyg_ref[...] * dgelu
        dd_ref[0:1, :] += _colsum(dy * u)
        du_ref[...] = dy * d_ref[0:1, :]
        dyb = dy.astype(BF16)
        ds_ref[0] = _dot_nt(dyb, c_ref[0])
        ds_ref[1] = -_dot_nt(dyb, c_ref[1])
        gc_ref[0] += _dot_tn(s_ref[0].astype(BF16), dyb)
        gc_ref[1] -= _dot_tn(s_ref[1].astype(BF16), dyb)

    grid_spec = pltpu.PrefetchScalarGridSpec(
        num_scalar_prefetch=1, grid=(T // tm,),
        in_specs=[pl.BlockSpec((tm, MIX_W), lambda i, l: (i, 3)),
                  pl.BlockSpec((2, tm, SSM_W), lambda i, l: (0, i, 0)),
                  pl.BlockSpec((tm, MIX_W), lambda i, l: (i, 0)),
                  pl.BlockSpec((None, 2, SSM_W, MIX_W), lambda i, l: (l[0], 0, 0, 0)),
                  pl.BlockSpec((None, 8, MIX_W), lambda i, l: (l[0], 0, 0))],
        out_specs=[pl.BlockSpec((2, tm, SSM_W), lambda i, l: (0, i, 0)),
                   pl.BlockSpec((tm, MIX_W), lambda i, l: (i, 0)),
                   pl.BlockSpec((2, SSM_W, MIX_W), lambda i, l: (0, 0, 0)),
                   pl.BlockSpec((8, MIX_W), lambda i, l: (0, 0))],
    )
    return pl.pallas_call(
        body, name="s5_bwd_y", grid_spec=grid_spec,
        out_shape=[jax.ShapeDtypeStruct((2, T, SSM_W), F32), jax.ShapeDtypeStruct((T, MIX_W), F32),
                   jax.ShapeDtypeStruct((2, SSM_W, MIX_W), F32), jax.ShapeDtypeStruct((8, MIX_W), F32)],
        compiler_params=_cparams(("arbitrary",)),
    )(l, p, s, dyg, c_bd, ssm_d)


def s5_bwd_u(l, p, lam_s, du_skip, b_bd, tm=512):
    T = p.shape[0]

    def body(l_ref, u_ref, ls_ref, dus_ref, b_ref, du_ref, gb_ref):
        @pl.when(pl.program_id(0) == 0)
        def _():
            gb_ref[...] = jnp.zeros_like(gb_ref)

        u = u_ref[...].astype(BF16)
        lr = ls_ref[0].astype(BF16)
        li = ls_ref[1].astype(BF16)
        gb_ref[0] += _dot_tn(u, lr)
        gb_ref[1] += _dot_tn(u, li)
        du_ref[...] = (dus_ref[...] + _dot_nt(lr, b_ref[0]) + _dot_nt(li, b_ref[1])).astype(BF16)

    grid_spec = pltpu.PrefetchScalarGridSpec(
        num_scalar_prefetch=1, grid=(T // tm,),
        in_specs=[pl.BlockSpec((tm, MIX_W), lambda i, l: (i, 3)),
                  pl.BlockSpec((2, tm, SSM_W), lambda i, l: (0, i, 0)),
                  pl.BlockSpec((tm, MIX_W), lambda i, l: (i, 0)),
                  pl.BlockSpec((None, 2, MIX_W, SSM_W), lambda i, l: (l[0], 0, 0, 0))],
        out_specs=[pl.BlockSpec((tm, MIX_W), lambda i, l: (i, 0)),
                   pl.BlockSpec((2, MIX_W, SSM_W), lambda i, l: (0, 0, 0))],
    )
    return pl.pallas_call(
        body, name="s5_bwd_u", grid_spec=grid_spec,
        out_shape=[jax.ShapeDtypeStruct((T, MIX_W), BF16), jax.ShapeDtypeStruct((2, MIX_W, SSM_W), F32)],
        compiler_params=_cparams(("arbitrary",)),
    )(l, p, lam_s, du_skip, b_bd)


def s5_bwd_a(s, lam_s):
    T = s.shape[1]
    nC = T // SEQ_CHUNK
    LW = 128

    def body(s_ref, ls_ref, da_ref):
        dre = jnp.zeros((1, LW), F32)
        dim = jnp.zeros((1, LW), F32)
        for c in range(nC):
            t0 = c * SEQ_CHUNK
            sr = _shift_down(_rows_with_lead(lambda a, n: s_ref[0, a:a + n, :], c, LW), 1)[HALO:]
            si = _shift_down(_rows_with_lead(lambda a, n: s_ref[1, a:a + n, :], c, LW), 1)[HALO:]
            lr = ls_ref[0, t0:t0 + SEQ_CHUNK, :]
            li = ls_ref[1, t0:t0 + SEQ_CHUNK, :]
            dre += _colsum(sr * lr + si * li)
            dim += _colsum(sr * li - si * lr)
        da_ref[...] = jnp.concatenate([dre, dim, jnp.zeros((6, LW), F32)], axis=0)

    blk = pl.BlockSpec((2, T, LW), lambda g: (0, 0, g))
    return pl.pallas_call(
        body, name="s5_bwd_a", grid=(SSM_W // LW,),
        in_specs=[blk, blk],
        out_specs=pl.BlockSpec((8, LW), lambda g: (0, g)),
        out_shape=jax.ShapeDtypeStruct((8, SSM_W), F32),
        compiler_params=_cparams(("arbitrary",)),
    )(s, lam_s)


SB_BLK = 128
SB_SCALE = SB_HEAD ** -0.5


def _split_bf16(x):
    hi = x.astype(BF16)
    return hi, (x - hi.astype(F32)).astype(BF16)


def _dot_split(x, tri):
    hi, lo = _split_bf16(x)
    return _dot(hi, tri) + _dot(lo, tri)


SB_SLABS = MIX_W // SB_BLK
SB_STACK = 2 * SB_SLABS * SB_BLK
SB_PAIR = 2 * SB_BLK


def _sb_valid(r0, c0):
    row = (lax.broadcasted_iota(jnp.int32, (SB_STACK, SB_BLK), 0) & (SB_BLK - 1)) + r0
    col = lax.broadcasted_iota(jnp.int32, (SB_STACK, SB_BLK), 1) + c0
    return col < row


def _sb_stack(ref, r0, scale):
    lane = lax.broadcasted_iota(jnp.int32, (SB_BLK, SB_BLK), 1)
    parts = []
    for s in range(SB_SLABS):
        blk = ref[pl.ds(r0, SB_BLK), s * SB_BLK:(s + 1) * SB_BLK] * scale
        parts += [jnp.where(lane < SB_HEAD, blk, 0.0), jnp.where(lane >= SB_HEAD, blk, 0.0)]
    return jnp.concatenate(parts, axis=0).astype(BF16)


def _sb_rows_nt(stack, ref, c0):
    return jnp.concatenate(
        [_dot_nt(stack[s * SB_PAIR:(s + 1) * SB_PAIR], ref[pl.ds(c0, SB_BLK), s * SB_BLK:(s + 1) * SB_BLK].astype(BF16))
         for s in range(SB_SLABS)], axis=0)


def _sb_wide(stack, s):
    return jnp.concatenate([stack[s * SB_PAIR:s * SB_PAIR + SB_BLK], stack[s * SB_PAIR + SB_BLK:(s + 1) * SB_PAIR]],
                           axis=1)


def _sb_logits(q_stack, k_ref, c0, valid):
    z = _sb_rows_nt(q_stack, k_ref, c0)
    sp = jnp.log(1.0 + jnp.exp(-jnp.abs(z)))
    ls_pos = jnp.minimum(z, 0.0) - sp
    lk = jnp.minimum(-z, 0.0) - sp
    if valid is not None:
        lk = jnp.where(valid, lk, 0.0)
    return z, ls_pos, lk


def _tri(lower):
    r = lax.broadcasted_iota(jnp.int32, (SB_BLK, SB_BLK), 0)
    c = lax.broadcasted_iota(jnp.int32, (SB_BLK, SB_BLK), 1)
    return jnp.where(r > c if lower else r < c, 1.0, 0.0).astype(BF16)


def sb_fwd(p):
    T = p.shape[0]
    W = MIX_W
    nB = T // SB_BLK

    def body(q_ref, k_ref, v_ref, o_ref, tot_ref, acc_sc):
        tri = _tri(True)

        def qblock(i, _):
            r0 = pl.multiple_of(i * SB_BLK, SB_BLK)
            q = _sb_stack(q_ref, r0, SB_SCALE)
            acc_sc[...] = jnp.zeros_like(acc_sc)

            def kblocks(c0s, run, valid):
                parts = [_sb_logits(q, k_ref, c0, valid) for c0 in c0s]
                for c0, (_, ls_pos, lk) in zip(c0s, parts):
                    a = jnp.exp(ls_pos + _dot_split(lk, tri) + run)
                    if valid is not None:
                        a = jnp.where(valid, a, 0.0)
                    a = a.astype(BF16)
                    v = _sb_stack(v_ref, c0, 1.0)
                    for s in range(SB_SLABS):
                        acc_sc[:, s * SB_BLK:(s + 1) * SB_BLK] += _dot(_sb_wide(a, s), v[s * SB_PAIR:(s + 1) * SB_PAIR])
                    run = run + jnp.sum(lk, axis=1, keepdims=True)
                return run

            def key_block(jj):
                return pl.multiple_of((i - jj) * SB_BLK, SB_BLK)

            run = kblocks([r0], jnp.zeros((SB_STACK, 1), F32), _sb_valid(0, 0))
            odd = i & 1
            run = lax.cond(odd == 1, lambda r: kblocks([key_block(1)], r, None), lambda r: r, run)
            total = lax.fori_loop(
                0, i >> 1, lambda t, r: kblocks([key_block(1 + odd + 2 * t), key_block(2 + odd + 2 * t)], r, None), run)
            o_ref[pl.ds(r0, SB_BLK), :] = acc_sc[...].astype(BF16)
            tot_ref[pl.ds(pl.multiple_of(i * SB_STACK, SB_STACK), SB_STACK), :] = jnp.broadcast_to(total, (SB_STACK, SB_BLK))
            return 0

        lax.fori_loop(0, nB, qblock, 0)

    return pl.pallas_call(
        body, name="sb_fwd", grid=(1,),
        in_specs=[pl.BlockSpec((T, W), lambda i: (0, 5)), pl.BlockSpec((T, W), lambda i: (0, 6)),
                  pl.BlockSpec((T, W), lambda i: (0, 7))],
        out_specs=[pl.BlockSpec((T, W), lambda i: (0, 0)), pl.BlockSpec((nB * SB_STACK, SB_BLK), lambda i: (0, 0))],
        out_shape=[jax.ShapeDtypeStruct((T, W), BF16), jax.ShapeDtypeStruct((nB * SB_STACK, SB_BLK), F32)],
        scratch_shapes=[pltpu.VMEM((SB_BLK, W), F32)],
        compiler_params=_cparams(("arbitrary",)),
    )(p, p, p)


def sb_bwd(p, do, tot):
    T = p.shape[0]
    W = MIX_W
    nB = T // SB_BLK

    def body(q_ref, k_ref, v_ref, do_ref, tot_ref, dqkv_ref, dq_sc, dk_sc, dv_sc):
        tri_gt = _tri(True)
        tri_lt = _tri(False)
        dq_sc[...] = jnp.zeros_like(dq_sc)
        dk_sc[...] = jnp.zeros_like(dk_sc)
        dv_sc[...] = jnp.zeros_like(dv_sc)
        zcol = jnp.zeros((SB_STACK, 1), F32)

        def qblock(i, _):
            r0 = pl.multiple_of(i * SB_BLK, SB_BLK)
            q = _sb_stack(q_ref, r0, SB_SCALE)
            dob = _sb_stack(do_ref, r0, 1.0)

            total = tot_ref[pl.ds(pl.multiple_of(i * SB_STACK, SB_STACK), SB_STACK), 0:1]

            def kblocks(c0s, carry, valid):
                pre, seen = carry
                parts = [_sb_logits(q, k_ref, c0, valid) for c0 in c0s]
                for c0, (z, ls_pos, lk) in zip(c0s, parts):
                    seen = seen + jnp.sum(lk, axis=1, keepdims=True)
                    a = jnp.exp(ls_pos + _dot_split(lk, tri_gt) + (total - seen))
                    if valid is not None:
                        a = jnp.where(valid, a, 0.0)
                    dlw = _sb_rows_nt(dob, v_ref, c0) * a
                    g = pre + _dot_split(dlw, tri_lt)
                    sig = _sigmoid(z)
                    dz = dlw * (1.0 - sig) - g * sig
                    if valid is not None:
                        dz = jnp.where(valid, dz, 0.0)
                    dz = dz.astype(BF16)
                    ab = a.astype(BF16)
                    km = _sb_stack(k_ref, c0, 1.0)
                    for s in range(SB_SLABS):
                        pair = slice(s * SB_PAIR, (s + 1) * SB_PAIR)
                        ls = slice(s * SB_BLK, (s + 1) * SB_BLK)
                        dk_sc[pl.ds(c0, SB_BLK), ls] += _dot_tn(dz[pair], q[pair])
                        dv_sc[pl.ds(c0, SB_BLK), ls] += _dot_tn(ab[pair], dob[pair])
                        dq_sc[pl.ds(r0, SB_BLK), ls] += _dot(_sb_wide(dz, s), km[pair])
                    pre = pre + jnp.sum(dlw, axis=1, keepdims=True)
                return pre, seen

            def key_block(j):
                return pl.multiple_of(j * SB_BLK, SB_BLK)

            carry = lax.fori_loop(
                0, i >> 1, lambda t, c: kblocks([key_block(2 * t), key_block(2 * t + 1)], c, None), (zcol, zcol))
            carry = lax.cond((i & 1) == 1, lambda c: kblocks([key_block(i - 1)], c, None), lambda c: c, carry)
            kblocks([r0], carry, _sb_valid(0, 0))
            return 0

        lax.fori_loop(0, nB, qblock, 0)
        dqkv_ref[:, 0:W] = (dq_sc[...] * SB_SCALE).astype(BF16)
        dqkv_ref[:, W:2 * W] = dk_sc[...].astype(BF16)
        dqkv_ref[:, 2 * W:3 * W] = dv_sc[...].astype(BF16)

    return pl.pallas_call(
        body, name="sb_bwd", grid=(1,),
        in_specs=[pl.BlockSpec((T, W), lambda i: (0, 5)), pl.BlockSpec((T, W), lambda i: (0, 6)),
                  pl.BlockSpec((T, W), lambda i: (0, 7)), pl.BlockSpec((T, W), lambda i: (0, 0)),
                  pl.BlockSpec((nB * SB_STACK, SB_BLK), lambda i: (0, 0))],
        out_specs=pl.BlockSpec((T, 3 * W), lambda i: (0, 0)),
        out_shape=jax.ShapeDtypeStruct((T, 3 * W), BF16),
        scratch_shapes=[pltpu.VMEM((T, W), F32), pltpu.VMEM((T, W), F32), pltpu.VMEM((T, W), F32)],
        compiler_params=_cparams(("arbitrary",)),
    )(p, p, p, do, tot)


def _dot_cols(a, w_ref):
    return jnp.concatenate([_dot(a, w_ref[j]) for j in range(N_DEV)], axis=1)


def _dot_cols_nt(dy, w_ref):
    n = w_ref.shape[2]
    out = _dot_nt(dy[:, 0:n], w_ref[0])
    for j in range(1, N_DEV):
        out += _dot_nt(dy[:, j * n:(j + 1) * n], w_ref[j])
    return out


def _acc_cols_tn(acc_ref, a, dy):
    n = acc_ref.shape[2]
    for j in range(N_DEV):
        acc_ref[j] += _dot_tn(a, dy[:, j * n:(j + 1) * n])


def _merge_branches(za_ref, yg_ref, z_ref, o_ref, gate_refs, wc_ref, wglu_ref, wp_ref, ws_ref):
    D = D_MODEL
    glu = _dot_cols(yg_ref[...], wglu_ref)
    glu_a, sg = glu[:, :D], _sigmoid(glu[:, D:])
    ys = [_dot_cols(za_ref[...], wc_ref), glu_a * sg, _dot_cols(z_ref[...], wp_ref), _dot_cols(o_ref[...], ws_ref)]
    gs = [_sigmoid(g[...]) for g in gate_refs]
    merged = gs[0] * ys[0] + gs[1] * ys[1] + gs[2] * ys[2] + gs[3] * ys[3]
    return ys, gs, glu_a, sg, merged


def _merge_specs(tm, D):
    W = MIX_W
    br = pl.BlockSpec((tm, W), lambda i, l: (i, 0))
    gates = [pl.BlockSpec((tm, D), functools.partial(lambda i, l, b: (i, 2 + b), b=b)) for b in range(4)]
    wsm = pl.BlockSpec((None, N_DEV, W, D // N_DEV), lambda i, l: (l[0], 0, 0, 0))
    weights = [wsm, pl.BlockSpec((None, N_DEV, W, 2 * D // N_DEV), lambda i, l: (l[0], 0, 0, 0)), wsm, wsm,
               pl.BlockSpec((None, D, D), lambda i, l: (l[0], 0, 0))]
    return [br] * 4 + gates, weights


def merge_fwd(l, p, za, yg, z, o, x, pv, wc, wglu, wp, ws, wo, tm=512):
    T, D = x.shape
    tm = min(tm, T)

    def body(l_ref, za_ref, yg_ref, z_ref, o_ref, g0, g1, g2, g3, x_ref, pv_ref,
             wc_ref, wglu_ref, wp_ref, ws_ref, wo_ref, xn_ref, m_ref):
        _, _, _, _, merged = _merge_branches(za_ref, yg_ref, z_ref, o_ref, (g0, g1, g2, g3),
                                             wc_ref, wglu_ref, wp_ref, ws_ref)
        m = _dot(merged.astype(BF16), wo_ref[...])
        m_ref[...] = m
        xn_ref[...] = _postnorm_res(x_ref[...], m, pv_ref, 1.0)

    acts, weights = _merge_specs(tm, D)
    tile = pl.BlockSpec((tm, D), lambda i, l: (i, 0))
    grid_spec = pltpu.PrefetchScalarGridSpec(
        num_scalar_prefetch=1, grid=(T // tm,),
        in_specs=acts + [tile, pl.BlockSpec((8, D), lambda i, l: (0, 0))] + weights,
        out_specs=[tile, tile],
    )
    return pl.pallas_call(
        body, name="merge_fwd", grid_spec=grid_spec,
        out_shape=[jax.ShapeDtypeStruct((T, D), F32), jax.ShapeDtypeStruct((T, D), F32)],
        compiler_params=_cparams(("arbitrary",)),
    )(l, za, yg, z, o, p, p, p, p, x, pv, wc, wglu, wp, ws, wo)


def merge_bwd(l, p, za, yg, z, o, m, dxn, pv, wc, wglu, wp, ws, wo, tm=256):
    T, D = m.shape
    W = MIX_W
    tm = min(tm, T)
    ni = T // tm

    def body(l_ref, za_ref, yg_ref, z_ref, o_ref, g0, g1, g2, g3, m_ref, dxn_ref, pv_ref,
             wc_ref, wglu_ref, wp_ref, ws_ref, wo_ref,
             dza_ref, dyg_ref, dz_ref, do_ref, dg_ref, pg_ref, gwc_ref, gwglu_ref, gwp_ref, gws_ref, gwo_ref,
             awc, awglu, awp, aws, awo):
        i = pl.program_id(0)

        @pl.when(i == 0)
        def _():
            pg_ref[...] = jnp.zeros_like(pg_ref)
            for a in (awc, awglu, awp, aws, awo):
                a[...] = jnp.zeros_like(a)

        ys, gs, glu_a, sg, merged = _merge_branches(za_ref, yg_ref, z_ref, o_ref, (g0, g1, g2, g3),
                                                    wc_ref, wglu_ref, wp_ref, ws_ref)
        dm = _postnorm_bwd(dxn_ref[...], m_ref[...], pv_ref, pg_ref, 1.0).astype(BF16)
        awo[...] += _dot_tn(merged.astype(BF16), dm)
        dmerged = _dot_nt(dm, wo_ref[...])
        for b in range(4):
            dg_ref[:, b * D:(b + 1) * D] = (dmerged * ys[b] * gs[b] * (1.0 - gs[b])).astype(BF16)
        dya = (dmerged * gs[0]).astype(BF16)
        _acc_cols_tn(awc, za_ref[...], dya)
        dza_ref[...] = _dot_cols_nt(dya, wc_ref)
        dyc = (dmerged * gs[2]).astype(BF16)
        _acc_cols_tn(awp, z_ref[...], dyc)
        dz_ref[...] = _dot_cols_nt(dyc, wp_ref)
        dyd = (dmerged * gs[3]).astype(BF16)
        _acc_cols_tn(aws, o_ref[...], dyd)
        do_ref[...] = _dot_cols_nt(dyd, ws_ref)
        dyb = dmerged * gs[1]
        dglu = jnp.concatenate([dyb * sg, dyb * glu_a * sg * (1.0 - sg)], axis=1).astype(BF16)
        _acc_cols_tn(awglu, yg_ref[...], dglu)
        dyg_ref[...] = _dot_cols_nt(dglu, wglu_ref)

        @pl.when(i == ni - 1)
        def _():
            gwc_ref[...] = awc[...].astype(BF16)
            gwglu_ref[...] = awglu[...].astype(BF16)
            gwp_ref[...] = awp[...].astype(BF16)
            gws_ref[...] = aws[...].astype(BF16)
            gwo_ref[...] = awo[...].astype(BF16)

    acts, weights = _merge_specs(tm, D)
    tile = pl.BlockSpec((tm, D), lambda i, l: (i, 0))
    br = pl.BlockSpec((tm, W), lambda i, l: (i, 0))
    full = lambda *s: pl.BlockSpec(s, lambda i, l: (0,) * len(s))
    sm, glu_s = (N_DEV, W, D // N_DEV), (N_DEV, W, 2 * D // N_DEV)
    grid_spec = pltpu.PrefetchScalarGridSpec(
        num_scalar_prefetch=1, grid=(ni,),
        in_specs=acts + [tile, tile, pl.BlockSpec((8, D), lambda i, l: (0, 0))] + weights,
        out_specs=[br, br, br, br, pl.BlockSpec((tm, 4 * D), lambda i, l: (i, 0)), full(8, D),
                   full(*sm), full(*glu_s), full(*sm), full(*sm), full(D, D)],
        scratch_shapes=[pltpu.VMEM(sm, F32), pltpu.VMEM(glu_s, F32), pltpu.VMEM(sm, F32),
                        pltpu.VMEM(sm, F32), pltpu.VMEM((D, D), F32)],
    )
    f32br = jax.ShapeDtypeStruct((T, W), F32)
    return pl.pallas_call(
        body, name="merge_bwd", grid_spec=grid_spec,
        out_shape=[f32br, f32br, f32br, f32br, jax.ShapeDtypeStruct((T, 4 * D), BF16),
                   jax.ShapeDtypeStruct((8, D), F32),
                   jax.ShapeDtypeStruct(sm, BF16), jax.ShapeDtypeStruct(glu_s, BF16),
                   jax.ShapeDtypeStruct(sm, BF16), jax.ShapeDtypeStruct(sm, BF16),
                   jax.ShapeDtypeStruct((D, D), BF16)],
        compiler_params=_cparams(("arbitrary",)),
    )(l, za, yg, z, o, p, p, p, p, m, dxn, pv, wc, wglu, wp, ws, wo)


def dp_assemble(d_conv, d_ssm, d_pool, d_qkv, d_gates, tm=512):
    T = d_conv.shape[0]
    W = MIX_W

    def body(c_ref, s_ref, p_ref, q_ref, g_ref, dp_ref):
        dp_ref[:, 0:3 * W] = c_ref[...]
        dp_ref[:, 3 * W:4 * W] = s_ref[...]
        dp_ref[:, 4 * W:5 * W] = p_ref[...]
        dp_ref[:, 5 * W:8 * W] = q_ref[...]
        dp_ref[:, GATE_OFF:] = g_ref[...]

    row = lambda w: pl.BlockSpec((tm, w), lambda i: (i, 0))
    return pl.pallas_call(
        body, name="dp_assemble", grid=(T // tm,),
        in_specs=[row(3 * W), row(W), row(W), row(3 * W), row(4 * D_MODEL)],
        out_specs=row(IN_COLS),
        out_shape=jax.ShapeDtypeStruct((T, IN_COLS), BF16),
        compiler_params=_cparams(("arbitrary",)),
    )(d_conv, d_ssm, d_pool, d_qkv, d_gates)


def loss_head(y, target, tm=512):
    T, D = y.shape

    def body(y_ref, t_ref, dy_ref, loss_ref):
        @pl.when(pl.program_id(0) == 0)
        def _():
            loss_ref[...] = jnp.zeros_like(loss_ref)

        err = y_ref[...] - t_ref[...]
        dy_ref[...] = err * (1.0 / D)
        loss_ref[...] += jnp.sum(err * err) * (0.5 / D)

    tile = pl.BlockSpec((tm, D), lambda i: (i, 0))
    return pl.pallas_call(
        body, name="loss_head", grid=(T // tm,),
        in_specs=[tile, tile],
        out_specs=[tile, pl.BlockSpec((8, 128), lambda i: (0, 0))],
        out_shape=[jax.ShapeDtypeStruct((T, D), F32), jax.ShapeDtypeStruct((8, 128), F32)],
        compiler_params=_cparams(("arbitrary",)),
    )(y, target)


def cast_layer(ld, items):
    def body(ld_ref, *refs):
        n = len(refs) // 2
        for src, dst in zip(refs[:n], refs[n:]):
            dst[...] = src[...].astype(BF16)

    def shard(w, k):
        return w.shape[1:] if k is None else w.shape[2:]

    def in_spec(w, k):
        sh = shard(w, k)
        if k is None:
            return pl.BlockSpec((None,) + sh, lambda i, ld, n=len(sh): (ld[0],) + (0,) * n)
        return pl.BlockSpec((None, None) + sh, lambda i, ld, n=len(sh), k=k: (ld[0], k) + (0,) * n)

    def out_spec(w, k):
        sh = shard(w, k)
        return pl.BlockSpec((None, None) + sh, lambda i, ld, n=len(sh): (0, ld[1]) + (0,) * n)

    grid_spec = pltpu.PrefetchScalarGridSpec(
        num_scalar_prefetch=1, grid=(1,),
        in_specs=[in_spec(w, k) for w, k in items], out_specs=[out_spec(w, k) for w, k in items])
    return pl.pallas_call(
        body, name="cast_layer", grid_spec=grid_spec,
        out_shape=[jax.ShapeDtypeStruct((1, N_DEV) + shard(w, k), BF16) for w, k in items],
        compiler_params=_cparams(("arbitrary",)),
    )(ld, *[w for w, _ in items])


def place_own(dev, a):
    def body(dev_ref, a_ref, o_ref):
        o_ref[...] = a_ref[...]

    grid_spec = pltpu.PrefetchScalarGridSpec(
        num_scalar_prefetch=1, grid=(1,),
        in_specs=[pl.BlockSpec(a.shape, lambda i, dev: (0, 0))],
        out_specs=pl.BlockSpec((None,) + a.shape, lambda i, dev: (dev[0], 0, 0)))
    return pl.pallas_call(
        body, name="place_own", grid_spec=grid_spec,
        out_shape=jax.ShapeDtypeStruct((N_DEV,) + a.shape, a.dtype),
        compiler_params=_cparams(("arbitrary",)),
    )(dev, a)


def _silu(x):
    return x * _sigmoid(x)


def ada_fwd(c_all, w_ada, b_cols):
    L, D, n = w_ada.shape

    def body(c_ref, w_ref, b_ref, o_ref):
        c_act = _silu(c_ref[...]).astype(BF16)
        o_ref[...] = _dot(c_act, w_ref[...].astype(BF16)) + b_ref[...]

    return pl.pallas_call(
        body, name="ada_fwd", grid=(L,),
        in_specs=[pl.BlockSpec((N_DEV, D), lambda l: (0, 0)), pl.BlockSpec((None, D, n), lambda l: (l, 0, 0)),
                  pl.BlockSpec((None, 1, n), lambda l: (l, 0, 0))],
        out_specs=pl.BlockSpec((None, N_DEV, n), lambda l: (l, 0, 0)),
        out_shape=jax.ShapeDtypeStruct((L, N_DEV, n), F32),
        compiler_params=_cparams(("arbitrary",)),
    )(c_all, w_ada, b_cols)


def _adamw(w, g, m, v):
    m = ADAM_B1 * m + (1.0 - ADAM_B1) * g
    v = ADAM_B2 * v + (1.0 - ADAM_B2) * (g * g)
    m_hat = m / (1.0 - ADAM_B1 ** ADAM_STEP)
    v_hat = v / (1.0 - ADAM_B2 ** ADAM_STEP)
    delta = -ADAM_LR * (m_hat / (jnp.sqrt(v_hat) + ADAM_EPS) + ADAM_WD * w)
    return delta, m, v


def ada_update(c_all, dada_cols, w, m, v, rb=256):
    L, D, n = w.shape

    def body(c_ref, d_ref, w_ref, m_ref, v_ref, g_ref, dl_ref, nm_ref, nv_ref):
        c_act = _silu(c_ref[...]).astype(BF16)
        g = _dot_tn(c_act, d_ref[...].astype(BF16))
        g_ref[...] = g
        dl_ref[...], nm_ref[...], nv_ref[...] = _adamw(w_ref[...], g, m_ref[...], v_ref[...])

    blk = pl.BlockSpec((None, rb, n), lambda l, i: (l, i, 0))
    out = jax.ShapeDtypeStruct((L, D, n), F32)
    return pl.pallas_call(
        body, name="ada_update", grid=(L, D // rb),
        in_specs=[pl.BlockSpec((N_DEV, rb), lambda l, i: (0, i)),
                  pl.BlockSpec((None, N_DEV, n), lambda l, i: (l, 0, 0)), blk, blk, blk],
        out_specs=[blk, blk, blk, blk], out_shape=[out, out, out, out],
        compiler_params=_cparams(("arbitrary", "arbitrary")),
    )(c_all, dada_cols, w, m, v)


SUM_UPDATE_RECV_BYTES = 12 * 1024 * 1024


def sum_update(dev, first, recvs, owns, w, m, v, prev=None, after=None):
    n_slots, R, C = w.shape
    S = len(recvs)
    assert len(owns) == S and first + S <= n_slots
    rb_max = SUM_UPDATE_RECV_BYTES // (S * N_DEV * C * 2)
    rb = max(r for r in range(8, R + 1, 8) if R % r == 0 and (r <= rb_max or r == 8))
    last = R // rb - 1
    n_prev = 0 if prev is None else 4
    extra = list(prev or ()) + ([] if after is None else [after])

    def body(dev_ref, *refs):
        r_refs, o_refs = refs[:S], refs[S:2 * S]
        w_ref, m_ref, v_ref = refs[2 * S:2 * S + 3]
        g_ref, dl_ref, nm_ref, nv_ref = refs[2 * S + 3 + len(extra):]
        me = dev_ref[0]
        for s in range(S):
            @pl.when(pl.program_id(0) == s)
            def _(s=s):
                g = jnp.zeros((rb, C), F32)
                for d in range(N_DEV):
                    g += jnp.where(me == d, o_refs[s][...], r_refs[s][d]).astype(F32)
                g_ref[...] = g
                dl_ref[...], nm_ref[...], nv_ref[...] = _adamw(w_ref[...], g, m_ref[...], v_ref[...])

    def row(sl, i, s):
        return jnp.where(sl == s, i, jnp.where(sl < s, 0, last))

    def rspec(s):
        return pl.BlockSpec((N_DEV, rb, C), lambda sl, i, dev: (0, row(sl, i, s), 0))

    def ospec(s):
        return pl.BlockSpec((None, rb, C), lambda sl, i, dev: (dev[0], row(sl, i, s), 0))

    blk = pl.BlockSpec((None, rb, C), lambda sl, i, dev: (first + sl, i, 0))
    out = jax.ShapeDtypeStruct((n_slots, R, C), F32)
    grid_spec = pltpu.PrefetchScalarGridSpec(
        num_scalar_prefetch=1, grid=(S, R // rb),
        in_specs=[rspec(s) for s in range(S)] + [ospec(s) for s in range(S)] + [blk, blk, blk] + [ANY] * len(extra),
        out_specs=[blk, blk, blk, blk],
    )
    n_in = 1 + 2 * S + 3
    return pl.pallas_call(
        body, name="sum_update", grid_spec=grid_spec, out_shape=[out, out, out, out],
        input_output_aliases={n_in + i: i for i in range(n_prev)},
        compiler_params=_cparams(("arbitrary", "arbitrary")),
    )(dev, *recvs, *owns, w, m, v, *extra)


def small_sum(gathered):
    _, R, C = gathered.shape

    def body(g_ref, o_ref):
        acc = g_ref[0]
        for d in range(1, N_DEV):
            acc += g_ref[d]
        o_ref[...] = acc

    return pl.pallas_call(
        body, name="small_sum", grid=(1,),
        in_specs=[pl.BlockSpec((N_DEV, R, C), lambda i: (0, 0, 0))],
        out_specs=pl.BlockSpec((R, C), lambda i: (0, 0)),
        out_shape=jax.ShapeDtypeStruct((R, C), F32),
        compiler_params=_cparams(("arbitrary",)),
    )(gathered)


def small_update(w, g, m, v):
    def body(w_ref, g_ref, m_ref, v_ref, dl_ref, nm_ref, nv_ref):
        dl_ref[...], nm_ref[...], nv_ref[...] = _adamw(w_ref[...], g_ref[...], m_ref[...], v_ref[...])

    blk = pl.BlockSpec(w.shape, lambda i: (0, 0))
    out = jax.ShapeDtypeStruct(w.shape, F32)
    return pl.pallas_call(
        body, name="small_update", grid=(1,),
        in_specs=[blk] * 4, out_specs=[blk] * 3, out_shape=[out] * 3,
        compiler_params=_cparams(("arbitrary",)),
    )(w, g, m, v)


MESH = pl.DeviceIdType.MESH
ANY = pl.BlockSpec(memory_space=pl.ANY)


def _coords():
    return lax.axis_index("x"), lax.axis_index("y"), lax.axis_index("c")


def _dev_index(x, y, c):
    return 4 * x + 2 * y + c


def _at_dev(ref, p, dev):
    return ref.at[(slice(None),) * p + (dev,)]


def all_gather(arrays, ps):
    n = len(arrays)

    def body(*refs):
        ins, outs = refs[:n], refs[n:2 * n]
        send_sems, recv_sems, local_sems = refs[2 * n:]
        x, y, c = _coords()
        me, sibling = (x, y, c), (x, y, 1 - c)
        chips = [(1 - x, y), (x, 1 - y), (1 - x, 1 - y)]

        def copy(a, k, block, to, src=None):
            dst = _at_dev(outs[a], ps[a], _dev_index(*block))
            return pltpu.make_async_remote_copy(
                src_ref=dst if src is None else src, dst_ref=dst,
                send_sem=send_sems.at[a, k], recv_sem=recv_sems.at[a, k], device_id=to, device_id_type=MESH)

        mine = [pltpu.make_async_copy(ins[a], _at_dev(outs[a], ps[a], _dev_index(*me)), local_sems.at[a])
                for a in range(n)]
        for cp in mine:
            cp.start()
        first = []
        for a in range(n):
            first.append(copy(a, 0, me, sibling, src=ins[a]))
            first += [copy(a, 1 + j, me, (*chip, c), src=ins[a]) for j, chip in enumerate(chips)]
        for cp in first:
            cp.start()
        passed = []
        for j, chip in enumerate(chips):
            for a in range(n):
                copy(a, 1 + j, (*chip, c), me).wait_recv()
                fwd = copy(a, 4 + j, (*chip, c), sibling)
                fwd.start()
                passed.append(fwd)
        for a in range(n):
            copy(a, 0, sibling, me).wait_recv()
            for j, chip in enumerate(chips):
                copy(a, 4 + j, (*chip, 1 - c), me).wait_recv()
        for cp in first + passed:
            cp.wait_send()
        for cp in mine:
            cp.wait()

    out_shape = [jax.ShapeDtypeStruct(a.shape[:p] + (N_DEV,) + a.shape[p:], a.dtype) for a, p in zip(arrays, ps)]
    return pl.pallas_call(
        body, name="all_gather", in_specs=[ANY] * n, out_specs=[ANY] * n, out_shape=out_shape,
        scratch_shapes=[pltpu.SemaphoreType.DMA((n, 7)), pltpu.SemaphoreType.DMA((n, 7)),
                        pltpu.SemaphoreType.DMA((n,))],
        compiler_params=pltpu.CompilerParams(has_side_effects=True),
    )(*arrays)


HBM = pl.BlockSpec(memory_space=pltpu.HBM)
SEM = pl.BlockSpec(memory_space=pltpu.SEMAPHORE)
EFFECT = pltpu.SideEffectType.DATAFLOW_SIDE_EFFECTING


def _peers(x, y, c):
    out = []
    for k in range(1, N_DEV):
        out.append((1 - x if k & 4 else x, 1 - y if k & 2 else y, 1 - c if k & 1 else c))
    return out


def _exchange_plan(n):
    def plan(refs, x, y, c):
        blocks, lands = refs[:n], refs[n:2 * n]
        me = _dev_index(x, y, c)
        moves = []
        for peer in _peers(x, y, c):
            q = _dev_index(*peer)
            moves += [(blocks[a].at[q], lands[a].at[me], peer, lands[a].at[q]) for a in range(n)]
        return moves
    return plan


def _gather_plan(ps, second):
    def plan(refs, x, y, c):
        me, sibling = (x, y, c), (x, y, 1 - c)
        chips = [(1 - x, y), (x, 1 - y), (1 - x, 1 - y)]
        if second:
            trips = [((*ch, c), sibling, (*ch, 1 - c)) for ch in chips]
        else:
            trips = [(me, sibling, sibling)] + [(me, (*ch, c), (*ch, c)) for ch in chips]
        moves = []
        for sent, to, arriving in trips:
            for ref, p in zip(refs, ps):
                blk = _at_dev(ref, p, _dev_index(*sent))
                moves.append((blk, blk, to, _at_dev(ref, p, _dev_index(*arriving))))
        return moves
    return plan


def copies_start(name, plan, n_moves, arrays, carry):
    n = len(arrays)

    def body(*refs):
        sems = refs[n + 1:n + 1 + 2 * n_moves]
        moves = plan(refs[:n], *_coords())
        assert len(moves) == n_moves
        for i, (src, dst, to, _) in enumerate(moves):
            pltpu.make_async_remote_copy(src_ref=src, dst_ref=dst, send_sem=sems[i], recv_sem=sems[n_moves + i],
                                         device_id=to, device_id_type=MESH).start()

    operands = [pltpu.with_memory_space_constraint(a, pltpu.HBM) for a in list(arrays) + [carry]]
    outs = pl.pallas_call(
        body, name=name,
        out_shape=[pltpu.SemaphoreType.DMA(())] * (2 * n_moves) + [pltpu.HBM(a.shape, a.dtype) for a in operands],
        in_specs=[HBM] * (n + 1), out_specs=[SEM] * (2 * n_moves) + [HBM] * (n + 1),
        input_output_aliases={i: 2 * n_moves + i for i in range(n + 1)},
        compiler_params=pltpu.CompilerParams(has_side_effects=EFFECT),
    )(*operands)
    return outs[:n_moves], outs[n_moves:2 * n_moves], outs[2 * n_moves:-1], outs[-1]


def copies_wait(name, plan, send_sems, recv_sems, arrays, after):
    n, n_moves = len(arrays), len(send_sems)

    def body(*refs):
        sems = refs[n:n + 2 * n_moves]
        for i, (src, _, to, arriving) in enumerate(plan(refs[:n], *_coords())):
            cp = pltpu.make_async_remote_copy(src_ref=src, dst_ref=arriving, send_sem=sems[i],
                                              recv_sem=sems[n_moves + i], device_id=to, device_id_type=MESH)
            cp.wait_send()
            cp.wait_recv()

    return pl.pallas_call(
        body, name=name,
        out_shape=[pltpu.HBM(a.shape, a.dtype) for a in arrays],
        in_specs=[HBM] * n + [SEM] * (2 * n_moves) + [ANY], out_specs=[HBM] * n,
        input_output_aliases={i: i for i in range(n)},
        compiler_params=pltpu.CompilerParams(has_side_effects=EFFECT),
    )(*arrays, *send_sems, *recv_sems, after)


WEIGHT_NAMES = ("w_ada", "b_ada", "g_pre", "g_post", "w_ff_in", "w_ff_out", "w_in", "conv_w", "w_conv_out",
                "lam_re", "lam_im", "log_dt", "ssm_b_re", "ssm_b_im", "ssm_c_re", "ssm_c_im", "ssm_d", "w_glu",
                "w_pool", "pool_scale", "w_pool_out", "w_sb_out", "w_out")
BIG_NAMES = ("w_ff_in", "w_ff_out", "w_in", "w_conv_out", "w_glu", "w_pool_out", "w_sb_out", "w_out")
SMALL_NAMES = ("b_ada", "g_pre", "g_post", "conv_w", "lam_re", "lam_im", "log_dt", "ssm_b_re", "ssm_b_im",
               "ssm_c_re", "ssm_c_im", "ssm_d", "w_pool", "pool_scale")
PACK_LANES = 128
PACK_ROWS = 8


def _pack(arrays):
    flat = jnp.concatenate([a.reshape(-1) for a in arrays])
    unit = PACK_LANES * PACK_ROWS
    flat = jnp.pad(flat, (0, -flat.shape[0] % unit))
    return flat.reshape(-1, PACK_LANES)


def _unpack(pack, shapes):
    flat = pack.reshape(-1)
    out, off = [], 0
    for s in shapes:
        n = 1
        for d in s:
            n *= d
        out.append(flat[off:off + n].reshape(s))
        off += n
    return out


def _pad_rows(a, rows=8):
    return jnp.pad(a, ((0, 0), (0, rows - a.shape[1]), (0, 0)))


def _tile_b(b):
    L = b.shape[0]
    return jnp.tile(b.transpose(0, 3, 1, 2).reshape(L, SSM_GROUP, SSM_W), (1, SSM_GROUPS, 1))


def _tile_c(c):
    L = c.shape[0]
    return jnp.tile(c.transpose(0, 3, 1, 2).reshape(L, SSM_STATE, MIX_W), (1, SSM_GROUPS, 1))


def _step(x, c, target, W, M, V):
    T, D = x.shape[1], x.shape[2]
    L = W["w_ada"].shape[0]
    x = x[0]
    target = target[0]
    ax, ay, ac = _coords()
    dev = _dev_index(ax, ay, ac)
    n_ada = W["w_ada"].shape[2]

    dev_s = jnp.reshape(dev, (1,)).astype(jnp.int32)
    items = ([(W["w_ff_in"], 0), (W["w_ff_in"], 1), (W["w_ff_out"], 0), (W["w_ff_out"], 1)]
             + [(W[k], None) for k in BIG_NAMES[2:]])
    bufs = [list(cast_layer(jnp.concatenate([jnp.array([l], jnp.int32), dev_s]), items)) for l in range(L)]
    ffn1_w, mixer_w, ffn2_w = (0, 2), (4, 5, 6, 7, 8, 9), (1, 3)
    all_w = tuple(range(len(items)))

    def gather_start(tag, second, l, idx, carry):
        plan = _gather_plan((1,) * len(idx), second)
        n_moves = (3 if second else 4) * len(idx)
        s_sem, r_sem, arrs, carry = copies_start(f"gather_{'b' if second else 'a'}_start_{tag}", plan, n_moves,
                                                 [bufs[l][i] for i in idx], carry)
        for i, a in zip(idx, arrs):
            bufs[l][i] = a
        return (plan, s_sem, r_sem), carry

    def gather_wait(tag, second, l, idx, flight, after):
        arrs = copies_wait(f"gather_{'b' if second else 'a'}_wait_{tag}", *flight, [bufs[l][i] for i in idx], after)
        for i, a in zip(idx, arrs):
            bufs[l][i] = a

    def gather_finish(tag, l, idx, flight, after, carry):
        gather_wait(tag, False, l, idx, flight, after)
        flight, carry = gather_start(tag, True, l, idx, carry)
        gather_wait(tag, True, l, idx, flight, carry)
        return carry

    first = []
    for g, idx in enumerate((ffn1_w, mixer_w, ffn2_w)):
        flight, x = gather_start(f"0_{g}", False, 0, idx, x)
        first.append(flight)

    gathered = all_gather([W["g_pre"], W["g_post"], W["conv_w"], c], [0, 0, 0, 0])
    g_pre = gathered[0].transpose(1, 2, 0, 3).reshape(L, N_SUB, D)
    g_post = gathered[1].transpose(1, 2, 0, 3).reshape(L, N_SUB, D)
    conv_w = _pad_rows(gathered[2].transpose(1, 2, 0, 3).reshape(L, 3, MIX_W))
    c_all = gathered[3].reshape(N_DEV, D)

    b_cols = lax.dynamic_slice_in_dim(W["b_ada"], dev * n_ada, n_ada, axis=1)[:, None, :]
    ada_cols = ada_fwd(c_all, W["w_ada"], b_cols)
    ada_all = all_gather([ada_cols], [0])[0]
    ada = lax.dynamic_index_in_dim(ada_all, dev, axis=2, keepdims=False)
    ada = ada.transpose(1, 0, 2).reshape(L, N_SUB, 3, D)
    zeros = jnp.zeros((L, N_SUB, D), F32)
    pv_all = jnp.stack([g_pre, ada[:, :, 0], ada[:, :, 1], g_post, ada[:, :, 2], zeros, zeros, zeros], axis=2)

    lam = jnp.stack([W["lam_re"].reshape(L, SSM_W), W["lam_im"].reshape(L, SSM_W),
                     jnp.repeat(W["log_dt"], SSM_STATE, axis=1)], axis=1)
    lam = _pad_rows(lam)
    b_t = jnp.stack([_tile_b(W["ssm_b_re"]), _tile_b(W["ssm_b_im"])], axis=1)
    c_t = jnp.stack([_tile_c(W["ssm_c_re"]), _tile_c(W["ssm_c_im"])], axis=1)
    avec, b_bd, c_bd = s5_params(lam, b_t, c_t)
    ssm_d = _pad_rows(W["ssm_d"][:, None, :])
    pool_scale = _pad_rows(W["pool_scale"][:, None, :])
    eye4 = jnp.eye(len(POOL_WINDOWS), dtype=F32)
    w_bd = jnp.einsum("lgcd,gh->lgchd", W["w_pool"], eye4).reshape(L, MIX_W, MIX_W).astype(BF16)

    x = gather_finish("0_0", 0, ffn1_w, first[0], pv_all, x)

    def ffn_weights(l, k):
        b = bufs[l]
        return b[k].reshape(1, 1, 2, 4, D, FF_BLK), b[2 + k].reshape(1, 1, 4, FF_BLK, D)

    def mixer_weights(l):
        b = bufs[l]
        return b[4], b[5], b[6], b[7], b[8], b[9].reshape(1, D, D)

    l0 = jnp.array([0], jnp.int32)
    k0 = jnp.array([0, 0], jnp.int32)
    saved = []
    grouped = {0: first}
    for l in range(L):
        li = jnp.array([l], jnp.int32)
        nxt = l + 1 < L and l + 1 != 1
        if l == 0 and L > 1:
            grouped[1] = []
            for g, idx in enumerate((ffn1_w, mixer_w, ffn2_w)):
                flight, x = gather_start(f"1_{g}", False, 1, idx, x)
                grouped[1].append(flight)
        if nxt:
            flight, x = gather_start(f"{l + 1}", False, l + 1, all_w, x)
        if l == 1:
            x = gather_finish("1_0", 1, ffn1_w, grouped[1][0], x, x)
        x0 = x
        ab0, f0, x1 = ffn_fwd(k0, x0, pv_all[l, 0], *ffn_weights(l, 0))
        if l in grouped:
            x1 = gather_finish(f"{l}_1", l, mixer_w, grouped[l][1], x1, x1)
        wg_in, wg_conv, wg_glu, wg_pool, wg_sb, wg_out = mixer_weights(l)
        p = mix_in_fwd(l0, x1, pv_all[l, 1], wg_in)
        za = conv_fwd(li, p, conv_w)
        z = pool_fwd(li, p, w_bd, pool_scale)
        s = s5_scan(li, avec, s5_bu(li, p, b_bd), False)
        yg = s5_out(li, p, s, c_bd, ssm_d)
        o, sb_tot = sb_fwd(p)
        x2, m = merge_fwd(l0, p, za, yg, z, o, x1, pv_all[l, 1], wg_conv, wg_glu, wg_pool, wg_sb, wg_out)
        if l in grouped:
            x2 = gather_finish(f"{l}_2", l, ffn2_w, grouped[l][2], x2, x2)
        if nxt:
            gather_wait(f"{l + 1}", False, l + 1, all_w, flight, x2)
            flight, x2 = gather_start(f"{l + 1}", True, l + 1, all_w, x2)
        ab1, f1, x = ffn_fwd(k0, x2, pv_all[l, 2], *ffn_weights(l, 1))
        if nxt:
            gather_wait(f"{l + 1}", True, l + 1, all_w, flight, x)
        saved.append((x0, ab0, f0, x1, p, za, z, s, yg, o, sb_tot, m, x2, ab1, f1))

    dx, loss_blk = loss_head(x, target)
    loss = lax.psum(loss_blk[0, 0], ("x", "y", "c"))

    n_blocks = 10
    ffn2_g, mixer_g, ffn1_g = (1, 3), (4, 5, 6, 7, 8, 9), (0, 2)
    recvs, owns, in_flight = [[None] * n_blocks for _ in range(L)], [[None] * n_blocks for _ in range(L)], []

    def exchange_start(tag, layer, idx, blocks, carry):
        n = len(idx)
        plan = _exchange_plan(n)
        arrays = list(blocks) + [lax.empty(a.shape, a.dtype) for a in blocks]
        s_sem, r_sem, arrays, carry = copies_start(f"exchange_start_{tag}", plan, (N_DEV - 1) * n, arrays, carry)
        in_flight.append((tag, layer, idx, plan, s_sem, r_sem, arrays))
        return carry

    def settle(after, upto):
        for flight in [f for f in in_flight if f[1] >= upto]:
            in_flight.remove(flight)
            tag, layer, idx, plan, s_sem, r_sem, arrays = flight
            arrays = copies_wait(f"exchange_wait_{tag}", plan, s_sem, r_sem, arrays, after)
            for j, i in enumerate(idx):
                owns[layer][i], recvs[layer][i] = arrays[j], arrays[len(idx) + j]

    pgs = [None] * L
    small = {k: [None] * L for k in ("conv_w", "w_bd", "pool_scale", "ssm_d", "gb", "gc", "da")}
    for l in reversed(range(L)):
        li = jnp.array([l], jnp.int32)
        x0, ab0, f0, x1, p, za, z, s, yg, o, sb_tot, m, x2, ab1, f1 = saved[l]
        wg_in, wg_conv, wg_glu, wg_pool, wg_sb, wg_out = mixer_weights(l)
        g_in1, g_out1, dx, pg2 = ffn_bwd(k0, dx, x2, f1, pv_all[l, 2], ab1, *ffn_weights(l, 1))
        dx = exchange_start(f"{l}_ffn2", l, ffn2_g,
                            [g_in1.reshape(N_DEV, D, FF_BLK), g_out1.reshape(N_DEV, D_FF // N_DEV, D)], dx)
        (dza, dyg, dz, do, dgates, pg1m, g_conv, g_glu, g_pool, g_sb, g_wo) = merge_bwd(
            l0, p, za, yg, z, o, m, dx, pv_all[l, 1], wg_conv, wg_glu, wg_pool, wg_sb, wg_out)
        d_conv, small["conv_w"][l] = conv_bwd(li, p, dza, conv_w)
        d_pool, small["w_bd"][l], small["pool_scale"][l] = pool_bwd(li, p, dz, w_bd, pool_scale)
        ds, du_skip, small["gc"][l], small["ssm_d"][l] = s5_bwd_y(li, p, s, dyg, c_bd, ssm_d)
        lam_s = s5_scan(li, avec, ds, True)
        d_ssm, small["gb"][l] = s5_bwd_u(li, p, lam_s, du_skip, b_bd)
        small["da"][l] = s5_bwd_a(s, lam_s)
        d_qkv = sb_bwd(p, do, sb_tot)
        dp = dp_assemble(d_conv, d_ssm, d_pool, d_qkv, dgates)
        dx, g_win, pg1i = mix_in_bwd(l0, dp, dx, x1, pv_all[l, 1], wg_in)
        settle(dx, l + 1)
        dx = exchange_start(f"{l}_mixer", l, mixer_g,
                            [g_win, g_conv, g_glu, g_pool, g_sb, g_wo.reshape(N_DEV, D // N_DEV, D)], dx)
        g_in0, g_out0, dx, pg0 = ffn_bwd(k0, dx, x0, f0, pv_all[l, 0], ab0, *ffn_weights(l, 0))
        pgs[l] = jnp.stack([pg0, pg1m + pg1i, pg2])
        last_ffn1 = [g_in0.reshape(N_DEV, D, FF_BLK), g_out0.reshape(N_DEV, D_FF // N_DEV, D)]
        if l > 0:
            dx = exchange_start(f"{l}_ffn1", l, ffn1_g, last_ffn1, dx)

    dlam, db_t, dc_t, dldt = s5_params_bwd(lam, b_t, jnp.stack(small["gb"]), jnp.stack(small["gc"]),
                                           jnp.stack(small["da"]))
    pg = jnp.stack(pgs)
    d_ada = jnp.stack([pg[:, :, PV_SHIFT], pg[:, :, PV_SCALE], pg[:, :, PV_GATE]], axis=2).reshape(L, N_SUB * 3 * D)
    db = db_t[..., :SSM_STATE].reshape(L, 2, SSM_GROUPS, SSM_GROUP, SSM_STATE).transpose(0, 1, 2, 4, 3)
    dc = dc_t[..., :SSM_GROUP].reshape(L, 2, SSM_GROUPS, SSM_STATE, SSM_GROUP).transpose(0, 1, 2, 4, 3)
    d_wpool = jnp.einsum("lgcgd->lgcd", jnp.stack(small["w_bd"]).reshape(L, 4, 64, 4, 64))
    contrib = {
        "b_ada": d_ada, "g_pre": pg[:, :, PV_GPRE], "g_post": pg[:, :, PV_GPOST],
        "conv_w": jnp.stack(small["conv_w"])[:, :3], "lam_re": dlam[:, 0].reshape(L, SSM_GROUPS, SSM_STATE),
        "lam_im": dlam[:, 1].reshape(L, SSM_GROUPS, SSM_STATE), "log_dt": dldt[:, 2, :SSM_GROUPS],
        "ssm_b_re": db[:, 0], "ssm_b_im": db[:, 1], "ssm_c_re": dc[:, 0], "ssm_c_im": dc[:, 1],
        "ssm_d": jnp.stack(small["ssm_d"])[:, 0], "w_pool": d_wpool,
        "pool_scale": jnp.stack(small["pool_scale"])[:, 0],
    }
    contrib_shapes = [contrib[k].shape for k in SMALL_NAMES]

    big_idx = {"w_ff_in": (0, 1), "w_ff_out": (2, 3), "w_in": (4,), "w_conv_out": (5,), "w_glu": (6,),
               "w_pool_out": (7,), "w_sb_out": (8,), "w_out": (9,)}

    def big(name, layers, prev, after=None):
        idx = big_idx[name]
        flat = (-1,) + W[name].shape[-2:]
        return sum_update(dev_s, layers[0] * len(idx), [recvs[l][i] for l in layers for i in idx],
                          [owns[l][i] for l in layers for i in idx],
                          W[name].reshape(flat), M[name].reshape(flat), V[name].reshape(flat), prev, after)

    partial = {}

    def partial_updates(names, after):
        for name in names:
            if L > 1:
                partial[name] = big(name, list(range(1, L)), None, after)
                after = partial[name][0]
        return after

    pack_buf = [place_own(dev_s, _pack([contrib[k] for k in SMALL_NAMES]))]
    plan_a, plan_b = _gather_plan((0,), False), _gather_plan((0,), True)
    s_sem, r_sem, pack_buf, dx = copies_start("small_gather_a_start", plan_a, 4, pack_buf, dx)
    dx = exchange_start("0_ffn1", 0, ffn1_g, last_ffn1, dx)
    after = partial_updates(BIG_NAMES[:1], dx)
    pack_buf = copies_wait("small_gather_a_wait", plan_a, s_sem, r_sem, pack_buf, after)
    s_sem, r_sem, pack_buf, dx = copies_start("small_gather_b_start", plan_b, 3, pack_buf, dx)
    after = partial_updates(BIG_NAMES[1:], dx)
    pack_all = copies_wait("small_gather_b_wait", plan_b, s_sem, r_sem, pack_buf, after)[0]
    total = dict(zip(SMALL_NAMES, _unpack(small_sum(pack_all), contrib_shapes)))
    d_ada_all = pack_all.reshape(N_DEV, -1)[:, :L * N_SUB * 3 * D].reshape(N_DEV, L, N_SUB * 3 * D)
    dada_cols = lax.dynamic_slice_in_dim(d_ada_all, dev * n_ada, n_ada, axis=2).transpose(1, 0, 2)
    n_g = D // N_DEV
    grads = {}
    for k in SMALL_NAMES:
        g = total[k]
        if k in ("g_pre", "g_post"):
            g = lax.dynamic_slice_in_dim(g, dev * n_g, n_g, axis=2)
        elif k == "conv_w":
            g = lax.dynamic_slice_in_dim(g, dev * (MIX_W // N_DEV), MIX_W // N_DEV, axis=2)
        grads[k] = g

    delta, new_m, new_v = {}, {}, {}
    shapes = [W[k].shape for k in SMALL_NAMES]
    dl, nm, nv = small_update(_pack([W[k] for k in SMALL_NAMES]), _pack([grads[k] for k in SMALL_NAMES]),
                              _pack([M[k] for k in SMALL_NAMES]), _pack([V[k] for k in SMALL_NAMES]))
    for k, a, b, cc in zip(SMALL_NAMES, _unpack(dl, shapes), _unpack(nm, shapes), _unpack(nv, shapes)):
        delta[k], new_m[k], new_v[k] = a, b, cc
    grads["w_ada"], delta["w_ada"], new_m["w_ada"], new_v["w_ada"] = ada_update(
        c_all, dada_cols, W["w_ada"], M["w_ada"], V["w_ada"])

    settle(new_m["w_ada"], 0)
    for name in BIG_NAMES:
        outs = big(name, [0], partial.get(name))
        grads[name], delta[name], new_m[name], new_v[name] = [o.reshape(W[name].shape) for o in outs]

    return (loss, dx[None], *[grads[k] for k in WEIGHT_NAMES], *[delta[k] for k in WEIGHT_NAMES],
            *[new_m[k] for k in WEIGHT_NAMES], *[new_v[k] for k in WEIGHT_NAMES])


def kernel(x, c, w_ada, b_ada, g_pre, g_post, w_ff_in, w_ff_out, w_in, conv_w, w_conv_out, lam_re, lam_im, log_dt, ssm_b_re, ssm_b_im, ssm_c_re, ssm_c_im, ssm_d, w_glu, w_pool, pool_scale, w_pool_out, w_sb_out, w_out, loss_target, m_w_ada, m_b_ada, m_g_pre, m_g_post, m_w_ff_in, m_w_ff_out, m_w_in, m_conv_w, m_w_conv_out, m_lam_re, m_lam_im, m_log_dt, m_ssm_b_re, m_ssm_b_im, m_ssm_c_re, m_ssm_c_im, m_ssm_d, m_w_glu, m_w_pool, m_pool_scale, m_w_pool_out, m_w_sb_out, m_w_out, v_w_ada, v_b_ada, v_g_pre, v_g_post, v_w_ff_in, v_w_ff_out, v_w_in, v_conv_w, v_w_conv_out, v_lam_re, v_lam_im, v_log_dt, v_ssm_b_re, v_ssm_b_im, v_ssm_c_re, v_ssm_c_im, v_ssm_d, v_w_glu, v_w_pool, v_pool_scale, v_w_pool_out, v_w_sb_out, v_w_out):
    w = (w_ada, b_ada, g_pre, g_post, w_ff_in, w_ff_out, w_in, conv_w, w_conv_out, lam_re, lam_im, log_dt, ssm_b_re, ssm_b_im, ssm_c_re, ssm_c_im, ssm_d, w_glu, w_pool, pool_scale, w_pool_out, w_sb_out, w_out)
    m = (m_w_ada, m_b_ada, m_g_pre, m_g_post, m_w_ff_in, m_w_ff_out, m_w_in, m_conv_w, m_w_conv_out, m_lam_re, m_lam_im, m_log_dt, m_ssm_b_re, m_ssm_b_im, m_ssm_c_re, m_ssm_c_im, m_ssm_d, m_w_glu, m_w_pool, m_pool_scale, m_w_pool_out, m_w_sb_out, m_w_out)
    v = (v_w_ada, v_b_ada, v_g_pre, v_g_post, v_w_ff_in, v_w_ff_out, v_w_in, v_conv_w, v_w_conv_out, v_lam_re, v_lam_im, v_log_dt, v_ssm_b_re, v_ssm_b_im, v_ssm_c_re, v_ssm_c_im, v_ssm_d, v_w_glu, v_w_pool, v_pool_scale, v_w_pool_out, v_w_sb_out, v_w_out)
    return _step(x, c, loss_target, dict(zip(WEIGHT_NAMES, w)), dict(zip(WEIGHT_NAMES, m)), dict(zip(WEIGHT_NAMES, v)))
```

```python
import functools

import jax
import jax.numpy as jnp
from jax import lax
from jax.experimental import pallas as pl
from jax.experimental.pallas import tpu as pltpu

F32 = jnp.float32
BF16 = jnp.bfloat16

N_DEV = 8
D_MODEL = 1024
D_FF = 2816
FF_BLK = D_FF // 4
N_SUB = 3
MIX_W = 256
IN_COLS = 6144
IN_BLK = IN_COLS // N_DEV
GATE_OFF = 2048
SSM_GROUPS, SSM_GROUP, SSM_STATE = 16, 16, 64
SSM_W = SSM_GROUPS * SSM_STATE
POOL_WINDOWS = (2, 4, 8, 16)
SB_HEAD = 64
EPS = 1e-6
DT_LAMBDA_RE_MAX = -1e-4
ADAM_LR, ADAM_B1, ADAM_B2, ADAM_EPS, ADAM_WD, ADAM_STEP = 0.001, 0.9, 0.999, 1e-08, 0.01, 10

VMEM_LIMIT = 56 * 1024 * 1024
FFN_BWD_VMEM_LIMIT = 60 * 1024 * 1024

PV_GPRE, PV_SHIFT, PV_SCALE, PV_GPOST, PV_GATE = 0, 1, 2, 3, 4


def _cparams(sem):
    return pltpu.CompilerParams(dimension_semantics=sem, vmem_limit_bytes=VMEM_LIMIT)


def _dot(a, b):
    return jnp.dot(a, b, preferred_element_type=F32)


def _dot_nt(a, b):
    return lax.dot_general(a, b, (((1,), (1,)), ((), ())), preferred_element_type=F32)


def _dot_tn(a, b):
    return lax.dot_general(a, b, (((0,), (0,)), ((), ())), preferred_element_type=F32)


def _rms(x):
    r = lax.rsqrt(jnp.mean(x * x, axis=-1, keepdims=True) + EPS)
    return x * r, r


def _rms_bwd(dn, n, r):
    return r * (dn - n * jnp.mean(dn * n, axis=-1, keepdims=True))


def _sigmoid(x):
    return 1.0 / (1.0 + jnp.exp(-x))


def _colsum(x):
    return jnp.sum(x, axis=0, keepdims=True)


def _prenorm(x, pv_ref):
    n, r = _rms(x)
    hn = n * pv_ref[PV_GPRE:PV_GPRE + 1, :]
    h = hn * (1.0 + pv_ref[PV_SCALE:PV_SCALE + 1, :]) + pv_ref[PV_SHIFT:PV_SHIFT + 1, :]
    return h, n, r, hn


def _prenorm_bwd(dh, dxn, x, pv_ref, pg_ref):
    _, n, r, hn = _prenorm(x, pv_ref)
    pg_ref[PV_SHIFT:PV_SHIFT + 1, :] += _colsum(dh)
    pg_ref[PV_SCALE:PV_SCALE + 1, :] += _colsum(dh * hn)
    dhn = dh * (1.0 + pv_ref[PV_SCALE:PV_SCALE + 1, :])
    pg_ref[PV_GPRE:PV_GPRE + 1, :] += _colsum(dhn * n)
    dn = dhn * pv_ref[PV_GPRE:PV_GPRE + 1, :]
    return dxn + _rms_bwd(dn, n, r)


def _postnorm_res(x, f, pv_ref, coef):
    nf, _ = _rms(f)
    return x + (coef * (1.0 + pv_ref[PV_GATE:PV_GATE + 1, :])) * (nf * pv_ref[PV_GPOST:PV_GPOST + 1, :])


def _postnorm_bwd(dxn, f, pv_ref, pg_ref, coef):
    nf, rf = _rms(f)
    g_post = pv_ref[PV_GPOST:PV_GPOST + 1, :]
    pg_ref[PV_GATE:PV_GATE + 1, :] += _colsum(dxn * (nf * g_post)) * coef
    dnfg = dxn * (coef * (1.0 + pv_ref[PV_GATE:PV_GATE + 1, :]))
    pg_ref[PV_GPOST:PV_GPOST + 1, :] += _colsum(dnfg * nf)
    return _rms_bwd(dnfg * g_post, nf, rf)


def ffn_fwd(lk, x, pv, wg_in, wg_out, tm=512):
    T, D = x.shape
    tm = min(tm, T)
    nj, ni = 4, T // tm

    def body(lk_ref, x_ref, pv_ref, win_ref, wout_ref, ab_ref, f_ref, xn_ref, h_all, acc_all):
        j, i = pl.program_id(0), pl.program_id(1)
        rows = pl.ds(pl.multiple_of(i * tm, tm), tm)

        @pl.when(j == 0)
        def _():
            h, _, _, _ = _prenorm(x_ref[...], pv_ref)
            h_all[rows, :] = h.astype(BF16)
            acc_all[rows, :] = jnp.zeros((tm, D), F32)

        h = h_all[rows, :]
        a = _dot(h, win_ref[0])
        b = _dot(h, win_ref[1])
        ab_ref[0] = a.astype(BF16)
        ab_ref[1] = b.astype(BF16)
        act = (a * _sigmoid(a) * b).astype(BF16)
        acc_all[rows, :] += _dot(act, wout_ref[...])

        @pl.when(j == nj - 1)
        def _():
            f = acc_all[rows, :]
            f_ref[...] = f
            xn_ref[...] = _postnorm_res(x_ref[...], f, pv_ref, 0.5)

    ends = lambda j, i, lk: (jnp.where((j == 0) | (j == nj - 1), i, ni - 1), 0)
    last = lambda j, i, lk: (jnp.where(j == nj - 1, i, 0), 0)
    grid_spec = pltpu.PrefetchScalarGridSpec(
        num_scalar_prefetch=1, grid=(nj, ni),
        in_specs=[
            pl.BlockSpec((tm, D), ends),
            pl.BlockSpec((8, D), lambda j, i, lk: (0, 0)),
            pl.BlockSpec((None, None, 2, None, D, FF_BLK), lambda j, i, lk: (lk[0], lk[1], 0, j, 0, 0)),
            pl.BlockSpec((None, None, None, FF_BLK, D), lambda j, i, lk: (lk[0], lk[1], j, 0, 0)),
        ],
        out_specs=[
            pl.BlockSpec((2, None, tm, FF_BLK), lambda j, i, lk: (0, j, i, 0)),
            pl.BlockSpec((tm, D), last),
            pl.BlockSpec((tm, D), last),
        ],
        scratch_shapes=[pltpu.VMEM((T, D), BF16), pltpu.VMEM((T, D), F32)],
    )
    return pl.pallas_call(
        body, name="ffn_fwd", grid_spec=grid_spec,
        out_shape=[jax.ShapeDtypeStruct((2, nj, T, FF_BLK), BF16),
                   jax.ShapeDtypeStruct((T, D), F32), jax.ShapeDtypeStruct((T, D), F32)],
        compiler_params=_cparams(("arbitrary", "arbitrary")),
    )(lk, x, pv, wg_in, wg_out)


def ffn_bwd(lk, dxn, x, f, pv, ab, wg_in, wg_out, tm=256):
    T, D = x.shape
    tm = min(tm, T)
    nj, ni = 4, T // tm

    def body(lk_ref, dxn_ref, x_ref, f_ref, pv_ref, ab_ref, win_ref, wout_ref,
             gin_ref, gout_ref, dx_ref, pg_ref, df_all, h_all, dh_all, acc_in, acc_out):
        j, i = pl.program_id(0), pl.program_id(1)
        rows = pl.ds(pl.multiple_of(i * tm, tm), tm)

        @pl.when((i == 0) & (j == 0))
        def _():
            pg_ref[...] = jnp.zeros_like(pg_ref)

        @pl.when(j == 0)
        def _():
            df = _postnorm_bwd(dxn_ref[...], f_ref[...], pv_ref, pg_ref, 0.5)
            df_all[rows, :] = df.astype(BF16)
            h, _, _, _ = _prenorm(x_ref[...], pv_ref)
            h_all[rows, :] = h.astype(BF16)
            dh_all[rows, :] = jnp.zeros((tm, D), F32)

        @pl.when(i == 0)
        def _():
            acc_in[...] = jnp.zeros_like(acc_in)
            acc_out[...] = jnp.zeros_like(acc_out)

        df_t = df_all[rows, :]
        h_t = h_all[rows, :]
        dact = _dot_nt(df_t, wout_ref[...])
        a = ab_ref[0].astype(F32)
        b = ab_ref[1].astype(F32)
        sig = _sigmoid(a)
        s = a * sig
        da = (dact * b * (sig * (1.0 + a * (1.0 - sig)))).astype(BF16)
        db = (dact * s).astype(BF16)
        act = (s * b).astype(BF16)
        dh_all[rows, :] += _dot_nt(da, win_ref[0]) + _dot_nt(db, win_ref[1])
        acc_in[0] += _dot_tn(da, h_t)
        acc_in[1] += _dot_tn(db, h_t)
        acc_out[...] += _dot_tn(act, df_t)

        @pl.when(i == ni - 1)
        def _():
            gin_ref[...] = acc_in[...].astype(BF16)
            gout_ref[...] = acc_out[...].astype(BF16)

        @pl.when(j == nj - 1)
        def _():
            dx_ref[...] = _prenorm_bwd(dh_all[rows, :], dxn_ref[...], x_ref[...], pv_ref, pg_ref)

    ends = lambda j, i, lk: (jnp.where((j == 0) | (j == nj - 1), i, ni - 1), 0)
    first = lambda j, i, lk: (jnp.where(j == 0, i, ni - 1), 0)
    grid_spec = pltpu.PrefetchScalarGridSpec(
        num_scalar_prefetch=1, grid=(nj, ni),
        in_specs=[pl.BlockSpec((tm, D), ends), pl.BlockSpec((tm, D), ends), pl.BlockSpec((tm, D), first),
                  pl.BlockSpec((8, D), lambda j, i, lk: (0, 0)),
                  pl.BlockSpec((2, None, tm, FF_BLK), lambda j, i, lk: (0, j, i, 0)),
                  pl.BlockSpec((None, None, 2, None, D, FF_BLK), lambda j, i, lk: (lk[0], lk[1], 0, j, 0, 0)),
                  pl.BlockSpec((None, None, None, FF_BLK, D), lambda j, i, lk: (lk[0], lk[1], j, 0, 0))],
        out_specs=[pl.BlockSpec((2, None, FF_BLK, D), lambda j, i, lk: (0, j, 0, 0)),
                   pl.BlockSpec((None, FF_BLK, D), lambda j, i, lk: (j, 0, 0)),
                   pl.BlockSpec((tm, D), lambda j, i, lk: (jnp.where(j == nj - 1, i, 0), 0)),
                   pl.BlockSpec((8, D), lambda j, i, lk: (0, 0))],
        scratch_shapes=[pltpu.VMEM((T, D), BF16), pltpu.VMEM((T, D), BF16), pltpu.VMEM((T, D), F32),
                        pltpu.VMEM((2, FF_BLK, D), F32), pltpu.VMEM((FF_BLK, D), F32)],
    )
    return pl.pallas_call(
        body, name="ffn_bwd", grid_spec=grid_spec,
        out_shape=[jax.ShapeDtypeStruct((2, nj, FF_BLK, D), BF16), jax.ShapeDtypeStruct((nj, FF_BLK, D), BF16),
                   jax.ShapeDtypeStruct((T, D), F32), jax.ShapeDtypeStruct((8, D), F32)],
        compiler_params=pltpu.CompilerParams(dimension_semantics=("arbitrary", "arbitrary"),
                                             vmem_limit_bytes=FFN_BWD_VMEM_LIMIT),
    )(lk, dxn, x, f, pv, ab, wg_in, wg_out)


def mix_in_fwd(l, x, pv, wg, tm=1024):
    T, D = x.shape
    tm = min(tm, T)

    def body(l_ref, x_ref, pv_ref, w_ref, p_ref, h_sc):
        @pl.when(pl.program_id(1) == 0)
        def _():
            h, _, _, _ = _prenorm(x_ref[...], pv_ref)
            h_sc[...] = h.astype(BF16)

        p_ref[...] = _dot(h_sc[...], w_ref[...])

    grid_spec = pltpu.PrefetchScalarGridSpec(
        num_scalar_prefetch=1, grid=(T // tm, N_DEV),
        in_specs=[pl.BlockSpec((tm, D), lambda i, j, l: (i, 0)),
                  pl.BlockSpec((8, D), lambda i, j, l: (0, 0)),
                  pl.BlockSpec((None, None, D, IN_BLK), lambda i, j, l: (l[0], j, 0, 0))],
        out_specs=pl.BlockSpec((tm, IN_BLK), lambda i, j, l: (i, j)),
        scratch_shapes=[pltpu.VMEM((tm, D), BF16)],
    )
    return pl.pallas_call(
        body, name="mix_in_fwd", grid_spec=grid_spec,
        out_shape=jax.ShapeDtypeStruct((T, IN_COLS), F32),
        compiler_params=_cparams(("arbitrary", "arbitrary")),
    )(l, x, pv, wg)


def mix_in_bwd(l, dp, dxn, x, pv, wg, tm=512):
    T, D = x.shape
    tm = min(tm, T)
    nj, ni = N_DEV, T // tm

    def body(l_ref, dp_ref, dxn_ref, x_ref, pv_ref, w_ref, dx_ref, gw_ref, pg_ref, h_all, dh_all, acc):
        j, i = pl.program_id(0), pl.program_id(1)
        rows = pl.ds(pl.multiple_of(i * tm, tm), tm)

        @pl.when((i == 0) & (j == 0))
        def _():
            pg_ref[...] = jnp.zeros_like(pg_ref)

        @pl.when(j == 0)
        def _():
            h, _, _, _ = _prenorm(x_ref[...], pv_ref)
            h_all[rows, :] = h.astype(BF16)
            dh_all[rows, :] = jnp.zeros((tm, D), F32)

        @pl.when(i == 0)
        def _():
            acc[...] = jnp.zeros_like(acc)

        dp_t = dp_ref[...]
        dh_all[rows, :] += _dot_nt(dp_t, w_ref[...])
        acc[...] += _dot_tn(h_all[rows, :], dp_t)

        @pl.when(i == ni - 1)
        def _():
            gw_ref[...] = acc[...].astype(BF16)

        @pl.when(j == nj - 1)
        def _():
            dx_ref[...] = _prenorm_bwd(dh_all[rows, :], dxn_ref[...], x_ref[...], pv_ref, pg_ref)

    ends = lambda j, i, l: (jnp.where((j == 0) | (j == nj - 1), i, ni - 1), 0)
    last = lambda j, i, l: (jnp.where(j == nj - 1, i, 0), 0)
    grid_spec = pltpu.PrefetchScalarGridSpec(
        num_scalar_prefetch=1, grid=(nj, ni),
        in_specs=[pl.BlockSpec((tm, IN_BLK), lambda j, i, l: (i, j)),
                  pl.BlockSpec((tm, D), last), pl.BlockSpec((tm, D), ends),
                  pl.BlockSpec((8, D), lambda j, i, l: (0, 0)),
                  pl.BlockSpec((None, None, D, IN_BLK), lambda j, i, l: (l[0], j, 0, 0))],
        out_specs=[pl.BlockSpec((tm, D), last), pl.BlockSpec((None, D, IN_BLK), lambda j, i, l: (j, 0, 0)),
                   pl.BlockSpec((8, D), lambda j, i, l: (0, 0))],
        scratch_shapes=[pltpu.VMEM((T, D), BF16), pltpu.VMEM((T, D), F32), pltpu.VMEM((D, IN_BLK), F32)],
    )
    return pl.pallas_call(
        body, name="mix_in_bwd", grid_spec=grid_spec,
        out_shape=[jax.ShapeDtypeStruct((T, D), F32), jax.ShapeDtypeStruct((nj, D, IN_BLK), BF16),
                   jax.ShapeDtypeStruct((8, D), F32)],
        compiler_params=_cparams(("arbitrary", "arbitrary")),
    )(l, dp, dxn, x, pv, wg)


SEQ_CHUNK = 256
HALO = 16


def _shift_down(ext, d):
    return pltpu.roll(ext, d, 0)


def _shift_up(ext, d):
    return pltpu.roll(ext, ext.shape[0] - d, 0)


def _rows_with_lead(load, c, width):
    t0 = c * SEQ_CHUNK
    if c == 0:
        return jnp.concatenate([jnp.zeros((HALO, width), F32), load(0, SEQ_CHUNK)], axis=0)
    return load(t0 - HALO, SEQ_CHUNK + HALO)


def _rows_with_tail(load, c, n_chunks, width):
    t0 = c * SEQ_CHUNK
    if c == n_chunks - 1:
        return jnp.concatenate([load(t0, SEQ_CHUNK), jnp.zeros((HALO, width), F32)], axis=0)
    return load(t0, SEQ_CHUNK + HALO)


def conv_fwd(l, p, conv_w):
    T = p.shape[0]
    W = MIX_W
    nC = T // SEQ_CHUNK

    def body(l_ref, p_ref, w_ref, za_ref):
        w0, w1, w2 = w_ref[0:1, :], w_ref[1:2, :], w_ref[2:3, :]
        for c in range(nC):
            ext = _rows_with_lead(lambda s, n: p_ref[s:s + n, W:2 * W] * p_ref[s:s + n, 2 * W:3 * W], c, W)
            y = w2 * ext + w1 * _shift_down(ext, 1) + w0 * _shift_down(ext, 2)
            t0 = c * SEQ_CHUNK
            za_ref[t0:t0 + SEQ_CHUNK, :] = (p_ref[t0:t0 + SEQ_CHUNK, 0:W] * y[HALO:]).astype(BF16)

    grid_spec = pltpu.PrefetchScalarGridSpec(
        num_scalar_prefetch=1, grid=(1,),
        in_specs=[pl.BlockSpec((T, 3 * W), lambda i, l: (0, 0)),
                  pl.BlockSpec((None, 8, W), lambda i, l: (l[0], 0, 0))],
        out_specs=pl.BlockSpec((T, W), lambda i, l: (0, 0)),
    )
    return pl.pallas_call(
        body, name="conv_fwd", grid_spec=grid_spec,
        out_shape=jax.ShapeDtypeStruct((T, W), BF16),
        compiler_params=_cparams(("arbitrary",)),
    )(l, p, conv_w)


def conv_bwd(l, p, dza, conv_w):
    T = p.shape[0]
    W = MIX_W
    nC = T // SEQ_CHUNK

    def body(l_ref, p_ref, dza_ref, w_ref, dp_ref, dw_ref):
        w0, w1, w2 = w_ref[0:1, :], w_ref[1:2, :], w_ref[2:3, :]
        dw = [jnp.zeros((1, W), F32) for _ in range(3)]
        for c in range(nC):
            t0 = c * SEQ_CHUNK
            ext = _rows_with_lead(lambda s, n: p_ref[s:s + n, W:2 * W] * p_ref[s:s + n, 2 * W:3 * W], c, W)
            u1, u2 = _shift_down(ext, 1)[HALO:], _shift_down(ext, 2)[HALO:]
            u0 = ext[HALO:]
            y = w2 * u0 + w1 * u1 + w0 * u2
            dza_c = dza_ref[t0:t0 + SEQ_CHUNK, :]
            dy = dza_c * p_ref[t0:t0 + SEQ_CHUNK, 0:W]
            dw[0] += _colsum(dy * u2)
            dw[1] += _colsum(dy * u1)
            dw[2] += _colsum(dy * u0)
            dye = _rows_with_tail(lambda s, n: dza_ref[s:s + n, :] * p_ref[s:s + n, 0:W], c, nC, W)
            du = (w2 * dye + w1 * _shift_up(dye, 1) + w0 * _shift_up(dye, 2))[:SEQ_CHUNK]
            dp_ref[t0:t0 + SEQ_CHUNK, 0:W] = (dza_c * y).astype(BF16)
            dp_ref[t0:t0 + SEQ_CHUNK, W:2 * W] = (du * p_ref[t0:t0 + SEQ_CHUNK, 2 * W:3 * W]).astype(BF16)
            dp_ref[t0:t0 + SEQ_CHUNK, 2 * W:3 * W] = (du * p_ref[t0:t0 + SEQ_CHUNK, W:2 * W]).astype(BF16)
        dw_ref[...] = jnp.concatenate(dw + [jnp.zeros((5, W), F32)], axis=0)

    grid_spec = pltpu.PrefetchScalarGridSpec(
        num_scalar_prefetch=1, grid=(1,),
        in_specs=[pl.BlockSpec((T, 3 * W), lambda i, l: (0, 0)),
                  pl.BlockSpec((T, W), lambda i, l: (0, 0)),
                  pl.BlockSpec((None, 8, W), lambda i, l: (l[0], 0, 0))],
        out_specs=[pl.BlockSpec((T, 3 * W), lambda i, l: (0, 0)), pl.BlockSpec((8, W), lambda i, l: (0, 0))],
    )
    return pl.pallas_call(
        body, name="conv_bwd", grid_spec=grid_spec,
        out_shape=[jax.ShapeDtypeStruct((T, 3 * W), BF16), jax.ShapeDtypeStruct((8, W), F32)],
        compiler_params=_cparams(("arbitrary",)),
    )(l, p, dza, conv_w)


def _pool_consts(rows, t0):
    lane = lax.broadcasted_iota(jnp.int32, (rows, MIX_W), 1)
    t = lax.broadcasted_iota(jnp.int32, (rows, MIX_W), 0) + t0
    win = jnp.where(lane < 64, 2, jnp.where(lane < 128, 4, jnp.where(lane < 192, 8, 16)))
    inv = 1.0 / jnp.minimum(t + 1, win).astype(F32)
    return lane, inv


def _pick_window(lane, s2, s4, s8, s16):
    return jnp.where(lane < 64, s2, jnp.where(lane < 128, s4, jnp.where(lane < 192, s8, s16)))


def _pooled_chunk(u_ref, c):
    ext = _rows_with_lead(lambda s, n: u_ref[s:s + n, :], c, MIX_W)
    s2 = ext + _shift_down(ext, 1)
    s4 = s2 + _shift_down(s2, 2)
    s8 = s4 + _shift_down(s4, 4)
    s16 = s8 + _shift_down(s8, 8)
    lane, inv = _pool_consts(SEQ_CHUNK, c * SEQ_CHUNK)
    return _pick_window(lane, s2[HALO:], s4[HALO:], s8[HALO:], s16[HALO:]) * inv - ext[HALO:]


def pool_fwd(l, p, w_bd, scale):
    T = p.shape[0]
    W = MIX_W
    nC = T // SEQ_CHUNK

    def body(l_ref, u_ref, w_ref, sc_ref, z_ref):
        for c in range(nC):
            pooled = _pooled_chunk(u_ref, c)
            mixed = _dot(pooled.astype(BF16), w_ref[...])
            z_ref[c * SEQ_CHUNK:(c + 1) * SEQ_CHUNK, :] = (mixed * sc_ref[0:1, :]).astype(BF16)

    grid_spec = pltpu.PrefetchScalarGridSpec(
        num_scalar_prefetch=1, grid=(1,),
        in_specs=[pl.BlockSpec((T, W), lambda i, l: (0, 4)),
                  pl.BlockSpec((None, W, W), lambda i, l: (l[0], 0, 0)),
                  pl.BlockSpec((None, 8, W), lambda i, l: (l[0], 0, 0))],
        out_specs=pl.BlockSpec((T, W), lambda i, l: (0, 0)),
    )
    return pl.pallas_call(
        body, name="pool_fwd", grid_spec=grid_spec,
        out_shape=jax.ShapeDtypeStruct((T, W), BF16),
        compiler_params=_cparams(("arbitrary",)),
    )(l, p, w_bd, scale)


def pool_bwd(l, p, dz, w_bd, scale):
    T = p.shape[0]
    W = MIX_W
    nC = T // SEQ_CHUNK

    def body(l_ref, u_ref, dz_ref, w_ref, sc_ref, du_ref, dw_ref, dsc_ref, e_sc, dpl_sc):
        dw = jnp.zeros((W, W), F32)
        dsc = jnp.zeros((1, W), F32)
        for c in range(nC):
            t0 = c * SEQ_CHUNK
            pooled = _pooled_chunk(u_ref, c).astype(BF16)
            mixed = _dot(pooled, w_ref[...])
            dz_c = dz_ref[t0:t0 + SEQ_CHUNK, :]
            dsc += _colsum(dz_c * mixed)
            dmixed = (dz_c * sc_ref[0:1, :]).astype(BF16)
            dw += _dot_tn(pooled, dmixed)
            dpooled = _dot_nt(dmixed, w_ref[...])
            _, inv = _pool_consts(SEQ_CHUNK, t0)
            dpl_sc[t0:t0 + SEQ_CHUNK, :] = dpooled
            e_sc[t0:t0 + SEQ_CHUNK, :] = dpooled * inv
        for c in range(nC):
            t0 = c * SEQ_CHUNK
            ext = _rows_with_tail(lambda s, n: e_sc[s:s + n, :], c, nC, W)
            s2 = ext + _shift_up(ext, 1)
            s4 = s2 + _shift_up(s2, 2)
            s8 = s4 + _shift_up(s4, 4)
            s16 = s8 + _shift_up(s8, 8)
            lane, _ = _pool_consts(SEQ_CHUNK, t0)
            n = SEQ_CHUNK
            du = _pick_window(lane, s2[:n], s4[:n], s8[:n], s16[:n]) - dpl_sc[t0:t0 + SEQ_CHUNK, :]
            du_ref[t0:t0 + SEQ_CHUNK, :] = du.astype(BF16)
        dw_ref[...] = dw
        dsc_ref[...] = jnp.concatenate([dsc, jnp.zeros((7, W), F32)], axis=0)

    grid_spec = pltpu.PrefetchScalarGridSpec(
        num_scalar_prefetch=1, grid=(1,),
        in_specs=[pl.BlockSpec((T, W), lambda i, l: (0, 4)),
                  pl.BlockSpec((T, W), lambda i, l: (0, 0)),
                  pl.BlockSpec((None, W, W), lambda i, l: (l[0], 0, 0)),
                  pl.BlockSpec((None, 8, W), lambda i, l: (l[0], 0, 0))],
        out_specs=[pl.BlockSpec((T, W), lambda i, l: (0, 0)), pl.BlockSpec((W, W), lambda i, l: (0, 0)),
                   pl.BlockSpec((8, W), lambda i, l: (0, 0))],
        scratch_shapes=[pltpu.VMEM((T, W), F32), pltpu.VMEM((T, W), F32)],
    )
    return pl.pallas_call(
        body, name="pool_bwd", grid_spec=grid_spec,
        out_shape=[jax.ShapeDtypeStruct((T, W), BF16), jax.ShapeDtypeStruct((W, W), F32),
                   jax.ShapeDtypeStruct((8, W), F32)],
        compiler_params=_cparams(("arbitrary",)),
    )(l, p, dz, w_bd, scale)


def _s5_disc(lre, lim, ldt):
    lr = jnp.minimum(lre, DT_LAMBDA_RE_MAX)
    dt = jnp.exp(ldt)
    mag = jnp.exp(lr * dt)
    a_re = mag * jnp.cos(lim * dt)
    a_im = mag * jnp.sin(lim * dt)
    den = lr * lr + lim * lim
    nr = a_re - 1.0
    return a_re, a_im, (nr * lr + a_im * lim) / den, (a_im * lr - nr * lim) / den


def _bd_mask(shape, row_blk, col_blk):
    r = lax.broadcasted_iota(jnp.int32, shape, 0) >> (row_blk.bit_length() - 1)
    c = lax.broadcasted_iota(jnp.int32, shape, 1) >> (col_blk.bit_length() - 1)
    return r == c


def s5_params(lam, b_t, c_t):
    L = lam.shape[0]

    def body(lam_ref, b_ref, c_ref, a_ref, bbd_ref, cbd_ref):
        a_re, a_im, f_re, f_im = _s5_disc(lam_ref[0:1, :], lam_ref[1:2, :], lam_ref[2:3, :])
        a_ref[...] = jnp.concatenate([a_re, a_im, jnp.zeros((6, SSM_W), F32)], axis=0)
        mb = _bd_mask((MIX_W, SSM_W), SSM_GROUP, SSM_STATE)
        bbd_ref[0] = jnp.where(mb, f_re * b_ref[0] - f_im * b_ref[1], 0.0).astype(BF16)
        bbd_ref[1] = jnp.where(mb, f_re * b_ref[1] + f_im * b_ref[0], 0.0).astype(BF16)
        mc = _bd_mask((SSM_W, MIX_W), SSM_STATE, SSM_GROUP)
        cbd_ref[0] = jnp.where(mc, c_ref[0], 0.0).astype(BF16)
        cbd_ref[1] = jnp.where(mc, c_ref[1], 0.0).astype(BF16)

    return pl.pallas_call(
        body, name="s5_params", grid=(L,),
        in_specs=[pl.BlockSpec((None, 8, SSM_W), lambda l: (l, 0, 0)),
                  pl.BlockSpec((None, 2, MIX_W, SSM_W), lambda l: (l, 0, 0, 0)),
                  pl.BlockSpec((None, 2, SSM_W, MIX_W), lambda l: (l, 0, 0, 0))],
        out_specs=[pl.BlockSpec((None, 8, SSM_W), lambda l: (l, 0, 0)),
                   pl.BlockSpec((None, 2, MIX_W, SSM_W), lambda l: (l, 0, 0, 0)),
                   pl.BlockSpec((None, 2, SSM_W, MIX_W), lambda l: (l, 0, 0, 0))],
        out_shape=[jax.ShapeDtypeStruct((L, 8, SSM_W), F32),
                   jax.ShapeDtypeStruct((L, 2, MIX_W, SSM_W), BF16),
                   jax.ShapeDtypeStruct((L, 2, SSM_W, MIX_W), BF16)],
        compiler_params=_cparams(("arbitrary",)),
    )(lam, b_t, c_t)


def s5_params_bwd(lam, b_t, gb, gc, da):
    L = lam.shape[0]
    exact = functools.partial(jnp.dot, preferred_element_type=F32, precision=lax.Precision.HIGHEST)

    def fold(shape, period):
        c = lax.broadcasted_iota(jnp.int32, shape, 0) & (period - 1)
        return jnp.where(c == lax.broadcasted_iota(jnp.int32, shape, 1), 1.0, 0.0)

    def body(lam_ref, b_ref, gb_ref, gc_ref, da_ref, dlam_ref, db_ref, dc_ref, dgrp_ref):
        lre, lim, ldt = lam_ref[0:1, :], lam_ref[1:2, :], lam_ref[2:3, :]
        (a_re, a_im, f_re, f_im), vjp = jax.vjp(_s5_disc, lre, lim, ldt)
        mb = _bd_mask((MIX_W, SSM_W), SSM_GROUP, SSM_STATE)
        gbr = jnp.where(mb, gb_ref[0], 0.0)
        gbi = jnp.where(mb, gb_ref[1], 0.0)
        df_re = _colsum(gbr * b_ref[0] + gbi * b_ref[1])
        df_im = _colsum(gbi * b_ref[0] - gbr * b_ref[1])
        fold_b = fold((SSM_W, 128), SSM_STATE)
        db_ref[0] = exact(f_re * gbr + f_im * gbi, fold_b)
        db_ref[1] = exact(f_re * gbi - f_im * gbr, fold_b)
        mc = _bd_mask((SSM_W, MIX_W), SSM_STATE, SSM_GROUP)
        fold_c = fold((MIX_W, 128), SSM_GROUP)
        dc_ref[0] = exact(jnp.where(mc, gc_ref[0], 0.0), fold_c)
        dc_ref[1] = exact(jnp.where(mc, gc_ref[1], 0.0), fold_c)
        dlre, dlim, dldt = vjp((da_ref[0:1, :], da_ref[1:2, :], df_re, df_im))
        dl = jnp.concatenate([dlre, dlim, dldt, jnp.zeros((5, SSM_W), F32)], axis=0)
        dlam_ref[...] = dl
        grp = jnp.where(_bd_mask((SSM_W, 128), SSM_STATE, 1), 1.0, 0.0)
        dgrp_ref[...] = jnp.dot(dl, grp, preferred_element_type=F32, precision=lax.Precision.HIGHEST)

    vec = pl.BlockSpec((None, 8, SSM_W), lambda l: (l, 0, 0))
    bsp = pl.BlockSpec((None, 2, MIX_W, SSM_W), lambda l: (l, 0, 0, 0))
    csp = pl.BlockSpec((None, 2, SSM_W, MIX_W), lambda l: (l, 0, 0, 0))
    return pl.pallas_call(
        body, name="s5_params_bwd", grid=(L,),
        in_specs=[vec, bsp, bsp, csp, vec],
        out_specs=[vec, pl.BlockSpec((None, 2, MIX_W, 128), lambda l: (l, 0, 0, 0)),
                   pl.BlockSpec((None, 2, SSM_W, 128), lambda l: (l, 0, 0, 0)),
                   pl.BlockSpec((None, 8, 128), lambda l: (l, 0, 0))],
        out_shape=[jax.ShapeDtypeStruct((L, 8, SSM_W), F32),
                   jax.ShapeDtypeStruct((L, 2, MIX_W, 128), F32),
                   jax.ShapeDtypeStruct((L, 2, SSM_W, 128), F32),
                   jax.ShapeDtypeStruct((L, 8, 128), F32)],
        compiler_params=_cparams(("arbitrary",)),
    )(lam, b_t, gb, gc, da)


def s5_bu(l, p, b_bd, tm=512):
    T = p.shape[0]

    def body(l_ref, u_ref, b_ref, bu_ref):
        u = u_ref[...].astype(BF16)
        bu_ref[0] = _dot(u, b_ref[0])
        bu_ref[1] = _dot(u, b_ref[1])

    grid_spec = pltpu.PrefetchScalarGridSpec(
        num_scalar_prefetch=1, grid=(T // tm,),
        in_specs=[pl.BlockSpec((tm, MIX_W), lambda i, l: (i, 3)),
                  pl.BlockSpec((None, 2, MIX_W, SSM_W), lambda i, l: (l[0], 0, 0, 0))],
        out_specs=pl.BlockSpec((2, tm, SSM_W), lambda i, l: (0, i, 0)),
    )
    return pl.pallas_call(
        body, name="s5_bu", grid_spec=grid_spec,
        out_shape=jax.ShapeDtypeStruct((2, T, SSM_W), F32),
        compiler_params=_cparams(("arbitrary",)),
    )(l, p, b_bd)


def s5_scan(l, avec, xs, reverse):
    T = xs.shape[1]
    CH = SEQ_CHUNK
    nC = T // CH
    LW = 128
    n_steps = CH.bit_length() - 1

    def body(l_ref, a_ref, x_ref, s_ref):
        ar = a_ref[0:1, :]
        ai = -a_ref[1:2, :] if reverse else a_ref[1:2, :]
        pows = [(ar, ai)]
        for _ in range(n_steps - 1):
            r, i = pows[-1]
            pows.append((r * r - i * i, 2.0 * r * i))
        row = lax.broadcasted_iota(jnp.int32, (CH, LW), 0)

        def local_scan(re, im):
            for k in range(n_steps):
                d = 1 << k
                pr, pi = pows[k]
                if reverse:
                    keep = row < CH - d
                    sr, si = _shift_up(re, d), _shift_up(im, d)
                else:
                    keep = row >= d
                    sr, si = _shift_down(re, d), _shift_down(im, d)
                sr = jnp.where(keep, sr, 0.0)
                si = jnp.where(keep, si, 0.0)
                re, im = re + pr * sr - pi * si, im + pr * si + pi * sr
            return re, im

        edge = CH - 1 if reverse else 0
        pw_re, pw_im = local_scan(jnp.where(row == edge, ar, 0.0), jnp.where(row == edge, ai, 0.0))
        last = 0 if reverse else CH - 1

        def chunk(c, carry):
            cr, ci = carry
            cc = nC - 1 - c if reverse else c
            t0 = pl.multiple_of(cc * CH, CH)
            re, im = local_scan(x_ref[0, pl.ds(t0, CH), :], x_ref[1, pl.ds(t0, CH), :])
            re2 = re + pw_re * cr - pw_im * ci
            im2 = im + pw_re * ci + pw_im * cr
            s_ref[0, pl.ds(t0, CH), :] = re2
            s_ref[1, pl.ds(t0, CH), :] = im2
            return re2[last:last + 1, :], im2[last:last + 1, :]

        lax.fori_loop(0, nC, chunk, (jnp.zeros((1, LW), F32), jnp.zeros((1, LW), F32)))

    grid_spec = pltpu.PrefetchScalarGridSpec(
        num_scalar_prefetch=1, grid=(SSM_W // LW,),
        in_specs=[pl.BlockSpec((None, 8, LW), lambda g, l: (l[0], 0, g)),
                  pl.BlockSpec((2, T, LW), lambda g, l: (0, 0, g))],
        out_specs=pl.BlockSpec((2, T, LW), lambda g, l: (0, 0, g)),
    )
    return pl.pallas_call(
        body, name="s5_scan_rev" if reverse else "s5_scan_fwd", grid_spec=grid_spec,
        out_shape=jax.ShapeDtypeStruct((2, T, SSM_W), F32),
        compiler_params=_cparams(("arbitrary",)),
    )(l, avec, xs)


_GELU_C = 0.7978845608028654
_GELU_K = 0.044715


def _s5_y(u, s_ref, c_ref, d_row):
    y = _dot(s_ref[0].astype(BF16), c_ref[0]) - _dot(s_ref[1].astype(BF16), c_ref[1])
    return y + d_row * u


def s5_out(l, p, s, c_bd, ssm_d, tm=512):
    T = p.shape[0]

    def body(l_ref, u_ref, s_ref, c_ref, d_ref, yg_ref):
        y = _s5_y(u_ref[...], s_ref, c_ref, d_ref[0:1, :])
        th = jnp.tanh(_GELU_C * (y + _GELU_K * y * y * y))
        yg_ref[...] = (0.5 * y * (1.0 + th)).astype(BF16)

    grid_spec = pltpu.PrefetchScalarGridSpec(
        num_scalar_prefetch=1, grid=(T // tm,),
        in_specs=[pl.BlockSpec((tm, MIX_W), lambda i, l: (i, 3)),
                  pl.BlockSpec((2, tm, SSM_W), lambda i, l: (0, i, 0)),
                  pl.BlockSpec((None, 2, SSM_W, MIX_W), lambda i, l: (l[0], 0, 0, 0)),
                  pl.BlockSpec((None, 8, MIX_W), lambda i, l: (l[0], 0, 0))],
        out_specs=pl.BlockSpec((tm, MIX_W), lambda i, l: (i, 0)),
    )
    return pl.pallas_call(
        body, name="s5_out", grid_spec=grid_spec,
        out_shape=jax.ShapeDtypeStruct((T, MIX_W), BF16),
        compiler_params=_cparams(("arbitrary",)),
    )(l, p, s, c_bd, ssm_d)


def s5_bwd_y(l, p, s, dyg, c_bd, ssm_d, tm=512):
    T = p.shape[0]

    def body(l_ref, u_ref, s_ref, dyg_ref, c_ref, d_ref, ds_ref, du_ref, gc_ref, dd_ref):
        @pl.when(pl.program_id(0) == 0)
        def _():
            gc_ref[...] = jnp.zeros_like(gc_ref)
            dd_ref[...] = jnp.zeros_like(dd_ref)

        u = u_ref[...]
        y = _s5_y(u, s_ref, c_ref, d_ref[0:1, :])
        inner = _GELU_C * (y + _GELU_K * y * y * y)
        th = jnp.tanh(inner)
        dgelu = 0.5 * (1.0 + th) + 0.5 * y * (1.0 - th * th) * (_GELU_C * (1.0 + 3.0 * _GELU_K * y * y))
        dy = dyg_ref[...] * dgelu
        dd_ref[0:1, :] += _colsum(dy * u)
        du_ref[...] = dy * d_ref[0:1, :]
        dyb = dy.astype(BF16)
        ds_ref[0] = _dot_nt(dyb, c_ref[0])
        ds_ref[1] = -_dot_nt(dyb, c_ref[1])
        gc_ref[0] += _dot_tn(s_ref[0].astype(BF16), dyb)
        gc_ref[1] -= _dot_tn(s_ref[1].astype(BF16), dyb)

    grid_spec = pltpu.PrefetchScalarGridSpec(
        num_scalar_prefetch=1, grid=(T // tm,),
        in_specs=[pl.BlockSpec((tm, MIX_W), lambda i, l: (i, 3)),
                  pl.BlockSpec((2, tm, SSM_W), lambda i, l: (0, i, 0)),
                  pl.BlockSpec((tm, MIX_W), lambda i, l: (i, 0)),
                  pl.BlockSpec((None, 2, SSM_W, MIX_W), lambda i, l: (l[0], 0, 0, 0)),
                  pl.BlockSpec((None, 8, MIX_W), lambda i, l: (l[0], 0, 0))],
        out_specs=[pl.BlockSpec((2, tm, SSM_W), lambda i, l: (0, i, 0)),
                   pl.BlockSpec((tm, MIX_W), lambda i, l: (i, 0)),
                   pl.BlockSpec((2, SSM_W, MIX_W), lambda i, l: (0, 0, 0)),
                   pl.BlockSpec((8, MIX_W), lambda i, l: (0, 0))],
    )
    return pl.pallas_call(
        body, name="s5_bwd_y", grid_spec=grid_spec,
        out_shape=[jax.ShapeDtypeStruct((2, T, SSM_W), F32), jax.ShapeDtypeStruct((T, MIX_W), F32),
                   jax.ShapeDtypeStruct((2, SSM_W, MIX_W), F32), jax.ShapeDtypeStruct((8, MIX_W), F32)],
        compiler_params=_cparams(("arbitrary",)),
    )(l, p, s, dyg, c_bd, ssm_d)


def s5_bwd_u(l, p, lam_s, du_skip, b_bd, tm=512):
    T = p.shape[0]

    def body(l_ref, u_ref, ls_ref, dus_ref, b_ref, du_ref, gb_ref):
        @pl.when(pl.program_id(0) == 0)
        def _():
            gb_ref[...] = jnp.zeros_like(gb_ref)

        u = u_ref[...].astype(BF16)
        lr = ls_ref[0].astype(BF16)
        li = ls_ref[1].astype(BF16)
        gb_ref[0] += _dot_tn(u, lr)
        gb_ref[1] += _dot_tn(u, li)
        du_ref[...] = (dus_ref[...] + _dot_nt(lr, b_ref[0]) + _dot_nt(li, b_ref[1])).astype(BF16)

    grid_spec = pltpu.PrefetchScalarGridSpec(
        num_scalar_prefetch=1, grid=(T // tm,),
        in_specs=[pl.BlockSpec((tm, MIX_W), lambda i, l: (i, 3)),
                  pl.BlockSpec((2, tm, SSM_W), lambda i, l: (0, i, 0)),
                  pl.BlockSpec((tm, MIX_W), lambda i, l: (i, 0)),
                  pl.BlockSpec((None, 2, MIX_W, SSM_W), lambda i, l: (l[0], 0, 0, 0))],
        out_specs=[pl.BlockSpec((tm, MIX_W), lambda i, l: (i, 0)),
                   pl.BlockSpec((2, MIX_W, SSM_W), lambda i, l: (0, 0, 0))],
    )
    return pl.pallas_call(
        body, name="s5_bwd_u", grid_spec=grid_spec,
        out_shape=[jax.ShapeDtypeStruct((T, MIX_W), BF16), jax.ShapeDtypeStruct((2, MIX_W, SSM_W), F32)],
        compiler_params=_cparams(("arbitrary",)),
    )(l, p, lam_s, du_skip, b_bd)


def s5_bwd_a(s, lam_s):
    T = s.shape[1]
    nC = T // SEQ_CHUNK
    LW = 128

    def body(s_ref, ls_ref, da_ref):
        dre = jnp.zeros((1, LW), F32)
        dim = jnp.zeros((1, LW), F32)
        for c in range(nC):
            t0 = c * SEQ_CHUNK
            sr = _shift_down(_rows_with_lead(lambda a, n: s_ref[0, a:a + n, :], c, LW), 1)[HALO:]
            si = _shift_down(_rows_with_lead(lambda a, n: s_ref[1, a:a + n, :], c, LW), 1)[HALO:]
            lr = ls_ref[0, t0:t0 + SEQ_CHUNK, :]
            li = ls_ref[1, t0:t0 + SEQ_CHUNK, :]
            dre += _colsum(sr * lr + si * li)
            dim += _colsum(sr * li - si * lr)
        da_ref[...] = jnp.concatenate([dre, dim, jnp.zeros((6, LW), F32)], axis=0)

    blk = pl.BlockSpec((2, T, LW), lambda g: (0, 0, g))
    return pl.pallas_call(
        body, name="s5_bwd_a", grid=(SSM_W // LW,),
        in_specs=[blk, blk],
        out_specs=pl.BlockSpec((8, LW), lambda g: (0, g)),
        out_shape=jax.ShapeDtypeStruct((8, SSM_W), F32),
        compiler_params=_cparams(("arbitrary",)),
    )(s, lam_s)


SB_BLK = 128
SB_SCALE = SB_HEAD ** -0.5


def _split_bf16(x):
    hi = x.astype(BF16)
    return hi, (x - hi.astype(F32)).astype(BF16)


def _dot_split(x, tri):
    hi, lo = _split_bf16(x)
    return _dot(hi, tri) + _dot(lo, tri)


SB_SLABS = MIX_W // SB_BLK
SB_STACK = 2 * SB_SLABS * SB_BLK
SB_PAIR = 2 * SB_BLK


def _sb_valid(r0, c0):
    row = (lax.broadcasted_iota(jnp.int32, (SB_STACK, SB_BLK), 0) & (SB_BLK - 1)) + r0
    col = lax.broadcasted_iota(jnp.int32, (SB_STACK, SB_BLK), 1) + c0
    return col < row


def _sb_stack(ref, r0, scale):
    lane = lax.broadcasted_iota(jnp.int32, (SB_BLK, SB_BLK), 1)
    parts = []
    for s in range(SB_SLABS):
        blk = ref[pl.ds(r0, SB_BLK), s * SB_BLK:(s + 1) * SB_BLK] * scale
        parts += [jnp.where(lane < SB_HEAD, blk, 0.0), jnp.where(lane >= SB_HEAD, blk, 0.0)]
    return jnp.concatenate(parts, axis=0).astype(BF16)


def _sb_rows_nt(stack, ref, c0):
    return jnp.concatenate(
        [_dot_nt(stack[s * SB_PAIR:(s + 1) * SB_PAIR], ref[pl.ds(c0, SB_BLK), s * SB_BLK:(s + 1) * SB_BLK].astype(BF16))
         for s in range(SB_SLABS)], axis=0)


def _sb_wide(stack, s):
    return jnp.concatenate([stack[s * SB_PAIR:s * SB_PAIR + SB_BLK], stack[s * SB_PAIR + SB_BLK:(s + 1) * SB_PAIR]],
                           axis=1)


def _sb_logits(q_stack, k_ref, c0, valid):
    z = _sb_rows_nt(q_stack, k_ref, c0)
    sp = jnp.log(1.0 + jnp.exp(-jnp.abs(z)))
    ls_pos = jnp.minimum(z, 0.0) - sp
    lk = jnp.minimum(-z, 0.0) - sp
    if valid is not None:
        lk = jnp.where(valid, lk, 0.0)
    return z, ls_pos, lk


def _tri(lower):
    r = lax.broadcasted_iota(jnp.int32, (SB_BLK, SB_BLK), 0)
    c = lax.broadcasted_iota(jnp.int32, (SB_BLK, SB_BLK), 1)
    return jnp.where(r > c if lower else r < c, 1.0, 0.0).astype(BF16)


def sb_fwd(p):
    T = p.shape[0]
    W = MIX_W
    nB = T // SB_BLK

    def body(q_ref, k_ref, v_ref, o_ref, tot_ref, acc_sc):
        tri = _tri(True)

        def qblock(i, _):
            r0 = pl.multiple_of(i * SB_BLK, SB_BLK)
            q = _sb_stack(q_ref, r0, SB_SCALE)
            acc_sc[...] = jnp.zeros_like(acc_sc)

            def kblocks(c0s, run, valid):
                parts = [_sb_logits(q, k_ref, c0, valid) for c0 in c0s]
                for c0, (_, ls_pos, lk) in zip(c0s, parts):
                    a = jnp.exp(ls_pos + _dot_split(lk, tri) + run)
                    if valid is not None:
                        a = jnp.where(valid, a, 0.0)
                    a = a.astype(BF16)
                    v = _sb_stack(v_ref, c0, 1.0)
                    for s in range(SB_SLABS):
                        acc_sc[:, s * SB_BLK:(s + 1) * SB_BLK] += _dot(_sb_wide(a, s), v[s * SB_PAIR:(s + 1) * SB_PAIR])
                    run = run + jnp.sum(lk, axis=1, keepdims=True)
                return run

            def key_block(jj):
                return pl.multiple_of((i - jj) * SB_BLK, SB_BLK)

            run = kblocks([r0], jnp.zeros((SB_STACK, 1), F32), _sb_valid(0, 0))
            odd = i & 1
            run = lax.cond(odd == 1, lambda r: kblocks([key_block(1)], r, None), lambda r: r, run)
            total = lax.fori_loop(
                0, i >> 1, lambda t, r: kblocks([key_block(1 + odd + 2 * t), key_block(2 + odd + 2 * t)], r, None), run)
            o_ref[pl.ds(r0, SB_BLK), :] = acc_sc[...].astype(BF16)
            tot_ref[pl.ds(pl.multiple_of(i * SB_STACK, SB_STACK), SB_STACK), :] = jnp.broadcast_to(total, (SB_STACK, SB_BLK))
            return 0

        lax.fori_loop(0, nB, qblock, 0)

    return pl.pallas_call(
        body, name="sb_fwd", grid=(1,),
        in_specs=[pl.BlockSpec((T, W), lambda i: (0, 5)), pl.BlockSpec((T, W), lambda i: (0, 6)),
                  pl.BlockSpec((T, W), lambda i: (0, 7))],
        out_specs=[pl.BlockSpec((T, W), lambda i: (0, 0)), pl.BlockSpec((nB * SB_STACK, SB_BLK), lambda i: (0, 0))],
        out_shape=[jax.ShapeDtypeStruct((T, W), BF16), jax.ShapeDtypeStruct((nB * SB_STACK, SB_BLK), F32)],
        scratch_shapes=[pltpu.VMEM((SB_BLK, W), F32)],
        compiler_params=_cparams(("arbitrary",)),
    )(p, p, p)


def sb_bwd(p, do, tot):
    T = p.shape[0]
    W = MIX_W
    nB = T // SB_BLK

    def body(q_ref, k_ref, v_ref, do_ref, tot_ref, dqkv_ref, dq_sc, dk_sc, dv_sc):
        tri_gt = _tri(True)
        tri_lt = _tri(False)
        dq_sc[...] = jnp.zeros_like(dq_sc)
        dk_sc[...] = jnp.zeros_like(dk_sc)
        dv_sc[...] = jnp.zeros_like(dv_sc)
        zcol = jnp.zeros((SB_STACK, 1), F32)

        def qblock(i, _):
            r0 = pl.multiple_of(i * SB_BLK, SB_BLK)
            q = _sb_stack(q_ref, r0, SB_SCALE)
            dob = _sb_stack(do_ref, r0, 1.0)

            total = tot_ref[pl.ds(pl.multiple_of(i * SB_STACK, SB_STACK), SB_STACK), 0:1]

            def kblocks(c0s, carry, valid):
                pre, seen = carry
                parts = [_sb_logits(q, k_ref, c0, valid) for c0 in c0s]
                for c0, (z, ls_pos, lk) in zip(c0s, parts):
                    seen = seen + jnp.sum(lk, axis=1, keepdims=True)
                    a = jnp.exp(ls_pos + _dot_split(lk, tri_gt) + (total - seen))
                    if valid is not None:
                        a = jnp.where(valid, a, 0.0)
                    dlw = _sb_rows_nt(dob, v_ref, c0) * a
                    g = pre + _dot_split(dlw, tri_lt)
                    sig = _sigmoid(z)
                    dz = dlw * (1.0 - sig) - g * sig
                    if valid is not None:
                        dz = jnp.where(valid, dz, 0.0)
                    dz = dz.astype(BF16)
                    ab = a.astype(BF16)
                    km = _sb_stack(k_ref, c0, 1.0)
                    for s in range(SB_SLABS):
                        pair = slice(s * SB_PAIR, (s + 1) * SB_PAIR)
                        ls = slice(s * SB_BLK, (s + 1) * SB_BLK)
                        dk_sc[pl.ds(c0, SB_BLK), ls] += _dot_tn(dz[pair], q[pair])
                        dv_sc[pl.ds(c0, SB_BLK), ls] += _dot_tn(ab[pair], dob[pair])
                        dq_sc[pl.ds(r0, SB_BLK), ls] += _dot(_sb_wide(dz, s), km[pair])
                    pre = pre + jnp.sum(dlw, axis=1, keepdims=True)
                return pre, seen

            def key_block(j):
                return pl.multiple_of(j * SB_BLK, SB_BLK)

            carry = lax.fori_loop(
                0, i >> 1, lambda t, c: kblocks([key_block(2 * t), key_block(2 * t + 1)], c, None), (zcol, zcol))
            carry = lax.cond((i & 1) == 1, lambda c: kblocks([key_block(i - 1)], c, None), lambda c: c, carry)
            kblocks([r0], carry, _sb_valid(0, 0))
            return 0

        lax.fori_loop(0, nB, qblock, 0)
        dqkv_ref[:, 0:W] = (dq_sc[...] * SB_SCALE).astype(BF16)
        dqkv_ref[:, W:2 * W] = dk_sc[...].astype(BF16)
        dqkv_ref[:, 2 * W:3 * W] = dv_sc[...].astype(BF16)

    return pl.pallas_call(
        body, name="sb_bwd", grid=(1,),
        in_specs=[pl.BlockSpec((T, W), lambda i: (0, 5)), pl.BlockSpec((T, W), lambda i: (0, 6)),
                  pl.BlockSpec((T, W), lambda i: (0, 7)), pl.BlockSpec((T, W), lambda i: (0, 0)),
                  pl.BlockSpec((nB * SB_STACK, SB_BLK), lambda i: (0, 0))],
        out_specs=pl.BlockSpec((T, 3 * W), lambda i: (0, 0)),
        out_shape=jax.ShapeDtypeStruct((T, 3 * W), BF16),
        scratch_shapes=[pltpu.VMEM((T, W), F32), pltpu.VMEM((T, W), F32), pltpu.VMEM((T, W), F32)],
        compiler_params=_cparams(("arbitrary",)),
    )(p, p, p, do, tot)


def _dot_cols(a, w_ref):
    return jnp.concatenate([_dot(a, w_ref[j]) for j in range(N_DEV)], axis=1)


def _dot_cols_nt(dy, w_ref):
    n = w_ref.shape[2]
    out = _dot_nt(dy[:, 0:n], w_ref[0])
    for j in range(1, N_DEV):
        out += _dot_nt(dy[:, j * n:(j + 1) * n], w_ref[j])
    return out


def _acc_cols_tn(acc_ref, a, dy):
    n = acc_ref.shape[2]
    for j in range(N_DEV):
        acc_ref[j] += _dot_tn(a, dy[:, j * n:(j + 1) * n])


def _merge_branches(za_ref, yg_ref, z_ref, o_ref, gate_refs, wc_ref, wglu_ref, wp_ref, ws_ref):
    D = D_MODEL
    glu = _dot_cols(yg_ref[...], wglu_ref)
    glu_a, sg = glu[:, :D], _sigmoid(glu[:, D:])
    ys = [_dot_cols(za_ref[...], wc_ref), glu_a * sg, _dot_cols(z_ref[...], wp_ref), _dot_cols(o_ref[...], ws_ref)]
    gs = [_sigmoid(g[...]) for g in gate_refs]
    merged = gs[0] * ys[0] + gs[1] * ys[1] + gs[2] * ys[2] + gs[3] * ys[3]
    return ys, gs, glu_a, sg, merged


def _merge_specs(tm, D):
    W = MIX_W
    br = pl.BlockSpec((tm, W), lambda i, l: (i, 0))
    gates = [pl.BlockSpec((tm, D), functools.partial(lambda i, l, b: (i, 2 + b), b=b)) for b in range(4)]
    wsm = pl.BlockSpec((None, N_DEV, W, D // N_DEV), lambda i, l: (l[0], 0, 0, 0))
    weights = [wsm, pl.BlockSpec((None, N_DEV, W, 2 * D // N_DEV), lambda i, l: (l[0], 0, 0, 0)), wsm, wsm,
               pl.BlockSpec((None, D, D), lambda i, l: (l[0], 0, 0))]
    return [br] * 4 + gates, weights


def merge_fwd(l, p, za, yg, z, o, x, pv, wc, wglu, wp, ws, wo, tm=512):
    T, D = x.shape
    tm = min(tm, T)

    def body(l_ref, za_ref, yg_ref, z_ref, o_ref, g0, g1, g2, g3, x_ref, pv_ref,
             wc_ref, wglu_ref, wp_ref, ws_ref, wo_ref, xn_ref, m_ref):
        _, _, _, _, merged = _merge_branches(za_ref, yg_ref, z_ref, o_ref, (g0, g1, g2, g3),
                                             wc_ref, wglu_ref, wp_ref, ws_ref)
        m = _dot(merged.astype(BF16), wo_ref[...])
        m_ref[...] = m
        xn_ref[...] = _postnorm_res(x_ref[...], m, pv_ref, 1.0)

    acts, weights = _merge_specs(tm, D)
    tile = pl.BlockSpec((tm, D), lambda i, l: (i, 0))
    grid_spec = pltpu.PrefetchScalarGridSpec(
        num_scalar_prefetch=1, grid=(T // tm,),
        in_specs=acts + [tile, pl.BlockSpec((8, D), lambda i, l: (0, 0))] + weights,
        out_specs=[tile, tile],
    )
    return pl.pallas_call(
        body, name="merge_fwd", grid_spec=grid_spec,
        out_shape=[jax.ShapeDtypeStruct((T, D), F32), jax.ShapeDtypeStruct((T, D), F32)],
        compiler_params=_cparams(("arbitrary",)),
    )(l, za, yg, z, o, p, p, p, p, x, pv, wc, wglu, wp, ws, wo)


def merge_bwd(l, p, za, yg, z, o, m, dxn, pv, wc, wglu, wp, ws, wo, tm=256):
    T, D = m.shape
    W = MIX_W
    tm = min(tm, T)
    ni = T // tm

    def body(l_ref, za_ref, yg_ref, z_ref, o_ref, g0, g1, g2, g3, m_ref, dxn_ref, pv_ref,
             wc_ref, wglu_ref, wp_ref, ws_ref, wo_ref,
             dza_ref, dyg_ref, dz_ref, do_ref, dg_ref, pg_ref, gwc_ref, gwglu_ref, gwp_ref, gws_ref, gwo_ref,
             awc, awglu, awp, aws, awo):
        i = pl.program_id(0)

        @pl.when(i == 0)
        def _():
            pg_ref[...] = jnp.zeros_like(pg_ref)
            for a in (awc, awglu, awp, aws, awo):
                a[...] = jnp.zeros_like(a)

        ys, gs, glu_a, sg, merged = _merge_branches(za_ref, yg_ref, z_ref, o_ref, (g0, g1, g2, g3),
                                                    wc_ref, wglu_ref, wp_ref, ws_ref)
        dm = _postnorm_bwd(dxn_ref[...], m_ref[...], pv_ref, pg_ref, 1.0).astype(BF16)
        awo[...] += _dot_tn(merged.astype(BF16), dm)
        dmerged = _dot_nt(dm, wo_ref[...])
        for b in range(4):
            dg_ref[:, b * D:(b + 1) * D] = (dmerged * ys[b] * gs[b] * (1.0 - gs[b])).astype(BF16)
        dya = (dmerged * gs[0]).astype(BF16)
        _acc_cols_tn(awc, za_ref[...], dya)
        dza_ref[...] = _dot_cols_nt(dya, wc_ref)
        dyc = (dmerged * gs[2]).astype(BF16)
        _acc_cols_tn(awp, z_ref[...], dyc)
        dz_ref[...] = _dot_cols_nt(dyc, wp_ref)
        dyd = (dmerged * gs[3]).astype(BF16)
        _acc_cols_tn(aws, o_ref[...], dyd)
        do_ref[...] = _dot_cols_nt(dyd, ws_ref)
        dyb = dmerged * gs[1]
        dglu = jnp.concatenate([dyb * sg, dyb * glu_a * sg * (1.0 - sg)], axis=1).astype(BF16)
        _acc_cols_tn(awglu, yg_ref[...], dglu)
        dyg_ref[...] = _dot_cols_nt(dglu, wglu_ref)

        @pl.when(i == ni - 1)
        def _():
            gwc_ref[...] = awc[...].astype(BF16)
            gwglu_ref[...] = awglu[...].astype(BF16)
            gwp_ref[...] = awp[...].astype(BF16)
            gws_ref[...] = aws[...].astype(BF16)
            gwo_ref[...] = awo[...].astype(BF16)

    acts, weights = _merge_specs(tm, D)
    tile = pl.BlockSpec((tm, D), lambda i, l: (i, 0))
    br = pl.BlockSpec((tm, W), lambda i, l: (i, 0))
    full = lambda *s: pl.BlockSpec(s, lambda i, l: (0,) * len(s))
    sm, glu_s = (N_DEV, W, D // N_DEV), (N_DEV, W, 2 * D // N_DEV)
    grid_spec = pltpu.PrefetchScalarGridSpec(
        num_scalar_prefetch=1, grid=(ni,),
        in_specs=acts + [tile, tile, pl.BlockSpec((8, D), lambda i, l: (0, 0))] + weights,
        out_specs=[br, br, br, br, pl.BlockSpec((tm, 4 * D), lambda i, l: (i, 0)), full(8, D),
                   full(*sm), full(*glu_s), full(*sm), full(*sm), full(D, D)],
        scratch_shapes=[pltpu.VMEM(sm, F32), pltpu.VMEM(glu_s, F32), pltpu.VMEM(sm, F32),
                        pltpu.VMEM(sm, F32), pltpu.VMEM((D, D), F32)],
    )
    f32br = jax.ShapeDtypeStruct((T, W), F32)
    return pl.pallas_call(
        body, name="merge_bwd", grid_spec=grid_spec,
        out_shape=[f32br, f32br, f32br, f32br, jax.ShapeDtypeStruct((T, 4 * D), BF16),
                   jax.ShapeDtypeStruct((8, D), F32),
                   jax.ShapeDtypeStruct(sm, BF16), jax.ShapeDtypeStruct(glu_s, BF16),
                   jax.ShapeDtypeStruct(sm, BF16), jax.ShapeDtypeStruct(sm, BF16),
                   jax.ShapeDtypeStruct((D, D), BF16)],
        compiler_params=_cparams(("arbitrary",)),
    )(l, za, yg, z, o, p, p, p, p, m, dxn, pv, wc, wglu, wp, ws, wo)


def dp_assemble(d_conv, d_ssm, d_pool, d_qkv, d_gates, tm=512):
    T = d_conv.shape[0]
    W = MIX_W

    def body(c_ref, s_ref, p_ref, q_ref, g_ref, dp_ref):
        dp_ref[:, 0:3 * W] = c_ref[...]
        dp_ref[:, 3 * W:4 * W] = s_ref[...]
        dp_ref[:, 4 * W:5 * W] = p_ref[...]
        dp_ref[:, 5 * W:8 * W] = q_ref[...]
        dp_ref[:, GATE_OFF:] = g_ref[...]

    row = lambda w: pl.BlockSpec((tm, w), lambda i: (i, 0))
    return pl.pallas_call(
        body, name="dp_assemble", grid=(T // tm,),
        in_specs=[row(3 * W), row(W), row(W), row(3 * W), row(4 * D_MODEL)],
        out_specs=row(IN_COLS),
        out_shape=jax.ShapeDtypeStruct((T, IN_COLS), BF16),
        compiler_params=_cparams(("arbitrary",)),
    )(d_conv, d_ssm, d_pool, d_qkv, d_gates)


def loss_head(y, target, tm=512):
    T, D = y.shape

    def body(y_ref, t_ref, dy_ref, loss_ref):
        @pl.when(pl.program_id(0) == 0)
        def _():
            loss_ref[...] = jnp.zeros_like(loss_ref)

        err = y_ref[...] - t_ref[...]
        dy_ref[...] = err * (1.0 / D)
        loss_ref[...] += jnp.sum(err * err) * (0.5 / D)

    tile = pl.BlockSpec((tm, D), lambda i: (i, 0))
    return pl.pallas_call(
        body, name="loss_head", grid=(T // tm,),
        in_specs=[tile, tile],
        out_specs=[tile, pl.BlockSpec((8, 128), lambda i: (0, 0))],
        out_shape=[jax.ShapeDtypeStruct((T, D), F32), jax.ShapeDtypeStruct((8, 128), F32)],
        compiler_params=_cparams(("arbitrary",)),
    )(y, target)


def cast_layer(ld, items):
    def body(ld_ref, *refs):
        n = len(refs) // 2
        for src, dst in zip(refs[:n], refs[n:]):
            dst[...] = src[...].astype(BF16)

    def shard(w, k):
        return w.shape[1:] if k is None else w.shape[2:]

    def in_spec(w, k):
        sh = shard(w, k)
        if k is None:
            return pl.BlockSpec((None,) + sh, lambda i, ld, n=len(sh): (ld[0],) + (0,) * n)
        return pl.BlockSpec((None, None) + sh, lambda i, ld, n=len(sh), k=k: (ld[0], k) + (0,) * n)

    def out_spec(w, k):
        sh = shard(w, k)
        return pl.BlockSpec((None, None) + sh, lambda i, ld, n=len(sh): (0, ld[1]) + (0,) * n)

    grid_spec = pltpu.PrefetchScalarGridSpec(
        num_scalar_prefetch=1, grid=(1,),
        in_specs=[in_spec(w, k) for w, k in items], out_specs=[out_spec(w, k) for w, k in items])
    return pl.pallas_call(
        body, name="cast_layer", grid_spec=grid_spec,
        out_shape=[jax.ShapeDtypeStruct((1, N_DEV) + shard(w, k), BF16) for w, k in items],
        compiler_params=_cparams(("arbitrary",)),
    )(ld, *[w for w, _ in items])


def place_own(dev, a):
    def body(dev_ref, a_ref, o_ref):
        o_ref[...] = a_ref[...]

    grid_spec = pltpu.PrefetchScalarGridSpec(
        num_scalar_prefetch=1, grid=(1,),
        in_specs=[pl.BlockSpec(a.shape, lambda i, dev: (0, 0))],
        out_specs=pl.BlockSpec((None,) + a.shape, lambda i, dev: (dev[0], 0, 0)))
    return pl.pallas_call(
        body, name="place_own", grid_spec=grid_spec,
        out_shape=jax.ShapeDtypeStruct((N_DEV,) + a.shape, a.dtype),
        compiler_params=_cparams(("arbitrary",)),
    )(dev, a)


def _silu(x):
    return x * _sigmoid(x)


def ada_fwd(c_all, w_ada, b_cols):
    L, D, n = w_ada.shape

    def body(c_ref, w_ref, b_ref, o_ref):
        c_act = _silu(c_ref[...]).astype(BF16)
        o_ref[...] = _dot(c_act, w_ref[...].astype(BF16)) + b_ref[...]

    return pl.pallas_call(
        body, name="ada_fwd", grid=(L,),
        in_specs=[pl.BlockSpec((N_DEV, D), lambda l: (0, 0)), pl.BlockSpec((None, D, n), lambda l: (l, 0, 0)),
                  pl.BlockSpec((None, 1, n), lambda l: (l, 0, 0))],
        out_specs=pl.BlockSpec((None, N_DEV, n), lambda l: (l, 0, 0)),
        out_shape=jax.ShapeDtypeStruct((L, N_DEV, n), F32),
        compiler_params=_cparams(("arbitrary",)),
    )(c_all, w_ada, b_cols)


def _adamw(w, g, m, v):
    m = ADAM_B1 * m + (1.0 - ADAM_B1) * g
    v = ADAM_B2 * v + (1.0 - ADAM_B2) * (g * g)
    m_hat = m / (1.0 - ADAM_B1 ** ADAM_STEP)
    v_hat = v / (1.0 - ADAM_B2 ** ADAM_STEP)
    delta = -ADAM_LR * (m_hat / (jnp.sqrt(v_hat) + ADAM_EPS) + ADAM_WD * w)
    return delta, m, v


def ada_update(c_all, dada_cols, w, m, v, rb=256):
    L, D, n = w.shape

    def body(c_ref, d_ref, w_ref, m_ref, v_ref, g_ref, dl_ref, nm_ref, nv_ref):
        c_act = _silu(c_ref[...]).astype(BF16)
        g = _dot_tn(c_act, d_ref[...].astype(BF16))
        g_ref[...] = g
        dl_ref[...], nm_ref[...], nv_ref[...] = _adamw(w_ref[...], g, m_ref[...], v_ref[...])

    blk = pl.BlockSpec((None, rb, n), lambda l, i: (l, i, 0))
    out = jax.ShapeDtypeStruct((L, D, n), F32)
    return pl.pallas_call(
        body, name="ada_update", grid=(L, D // rb),
        in_specs=[pl.BlockSpec((N_DEV, rb), lambda l, i: (0, i)),
                  pl.BlockSpec((None, N_DEV, n), lambda l, i: (l, 0, 0)), blk, blk, blk],
        out_specs=[blk, blk, blk, blk], out_shape=[out, out, out, out],
        compiler_params=_cparams(("arbitrary", "arbitrary")),
    )(c_all, dada_cols, w, m, v)


SUM_UPDATE_RECV_BYTES = 12 * 1024 * 1024


def sum_update(dev, first, recvs, owns, w, m, v, prev=None, after=None):
    n_slots, R, C = w.shape
    S = len(recvs)
    assert len(owns) == S and first + S <= n_slots
    rb_max = SUM_UPDATE_RECV_BYTES // (S * N_DEV * C * 2)
    rb = max(r for r in range(8, R + 1, 8) if R % r == 0 and (r <= rb_max or r == 8))
    last = R // rb - 1
    n_prev = 0 if prev is None else 4
    extra = list(prev or ()) + ([] if after is None else [after])

    def body(dev_ref, *refs):
        r_refs, o_refs = refs[:S], refs[S:2 * S]
        w_ref, m_ref, v_ref = refs[2 * S:2 * S + 3]
        g_ref, dl_ref, nm_ref, nv_ref = refs[2 * S + 3 + len(extra):]
        me = dev_ref[0]
        for s in range(S):
            @pl.when(pl.program_id(0) == s)
            def _(s=s):
                g = jnp.zeros((rb, C), F32)
                for d in range(N_DEV):
                    g += jnp.where(me == d, o_refs[s][...], r_refs[s][d]).astype(F32)
                g_ref[...] = g
                dl_ref[...], nm_ref[...], nv_ref[...] = _adamw(w_ref[...], g, m_ref[...], v_ref[...])

    def row(sl, i, s):
        return jnp.where(sl == s, i, jnp.where(sl < s, 0, last))

    def rspec(s):
        return pl.BlockSpec((N_DEV, rb, C), lambda sl, i, dev: (0, row(sl, i, s), 0))

    def ospec(s):
        return pl.BlockSpec((None, rb, C), lambda sl, i, dev: (dev[0], row(sl, i, s), 0))

    blk = pl.BlockSpec((None, rb, C), lambda sl, i, dev: (first + sl, i, 0))
    out = jax.ShapeDtypeStruct((n_slots, R, C), F32)
    grid_spec = pltpu.PrefetchScalarGridSpec(
        num_scalar_prefetch=1, grid=(S, R // rb),
        in_specs=[rspec(s) for s in range(S)] + [ospec(s) for s in range(S)] + [blk, blk, blk] + [ANY] * len(extra),
        out_specs=[blk, blk, blk, blk],
    )
    n_in = 1 + 2 * S + 3
    return pl.pallas_call(
        body, name="sum_update", grid_spec=grid_spec, out_shape=[out, out, out, out],
        input_output_aliases={n_in + i: i for i in range(n_prev)},
        compiler_params=_cparams(("arbitrary", "arbitrary")),
    )(dev, *recvs, *owns, w, m, v, *extra)


def small_sum(gathered):
    _, R, C = gathered.shape

    def body(g_ref, o_ref):
        acc = g_ref[0]
        for d in range(1, N_DEV):
            acc += g_ref[d]
        o_ref[...] = acc

    return pl.pallas_call(
        body, name="small_sum", grid=(1,),
        in_specs=[pl.BlockSpec((N_DEV, R, C), lambda i: (0, 0, 0))],
        out_specs=pl.BlockSpec((R, C), lambda i: (0, 0)),
        out_shape=jax.ShapeDtypeStruct((R, C), F32),
        compiler_params=_cparams(("arbitrary",)),
    )(gathered)


def small_update(w, g, m, v):
    def body(w_ref, g_ref, m_ref, v_ref, dl_ref, nm_ref, nv_ref):
        dl_ref[...], nm_ref[...], nv_ref[...] = _adamw(w_ref[...], g_ref[...], m_ref[...], v_ref[...])

    blk = pl.BlockSpec(w.shape, lambda i: (0, 0))
    out = jax.ShapeDtypeStruct(w.shape, F32)
    return pl.pallas_call(
        body, name="small_update", grid=(1,),
        in_specs=[blk] * 4, out_specs=[blk] * 3, out_shape=[out] * 3,
        compiler_params=_cparams(("arbitrary",)),
    )(w, g, m, v)


MESH = pl.DeviceIdType.MESH
ANY = pl.BlockSpec(memory_space=pl.ANY)


def _coords():
    return lax.axis_index("x"), lax.axis_index("y"), lax.axis_index("c")


def _dev_index(x, y, c):
    return 4 * x + 2 * y + c


def _at_dev(ref, p, dev):
    return ref.at[(slice(None),) * p + (dev,)]


def all_gather(arrays, ps):
    n = len(arrays)

    def body(*refs):
        ins, outs = refs[:n], refs[n:2 * n]
        send_sems, recv_sems, local_sems = refs[2 * n:]
        x, y, c = _coords()
        me, sibling = (x, y, c), (x, y, 1 - c)
        chips = [(1 - x, y), (x, 1 - y), (1 - x, 1 - y)]

        def copy(a, k, block, to, src=None):
            dst = _at_dev(outs[a], ps[a], _dev_index(*block))
            return pltpu.make_async_remote_copy(
                src_ref=dst if src is None else src, dst_ref=dst,
                send_sem=send_sems.at[a, k], recv_sem=recv_sems.at[a, k], device_id=to, device_id_type=MESH)

        mine = [pltpu.make_async_copy(ins[a], _at_dev(outs[a], ps[a], _dev_index(*me)), local_sems.at[a])
                for a in range(n)]
        for cp in mine:
            cp.start()
        first = []
        for a in range(n):
            first.append(copy(a, 0, me, sibling, src=ins[a]))
            first += [copy(a, 1 + j, me, (*chip, c), src=ins[a]) for j, chip in enumerate(chips)]
        for cp in first:
            cp.start()
        passed = []
        for j, chip in enumerate(chips):
            for a in range(n):
                copy(a, 1 + j, (*chip, c), me).wait_recv()
                fwd = copy(a, 4 + j, (*chip, c), sibling)
                fwd.start()
                passed.append(fwd)
        for a in range(n):
            copy(a, 0, sibling, me).wait_recv()
            for j, chip in enumerate(chips):
                copy(a, 4 + j, (*chip, 1 - c), me).wait_recv()
        for cp in first + passed:
            cp.wait_send()
        for cp in mine:
            cp.wait()

    out_shape = [jax.ShapeDtypeStruct(a.shape[:p] + (N_DEV,) + a.shape[p:], a.dtype) for a, p in zip(arrays, ps)]
    return pl.pallas_call(
        body, name="all_gather", in_specs=[ANY] * n, out_specs=[ANY] * n, out_shape=out_shape,
        scratch_shapes=[pltpu.SemaphoreType.DMA((n, 7)), pltpu.SemaphoreType.DMA((n, 7)),
                        pltpu.SemaphoreType.DMA((n,))],
        compiler_params=pltpu.CompilerParams(has_side_effects=True),
    )(*arrays)


HBM = pl.BlockSpec(memory_space=pltpu.HBM)
SEM = pl.BlockSpec(memory_space=pltpu.SEMAPHORE)
EFFECT = pltpu.SideEffectType.DATAFLOW_SIDE_EFFECTING


def _peers(x, y, c):
    out = []
    for k in range(1, N_DEV):
        out.append((1 - x if k & 4 else x, 1 - y if k & 2 else y, 1 - c if k & 1 else c))
    return out


def _exchange_plan(n):
    def plan(refs, x, y, c):
        blocks, lands = refs[:n], refs[n:2 * n]
        me = _dev_index(x, y, c)
        moves = []
        for peer in _peers(x, y, c):
            q = _dev_index(*peer)
            moves += [(blocks[a].at[q], lands[a].at[me], peer, lands[a].at[q]) for a in range(n)]
        return moves
    return plan


def _gather_plan(ps, second):
    def plan(refs, x, y, c):
        me, sibling = (x, y, c), (x, y, 1 - c)
        chips = [(1 - x, y), (x, 1 - y), (1 - x, 1 - y)]
        if second:
            trips = [((*ch, c), sibling, (*ch, 1 - c)) for ch in chips]
        else:
            trips = [(me, sibling, sibling)] + [(me, (*ch, c), (*ch, c)) for ch in chips]
        moves = []
        for sent, to, arriving in trips:
            for ref, p in zip(refs, ps):
                blk = _at_dev(ref, p, _dev_index(*sent))
                moves.append((blk, blk, to, _at_dev(ref, p, _dev_index(*arriving))))
        return moves
    return plan


def copies_start(name, plan, n_moves, arrays, carry):
    n = len(arrays)

    def body(*refs):
        sems = refs[n + 1:n + 1 + 2 * n_moves]
        moves = plan(refs[:n], *_coords())
        assert len(moves) == n_moves
        for i, (src, dst, to, _) in enumerate(moves):
            pltpu.make_async_remote_copy(src_ref=src, dst_ref=dst, send_sem=sems[i], recv_sem=sems[n_moves + i],
                                         device_id=to, device_id_type=MESH).start()

    operands = [pltpu.with_memory_space_constraint(a, pltpu.HBM) for a in list(arrays) + [carry]]
    outs = pl.pallas_call(
        body, name=name,
        out_shape=[pltpu.SemaphoreType.DMA(())] * (2 * n_moves) + [pltpu.HBM(a.shape, a.dtype) for a in operands],
        in_specs=[HBM] * (n + 1), out_specs=[SEM] * (2 * n_moves) + [HBM] * (n + 1),
        input_output_aliases={i: 2 * n_moves + i for i in range(n + 1)},
        compiler_params=pltpu.CompilerParams(has_side_effects=EFFECT),
    )(*operands)
    return outs[:n_moves], outs[n_moves:2 * n_moves], outs[2 * n_moves:-1], outs[-1]


def copies_wait(name, plan, send_sems, recv_sems, arrays, after):
    n, n_moves = len(arrays), len(send_sems)

    def body(*refs):
        sems = refs[n:n + 2 * n_moves]
        for i, (src, _, to, arriving) in enumerate(plan(refs[:n], *_coords())):
            cp = pltpu.make_async_remote_copy(src_ref=src, dst_ref=arriving, send_sem=sems[i],
                                              recv_sem=sems[n_moves + i], device_id=to, device_id_type=MESH)
            cp.wait_send()
            cp.wait_recv()

    return pl.pallas_call(
        body, name=name,
        out_shape=[pltpu.HBM(a.shape, a.dtype) for a in arrays],
        in_specs=[HBM] * n + [SEM] * (2 * n_moves) + [ANY], out_specs=[HBM] * n,
        input_output_aliases={i: i for i in range(n)},
        compiler_params=pltpu.CompilerParams(has_side_effects=EFFECT),
    )(*arrays, *send_sems, *recv_sems, after)


WEIGHT_NAMES = ("w_ada", "b_ada", "g_pre", "g_post", "w_ff_in", "w_ff_out", "w_in", "conv_w", "w_conv_out",
                "lam_re", "lam_im", "log_dt", "ssm_b_re", "ssm_b_im", "ssm_c_re", "ssm_c_im", "ssm_d", "w_glu",
                "w_pool", "pool_scale", "w_pool_out", "w_sb_out", "w_out")
BIG_NAMES = ("w_ff_in", "w_ff_out", "w_in", "w_conv_out", "w_glu", "w_pool_out", "w_sb_out", "w_out")
SMALL_NAMES = ("b_ada", "g_pre", "g_post", "conv_w", "lam_re", "lam_im", "log_dt", "ssm_b_re", "ssm_b_im",
               "ssm_c_re", "ssm_c_im", "ssm_d", "w_pool", "pool_scale")
PACK_LANES = 128
PACK_ROWS = 8


def _pack(arrays):
    flat = jnp.concatenate([a.reshape(-1) for a in arrays])
    unit = PACK_LANES * PACK_ROWS
    flat = jnp.pad(flat, (0, -flat.shape[0] % unit))
    return flat.reshape(-1, PACK_LANES)


def _unpack(pack, shapes):
    flat = pack.reshape(-1)
    out, off = [], 0
    for s in shapes:
        n = 1
        for d in s:
            n *= d
        out.append(flat[off:off + n].reshape(s))
        off += n
    return out


def _pad_rows(a, rows=8):
    return jnp.pad(a, ((0, 0), (0, rows - a.shape[1]), (0, 0)))


def _tile_b(b):
    L = b.shape[0]
    return jnp.tile(b.transpose(0, 3, 1, 2).reshape(L, SSM_GROUP, SSM_W), (1, SSM_GROUPS, 1))


def _tile_c(c):
    L = c.shape[0]
    return jnp.tile(c.transpose(0, 3, 1, 2).reshape(L, SSM_STATE, MIX_W), (1, SSM_GROUPS, 1))


def _step(x, c, target, W, M, V):
    T, D = x.shape[1], x.shape[2]
    L = W["w_ada"].shape[0]
    x = x[0]
    target = target[0]
    ax, ay, ac = _coords()
    dev = _dev_index(ax, ay, ac)
    n_ada = W["w_ada"].shape[2]

    dev_s = jnp.reshape(dev, (1,)).astype(jnp.int32)
    items = ([(W["w_ff_in"], 0), (W["w_ff_in"], 1), (W["w_ff_out"], 0), (W["w_ff_out"], 1)]
             + [(W[k], None) for k in BIG_NAMES[2:]])
    bufs = [list(cast_layer(jnp.concatenate([jnp.array([l], jnp.int32), dev_s]), items)) for l in range(L)]
    ffn1_w, mixer_w, ffn2_w = (0, 2), (4, 5, 6, 7, 8, 9), (1, 3)
    all_w = tuple(range(len(items)))

    def gather_start(tag, second, l, idx, carry):
        plan = _gather_plan((1,) * len(idx), second)
        n_moves = (3 if second else 4) * len(idx)
        s_sem, r_sem, arrs, carry = copies_start(f"gather_{'b' if second else 'a'}_start_{tag}", plan, n_moves,
                                                 [bufs[l][i] for i in idx], carry)
        for i, a in zip(idx, arrs):
            bufs[l][i] = a
        return (plan, s_sem, r_sem), carry

    def gather_wait(tag, second, l, idx, flight, after):
        arrs = copies_wait(f"gather_{'b' if second else 'a'}_wait_{tag}", *flight, [bufs[l][i] for i in idx], after)
        for i, a in zip(idx, arrs):
            bufs[l][i] = a

    def gather_finish(tag, l, idx, flight, after, carry):
        gather_wait(tag, False, l, idx, flight, after)
        flight, carry = gather_start(tag, True, l, idx, carry)
        gather_wait(tag, True, l, idx, flight, carry)
        return carry

    first = []
    for g, idx in enumerate((ffn1_w, mixer_w, ffn2_w)):
        flight, x = gather_start(f"0_{g}", False, 0, idx, x)
        first.append(flight)

    gathered = all_gather([W["g_pre"], W["g_post"], W["conv_w"], c], [0, 0, 0, 0])
    g_pre = gathered[0].transpose(1, 2, 0, 3).reshape(L, N_SUB, D)
    g_post = gathered[1].transpose(1, 2, 0, 3).reshape(L, N_SUB, D)
    conv_w = _pad_rows(gathered[2].transpose(1, 2, 0, 3).reshape(L, 3, MIX_W))
    c_all = gathered[3].reshape(N_DEV, D)

    b_cols = lax.dynamic_slice_in_dim(W["b_ada"], dev * n_ada, n_ada, axis=1)[:, None, :]
    ada_cols = ada_fwd(c_all, W["w_ada"], b_cols)
    ada_all = all_gather([ada_cols], [0])[0]
    ada = lax.dynamic_index_in_dim(ada_all, dev, axis=2, keepdims=False)
    ada = ada.transpose(1, 0, 2).reshape(L, N_SUB, 3, D)
    zeros = jnp.zeros((L, N_SUB, D), F32)
    pv_all = jnp.stack([g_pre, ada[:, :, 0], ada[:, :, 1], g_post, ada[:, :, 2], zeros, zeros, zeros], axis=2)

    lam = jnp.stack([W["lam_re"].reshape(L, SSM_W), W["lam_im"].reshape(L, SSM_W),
                     jnp.repeat(W["log_dt"], SSM_STATE, axis=1)], axis=1)
    lam = _pad_rows(lam)
    b_t = jnp.stack([_tile_b(W["ssm_b_re"]), _tile_b(W["ssm_b_im"])], axis=1)
    c_t = jnp.stack([_tile_c(W["ssm_c_re"]), _tile_c(W["ssm_c_im"])], axis=1)
    avec, b_bd, c_bd = s5_params(lam, b_t, c_t)
    ssm_d = _pad_rows(W["ssm_d"][:, None, :])
    pool_scale = _pad_rows(W["pool_scale"][:, None, :])
    eye4 = jnp.eye(len(POOL_WINDOWS), dtype=F32)
    w_bd = jnp.einsum("lgcd,gh->lgchd", W["w_pool"], eye4).reshape(L, MIX_W, MIX_W).astype(BF16)

    x = gather_finish("0_0", 0, ffn1_w, first[0], pv_all, x)

    def ffn_weights(l, k):
        b = bufs[l]
        return b[k].reshape(1, 1, 2, 4, D, FF_BLK), b[2 + k].reshape(1, 1, 4, FF_BLK, D)

    def mixer_weights(l):
        b = bufs[l]
        return b[4], b[5], b[6], b[7], b[8], b[9].reshape(1, D, D)

    l0 = jnp.array([0], jnp.int32)
    k0 = jnp.array([0, 0], jnp.int32)
    saved = []
    grouped = {0: first}
    for l in range(L):
        li = jnp.array([l], jnp.int32)
        nxt = l + 1 < L and l + 1 != 1
        if l == 0 and L > 1:
            grouped[1] = []
            for g, idx in enumerate((ffn1_w, mixer_w, ffn2_w)):
                flight, x = gather_start(f"1_{g}", False, 1, idx, x)
                grouped[1].append(flight)
        if nxt:
            flight, x = gather_start(f"{l + 1}", False, l + 1, all_w, x)
        if l == 1:
            x = gather_finish("1_0", 1, ffn1_w, grouped[1][0], x, x)
        x0 = x
        ab0, f0, x1 = ffn_fwd(k0, x0, pv_all[l, 0], *ffn_weights(l, 0))
        if l in grouped:
            x1 = gather_finish(f"{l}_1", l, mixer_w, grouped[l][1], x1, x1)
        wg_in, wg_conv, wg_glu, wg_pool, wg_sb, wg_out = mixer_weights(l)
        p = mix_in_fwd(l0, x1, pv_all[l, 1], wg_in)
        za = conv_fwd(li, p, conv_w)
        z = pool_fwd(li, p, w_bd, pool_scale)
        s = s5_scan(li, avec, s5_bu(li, p, b_bd), False)
        yg = s5_out(li, p, s, c_bd, ssm_d)
        o, sb_tot = sb_fwd(p)
        x2, m = merge_fwd(l0, p, za, yg, z, o, x1, pv_all[l, 1], wg_conv, wg_glu, wg_pool, wg_sb, wg_out)
        if l in grouped:
            x2 = gather_finish(f"{l}_2", l, ffn2_w, grouped[l][2], x2, x2)
        if nxt:
            gather_wait(f"{l + 1}", False, l + 1, all_w, flight, x2)
            flight, x2 = gather_start(f"{l + 1}", True, l + 1, all_w, x2)
        ab1, f1, x = ffn_fwd(k0, x2, pv_all[l, 2], *ffn_weights(l, 1))
        if nxt:
            gather_wait(f"{l + 1}", True, l + 1, all_w, flight, x)
        saved.append((x0, ab0, f0, x1, p, za, z, s, yg, o, sb_tot, m, x2, ab1, f1))

    dx, loss_blk = loss_head(x, target)
    loss = lax.psum(loss_blk[0, 0], ("x", "y", "c"))

    n_blocks = 10
    ffn2_g, mixer_g, ffn1_g = (1, 3), (4, 5, 6, 7, 8, 9), (0, 2)
    recvs, owns, in_flight = [[None] * n_blocks for _ in range(L)], [[None] * n_blocks for _ in range(L)], []

    def exchange_start(tag, layer, idx, blocks, carry):
        n = len(idx)
        plan = _exchange_plan(n)
        arrays = list(blocks) + [lax.empty(a.shape, a.dtype) for a in blocks]
        s_sem, r_sem, arrays, carry = copies_start(f"exchange_start_{tag}", plan, (N_DEV - 1) * n, arrays, carry)
        in_flight.append((tag, layer, idx, plan, s_sem, r_sem, arrays))
        return carry

    def settle(after, upto):
        for flight in [f for f in in_flight if f[1] >= upto]:
            in_flight.remove(flight)
            tag, layer, idx, plan, s_sem, r_sem, arrays = flight
            arrays = copies_wait(f"exchange_wait_{tag}", plan, s_sem, r_sem, arrays, after)
            for j, i in enumerate(idx):
                owns[layer][i], recvs[layer][i] = arrays[j], arrays[len(idx) + j]

    pgs = [None] * L
    small = {k: [None] * L for k in ("conv_w", "w_bd", "pool_scale", "ssm_d", "gb", "gc", "da")}
    for l in reversed(range(L)):
        li = jnp.array([l], jnp.int32)
        x0, ab0, f0, x1, p, za, z, s, yg, o, sb_tot, m, x2, ab1, f1 = saved[l]
        wg_in, wg_conv, wg_glu, wg_pool, wg_sb, wg_out = mixer_weights(l)
        g_in1, g_out1, dx, pg2 = ffn_bwd(k0, dx, x2, f1, pv_all[l, 2], ab1, *ffn_weights(l, 1))
        dx = exchange_start(f"{l}_ffn2", l, ffn2_g,
                            [g_in1.reshape(N_DEV, FF_BLK, D), g_out1.reshape(N_DEV, D_FF // N_DEV, D)], dx)
        (dza, dyg, dz, do, dgates, pg1m, g_conv, g_glu, g_pool, g_sb, g_wo) = merge_bwd(
            l0, p, za, yg, z, o, m, dx, pv_all[l, 1], wg_conv, wg_glu, wg_pool, wg_sb, wg_out)
        d_conv, small["conv_w"][l] = conv_bwd(li, p, dza, conv_w)
        d_pool, small["w_bd"][l], small["pool_scale"][l] = pool_bwd(li, p, dz, w_bd, pool_scale)
        ds, du_skip, small["gc"][l], small["ssm_d"][l] = s5_bwd_y(li, p, s, dyg, c_bd, ssm_d)
        lam_s = s5_scan(li, avec, ds, True)
        d_ssm, small["gb"][l] = s5_bwd_u(li, p, lam_s, du_skip, b_bd)
        small["da"][l] = s5_bwd_a(s, lam_s)
        d_qkv = sb_bwd(p, do, sb_tot)
        dp = dp_assemble(d_conv, d_ssm, d_pool, d_qkv, dgates)
        dx, g_win, pg1i = mix_in_bwd(l0, dp, dx, x1, pv_all[l, 1], wg_in)
        settle(dx, l + 1)
        dx = exchange_start(f"{l}_mixer", l, mixer_g,
                            [g_win, g_conv, g_glu, g_pool, g_sb, g_wo.reshape(N_DEV, D // N_DEV, D)], dx)
        g_in0, g_out0, dx, pg0 = ffn_bwd(k0, dx, x0, f0, pv_all[l, 0], ab0, *ffn_weights(l, 0))
        pgs[l] = jnp.stack([pg0, pg1m + pg1i, pg2])
        last_ffn1 = [g_in0.reshape(N_DEV, FF_BLK, D), g_out0.reshape(N_DEV, D_FF // N_DEV, D)]
        if l > 0:
            dx = exchange_start(f"{l}_ffn1", l, ffn1_g, last_ffn1, dx)

    dlam, db_t, dc_t, dldt = s5_params_bwd(lam, b_t, jnp.stack(small["gb"]), jnp.stack(small["gc"]),
                                           jnp.stack(small["da"]))
    pg = jnp.stack(pgs)
    d_ada = jnp.stack([pg[:, :, PV_SHIFT], pg[:, :, PV_SCALE], pg[:, :, PV_GATE]], axis=2).reshape(L, N_SUB * 3 * D)
    db = db_t[..., :SSM_STATE].reshape(L, 2, SSM_GROUPS, SSM_GROUP, SSM_STATE).transpose(0, 1, 2, 4, 3)
    dc = dc_t[..., :SSM_GROUP].reshape(L, 2, SSM_GROUPS, SSM_STATE, SSM_GROUP).transpose(0, 1, 2, 4, 3)
    d_wpool = jnp.einsum("lgcgd->lgcd", jnp.stack(small["w_bd"]).reshape(L, 4, 64, 4, 64))
    contrib = {
        "b_ada": d_ada, "g_pre": pg[:, :, PV_GPRE], "g_post": pg[:, :, PV_GPOST],
        "conv_w": jnp.stack(small["conv_w"])[:, :3], "lam_re": dlam[:, 0].reshape(L, SSM_GROUPS, SSM_STATE),
        "lam_im": dlam[:, 1].reshape(L, SSM_GROUPS, SSM_STATE), "log_dt": dldt[:, 2, :SSM_GROUPS],
        "ssm_b_re": db[:, 0], "ssm_b_im": db[:, 1], "ssm_c_re": dc[:, 0], "ssm_c_im": dc[:, 1],
        "ssm_d": jnp.stack(small["ssm_d"])[:, 0], "w_pool": d_wpool,
        "pool_scale": jnp.stack(small["pool_scale"])[:, 0],
    }
    contrib_shapes = [contrib[k].shape for k in SMALL_NAMES]

    big_idx = {"w_ff_in": (0, 1), "w_ff_out": (2, 3), "w_in": (4,), "w_conv_out": (5,), "w_glu": (6,),
               "w_pool_out": (7,), "w_sb_out": (8,), "w_out": (9,)}

    def slots(name, a):
        a = a.reshape((-1,) + a.shape[-2:])
        return a.swapaxes(1, 2) if name == "w_ff_in" else a

    def unslots(name, a):
        a = a.swapaxes(1, 2) if name == "w_ff_in" else a
        return a.reshape(W[name].shape)

    def big(name, layers, prev, after=None):
        idx = big_idx[name]
        return sum_update(dev_s, layers[0] * len(idx), [recvs[l][i] for l in layers for i in idx],
                          [owns[l][i] for l in layers for i in idx],
                          slots(name, W[name]), slots(name, M[name]), slots(name, V[name]), prev, after)

    partial = {}

    def partial_updates(names, after):
        for name in names:
            if L > 1:
                partial[name] = big(name, list(range(1, L)), None, after)
                after = partial[name][0]
        return after

    pack_buf = [place_own(dev_s, _pack([contrib[k] for k in SMALL_NAMES]))]
    plan_a, plan_b = _gather_plan((0,), False), _gather_plan((0,), True)
    s_sem, r_sem, pack_buf, dx = copies_start("small_gather_a_start", plan_a, 4, pack_buf, dx)
    dx = exchange_start("0_ffn1", 0, ffn1_g, last_ffn1, dx)
    after = partial_updates(BIG_NAMES[:1], dx)
    pack_buf = copies_wait("small_gather_a_wait", plan_a, s_sem, r_sem, pack_buf, after)
    s_sem, r_sem, pack_buf, dx = copies_start("small_gather_b_start", plan_b, 3, pack_buf, dx)
    after = partial_updates(BIG_NAMES[1:], dx)
    pack_all = copies_wait("small_gather_b_wait", plan_b, s_sem, r_sem, pack_buf, after)[0]
    total = dict(zip(SMALL_NAMES, _unpack(small_sum(pack_all), contrib_shapes)))
    d_ada_all = pack_all.reshape(N_DEV, -1)[:, :L * N_SUB * 3 * D].reshape(N_DEV, L, N_SUB * 3 * D)
    dada_cols = lax.dynamic_slice_in_dim(d_ada_all, dev * n_ada, n_ada, axis=2).transpose(1, 0, 2)
    n_g = D // N_DEV
    grads = {}
    for k in SMALL_NAMES:
        g = total[k]
        if k in ("g_pre", "g_post"):
            g = lax.dynamic_slice_in_dim(g, dev * n_g, n_g, axis=2)
        elif k == "conv_w":
            g = lax.dynamic_slice_in_dim(g, dev * (MIX_W // N_DEV), MIX_W // N_DEV, axis=2)
        grads[k] = g

    delta, new_m, new_v = {}, {}, {}
    shapes = [W[k].shape for k in SMALL_NAMES]
    dl, nm, nv = small_update(_pack([W[k] for k in SMALL_NAMES]), _pack([grads[k] for k in SMALL_NAMES]),
                              _pack([M[k] for k in SMALL_NAMES]), _pack([V[k] for k in SMALL_NAMES]))
    for k, a, b, cc in zip(SMALL_NAMES, _unpack(dl, shapes), _unpack(nm, shapes), _unpack(nv, shapes)):
        delta[k], new_m[k], new_v[k] = a, b, cc
    grads["w_ada"], delta["w_ada"], new_m["w_ada"], new_v["w_ada"] = ada_update(
        c_all, dada_cols, W["w_ada"], M["w_ada"], V["w_ada"])

    settle(new_m["w_ada"], 0)
    for name in BIG_NAMES:
        outs = big(name, [0], partial.get(name))
        grads[name], delta[name], new_m[name], new_v[name] = [unslots(name, o) for o in outs]

    return (loss, dx[None], *[grads[k] for k in WEIGHT_NAMES], *[delta[k] for k in WEIGHT_NAMES],
            *[new_m[k] for k in WEIGHT_NAMES], *[new_v[k] for k in WEIGHT_NAMES])


def kernel(x, c, w_ada, b_ada, g_pre, g_post, w_ff_in, w_ff_out, w_in, conv_w, w_conv_out, lam_re, lam_im, log_dt, ssm_b_re, ssm_b_im, ssm_c_re, ssm_c_im, ssm_d, w_glu, w_pool, pool_scale, w_pool_out, w_sb_out, w_out, loss_target, m_w_ada, m_b_ada, m_g_pre, m_g_post, m_w_ff_in, m_w_ff_out, m_w_in, m_conv_w, m_w_conv_out, m_lam_re, m_lam_im, m_log_dt, m_ssm_b_re, m_ssm_b_im, m_ssm_c_re, m_ssm_c_im, m_ssm_d, m_w_glu, m_w_pool, m_pool_scale, m_w_pool_out, m_w_sb_out, m_w_out, v_w_ada, v_b_ada, v_g_pre, v_g_post, v_w_ff_in, v_w_ff_out, v_w_in, v_conv_w, v_w_conv_out, v_lam_re, v_lam_im, v_log_dt, v_ssm_b_re, v_ssm_b_im, v_ssm_c_re, v_ssm_c_im, v_ssm_d, v_w_glu, v_w_pool, v_pool_scale, v_w_pool_out, v_w_sb_out, v_w_out):
    w = (w_ada, b_ada, g_pre, g_post, w_ff_in, w_ff_out, w_in, conv_w, w_conv_out, lam_re, lam_im, log_dt, ssm_b_re, ssm_b_im, ssm_c_re, ssm_c_im, ssm_d, w_glu, w_pool, pool_scale, w_pool_out, w_sb_out, w_out)
    m = (m_w_ada, m_b_ada, m_g_pre, m_g_post, m_w_ff_in, m_w_ff_out, m_w_in, m_conv_w, m_w_conv_out, m_lam_re, m_lam_im, m_log_dt, m_ssm_b_re, m_ssm_b_im, m_ssm_c_re, m_ssm_c_im, m_ssm_d, m_w_glu, m_w_pool, m_pool_scale, m_w_pool_out, m_w_sb_out, m_w_out)
    v = (v_w_ada, v_b_ada, v_g_pre, v_g_post, v_w_ff_in, v_w_ff_out, v_w_in, v_conv_w, v_w_conv_out, v_lam_re, v_lam_im, v_log_dt, v_ssm_b_re, v_ssm_b_im, v_ssm_c_re, v_ssm_c_im, v_ssm_d, v_w_glu, v_w_pool, v_pool_scale, v_w_pool_out, v_w_sb_out, v_w_out)
    return _step(x, c, loss_target, dict(zip(WEIGHT_NAMES, w)), dict(zip(WEIGHT_NAMES, m)), dict(zip(WEIGHT_NAMES, v)))
```

```python
import functools

import jax
import jax.numpy as jnp
from jax import lax
from jax.experimental import pallas as pl
from jax.experimental.pallas import tpu as pltpu

F32 = jnp.float32
BF16 = jnp.bfloat16

N_DEV = 8
D_MODEL = 1024
D_FF = 2816
FF_BLK = D_FF // 4
N_SUB = 3
MIX_W = 256
IN_COLS = 6144
IN_BLK = IN_COLS // N_DEV
GATE_OFF = 2048
SSM_GROUPS, SSM_GROUP, SSM_STATE = 16, 16, 64
SSM_W = SSM_GROUPS * SSM_STATE
POOL_WINDOWS = (2, 4, 8, 16)
SB_HEAD = 64
EPS = 1e-6
DT_LAMBDA_RE_MAX = -1e-4
ADAM_LR, ADAM_B1, ADAM_B2, ADAM_EPS, ADAM_WD, ADAM_STEP = 0.001, 0.9, 0.999, 1e-08, 0.01, 10

VMEM_LIMIT = 56 * 1024 * 1024
FFN_BWD_VMEM_LIMIT = 60 * 1024 * 1024

PV_GPRE, PV_SHIFT, PV_SCALE, PV_GPOST, PV_GATE = 0, 1, 2, 3, 4


def _cparams(sem):
    return pltpu.CompilerParams(dimension_semantics=sem, vmem_limit_bytes=VMEM_LIMIT)


def _dot(a, b):
    return jnp.dot(a, b, preferred_element_type=F32)


def _dot_nt(a, b):
    return lax.dot_general(a, b, (((1,), (1,)), ((), ())), preferred_element_type=F32)


def _dot_tn(a, b):
    return lax.dot_general(a, b, (((0,), (0,)), ((), ())), preferred_element_type=F32)


def _rms(x):
    r = lax.rsqrt(jnp.mean(x * x, axis=-1, keepdims=True) + EPS)
    return x * r, r


def _rms_bwd(dn, n, r):
    return r * (dn - n * jnp.mean(dn * n, axis=-1, keepdims=True))


def _sigmoid(x):
    return 1.0 / (1.0 + jnp.exp(-x))


def _colsum(x):
    return jnp.sum(x, axis=0, keepdims=True)


def _prenorm(x, pv_ref):
    n, r = _rms(x)
    hn = n * pv_ref[PV_GPRE:PV_GPRE + 1, :]
    h = hn * (1.0 + pv_ref[PV_SCALE:PV_SCALE + 1, :]) + pv_ref[PV_SHIFT:PV_SHIFT + 1, :]
    return h, n, r, hn


def _prenorm_bwd(dh, dxn, x, pv_ref, pg_ref):
    _, n, r, hn = _prenorm(x, pv_ref)
    pg_ref[PV_SHIFT:PV_SHIFT + 1, :] += _colsum(dh)
    pg_ref[PV_SCALE:PV_SCALE + 1, :] += _colsum(dh * hn)
    dhn = dh * (1.0 + pv_ref[PV_SCALE:PV_SCALE + 1, :])
    pg_ref[PV_GPRE:PV_GPRE + 1, :] += _colsum(dhn * n)
    dn = dhn * pv_ref[PV_GPRE:PV_GPRE + 1, :]
    return dxn + _rms_bwd(dn, n, r)


def _postnorm_res(x, f, pv_ref, coef):
    nf, _ = _rms(f)
    return x + (coef * (1.0 + pv_ref[PV_GATE:PV_GATE + 1, :])) * (nf * pv_ref[PV_GPOST:PV_GPOST + 1, :])


def _postnorm_bwd(dxn, f, pv_ref, pg_ref, coef):
    nf, rf = _rms(f)
    g_post = pv_ref[PV_GPOST:PV_GPOST + 1, :]
    pg_ref[PV_GATE:PV_GATE + 1, :] += _colsum(dxn * (nf * g_post)) * coef
    dnfg = dxn * (coef * (1.0 + pv_ref[PV_GATE:PV_GATE + 1, :]))
    pg_ref[PV_GPOST:PV_GPOST + 1, :] += _colsum(dnfg * nf)
    return _rms_bwd(dnfg * g_post, nf, rf)


def ffn_fwd(lk, x, pv, wg_in, wg_out, tm=512):
    T, D = x.shape
    tm = min(tm, T)
    nj, ni = 4, T // tm

    def body(lk_ref, x_ref, pv_ref, win_ref, wout_ref, ab_ref, f_ref, xn_ref, h_all, acc_all):
        j, i = pl.program_id(0), pl.program_id(1)
        rows = pl.ds(pl.multiple_of(i * tm, tm), tm)

        @pl.when(j == 0)
        def _():
            h, _, _, _ = _prenorm(x_ref[...], pv_ref)
            h_all[rows, :] = h.astype(BF16)
            acc_all[rows, :] = jnp.zeros((tm, D), F32)

        h = h_all[rows, :]
        a = _dot(h, win_ref[0])
        b = _dot(h, win_ref[1])
        ab_ref[0] = a.astype(BF16)
        ab_ref[1] = b.astype(BF16)
        act = (a * _sigmoid(a) * b).astype(BF16)
        acc_all[rows, :] += _dot(act, wout_ref[...])

        @pl.when(j == nj - 1)
        def _():
            f = acc_all[rows, :]
            f_ref[...] = f
            xn_ref[...] = _postnorm_res(x_ref[...], f, pv_ref, 0.5)

    ends = lambda j, i, lk: (jnp.where((j == 0) | (j == nj - 1), i, ni - 1), 0)
    last = lambda j, i, lk: (jnp.where(j == nj - 1, i, 0), 0)
    grid_spec = pltpu.PrefetchScalarGridSpec(
        num_scalar_prefetch=1, grid=(nj, ni),
        in_specs=[
            pl.BlockSpec((tm, D), ends),
            pl.BlockSpec((8, D), lambda j, i, lk: (0, 0)),
            pl.BlockSpec((None, None, 2, None, D, FF_BLK), lambda j, i, lk: (lk[0], lk[1], 0, j, 0, 0)),
            pl.BlockSpec((None, None, None, FF_BLK, D), lambda j, i, lk: (lk[0], lk[1], j, 0, 0)),
        ],
        out_specs=[
            pl.BlockSpec((2, None, tm, FF_BLK), lambda j, i, lk: (0, j, i, 0)),
            pl.BlockSpec((tm, D), last),
            pl.BlockSpec((tm, D), last),
        ],
        scratch_shapes=[pltpu.VMEM((T, D), BF16), pltpu.VMEM((T, D), F32)],
    )
    return pl.pallas_call(
        body, name="ffn_fwd", grid_spec=grid_spec,
        out_shape=[jax.ShapeDtypeStruct((2, nj, T, FF_BLK), BF16),
                   jax.ShapeDtypeStruct((T, D), F32), jax.ShapeDtypeStruct((T, D), F32)],
        compiler_params=_cparams(("arbitrary", "arbitrary")),
    )(lk, x, pv, wg_in, wg_out)


def ffn_bwd(lk, dxn, x, f, pv, ab, wg_in, wg_out, tm=256):
    T, D = x.shape
    tm = min(tm, T)
    nj, ni = 4, T // tm

    def body(lk_ref, dxn_ref, x_ref, f_ref, pv_ref, ab_ref, win_ref, wout_ref,
             gin_ref, gout_ref, dx_ref, pg_ref, df_all, h_all, dh_all, acc_in, acc_out):
        j, i = pl.program_id(0), pl.program_id(1)
        rows = pl.ds(pl.multiple_of(i * tm, tm), tm)

        @pl.when((i == 0) & (j == 0))
        def _():
            pg_ref[...] = jnp.zeros_like(pg_ref)

        @pl.when(j == 0)
        def _():
            df = _postnorm_bwd(dxn_ref[...], f_ref[...], pv_ref, pg_ref, 0.5)
            df_all[rows, :] = df.astype(BF16)
            h, _, _, _ = _prenorm(x_ref[...], pv_ref)
            h_all[rows, :] = h.astype(BF16)
            dh_all[rows, :] = jnp.zeros((tm, D), F32)

        @pl.when(i == 0)
        def _():
            acc_in[...] = jnp.zeros_like(acc_in)
            acc_out[...] = jnp.zeros_like(acc_out)

        df_t = df_all[rows, :]
        h_t = h_all[rows, :]
        dact = _dot_nt(df_t, wout_ref[...])
        a = ab_ref[0].astype(F32)
        b = ab_ref[1].astype(F32)
        sig = _sigmoid(a)
        s = a * sig
        da = (dact * b * (sig * (1.0 + a * (1.0 - sig)))).astype(BF16)
        db = (dact * s).astype(BF16)
        act = (s * b).astype(BF16)
        dh_all[rows, :] += _dot_nt(da, win_ref[0]) + _dot_nt(db, win_ref[1])
        acc_in[0] += _dot_tn(da, h_t)
        acc_in[1] += _dot_tn(db, h_t)
        acc_out[...] += _dot_tn(act, df_t)

        @pl.when(i == ni - 1)
        def _():
            gin_ref[...] = acc_in[...].astype(BF16)
            gout_ref[...] = acc_out[...].astype(BF16)

        @pl.when(j == nj - 1)
        def _():
            dx_ref[...] = _prenorm_bwd(dh_all[rows, :], dxn_ref[...], x_ref[...], pv_ref, pg_ref)

    ends = lambda j, i, lk: (jnp.where((j == 0) | (j == nj - 1), i, ni - 1), 0)
    first = lambda j, i, lk: (jnp.where(j == 0, i, ni - 1), 0)
    grid_spec = pltpu.PrefetchScalarGridSpec(
        num_scalar_prefetch=1, grid=(nj, ni),
        in_specs=[pl.BlockSpec((tm, D), ends), pl.BlockSpec((tm, D), ends), pl.BlockSpec((tm, D), first),
                  pl.BlockSpec((8, D), lambda j, i, lk: (0, 0)),
                  pl.BlockSpec((2, None, tm, FF_BLK), lambda j, i, lk: (0, j, i, 0)),
                  pl.BlockSpec((None, None, 2, None, D, FF_BLK), lambda j, i, lk: (lk[0], lk[1], 0, j, 0, 0)),
                  pl.BlockSpec((None, None, None, FF_BLK, D), lambda j, i, lk: (lk[0], lk[1], j, 0, 0))],
        out_specs=[pl.BlockSpec((2, None, FF_BLK, D), lambda j, i, lk: (0, j, 0, 0)),
                   pl.BlockSpec((None, FF_BLK, D), lambda j, i, lk: (j, 0, 0)),
                   pl.BlockSpec((tm, D), lambda j, i, lk: (jnp.where(j == nj - 1, i, 0), 0)),
                   pl.BlockSpec((8, D), lambda j, i, lk: (0, 0))],
        scratch_shapes=[pltpu.VMEM((T, D), BF16), pltpu.VMEM((T, D), BF16), pltpu.VMEM((T, D), F32),
                        pltpu.VMEM((2, FF_BLK, D), F32), pltpu.VMEM((FF_BLK, D), F32)],
    )
    return pl.pallas_call(
        body, name="ffn_bwd", grid_spec=grid_spec,
        out_shape=[jax.ShapeDtypeStruct((2, nj, FF_BLK, D), BF16), jax.ShapeDtypeStruct((nj, FF_BLK, D), BF16),
                   jax.ShapeDtypeStruct((T, D), F32), jax.ShapeDtypeStruct((8, D), F32)],
        compiler_params=pltpu.CompilerParams(dimension_semantics=("arbitrary", "arbitrary"),
                                             vmem_limit_bytes=FFN_BWD_VMEM_LIMIT),
    )(lk, dxn, x, f, pv, ab, wg_in, wg_out)


def mix_in_fwd(l, x, pv, wg, tm=1024):
    T, D = x.shape
    tm = min(tm, T)

    def body(l_ref, x_ref, pv_ref, w_ref, p_ref, h_sc):
        @pl.when(pl.program_id(1) == 0)
        def _():
            h, _, _, _ = _prenorm(x_ref[...], pv_ref)
            h_sc[...] = h.astype(BF16)

        p_ref[...] = _dot(h_sc[...], w_ref[...])

    grid_spec = pltpu.PrefetchScalarGridSpec(
        num_scalar_prefetch=1, grid=(T // tm, N_DEV),
        in_specs=[pl.BlockSpec((tm, D), lambda i, j, l: (i, 0)),
                  pl.BlockSpec((8, D), lambda i, j, l: (0, 0)),
                  pl.BlockSpec((None, None, D, IN_BLK), lambda i, j, l: (l[0], j, 0, 0))],
        out_specs=pl.BlockSpec((tm, IN_BLK), lambda i, j, l: (i, j)),
        scratch_shapes=[pltpu.VMEM((tm, D), BF16)],
    )
    return pl.pallas_call(
        body, name="mix_in_fwd", grid_spec=grid_spec,
        out_shape=jax.ShapeDtypeStruct((T, IN_COLS), F32),
        compiler_params=_cparams(("arbitrary", "arbitrary")),
    )(l, x, pv, wg)


def mix_in_bwd(l, dp, dxn, x, pv, wg, tm=512):
    T, D = x.shape
    tm = min(tm, T)
    nj, ni = N_DEV, T // tm

    def body(l_ref, dp_ref, dxn_ref, x_ref, pv_ref, w_ref, dx_ref, gw_ref, pg_ref, h_all, dh_all, acc):
        j, i = pl.program_id(0), pl.program_id(1)
        rows = pl.ds(pl.multiple_of(i * tm, tm), tm)

        @pl.when((i == 0) & (j == 0))
        def _():
            pg_ref[...] = jnp.zeros_like(pg_ref)

        @pl.when(j == 0)
        def _():
            h, _, _, _ = _prenorm(x_ref[...], pv_ref)
            h_all[rows, :] = h.astype(BF16)
            dh_all[rows, :] = jnp.zeros((tm, D), F32)

        @pl.when(i == 0)
        def _():
            acc[...] = jnp.zeros_like(acc)

        dp_t = dp_ref[...]
        dh_all[rows, :] += _dot_nt(dp_t, w_ref[...])
        acc[...] += _dot_tn(h_all[rows, :], dp_t)

        @pl.when(i == ni - 1)
        def _():
            gw_ref[...] = acc[...].astype(BF16)

        @pl.when(j == nj - 1)
        def _():
            dx_ref[...] = _prenorm_bwd(dh_all[rows, :], dxn_ref[...], x_ref[...], pv_ref, pg_ref)

    ends = lambda j, i, l: (jnp.where((j == 0) | (j == nj - 1), i, ni - 1), 0)
    last = lambda j, i, l: (jnp.where(j == nj - 1, i, 0), 0)
    grid_spec = pltpu.PrefetchScalarGridSpec(
        num_scalar_prefetch=1, grid=(nj, ni),
        in_specs=[pl.BlockSpec((tm, IN_BLK), lambda j, i, l: (i, j)),
                  pl.BlockSpec((tm, D), last), pl.BlockSpec((tm, D), ends),
                  pl.BlockSpec((8, D), lambda j, i, l: (0, 0)),
                  pl.BlockSpec((None, None, D, IN_BLK), lambda j, i, l: (l[0], j, 0, 0))],
        out_specs=[pl.BlockSpec((tm, D), last), pl.BlockSpec((None, D, IN_BLK), lambda j, i, l: (j, 0, 0)),
                   pl.BlockSpec((8, D), lambda j, i, l: (0, 0))],
        scratch_shapes=[pltpu.VMEM((T, D), BF16), pltpu.VMEM((T, D), F32), pltpu.VMEM((D, IN_BLK), F32)],
    )
    return pl.pallas_call(
        body, name="mix_in_bwd", grid_spec=grid_spec,
        out_shape=[jax.ShapeDtypeStruct((T, D), F32), jax.ShapeDtypeStruct((nj, D, IN_BLK), BF16),
                   jax.ShapeDtypeStruct((8, D), F32)],
        compiler_params=_cparams(("arbitrary", "arbitrary")),
    )(l, dp, dxn, x, pv, wg)


SEQ_CHUNK = 256
HALO = 16


def _shift_down(ext, d):
    return pltpu.roll(ext, d, 0)


def _shift_up(ext, d):
    return pltpu.roll(ext, ext.shape[0] - d, 0)


def _rows_with_lead(load, c, width):
    t0 = c * SEQ_CHUNK
    if c == 0:
        return jnp.concatenate([jnp.zeros((HALO, width), F32), load(0, SEQ_CHUNK)], axis=0)
    return load(t0 - HALO, SEQ_CHUNK + HALO)


def _rows_with_tail(load, c, n_chunks, width):
    t0 = c * SEQ_CHUNK
    if c == n_chunks - 1:
        return jnp.concatenate([load(t0, SEQ_CHUNK), jnp.zeros((HALO, width), F32)], axis=0)
    return load(t0, SEQ_CHUNK + HALO)


def conv_fwd(l, p, conv_w):
    T = p.shape[0]
    W = MIX_W
    nC = T // SEQ_CHUNK

    def body(l_ref, p_ref, w_ref, za_ref):
        w0, w1, w2 = w_ref[0:1, :], w_ref[1:2, :], w_ref[2:3, :]
        for c in range(nC):
            ext = _rows_with_lead(lambda s, n: p_ref[s:s + n, W:2 * W] * p_ref[s:s + n, 2 * W:3 * W], c, W)
            y = w2 * ext + w1 * _shift_down(ext, 1) + w0 * _shift_down(ext, 2)
            t0 = c * SEQ_CHUNK
            za_ref[t0:t0 + SEQ_CHUNK, :] = (p_ref[t0:t0 + SEQ_CHUNK, 0:W] * y[HALO:]).astype(BF16)

    grid_spec = pltpu.PrefetchScalarGridSpec(
        num_scalar_prefetch=1, grid=(1,),
        in_specs=[pl.BlockSpec((T, 3 * W), lambda i, l: (0, 0)),
                  pl.BlockSpec((None, 8, W), lambda i, l: (l[0], 0, 0))],
        out_specs=pl.BlockSpec((T, W), lambda i, l: (0, 0)),
    )
    return pl.pallas_call(
        body, name="conv_fwd", grid_spec=grid_spec,
        out_shape=jax.ShapeDtypeStruct((T, W), BF16),
        compiler_params=_cparams(("arbitrary",)),
    )(l, p, conv_w)


def conv_bwd(l, p, dza, conv_w):
    T = p.shape[0]
    W = MIX_W
    nC = T // SEQ_CHUNK

    def body(l_ref, p_ref, dza_ref, w_ref, dp_ref, dw_ref):
        w0, w1, w2 = w_ref[0:1, :], w_ref[1:2, :], w_ref[2:3, :]
        dw = [jnp.zeros((1, W), F32) for _ in range(3)]
        for c in range(nC):
            t0 = c * SEQ_CHUNK
            ext = _rows_with_lead(lambda s, n: p_ref[s:s + n, W:2 * W] * p_ref[s:s + n, 2 * W:3 * W], c, W)
            u1, u2 = _shift_down(ext, 1)[HALO:], _shift_down(ext, 2)[HALO:]
            u0 = ext[HALO:]
            y = w2 * u0 + w1 * u1 + w0 * u2
            dza_c = dza_ref[t0:t0 + SEQ_CHUNK, :]
            dy = dza_c * p_ref[t0:t0 + SEQ_CHUNK, 0:W]
            dw[0] += _colsum(dy * u2)
            dw[1] += _colsum(dy * u1)
            dw[2] += _colsum(dy * u0)
            dye = _rows_with_tail(lambda s, n: dza_ref[s:s + n, :] * p_ref[s:s + n, 0:W], c, nC, W)
            du = (w2 * dye + w1 * _shift_up(dye, 1) + w0 * _shift_up(dye, 2))[:SEQ_CHUNK]
            dp_ref[t0:t0 + SEQ_CHUNK, 0:W] = (dza_c * y).astype(BF16)
            dp_ref[t0:t0 + SEQ_CHUNK, W:2 * W] = (du * p_ref[t0:t0 + SEQ_CHUNK, 2 * W:3 * W]).astype(BF16)
            dp_ref[t0:t0 + SEQ_CHUNK, 2 * W:3 * W] = (du * p_ref[t0:t0 + SEQ_CHUNK, W:2 * W]).astype(BF16)
        dw_ref[...] = jnp.concatenate(dw + [jnp.zeros((5, W), F32)], axis=0)

    grid_spec = pltpu.PrefetchScalarGridSpec(
        num_scalar_prefetch=1, grid=(1,),
        in_specs=[pl.BlockSpec((T, 3 * W), lambda i, l: (0, 0)),
                  pl.BlockSpec((T, W), lambda i, l: (0, 0)),
                  pl.BlockSpec((None, 8, W), lambda i, l: (l[0], 0, 0))],
        out_specs=[pl.BlockSpec((T, 3 * W), lambda i, l: (0, 0)), pl.BlockSpec((8, W), lambda i, l: (0, 0))],
    )
    return pl.pallas_call(
        body, name="conv_bwd", grid_spec=grid_spec,
        out_shape=[jax.ShapeDtypeStruct((T, 3 * W), BF16), jax.ShapeDtypeStruct((8, W), F32)],
        compiler_params=_cparams(("arbitrary",)),
    )(l, p, dza, conv_w)


def _pool_consts(rows, t0):
    lane = lax.broadcasted_iota(jnp.int32, (rows, MIX_W), 1)
    t = lax.broadcasted_iota(jnp.int32, (rows, MIX_W), 0) + t0
    win = jnp.where(lane < 64, 2, jnp.where(lane < 128, 4, jnp.where(lane < 192, 8, 16)))
    inv = 1.0 / jnp.minimum(t + 1, win).astype(F32)
    return lane, inv


def _pick_window(lane, s2, s4, s8, s16):
    return jnp.where(lane < 64, s2, jnp.where(lane < 128, s4, jnp.where(lane < 192, s8, s16)))


def _pooled_chunk(u_ref, c):
    ext = _rows_with_lead(lambda s, n: u_ref[s:s + n, :], c, MIX_W)
    s2 = ext + _shift_down(ext, 1)
    s4 = s2 + _shift_down(s2, 2)
    s8 = s4 + _shift_down(s4, 4)
    s16 = s8 + _shift_down(s8, 8)
    lane, inv = _pool_consts(SEQ_CHUNK, c * SEQ_CHUNK)
    return _pick_window(lane, s2[HALO:], s4[HALO:], s8[HALO:], s16[HALO:]) * inv - ext[HALO:]


def pool_fwd(l, p, w_bd, scale):
    T = p.shape[0]
    W = MIX_W
    nC = T // SEQ_CHUNK

    def body(l_ref, u_ref, w_ref, sc_ref, z_ref):
        for c in range(nC):
            pooled = _pooled_chunk(u_ref, c)
            mixed = _dot(pooled.astype(BF16), w_ref[...])
            z_ref[c * SEQ_CHUNK:(c + 1) * SEQ_CHUNK, :] = (mixed * sc_ref[0:1, :]).astype(BF16)

    grid_spec = pltpu.PrefetchScalarGridSpec(
        num_scalar_prefetch=1, grid=(1,),
        in_specs=[pl.BlockSpec((T, W), lambda i, l: (0, 4)),
                  pl.BlockSpec((None, W, W), lambda i, l: (l[0], 0, 0)),
                  pl.BlockSpec((None, 8, W), lambda i, l: (l[0], 0, 0))],
        out_specs=pl.BlockSpec((T, W), lambda i, l: (0, 0)),
    )
    return pl.pallas_call(
        body, name="pool_fwd", grid_spec=grid_spec,
        out_shape=jax.ShapeDtypeStruct((T, W), BF16),
        compiler_params=_cparams(("arbitrary",)),
    )(l, p, w_bd, scale)


def pool_bwd(l, p, dz, w_bd, scale):
    T = p.shape[0]
    W = MIX_W
    nC = T // SEQ_CHUNK

    def body(l_ref, u_ref, dz_ref, w_ref, sc_ref, du_ref, dw_ref, dsc_ref, e_sc, dpl_sc):
        dw = jnp.zeros((W, W), F32)
        dsc = jnp.zeros((1, W), F32)
        for c in range(nC):
            t0 = c * SEQ_CHUNK
            pooled = _pooled_chunk(u_ref, c).astype(BF16)
            mixed = _dot(pooled, w_ref[...])
            dz_c = dz_ref[t0:t0 + SEQ_CHUNK, :]
            dsc += _colsum(dz_c * mixed)
            dmixed = (dz_c * sc_ref[0:1, :]).astype(BF16)
            dw += _dot_tn(pooled, dmixed)
            dpooled = _dot_nt(dmixed, w_ref[...])
            _, inv = _pool_consts(SEQ_CHUNK, t0)
            dpl_sc[t0:t0 + SEQ_CHUNK, :] = dpooled
            e_sc[t0:t0 + SEQ_CHUNK, :] = dpooled * inv
        for c in range(nC):
            t0 = c * SEQ_CHUNK
            ext = _rows_with_tail(lambda s, n: e_sc[s:s + n, :], c, nC, W)
            s2 = ext + _shift_up(ext, 1)
            s4 = s2 + _shift_up(s2, 2)
            s8 = s4 + _shift_up(s4, 4)
            s16 = s8 + _shift_up(s8, 8)
            lane, _ = _pool_consts(SEQ_CHUNK, t0)
            n = SEQ_CHUNK
            du = _pick_window(lane, s2[:n], s4[:n], s8[:n], s16[:n]) - dpl_sc[t0:t0 + SEQ_CHUNK, :]
            du_ref[t0:t0 + SEQ_CHUNK, :] = du.astype(BF16)
        dw_ref[...] = dw
        dsc_ref[...] = jnp.concatenate([dsc, jnp.zeros((7, W), F32)], axis=0)

    grid_spec = pltpu.PrefetchScalarGridSpec(
        num_scalar_prefetch=1, grid=(1,),
        in_specs=[pl.BlockSpec((T, W), lambda i, l: (0, 4)),
                  pl.BlockSpec((T, W), lambda i, l: (0, 0)),
                  pl.BlockSpec((None, W, W), lambda i, l: (l[0], 0, 0)),
                  pl.BlockSpec((None, 8, W), lambda i, l: (l[0], 0, 0))],
        out_specs=[pl.BlockSpec((T, W), lambda i, l: (0, 0)), pl.BlockSpec((W, W), lambda i, l: (0, 0)),
                   pl.BlockSpec((8, W), lambda i, l: (0, 0))],
        scratch_shapes=[pltpu.VMEM((T, W), F32), pltpu.VMEM((T, W), F32)],
    )
    return pl.pallas_call(
        body, name="pool_bwd", grid_spec=grid_spec,
        out_shape=[jax.ShapeDtypeStruct((T, W), BF16), jax.ShapeDtypeStruct((W, W), F32),
                   jax.ShapeDtypeStruct((8, W), F32)],
        compiler_params=_cparams(("arbitrary",)),
    )(l, p, dz, w_bd, scale)


def _s5_disc(lre, lim, ldt):
    lr = jnp.minimum(lre, DT_LAMBDA_RE_MAX)
    dt = jnp.exp(ldt)
    mag = jnp.exp(lr * dt)
    a_re = mag * jnp.cos(lim * dt)
    a_im = mag * jnp.sin(lim * dt)
    den = lr * lr + lim * lim
    nr = a_re - 1.0
    return a_re, a_im, (nr * lr + a_im * lim) / den, (a_im * lr - nr * lim) / den


def _bd_mask(shape, row_blk, col_blk):
    r = lax.broadcasted_iota(jnp.int32, shape, 0) >> (row_blk.bit_length() - 1)
    c = lax.broadcasted_iota(jnp.int32, shape, 1) >> (col_blk.bit_length() - 1)
    return r == c


def s5_params(lam, b_t, c_t):
    L = lam.shape[0]

    def body(lam_ref, b_ref, c_ref, a_ref, bbd_ref, cbd_ref):
        a_re, a_im, f_re, f_im = _s5_disc(lam_ref[0:1, :], lam_ref[1:2, :], lam_ref[2:3, :])
        a_ref[...] = jnp.concatenate([a_re, a_im, jnp.zeros((6, SSM_W), F32)], axis=0)
        mb = _bd_mask((MIX_W, SSM_W), SSM_GROUP, SSM_STATE)
        bbd_ref[0] = jnp.where(mb, f_re * b_ref[0] - f_im * b_ref[1], 0.0).astype(BF16)
        bbd_ref[1] = jnp.where(mb, f_re * b_ref[1] + f_im * b_ref[0], 0.0).astype(BF16)
        mc = _bd_mask((SSM_W, MIX_W), SSM_STATE, SSM_GROUP)
        cbd_ref[0] = jnp.where(mc, c_ref[0], 0.0).astype(BF16)
        cbd_ref[1] = jnp.where(mc, c_ref[1], 0.0).astype(BF16)

    return pl.pallas_call(
        body, name="s5_params", grid=(L,),
        in_specs=[pl.BlockSpec((None, 8, SSM_W), lambda l: (l, 0, 0)),
                  pl.BlockSpec((None, 2, MIX_W, SSM_W), lambda l: (l, 0, 0, 0)),
                  pl.BlockSpec((None, 2, SSM_W, MIX_W), lambda l: (l, 0, 0, 0))],
        out_specs=[pl.BlockSpec((None, 8, SSM_W), lambda l: (l, 0, 0)),
                   pl.BlockSpec((None, 2, MIX_W, SSM_W), lambda l: (l, 0, 0, 0)),
                   pl.BlockSpec((None, 2, SSM_W, MIX_W), lambda l: (l, 0, 0, 0))],
        out_shape=[jax.ShapeDtypeStruct((L, 8, SSM_W), F32),
                   jax.ShapeDtypeStruct((L, 2, MIX_W, SSM_W), BF16),
                   jax.ShapeDtypeStruct((L, 2, SSM_W, MIX_W), BF16)],
        compiler_params=_cparams(("arbitrary",)),
    )(lam, b_t, c_t)


def s5_params_bwd(lam, b_t, gb, gc, da):
    L = lam.shape[0]
    exact = functools.partial(jnp.dot, preferred_element_type=F32, precision=lax.Precision.HIGHEST)

    def fold(shape, period):
        c = lax.broadcasted_iota(jnp.int32, shape, 0) & (period - 1)
        return jnp.where(c == lax.broadcasted_iota(jnp.int32, shape, 1), 1.0, 0.0)

    def body(lam_ref, b_ref, gb_ref, gc_ref, da_ref, dlam_ref, db_ref, dc_ref, dgrp_ref):
        lre, lim, ldt = lam_ref[0:1, :], lam_ref[1:2, :], lam_ref[2:3, :]
        (a_re, a_im, f_re, f_im), vjp = jax.vjp(_s5_disc, lre, lim, ldt)
        mb = _bd_mask((MIX_W, SSM_W), SSM_GROUP, SSM_STATE)
        gbr = jnp.where(mb, gb_ref[0], 0.0)
        gbi = jnp.where(mb, gb_ref[1], 0.0)
        df_re = _colsum(gbr * b_ref[0] + gbi * b_ref[1])
        df_im = _colsum(gbi * b_ref[0] - gbr * b_ref[1])
        fold_b = fold((SSM_W, 128), SSM_STATE)
        db_ref[0] = exact(f_re * gbr + f_im * gbi, fold_b)
        db_ref[1] = exact(f_re * gbi - f_im * gbr, fold_b)
        mc = _bd_mask((SSM_W, MIX_W), SSM_STATE, SSM_GROUP)
        fold_c = fold((MIX_W, 128), SSM_GROUP)
        dc_ref[0] = exact(jnp.where(mc, gc_ref[0], 0.0), fold_c)
        dc_ref[1] = exact(jnp.where(mc, gc_ref[1], 0.0), fold_c)
        dlre, dlim, dldt = vjp((da_ref[0:1, :], da_ref[1:2, :], df_re, df_im))
        dl = jnp.concatenate([dlre, dlim, dldt, jnp.zeros((5, SSM_W), F32)], axis=0)
        dlam_ref[...] = dl
        grp = jnp.where(_bd_mask((SSM_W, 128), SSM_STATE, 1), 1.0, 0.0)
        dgrp_ref[...] = jnp.dot(dl, grp, preferred_element_type=F32, precision=lax.Precision.HIGHEST)

    vec = pl.BlockSpec((None, 8, SSM_W), lambda l: (l, 0, 0))
    bsp = pl.BlockSpec((None, 2, MIX_W, SSM_W), lambda l: (l, 0, 0, 0))
    csp = pl.BlockSpec((None, 2, SSM_W, MIX_W), lambda l: (l, 0, 0, 0))
    return pl.pallas_call(
        body, name="s5_params_bwd", grid=(L,),
        in_specs=[vec, bsp, bsp, csp, vec],
        out_specs=[vec, pl.BlockSpec((None, 2, MIX_W, 128), lambda l: (l, 0, 0, 0)),
                   pl.BlockSpec((None, 2, SSM_W, 128), lambda l: (l, 0, 0, 0)),
                   pl.BlockSpec((None, 8, 128), lambda l: (l, 0, 0))],
        out_shape=[jax.ShapeDtypeStruct((L, 8, SSM_W), F32),
                   jax.ShapeDtypeStruct((L, 2, MIX_W, 128), F32),
                   jax.ShapeDtypeStruct((L, 2, SSM_W, 128), F32),
                   jax.ShapeDtypeStruct((L, 8, 128), F32)],
        compiler_params=_cparams(("arbitrary",)),
    )(lam, b_t, gb, gc, da)


def s5_bu(l, p, b_bd, tm=512):
    T = p.shape[0]

    def body(l_ref, u_ref, b_ref, bu_ref):
        u = u_ref[...].astype(BF16)
        bu_ref[0] = _dot(u, b_ref[0])
        bu_ref[1] = _dot(u, b_ref[1])

    grid_spec = pltpu.PrefetchScalarGridSpec(
        num_scalar_prefetch=1, grid=(T // tm,),
        in_specs=[pl.BlockSpec((tm, MIX_W), lambda i, l: (i, 3)),
                  pl.BlockSpec((None, 2, MIX_W, SSM_W), lambda i, l: (l[0], 0, 0, 0))],
        out_specs=pl.BlockSpec((2, tm, SSM_W), lambda i, l: (0, i, 0)),
    )
    return pl.pallas_call(
        body, name="s5_bu", grid_spec=grid_spec,
        out_shape=jax.ShapeDtypeStruct((2, T, SSM_W), F32),
        compiler_params=_cparams(("arbitrary",)),
    )(l, p, b_bd)


def s5_scan(l, avec, xs, reverse):
    T = xs.shape[1]
    CH = SEQ_CHUNK
    nC = T // CH
    LW = 128
    n_steps = CH.bit_length() - 1

    def body(l_ref, a_ref, x_ref, s_ref):
        ar = a_ref[0:1, :]
        ai = -a_ref[1:2, :] if reverse else a_ref[1:2, :]
        pows = [(ar, ai)]
        for _ in range(n_steps - 1):
            r, i = pows[-1]
            pows.append((r * r - i * i, 2.0 * r * i))
        row = lax.broadcasted_iota(jnp.int32, (CH, LW), 0)

        def local_scan(re, im):
            for k in range(n_steps):
                d = 1 << k
                pr, pi = pows[k]
                if reverse:
                    keep = row < CH - d
                    sr, si = _shift_up(re, d), _shift_up(im, d)
                else:
                    keep = row >= d
                    sr, si = _shift_down(re, d), _shift_down(im, d)
                sr = jnp.where(keep, sr, 0.0)
                si = jnp.where(keep, si, 0.0)
                re, im = re + pr * sr - pi * si, im + pr * si + pi * sr
            return re, im

        edge = CH - 1 if reverse else 0
        pw_re, pw_im = local_scan(jnp.where(row == edge, ar, 0.0), jnp.where(row == edge, ai, 0.0))
        last = 0 if reverse else CH - 1

        def chunk(c, carry):
            cr, ci = carry
            cc = nC - 1 - c if reverse else c
            t0 = pl.multiple_of(cc * CH, CH)
            re, im = local_scan(x_ref[0, pl.ds(t0, CH), :], x_ref[1, pl.ds(t0, CH), :])
            re2 = re + pw_re * cr - pw_im * ci
            im2 = im + pw_re * ci + pw_im * cr
            s_ref[0, pl.ds(t0, CH), :] = re2
            s_ref[1, pl.ds(t0, CH), :] = im2
            return re2[last:last + 1, :], im2[last:last + 1, :]

        lax.fori_loop(0, nC, chunk, (jnp.zeros((1, LW), F32), jnp.zeros((1, LW), F32)))

    grid_spec = pltpu.PrefetchScalarGridSpec(
        num_scalar_prefetch=1, grid=(SSM_W // LW,),
        in_specs=[pl.BlockSpec((None, 8, LW), lambda g, l: (l[0], 0, g)),
                  pl.BlockSpec((2, T, LW), lambda g, l: (0, 0, g))],
        out_specs=pl.BlockSpec((2, T, LW), lambda g, l: (0, 0, g)),
    )
    return pl.pallas_call(
        body, name="s5_scan_rev" if reverse else "s5_scan_fwd", grid_spec=grid_spec,
        out_shape=jax.ShapeDtypeStruct((2, T, SSM_W), F32),
        compiler_params=_cparams(("arbitrary",)),
    )(l, avec, xs)


_GELU_C = 0.7978845608028654
_GELU_K = 0.044715


def _s5_y(u, s_ref, c_ref, d_row):
    y = _dot(s_ref[0].astype(BF16), c_ref[0]) - _dot(s_ref[1].astype(BF16), c_ref[1])
    return y + d_row * u


def s5_out(l, p, s, c_bd, ssm_d, tm=512):
    T = p.shape[0]

    def body(l_ref, u_ref, s_ref, c_ref, d_ref, yg_ref):
        y = _s5_y(u_ref[...], s_ref, c_ref, d_ref[0:1, :])
        th = jnp.tanh(_GELU_C * (y + _GELU_K * y * y * y))
        yg_ref[...] = (0.5 * y * (1.0 + th)).astype(BF16)

    grid_spec = pltpu.PrefetchScalarGridSpec(
        num_scalar_prefetch=1, grid=(T // tm,),
        in_specs=[pl.BlockSpec((tm, MIX_W), lambda i, l: (i, 3)),
                  pl.BlockSpec((2, tm, SSM_W), lambda i, l: (0, i, 0)),
                  pl.BlockSpec((None, 2, SSM_W, MIX_W), lambda i, l: (l[0], 0, 0, 0)),
                  pl.BlockSpec((None, 8, MIX_W), lambda i, l: (l[0], 0, 0))],
        out_specs=pl.BlockSpec((tm, MIX_W), lambda i, l: (i, 0)),
    )
    return pl.pallas_call(
        body, name="s5_out", grid_spec=grid_spec,
        out_shape=jax.ShapeDtypeStruct((T, MIX_W), BF16),
        compiler_params=_cparams(("arbitrary",)),
    )(l, p, s, c_bd, ssm_d)


def s5_bwd_y(l, p, s, dyg, c_bd, ssm_d, tm=512):
    T = p.shape[0]

    def body(l_ref, u_ref, s_ref, dyg_ref, c_ref, d_ref, ds_ref, du_ref, gc_ref, dd_ref):
        @pl.when(pl.program_id(0) == 0)
        def _():
            gc_ref[...] = jnp.zeros_like(gc_ref)
            dd_ref[...] = jnp.zeros_like(dd_ref)

        u = u_ref[...]
        y = _s5_y(u, s_ref, c_ref, d_ref[0:1, :])
        inner = _GELU_C * (y + _GELU_K * y * y * y)
        th = jnp.tanh(inner)
        dgelu = 0.5 * (1.0 + th) + 0.5 * y * (1.0 - th * th) * (_GELU_C * (1.0 + 3.0 * _GELU_K * y * y))
        dy = dyg_ref[...] * dgelu
        dd_ref[0:1, :] += _colsum(dy * u)
        du_ref[...] = dy * d_ref[0:1, :]
        dyb = dy.astype(BF16)
        ds_ref[0] = _dot_nt(dyb, c_ref[0])
        ds_ref[1] = -_dot_nt(dyb, c_ref[1])
        gc_ref[0] += _dot_tn(s_ref[0].astype(BF16), dyb)
        gc_ref[1] -= _dot_tn(s_ref[1].astype(BF16), dyb)

    grid_spec = pltpu.PrefetchScalarGridSpec(
        num_scalar_prefetch=1, grid=(T // tm,),
        in_specs=[pl.BlockSpec((tm, MIX_W), lambda i, l: (i, 3)),
                  pl.BlockSpec((2, tm, SSM_W), lambda i, l: (0, i, 0)),
                  pl.BlockSpec((tm, MIX_W), lambda i, l: (i, 0)),
                  pl.BlockSpec((None, 2, SSM_W, MIX_W), lambda i, l: (l[0], 0, 0, 0)),
                  pl.BlockSpec((None, 8, MIX_W), lambda i, l: (l[0], 0, 0))],
        out_specs=[pl.BlockSpec((2, tm, SSM_W), lambda i, l: (0, i, 0)),
                   pl.BlockSpec((tm, MIX_W), lambda i, l: (i, 0)),
                   pl.BlockSpec((2, SSM_W, MIX_W), lambda i, l: (0, 0, 0)),
                   pl.BlockSpec((8, MIX_W), lambda i, l: (0, 0))],
    )
    return pl.pallas_call(
        body, name="s5_bwd_y", grid_spec=grid_spec,
        out_shape=[jax.ShapeDtypeStruct((2, T, SSM_W), F32), jax.ShapeDtypeStruct((T, MIX_W), F32),
                   jax.ShapeDtypeStruct((2, SSM_W, MIX_W), F32), jax.ShapeDtypeStruct((8, MIX_W), F32)],
        compiler_params=_cparams(("arbitrary",)),
    )(l, p, s, dyg, c_bd, ssm_d)


def s5_bwd_u(l, p, lam_s, du_skip, b_bd, tm=512):
    T = p.shape[0]

    def body(l_ref, u_ref, ls_ref, dus_ref, b_ref, du_ref, gb_ref):
        @pl.when(pl.program_id(0) == 0)
        def _():
            gb_ref[...] = jnp.zeros_like(gb_ref)

        u = u_ref[...].astype(BF16)
        lr = ls_ref[0].astype(BF16)
        li = ls_ref[1].astype(BF16)
        gb_ref[0] += _dot_tn(u, lr)
        gb_ref[1] += _dot_tn(u, li)
        du_ref[...] = (dus_ref[...] + _dot_nt(lr, b_ref[0]) + _dot_nt(li, b_ref[1])).astype(BF16)

    grid_spec = pltpu.PrefetchScalarGridSpec(
        num_scalar_prefetch=1, grid=(T // tm,),
        in_specs=[pl.BlockSpec((tm, MIX_W), lambda i, l: (i, 3)),
                  pl.BlockSpec((2, tm, SSM_W), lambda i, l: (0, i, 0)),
                  pl.BlockSpec((tm, MIX_W), lambda i, l: (i, 0)),
                  pl.BlockSpec((None, 2, MIX_W, SSM_W), lambda i, l: (l[0], 0, 0, 0))],
        out_specs=[pl.BlockSpec((tm, MIX_W), lambda i, l: (i, 0)),
                   pl.BlockSpec((2, MIX_W, SSM_W), lambda i, l: (0, 0, 0))],
    )
    return pl.pallas_call(
        body, name="s5_bwd_u", grid_spec=grid_spec,
        out_shape=[jax.ShapeDtypeStruct((T, MIX_W), BF16), jax.ShapeDtypeStruct((2, MIX_W, SSM_W), F32)],
        compiler_params=_cparams(("arbitrary",)),
    )(l, p, lam_s, du_skip, b_bd)


def s5_bwd_a(s, lam_s):
    T = s.shape[1]
    nC = T // SEQ_CHUNK
    LW = 128

    def body(s_ref, ls_ref, da_ref):
        dre = jnp.zeros((1, LW), F32)
        dim = jnp.zeros((1, LW), F32)
        for c in range(nC):
            t0 = c * SEQ_CHUNK
            sr = _shift_down(_rows_with_lead(lambda a, n: s_ref[0, a:a + n, :], c, LW), 1)[HALO:]
            si = _shift_down(_rows_with_lead(lambda a, n: s_ref[1, a:a + n, :], c, LW), 1)[HALO:]
            lr = ls_ref[0, t0:t0 + SEQ_CHUNK, :]
            li = ls_ref[1, t0:t0 + SEQ_CHUNK, :]
            dre += _colsum(sr * lr + si * li)
            dim += _colsum(sr * li - si * lr)
        da_ref[...] = jnp.concatenate([dre, dim, jnp.zeros((6, LW), F32)], axis=0)

    blk = pl.BlockSpec((2, T, LW), lambda g: (0, 0, g))
    return pl.pallas_call(
        body, name="s5_bwd_a", grid=(SSM_W // LW,),
        in_specs=[blk, blk],
        out_specs=pl.BlockSpec((8, LW), lambda g: (0, g)),
        out_shape=jax.ShapeDtypeStruct((8, SSM_W), F32),
        compiler_params=_cparams(("arbitrary",)),
    )(s, lam_s)


SB_BLK = 128
SB_SCALE = SB_HEAD ** -0.5


def _split_bf16(x):
    hi = x.astype(BF16)
    return hi, (x - hi.astype(F32)).astype(BF16)


def _dot_split(x, tri):
    hi, lo = _split_bf16(x)
    return _dot(hi, tri) + _dot(lo, tri)


SB_SLABS = MIX_W // SB_BLK
SB_STACK = 2 * SB_SLABS * SB_BLK
SB_PAIR = 2 * SB_BLK


def _sb_valid(r0, c0):
    row = (lax.broadcasted_iota(jnp.int32, (SB_STACK, SB_BLK), 0) & (SB_BLK - 1)) + r0
    col = lax.broadcasted_iota(jnp.int32, (SB_STACK, SB_BLK), 1) + c0
    return col < row


def _sb_stack(ref, r0, scale):
    lane = lax.broadcasted_iota(jnp.int32, (SB_BLK, SB_BLK), 1)
    parts = []
    for s in range(SB_SLABS):
        blk = ref[pl.ds(r0, SB_BLK), s * SB_BLK:(s + 1) * SB_BLK] * scale
        parts += [jnp.where(lane < SB_HEAD, blk, 0.0), jnp.where(lane >= SB_HEAD, blk, 0.0)]
    return jnp.concatenate(parts, axis=0).astype(BF16)


def _sb_rows_nt(stack, ref, c0):
    return jnp.concatenate(
        [_dot_nt(stack[s * SB_PAIR:(s + 1) * SB_PAIR], ref[pl.ds(c0, SB_BLK), s * SB_BLK:(s + 1) * SB_BLK].astype(BF16))
         for s in range(SB_SLABS)], axis=0)


def _sb_wide(stack, s):
    return jnp.concatenate([stack[s * SB_PAIR:s * SB_PAIR + SB_BLK], stack[s * SB_PAIR + SB_BLK:(s + 1) * SB_PAIR]],
                           axis=1)


def _sb_logits(q_stack, k_ref, c0, valid):
    z = _sb_rows_nt(q_stack, k_ref, c0)
    sp = jnp.log(1.0 + jnp.exp(-jnp.abs(z)))
    ls_pos = jnp.minimum(z, 0.0) - sp
    lk = jnp.minimum(-z, 0.0) - sp
    if valid is not None:
        lk = jnp.where(valid, lk, 0.0)
    return z, ls_pos, lk


def _tri(lower):
    r = lax.broadcasted_iota(jnp.int32, (SB_BLK, SB_BLK), 0)
    c = lax.broadcasted_iota(jnp.int32, (SB_BLK, SB_BLK), 1)
    return jnp.where(r > c if lower else r < c, 1.0, 0.0).astype(BF16)


def sb_fwd(p):
    T = p.shape[0]
    W = MIX_W
    nB = T // SB_BLK

    def body(q_ref, k_ref, v_ref, o_ref, tot_ref, acc_sc):
        tri = _tri(True)

        def qblock(i, _):
            r0 = pl.multiple_of(i * SB_BLK, SB_BLK)
            q = _sb_stack(q_ref, r0, SB_SCALE)
            acc_sc[...] = jnp.zeros_like(acc_sc)

            def kblocks(c0s, run, valid):
                parts = [_sb_logits(q, k_ref, c0, valid) for c0 in c0s]
                for c0, (_, ls_pos, lk) in zip(c0s, parts):
                    a = jnp.exp(ls_pos + _dot_split(lk, tri) + run)
                    if valid is not None:
                        a = jnp.where(valid, a, 0.0)
                    a = a.astype(BF16)
                    v = _sb_stack(v_ref, c0, 1.0)
                    for s in range(SB_SLABS):
                        acc_sc[:, s * SB_BLK:(s + 1) * SB_BLK] += _dot(_sb_wide(a, s), v[s * SB_PAIR:(s + 1) * SB_PAIR])
                    run = run + jnp.sum(lk, axis=1, keepdims=True)
                return run

            def key_block(jj):
                return pl.multiple_of((i - jj) * SB_BLK, SB_BLK)

            run = kblocks([r0], jnp.zeros((SB_STACK, 1), F32), _sb_valid(0, 0))
            odd = i & 1
            run = lax.cond(odd == 1, lambda r: kblocks([key_block(1)], r, None), lambda r: r, run)
            total = lax.fori_loop(
                0, i >> 1, lambda t, r: kblocks([key_block(1 + odd + 2 * t), key_block(2 + odd + 2 * t)], r, None), run)
            o_ref[pl.ds(r0, SB_BLK), :] = acc_sc[...].astype(BF16)
            tot_ref[pl.ds(pl.multiple_of(i * SB_STACK, SB_STACK), SB_STACK), :] = jnp.broadcast_to(total, (SB_STACK, SB_BLK))
            return 0

        lax.fori_loop(0, nB, qblock, 0)

    return pl.pallas_call(
        body, name="sb_fwd", grid=(1,),
        in_specs=[pl.BlockSpec((T, W), lambda i: (0, 5)), pl.BlockSpec((T, W), lambda i: (0, 6)),
                  pl.BlockSpec((T, W), lambda i: (0, 7))],
        out_specs=[pl.BlockSpec((T, W), lambda i: (0, 0)), pl.BlockSpec((nB * SB_STACK, SB_BLK), lambda i: (0, 0))],
        out_shape=[jax.ShapeDtypeStruct((T, W), BF16), jax.ShapeDtypeStruct((nB * SB_STACK, SB_BLK), F32)],
        scratch_shapes=[pltpu.VMEM((SB_BLK, W), F32)],
        compiler_params=_cparams(("arbitrary",)),
    )(p, p, p)


def sb_bwd(p, do, tot):
    T = p.shape[0]
    W = MIX_W
    nB = T // SB_BLK

    def body(q_ref, k_ref, v_ref, do_ref, tot_ref, dqkv_ref, dq_sc, dk_sc, dv_sc):
        tri_gt = _tri(True)
        tri_lt = _tri(False)
        dq_sc[...] = jnp.zeros_like(dq_sc)
        dk_sc[...] = jnp.zeros_like(dk_sc)
        dv_sc[...] = jnp.zeros_like(dv_sc)
        zcol = jnp.zeros((SB_STACK, 1), F32)

        def qblock(i, _):
            r0 = pl.multiple_of(i * SB_BLK, SB_BLK)
            q = _sb_stack(q_ref, r0, SB_SCALE)
            dob = _sb_stack(do_ref, r0, 1.0)

            total = tot_ref[pl.ds(pl.multiple_of(i * SB_STACK, SB_STACK), SB_STACK), 0:1]

            def kblocks(c0s, carry, valid):
                pre, seen = carry
                parts = [_sb_logits(q, k_ref, c0, valid) for c0 in c0s]
                for c0, (z, ls_pos, lk) in zip(c0s, parts):
                    seen = seen + jnp.sum(lk, axis=1, keepdims=True)
                    a = jnp.exp(ls_pos + _dot_split(lk, tri_gt) + (total - seen))
                    if valid is not None:
                        a = jnp.where(valid, a, 0.0)
                    dlw = _sb_rows_nt(dob, v_ref, c0) * a
                    g = pre + _dot_split(dlw, tri_lt)
                    sig = _sigmoid(z)
                    dz = dlw * (1.0 - sig) - g * sig
                    if valid is not None:
                        dz = jnp.where(valid, dz, 0.0)
                    dz = dz.astype(BF16)
                    ab = a.astype(BF16)
                    km = _sb_stack(k_ref, c0, 1.0)
                    for s in range(SB_SLABS):
                        pair = slice(s * SB_PAIR, (s + 1) * SB_PAIR)
                        ls = slice(s * SB_BLK, (s + 1) * SB_BLK)
                        dk_sc[pl.ds(c0, SB_BLK), ls] += _dot_tn(dz[pair], q[pair])
                        dv_sc[pl.ds(c0, SB_BLK), ls] += _dot_tn(ab[pair], dob[pair])
                        dq_sc[pl.ds(r0, SB_BLK), ls] += _dot(_sb_wide(dz, s), km[pair])
                    pre = pre + jnp.sum(dlw, axis=1, keepdims=True)
                return pre, seen

            def key_block(j):
                return pl.multiple_of(j * SB_BLK, SB_BLK)

            carry = lax.fori_loop(
                0, i >> 1, lambda t, c: kblocks([key_block(2 * t), key_block(2 * t + 1)], c, None), (zcol, zcol))
            carry = lax.cond((i & 1) == 1, lambda c: kblocks([key_block(i - 1)], c, None), lambda c: c, carry)
            kblocks([r0], carry, _sb_valid(0, 0))
            return 0

        lax.fori_loop(0, nB, qblock, 0)
        dqkv_ref[:, 0:W] = (dq_sc[...] * SB_SCALE).astype(BF16)
        dqkv_ref[:, W:2 * W] = dk_sc[...].astype(BF16)
        dqkv_ref[:, 2 * W:3 * W] = dv_sc[...].astype(BF16)

    return pl.pallas_call(
        body, name="sb_bwd", grid=(1,),
        in_specs=[pl.BlockSpec((T, W), lambda i: (0, 5)), pl.BlockSpec((T, W), lambda i: (0, 6)),
                  pl.BlockSpec((T, W), lambda i: (0, 7)), pl.BlockSpec((T, W), lambda i: (0, 0)),
                  pl.BlockSpec((nB * SB_STACK, SB_BLK), lambda i: (0, 0))],
        out_specs=pl.BlockSpec((T, 3 * W), lambda i: (0, 0)),
        out_shape=jax.ShapeDtypeStruct((T, 3 * W), BF16),
        scratch_shapes=[pltpu.VMEM((T, W), F32), pltpu.VMEM((T, W), F32), pltpu.VMEM((T, W), F32)],
        compiler_params=_cparams(("arbitrary",)),
    )(p, p, p, do, tot)


def _dot_cols(a, w_ref):
    return jnp.concatenate([_dot(a, w_ref[j]) for j in range(N_DEV)], axis=1)


def _dot_cols_nt(dy, w_ref):
    n = w_ref.shape[2]
    out = _dot_nt(dy[:, 0:n], w_ref[0])
    for j in range(1, N_DEV):
        out += _dot_nt(dy[:, j * n:(j + 1) * n], w_ref[j])
    return out


def _acc_cols_tn(acc_ref, a, dy):
    n = acc_ref.shape[2]
    for j in range(N_DEV):
        acc_ref[j] += _dot_tn(a, dy[:, j * n:(j + 1) * n])


def _merge_branches(za_ref, yg_ref, z_ref, o_ref, gate_refs, wc_ref, wglu_ref, wp_ref, ws_ref):
    D = D_MODEL
    glu = _dot_cols(yg_ref[...], wglu_ref)
    glu_a, sg = glu[:, :D], _sigmoid(glu[:, D:])
    ys = [_dot_cols(za_ref[...], wc_ref), glu_a * sg, _dot_cols(z_ref[...], wp_ref), _dot_cols(o_ref[...], ws_ref)]
    gs = [_sigmoid(g[...]) for g in gate_refs]
    merged = gs[0] * ys[0] + gs[1] * ys[1] + gs[2] * ys[2] + gs[3] * ys[3]
    return ys, gs, glu_a, sg, merged


def _merge_specs(tm, D):
    W = MIX_W
    br = pl.BlockSpec((tm, W), lambda i, l: (i, 0))
    gates = [pl.BlockSpec((tm, D), functools.partial(lambda i, l, b: (i, 2 + b), b=b)) for b in range(4)]
    wsm = pl.BlockSpec((None, N_DEV, W, D // N_DEV), lambda i, l: (l[0], 0, 0, 0))
    weights = [wsm, pl.BlockSpec((None, N_DEV, W, 2 * D // N_DEV), lambda i, l: (l[0], 0, 0, 0)), wsm, wsm,
               pl.BlockSpec((None, D, D), lambda i, l: (l[0], 0, 0))]
    return [br] * 4 + gates, weights


def merge_fwd(l, p, za, yg, z, o, x, pv, wc, wglu, wp, ws, wo, tm=512):
    T, D = x.shape
    tm = min(tm, T)

    def body(l_ref, za_ref, yg_ref, z_ref, o_ref, g0, g1, g2, g3, x_ref, pv_ref,
             wc_ref, wglu_ref, wp_ref, ws_ref, wo_ref, xn_ref, m_ref):
        _, _, _, _, merged = _merge_branches(za_ref, yg_ref, z_ref, o_ref, (g0, g1, g2, g3),
                                             wc_ref, wglu_ref, wp_ref, ws_ref)
        m = _dot(merged.astype(BF16), wo_ref[...])
        m_ref[...] = m
        xn_ref[...] = _postnorm_res(x_ref[...], m, pv_ref, 1.0)

    acts, weights = _merge_specs(tm, D)
    tile = pl.BlockSpec((tm, D), lambda i, l: (i, 0))
    grid_spec = pltpu.PrefetchScalarGridSpec(
        num_scalar_prefetch=1, grid=(T // tm,),
        in_specs=acts + [tile, pl.BlockSpec((8, D), lambda i, l: (0, 0))] + weights,
        out_specs=[tile, tile],
    )
    return pl.pallas_call(
        body, name="merge_fwd", grid_spec=grid_spec,
        out_shape=[jax.ShapeDtypeStruct((T, D), F32), jax.ShapeDtypeStruct((T, D), F32)],
        compiler_params=_cparams(("arbitrary",)),
    )(l, za, yg, z, o, p, p, p, p, x, pv, wc, wglu, wp, ws, wo)


def merge_bwd(l, p, za, yg, z, o, m, dxn, pv, wc, wglu, wp, ws, wo, tm=256):
    T, D = m.shape
    W = MIX_W
    tm = min(tm, T)
    ni = T // tm

    def body(l_ref, za_ref, yg_ref, z_ref, o_ref, g0, g1, g2, g3, m_ref, dxn_ref, pv_ref,
             wc_ref, wglu_ref, wp_ref, ws_ref, wo_ref,
             dza_ref, dyg_ref, dz_ref, do_ref, dg_ref, pg_ref, gwc_ref, gwglu_ref, gwp_ref, gws_ref, gwo_ref,
             awc, awglu, awp, aws, awo):
        i = pl.program_id(0)

        @pl.when(i == 0)
        def _():
            pg_ref[...] = jnp.zeros_like(pg_ref)
            for a in (awc, awglu, awp, aws, awo):
                a[...] = jnp.zeros_like(a)

        ys, gs, glu_a, sg, merged = _merge_branches(za_ref, yg_ref, z_ref, o_ref, (g0, g1, g2, g3),
                                                    wc_ref, wglu_ref, wp_ref, ws_ref)
        dm = _postnorm_bwd(dxn_ref[...], m_ref[...], pv_ref, pg_ref, 1.0).astype(BF16)
        awo[...] += _dot_tn(merged.astype(BF16), dm)
        dmerged = _dot_nt(dm, wo_ref[...])
        for b in range(4):
            dg_ref[:, b * D:(b + 1) * D] = (dmerged * ys[b] * gs[b] * (1.0 - gs[b])).astype(BF16)
        dya = (dmerged * gs[0]).astype(BF16)
        _acc_cols_tn(awc, za_ref[...], dya)
        dza_ref[...] = _dot_cols_nt(dya, wc_ref)
        dyc = (dmerged * gs[2]).astype(BF16)
        _acc_cols_tn(awp, z_ref[...], dyc)
        dz_ref[...] = _dot_cols_nt(dyc, wp_ref)
        dyd = (dmerged * gs[3]).astype(BF16)
        _acc_cols_tn(aws, o_ref[...], dyd)
        do_ref[...] = _dot_cols_nt(dyd, ws_ref)
        dyb = dmerged * gs[1]
        dglu = jnp.concatenate([dyb * sg, dyb * glu_a * sg * (1.0 - sg)], axis=1).astype(BF16)
        _acc_cols_tn(awglu, yg_ref[...], dglu)
        dyg_ref[...] = _dot_cols_nt(dglu, wglu_ref)

        @pl.when(i == ni - 1)
        def _():
            gwc_ref[...] = awc[...].astype(BF16)
            gwglu_ref[...] = awglu[...].astype(BF16)
            gwp_ref[...] = awp[...].astype(BF16)
            gws_ref[...] = aws[...].astype(BF16)
            gwo_ref[...] = awo[...].astype(BF16)

    acts, weights = _merge_specs(tm, D)
    tile = pl.BlockSpec((tm, D), lambda i, l: (i, 0))
    br = pl.BlockSpec((tm, W), lambda i, l: (i, 0))
    full = lambda *s: pl.BlockSpec(s, lambda i, l: (0,) * len(s))
    sm, glu_s = (N_DEV, W, D // N_DEV), (N_DEV, W, 2 * D // N_DEV)
    grid_spec = pltpu.PrefetchScalarGridSpec(
        num_scalar_prefetch=1, grid=(ni,),
        in_specs=acts + [tile, tile, pl.BlockSpec((8, D), lambda i, l: (0, 0))] + weights,
        out_specs=[br, br, br, br, pl.BlockSpec((tm, 4 * D), lambda i, l: (i, 0)), full(8, D),
                   full(*sm), full(*glu_s), full(*sm), full(*sm), full(D, D)],
        scratch_shapes=[pltpu.VMEM(sm, F32), pltpu.VMEM(glu_s, F32), pltpu.VMEM(sm, F32),
                        pltpu.VMEM(sm, F32), pltpu.VMEM((D, D), F32)],
    )
    f32br = jax.ShapeDtypeStruct((T, W), F32)
    return pl.pallas_call(
        body, name="merge_bwd", grid_spec=grid_spec,
        out_shape=[f32br, f32br, f32br, f32br, jax.ShapeDtypeStruct((T, 4 * D), BF16),
                   jax.ShapeDtypeStruct((8, D), F32),
                   jax.ShapeDtypeStruct(sm, BF16), jax.ShapeDtypeStruct(glu_s, BF16),
                   jax.ShapeDtypeStruct(sm, BF16), jax.ShapeDtypeStruct(sm, BF16),
                   jax.ShapeDtypeStruct((D, D), BF16)],
        compiler_params=_cparams(("arbitrary",)),
    )(l, za, yg, z, o, p, p, p, p, m, dxn, pv, wc, wglu, wp, ws, wo)


def dp_assemble(d_conv, d_ssm, d_pool, d_qkv, d_gates, tm=512):
    T = d_conv.shape[0]
    W = MIX_W

    def body(c_ref, s_ref, p_ref, q_ref, g_ref, dp_ref):
        dp_ref[:, 0:3 * W] = c_ref[...]
        dp_ref[:, 3 * W:4 * W] = s_ref[...]
        dp_ref[:, 4 * W:5 * W] = p_ref[...]
        dp_ref[:, 5 * W:8 * W] = q_ref[...]
        dp_ref[:, GATE_OFF:] = g_ref[...]

    row = lambda w: pl.BlockSpec((tm, w), lambda i: (i, 0))
    return pl.pallas_call(
        body, name="dp_assemble", grid=(T // tm,),
        in_specs=[row(3 * W), row(W), row(W), row(3 * W), row(4 * D_MODEL)],
        out_specs=row(IN_COLS),
        out_shape=jax.ShapeDtypeStruct((T, IN_COLS), BF16),
        compiler_params=_cparams(("arbitrary",)),
    )(d_conv, d_ssm, d_pool, d_qkv, d_gates)


def loss_head(y, target, tm=512):
    T, D = y.shape

    def body(y_ref, t_ref, dy_ref, loss_ref):
        @pl.when(pl.program_id(0) == 0)
        def _():
            loss_ref[...] = jnp.zeros_like(loss_ref)

        err = y_ref[...] - t_ref[...]
        dy_ref[...] = err * (1.0 / D)
        loss_ref[...] += jnp.sum(err * err) * (0.5 / D)

    tile = pl.BlockSpec((tm, D), lambda i: (i, 0))
    return pl.pallas_call(
        body, name="loss_head", grid=(T // tm,),
        in_specs=[tile, tile],
        out_specs=[tile, pl.BlockSpec((8, 128), lambda i: (0, 0))],
        out_shape=[jax.ShapeDtypeStruct((T, D), F32), jax.ShapeDtypeStruct((8, 128), F32)],
        compiler_params=_cparams(("arbitrary",)),
    )(y, target)


def cast_layer(ld, items, after=None):
    n = len(items)
    extra = [] if after is None else [after]

    def body(ld_ref, *refs):
        for src, dst in zip(refs[:n], refs[n + len(extra):]):
            dst[...] = src[...].astype(BF16)

    def shard(w, k):
        return w.shape[1:] if k is None else w.shape[2:]

    def in_spec(w, k):
        sh = shard(w, k)
        if k is None:
            return pl.BlockSpec((None,) + sh, lambda i, ld, n=len(sh): (ld[0],) + (0,) * n)
        return pl.BlockSpec((None, None) + sh, lambda i, ld, n=len(sh), k=k: (ld[0], k) + (0,) * n)

    def out_spec(w, k):
        sh = shard(w, k)
        return pl.BlockSpec((None, None) + sh, lambda i, ld, n=len(sh): (0, ld[1]) + (0,) * n)

    grid_spec = pltpu.PrefetchScalarGridSpec(
        num_scalar_prefetch=1, grid=(1,),
        in_specs=[in_spec(w, k) for w, k in items] + [ANY] * len(extra),
        out_specs=[out_spec(w, k) for w, k in items])
    return pl.pallas_call(
        body, name="cast_layer", grid_spec=grid_spec,
        out_shape=[jax.ShapeDtypeStruct((1, N_DEV) + shard(w, k), BF16) for w, k in items],
        compiler_params=_cparams(("arbitrary",)),
    )(ld, *[w for w, _ in items], *extra)


def place_own(dev, a):
    def body(dev_ref, a_ref, o_ref):
        o_ref[...] = a_ref[...]

    grid_spec = pltpu.PrefetchScalarGridSpec(
        num_scalar_prefetch=1, grid=(1,),
        in_specs=[pl.BlockSpec(a.shape, lambda i, dev: (0, 0))],
        out_specs=pl.BlockSpec((None,) + a.shape, lambda i, dev: (dev[0], 0, 0)))
    return pl.pallas_call(
        body, name="place_own", grid_spec=grid_spec,
        out_shape=jax.ShapeDtypeStruct((N_DEV,) + a.shape, a.dtype),
        compiler_params=_cparams(("arbitrary",)),
    )(dev, a)


def _silu(x):
    return x * _sigmoid(x)


def ada_fwd(c_all, w_ada, b_cols):
    L, D, n = w_ada.shape

    def body(c_ref, w_ref, b_ref, o_ref):
        c_act = _silu(c_ref[...]).astype(BF16)
        o_ref[...] = _dot(c_act, w_ref[...].astype(BF16)) + b_ref[...]

    return pl.pallas_call(
        body, name="ada_fwd", grid=(L,),
        in_specs=[pl.BlockSpec((N_DEV, D), lambda l: (0, 0)), pl.BlockSpec((None, D, n), lambda l: (l, 0, 0)),
                  pl.BlockSpec((None, 1, n), lambda l: (l, 0, 0))],
        out_specs=pl.BlockSpec((None, N_DEV, n), lambda l: (l, 0, 0)),
        out_shape=jax.ShapeDtypeStruct((L, N_DEV, n), F32),
        compiler_params=_cparams(("arbitrary",)),
    )(c_all, w_ada, b_cols)


def _adamw(w, g, m, v):
    m = ADAM_B1 * m + (1.0 - ADAM_B1) * g
    v = ADAM_B2 * v + (1.0 - ADAM_B2) * (g * g)
    m_hat = m / (1.0 - ADAM_B1 ** ADAM_STEP)
    v_hat = v / (1.0 - ADAM_B2 ** ADAM_STEP)
    delta = -ADAM_LR * (m_hat / (jnp.sqrt(v_hat) + ADAM_EPS) + ADAM_WD * w)
    return delta, m, v


def ada_update(c_all, dada_cols, w, m, v, rb=256):
    L, D, n = w.shape

    def body(c_ref, d_ref, w_ref, m_ref, v_ref, g_ref, dl_ref, nm_ref, nv_ref):
        c_act = _silu(c_ref[...]).astype(BF16)
        g = _dot_tn(c_act, d_ref[...].astype(BF16))
        g_ref[...] = g
        dl_ref[...], nm_ref[...], nv_ref[...] = _adamw(w_ref[...], g, m_ref[...], v_ref[...])

    blk = pl.BlockSpec((None, rb, n), lambda l, i: (l, i, 0))
    out = jax.ShapeDtypeStruct((L, D, n), F32)
    return pl.pallas_call(
        body, name="ada_update", grid=(L, D // rb),
        in_specs=[pl.BlockSpec((N_DEV, rb), lambda l, i: (0, i)),
                  pl.BlockSpec((None, N_DEV, n), lambda l, i: (l, 0, 0)), blk, blk, blk],
        out_specs=[blk, blk, blk, blk], out_shape=[out, out, out, out],
        compiler_params=_cparams(("arbitrary", "arbitrary")),
    )(c_all, dada_cols, w, m, v)


SUM_UPDATE_RECV_BYTES = 12 * 1024 * 1024


def sum_update(dev, first, recvs, owns, w, m, v, prev=None, after=None):
    n_slots, R, C = w.shape
    S = len(recvs)
    assert len(owns) == S and first + S <= n_slots
    rb_max = SUM_UPDATE_RECV_BYTES // (S * N_DEV * C * 2)
    rb = max(r for r in range(8, R + 1, 8) if R % r == 0 and (r <= rb_max or r == 8))
    last = R // rb - 1
    n_prev = 0 if prev is None else 4
    extra = list(prev or ()) + ([] if after is None else [after])

    def body(dev_ref, *refs):
        r_refs, o_refs = refs[:S], refs[S:2 * S]
        w_ref, m_ref, v_ref = refs[2 * S:2 * S + 3]
        g_ref, dl_ref, nm_ref, nv_ref = refs[2 * S + 3 + len(extra):]
        me = dev_ref[0]
        for s in range(S):
            @pl.when(pl.program_id(0) == s)
            def _(s=s):
                g = jnp.zeros((rb, C), F32)
                for d in range(N_DEV):
                    g += jnp.where(me == d, o_refs[s][...], r_refs[s][d]).astype(F32)
                g_ref[...] = g
                dl_ref[...], nm_ref[...], nv_ref[...] = _adamw(w_ref[...], g, m_ref[...], v_ref[...])

    def row(sl, i, s):
        return jnp.where(sl == s, i, jnp.where(sl < s, 0, last))

    def rspec(s):
        return pl.BlockSpec((N_DEV, rb, C), lambda sl, i, dev: (0, row(sl, i, s), 0))

    def ospec(s):
        return pl.BlockSpec((None, rb, C), lambda sl, i, dev: (dev[0], row(sl, i, s), 0))

    blk = pl.BlockSpec((None, rb, C), lambda sl, i, dev: (first + sl, i, 0))
    out = jax.ShapeDtypeStruct((n_slots, R, C), F32)
    grid_spec = pltpu.PrefetchScalarGridSpec(
        num_scalar_prefetch=1, grid=(S, R // rb),
        in_specs=[rspec(s) for s in range(S)] + [ospec(s) for s in range(S)] + [blk, blk, blk] + [ANY] * len(extra),
        out_specs=[blk, blk, blk, blk],
    )
    n_in = 1 + 2 * S + 3
    return pl.pallas_call(
        body, name="sum_update", grid_spec=grid_spec, out_shape=[out, out, out, out],
        input_output_aliases={n_in + i: i for i in range(n_prev)},
        compiler_params=_cparams(("arbitrary", "arbitrary")),
    )(dev, *recvs, *owns, w, m, v, *extra)


def small_sum(gathered):
    _, R, C = gathered.shape

    def body(g_ref, o_ref):
        acc = g_ref[0]
        for d in range(1, N_DEV):
            acc += g_ref[d]
        o_ref[...] = acc

    return pl.pallas_call(
        body, name="small_sum", grid=(1,),
        in_specs=[pl.BlockSpec((N_DEV, R, C), lambda i: (0, 0, 0))],
        out_specs=pl.BlockSpec((R, C), lambda i: (0, 0)),
        out_shape=jax.ShapeDtypeStruct((R, C), F32),
        compiler_params=_cparams(("arbitrary",)),
    )(gathered)


def small_update(w, g, m, v):
    def body(w_ref, g_ref, m_ref, v_ref, dl_ref, nm_ref, nv_ref):
        dl_ref[...], nm_ref[...], nv_ref[...] = _adamw(w_ref[...], g_ref[...], m_ref[...], v_ref[...])

    blk = pl.BlockSpec(w.shape, lambda i: (0, 0))
    out = jax.ShapeDtypeStruct(w.shape, F32)
    return pl.pallas_call(
        body, name="small_update", grid=(1,),
        in_specs=[blk] * 4, out_specs=[blk] * 3, out_shape=[out] * 3,
        compiler_params=_cparams(("arbitrary",)),
    )(w, g, m, v)


MESH = pl.DeviceIdType.MESH
ANY = pl.BlockSpec(memory_space=pl.ANY)


def _coords():
    return lax.axis_index("x"), lax.axis_index("y"), lax.axis_index("c")


def _dev_index(x, y, c):
    return 4 * x + 2 * y + c


def _at_dev(ref, p, dev):
    return ref.at[(slice(None),) * p + (dev,)]


def all_gather(arrays, ps):
    n = len(arrays)

    def body(*refs):
        ins, outs = refs[:n], refs[n:2 * n]
        send_sems, recv_sems, local_sems = refs[2 * n:]
        x, y, c = _coords()
        me, sibling = (x, y, c), (x, y, 1 - c)
        chips = [(1 - x, y), (x, 1 - y), (1 - x, 1 - y)]

        def copy(a, k, block, to, src=None):
            dst = _at_dev(outs[a], ps[a], _dev_index(*block))
            return pltpu.make_async_remote_copy(
                src_ref=dst if src is None else src, dst_ref=dst,
                send_sem=send_sems.at[a, k], recv_sem=recv_sems.at[a, k], device_id=to, device_id_type=MESH)

        mine = [pltpu.make_async_copy(ins[a], _at_dev(outs[a], ps[a], _dev_index(*me)), local_sems.at[a])
                for a in range(n)]
        for cp in mine:
            cp.start()
        first = []
        for a in range(n):
            first.append(copy(a, 0, me, sibling, src=ins[a]))
            first += [copy(a, 1 + j, me, (*chip, c), src=ins[a]) for j, chip in enumerate(chips)]
        for cp in first:
            cp.start()
        passed = []
        for j, chip in enumerate(chips):
            for a in range(n):
                copy(a, 1 + j, (*chip, c), me).wait_recv()
                fwd = copy(a, 4 + j, (*chip, c), sibling)
                fwd.start()
                passed.append(fwd)
        for a in range(n):
            copy(a, 0, sibling, me).wait_recv()
            for j, chip in enumerate(chips):
                copy(a, 4 + j, (*chip, 1 - c), me).wait_recv()
        for cp in first + passed:
            cp.wait_send()
        for cp in mine:
            cp.wait()

    out_shape = [jax.ShapeDtypeStruct(a.shape[:p] + (N_DEV,) + a.shape[p:], a.dtype) for a, p in zip(arrays, ps)]
    return pl.pallas_call(
        body, name="all_gather", in_specs=[ANY] * n, out_specs=[ANY] * n, out_shape=out_shape,
        scratch_shapes=[pltpu.SemaphoreType.DMA((n, 7)), pltpu.SemaphoreType.DMA((n, 7)),
                        pltpu.SemaphoreType.DMA((n,))],
        compiler_params=pltpu.CompilerParams(has_side_effects=True),
    )(*arrays)


HBM = pl.BlockSpec(memory_space=pltpu.HBM)
SEM = pl.BlockSpec(memory_space=pltpu.SEMAPHORE)
EFFECT = pltpu.SideEffectType.DATAFLOW_SIDE_EFFECTING


def _peers(x, y, c):
    out = []
    for k in range(1, N_DEV):
        out.append((1 - x if k & 4 else x, 1 - y if k & 2 else y, 1 - c if k & 1 else c))
    return out


def _exchange_plan(n):
    def plan(refs, x, y, c):
        blocks, lands = refs[:n], refs[n:2 * n]
        me = _dev_index(x, y, c)
        moves = []
        for peer in _peers(x, y, c):
            q = _dev_index(*peer)
            moves += [(blocks[a].at[q], lands[a].at[me], peer, lands[a].at[q]) for a in range(n)]
        return moves
    return plan


def _gather_plan(ps, second):
    def plan(refs, x, y, c):
        me, sibling = (x, y, c), (x, y, 1 - c)
        chips = [(1 - x, y), (x, 1 - y), (1 - x, 1 - y)]
        if second:
            trips = [((*ch, c), sibling, (*ch, 1 - c)) for ch in chips]
        else:
            trips = [(me, sibling, sibling)] + [(me, (*ch, c), (*ch, c)) for ch in chips]
        moves = []
        for sent, to, arriving in trips:
            for ref, p in zip(refs, ps):
                blk = _at_dev(ref, p, _dev_index(*sent))
                moves.append((blk, blk, to, _at_dev(ref, p, _dev_index(*arriving))))
        return moves
    return plan


def copies_start(name, plan, n_moves, arrays, carry):
    n = len(arrays)

    def body(*refs):
        sems = refs[n + 1:n + 1 + 2 * n_moves]
        moves = plan(refs[:n], *_coords())
        assert len(moves) == n_moves
        for i, (src, dst, to, _) in enumerate(moves):
            pltpu.make_async_remote_copy(src_ref=src, dst_ref=dst, send_sem=sems[i], recv_sem=sems[n_moves + i],
                                         device_id=to, device_id_type=MESH).start()

    operands = [pltpu.with_memory_space_constraint(a, pltpu.HBM) for a in list(arrays) + [carry]]
    outs = pl.pallas_call(
        body, name=name,
        out_shape=[pltpu.SemaphoreType.DMA(())] * (2 * n_moves) + [pltpu.HBM(a.shape, a.dtype) for a in operands],
        in_specs=[HBM] * (n + 1), out_specs=[SEM] * (2 * n_moves) + [HBM] * (n + 1),
        input_output_aliases={i: 2 * n_moves + i for i in range(n + 1)},
        compiler_params=pltpu.CompilerParams(has_side_effects=EFFECT),
    )(*operands)
    return outs[:n_moves], outs[n_moves:2 * n_moves], outs[2 * n_moves:-1], outs[-1]


def copies_wait(name, plan, send_sems, recv_sems, arrays, after):
    n, n_moves = len(arrays), len(send_sems)

    def body(*refs):
        sems = refs[n:n + 2 * n_moves]
        for i, (src, _, to, arriving) in enumerate(plan(refs[:n], *_coords())):
            cp = pltpu.make_async_remote_copy(src_ref=src, dst_ref=arriving, send_sem=sems[i],
                                              recv_sem=sems[n_moves + i], device_id=to, device_id_type=MESH)
            cp.wait_send()
            cp.wait_recv()

    return pl.pallas_call(
        body, name=name,
        out_shape=[pltpu.HBM(a.shape, a.dtype) for a in arrays],
        in_specs=[HBM] * n + [SEM] * (2 * n_moves) + [ANY], out_specs=[HBM] * n,
        input_output_aliases={i: i for i in range(n)},
        compiler_params=pltpu.CompilerParams(has_side_effects=EFFECT),
    )(*arrays, *send_sems, *recv_sems, after)


WEIGHT_NAMES = ("w_ada", "b_ada", "g_pre", "g_post", "w_ff_in", "w_ff_out", "w_in", "conv_w", "w_conv_out",
                "lam_re", "lam_im", "log_dt", "ssm_b_re", "ssm_b_im", "ssm_c_re", "ssm_c_im", "ssm_d", "w_glu",
                "w_pool", "pool_scale", "w_pool_out", "w_sb_out", "w_out")
BIG_NAMES = ("w_ff_in", "w_ff_out", "w_in", "w_conv_out", "w_glu", "w_pool_out", "w_sb_out", "w_out")
SMALL_NAMES = ("b_ada", "g_pre", "g_post", "conv_w", "lam_re", "lam_im", "log_dt", "ssm_b_re", "ssm_b_im",
               "ssm_c_re", "ssm_c_im", "ssm_d", "w_pool", "pool_scale")
PACK_LANES = 128
PACK_ROWS = 8


def _pack(arrays):
    flat = jnp.concatenate([a.reshape(-1) for a in arrays])
    unit = PACK_LANES * PACK_ROWS
    flat = jnp.pad(flat, (0, -flat.shape[0] % unit))
    return flat.reshape(-1, PACK_LANES)


def _unpack(pack, shapes):
    flat = pack.reshape(-1)
    out, off = [], 0
    for s in shapes:
        n = 1
        for d in s:
            n *= d
        out.append(flat[off:off + n].reshape(s))
        off += n
    return out


def _pad_rows(a, rows=8):
    return jnp.pad(a, ((0, 0), (0, rows - a.shape[1]), (0, 0)))


def _tile_b(b):
    L = b.shape[0]
    return jnp.tile(b.transpose(0, 3, 1, 2).reshape(L, SSM_GROUP, SSM_W), (1, SSM_GROUPS, 1))


def _tile_c(c):
    L = c.shape[0]
    return jnp.tile(c.transpose(0, 3, 1, 2).reshape(L, SSM_STATE, MIX_W), (1, SSM_GROUPS, 1))


def _step(x, c, target, W, M, V):
    T, D = x.shape[1], x.shape[2]
    L = W["w_ada"].shape[0]
    x = x[0]
    target = target[0]
    ax, ay, ac = _coords()
    dev = _dev_index(ax, ay, ac)
    n_ada = W["w_ada"].shape[2]

    dev_s = jnp.reshape(dev, (1,)).astype(jnp.int32)
    items = ([(W["w_ff_in"], 0), (W["w_ff_in"], 1), (W["w_ff_out"], 0), (W["w_ff_out"], 1)]
             + [(W[k], None) for k in BIG_NAMES[2:]])
    def cast(l, after=None):
        return list(cast_layer(jnp.concatenate([jnp.array([l], jnp.int32), dev_s]), items, after))

    bufs = [cast(0)] + [None] * (L - 1)
    ffn1_w, mixer_w, ffn2_w = (0, 2), (4, 5, 6, 7, 8, 9), (1, 3)
    all_w = tuple(range(len(items)))

    def gather_start(tag, second, l, idx, carry):
        plan = _gather_plan((1,) * len(idx), second)
        n_moves = (3 if second else 4) * len(idx)
        s_sem, r_sem, arrs, carry = copies_start(f"gather_{'b' if second else 'a'}_start_{tag}", plan, n_moves,
                                                 [bufs[l][i] for i in idx], carry)
        for i, a in zip(idx, arrs):
            bufs[l][i] = a
        return (plan, s_sem, r_sem), carry

    def gather_wait(tag, second, l, idx, flight, after):
        arrs = copies_wait(f"gather_{'b' if second else 'a'}_wait_{tag}", *flight, [bufs[l][i] for i in idx], after)
        for i, a in zip(idx, arrs):
            bufs[l][i] = a

    def gather_finish(tag, l, idx, flight, after, carry):
        gather_wait(tag, False, l, idx, flight, after)
        flight, carry = gather_start(tag, True, l, idx, carry)
        gather_wait(tag, True, l, idx, flight, carry)
        return carry

    first = []
    for g, idx in enumerate((ffn1_w, mixer_w, ffn2_w)):
        flight, x = gather_start(f"0_{g}", False, 0, idx, x)
        first.append(flight)
    for l in range(1, L):
        bufs[l] = cast(l, x)

    gathered = all_gather([W["g_pre"], W["g_post"], W["conv_w"], c], [0, 0, 0, 0])
    g_pre = gathered[0].transpose(1, 2, 0, 3).reshape(L, N_SUB, D)
    g_post = gathered[1].transpose(1, 2, 0, 3).reshape(L, N_SUB, D)
    conv_w = _pad_rows(gathered[2].transpose(1, 2, 0, 3).reshape(L, 3, MIX_W))
    c_all = gathered[3].reshape(N_DEV, D)

    b_cols = lax.dynamic_slice_in_dim(W["b_ada"], dev * n_ada, n_ada, axis=1)[:, None, :]
    ada_cols = ada_fwd(c_all, W["w_ada"], b_cols)
    ada_all = all_gather([ada_cols], [0])[0]
    ada = lax.dynamic_index_in_dim(ada_all, dev, axis=2, keepdims=False)
    ada = ada.transpose(1, 0, 2).reshape(L, N_SUB, 3, D)
    zeros = jnp.zeros((L, N_SUB, D), F32)
    pv_all = jnp.stack([g_pre, ada[:, :, 0], ada[:, :, 1], g_post, ada[:, :, 2], zeros, zeros, zeros], axis=2)

    lam = jnp.stack([W["lam_re"].reshape(L, SSM_W), W["lam_im"].reshape(L, SSM_W),
                     jnp.repeat(W["log_dt"], SSM_STATE, axis=1)], axis=1)
    lam = _pad_rows(lam)
    b_t = jnp.stack([_tile_b(W["ssm_b_re"]), _tile_b(W["ssm_b_im"])], axis=1)
    c_t = jnp.stack([_tile_c(W["ssm_c_re"]), _tile_c(W["ssm_c_im"])], axis=1)
    avec, b_bd, c_bd = s5_params(lam, b_t, c_t)
    ssm_d = _pad_rows(W["ssm_d"][:, None, :])
    pool_scale = _pad_rows(W["pool_scale"][:, None, :])
    eye4 = jnp.eye(len(POOL_WINDOWS), dtype=F32)
    w_bd = jnp.einsum("lgcd,gh->lgchd", W["w_pool"], eye4).reshape(L, MIX_W, MIX_W).astype(BF16)

    x = gather_finish("0_0", 0, ffn1_w, first[0], pv_all, x)

    def ffn_weights(l, k):
        b = bufs[l]
        return b[k].reshape(1, 1, 2, 4, D, FF_BLK), b[2 + k].reshape(1, 1, 4, FF_BLK, D)

    def mixer_weights(l):
        b = bufs[l]
        return b[4], b[5], b[6], b[7], b[8], b[9].reshape(1, D, D)

    l0 = jnp.array([0], jnp.int32)
    k0 = jnp.array([0, 0], jnp.int32)
    saved = []
    grouped = {0: first}
    for l in range(L):
        li = jnp.array([l], jnp.int32)
        nxt = l + 1 < L and l + 1 != 1
        if l == 0 and L > 1:
            grouped[1] = []
            for g, idx in enumerate((ffn1_w, mixer_w, ffn2_w)):
                flight, x = gather_start(f"1_{g}", False, 1, idx, x)
                grouped[1].append(flight)
        if nxt:
            flight, x = gather_start(f"{l + 1}", False, l + 1, all_w, x)
        if l == 1:
            x = gather_finish("1_0", 1, ffn1_w, grouped[1][0], x, x)
        x0 = x
        ab0, f0, x1 = ffn_fwd(k0, x0, pv_all[l, 0], *ffn_weights(l, 0))
        if l in grouped:
            x1 = gather_finish(f"{l}_1", l, mixer_w, grouped[l][1], x1, x1)
        wg_in, wg_conv, wg_glu, wg_pool, wg_sb, wg_out = mixer_weights(l)
        p = mix_in_fwd(l0, x1, pv_all[l, 1], wg_in)
        za = conv_fwd(li, p, conv_w)
        z = pool_fwd(li, p, w_bd, pool_scale)
        s = s5_scan(li, avec, s5_bu(li, p, b_bd), False)
        yg = s5_out(li, p, s, c_bd, ssm_d)
        o, sb_tot = sb_fwd(p)
        x2, m = merge_fwd(l0, p, za, yg, z, o, x1, pv_all[l, 1], wg_conv, wg_glu, wg_pool, wg_sb, wg_out)
        if l in grouped:
            x2 = gather_finish(f"{l}_2", l, ffn2_w, grouped[l][2], x2, x2)
        if nxt:
            gather_wait(f"{l + 1}", False, l + 1, all_w, flight, x2)
            flight, x2 = gather_start(f"{l + 1}", True, l + 1, all_w, x2)
        ab1, f1, x = ffn_fwd(k0, x2, pv_all[l, 2], *ffn_weights(l, 1))
        if nxt:
            gather_wait(f"{l + 1}", True, l + 1, all_w, flight, x)
        saved.append((x0, ab0, f0, x1, p, za, z, s, yg, o, sb_tot, m, x2, ab1, f1))

    dx, loss_blk = loss_head(x, target)
    loss = lax.psum(loss_blk[0, 0], ("x", "y", "c"))

    n_blocks = 10
    ffn2_g, mixer_g, ffn1_g = (1, 3), (4, 5, 6, 7, 8, 9), (0, 2)
    recvs, owns, in_flight = [[None] * n_blocks for _ in range(L)], [[None] * n_blocks for _ in range(L)], []

    def exchange_start(tag, layer, idx, blocks, carry):
        n = len(idx)
        plan = _exchange_plan(n)
        arrays = list(blocks) + [lax.empty(a.shape, a.dtype) for a in blocks]
        s_sem, r_sem, arrays, carry = copies_start(f"exchange_start_{tag}", plan, (N_DEV - 1) * n, arrays, carry)
        in_flight.append((tag, layer, idx, plan, s_sem, r_sem, arrays))
        return carry

    def settle(after, upto):
        for flight in [f for f in in_flight if f[1] >= upto]:
            in_flight.remove(flight)
            tag, layer, idx, plan, s_sem, r_sem, arrays = flight
            arrays = copies_wait(f"exchange_wait_{tag}", plan, s_sem, r_sem, arrays, after)
            for j, i in enumerate(idx):
                owns[layer][i], recvs[layer][i] = arrays[j], arrays[len(idx) + j]

    pgs = [None] * L
    small = {k: [None] * L for k in ("conv_w", "w_bd", "pool_scale", "ssm_d", "gb", "gc", "da")}
    for l in reversed(range(L)):
        li = jnp.array([l], jnp.int32)
        x0, ab0, f0, x1, p, za, z, s, yg, o, sb_tot, m, x2, ab1, f1 = saved[l]
        wg_in, wg_conv, wg_glu, wg_pool, wg_sb, wg_out = mixer_weights(l)
        g_in1, g_out1, dx, pg2 = ffn_bwd(k0, dx, x2, f1, pv_all[l, 2], ab1, *ffn_weights(l, 1))
        dx = exchange_start(f"{l}_ffn2", l, ffn2_g,
                            [g_in1.reshape(N_DEV, FF_BLK, D), g_out1.reshape(N_DEV, D_FF // N_DEV, D)], dx)
        (dza, dyg, dz, do, dgates, pg1m, g_conv, g_glu, g_pool, g_sb, g_wo) = merge_bwd(
            l0, p, za, yg, z, o, m, dx, pv_all[l, 1], wg_conv, wg_glu, wg_pool, wg_sb, wg_out)
        d_conv, small["conv_w"][l] = conv_bwd(li, p, dza, conv_w)
        d_pool, small["w_bd"][l], small["pool_scale"][l] = pool_bwd(li, p, dz, w_bd, pool_scale)
        ds, du_skip, small["gc"][l], small["ssm_d"][l] = s5_bwd_y(li, p, s, dyg, c_bd, ssm_d)
        lam_s = s5_scan(li, avec, ds, True)
        d_ssm, small["gb"][l] = s5_bwd_u(li, p, lam_s, du_skip, b_bd)
        small["da"][l] = s5_bwd_a(s, lam_s)
        d_qkv = sb_bwd(p, do, sb_tot)
        dp = dp_assemble(d_conv, d_ssm, d_pool, d_qkv, dgates)
        dx, g_win, pg1i = mix_in_bwd(l0, dp, dx, x1, pv_all[l, 1], wg_in)
        settle(dx, l + 1)
        dx = exchange_start(f"{l}_mixer", l, mixer_g,
                            [g_win, g_conv, g_glu, g_pool, g_sb, g_wo.reshape(N_DEV, D // N_DEV, D)], dx)
        g_in0, g_out0, dx, pg0 = ffn_bwd(k0, dx, x0, f0, pv_all[l, 0], ab0, *ffn_weights(l, 0))
        pgs[l] = jnp.stack([pg0, pg1m + pg1i, pg2])
        last_ffn1 = [g_in0.reshape(N_DEV, FF_BLK, D), g_out0.reshape(N_DEV, D_FF // N_DEV, D)]
        if l > 0:
            dx = exchange_start(f"{l}_ffn1", l, ffn1_g, last_ffn1, dx)

    dlam, db_t, dc_t, dldt = s5_params_bwd(lam, b_t, jnp.stack(small["gb"]), jnp.stack(small["gc"]),
                                           jnp.stack(small["da"]))
    pg = jnp.stack(pgs)
    d_ada = jnp.stack([pg[:, :, PV_SHIFT], pg[:, :, PV_SCALE], pg[:, :, PV_GATE]], axis=2).reshape(L, N_SUB * 3 * D)
    db = db_t[..., :SSM_STATE].reshape(L, 2, SSM_GROUPS, SSM_GROUP, SSM_STATE).transpose(0, 1, 2, 4, 3)
    dc = dc_t[..., :SSM_GROUP].reshape(L, 2, SSM_GROUPS, SSM_STATE, SSM_GROUP).transpose(0, 1, 2, 4, 3)
    d_wpool = jnp.einsum("lgcgd->lgcd", jnp.stack(small["w_bd"]).reshape(L, 4, 64, 4, 64))
    contrib = {
        "b_ada": d_ada, "g_pre": pg[:, :, PV_GPRE], "g_post": pg[:, :, PV_GPOST],
        "conv_w": jnp.stack(small["conv_w"])[:, :3], "lam_re": dlam[:, 0].reshape(L, SSM_GROUPS, SSM_STATE),
        "lam_im": dlam[:, 1].reshape(L, SSM_GROUPS, SSM_STATE), "log_dt": dldt[:, 2, :SSM_GROUPS],
        "ssm_b_re": db[:, 0], "ssm_b_im": db[:, 1], "ssm_c_re": dc[:, 0], "ssm_c_im": dc[:, 1],
        "ssm_d": jnp.stack(small["ssm_d"])[:, 0], "w_pool": d_wpool,
        "pool_scale": jnp.stack(small["pool_scale"])[:, 0],
    }
    contrib_shapes = [contrib[k].shape for k in SMALL_NAMES]

    big_idx = {"w_ff_in": (0, 1), "w_ff_out": (2, 3), "w_in": (4,), "w_conv_out": (5,), "w_glu": (6,),
               "w_pool_out": (7,), "w_sb_out": (8,), "w_out": (9,)}

    def slots(name, a):
        a = a.reshape((-1,) + a.shape[-2:])
        return a.swapaxes(1, 2) if name == "w_ff_in" else a

    def unslots(name, a):
        a = a.swapaxes(1, 2) if name == "w_ff_in" else a
        return a.reshape(W[name].shape)

    def big(name, layers, prev, after=None):
        idx = big_idx[name]
        return sum_update(dev_s, layers[0] * len(idx), [recvs[l][i] for l in layers for i in idx],
                          [owns[l][i] for l in layers for i in idx],
                          slots(name, W[name]), slots(name, M[name]), slots(name, V[name]), prev, after)

    partial = {}

    def partial_updates(names, after):
        for name in names:
            if L > 1:
                partial[name] = big(name, list(range(1, L)), None, after)
                after = partial[name][0]
        return after

    pack_buf = [place_own(dev_s, _pack([contrib[k] for k in SMALL_NAMES]))]
    plan_a, plan_b = _gather_plan((0,), False), _gather_plan((0,), True)
    s_sem, r_sem, pack_buf, dx = copies_start("small_gather_a_start", plan_a, 4, pack_buf, dx)
    dx = exchange_start("0_ffn1", 0, ffn1_g, last_ffn1, dx)
    after = partial_updates(BIG_NAMES[:1], dx)
    pack_buf = copies_wait("small_gather_a_wait", plan_a, s_sem, r_sem, pack_buf, after)
    s_sem, r_sem, pack_buf, dx = copies_start("small_gather_b_start", plan_b, 3, pack_buf, dx)
    after = partial_updates(BIG_NAMES[1:], dx)
    pack_all = copies_wait("small_gather_b_wait", plan_b, s_sem, r_sem, pack_buf, after)[0]
    total = dict(zip(SMALL_NAMES, _unpack(small_sum(pack_all), contrib_shapes)))
    d_ada_all = pack_all.reshape(N_DEV, -1)[:, :L * N_SUB * 3 * D].reshape(N_DEV, L, N_SUB * 3 * D)
    dada_cols = lax.dynamic_slice_in_dim(d_ada_all, dev * n_ada, n_ada, axis=2).transpose(1, 0, 2)
    n_g = D // N_DEV
    grads = {}
    for k in SMALL_NAMES:
        g = total[k]
        if k in ("g_pre", "g_post"):
            g = lax.dynamic_slice_in_dim(g, dev * n_g, n_g, axis=2)
        elif k == "conv_w":
            g = lax.dynamic_slice_in_dim(g, dev * (MIX_W // N_DEV), MIX_W // N_DEV, axis=2)
        grads[k] = g

    delta, new_m, new_v = {}, {}, {}
    shapes = [W[k].shape for k in SMALL_NAMES]
    dl, nm, nv = small_update(_pack([W[k] for k in SMALL_NAMES]), _pack([grads[k] for k in SMALL_NAMES]),
                              _pack([M[k] for k in SMALL_NAMES]), _pack([V[k] for k in SMALL_NAMES]))
    for k, a, b, cc in zip(SMALL_NAMES, _unpack(dl, shapes), _unpack(nm, shapes), _unpack(nv, shapes)):
        delta[k], new_m[k], new_v[k] = a, b, cc
    grads["w_ada"], delta["w_ada"], new_m["w_ada"], new_v["w_ada"] = ada_update(
        c_all, dada_cols, W["w_ada"], M["w_ada"], V["w_ada"])

    settle(new_m["w_ada"], 0)
    for name in BIG_NAMES:
        outs = big(name, [0], partial.get(name))
        grads[name], delta[name], new_m[name], new_v[name] = [unslots(name, o) for o in outs]

    return (loss, dx[None], *[grads[k] for k in WEIGHT_NAMES], *[delta[k] for k in WEIGHT_NAMES],
            *[new_m[k] for k in WEIGHT_NAMES], *[new_v[k] for k in WEIGHT_NAMES])


def kernel(x, c, w_ada, b_ada, g_pre, g_post, w_ff_in, w_ff_out, w_in, conv_w, w_conv_out, lam_re, lam_im, log_dt, ssm_b_re, ssm_b_im, ssm_c_re, ssm_c_im, ssm_d, w_glu, w_pool, pool_scale, w_pool_out, w_sb_out, w_out, loss_target, m_w_ada, m_b_ada, m_g_pre, m_g_post, m_w_ff_in, m_w_ff_out, m_w_in, m_conv_w, m_w_conv_out, m_lam_re, m_lam_im, m_log_dt, m_ssm_b_re, m_ssm_b_im, m_ssm_c_re, m_ssm_c_im, m_ssm_d, m_w_glu, m_w_pool, m_pool_scale, m_w_pool_out, m_w_sb_out, m_w_out, v_w_ada, v_b_ada, v_g_pre, v_g_post, v_w_ff_in, v_w_ff_out, v_w_in, v_conv_w, v_w_conv_out, v_lam_re, v_lam_im, v_log_dt, v_ssm_b_re, v_ssm_b_im, v_ssm_c_re, v_ssm_c_im, v_ssm_d, v_w_glu, v_w_pool, v_pool_scale, v_w_pool_out, v_w_sb_out, v_w_out):
    w = (w_ada, b_ada, g_pre, g_post, w_ff_in, w_ff_out, w_in, conv_w, w_conv_out, lam_re, lam_im, log_dt, ssm_b_re, ssm_b_im, ssm_c_re, ssm_c_im, ssm_d, w_glu, w_pool, pool_scale, w_pool_out, w_sb_out, w_out)
    m = (m_w_ada, m_b_ada, m_g_pre, m_g_post, m_w_ff_in, m_w_ff_out, m_w_in, m_conv_w, m_w_conv_out, m_lam_re, m_lam_im, m_log_dt, m_ssm_b_re, m_ssm_b_im, m_ssm_c_re, m_ssm_c_im, m_ssm_d, m_w_glu, m_w_pool, m_pool_scale, m_w_pool_out, m_w_sb_out, m_w_out)
    v = (v_w_ada, v_b_ada, v_g_pre, v_g_post, v_w_ff_in, v_w_ff_out, v_w_in, v_conv_w, v_w_conv_out, v_lam_re, v_lam_im, v_log_dt, v_ssm_b_re, v_ssm_b_im, v_ssm_c_re, v_ssm_c_im, v_ssm_d, v_w_glu, v_w_pool, v_pool_scale, v_w_pool_out, v_w_sb_out, v_w_out)
    return _step(x, c, loss_target, dict(zip(WEIGHT_NAMES, w)), dict(zip(WEIGHT_NAMES, m)), dict(zip(WEIGHT_NAMES, v)))
```
